```python
import jax, jax.numpy as jnp
from jax import lax
import numpy as np

D_MODEL = 1024
BATCH = 8
SEQ = 4096
DEPTH = 4

CHUNK = 64
PLE_DIM = 256
N_EVEN = (DEPTH + 1) // 2
N_ODD = DEPTH // 2
D_A = D_MODEL
CONV_A = 31
D_B = D_MODEL
HEAD_DIM = 64
H_B = D_B // HEAD_DIM
N_GROUPS = 4
N_STATE = 128
CONV_B = 4
XBC_DIM = D_B + 2 * N_GROUPS * N_STATE
E_IN = 2 * D_A + D_B + XBC_DIM + H_B
D_C = D_MODEL
CONV_C = 3
D_FF = 2816
CONV_F = 3
LN_EPS = 1e-5

kernel_name = "hybrid_conformer_ssd_shortconv_trunk"


def _layer_norm(x, g, b):
    xf = x.astype(jnp.float32)
    mu = jnp.mean(xf, axis=-1, keepdims=True)
    var = jnp.mean(jnp.square(xf - mu), axis=-1, keepdims=True)
    return ((xf - mu) * lax.rsqrt(var + LN_EPS) * g + b).astype(x.dtype)


def _rms_norm(x, g):
    xf = x.astype(jnp.float32)
    return (xf * lax.rsqrt(jnp.mean(jnp.square(xf), axis=-1, keepdims=True) + LN_EPS) * g).astype(x.dtype)


def _dwconv_causal(x, w, b=None):
    k, c = w.shape
    y = lax.conv_general_dilated(x, w[:, None, :].astype(x.dtype), window_strides=(1,),
                                 padding=[(k - 1, 0)], dimension_numbers=("NWC", "WIO", "NWC"),
                                 feature_group_count=c)
    if b is not None:
        y = y + b
    return y


def _conformer_conv(u, conv_w, conv_b, ln_g, ln_b):
    a = u[..., :D_A] * jax.nn.sigmoid(u[..., D_A:])
    a = _dwconv_causal(a, conv_w, conv_b)
    return jax.nn.silu(_layer_norm(a, ln_g, ln_b))


def _ssd(xh, dt, a, bm, cm):
    bsz, l, h, p = xh.shape
    g, n = bm.shape[2], bm.shape[3]
    hg = h // g
    c = l // CHUNK
    x = xh.reshape(bsz, c, CHUNK, g, hg, p)
    dtc = dt.reshape(bsz, c, CHUNK, g, hg)
    bc = bm.reshape(bsz, c, CHUNK, g, n)
    cc = cm.reshape(bsz, c, CHUNK, g, n)
    cum = jnp.cumsum(dtc * a.reshape(g, hg), axis=2)
    mask = jnp.tril(jnp.ones((CHUNK, CHUNK), dtype=bool))[:, :, None, None]
    seg = cum[:, :, :, None] - cum[:, :, None, :]
    decay = jnp.where(mask, jnp.exp(jnp.where(mask, seg, 0.0)), 0.0)
    cb = jnp.einsum("bclgn,bcsgn->bclsg", cc, bc)
    w = cb[..., None] * decay * dtc[:, :, None]
    y_diag = jnp.einsum("bclsgh,bcsghp->bclghp", w, x)
    decay_states = jnp.exp(cum[:, :, -1:] - cum) * dtc
    states = jnp.einsum("bclgn,bclgh,bclghp->bcghpn", bc, decay_states, x)
    chunk_decay = jnp.exp(cum[:, :, -1])

    def step(hstate, inp):
        dec, st = inp
        return hstate * dec[..., None, None] + st, hstate

    h0 = jnp.zeros((bsz, g, hg, p, n), dtype=states.dtype)
    _, prev = lax.scan(step, h0, (jnp.moveaxis(chunk_decay, 1, 0), jnp.moveaxis(states, 1, 0)))
    prev = jnp.moveaxis(prev, 0, 1)
    y_off = jnp.einsum("bclgn,bcghpn,bclgh->bclghp", cc, prev, jnp.exp(cum))
    return (y_diag + y_off).reshape(bsz, l, h, p).astype(xh.dtype)


def _mamba2(z, xbc, dt_raw, conv_w, conv_b, dt_bias, a_log, d_skip, norm_g):
    bsz, l, _ = z.shape
    xbc = jax.nn.silu(_dwconv_causal(xbc, conv_w, conv_b))
    gn = N_GROUPS * N_STATE
    xs = xbc[..., :D_B].reshape(bsz, l, H_B, HEAD_DIM)
    bm = xbc[..., D_B:D_B + gn].reshape(bsz, l, N_GROUPS, N_STATE)
    cm = xbc[..., D_B + gn:].reshape(bsz, l, N_GROUPS, N_STATE)
    dt = jax.nn.softplus(dt_raw.astype(jnp.float32) + dt_bias)
    a = -jnp.exp(a_log.astype(jnp.float32))
    y = _ssd(xs, dt, a, bm, cm) + d_skip[:, None] * xs
    y = y.reshape(bsz, l, D_B) * jax.nn.silu(z)
    return _rms_norm(y, norm_g)


def _short_conv(x, w_in, conv_w, w_out):
    u = x @ w_in
    bg, cg, v = u[..., :D_C], u[..., D_C:2 * D_C], u[..., 2 * D_C:]
    return (bg * _dwconv_causal(cg * v, conv_w)) @ w_out


def _conv_ffn(x, w_up, conv_w, conv_b, w_down):
    h = _dwconv_causal(x @ w_up, conv_w, conv_b)
    return (jax.nn.silu(h[..., :D_FF]) * h[..., D_FF:]) @ w_down


def _fwd_setup_inputs(seed: int = 0) -> dict:
    key = jax.random.key(seed)
    ks = jax.random.split(key, 32)
    beta = (8.0 * DEPTH) ** -0.25
    nrm = jax.random.normal

    def dense(k, shape, fan_in, scale=1.0):
        return nrm(k, shape, jnp.float32) * (fan_in ** -0.5) * scale

    dt0 = jnp.exp(jax.random.uniform(ks[8], (N_EVEN, H_B), jnp.float32) * (np.log(0.1) - np.log(0.001)) + np.log(0.001))
    return {
        "x": nrm(ks[0], (BATCH, SEQ, D_MODEL), jnp.float32),
        "p": nrm(ks[1], (DEPTH, BATCH, SEQ, PLE_DIM), jnp.float32),
        "e_w_in": dense(ks[2], (N_EVEN, D_MODEL, E_IN), D_MODEL),
        "e_conv_a_w": dense(ks[3], (N_EVEN, CONV_A, D_A), CONV_A),
        "e_conv_a_b": 0.02 * nrm(ks[4], (N_EVEN, D_A), jnp.float32),
        "e_ln_a_g": 1.0 + 0.02 * nrm(ks[5], (N_EVEN, D_A), jnp.float32),
        "e_ln_a_b": 0.02 * nrm(ks[6], (N_EVEN, D_A), jnp.float32),
        "e_conv_b_w": dense(ks[7], (N_EVEN, CONV_B, XBC_DIM), CONV_B),
        "e_conv_b_b": 0.02 * nrm(ks[9], (N_EVEN, XBC_DIM), jnp.float32),
        "e_dt_bias": dt0 + jnp.log(-jnp.expm1(-dt0)),
        "e_a_log": jnp.log(jax.random.uniform(ks[10], (N_EVEN, H_B), jnp.float32, 1.0, 16.0)),
        "e_d_skip": 1.0 + 0.1 * nrm(ks[11], (N_EVEN, H_B), jnp.float32),
        "e_norm_b_g": 1.0 + 0.02 * nrm(ks[12], (N_EVEN, D_B), jnp.float32),
        "e_w_out": dense(ks[13], (N_EVEN, D_A + D_B, D_MODEL), D_A + D_B, beta),
        "o_w_in": dense(ks[14], (N_ODD, D_MODEL, 3 * D_C), D_MODEL),
        "o_conv_w": dense(ks[15], (N_ODD, CONV_C, D_C), CONV_C),
        "o_w_out": dense(ks[16], (N_ODD, D_C, D_MODEL), D_C, beta),
        "f_w_up": dense(ks[17], (DEPTH, D_MODEL, 2 * D_FF), D_MODEL),
        "f_conv_w": dense(ks[18], (DEPTH, CONV_F, 2 * D_FF), CONV_F),
        "f_conv_b": 0.02 * nrm(ks[19], (DEPTH, 2 * D_FF), jnp.float32),
        "f_w_down": dense(ks[20], (DEPTH, D_FF, D_MODEL), D_FF, beta),
        "ple_w_proj": dense(ks[21], (DEPTH, PLE_DIM, D_MODEL), PLE_DIM, beta),
        "ple_w_gate": dense(ks[22], (DEPTH, D_MODEL, D_MODEL), D_MODEL),
        "ln_g": 1.0 + 0.02 * nrm(ks[23], (DEPTH, 2, D_MODEL), jnp.float32),
        "ln_b": 0.02 * nrm(ks[24], (DEPTH, 2, D_MODEL), jnp.float32),
    }


def _fwd_reference(x, p, e_w_in, e_conv_a_w, e_conv_a_b, e_ln_a_g, e_ln_a_b, e_conv_b_w, e_conv_b_b,
              e_dt_bias, e_a_log, e_d_skip, e_norm_b_g, e_w_out, o_w_in, o_conv_w, o_w_out,
              f_w_up, f_conv_w, f_conv_b, f_w_down, ple_w_proj, ple_w_gate, ln_g, ln_b):
    alpha = (2.0 * DEPTH) ** 0.25
    o_a = 2 * D_A
    o_x = o_a + D_B
    o_dt = o_x + XBC_DIM
    for i in range(DEPTH):
        j = i // 2
        if i % 2 == 0:
            u = x @ e_w_in[j]
            ya = _conformer_conv(u[..., :o_a], e_conv_a_w[j], e_conv_a_b[j], e_ln_a_g[j], e_ln_a_b[j])
            yb = _mamba2(u[..., o_a:o_x], u[..., o_x:o_dt], u[..., o_dt:], e_conv_b_w[j], e_conv_b_b[j],
                         e_dt_bias[j], e_a_log[j], e_d_skip[j], e_norm_b_g[j])
            mix = jnp.concatenate([ya, yb], axis=-1) @ e_w_out[j]
        else:
            mix = _short_conv(x, o_w_in[j], o_conv_w[j], o_w_out[j])
        x = _layer_norm(alpha * x + mix, ln_g[i, 0], ln_b[i, 0])
        ffn = _conv_ffn(x, f_w_up[i], f_conv_w[i], f_conv_b[i], f_w_down[i])
        ple = (p[i] @ ple_w_proj[i]) * jax.nn.sigmoid(x @ ple_w_gate[i])
        x = _layer_norm(alpha * x + ffn + ple, ln_g[i, 1], ln_b[i, 1])
    return x


import jax as _jax
import jax.numpy as _jnp

TWIN_FORMAT = 'train_step'
FWD_PARAMS = ['x', 'p', 'e_w_in', 'e_conv_a_w', 'e_conv_a_b', 'e_ln_a_g', 'e_ln_a_b', 'e_conv_b_w', 'e_conv_b_b', 'e_dt_bias', 'e_a_log', 'e_d_skip', 'e_norm_b_g', 'e_w_out', 'o_w_in', 'o_conv_w', 'o_w_out', 'f_w_up', 'f_conv_w', 'f_conv_b', 'f_w_down', 'ple_w_proj', 'ple_w_gate', 'ln_g', 'ln_b']
TWIN_WEIGHTS = ['e_w_in', 'e_conv_a_w', 'e_conv_a_b', 'e_ln_a_g', 'e_ln_a_b', 'e_conv_b_w', 'e_conv_b_b', 'e_dt_bias', 'e_a_log', 'e_d_skip', 'e_norm_b_g', 'e_w_out', 'o_w_in', 'o_conv_w', 'o_w_out', 'f_w_up', 'f_conv_w', 'f_conv_b', 'f_w_down', 'ple_w_proj', 'ple_w_gate', 'ln_g', 'ln_b']
TWIN_DIFF_INPUT = 'x'
TWIN_INPUTS = ['x', 'p', 'e_w_in', 'e_conv_a_w', 'e_conv_a_b', 'e_ln_a_g', 'e_ln_a_b', 'e_conv_b_w', 'e_conv_b_b', 'e_dt_bias', 'e_a_log', 'e_d_skip', 'e_norm_b_g', 'e_w_out', 'o_w_in', 'o_conv_w', 'o_w_out', 'f_w_up', 'f_conv_w', 'f_conv_b', 'f_w_down', 'ple_w_proj', 'ple_w_gate', 'ln_g', 'ln_b', 'loss_target', 'm_e_w_in', 'm_e_conv_a_w', 'm_e_conv_a_b', 'm_e_ln_a_g', 'm_e_ln_a_b', 'm_e_conv_b_w', 'm_e_conv_b_b', 'm_e_dt_bias', 'm_e_a_log', 'm_e_d_skip', 'm_e_norm_b_g', 'm_e_w_out', 'm_o_w_in', 'm_o_conv_w', 'm_o_w_out', 'm_f_w_up', 'm_f_conv_w', 'm_f_conv_b', 'm_f_w_down', 'm_ple_w_proj', 'm_ple_w_gate', 'm_ln_g', 'm_ln_b', 'v_e_w_in', 'v_e_conv_a_w', 'v_e_conv_a_b', 'v_e_ln_a_g', 'v_e_ln_a_b', 'v_e_conv_b_w', 'v_e_conv_b_b', 'v_e_dt_bias', 'v_e_a_log', 'v_e_d_skip', 'v_e_norm_b_g', 'v_e_w_out', 'v_o_w_in', 'v_o_conv_w', 'v_o_w_out', 'v_f_w_up', 'v_f_conv_w', 'v_f_conv_b', 'v_f_w_down', 'v_ple_w_proj', 'v_ple_w_gate', 'v_ln_g', 'v_ln_b']
TWIN_OUTPUTS = ['loss', 'grad_x', 'grad_e_w_in', 'grad_e_conv_a_w', 'grad_e_conv_a_b', 'grad_e_ln_a_g', 'grad_e_ln_a_b', 'grad_e_conv_b_w', 'grad_e_conv_b_b', 'grad_e_dt_bias', 'grad_e_a_log', 'grad_e_d_skip', 'grad_e_norm_b_g', 'grad_e_w_out', 'grad_o_w_in', 'grad_o_conv_w', 'grad_o_w_out', 'grad_f_w_up', 'grad_f_conv_w', 'grad_f_conv_b', 'grad_f_w_down', 'grad_ple_w_proj', 'grad_ple_w_gate', 'grad_ln_g', 'grad_ln_b', 'delta_e_w_in', 'delta_e_conv_a_w', 'delta_e_conv_a_b', 'delta_e_ln_a_g', 'delta_e_ln_a_b', 'delta_e_conv_b_w', 'delta_e_conv_b_b', 'delta_e_dt_bias', 'delta_e_a_log', 'delta_e_d_skip', 'delta_e_norm_b_g', 'delta_e_w_out', 'delta_o_w_in', 'delta_o_conv_w', 'delta_o_w_out', 'delta_f_w_up', 'delta_f_conv_w', 'delta_f_conv_b', 'delta_f_w_down', 'delta_ple_w_proj', 'delta_ple_w_gate', 'delta_ln_g', 'delta_ln_b', 'new_m_e_w_in', 'new_m_e_conv_a_w', 'new_m_e_conv_a_b', 'new_m_e_ln_a_g', 'new_m_e_ln_a_b', 'new_m_e_conv_b_w', 'new_m_e_conv_b_b', 'new_m_e_dt_bias', 'new_m_e_a_log', 'new_m_e_d_skip', 'new_m_e_norm_b_g', 'new_m_e_w_out', 'new_m_o_w_in', 'new_m_o_conv_w', 'new_m_o_w_out', 'new_m_f_w_up', 'new_m_f_conv_w', 'new_m_f_conv_b', 'new_m_f_w_down', 'new_m_ple_w_proj', 'new_m_ple_w_gate', 'new_m_ln_g', 'new_m_ln_b', 'new_v_e_w_in', 'new_v_e_conv_a_w', 'new_v_e_conv_a_b', 'new_v_e_ln_a_g', 'new_v_e_ln_a_b', 'new_v_e_conv_b_w', 'new_v_e_conv_b_b', 'new_v_e_dt_bias', 'new_v_e_a_log', 'new_v_e_d_skip', 'new_v_e_norm_b_g', 'new_v_e_w_out', 'new_v_o_w_in', 'new_v_o_conv_w', 'new_v_o_w_out', 'new_v_f_w_up', 'new_v_f_conv_w', 'new_v_f_conv_b', 'new_v_f_w_down', 'new_v_ple_w_proj', 'new_v_ple_w_gate', 'new_v_ln_g', 'new_v_ln_b']
TWIN_LEAF_KINDS = {'loss': 'loss', 'grad_x': 'grad_x', 'grad_e_w_in': 'grad_w', 'grad_e_conv_a_w': 'grad_w', 'grad_e_conv_a_b': 'grad_w', 'grad_e_ln_a_g': 'grad_w', 'grad_e_ln_a_b': 'grad_w', 'grad_e_conv_b_w': 'grad_w', 'grad_e_conv_b_b': 'grad_w', 'grad_e_dt_bias': 'grad_w', 'grad_e_a_log': 'grad_w', 'grad_e_d_skip': 'grad_w', 'grad_e_norm_b_g': 'grad_w', 'grad_e_w_out': 'grad_w', 'grad_o_w_in': 'grad_w', 'grad_o_conv_w': 'grad_w', 'grad_o_w_out': 'grad_w', 'grad_f_w_up': 'grad_w', 'grad_f_conv_w': 'grad_w', 'grad_f_conv_b': 'grad_w', 'grad_f_w_down': 'grad_w', 'grad_ple_w_proj': 'grad_w', 'grad_ple_w_gate': 'grad_w', 'grad_ln_g': 'grad_w', 'grad_ln_b': 'grad_w', 'delta_e_w_in': 'delta_w', 'delta_e_conv_a_w': 'delta_w', 'delta_e_conv_a_b': 'delta_w', 'delta_e_ln_a_g': 'delta_w', 'delta_e_ln_a_b': 'delta_w', 'delta_e_conv_b_w': 'delta_w', 'delta_e_conv_b_b': 'delta_w', 'delta_e_dt_bias': 'delta_w', 'delta_e_a_log': 'delta_w', 'delta_e_d_skip': 'delta_w', 'delta_e_norm_b_g': 'delta_w', 'delta_e_w_out': 'delta_w', 'delta_o_w_in': 'delta_w', 'delta_o_conv_w': 'delta_w', 'delta_o_w_out': 'delta_w', 'delta_f_w_up': 'delta_w', 'delta_f_conv_w': 'delta_w', 'delta_f_conv_b': 'delta_w', 'delta_f_w_down': 'delta_w', 'delta_ple_w_proj': 'delta_w', 'delta_ple_w_gate': 'delta_w', 'delta_ln_g': 'delta_w', 'delta_ln_b': 'delta_w', 'new_m_e_w_in': 'new_m', 'new_m_e_conv_a_w': 'new_m', 'new_m_e_conv_a_b': 'new_m', 'new_m_e_ln_a_g': 'new_m', 'new_m_e_ln_a_b': 'new_m', 'new_m_e_conv_b_w': 'new_m', 'new_m_e_conv_b_b': 'new_m', 'new_m_e_dt_bias': 'new_m', 'new_m_e_a_log': 'new_m', 'new_m_e_d_skip': 'new_m', 'new_m_e_norm_b_g': 'new_m', 'new_m_e_w_out': 'new_m', 'new_m_o_w_in': 'new_m', 'new_m_o_conv_w': 'new_m', 'new_m_o_w_out': 'new_m', 'new_m_f_w_up': 'new_m', 'new_m_f_conv_w': 'new_m', 'new_m_f_conv_b': 'new_m', 'new_m_f_w_down': 'new_m', 'new_m_ple_w_proj': 'new_m', 'new_m_ple_w_gate': 'new_m', 'new_m_ln_g': 'new_m', 'new_m_ln_b': 'new_m', 'new_v_e_w_in': 'new_v', 'new_v_e_conv_a_w': 'new_v', 'new_v_e_conv_a_b': 'new_v', 'new_v_e_ln_a_g': 'new_v', 'new_v_e_ln_a_b': 'new_v', 'new_v_e_conv_b_w': 'new_v', 'new_v_e_conv_b_b': 'new_v', 'new_v_e_dt_bias': 'new_v', 'new_v_e_a_log': 'new_v', 'new_v_e_d_skip': 'new_v', 'new_v_e_norm_b_g': 'new_v', 'new_v_e_w_out': 'new_v', 'new_v_o_w_in': 'new_v', 'new_v_o_conv_w': 'new_v', 'new_v_o_w_out': 'new_v', 'new_v_f_w_up': 'new_v', 'new_v_f_conv_w': 'new_v', 'new_v_f_conv_b': 'new_v', 'new_v_f_w_down': 'new_v', 'new_v_ple_w_proj': 'new_v', 'new_v_ple_w_gate': 'new_v', 'new_v_ln_g': 'new_v', 'new_v_ln_b': 'new_v'}


def _forward(args):
    return _fwd_reference(*[args[k] for k in FWD_PARAMS])


def _output_shape():
    out = _jax.eval_shape(lambda: _forward(_fwd_setup_inputs(0)))
    return out.shape, out.dtype

N_MICROBATCH = 1
ADAM_LR = 0.001
ADAM_B1 = 0.9
ADAM_B2 = 0.999
ADAM_EPS = 1e-08
ADAM_WD = 0.01
ADAM_STEP = 10
PER_EXAMPLE_BATCH_AXIS = {'x': 0, 'p': 1, 'loss_target': 0}
SHARED_INPUTS = []
_WEIGHT_DTYPES = {'e_w_in': _jnp.float32, 'e_conv_a_w': _jnp.float32, 'e_conv_a_b': _jnp.float32, 'e_ln_a_g': _jnp.float32, 'e_ln_a_b': _jnp.float32, 'e_conv_b_w': _jnp.float32, 'e_conv_b_b': _jnp.float32, 'e_dt_bias': _jnp.float32, 'e_a_log': _jnp.float32, 'e_d_skip': _jnp.float32, 'e_norm_b_g': _jnp.float32, 'e_w_out': _jnp.float32, 'o_w_in': _jnp.float32, 'o_conv_w': _jnp.float32, 'o_w_out': _jnp.float32, 'f_w_up': _jnp.float32, 'f_conv_w': _jnp.float32, 'f_conv_b': _jnp.float32, 'f_w_down': _jnp.float32, 'ple_w_proj': _jnp.float32, 'ple_w_gate': _jnp.float32, 'ln_g': _jnp.float32, 'ln_b': _jnp.float32}
MOMENT_SCALE = {'e_w_in': 2.476621e-02, 'e_conv_a_w': 2.165330e-02, 'e_conv_a_b': 6.252988e-02, 'e_ln_a_g': 3.341452e-02, 'e_ln_a_b': 3.852392e-02, 'e_conv_b_w': 2.544408e-02, 'e_conv_b_b': 4.222575e-02, 'e_dt_bias': 1.245367e-01, 'e_a_log': 9.513363e-02, 'e_d_skip': 2.518149e-01, 'e_norm_b_g': 3.621487e-02, 'e_w_out': 1.020033e-01, 'o_w_in': 4.539138e-02, 'o_conv_w': 4.643284e-02, 'o_w_out': 1.079317e-01, 'f_w_up': 1.755800e-02, 'f_conv_w': 1.739983e-02, 'f_conv_b': 1.910858e-02, 'f_w_down': 6.815207e-02, 'ple_w_proj': 6.019391e-02, 'ple_w_gate': 9.875164e-03, 'ln_g': 1.134827e+01, 'ln_b': 8.843450e-01}


def _to_microbatches(a, axis):
    t = _jnp.moveaxis(a, axis, 0)
    t = t.reshape((N_MICROBATCH, t.shape[0] // N_MICROBATCH) + t.shape[1:])
    return _jnp.moveaxis(t, 1, axis + 1)


def setup_inputs(seed: int = 0) -> dict:
    inp = _fwd_setup_inputs(seed)
    key = _jax.random.fold_in(_jax.random.key(seed), 7919)
    shape, _ = _output_shape()
    out = dict(inp)
    out["loss_target"] = _jax.random.normal(_jax.random.fold_in(key, 0), shape, _jnp.float32)
    for i, name in enumerate(TWIN_WEIGHTS):
        w = inp[name].astype(_jnp.float32)
        if MOMENT_SCALE is None:
            s = _jnp.sqrt(_jnp.mean(_jnp.square(w)) + 1e-30)
        else:
            s = MOMENT_SCALE[name]
        km, kv = _jax.random.split(_jax.random.fold_in(key, i + 1))
        out[name] = w
        out["m_" + name] = s * _jax.random.normal(km, w.shape, _jnp.float32)
        out["v_" + name] = (s * s) * _jax.random.uniform(kv, w.shape, _jnp.float32, 0.5, 1.5)
    if N_MICROBATCH > 1:
        for name, axis in PER_EXAMPLE_BATCH_AXIS.items():
            out[name] = _to_microbatches(out[name], axis)
    return {'x': out['x'], 'p': out['p'], 'e_w_in': out['e_w_in'], 'e_conv_a_w': out['e_conv_a_w'], 'e_conv_a_b': out['e_conv_a_b'], 'e_ln_a_g': out['e_ln_a_g'], 'e_ln_a_b': out['e_ln_a_b'], 'e_conv_b_w': out['e_conv_b_w'], 'e_conv_b_b': out['e_conv_b_b'], 'e_dt_bias': out['e_dt_bias'], 'e_a_log': out['e_a_log'], 'e_d_skip': out['e_d_skip'], 'e_norm_b_g': out['e_norm_b_g'], 'e_w_out': out['e_w_out'], 'o_w_in': out['o_w_in'], 'o_conv_w': out['o_conv_w'], 'o_w_out': out['o_w_out'], 'f_w_up': out['f_w_up'], 'f_conv_w': out['f_conv_w'], 'f_conv_b': out['f_conv_b'], 'f_w_down': out['f_w_down'], 'ple_w_proj': out['ple_w_proj'], 'ple_w_gate': out['ple_w_gate'], 'ln_g': out['ln_g'], 'ln_b': out['ln_b'], 'loss_target': out['loss_target'], 'm_e_w_in': out['m_e_w_in'], 'm_e_conv_a_w': out['m_e_conv_a_w'], 'm_e_conv_a_b': out['m_e_conv_a_b'], 'm_e_ln_a_g': out['m_e_ln_a_g'], 'm_e_ln_a_b': out['m_e_ln_a_b'], 'm_e_conv_b_w': out['m_e_conv_b_w'], 'm_e_conv_b_b': out['m_e_conv_b_b'], 'm_e_dt_bias': out['m_e_dt_bias'], 'm_e_a_log': out['m_e_a_log'], 'm_e_d_skip': out['m_e_d_skip'], 'm_e_norm_b_g': out['m_e_norm_b_g'], 'm_e_w_out': out['m_e_w_out'], 'm_o_w_in': out['m_o_w_in'], 'm_o_conv_w': out['m_o_conv_w'], 'm_o_w_out': out['m_o_w_out'], 'm_f_w_up': out['m_f_w_up'], 'm_f_conv_w': out['m_f_conv_w'], 'm_f_conv_b': out['m_f_conv_b'], 'm_f_w_down': out['m_f_w_down'], 'm_ple_w_proj': out['m_ple_w_proj'], 'm_ple_w_gate': out['m_ple_w_gate'], 'm_ln_g': out['m_ln_g'], 'm_ln_b': out['m_ln_b'], 'v_e_w_in': out['v_e_w_in'], 'v_e_conv_a_w': out['v_e_conv_a_w'], 'v_e_conv_a_b': out['v_e_conv_a_b'], 'v_e_ln_a_g': out['v_e_ln_a_g'], 'v_e_ln_a_b': out['v_e_ln_a_b'], 'v_e_conv_b_w': out['v_e_conv_b_w'], 'v_e_conv_b_b': out['v_e_conv_b_b'], 'v_e_dt_bias': out['v_e_dt_bias'], 'v_e_a_log': out['v_e_a_log'], 'v_e_d_skip': out['v_e_d_skip'], 'v_e_norm_b_g': out['v_e_norm_b_g'], 'v_e_w_out': out['v_e_w_out'], 'v_o_w_in': out['v_o_w_in'], 'v_o_conv_w': out['v_o_conv_w'], 'v_o_w_out': out['v_o_w_out'], 'v_f_w_up': out['v_f_w_up'], 'v_f_conv_w': out['v_f_conv_w'], 'v_f_conv_b': out['v_f_conv_b'], 'v_f_w_down': out['v_f_w_down'], 'v_ple_w_proj': out['v_ple_w_proj'], 'v_ple_w_gate': out['v_ple_w_gate'], 'v_ln_g': out['v_ln_g'], 'v_ln_b': out['v_ln_b']}


def _loss(weights, diff, rest, loss_target):
    with _jax.named_scope("forward"):
        args = {**rest, TWIN_DIFF_INPUT: diff, **{k: w.astype(_WEIGHT_DTYPES[k]) for k, w in weights.items()}}
        y = _forward(args)
    with _jax.named_scope("loss_head"):
        err = _jnp.square(y.astype(_jnp.float32) - loss_target)
        return 0.5 * _jnp.sum(_jnp.mean(err, axis=-1)) if err.ndim else 0.5 * err


def _adamw(w, g, m, v):
    m = ADAM_B1 * m + (1.0 - ADAM_B1) * g
    v = ADAM_B2 * v + (1.0 - ADAM_B2) * _jnp.square(g)
    m_hat = m / (1.0 - ADAM_B1 ** ADAM_STEP)
    v_hat = v / (1.0 - ADAM_B2 ** ADAM_STEP)
    delta = -ADAM_LR * (m_hat / (_jnp.sqrt(v_hat) + ADAM_EPS) + ADAM_WD * w)
    return delta, m, v


def reference(x, p, e_w_in, e_conv_a_w, e_conv_a_b, e_ln_a_g, e_ln_a_b, e_conv_b_w, e_conv_b_b, e_dt_bias, e_a_log, e_d_skip, e_norm_b_g, e_w_out, o_w_in, o_conv_w, o_w_out, f_w_up, f_conv_w, f_conv_b, f_w_down, ple_w_proj, ple_w_gate, ln_g, ln_b, loss_target, m_e_w_in, m_e_conv_a_w, m_e_conv_a_b, m_e_ln_a_g, m_e_ln_a_b, m_e_conv_b_w, m_e_conv_b_b, m_e_dt_bias, m_e_a_log, m_e_d_skip, m_e_norm_b_g, m_e_w_out, m_o_w_in, m_o_conv_w, m_o_w_out, m_f_w_up, m_f_conv_w, m_f_conv_b, m_f_w_down, m_ple_w_proj, m_ple_w_gate, m_ln_g, m_ln_b, v_e_w_in, v_e_conv_a_w, v_e_conv_a_b, v_e_ln_a_g, v_e_ln_a_b, v_e_conv_b_w, v_e_conv_b_b, v_e_dt_bias, v_e_a_log, v_e_d_skip, v_e_norm_b_g, v_e_w_out, v_o_w_in, v_o_conv_w, v_o_w_out, v_f_w_up, v_f_conv_w, v_f_conv_b, v_f_w_down, v_ple_w_proj, v_ple_w_gate, v_ln_g, v_ln_b):
    given = dict(x=x, p=p, e_w_in=e_w_in, e_conv_a_w=e_conv_a_w, e_conv_a_b=e_conv_a_b, e_ln_a_g=e_ln_a_g, e_ln_a_b=e_ln_a_b, e_conv_b_w=e_conv_b_w, e_conv_b_b=e_conv_b_b, e_dt_bias=e_dt_bias, e_a_log=e_a_log, e_d_skip=e_d_skip, e_norm_b_g=e_norm_b_g, e_w_out=e_w_out, o_w_in=o_w_in, o_conv_w=o_conv_w, o_w_out=o_w_out, f_w_up=f_w_up, f_conv_w=f_conv_w, f_conv_b=f_conv_b, f_w_down=f_w_down, ple_w_proj=ple_w_proj, ple_w_gate=ple_w_gate, ln_g=ln_g, ln_b=ln_b, loss_target=loss_target, m_e_w_in=m_e_w_in, m_e_conv_a_w=m_e_conv_a_w, m_e_conv_a_b=m_e_conv_a_b, m_e_ln_a_g=m_e_ln_a_g, m_e_ln_a_b=m_e_ln_a_b, m_e_conv_b_w=m_e_conv_b_w, m_e_conv_b_b=m_e_conv_b_b, m_e_dt_bias=m_e_dt_bias, m_e_a_log=m_e_a_log, m_e_d_skip=m_e_d_skip, m_e_norm_b_g=m_e_norm_b_g, m_e_w_out=m_e_w_out, m_o_w_in=m_o_w_in, m_o_conv_w=m_o_conv_w, m_o_w_out=m_o_w_out, m_f_w_up=m_f_w_up, m_f_conv_w=m_f_conv_w, m_f_conv_b=m_f_conv_b, m_f_w_down=m_f_w_down, m_ple_w_proj=m_ple_w_proj, m_ple_w_gate=m_ple_w_gate, m_ln_g=m_ln_g, m_ln_b=m_ln_b, v_e_w_in=v_e_w_in, v_e_conv_a_w=v_e_conv_a_w, v_e_conv_a_b=v_e_conv_a_b, v_e_ln_a_g=v_e_ln_a_g, v_e_ln_a_b=v_e_ln_a_b, v_e_conv_b_w=v_e_conv_b_w, v_e_conv_b_b=v_e_conv_b_b, v_e_dt_bias=v_e_dt_bias, v_e_a_log=v_e_a_log, v_e_d_skip=v_e_d_skip, v_e_norm_b_g=v_e_norm_b_g, v_e_w_out=v_e_w_out, v_o_w_in=v_o_w_in, v_o_conv_w=v_o_conv_w, v_o_w_out=v_o_w_out, v_f_w_up=v_f_w_up, v_f_conv_w=v_f_conv_w, v_f_conv_b=v_f_conv_b, v_f_w_down=v_f_w_down, v_ple_w_proj=v_ple_w_proj, v_ple_w_gate=v_ple_w_gate, v_ln_g=v_ln_g, v_ln_b=v_ln_b)
    weights = {n: given[n] for n in TWIN_WEIGHTS}
    shared = {n: given[n] for n in SHARED_INPUTS}
    per_example = {n: given[n] for n in ['x', 'p']}
    grad_fn = _jax.value_and_grad(_loss, argnums=(0, 1))

    def one_microbatch(ex, loss_target):
        ex = dict(ex)
        diff = ex.pop(TWIN_DIFF_INPUT)
        return grad_fn(weights, diff, {**shared, **ex}, loss_target)

    if N_MICROBATCH == 1:
        loss, (grad_w, grad_x) = one_microbatch(per_example, given["loss_target"])
    else:
        def body(carry, xs):
            loss_sum, grad_sum = carry
            l_k, (gw_k, gx_k) = one_microbatch(xs[0], xs[1])
            with _jax.named_scope("update"):
                return (loss_sum + l_k, _jax.tree.map(_jnp.add, grad_sum, gw_k)), gx_k

        init = (_jnp.zeros((), _jnp.float32), _jax.tree.map(_jnp.zeros_like, weights))
        (loss, grad_w), grad_x = _jax.lax.scan(body, init, (per_example, given["loss_target"]))
    with _jax.named_scope("update"):
        delta_w, new_m, new_v = {}, {}, {}
        for n in TWIN_WEIGHTS:
            delta_w[n], new_m[n], new_v[n] = _adamw(weights[n], grad_w[n], given["m_" + n], given["v_" + n])
    return (loss, grad_x, *[grad_w[n] for n in TWIN_WEIGHTS], *[delta_w[n] for n in TWIN_WEIGHTS],
            *[new_m[n] for n in TWIN_WEIGHTS], *[new_v[n] for n in TWIN_WEIGHTS])
```

```python
import functools
import math

import jax
import jax.numpy as jnp
from jax import lax
from jax.experimental import pallas as pl
from jax.experimental.pallas import tpu as pltpu

F32 = jnp.float32
MXU_DTYPE = jnp.bfloat16
MESH_AXES = ("x", "y", "c")
N_DEV = 8
LANE = 128
SUBLANE = 8
ROWWISE_VMEM_BUDGET = 20 * 1024 * 1024
LN_EPS = 1e-5
CHUNK = 64
HEAD_DIM = 64
N_GROUPS = 4
N_STATE = 128
CONV_PAD = 32
CONV_ROWS = 256
ADAM_LR, ADAM_B1, ADAM_B2, ADAM_EPS, ADAM_WD, ADAM_STEP = 0.001, 0.9, 0.999, 1e-08, 0.01, 10

BIG = ("e_w_in", "e_w_out", "o_w_in", "o_w_out", "f_w_up", "f_w_down", "ple_w_proj", "ple_w_gate")
SMALL_SHARDED = ("e_conv_a_w", "e_conv_b_w", "o_conv_w", "f_conv_w", "ln_g", "ln_b")
REPLICATED = ("e_conv_a_b", "e_ln_a_g", "e_ln_a_b", "e_conv_b_b", "e_dt_bias", "e_a_log", "e_d_skip",
              "e_norm_b_g", "f_conv_b")
SHARD_AXIS = {"e_w_in": 2, "e_conv_a_w": 2, "e_conv_b_w": 2, "e_w_out": 1, "o_w_in": 2, "o_conv_w": 2,
              "o_w_out": 1, "f_w_up": 2, "f_conv_w": 2, "f_w_down": 1, "ple_w_proj": 2, "ple_w_gate": 1,
              "ln_g": 2, "ln_b": 2}
WEIGHTS = ("e_w_in", "e_conv_a_w", "e_conv_a_b", "e_ln_a_g", "e_ln_a_b", "e_conv_b_w", "e_conv_b_b", "e_dt_bias",
           "e_a_log", "e_d_skip", "e_norm_b_g", "e_w_out", "o_w_in", "o_conv_w", "o_w_out", "f_w_up", "f_conv_w",
           "f_conv_b", "f_w_down", "ple_w_proj", "ple_w_gate", "ln_g", "ln_b")

S = jax.ShapeDtypeStruct


class Win:
    def __init__(self, arr, w=None, idx=0, coef=1.0):
        self.arr, self.w, self.idx, self.coef = arr, (arr.shape[1] if w is None else w), idx, coef


def _win(a):
    return a if isinstance(a, Win) else Win(a)


def _pick(n, prefs):
    for p in prefs:
        if p <= n and n % p == 0:
            return p
    return n


_MM_DIMS = {"nn": (1, 0), "nt": (1, 1), "tn": (0, 0)}


def mm(name, a, b, mode):
    ca, cb = _MM_DIMS[mode]
    kdim = a.shape[ca]
    m = a.shape[1 - ca]
    n = b.shape[1 - cb]
    assert b.shape[cb] == kdim, (name, a.shape, b.shape, mode)
    tm = _pick(m, (1024, 512, 256, 128))
    tn = _pick(n, (1024, 512, 256, 128))
    tk = kdim if kdim <= 1024 else _pick(kdim, (512, 256, 128))
    nk = kdim // tk

    def body(a_ref, b_ref, o_ref, acc_ref):
        k = pl.program_id(2)

        @pl.when(k == 0)
        def _():
            acc_ref[...] = jnp.zeros_like(acc_ref)

        acc_ref[...] += lax.dot_general(a_ref[...].astype(MXU_DTYPE), b_ref[...].astype(MXU_DTYPE),
                                        (((ca,), (cb,)), ((), ())), preferred_element_type=F32)

        @pl.when(k == nk - 1)
        def _():
            o_ref[...] = acc_ref[...]

    a_spec = pl.BlockSpec((tm, tk), lambda i, j, k: (i, k)) if ca == 1 else pl.BlockSpec((tk, tm), lambda i, j, k: (k, i))
    b_spec = pl.BlockSpec((tk, tn), lambda i, j, k: (k, j)) if cb == 0 else pl.BlockSpec((tn, tk), lambda i, j, k: (j, k))
    return pl.pallas_call(
        body, name=name, grid=(m // tm, n // tn, nk),
        in_specs=[a_spec, b_spec], out_specs=pl.BlockSpec((tm, tn), lambda i, j, k: (i, j)),
        out_shape=S((m, n), F32), scratch_shapes=[pltpu.VMEM((tm, tn), F32)],
        compiler_params=pltpu.CompilerParams(dimension_semantics=("parallel", "parallel", "arbitrary")),
    )(a, b)


def _row_block(t, widths):
    tb = 512
    while tb > SUBLANE and (t % tb or tb * sum(widths) * 8 > ROWWISE_VMEM_BUDGET):
        tb //= 2
    return tb


def _tok_spec(tb, w):
    return pl.BlockSpec((tb, w.w), functools.partial(lambda i, idx: (i, idx), idx=w.idx))


def _par_spec(p):
    return pl.BlockSpec((1, p.shape[1]), lambda i: (0, 0))


def rowwise(name, fn, tok, par, out_widths, red_widths=()):
    tok = [_win(t) for t in tok]
    t = tok[0].arr.shape[0]
    tb = _row_block(t, [w.w for w in tok] + list(out_widths))
    n_tok, n_par, n_out = len(tok), len(par), len(out_widths)

    def body(*refs):
        ins = [r[...] for r in refs[:n_tok + n_par]]
        res = fn(*ins)
        out_refs = refs[n_tok + n_par:n_tok + n_par + n_out]
        red_refs = refs[n_tok + n_par + n_out:]
        for r, v in zip(out_refs, res[:n_out]):
            r[...] = v
        if red_refs:
            @pl.when(pl.program_id(0) == 0)
            def _():
                for r in red_refs:
                    r[...] = jnp.zeros_like(r)
            for r, v in zip(red_refs, res[n_out:]):
                r[...] += v

    outs = pl.pallas_call(
        body, name=name, grid=(t // tb,),
        in_specs=[_tok_spec(tb, w) for w in tok] + [_par_spec(p) for p in par],
        out_specs=[pl.BlockSpec((tb, w), lambda i: (i, 0)) for w in out_widths]
        + [pl.BlockSpec((1, w), lambda i: (0, 0)) for w in red_widths],
        out_shape=[S((t, w), F32) for w in out_widths] + [S((1, w), F32) for w in red_widths],
        compiler_params=pltpu.CompilerParams(dimension_semantics=("arbitrary",)),
    )(*[w.arr for w in tok], *par)
    return outs


def rowwise_bwd(name, fn, tok, par, cts, need=None):
    tok = [_win(t) for t in tok]
    cts = [[_win(c) for c in group] for group in cts]
    need = [True] * len(tok) if need is None else list(need)
    t = tok[0].arr.shape[0]
    flat_cts = [c for group in cts for c in group]
    d_widths = [w.w for w, nd in zip(tok, need) if nd]
    tb = _row_block(t, [w.w for w in tok] + [c.w for c in flat_cts] + d_widths)
    n_tok, n_par, n_ct = len(tok), len(par), len(flat_cts)
    n_d = len(d_widths)

    def body(*refs):
        tok_vals = [r[...] for r in refs[:n_tok]]
        par_vals = [r[...] for r in refs[n_tok:n_tok + n_par]]
        ct_refs = refs[n_tok + n_par:n_tok + n_par + n_ct]
        d_refs = refs[n_tok + n_par + n_ct:n_tok + n_par + n_ct + n_d]
        dp_refs = refs[n_tok + n_par + n_ct + n_d:]
        ct_vals, pos = [], 0
        for group in cts:
            acc = None
            for c in group:
                v = ct_refs[pos][...]
                if c.coef != 1.0:
                    v = v * c.coef
                acc = v if acc is None else acc + v
                pos += 1
            ct_vals.append(acc)
        diff_idx = [i for i, nd in enumerate(need) if nd]

        def g(*dargs):
            full = list(tok_vals)
            for i, v in zip(diff_idx, dargs[:n_d]):
                full[i] = v
            return tuple(fn(*full, *dargs[n_d:]))

        _, vjp = jax.vjp(g, *[tok_vals[i] for i in diff_idx], *par_vals)
        grads = vjp(tuple(ct_vals))
        for r, v in zip(d_refs, grads[:n_d]):
            r[...] = v
        if dp_refs:
            @pl.when(pl.program_id(0) == 0)
            def _():
                for r in dp_refs:
                    r[...] = jnp.zeros_like(r)
            for r, v in zip(dp_refs, grads[n_d:]):
                r[...] += v

    outs = pl.pallas_call(
        body, name=name, grid=(t // tb,),
        in_specs=[_tok_spec(tb, w) for w in tok] + [_par_spec(p) for p in par] + [_tok_spec(tb, c) for c in flat_cts],
        out_specs=[pl.BlockSpec((tb, w), lambda i: (i, 0)) for w in d_widths] + [_par_spec(p) for p in par],
        out_shape=[S((t, w), F32) for w in d_widths] + [S(p.shape, F32) for p in par],
        compiler_params=pltpu.CompilerParams(dimension_semantics=("arbitrary",)),
    )(*[w.arr for w in tok], *par, *[c.arr for c in flat_cts])
    return outs[:n_d], outs[n_d:]


def _sigmoid(x):
    return 1.0 / (1.0 + jnp.exp(-x))


def _silu(x):
    return x * _sigmoid(x)


def _ln(x, g, b):
    mu = jnp.mean(x, axis=-1, keepdims=True)
    var = jnp.mean(jnp.square(x - mu), axis=-1, keepdims=True)
    return (x - mu) * lax.rsqrt(var + LN_EPS) * g + b


def f_glu(ua, ug):
    return (ua * _sigmoid(ug),)


def f_ln_silu(a1, g, b):
    return (_silu(_ln(a1, g, b)),)


def f_silu3(a, b, c):
    return (_silu(a), _silu(b), _silu(c))


def f_softplus(dt_raw, bias):
    return (jax.nn.softplus(dt_raw + bias),)


def f_gate_rms(yssd, z, g):
    y = yssd * _silu(z)
    return (y * lax.rsqrt(jnp.mean(jnp.square(y), axis=-1, keepdims=True) + LN_EPS) * g,)


def f_ln(pre, g, b):
    return (_ln(pre, g, b),)


def f_mul(a, b):
    return (a * b,)


def f_act(h1, h2):
    return (_silu(h1) * h2,)


def f_gate_mul(pp, gt):
    return (pp * _sigmoid(gt),)


def conv_fwd(name, x, w, b):
    x = _win(x)
    t, c = x.arr.shape[0], x.w
    kw = w.shape[0]
    cb = LANE
    off = x.idx * (c // cb)
    rows = min(CONV_ROWS, t)
    has_b = b is not None

    def body(*refs):
        if has_b:
            x_ref, w_ref, b_ref, y_ref, xp_ref = refs
        else:
            x_ref, w_ref, y_ref, xp_ref = refs
        xp_ref[0:CONV_PAD, :] = jnp.zeros((CONV_PAD, cb), F32)
        xp_ref[CONV_PAD:CONV_PAD + t, :] = x_ref[...]

        def step(s, carry):
            base = pl.multiple_of(s * rows, rows)
            acc = jnp.zeros((rows, cb), F32)
            if has_b:
                acc = acc + b_ref[...]
            for k in range(kw):
                acc = acc + w_ref[k:k + 1, :] * xp_ref[pl.ds(base + CONV_PAD - (kw - 1) + k, rows), :]
            y_ref[pl.ds(base, rows), :] = acc
            return carry

        lax.fori_loop(0, t // rows, step, 0)

    in_specs = [pl.BlockSpec((t, cb), lambda j: (0, off + j)), pl.BlockSpec((kw, cb), lambda j: (0, j))]
    args = [x.arr, w]
    if has_b:
        in_specs.append(pl.BlockSpec((1, cb), lambda j: (0, j)))
        args.append(b)
    return pl.pallas_call(
        body, name=name, grid=(c // cb,), in_specs=in_specs,
        out_specs=pl.BlockSpec((t, cb), lambda j: (0, j)), out_shape=S((t, c), F32),
        scratch_shapes=[pltpu.VMEM((CONV_PAD + t, cb), F32)],
        compiler_params=pltpu.CompilerParams(dimension_semantics=("parallel",)),
    )(*args)


def conv_bwd(name, x, dy, w):
    x, dy = _win(x), _win(dy)
    t, c = x.arr.shape[0], x.w
    kw = w.shape[0]
    cb = LANE
    xoff = x.idx * (c // cb)
    dyoff = dy.idx * (c // cb)
    rows = min(CONV_ROWS, t)

    def body(x_ref, dy_ref, w_ref, dx_ref, dw_ref, db_ref, xp_ref, dyp_ref):
        xp_ref[0:CONV_PAD, :] = jnp.zeros((CONV_PAD, cb), F32)
        xp_ref[CONV_PAD:CONV_PAD + t, :] = x_ref[...]
        dyp_ref[0:t, :] = dy_ref[...]
        dyp_ref[t:t + CONV_PAD, :] = jnp.zeros((CONV_PAD, cb), F32)

        def fold(v):
            return jnp.sum(v.reshape(rows // SUBLANE, SUBLANE, cb), axis=0)

        def step(s, carry):
            base = pl.multiple_of(s * rows, rows)
            dyc = dy_ref[pl.ds(base, rows), :]
            acc = jnp.zeros((rows, cb), F32)
            new = []
            for k in range(kw):
                acc = acc + w_ref[k:k + 1, :] * dyp_ref[pl.ds(base + (kw - 1) - k, rows), :]
                new.append(carry[k] + fold(dyc * xp_ref[pl.ds(base + CONV_PAD - (kw - 1) + k, rows), :]))
            new.append(carry[kw] + fold(dyc))
            dx_ref[pl.ds(base, rows), :] = acc
            return tuple(new)

        init = tuple(jnp.zeros((SUBLANE, cb), F32) for _ in range(kw + 1))
        parts = lax.fori_loop(0, t // rows, step, init)
        for k in range(kw):
            dw_ref[k:k + 1, :] = jnp.sum(parts[k], axis=0, keepdims=True)
        db_ref[...] = jnp.sum(parts[kw], axis=0, keepdims=True)

    return pl.pallas_call(
        body, name=name, grid=(c // cb,),
        in_specs=[pl.BlockSpec((t, cb), lambda j: (0, xoff + j)), pl.BlockSpec((t, cb), lambda j: (0, dyoff + j)),
                  pl.BlockSpec((kw, cb), lambda j: (0, j))],
        out_specs=[pl.BlockSpec((t, cb), lambda j: (0, j)), pl.BlockSpec((kw, cb), lambda j: (0, j)),
                   pl.BlockSpec((1, cb), lambda j: (0, j))],
        out_shape=[S((t, c), F32), S((kw, c), F32), S((1, c), F32)],
        scratch_shapes=[pltpu.VMEM((CONV_PAD + t, cb), F32), pltpu.VMEM((CONV_PAD + t, cb), F32)],
        compiler_params=pltpu.CompilerParams(dimension_semantics=("parallel",)),
    )(x.arr, dy.arr, w)


def _bdot(a, b, ca, cb):
    return lax.dot_general(a.astype(MXU_DTYPE), b.astype(MXU_DTYPE), (((ca,), (cb,)), ((0,), (0,))),
                           preferred_element_type=F32)


@jax.custom_vjp
def bmm_nn(a, b):
    return _bdot(a, b, 2, 1)


bmm_nn.defvjp(lambda a, b: (_bdot(a, b, 2, 1), (a, b)),
              lambda r, g: (_bdot(g, r[1], 2, 2), _bdot(r[0], g, 1, 1)))


@jax.custom_vjp
def bmm_tn(a, b):
    return _bdot(a, b, 1, 1)


bmm_tn.defvjp(lambda a, b: (_bdot(a, b, 1, 1), (a, b)),
              lambda r, g: (_bdot(r[1], g, 2, 2), _bdot(r[0], g, 2, 1)))


@jax.custom_vjp
def bmm_nt(a, b):
    return _bdot(a, b, 2, 2)


bmm_nt.defvjp(lambda a, b: (_bdot(a, b, 2, 2), (a, b)),
              lambda r, g: (_bdot(g, r[1], 2, 1), _bdot(g, r[0], 1, 1)))


def ssd_chunk(x, dt, bm, cm, hprev, a_log, dsk):
    hg, ln, _ = x.shape
    n = bm.shape[1]
    ii = lax.broadcasted_iota(jnp.int32, (ln, ln), 0)
    jj = lax.broadcasted_iota(jnp.int32, (ln, ln), 1)
    tril, eye, triu = (ii >= jj)[None], (ii == jj)[None], (ii <= jj)[None]
    da = dt * (-jnp.exp(a_log))
    da_row = jnp.sum(jnp.where(eye, da, 0.0), axis=1, keepdims=True)
    dt_row = jnp.sum(jnp.where(eye, dt, 0.0), axis=1, keepdims=True)
    cum_c = jnp.sum(jnp.where(tril, da_row, 0.0), axis=2, keepdims=True)
    cum_r = jnp.sum(jnp.where(triu, da, 0.0), axis=1, keepdims=True)
    last = jnp.sum(da, axis=1, keepdims=True)
    decay = jnp.where(tril, jnp.exp(jnp.where(tril, cum_c - cum_r, 0.0)), 0.0)
    cb = bmm_nt(cm[None], bm[None])
    y_diag = bmm_nn(cb * decay * dt_row, x)
    bb = jnp.broadcast_to(bm[None], (hg, ln, n))
    cc = jnp.broadcast_to(cm[None], (hg, ln, n))
    states = bmm_tn(x * (jnp.exp(last - cum_c) * dt), bb)
    y_off = bmm_nt(cc, hprev) * jnp.exp(cum_c)
    hnew = hprev * jnp.exp(last) + states
    return y_diag + y_off + dsk * x, hnew


def _ssd_dims(xh, bm):
    h, t, p = xh.shape
    hg = h // N_GROUPS
    n = bm.shape[1] // N_GROUPS
    return h, t, p, hg, n, t // CHUNK


def ssd_fwd(name, xh, dth, bm, cm, a_log, dsk):
    h, t, p, hg, n, nc = _ssd_dims(xh, bm)

    def body(al_ref, dk_ref, x_ref, dt_ref, b_ref, c_ref, y_ref, hp_ref, h_scr):
        @pl.when(pl.program_id(1) == 0)
        def _():
            h_scr[...] = jnp.zeros_like(h_scr)

        hprev = h_scr[...]
        hp_ref[:, 0] = hprev
        y, hnew = ssd_chunk(x_ref[...], dt_ref[...], b_ref[...], c_ref[...], hprev, al_ref[...], dk_ref[...])
        y_ref[...] = y
        h_scr[...] = hnew

    head = pl.BlockSpec((hg, 1, 1), lambda g, c: (g, 0, 0))
    return pl.pallas_call(
        body, name=name, grid=(N_GROUPS, nc),
        in_specs=[head, head, pl.BlockSpec((hg, CHUNK, p), lambda g, c: (g, c, 0)),
                  pl.BlockSpec((hg, CHUNK, 1), lambda g, c: (g, c, 0)),
                  pl.BlockSpec((CHUNK, n), lambda g, c: (c, g)), pl.BlockSpec((CHUNK, n), lambda g, c: (c, g))],
        out_specs=[pl.BlockSpec((hg, CHUNK, p), lambda g, c: (g, c, 0)),
                   pl.BlockSpec((hg, 1, p, n), lambda g, c: (g, c, 0, 0))],
        out_shape=[S((h, t, p), F32), S((h, nc, p, n), F32)],
        scratch_shapes=[pltpu.VMEM((hg, p, n), F32)],
        compiler_params=pltpu.CompilerParams(dimension_semantics=("parallel", "arbitrary")),
    )(a_log, dsk, xh, dth, bm, cm)


def ssd_bwd(name, xh, dth, bm, cm, a_log, dsk, hp, dyh):
    h, t, p, hg, n, nc = _ssd_dims(xh, bm)

    def body(al_ref, dk_ref, x_ref, dt_ref, b_ref, c_ref, hp_ref, dy_ref,
             dx_ref, ddt_ref, db_ref, dc_ref, dal_ref, ddk_ref, dh_scr):
        @pl.when(pl.program_id(1) == 0)
        def _():
            dh_scr[...] = jnp.zeros_like(dh_scr)
            dal_ref[...] = jnp.zeros_like(dal_ref)
            ddk_ref[...] = jnp.zeros_like(ddk_ref)

        _, vjp = jax.vjp(ssd_chunk, x_ref[...], dt_ref[...], b_ref[...], c_ref[...], hp_ref[:, 0],
                         al_ref[...], dk_ref[...])
        gx, gdt, gb, gc, ghp, gal, gdk = vjp((dy_ref[...], dh_scr[...]))
        dx_ref[...] = gx
        ddt_ref[...] = gdt
        db_ref[...] = gb
        dc_ref[...] = gc
        dh_scr[...] = ghp
        dal_ref[...] += gal
        ddk_ref[...] += gdk

    head = pl.BlockSpec((hg, 1, 1), lambda g, c: (g, 0, 0))
    xs = pl.BlockSpec((hg, CHUNK, p), lambda g, c: (g, nc - 1 - c, 0))
    ds = pl.BlockSpec((hg, CHUNK, 1), lambda g, c: (g, nc - 1 - c, 0))
    bs = pl.BlockSpec((CHUNK, n), lambda g, c: (nc - 1 - c, g))
    return pl.pallas_call(
        body, name=name, grid=(N_GROUPS, nc),
        in_specs=[head, head, xs, ds, bs, bs, pl.BlockSpec((hg, 1, p, n), lambda g, c: (g, nc - 1 - c, 0, 0)), xs],
        out_specs=[xs, ds, bs, bs, head, head],
        out_shape=[S((h, t, p), F32), S((h, t, 1), F32), S(bm.shape, F32), S(cm.shape, F32),
                   S((h, 1, 1), F32), S((h, 1, 1), F32)],
        scratch_shapes=[pltpu.VMEM((hg, p, n), F32)],
        compiler_params=pltpu.CompilerParams(dimension_semantics=("parallel", "arbitrary")),
    )(a_log, dsk, xh, dth, bm, cm, hp, dyh)


def to_heads(v):
    t = v.shape[0]
    return v.reshape(t, -1, HEAD_DIM).transpose(1, 0, 2)


def from_heads(v):
    return v.transpose(1, 0, 2).reshape(v.shape[1], -1)


def _alpha(depth):
    return (2.0 * depth) ** 0.25


def _pad_lanes(v):
    return jnp.pad(v, ((0, 0), (0, LANE - v.shape[1])))


def split_even_weights(w, j):
    d = w["e_w_in"].shape[1]
    da = w["e_conv_a_w"].shape[2]
    db = w["e_norm_b_g"].shape[1]
    gn = N_GROUPS * N_STATE
    nh = w["e_dt_bias"].shape[1]
    main = 2 * da + 2 * db + 2 * gn
    win = w["e_w_in"][j]
    ox = 2 * da + db
    cw, cbias = w["e_conv_b_w"][j], w["e_conv_b_b"][j][None]
    return dict(
        d=d, da=da, db=db, gn=gn, nh=nh, main=main,
        win_main=win[:, :main], win_dt=_pad_lanes(win[:, main:]),
        caw=w["e_conv_a_w"][j], cab=w["e_conv_a_b"][j][None], lag=w["e_ln_a_g"][j][None], lab=w["e_ln_a_b"][j][None],
        cw_xs=cw[:, :db], cw_b=cw[:, db:db + gn], cw_c=cw[:, db + gn:],
        cb_xs=cbias[:, :db], cb_b=cbias[:, db:db + gn], cb_c=cbias[:, db + gn:],
        dt_bias=_pad_lanes(w["e_dt_bias"][j][None]), a_log=w["e_a_log"][j].reshape(nh, 1, 1),
        dsk=w["e_d_skip"][j].reshape(nh, 1, 1), norm_g=w["e_norm_b_g"][j][None],
        wout_a=w["e_w_out"][j][:da], wout_b=w["e_w_out"][j][da:],
    )


def even_fwd(tag, x, lw, ln_g, ln_b, alpha):
    t = x.shape[0]
    da, db, gn, nh = lw["da"], lw["db"], lw["gn"], lw["nh"]
    u = mm(tag + "_win", x, lw["win_main"], "nn")
    udt = mm(tag + "_windt", x, lw["win_dt"], "nn")
    ua, ug, z, xs_pre = Win(u, da, 0), Win(u, da, 1), Win(u, db, 2 * da // db), Win(u, db, (2 * da + db) // db)
    b_pre, c_pre = Win(u, gn, (2 * da + 2 * db) // gn), Win(u, gn, (2 * da + 2 * db + gn) // gn)
    (a0,) = rowwise(tag + "_glu", f_glu, [ua, ug], [], [da])
    a1 = conv_fwd(tag + "_conva", a0, lw["caw"], lw["cab"])
    (ya,) = rowwise(tag + "_lna", f_ln_silu, [a1], [lw["lag"], lw["lab"]], [da])
    xs_c = conv_fwd(tag + "_convxs", xs_pre, lw["cw_xs"], lw["cb_xs"])
    b_c = conv_fwd(tag + "_convb", b_pre, lw["cw_b"], lw["cb_b"])
    c_c = conv_fwd(tag + "_convc", c_pre, lw["cw_c"], lw["cb_c"])
    xs, bm, cm = rowwise(tag + "_silu3", f_silu3, [xs_c, b_c, c_c], [], [db, gn, gn])
    (dt,) = rowwise(tag + "_dt", f_softplus, [udt], [lw["dt_bias"]], [LANE])
    xs_h = to_heads(xs)
    dt_h = dt[:, :nh].T.reshape(nh, t, 1)
    y_h, hp = ssd_fwd(tag + "_ssd", xs_h, dt_h, bm, cm, lw["a_log"], lw["dsk"])
    yssd = from_heads(y_h)
    (yb,) = rowwise(tag + "_gate", f_gate_rms, [yssd, z], [lw["norm_g"]], [db])
    ma = mm(tag + "_wouta", ya, lw["wout_a"], "nn")
    mb = mm(tag + "_woutb", yb, lw["wout_b"], "nn")

    def f_res(xv, mav, mbv, g, b):
        pre = alpha * xv + mav + mbv
        return _ln(pre, g, b), pre

    x1, pre = rowwise(tag + "_res", f_res, [x, ma, mb], [ln_g, ln_b], [x.shape[1]] * 2)
    saved = dict(x=x, u=u, udt=udt, a0=a0, a1=a1, ya=ya, xs_c=xs_c, b_c=b_c, c_c=c_c, xs_h=xs_h, dt_h=dt_h, bm=bm,
                 cm=cm, hp=hp, yssd=yssd, yb=yb, pre=pre)
    return x1, saved


def even_bwd(tag, dx1_pieces, sv, lw, ln_g, ln_b, alpha):
    t = sv["x"].shape[0]
    da, db, gn, nh = lw["da"], lw["db"], lw["gn"], lw["nh"]
    u, x = sv["u"], sv["x"]
    ua, ug, z, xs_pre = Win(u, da, 0), Win(u, da, 1), Win(u, db, 2 * da // db), Win(u, db, (2 * da + db) // db)
    b_pre, c_pre = Win(u, gn, (2 * da + 2 * db) // gn), Win(u, gn, (2 * da + 2 * db + gn) // gn)
    (dpre,), (dg0, db0) = rowwise_bwd(tag + "_res_b", f_ln, [sv["pre"]], [ln_g, ln_b], [dx1_pieces])
    dya = mm(tag + "_dya", dpre, lw["wout_a"], "nt")
    dyb = mm(tag + "_dyb", dpre, lw["wout_b"], "nt")
    dwout_a = mm(tag + "_dwouta", sv["ya"], dpre, "tn")
    dwout_b = mm(tag + "_dwoutb", sv["yb"], dpre, "tn")
    (dyssd, dz), (dnorm_g,) = rowwise_bwd(tag + "_gate_b", f_gate_rms, [sv["yssd"], z], [lw["norm_g"]], [[dyb]])
    dxs_h, ddt_h, dbm, dcm, dalog, ddsk = ssd_bwd(tag + "_ssd_b", sv["xs_h"], sv["dt_h"], sv["bm"], sv["cm"],
                                                  lw["a_log"], lw["dsk"], sv["hp"], to_heads(dyssd))
    dxs = from_heads(dxs_h)
    ddt = _pad_lanes(ddt_h.reshape(nh, t).T)
    (dudt,), (ddt_bias,) = rowwise_bwd(tag + "_dt_b", f_softplus, [sv["udt"]], [lw["dt_bias"]], [[ddt]])
    (dxs_c, db_c, dc_c), _ = rowwise_bwd(tag + "_silu3_b", f_silu3, [sv["xs_c"], sv["b_c"], sv["c_c"]], [],
                                         [[dxs], [dbm], [dcm]])
    dxs_pre, dcw_xs, dcb_xs = conv_bwd(tag + "_convxs_b", xs_pre, dxs_c, lw["cw_xs"])
    db_pre, dcw_b, dcb_b = conv_bwd(tag + "_convb_b", b_pre, db_c, lw["cw_b"])
    dc_pre, dcw_c, dcb_c = conv_bwd(tag + "_convc_b", c_pre, dc_c, lw["cw_c"])
    (da1,), (dlag, dlab) = rowwise_bwd(tag + "_lna_b", f_ln_silu, [sv["a1"]], [lw["lag"], lw["lab"]], [[dya]])
    da0, dcaw, dcab = conv_bwd(tag + "_conva_b", sv["a0"], da1, lw["caw"])
    (dua, dug), _ = rowwise_bwd(tag + "_glu_b", f_glu, [ua, ug], [], [[da0]])
    du = jnp.concatenate([dua, dug, dz, dxs_pre, db_pre, dc_pre], axis=1)
    dx_m = mm(tag + "_dxm", du, lw["win_main"], "nt")
    dx_dt = mm(tag + "_dxdt", dudt, lw["win_dt"], "nt")
    dwin_main = mm(tag + "_dwin", x, du, "tn")
    dwin_dt = mm(tag + "_dwindt", x, dudt, "tn")
    grads = dict(
        e_w_in=jnp.concatenate([dwin_main, dwin_dt[:, :nh]], axis=1),
        e_conv_a_w=dcaw, e_conv_a_b=dcab[0], e_ln_a_g=dlag[0], e_ln_a_b=dlab[0],
        e_conv_b_w=jnp.concatenate([dcw_xs, dcw_b, dcw_c], axis=1),
        e_conv_b_b=jnp.concatenate([dcb_xs, dcb_b, dcb_c], axis=1)[0],
        e_dt_bias=ddt_bias[0, :nh], e_a_log=dalog.reshape(nh), e_d_skip=ddsk.reshape(nh), e_norm_b_g=dnorm_g[0],
        e_w_out=jnp.concatenate([dwout_a, dwout_b], axis=0), ln_g0=dg0[0], ln_b0=db0[0],
    )
    return [Win(dpre, coef=alpha), dx_m, dx_dt], grads


def odd_fwd(tag, x, w, j, ln_g, ln_b, alpha):
    d = x.shape[1]
    u = mm(tag + "_win", x, w["o_w_in"][j], "nn")
    bg, cg, v = Win(u, d, 0), Win(u, d, 1), Win(u, d, 2)
    (s,) = rowwise(tag + "_cv", f_mul, [cg, v], [], [d])
    cs = conv_fwd(tag + "_conv", s, w["o_conv_w"][j], None)
    (m,) = rowwise(tag + "_bm", f_mul, [bg, cs], [], [d])
    mix = mm(tag + "_wout", m, w["o_w_out"][j], "nn")

    def f_res(xv, mv, g, b):
        pre = alpha * xv + mv
        return _ln(pre, g, b), pre

    x1, pre = rowwise(tag + "_res", f_res, [x, mix], [ln_g, ln_b], [d] * 2)
    return x1, dict(x=x, u=u, s=s, cs=cs, m=m, pre=pre)


def odd_bwd(tag, dx1_pieces, sv, w, j, ln_g, ln_b, alpha):
    x, u = sv["x"], sv["u"]
    d = x.shape[1]
    bg, cg, v = Win(u, d, 0), Win(u, d, 1), Win(u, d, 2)
    (dpre,), (dg0, db0) = rowwise_bwd(tag + "_res_b", f_ln, [sv["pre"]], [ln_g, ln_b], [dx1_pieces])
    dm = mm(tag + "_dm", dpre, w["o_w_out"][j], "nt")
    dwout = mm(tag + "_dwout", sv["m"], dpre, "tn")
    (dbg, dcs), _ = rowwise_bwd(tag + "_bm_b", f_mul, [bg, sv["cs"]], [], [[dm]])
    ds, dcw, _ = conv_bwd(tag + "_conv_b", sv["s"], dcs, w["o_conv_w"][j])
    (dcg, dv), _ = rowwise_bwd(tag + "_cv_b", f_mul, [cg, v], [], [[ds]])
    du = jnp.concatenate([dbg, dcg, dv], axis=1)
    dx_u = mm(tag + "_dx", du, w["o_w_in"][j], "nt")
    dwin = mm(tag + "_dwin", x, du, "tn")
    grads = dict(o_w_in=dwin, o_conv_w=dcw, o_w_out=dwout, ln_g0=dg0[0], ln_b0=db0[0])
    return [Win(dpre, coef=alpha), dx_u], grads


def ffn_fwd(tag, x1, p_i, w, i, ln_g, ln_b, alpha):
    d = x1.shape[1]
    ff = w["f_w_down"].shape[1]
    hpre = mm(tag + "_wup", x1, w["f_w_up"][i], "nn")
    h = conv_fwd(tag + "_fconv", hpre, w["f_conv_w"][i], w["f_conv_b"][i][None])
    (act,) = rowwise(tag + "_act", f_act, [Win(h, ff, 0), Win(h, ff, 1)], [], [ff])
    ffn = mm(tag + "_wdown", act, w["f_w_down"][i], "nn")
    pp = mm(tag + "_pproj", p_i, w["ple_w_proj"][i], "nn")
    gt = mm(tag + "_pgate", x1, w["ple_w_gate"][i], "nn")

    def f_res2(xv, fv, ppv, gtv, g, b):
        pre = alpha * xv + fv + ppv * _sigmoid(gtv)
        return _ln(pre, g, b), pre

    x2, pre = rowwise(tag + "_res2", f_res2, [x1, ffn, pp, gt], [ln_g, ln_b], [d] * 2)
    return x2, dict(x1=x1, hpre=hpre, h=h, act=act, pp=pp, gt=gt, pre=pre)


def ffn_bwd(tag, dx2_pieces, sv, p_i, w, i, ln_g, ln_b, alpha):
    x1 = sv["x1"]
    ff = w["f_w_down"].shape[1]
    (dpre,), (dg1, db1) = rowwise_bwd(tag + "_res2_b", f_ln, [sv["pre"]], [ln_g, ln_b], [dx2_pieces])
    (dpp, dgt), _ = rowwise_bwd(tag + "_pg_b", f_gate_mul, [sv["pp"], sv["gt"]], [], [[dpre]])
    dwproj = mm(tag + "_dwproj", p_i, dpp, "tn")
    dwgate = mm(tag + "_dwgate", x1, dgt, "tn")
    dx1_a = mm(tag + "_dx1a", dgt, w["ple_w_gate"][i], "nt")
    dact = mm(tag + "_dact", dpre, w["f_w_down"][i], "nt")
    dwdown = mm(tag + "_dwdown", sv["act"], dpre, "tn")
    h = sv["h"]
    (dh1, dh2), _ = rowwise_bwd(tag + "_act_b", f_act, [Win(h, ff, 0), Win(h, ff, 1)], [], [[dact]])
    dh = jnp.concatenate([dh1, dh2], axis=1)
    dhpre, dfcw, dfcb = conv_bwd(tag + "_fconv_b", sv["hpre"], dh, w["f_conv_w"][i])
    dwup = mm(tag + "_dwup", x1, dhpre, "tn")
    dx1_b = mm(tag + "_dx1b", dhpre, w["f_w_up"][i], "nt")
    grads = dict(f_w_up=dwup, f_conv_w=dfcw, f_conv_b=dfcb[0], f_w_down=dwdown, ple_w_proj=dwproj, ple_w_gate=dwgate,
                 ln_g1=dg1[0], ln_b1=db1[0])
    return [Win(dpre, coef=alpha), dx1_a, dx1_b], grads


def local_step(x, p, w, target):
    depth = w["ln_g"].shape[0]
    alpha = _alpha(depth)
    d = x.shape[1]
    saved = []
    h = x
    for i in range(depth):
        j = i // 2
        g0, b0, g1, b1 = w["ln_g"][i, 0][None], w["ln_b"][i, 0][None], w["ln_g"][i, 1][None], w["ln_b"][i, 1][None]
        tag = "l%d" % i
        if i % 2 == 0:
            lw = split_even_weights(w, j)
            h, sv_m = even_fwd(tag, h, lw, g0, b0, alpha)
        else:
            lw = None
            h, sv_m = odd_fwd(tag, h, w, j, g0, b0, alpha)
        h, sv_f = ffn_fwd(tag, h, p[i], w, i, g1, b1, alpha)
        saved.append((lw, sv_m, sv_f, (g0, b0, g1, b1)))

    def f_loss(xf, tg):
        diff = xf - tg
        sq = jnp.sum(jnp.sum(jnp.square(diff), axis=1, keepdims=True), axis=0, keepdims=True)
        return diff * (1.0 / d), jnp.broadcast_to(sq, (1, LANE))

    dxf, sq = rowwise("loss", f_loss, [h, target], [], [d], red_widths=[LANE])
    loss = sq[0, 0] * (0.5 / d)

    per_layer = []
    pieces = [dxf]
    for i in reversed(range(depth)):
        j = i // 2
        lw, sv_m, sv_f, (g0, b0, g1, b1) = saved[i]
        tag = "l%d" % i
        pieces, gf = ffn_bwd(tag, pieces, sv_f, p[i], w, i, g1, b1, alpha)
        if i % 2 == 0:
            pieces, gm = even_bwd(tag, pieces, sv_m, lw, g0, b0, alpha)
        else:
            pieces, gm = odd_bwd(tag, pieces, sv_m, w, j, g0, b0, alpha)
        per_layer.append((i, gm, gf))

    def f_sum(*vs):
        acc = None
        for v, c in zip(vs, [pc.coef for pc in map(_win, pieces)]):
            v = v if c == 1.0 else v * c
            acc = v if acc is None else acc + v
        return (acc,)

    (grad_x,) = rowwise("grad_x", f_sum, [Win(_win(pc).arr) for pc in pieces], [], [d])

    by_layer = {i: (gm, gf) for i, gm, gf in per_layer}
    grads = {}
    n_even, n_odd = (depth + 1) // 2, depth // 2
    for name in ("e_w_in", "e_conv_a_w", "e_conv_a_b", "e_ln_a_g", "e_ln_a_b", "e_conv_b_w", "e_conv_b_b", "e_dt_bias",
                 "e_a_log", "e_d_skip", "e_norm_b_g", "e_w_out"):
        grads[name] = jnp.stack([by_layer[2 * j][0][name] for j in range(n_even)])
    for name in ("o_w_in", "o_conv_w", "o_w_out"):
        grads[name] = jnp.stack([by_layer[2 * j + 1][0][name] for j in range(n_odd)])
    for name in ("f_w_up", "f_conv_w", "f_conv_b", "f_w_down", "ple_w_proj", "ple_w_gate"):
        grads[name] = jnp.stack([by_layer[i][1][name] for i in range(depth)])
    grads["ln_g"] = jnp.stack([jnp.stack([by_layer[i][0]["ln_g0"], by_layer[i][1]["ln_g1"]]) for i in range(depth)])
    grads["ln_b"] = jnp.stack([jnp.stack([by_layer[i][0]["ln_b0"], by_layer[i][1]["ln_b1"]]) for i in range(depth)])
    return loss, grad_x, grads


_ANY = pl.BlockSpec(memory_space=pl.ANY)
_MESH = pl.DeviceIdType.MESH


def all_gather(name, xl):
    r, c_ = xl.shape

    def body(x_ref, out_ref, send_sems, recv_sems, local_sem):
        x, y, c = lax.axis_index("x"), lax.axis_index("y"), lax.axis_index("c")
        me, sibling = (x, y, c), (x, y, 1 - c)
        chips = [(1 - x, y), (x, 1 - y), (1 - x, 1 - y)]

        def slot(px, py, pc):
            return out_ref.at[4 * px + 2 * py + pc]

        def copy(k, block, to, src=None):
            return pltpu.make_async_remote_copy(
                src_ref=slot(*block) if src is None else src, dst_ref=slot(*block),
                send_sem=send_sems.at[k], recv_sem=recv_sems.at[k], device_id=to, device_id_type=_MESH)

        mine = pltpu.make_async_copy(x_ref, slot(*me), local_sem)
        mine.start()
        first = [copy(0, me, sibling, src=x_ref)]
        first += [copy(1 + j, me, (*chip, c), src=x_ref) for j, chip in enumerate(chips)]
        for cp in first:
            cp.start()
        passed = [copy(4 + j, (*chip, c), sibling) for j, chip in enumerate(chips)]
        for j, chip in enumerate(chips):
            copy(1 + j, (*chip, c), me).wait_recv()
            passed[j].start()
        copy(0, sibling, me).wait_recv()
        for j, chip in enumerate(chips):
            copy(4 + j, (*chip, 1 - c), me).wait_recv()
        for cp in first + passed:
            cp.wait_send()
        mine.wait()

    return pl.pallas_call(
        body, name=name, out_shape=S((N_DEV, r, c_), xl.dtype), in_specs=[_ANY], out_specs=_ANY,
        scratch_shapes=[pltpu.SemaphoreType.DMA((7,)), pltpu.SemaphoreType.DMA((7,)), pltpu.SemaphoreType.DMA],
    )(xl)


def exchange_sibling(name, g4):
    _, _, r, c_ = g4.shape

    def body(g_ref, out_ref, send_sems, recv_sems):
        x, y, c = lax.axis_index("x"), lax.axis_index("y"), lax.axis_index("c")
        copies = [pltpu.make_async_remote_copy(
            src_ref=g_ref.at[q, 1 - c], dst_ref=out_ref.at[q], send_sem=send_sems.at[q], recv_sem=recv_sems.at[q],
            device_id=(x, y, 1 - c), device_id_type=_MESH) for q in range(4)]
        for cp in copies:
            cp.start()
        for cp in copies:
            cp.wait()

    return pl.pallas_call(
        body, name=name, out_shape=S((4, r, c_), g4.dtype), in_specs=[_ANY], out_specs=_ANY,
        scratch_shapes=[pltpu.SemaphoreType.DMA((4,)), pltpu.SemaphoreType.DMA((4,))],
    )(g4)


def exchange_chips(name, p4):
    _, r, c_ = p4.shape

    def body(p_ref, out_ref, send_sems, recv_sems):
        x, y, c = lax.axis_index("x"), lax.axis_index("y"), lax.axis_index("c")
        chips = [(1 - x, y), (x, 1 - y), (1 - x, 1 - y)]
        copies = [pltpu.make_async_remote_copy(
            src_ref=p_ref.at[2 * cx + cy], dst_ref=out_ref.at[k], send_sem=send_sems.at[k], recv_sem=recv_sems.at[k],
            device_id=(cx, cy, c), device_id_type=_MESH) for k, (cx, cy) in enumerate(chips)]
        for cp in copies:
            cp.start()
        for cp in copies:
            cp.wait()

    return pl.pallas_call(
        body, name=name, out_shape=S((3, r, c_), p4.dtype), in_specs=[_ANY], out_specs=_ANY,
        scratch_shapes=[pltpu.SemaphoreType.DMA((3,)), pltpu.SemaphoreType.DMA((3,))],
    )(p4)


def sum_with_sibling(name, g4, recv, core):
    _, _, r, c_ = g4.shape
    tr = _pick(r, (256, 128, 64, 32, 16, 8))

    def body(core_ref, g_ref, r_ref, o_ref):
        o_ref[...] = g_ref[0] + r_ref[...]

    return pl.pallas_call(
        body, name=name, out_shape=S((4, r, c_), F32),
        grid_spec=pltpu.PrefetchScalarGridSpec(
            num_scalar_prefetch=1, grid=(4, r // tr),
            in_specs=[pl.BlockSpec((1, 1, tr, c_), lambda q, i, cr: (q, cr[0], i, 0)),
                      pl.BlockSpec((1, tr, c_), lambda q, i, cr: (q, i, 0))],
            out_specs=pl.BlockSpec((1, tr, c_), lambda q, i, cr: (q, i, 0))),
    )(core, g4, recv)


def sum_with_chips(name, p4, recv, chip):
    _, r, c_ = p4.shape
    tr = _pick(r, (256, 128, 64, 32, 16, 8))

    def body(chip_ref, p_ref, r_ref, o_ref):
        o_ref[...] = ((p_ref[0] + r_ref[0]) + r_ref[1]) + r_ref[2]

    return pl.pallas_call(
        body, name=name, out_shape=S((r, c_), F32),
        grid_spec=pltpu.PrefetchScalarGridSpec(
            num_scalar_prefetch=1, grid=(r // tr,),
            in_specs=[pl.BlockSpec((1, tr, c_), lambda i, ch: (ch[0], i, 0)),
                      pl.BlockSpec((3, tr, c_), lambda i, ch: (0, i, 0))],
            out_specs=pl.BlockSpec((tr, c_), lambda i, ch: (i, 0))),
    )(chip, p4, recv)


def sum_devices(name, g8):
    _, r, c_ = g8.shape

    def body(g_ref, o_ref):
        acc = g_ref[0]
        for k in range(1, N_DEV):
            acc = acc + g_ref[k]
        o_ref[...] = acc

    return pl.pallas_call(body, name=name, out_shape=S((r, c_), F32))(g8)


def _flatten(parts, cols, row_mult):
    flat = jnp.concatenate([v.reshape(-1) for v in parts])
    n = flat.shape[0]
    rows = -(-n // (cols * row_mult)) * row_mult
    return jnp.pad(flat, (0, rows * cols - n)).reshape(rows, cols)


def _unshard(seg, axis):
    if axis == 2:
        return seg.transpose(1, 2, 0, 3).reshape(seg.shape[1], seg.shape[2], -1)
    return seg.transpose(1, 0, 2, 3).reshape(seg.shape[1], -1, seg.shape[3])


def _reshard(full, axis):
    l, a, b = full.shape
    if axis == 2:
        return full.reshape(l, a, N_DEV, b // N_DEV).transpose(2, 0, 1, 3).reshape(N_DEV, -1)
    return full.reshape(l, N_DEV, a // N_DEV, b).transpose(1, 0, 2, 3).reshape(N_DEV, -1)


def gather_weights(name, local, names, dtype, cols):
    flat = _flatten([local[n].astype(dtype) for n in names], cols, 1)
    got = all_gather(name, flat).reshape(N_DEV, -1)
    full, off = {}, 0
    for n in names:
        size = math.prod(local[n].shape)
        full[n] = _unshard(got[:, off:off + size].reshape((N_DEV,) + local[n].shape), SHARD_AXIS[n])
        off += size
    return full


def reduce_scatter_grads(grads, names, local_shapes):
    cols = 1024
    dev_major = jnp.concatenate([_reshard(grads[n], SHARD_AXIS[n]) for n in names], axis=1)
    n = dev_major.shape[1]
    rows = -(-n // (cols * 256)) * 256
    dev_major = jnp.pad(dev_major, ((0, 0), (0, rows * cols - n)))
    g4 = dev_major.reshape(4, 2, rows, cols)
    x, y, c = lax.axis_index("x"), lax.axis_index("y"), lax.axis_index("c")
    from_sibling = exchange_sibling("rs_sibling", g4)
    p4 = sum_with_sibling("rs_sum_sibling", g4, from_sibling, c.astype(jnp.int32).reshape(1))
    from_chips = exchange_chips("rs_chips", p4)
    mine = sum_with_chips("rs_sum_chips", p4, from_chips, (2 * x + y).astype(jnp.int32).reshape(1)).reshape(-1)
    out, off = {}, 0
    for nm in names:
        size = math.prod(local_shapes[nm])
        out[nm] = mine[off:off + size].reshape(local_shapes[nm])
        off += size
    return out


def all_reduce_small(grads, names):
    flat = _flatten([grads[n] for n in names], LANE, SUBLANE)
    total = sum_devices("ar_sum", all_gather("ar_gather", flat)).reshape(-1)
    out, off = {}, 0
    for nm in names:
        size = math.prod(grads[nm].shape)
        out[nm] = total[off:off + size].reshape(grads[nm].shape)
        off += size
    return out


def adamw(name, w, g, m, v):
    shape = w.shape
    cols = shape[-1]
    rows = math.prod(shape[:-1])
    tr = _pick(rows, (256, 128, 64, 32, 16, 8)) if rows * cols > 256 * 1024 else rows
    c1 = 1.0 - ADAM_B1 ** ADAM_STEP
    c2 = 1.0 - ADAM_B2 ** ADAM_STEP

    def body(w_ref, g_ref, m_ref, v_ref, d_ref, nm_ref, nv_ref):
        gv = g_ref[...]
        m2 = ADAM_B1 * m_ref[...] + (1.0 - ADAM_B1) * gv
        v2 = ADAM_B2 * v_ref[...] + (1.0 - ADAM_B2) * jnp.square(gv)
        d_ref[...] = -ADAM_LR * ((m2 / c1) / (jnp.sqrt(v2 / c2) + ADAM_EPS) + ADAM_WD * w_ref[...])
        nm_ref[...] = m2
        nv_ref[...] = v2

    spec = pl.BlockSpec((tr, cols), lambda i: (i, 0))
    outs = pl.pallas_call(
        body, name=name, grid=(rows // tr,), in_specs=[spec] * 4, out_specs=[spec] * 3,
        out_shape=[S((rows, cols), F32)] * 3,
        compiler_params=pltpu.CompilerParams(dimension_semantics=("parallel",)),
    )(*[a.reshape(rows, cols) for a in (w, g, m, v)])
    return tuple(o.reshape(shape) for o in outs)


def kernel(x, p, e_w_in, e_conv_a_w, e_conv_a_b, e_ln_a_g, e_ln_a_b, e_conv_b_w, e_conv_b_b, e_dt_bias, e_a_log, e_d_skip, e_norm_b_g, e_w_out, o_w_in, o_conv_w, o_w_out, f_w_up, f_conv_w, f_conv_b, f_w_down, ple_w_proj, ple_w_gate, ln_g, ln_b, loss_target, m_e_w_in, m_e_conv_a_w, m_e_conv_a_b, m_e_ln_a_g, m_e_ln_a_b, m_e_conv_b_w, m_e_conv_b_b, m_e_dt_bias, m_e_a_log, m_e_d_skip, m_e_norm_b_g, m_e_w_out, m_o_w_in, m_o_conv_w, m_o_w_out, m_f_w_up, m_f_conv_w, m_f_conv_b, m_f_w_down, m_ple_w_proj, m_ple_w_gate, m_ln_g, m_ln_b, v_e_w_in, v_e_conv_a_w, v_e_conv_a_b, v_e_ln_a_g, v_e_ln_a_b, v_e_conv_b_w, v_e_conv_b_b, v_e_dt_bias, v_e_a_log, v_e_d_skip, v_e_norm_b_g, v_e_w_out, v_o_w_in, v_o_conv_w, v_o_w_out, v_f_w_up, v_f_conv_w, v_f_conv_b, v_f_w_down, v_ple_w_proj, v_ple_w_gate, v_ln_g, v_ln_b):
    args = locals()
    local = {n: args[n] for n in WEIGHTS}
    mom = {n: args["m_" + n] for n in WEIGHTS}
    var = {n: args["v_" + n] for n in WEIGHTS}

    full = {n: local[n] for n in REPLICATED}
    full.update(gather_weights("ag_big", local, BIG, MXU_DTYPE, 1024))
    full.update(gather_weights("ag_small", local, SMALL_SHARDED, F32, LANE))

    loss_local, grad_x, grads = local_step(x[0], p[:, 0], full, loss_target[0])
    loss = lax.psum(loss_local, MESH_AXES)

    sharded = BIG + SMALL_SHARDED
    g_local = reduce_scatter_grads(grads, sharded, {n: local[n].shape for n in sharded})
    g_local.update(all_reduce_small(grads, REPLICATED))

    delta, new_m, new_v = {}, {}, {}
    for n in WEIGHTS:
        delta[n], new_m[n], new_v[n] = adamw("adamw_" + n, local[n], g_local[n], mom[n], var[n])
    return (loss, grad_x[None], *[g_local[n] for n in WEIGHTS], *[delta[n] for n in WEIGHTS],
            *[new_m[n] for n in WEIGHTS], *[new_v[n] for n in WEIGHTS])
```

```python
import functools
import math

import jax
import jax.numpy as jnp
from jax import lax
from jax.experimental import pallas as pl
from jax.experimental.pallas import tpu as pltpu

F32 = jnp.float32
MXU_DTYPE = jnp.bfloat16
MESH_AXES = ("x", "y", "c")
N_DEV = 8
LANE = 128
SUBLANE = 8
ROWWISE_VMEM_BUDGET = 20 * 1024 * 1024
LN_EPS = 1e-5
CHUNK = 64
HEAD_DIM = 64
N_GROUPS = 4
N_STATE = 128
CONV_PAD = 32
CONV_ROWS = 256
ADAM_LR, ADAM_B1, ADAM_B2, ADAM_EPS, ADAM_WD, ADAM_STEP = 0.001, 0.9, 0.999, 1e-08, 0.01, 10

BIG = ("e_w_in", "e_w_out", "o_w_in", "o_w_out", "f_w_up", "f_w_down", "ple_w_proj", "ple_w_gate")
SMALL_SHARDED = ("e_conv_a_w", "e_conv_b_w", "o_conv_w", "f_conv_w", "ln_g", "ln_b")
REPLICATED = ("e_conv_a_b", "e_ln_a_g", "e_ln_a_b", "e_conv_b_b", "e_dt_bias", "e_a_log", "e_d_skip",
              "e_norm_b_g", "f_conv_b")
TRANSPOSED = ("e_w_in", "o_w_in", "f_w_up", "ple_w_proj")
EXCHANGE_ORDER = ("e_w_out", "o_w_out", "ple_w_gate", "ple_w_proj", "o_w_in", "f_w_down", "f_w_up", "e_w_in")
WEIGHTS = ("e_w_in", "e_conv_a_w", "e_conv_a_b", "e_ln_a_g", "e_ln_a_b", "e_conv_b_w", "e_conv_b_b", "e_dt_bias",
           "e_a_log", "e_d_skip", "e_norm_b_g", "e_w_out", "o_w_in", "o_conv_w", "o_w_out", "f_w_up", "f_conv_w",
           "f_conv_b", "f_w_down", "ple_w_proj", "ple_w_gate", "ln_g", "ln_b")

S = jax.ShapeDtypeStruct


class Win:
    def __init__(self, arr, w=None, idx=0, coef=1.0):
        self.arr, self.w, self.idx, self.coef = arr, (arr.shape[1] if w is None else w), idx, coef


def _win(a):
    return a if isinstance(a, Win) else Win(a)


def _pick(n, prefs):
    for p in prefs:
        if p <= n and n % p == 0:
            return p
    return n


_MM_DIMS = {"nn": (1, 0), "nt": (1, 1), "tn": (0, 0)}


def mm(name, a, b, mode):
    ca, cb = _MM_DIMS[mode]
    kdim = a.shape[ca]
    m = a.shape[1 - ca]
    n = b.shape[1 - cb]
    assert b.shape[cb] == kdim, (name, a.shape, b.shape, mode)
    tm = _pick(m, (1024, 512, 256, 128))
    tn = _pick(n, (1024, 512, 256, 128))
    tk = kdim if kdim <= 1024 else _pick(kdim, (512, 256, 128))
    nk = kdim // tk

    def body(a_ref, b_ref, o_ref, acc_ref):
        k = pl.program_id(2)

        @pl.when(k == 0)
        def _():
            acc_ref[...] = jnp.zeros_like(acc_ref)

        acc_ref[...] += lax.dot_general(a_ref[...].astype(MXU_DTYPE), b_ref[...].astype(MXU_DTYPE),
                                        (((ca,), (cb,)), ((), ())), preferred_element_type=F32)

        @pl.when(k == nk - 1)
        def _():
            o_ref[...] = acc_ref[...]

    a_spec = pl.BlockSpec((tm, tk), lambda i, j, k: (i, k)) if ca == 1 else pl.BlockSpec((tk, tm), lambda i, j, k: (k, i))
    b_spec = pl.BlockSpec((tk, tn), lambda i, j, k: (k, j)) if cb == 0 else pl.BlockSpec((tn, tk), lambda i, j, k: (j, k))
    return pl.pallas_call(
        body, name=name, grid=(m // tm, n // tn, nk),
        in_specs=[a_spec, b_spec], out_specs=pl.BlockSpec((tm, tn), lambda i, j, k: (i, j)),
        out_shape=S((m, n), F32), scratch_shapes=[pltpu.VMEM((tm, tn), F32)],
        compiler_params=pltpu.CompilerParams(dimension_semantics=("parallel", "parallel", "arbitrary")),
    )(a, b)


def _row_block(t, widths):
    tb = 512
    while tb > SUBLANE and (t % tb or tb * sum(widths) * 8 > ROWWISE_VMEM_BUDGET):
        tb //= 2
    return tb


def _tok_spec(tb, w):
    return pl.BlockSpec((tb, w.w), functools.partial(lambda i, idx: (i, idx), idx=w.idx))


def _par_spec(p):
    return pl.BlockSpec((1, p.shape[1]), lambda i: (0, 0))


def rowwise(name, fn, tok, par, out_widths, red_widths=()):
    tok = [_win(t) for t in tok]
    t = tok[0].arr.shape[0]
    tb = _row_block(t, [w.w for w in tok] + list(out_widths))
    n_tok, n_par, n_out = len(tok), len(par), len(out_widths)

    def body(*refs):
        ins = [r[...] for r in refs[:n_tok + n_par]]
        res = fn(*ins)
        out_refs = refs[n_tok + n_par:n_tok + n_par + n_out]
        red_refs = refs[n_tok + n_par + n_out:]
        for r, v in zip(out_refs, res[:n_out]):
            r[...] = v
        if red_refs:
            @pl.when(pl.program_id(0) == 0)
            def _():
                for r in red_refs:
                    r[...] = jnp.zeros_like(r)
            for r, v in zip(red_refs, res[n_out:]):
                r[...] += v

    outs = pl.pallas_call(
        body, name=name, grid=(t // tb,),
        in_specs=[_tok_spec(tb, w) for w in tok] + [_par_spec(p) for p in par],
        out_specs=[pl.BlockSpec((tb, w), lambda i: (i, 0)) for w in out_widths]
        + [pl.BlockSpec((1, w), lambda i: (0, 0)) for w in red_widths],
        out_shape=[S((t, w), F32) for w in out_widths] + [S((1, w), F32) for w in red_widths],
        compiler_params=pltpu.CompilerParams(dimension_semantics=("arbitrary",)),
    )(*[w.arr for w in tok], *par)
    return outs


def rowwise_bwd(name, fn, tok, par, cts, need=None):
    tok = [_win(t) for t in tok]
    cts = [[_win(c) for c in group] for group in cts]
    need = [True] * len(tok) if need is None else list(need)
    t = tok[0].arr.shape[0]
    flat_cts = [c for group in cts for c in group]
    d_widths = [w.w for w, nd in zip(tok, need) if nd]
    tb = _row_block(t, [w.w for w in tok] + [c.w for c in flat_cts] + d_widths)
    n_tok, n_par, n_ct = len(tok), len(par), len(flat_cts)
    n_d = len(d_widths)

    def body(*refs):
        tok_vals = [r[...] for r in refs[:n_tok]]
        par_vals = [r[...] for r in refs[n_tok:n_tok + n_par]]
        ct_refs = refs[n_tok + n_par:n_tok + n_par + n_ct]
        d_refs = refs[n_tok + n_par + n_ct:n_tok + n_par + n_ct + n_d]
        dp_refs = refs[n_tok + n_par + n_ct + n_d:]
        ct_vals, pos = [], 0
        for group in cts:
            acc = None
            for c in group:
                v = ct_refs[pos][...]
                if c.coef != 1.0:
                    v = v * c.coef
                acc = v if acc is None else acc + v
                pos += 1
            ct_vals.append(acc)
        diff_idx = [i for i, nd in enumerate(need) if nd]

        def g(*dargs):
            full = list(tok_vals)
            for i, v in zip(diff_idx, dargs[:n_d]):
                full[i] = v
            return tuple(fn(*full, *dargs[n_d:]))

        _, vjp = jax.vjp(g, *[tok_vals[i] for i in diff_idx], *par_vals)
        grads = vjp(tuple(ct_vals))
        for r, v in zip(d_refs, grads[:n_d]):
            r[...] = v
        if dp_refs:
            @pl.when(pl.program_id(0) == 0)
            def _():
                for r in dp_refs:
                    r[...] = jnp.zeros_like(r)
            for r, v in zip(dp_refs, grads[n_d:]):
                r[...] += v

    outs = pl.pallas_call(
        body, name=name, grid=(t // tb,),
        in_specs=[_tok_spec(tb, w) for w in tok] + [_par_spec(p) for p in par] + [_tok_spec(tb, c) for c in flat_cts],
        out_specs=[pl.BlockSpec((tb, w), lambda i: (i, 0)) for w in d_widths] + [_par_spec(p) for p in par],
        out_shape=[S((t, w), F32) for w in d_widths] + [S(p.shape, F32) for p in par],
        compiler_params=pltpu.CompilerParams(dimension_semantics=("arbitrary",)),
    )(*[w.arr for w in tok], *par, *[c.arr for c in flat_cts])
    return outs[:n_d], outs[n_d:]


def _sigmoid(x):
    return 1.0 / (1.0 + jnp.exp(-x))


def _silu(x):
    return x * _sigmoid(x)


def _ln(x, g, b):
    mu = jnp.mean(x, axis=-1, keepdims=True)
    var = jnp.mean(jnp.square(x - mu), axis=-1, keepdims=True)
    return (x - mu) * lax.rsqrt(var + LN_EPS) * g + b


def f_glu(ua, ug):
    return (ua * _sigmoid(ug),)


def f_ln_silu(a1, g, b):
    return (_silu(_ln(a1, g, b)),)


def f_silu3(a, b, c):
    return (_silu(a), _silu(b), _silu(c))


def f_softplus(dt_raw, bias):
    return (jax.nn.softplus(dt_raw + bias),)


def f_gate_rms(yssd, z, g):
    y = yssd * _silu(z)
    return (y * lax.rsqrt(jnp.mean(jnp.square(y), axis=-1, keepdims=True) + LN_EPS) * g,)


def f_ln(pre, g, b):
    return (_ln(pre, g, b),)


def f_mul(a, b):
    return (a * b,)


def f_act(h1, h2):
    return (_silu(h1) * h2,)


def f_gate_mul(pp, gt):
    return (pp * _sigmoid(gt),)


def conv_fwd(name, x, w, b):
    x = _win(x)
    t, c = x.arr.shape[0], x.w
    kw = w.shape[0]
    cb = LANE
    off = x.idx * (c // cb)
    rows = min(CONV_ROWS, t)
    has_b = b is not None

    def body(*refs):
        if has_b:
            x_ref, w_ref, b_ref, y_ref, xp_ref = refs
        else:
            x_ref, w_ref, y_ref, xp_ref = refs
        xp_ref[0:CONV_PAD, :] = jnp.zeros((CONV_PAD, cb), F32)
        xp_ref[CONV_PAD:CONV_PAD + t, :] = x_ref[...]

        def step(s, carry):
            base = pl.multiple_of(s * rows, rows)
            acc = jnp.zeros((rows, cb), F32)
            if has_b:
                acc = acc + b_ref[...]
            for k in range(kw):
                acc = acc + w_ref[k:k + 1, :] * xp_ref[pl.ds(base + CONV_PAD - (kw - 1) + k, rows), :]
            y_ref[pl.ds(base, rows), :] = acc
            return carry

        lax.fori_loop(0, t // rows, step, 0)

    in_specs = [pl.BlockSpec((t, cb), lambda j: (0, off + j)), pl.BlockSpec((kw, cb), lambda j: (0, j))]
    args = [x.arr, w]
    if has_b:
        in_specs.append(pl.BlockSpec((1, cb), lambda j: (0, j)))
        args.append(b)
    return pl.pallas_call(
        body, name=name, grid=(c // cb,), in_specs=in_specs,
        out_specs=pl.BlockSpec((t, cb), lambda j: (0, j)), out_shape=S((t, c), F32),
        scratch_shapes=[pltpu.VMEM((CONV_PAD + t, cb), F32)],
        compiler_params=pltpu.CompilerParams(dimension_semantics=("parallel",)),
    )(*args)


def conv_bwd(name, x, dy, w):
    x, dy = _win(x), _win(dy)
    t, c = x.arr.shape[0], x.w
    kw = w.shape[0]
    cb = LANE
    xoff = x.idx * (c // cb)
    dyoff = dy.idx * (c // cb)
    rows = min(CONV_ROWS, t)

    def body(x_ref, dy_ref, w_ref, dx_ref, dw_ref, db_ref, xp_ref, dyp_ref):
        xp_ref[0:CONV_PAD, :] = jnp.zeros((CONV_PAD, cb), F32)
        xp_ref[CONV_PAD:CONV_PAD + t, :] = x_ref[...]
        dyp_ref[0:t, :] = dy_ref[...]
        dyp_ref[t:t + CONV_PAD, :] = jnp.zeros((CONV_PAD, cb), F32)

        def fold(v):
            return jnp.sum(v.reshape(rows // SUBLANE, SUBLANE, cb), axis=0)

        def step(s, carry):
            base = pl.multiple_of(s * rows, rows)
            dyc = dy_ref[pl.ds(base, rows), :]
            acc = jnp.zeros((rows, cb), F32)
            new = []
            for k in range(kw):
                acc = acc + w_ref[k:k + 1, :] * dyp_ref[pl.ds(base + (kw - 1) - k, rows), :]
                new.append(carry[k] + fold(dyc * xp_ref[pl.ds(base + CONV_PAD - (kw - 1) + k, rows), :]))
            new.append(carry[kw] + fold(dyc))
            dx_ref[pl.ds(base, rows), :] = acc
            return tuple(new)

        init = tuple(jnp.zeros((SUBLANE, cb), F32) for _ in range(kw + 1))
        parts = lax.fori_loop(0, t // rows, step, init)
        for k in range(kw):
            dw_ref[k:k + 1, :] = jnp.sum(parts[k], axis=0, keepdims=True)
        db_ref[...] = jnp.sum(parts[kw], axis=0, keepdims=True)

    return pl.pallas_call(
        body, name=name, grid=(c // cb,),
        in_specs=[pl.BlockSpec((t, cb), lambda j: (0, xoff + j)), pl.BlockSpec((t, cb), lambda j: (0, dyoff + j)),
                  pl.BlockSpec((kw, cb), lambda j: (0, j))],
        out_specs=[pl.BlockSpec((t, cb), lambda j: (0, j)), pl.BlockSpec((kw, cb), lambda j: (0, j)),
                   pl.BlockSpec((1, cb), lambda j: (0, j))],
        out_shape=[S((t, c), F32), S((kw, c), F32), S((1, c), F32)],
        scratch_shapes=[pltpu.VMEM((CONV_PAD + t, cb), F32), pltpu.VMEM((CONV_PAD + t, cb), F32)],
        compiler_params=pltpu.CompilerParams(dimension_semantics=("parallel",)),
    )(x.arr, dy.arr, w)


def _bdot(a, b, ca, cb):
    return lax.dot_general(a.astype(MXU_DTYPE), b.astype(MXU_DTYPE), (((ca,), (cb,)), ((0,), (0,))),
                           preferred_element_type=F32)


@jax.custom_vjp
def bmm_nn(a, b):
    return _bdot(a, b, 2, 1)


bmm_nn.defvjp(lambda a, b: (_bdot(a, b, 2, 1), (a, b)),
              lambda r, g: (_bdot(g, r[1], 2, 2), _bdot(r[0], g, 1, 1)))


@jax.custom_vjp
def bmm_tn(a, b):
    return _bdot(a, b, 1, 1)


bmm_tn.defvjp(lambda a, b: (_bdot(a, b, 1, 1), (a, b)),
              lambda r, g: (_bdot(r[1], g, 2, 2), _bdot(r[0], g, 2, 1)))


@jax.custom_vjp
def bmm_nt(a, b):
    return _bdot(a, b, 2, 2)


bmm_nt.defvjp(lambda a, b: (_bdot(a, b, 2, 2), (a, b)),
              lambda r, g: (_bdot(g, r[1], 2, 1), _bdot(g, r[0], 1, 1)))


def ssd_chunk(x, dt, bm, cm, hprev, a_log, dsk):
    hg, ln, _ = x.shape
    n = bm.shape[1]
    ii = lax.broadcasted_iota(jnp.int32, (ln, ln), 0)
    jj = lax.broadcasted_iota(jnp.int32, (ln, ln), 1)
    tril, eye, triu = (ii >= jj)[None], (ii == jj)[None], (ii <= jj)[None]
    da = dt * (-jnp.exp(a_log))
    da_row = jnp.sum(jnp.where(eye, da, 0.0), axis=1, keepdims=True)
    dt_row = jnp.sum(jnp.where(eye, dt, 0.0), axis=1, keepdims=True)
    cum_c = jnp.sum(jnp.where(tril, da_row, 0.0), axis=2, keepdims=True)
    cum_r = jnp.sum(jnp.where(triu, da, 0.0), axis=1, keepdims=True)
    last = jnp.sum(da, axis=1, keepdims=True)
    decay = jnp.where(tril, jnp.exp(jnp.where(tril, cum_c - cum_r, 0.0)), 0.0)
    cb = bmm_nt(cm[None], bm[None])
    y_diag = bmm_nn(cb * decay * dt_row, x)
    bb = jnp.broadcast_to(bm[None], (hg, ln, n))
    cc = jnp.broadcast_to(cm[None], (hg, ln, n))
    states = bmm_tn(x * (jnp.exp(last - cum_c) * dt), bb)
    y_off = bmm_nt(cc, hprev) * jnp.exp(cum_c)
    hnew = hprev * jnp.exp(last) + states
    return y_diag + y_off + dsk * x, hnew


def _ssd_dims(xh, bm):
    h, t, p = xh.shape
    hg = h // N_GROUPS
    n = bm.shape[1] // N_GROUPS
    return h, t, p, hg, n, t // CHUNK


def ssd_fwd(name, xh, dth, bm, cm, a_log, dsk):
    h, t, p, hg, n, nc = _ssd_dims(xh, bm)

    def body(al_ref, dk_ref, x_ref, dt_ref, b_ref, c_ref, y_ref, hp_ref, h_scr):
        @pl.when(pl.program_id(1) == 0)
        def _():
            h_scr[...] = jnp.zeros_like(h_scr)

        hprev = h_scr[...]
        hp_ref[:, 0] = hprev
        y, hnew = ssd_chunk(x_ref[...], dt_ref[...], b_ref[...], c_ref[...], hprev, al_ref[...], dk_ref[...])
        y_ref[...] = y
        h_scr[...] = hnew

    head = pl.BlockSpec((hg, 1, 1), lambda g, c: (g, 0, 0))
    return pl.pallas_call(
        body, name=name, grid=(N_GROUPS, nc),
        in_specs=[head, head, pl.BlockSpec((hg, CHUNK, p), lambda g, c: (g, c, 0)),
                  pl.BlockSpec((hg, CHUNK, 1), lambda g, c: (g, c, 0)),
                  pl.BlockSpec((CHUNK, n), lambda g, c: (c, g)), pl.BlockSpec((CHUNK, n), lambda g, c: (c, g))],
        out_specs=[pl.BlockSpec((hg, CHUNK, p), lambda g, c: (g, c, 0)),
                   pl.BlockSpec((hg, 1, p, n), lambda g, c: (g, c, 0, 0))],
        out_shape=[S((h, t, p), F32), S((h, nc, p, n), F32)],
        scratch_shapes=[pltpu.VMEM((hg, p, n), F32)],
        compiler_params=pltpu.CompilerParams(dimension_semantics=("parallel", "arbitrary")),
    )(a_log, dsk, xh, dth, bm, cm)


def ssd_bwd(name, xh, dth, bm, cm, a_log, dsk, hp, dyh):
    h, t, p, hg, n, nc = _ssd_dims(xh, bm)

    def body(al_ref, dk_ref, x_ref, dt_ref, b_ref, c_ref, hp_ref, dy_ref,
             dx_ref, ddt_ref, db_ref, dc_ref, dal_ref, ddk_ref, dh_scr):
        @pl.when(pl.program_id(1) == 0)
        def _():
            dh_scr[...] = jnp.zeros_like(dh_scr)
            dal_ref[...] = jnp.zeros_like(dal_ref)
            ddk_ref[...] = jnp.zeros_like(ddk_ref)

        _, vjp = jax.vjp(ssd_chunk, x_ref[...], dt_ref[...], b_ref[...], c_ref[...], hp_ref[:, 0],
                         al_ref[...], dk_ref[...])
        gx, gdt, gb, gc, ghp, gal, gdk = vjp((dy_ref[...], dh_scr[...]))
        dx_ref[...] = gx
        ddt_ref[...] = gdt
        db_ref[...] = gb
        dc_ref[...] = gc
        dh_scr[...] = ghp
        dal_ref[...] += gal
        ddk_ref[...] += gdk

    head = pl.BlockSpec((hg, 1, 1), lambda g, c: (g, 0, 0))
    xs = pl.BlockSpec((hg, CHUNK, p), lambda g, c: (g, nc - 1 - c, 0))
    ds = pl.BlockSpec((hg, CHUNK, 1), lambda g, c: (g, nc - 1 - c, 0))
    bs = pl.BlockSpec((CHUNK, n), lambda g, c: (nc - 1 - c, g))
    return pl.pallas_call(
        body, name=name, grid=(N_GROUPS, nc),
        in_specs=[head, head, xs, ds, bs, bs, pl.BlockSpec((hg, 1, p, n), lambda g, c: (g, nc - 1 - c, 0, 0)), xs],
        out_specs=[xs, ds, bs, bs, head, head],
        out_shape=[S((h, t, p), F32), S((h, t, 1), F32), S(bm.shape, F32), S(cm.shape, F32),
                   S((h, 1, 1), F32), S((h, 1, 1), F32)],
        scratch_shapes=[pltpu.VMEM((hg, p, n), F32)],
        compiler_params=pltpu.CompilerParams(dimension_semantics=("parallel", "arbitrary")),
    )(a_log, dsk, xh, dth, bm, cm, hp, dyh)


def to_heads(v):
    t = v.shape[0]
    return v.reshape(t, -1, HEAD_DIM).transpose(1, 0, 2)


def from_heads(v):
    return v.transpose(1, 0, 2).reshape(v.shape[1], -1)


def _alpha(depth):
    return (2.0 * depth) ** 0.25


def _pad_lanes(v):
    return jnp.pad(v, ((0, 0), (0, LANE - v.shape[1])))


def split_even_weights(w, j):
    d = w["e_w_in"].shape[2]
    da = w["e_conv_a_w"].shape[2]
    db = w["e_norm_b_g"].shape[1]
    gn = N_GROUPS * N_STATE
    nh = w["e_dt_bias"].shape[1]
    main = 2 * da + 2 * db + 2 * gn
    win = w["e_w_in"][j]
    ox = 2 * da + db
    cw, cbias = w["e_conv_b_w"][j], w["e_conv_b_b"][j][None]
    return dict(
        d=d, da=da, db=db, gn=gn, nh=nh, main=main,
        win_main=win[:main], win_dt=jnp.pad(win[main:], ((0, LANE - nh), (0, 0))),
        caw=w["e_conv_a_w"][j], cab=w["e_conv_a_b"][j][None], lag=w["e_ln_a_g"][j][None], lab=w["e_ln_a_b"][j][None],
        cw_xs=cw[:, :db], cw_b=cw[:, db:db + gn], cw_c=cw[:, db + gn:],
        cb_xs=cbias[:, :db], cb_b=cbias[:, db:db + gn], cb_c=cbias[:, db + gn:],
        dt_bias=_pad_lanes(w["e_dt_bias"][j][None]), a_log=w["e_a_log"][j].reshape(nh, 1, 1),
        dsk=w["e_d_skip"][j].reshape(nh, 1, 1), norm_g=w["e_norm_b_g"][j][None],
        wout_a=w["e_w_out"][j][:da], wout_b=w["e_w_out"][j][da:],
    )


def even_fwd(tag, x, lw, ln_g, ln_b, alpha):
    t = x.shape[0]
    da, db, gn, nh = lw["da"], lw["db"], lw["gn"], lw["nh"]
    u = mm(tag + "_win", x, lw["win_main"], "nt")
    udt = mm(tag + "_windt", x, lw["win_dt"], "nt")
    ua, ug, z, xs_pre = Win(u, da, 0), Win(u, da, 1), Win(u, db, 2 * da // db), Win(u, db, (2 * da + db) // db)
    b_pre, c_pre = Win(u, gn, (2 * da + 2 * db) // gn), Win(u, gn, (2 * da + 2 * db + gn) // gn)
    (a0,) = rowwise(tag + "_glu", f_glu, [ua, ug], [], [da])
    a1 = conv_fwd(tag + "_conva", a0, lw["caw"], lw["cab"])
    (ya,) = rowwise(tag + "_lna", f_ln_silu, [a1], [lw["lag"], lw["lab"]], [da])
    xs_c = conv_fwd(tag + "_convxs", xs_pre, lw["cw_xs"], lw["cb_xs"])
    b_c = conv_fwd(tag + "_convb", b_pre, lw["cw_b"], lw["cb_b"])
    c_c = conv_fwd(tag + "_convc", c_pre, lw["cw_c"], lw["cb_c"])
    xs, bm, cm = rowwise(tag + "_silu3", f_silu3, [xs_c, b_c, c_c], [], [db, gn, gn])
    (dt,) = rowwise(tag + "_dt", f_softplus, [udt], [lw["dt_bias"]], [LANE])
    xs_h = to_heads(xs)
    dt_h = dt[:, :nh].T.reshape(nh, t, 1)
    y_h, hp = ssd_fwd(tag + "_ssd", xs_h, dt_h, bm, cm, lw["a_log"], lw["dsk"])
    yssd = from_heads(y_h)
    (yb,) = rowwise(tag + "_gate", f_gate_rms, [yssd, z], [lw["norm_g"]], [db])
    ma = mm(tag + "_wouta", ya, lw["wout_a"], "nn")
    mb = mm(tag + "_woutb", yb, lw["wout_b"], "nn")

    def f_res(xv, mav, mbv, g, b):
        pre = alpha * xv + mav + mbv
        return _ln(pre, g, b), pre

    x1, pre = rowwise(tag + "_res", f_res, [x, ma, mb], [ln_g, ln_b], [x.shape[1]] * 2)
    saved = dict(x=x, u=u, udt=udt, a0=a0, a1=a1, ya=ya, xs_c=xs_c, b_c=b_c, c_c=c_c, xs_h=xs_h, dt_h=dt_h, bm=bm,
                 cm=cm, hp=hp, yssd=yssd, yb=yb, pre=pre)
    return x1, saved


def even_bwd(tag, dx1_pieces, sv, lw, ln_g, ln_b, alpha):
    t = sv["x"].shape[0]
    da, db, gn, nh = lw["da"], lw["db"], lw["gn"], lw["nh"]
    u, x = sv["u"], sv["x"]
    ua, ug, z, xs_pre = Win(u, da, 0), Win(u, da, 1), Win(u, db, 2 * da // db), Win(u, db, (2 * da + db) // db)
    b_pre, c_pre = Win(u, gn, (2 * da + 2 * db) // gn), Win(u, gn, (2 * da + 2 * db + gn) // gn)
    (dpre,), (dg0, db0) = rowwise_bwd(tag + "_res_b", f_ln, [sv["pre"]], [ln_g, ln_b], [dx1_pieces])
    dya = mm(tag + "_dya", dpre, lw["wout_a"], "nt")
    dyb = mm(tag + "_dyb", dpre, lw["wout_b"], "nt")
    dwout_a = mm(tag + "_dwouta", sv["ya"], dpre, "tn")
    dwout_b = mm(tag + "_dwoutb", sv["yb"], dpre, "tn")
    (dyssd, dz), (dnorm_g,) = rowwise_bwd(tag + "_gate_b", f_gate_rms, [sv["yssd"], z], [lw["norm_g"]], [[dyb]])
    dxs_h, ddt_h, dbm, dcm, dalog, ddsk = ssd_bwd(tag + "_ssd_b", sv["xs_h"], sv["dt_h"], sv["bm"], sv["cm"],
                                                  lw["a_log"], lw["dsk"], sv["hp"], to_heads(dyssd))
    dxs = from_heads(dxs_h)
    ddt = _pad_lanes(ddt_h.reshape(nh, t).T)
    (dudt,), (ddt_bias,) = rowwise_bwd(tag + "_dt_b", f_softplus, [sv["udt"]], [lw["dt_bias"]], [[ddt]])
    (dxs_c, db_c, dc_c), _ = rowwise_bwd(tag + "_silu3_b", f_silu3, [sv["xs_c"], sv["b_c"], sv["c_c"]], [],
                                         [[dxs], [dbm], [dcm]])
    dxs_pre, dcw_xs, dcb_xs = conv_bwd(tag + "_convxs_b", xs_pre, dxs_c, lw["cw_xs"])
    db_pre, dcw_b, dcb_b = conv_bwd(tag + "_convb_b", b_pre, db_c, lw["cw_b"])
    dc_pre, dcw_c, dcb_c = conv_bwd(tag + "_convc_b", c_pre, dc_c, lw["cw_c"])
    (da1,), (dlag, dlab) = rowwise_bwd(tag + "_lna_b", f_ln_silu, [sv["a1"]], [lw["lag"], lw["lab"]], [[dya]])
    da0, dcaw, dcab = conv_bwd(tag + "_conva_b", sv["a0"], da1, lw["caw"])
    (dua, dug), _ = rowwise_bwd(tag + "_glu_b", f_glu, [ua, ug], [], [[da0]])
    du = jnp.concatenate([dua, dug, dz, dxs_pre, db_pre, dc_pre], axis=1)
    dx_m = mm(tag + "_dxm", du, lw["win_main"], "nn")
    dx_dt = mm(tag + "_dxdt", dudt, lw["win_dt"], "nn")
    dwin_main = mm(tag + "_dwin", du, x, "tn")
    dwin_dt = mm(tag + "_dwindt", dudt, x, "tn")
    grads = dict(
        e_w_in=jnp.concatenate([dwin_main, dwin_dt[:nh]], axis=0),
        e_conv_a_w=dcaw, e_conv_a_b=dcab[0], e_ln_a_g=dlag[0], e_ln_a_b=dlab[0],
        e_conv_b_w=jnp.concatenate([dcw_xs, dcw_b, dcw_c], axis=1),
        e_conv_b_b=jnp.concatenate([dcb_xs, dcb_b, dcb_c], axis=1)[0],
        e_dt_bias=ddt_bias[0, :nh], e_a_log=dalog.reshape(nh), e_d_skip=ddsk.reshape(nh), e_norm_b_g=dnorm_g[0],
        e_w_out=jnp.concatenate([dwout_a, dwout_b], axis=0), ln_g0=dg0[0], ln_b0=db0[0],
    )
    return [Win(dpre, coef=alpha), dx_m, dx_dt], grads


def odd_fwd(tag, x, w, j, ln_g, ln_b, alpha):
    d = x.shape[1]
    u = mm(tag + "_win", x, w["o_w_in"][j], "nt")
    bg, cg, v = Win(u, d, 0), Win(u, d, 1), Win(u, d, 2)
    (s,) = rowwise(tag + "_cv", f_mul, [cg, v], [], [d])
    cs = conv_fwd(tag + "_conv", s, w["o_conv_w"][j], None)
    (m,) = rowwise(tag + "_bm", f_mul, [bg, cs], [], [d])
    mix = mm(tag + "_wout", m, w["o_w_out"][j], "nn")

    def f_res(xv, mv, g, b):
        pre = alpha * xv + mv
        return _ln(pre, g, b), pre

    x1, pre = rowwise(tag + "_res", f_res, [x, mix], [ln_g, ln_b], [d] * 2)
    return x1, dict(x=x, u=u, s=s, cs=cs, m=m, pre=pre)


def odd_bwd(tag, dx1_pieces, sv, w, j, ln_g, ln_b, alpha):
    x, u = sv["x"], sv["u"]
    d = x.shape[1]
    bg, cg, v = Win(u, d, 0), Win(u, d, 1), Win(u, d, 2)
    (dpre,), (dg0, db0) = rowwise_bwd(tag + "_res_b", f_ln, [sv["pre"]], [ln_g, ln_b], [dx1_pieces])
    dm = mm(tag + "_dm", dpre, w["o_w_out"][j], "nt")
    dwout = mm(tag + "_dwout", sv["m"], dpre, "tn")
    (dbg, dcs), _ = rowwise_bwd(tag + "_bm_b", f_mul, [bg, sv["cs"]], [], [[dm]])
    ds, dcw, _ = conv_bwd(tag + "_conv_b", sv["s"], dcs, w["o_conv_w"][j])
    (dcg, dv), _ = rowwise_bwd(tag + "_cv_b", f_mul, [cg, v], [], [[ds]])
    du = jnp.concatenate([dbg, dcg, dv], axis=1)
    dx_u = mm(tag + "_dx", du, w["o_w_in"][j], "nn")
    dwin = mm(tag + "_dwin", du, x, "tn")
    grads = dict(o_w_in=dwin, o_conv_w=dcw, o_w_out=dwout, ln_g0=dg0[0], ln_b0=db0[0])
    return [Win(dpre, coef=alpha), dx_u], grads


def ffn_fwd(tag, x1, p_i, w, i, ln_g, ln_b, alpha):
    d = x1.shape[1]
    ff = w["f_w_down"].shape[1]
    hpre = mm(tag + "_wup", x1, w["f_w_up"][i], "nt")
    h = conv_fwd(tag + "_fconv", hpre, w["f_conv_w"][i], w["f_conv_b"][i][None])
    (act,) = rowwise(tag + "_act", f_act, [Win(h, ff, 0), Win(h, ff, 1)], [], [ff])
    ffn = mm(tag + "_wdown", act, w["f_w_down"][i], "nn")
    pp = mm(tag + "_pproj", p_i, w["ple_w_proj"][i], "nt")
    gt = mm(tag + "_pgate", x1, w["ple_w_gate"][i], "nn")

    def f_res2(xv, fv, ppv, gtv, g, b):
        pre = alpha * xv + fv + ppv * _sigmoid(gtv)
        return _ln(pre, g, b), pre

    x2, pre = rowwise(tag + "_res2", f_res2, [x1, ffn, pp, gt], [ln_g, ln_b], [d] * 2)
    return x2, dict(x1=x1, hpre=hpre, h=h, act=act, pp=pp, gt=gt, pre=pre)


def ffn_bwd(tag, dx2_pieces, sv, p_i, w, i, ln_g, ln_b, alpha):
    x1 = sv["x1"]
    ff = w["f_w_down"].shape[1]
    (dpre,), (dg1, db1) = rowwise_bwd(tag + "_res2_b", f_ln, [sv["pre"]], [ln_g, ln_b], [dx2_pieces])
    (dpp, dgt), _ = rowwise_bwd(tag + "_pg_b", f_gate_mul, [sv["pp"], sv["gt"]], [], [[dpre]])
    dwproj = mm(tag + "_dwproj", dpp, p_i, "tn")
    dwgate = mm(tag + "_dwgate", x1, dgt, "tn")
    dx1_a = mm(tag + "_dx1a", dgt, w["ple_w_gate"][i], "nt")
    dact = mm(tag + "_dact", dpre, w["f_w_down"][i], "nt")
    dwdown = mm(tag + "_dwdown", sv["act"], dpre, "tn")
    h = sv["h"]
    (dh1, dh2), _ = rowwise_bwd(tag + "_act_b", f_act, [Win(h, ff, 0), Win(h, ff, 1)], [], [[dact]])
    dh = jnp.concatenate([dh1, dh2], axis=1)
    dhpre, dfcw, dfcb = conv_bwd(tag + "_fconv_b", sv["hpre"], dh, w["f_conv_w"][i])
    dwup = mm(tag + "_dwup", dhpre, x1, "tn")
    dx1_b = mm(tag + "_dx1b", dhpre, w["f_w_up"][i], "nn")
    grads = dict(f_w_up=dwup, f_conv_w=dfcw, f_conv_b=dfcb[0], f_w_down=dwdown, ple_w_proj=dwproj, ple_w_gate=dwgate,
                 ln_g1=dg1[0], ln_b1=db1[0])
    return [Win(dpre, coef=alpha), dx1_a, dx1_b], grads


def local_step(x, p, w, target):
    depth = w["ln_g"].shape[0]
    alpha = _alpha(depth)
    d = x.shape[1]
    saved = []
    h = x
    for i in range(depth):
        j = i // 2
        g0, b0, g1, b1 = w["ln_g"][i, 0][None], w["ln_b"][i, 0][None], w["ln_g"][i, 1][None], w["ln_b"][i, 1][None]
        tag = "l%d" % i
        if i % 2 == 0:
            lw = split_even_weights(w, j)
            h, sv_m = even_fwd(tag, h, lw, g0, b0, alpha)
        else:
            lw = None
            h, sv_m = odd_fwd(tag, h, w, j, g0, b0, alpha)
        h, sv_f = ffn_fwd(tag, h, p[i], w, i, g1, b1, alpha)
        saved.append((lw, sv_m, sv_f, (g0, b0, g1, b1)))

    def f_loss(xf, tg):
        diff = xf - tg
        sq = jnp.sum(jnp.sum(jnp.square(diff), axis=1, keepdims=True), axis=0, keepdims=True)
        return diff * (1.0 / d), jnp.broadcast_to(sq, (1, LANE))

    dxf, sq = rowwise("loss", f_loss, [h, target], [], [d], red_widths=[LANE])
    loss = sq[0, 0] * (0.5 / d)

    per_layer = []
    pieces = [dxf]
    for i in reversed(range(depth)):
        j = i // 2
        lw, sv_m, sv_f, (g0, b0, g1, b1) = saved[i]
        tag = "l%d" % i
        pieces, gf = ffn_bwd(tag, pieces, sv_f, p[i], w, i, g1, b1, alpha)
        if i % 2 == 0:
            pieces, gm = even_bwd(tag, pieces, sv_m, lw, g0, b0, alpha)
        else:
            pieces, gm = odd_bwd(tag, pieces, sv_m, w, j, g0, b0, alpha)
        per_layer.append((i, gm, gf))

    def f_sum(*vs):
        acc = None
        for v, c in zip(vs, [pc.coef for pc in map(_win, pieces)]):
            v = v if c == 1.0 else v * c
            acc = v if acc is None else acc + v
        return (acc,)

    (grad_x,) = rowwise("grad_x", f_sum, [Win(_win(pc).arr) for pc in pieces], [], [d])

    by_layer = {i: (gm, gf) for i, gm, gf in per_layer}
    grads = {}
    n_even, n_odd = (depth + 1) // 2, depth // 2
    for name in ("e_w_in", "e_conv_a_w", "e_conv_a_b", "e_ln_a_g", "e_ln_a_b", "e_conv_b_w", "e_conv_b_b", "e_dt_bias",
                 "e_a_log", "e_d_skip", "e_norm_b_g", "e_w_out"):
        grads[name] = jnp.stack([by_layer[2 * j][0][name] for j in range(n_even)])
    for name in ("o_w_in", "o_conv_w", "o_w_out"):
        grads[name] = jnp.stack([by_layer[2 * j + 1][0][name] for j in range(n_odd)])
    for name in ("f_w_up", "f_conv_w", "f_conv_b", "f_w_down", "ple_w_proj", "ple_w_gate"):
        grads[name] = jnp.stack([by_layer[i][1][name] for i in range(depth)])
    grads["ln_g"] = jnp.stack([jnp.stack([by_layer[i][0]["ln_g0"], by_layer[i][1]["ln_g1"]]) for i in range(depth)])
    grads["ln_b"] = jnp.stack([jnp.stack([by_layer[i][0]["ln_b0"], by_layer[i][1]["ln_b1"]]) for i in range(depth)])
    return loss, grad_x, grads


_ANY = pl.BlockSpec(memory_space=pl.ANY)
_MESH = pl.DeviceIdType.MESH


def all_gather(name, xl):
    r, c_ = xl.shape

    def body(x_ref, out_ref, send_sems, recv_sems, local_sem):
        x, y, c = lax.axis_index("x"), lax.axis_index("y"), lax.axis_index("c")
        me, sibling = (x, y, c), (x, y, 1 - c)
        chips = [(1 - x, y), (x, 1 - y), (1 - x, 1 - y)]

        def slot(px, py, pc):
            return out_ref.at[4 * px + 2 * py + pc]

        def copy(k, block, to, src=None):
            return pltpu.make_async_remote_copy(
                src_ref=slot(*block) if src is None else src, dst_ref=slot(*block),
                send_sem=send_sems.at[k], recv_sem=recv_sems.at[k], device_id=to, device_id_type=_MESH)

        mine = pltpu.make_async_copy(x_ref, slot(*me), local_sem)
        mine.start()
        first = [copy(0, me, sibling, src=x_ref)]
        first += [copy(1 + j, me, (*chip, c), src=x_ref) for j, chip in enumerate(chips)]
        for cp in first:
            cp.start()
        passed = [copy(4 + j, (*chip, c), sibling) for j, chip in enumerate(chips)]
        for j, chip in enumerate(chips):
            copy(1 + j, (*chip, c), me).wait_recv()
            passed[j].start()
        copy(0, sibling, me).wait_recv()
        for j, chip in enumerate(chips):
            copy(4 + j, (*chip, 1 - c), me).wait_recv()
        for cp in first + passed:
            cp.wait_send()
        mine.wait()

    return pl.pallas_call(
        body, name=name, out_shape=S((N_DEV, r, c_), xl.dtype), in_specs=[_ANY], out_specs=_ANY,
        scratch_shapes=[pltpu.SemaphoreType.DMA((7,)), pltpu.SemaphoreType.DMA((7,)), pltpu.SemaphoreType.DMA],
    )(xl)


def exchange_sibling(name, g4):
    _, _, r, c_ = g4.shape

    def body(g_ref, out_ref, send_sems, recv_sems):
        x, y, c = lax.axis_index("x"), lax.axis_index("y"), lax.axis_index("c")
        copies = [pltpu.make_async_remote_copy(
            src_ref=g_ref.at[q, 1 - c], dst_ref=out_ref.at[q], send_sem=send_sems.at[q], recv_sem=recv_sems.at[q],
            device_id=(x, y, 1 - c), device_id_type=_MESH) for q in range(4)]
        for cp in copies:
            cp.start()
        for cp in copies:
            cp.wait()

    return pl.pallas_call(
        body, name=name, out_shape=S((4, r, c_), g4.dtype), in_specs=[_ANY], out_specs=_ANY,
        scratch_shapes=[pltpu.SemaphoreType.DMA((4,)), pltpu.SemaphoreType.DMA((4,))],
    )(g4)


def exchange_chips(name, p4):
    _, r, c_ = p4.shape

    def body(p_ref, out_ref, send_sems, recv_sems):
        x, y, c = lax.axis_index("x"), lax.axis_index("y"), lax.axis_index("c")
        chips = [(1 - x, y), (x, 1 - y), (1 - x, 1 - y)]
        copies = [pltpu.make_async_remote_copy(
            src_ref=p_ref.at[2 * cx + cy], dst_ref=out_ref.at[k], send_sem=send_sems.at[k], recv_sem=recv_sems.at[k],
            device_id=(cx, cy, c), device_id_type=_MESH) for k, (cx, cy) in enumerate(chips)]
        for cp in copies:
            cp.start()
        for cp in copies:
            cp.wait()

    return pl.pallas_call(
        body, name=name, out_shape=S((3, r, c_), p4.dtype), in_specs=[_ANY], out_specs=_ANY,
        scratch_shapes=[pltpu.SemaphoreType.DMA((3,)), pltpu.SemaphoreType.DMA((3,))],
    )(p4)


def sum_with_sibling(name, g4, recv, core):
    _, _, r, c_ = g4.shape
    tr = _pick(r, (256, 128, 64, 32, 16, 8))

    def body(core_ref, g_ref, r_ref, o_ref):
        o_ref[...] = g_ref[0] + r_ref[...]

    return pl.pallas_call(
        body, name=name, out_shape=S((4, r, c_), F32),
        grid_spec=pltpu.PrefetchScalarGridSpec(
            num_scalar_prefetch=1, grid=(4, r // tr),
            in_specs=[pl.BlockSpec((1, 1, tr, c_), lambda q, i, cr: (q, cr[0], i, 0)),
                      pl.BlockSpec((1, tr, c_), lambda q, i, cr: (q, i, 0))],
            out_specs=pl.BlockSpec((1, tr, c_), lambda q, i, cr: (q, i, 0))),
    )(core, g4, recv)


def sum_with_chips(name, p4, recv, chip):
    _, r, c_ = p4.shape
    tr = _pick(r, (256, 128, 64, 32, 16, 8))

    def body(chip_ref, p_ref, r_ref, o_ref):
        o_ref[...] = ((p_ref[0] + r_ref[0]) + r_ref[1]) + r_ref[2]

    return pl.pallas_call(
        body, name=name, out_shape=S((r, c_), F32),
        grid_spec=pltpu.PrefetchScalarGridSpec(
            num_scalar_prefetch=1, grid=(r // tr,),
            in_specs=[pl.BlockSpec((1, tr, c_), lambda i, ch: (ch[0], i, 0)),
                      pl.BlockSpec((3, tr, c_), lambda i, ch: (0, i, 0))],
            out_specs=pl.BlockSpec((tr, c_), lambda i, ch: (i, 0))),
    )(chip, p4, recv)


def sum_devices(name, g8):
    _, r, c_ = g8.shape

    def body(g_ref, o_ref):
        acc = g_ref[0]
        for k in range(1, N_DEV):
            acc = acc + g_ref[k]
        o_ref[...] = acc

    return pl.pallas_call(body, name=name, out_shape=S((r, c_), F32))(g8)


def _flatten(parts, cols, row_mult):
    flat = jnp.concatenate([v.reshape(-1) for v in parts])
    n = flat.shape[0]
    rows = -(-n // (cols * row_mult)) * row_mult
    return jnp.pad(flat, (0, rows * cols - n)).reshape(rows, cols)


def _exchange_dims(name, lshape):
    l, r, c = lshape
    return (l, c, r) if name in TRANSPOSED else (l, r, c)


def gather_big(name, local):
    cols = local["e_w_out"].shape[2]
    parts = []
    for n in EXCHANGE_ORDER:
        v = local[n].astype(MXU_DTYPE)
        parts.append((v.transpose(0, 2, 1) if n in TRANSPOSED else v).reshape(-1, cols))
    got = all_gather(name, jnp.concatenate(parts, axis=0))
    full, r0 = {}, 0
    for n, part in zip(EXCHANGE_ORDER, parts):
        l, a, b = _exchange_dims(n, local[n].shape)
        seg = got[:, r0:r0 + part.shape[0]].reshape(N_DEV, l, a, b)
        full[n] = seg.transpose(1, 0, 2, 3).reshape(l, N_DEV * a, b)
        r0 += part.shape[0]
    return full


def reduce_scatter_big(grads, local_shapes):
    cols = local_shapes["e_w_out"][2]
    parts = []
    for n in EXCHANGE_ORDER:
        l, a, b = _exchange_dims(n, local_shapes[n])
        parts.append(grads[n].reshape(l, N_DEV, a, b).transpose(1, 0, 2, 3).reshape(N_DEV, -1, cols))
    n_rows = sum(pt.shape[1] for pt in parts)
    rows = -(-n_rows // 256) * 256
    if rows > n_rows:
        parts.append(jnp.zeros((N_DEV, rows - n_rows, cols), F32))
    g4 = jnp.concatenate(parts, axis=1).reshape(4, 2, rows, cols)
    x, y, c = lax.axis_index("x"), lax.axis_index("y"), lax.axis_index("c")
    from_sibling = exchange_sibling("rs_sibling", g4)
    p4 = sum_with_sibling("rs_sum_sibling", g4, from_sibling, c.astype(jnp.int32).reshape(1))
    from_chips = exchange_chips("rs_chips", p4)
    mine = sum_with_chips("rs_sum_chips", p4, from_chips, (2 * x + y).astype(jnp.int32).reshape(1))
    out, r0 = {}, 0
    for n, part in zip(EXCHANGE_ORDER, parts):
        l, a, b = _exchange_dims(n, local_shapes[n])
        seg = mine[r0:r0 + part.shape[1]].reshape(l, a, b)
        out[n] = seg.transpose(0, 2, 1) if n in TRANSPOSED else seg
        r0 += part.shape[1]
    return out


def gather_small(name, local, names):
    flat = _flatten([local[n] for n in names], LANE, 1)
    got = all_gather(name, flat).reshape(N_DEV, -1)
    full, off = {}, 0
    for n in names:
        size = math.prod(local[n].shape)
        seg = got[:, off:off + size].reshape((N_DEV,) + local[n].shape)
        full[n] = seg.transpose(1, 2, 0, 3).reshape(seg.shape[1], seg.shape[2], -1)
        off += size
    return full


def all_reduce_small(grads, names):
    flat = _flatten([grads[n] for n in names], LANE, SUBLANE)
    total = sum_devices("ar_sum", all_gather("ar_gather", flat)).reshape(-1)
    out, off = {}, 0
    for nm in names:
        size = math.prod(grads[nm].shape)
        out[nm] = total[off:off + size].reshape(grads[nm].shape)
        off += size
    return out


def adamw(name, w, g, m, v):
    shape = w.shape
    cols = shape[-1]
    rows = math.prod(shape[:-1])
    tr = _pick(rows, (256, 128, 64, 32, 16, 8)) if rows * cols > 256 * 1024 else rows
    c1 = 1.0 - ADAM_B1 ** ADAM_STEP
    c2 = 1.0 - ADAM_B2 ** ADAM_STEP

    def body(w_ref, g_ref, m_ref, v_ref, d_ref, nm_ref, nv_ref):
        gv = g_ref[...]
        m2 = ADAM_B1 * m_ref[...] + (1.0 - ADAM_B1) * gv
        v2 = ADAM_B2 * v_ref[...] + (1.0 - ADAM_B2) * jnp.square(gv)
        d_ref[...] = -ADAM_LR * ((m2 / c1) / (jnp.sqrt(v2 / c2) + ADAM_EPS) + ADAM_WD * w_ref[...])
        nm_ref[...] = m2
        nv_ref[...] = v2

    spec = pl.BlockSpec((tr, cols), lambda i: (i, 0))
    outs = pl.pallas_call(
        body, name=name, grid=(rows // tr,), in_specs=[spec] * 4, out_specs=[spec] * 3,
        out_shape=[S((rows, cols), F32)] * 3,
        compiler_params=pltpu.CompilerParams(dimension_semantics=("parallel",)),
    )(*[a.reshape(rows, cols) for a in (w, g, m, v)])
    return tuple(o.reshape(shape) for o in outs)


def kernel(x, p, e_w_in, e_conv_a_w, e_conv_a_b, e_ln_a_g, e_ln_a_b, e_conv_b_w, e_conv_b_b, e_dt_bias, e_a_log, e_d_skip, e_norm_b_g, e_w_out, o_w_in, o_conv_w, o_w_out, f_w_up, f_conv_w, f_conv_b, f_w_down, ple_w_proj, ple_w_gate, ln_g, ln_b, loss_target, m_e_w_in, m_e_conv_a_w, m_e_conv_a_b, m_e_ln_a_g, m_e_ln_a_b, m_e_conv_b_w, m_e_conv_b_b, m_e_dt_bias, m_e_a_log, m_e_d_skip, m_e_norm_b_g, m_e_w_out, m_o_w_in, m_o_conv_w, m_o_w_out, m_f_w_up, m_f_conv_w, m_f_conv_b, m_f_w_down, m_ple_w_proj, m_ple_w_gate, m_ln_g, m_ln_b, v_e_w_in, v_e_conv_a_w, v_e_conv_a_b, v_e_ln_a_g, v_e_ln_a_b, v_e_conv_b_w, v_e_conv_b_b, v_e_dt_bias, v_e_a_log, v_e_d_skip, v_e_norm_b_g, v_e_w_out, v_o_w_in, v_o_conv_w, v_o_w_out, v_f_w_up, v_f_conv_w, v_f_conv_b, v_f_w_down, v_ple_w_proj, v_ple_w_gate, v_ln_g, v_ln_b):
    args = locals()
    local = {n: args[n] for n in WEIGHTS}
    mom = {n: args["m_" + n] for n in WEIGHTS}
    var = {n: args["v_" + n] for n in WEIGHTS}

    full = {n: local[n] for n in REPLICATED}
    full.update(gather_big("ag_big", local))
    full.update(gather_small("ag_small", local, SMALL_SHARDED))

    loss_local, grad_x, grads = local_step(x[0], p[:, 0], full, loss_target[0])
    loss = lax.psum(loss_local, MESH_AXES)

    g_local = reduce_scatter_big(grads, {n: local[n].shape for n in BIG})
    small = all_reduce_small(grads, REPLICATED + SMALL_SHARDED)
    dev = 4 * lax.axis_index("x") + 2 * lax.axis_index("y") + lax.axis_index("c")
    for n in REPLICATED:
        g_local[n] = small[n]
    for n in SMALL_SHARDED:
        width = local[n].shape[2]
        g_local[n] = lax.dynamic_slice_in_dim(small[n], dev * width, width, axis=2)

    delta, new_m, new_v = {}, {}, {}
    for n in WEIGHTS:
        delta[n], new_m[n], new_v[n] = adamw("adamw_" + n, local[n], g_local[n], mom[n], var[n])
    return (loss, grad_x[None], *[g_local[n] for n in WEIGHTS], *[delta[n] for n in WEIGHTS],
            *[new_m[n] for n in WEIGHTS], *[new_v[n] for n in WEIGHTS])
```

```python
import functools
import math

import jax
import jax.numpy as jnp
from jax import lax
from jax.experimental import pallas as pl
from jax.experimental.pallas import tpu as pltpu

F32 = jnp.float32
MXU_DTYPE = jnp.bfloat16
MESH_AXES = ("x", "y", "c")
N_DEV = 8
LANE = 128
SUBLANE = 8
ROWWISE_VMEM_BUDGET = 20 * 1024 * 1024
MM_TILES = (1408, 1024, 512, 256, 128)
EXCHANGE_DTYPE = jnp.bfloat16
LN_EPS = 1e-5
CHUNK = 64
HEAD_DIM = 64
N_GROUPS = 4
N_STATE = 128
CONV_PAD = 32
CONV_ROWS = 256
ADAM_LR, ADAM_B1, ADAM_B2, ADAM_EPS, ADAM_WD, ADAM_STEP = 0.001, 0.9, 0.999, 1e-08, 0.01, 10

BIG = ("e_w_in", "e_w_out", "o_w_in", "o_w_out", "f_w_up", "f_w_down", "ple_w_proj", "ple_w_gate")
SMALL_SHARDED = ("e_conv_a_w", "e_conv_b_w", "o_conv_w", "f_conv_w", "ln_g", "ln_b")
REPLICATED = ("e_conv_a_b", "e_ln_a_g", "e_ln_a_b", "e_conv_b_b", "e_dt_bias", "e_a_log", "e_d_skip",
              "e_norm_b_g", "f_conv_b")
TRANSPOSED = ("e_w_in", "o_w_in", "f_w_up", "ple_w_proj")
EXCHANGE_ORDER = ("e_w_out", "o_w_out", "ple_w_gate", "ple_w_proj", "o_w_in", "f_w_down", "f_w_up", "e_w_in")
WEIGHTS = ("e_w_in", "e_conv_a_w", "e_conv_a_b", "e_ln_a_g", "e_ln_a_b", "e_conv_b_w", "e_conv_b_b", "e_dt_bias",
           "e_a_log", "e_d_skip", "e_norm_b_g", "e_w_out", "o_w_in", "o_conv_w", "o_w_out", "f_w_up", "f_conv_w",
           "f_conv_b", "f_w_down", "ple_w_proj", "ple_w_gate", "ln_g", "ln_b")

S = jax.ShapeDtypeStruct


class Win:
    def __init__(self, arr, w=None, idx=0, coef=1.0):
        self.arr, self.w, self.idx, self.coef = arr, (arr.shape[1] if w is None else w), idx, coef


def _win(a):
    return a if isinstance(a, Win) else Win(a)


def _pick(n, prefs):
    for p in prefs:
        if p <= n and n % p == 0:
            return p
    return n


_MM_DIMS = {"nn": (1, 0), "nt": (1, 1), "tn": (0, 0)}


def mm(name, a, b, mode):
    ca, cb = _MM_DIMS[mode]
    kdim = a.shape[ca]
    m = a.shape[1 - ca]
    n = b.shape[1 - cb]
    assert b.shape[cb] == kdim, (name, a.shape, b.shape, mode)
    tm = _pick(m, MM_TILES)
    tn = _pick(n, MM_TILES)
    tk = kdim if kdim <= MM_TILES[0] else _pick(kdim, MM_TILES)
    nk = kdim // tk

    def body(a_ref, b_ref, o_ref):
        d = lax.dot_general(a_ref[...].astype(MXU_DTYPE), b_ref[...].astype(MXU_DTYPE),
                            (((ca,), (cb,)), ((), ())), preferred_element_type=F32)
        if nk == 1:
            o_ref[...] = d
        else:
            k = pl.program_id(2)

            @pl.when(k == 0)
            def _():
                o_ref[...] = d

            @pl.when(k > 0)
            def _():
                o_ref[...] += d

    a_spec = pl.BlockSpec((tm, tk), lambda i, j, k: (i, k)) if ca == 1 else pl.BlockSpec((tk, tm), lambda i, j, k: (k, i))
    b_spec = pl.BlockSpec((tk, tn), lambda i, j, k: (k, j)) if cb == 0 else pl.BlockSpec((tn, tk), lambda i, j, k: (j, k))
    return pl.pallas_call(
        body, name=name, grid=(m // tm, n // tn, nk),
        in_specs=[a_spec, b_spec], out_specs=pl.BlockSpec((tm, tn), lambda i, j, k: (i, j)),
        out_shape=S((m, n), F32),
        compiler_params=pltpu.CompilerParams(dimension_semantics=("parallel", "parallel", "arbitrary")),
    )(a, b)


def _row_block(t, widths):
    tb = 512
    while tb > SUBLANE and (t % tb or tb * sum(widths) * 8 > ROWWISE_VMEM_BUDGET):
        tb //= 2
    return tb


def _tok_spec(tb, w):
    return pl.BlockSpec((tb, w.w), functools.partial(lambda i, idx: (i, idx), idx=w.idx))


def _par_spec(p):
    return pl.BlockSpec((1, p.shape[1]), lambda i: (0, 0))


def rowwise(name, fn, tok, par, out_widths, red_widths=(), out_dtypes=None):
    tok = [_win(t) for t in tok]
    t = tok[0].arr.shape[0]
    tb = _row_block(t, [w.w for w in tok] + list(out_widths))
    n_tok, n_par, n_out = len(tok), len(par), len(out_widths)
    out_dtypes = [F32] * n_out if out_dtypes is None else out_dtypes

    def body(*refs):
        ins = [r[...] for r in refs[:n_tok + n_par]]
        res = fn(*ins)
        out_refs = refs[n_tok + n_par:n_tok + n_par + n_out]
        red_refs = refs[n_tok + n_par + n_out:]
        for r, v in zip(out_refs, res[:n_out]):
            r[...] = v.astype(r.dtype)
        if red_refs:
            @pl.when(pl.program_id(0) == 0)
            def _():
                for r in red_refs:
                    r[...] = jnp.zeros_like(r)
            for r, v in zip(red_refs, res[n_out:]):
                r[...] += v

    outs = pl.pallas_call(
        body, name=name, grid=(t // tb,),
        in_specs=[_tok_spec(tb, w) for w in tok] + [_par_spec(p) for p in par],
        out_specs=[pl.BlockSpec((tb, w), lambda i: (i, 0)) for w in out_widths]
        + [pl.BlockSpec((1, w), lambda i: (0, 0)) for w in red_widths],
        out_shape=[S((t, w), dt) for w, dt in zip(out_widths, out_dtypes)] + [S((1, w), F32) for w in red_widths],
        compiler_params=pltpu.CompilerParams(dimension_semantics=("arbitrary",)),
    )(*[w.arr for w in tok], *par)
    return outs


def rowwise_bwd(name, fn, tok, par, cts, d_dtypes=None):
    tok = [_win(t) for t in tok]
    cts = [[_win(c) for c in group] for group in cts]
    d_dtypes = [(F32,)] * len(tok) if d_dtypes is None else d_dtypes
    t = tok[0].arr.shape[0]
    flat_cts = [c for group in cts for c in group]
    d_outs = [(i, w.w, dt) for i, (w, dts) in enumerate(zip(tok, d_dtypes)) for dt in dts]
    tb = _row_block(t, [w.w for w in tok] + [c.w for c in flat_cts] + [w for _, w, _ in d_outs])
    n_tok, n_par, n_ct, n_d = len(tok), len(par), len(flat_cts), len(d_outs)

    def body(*refs):
        tok_vals = [r[...] for r in refs[:n_tok]]
        par_vals = [r[...] for r in refs[n_tok:n_tok + n_par]]
        ct_refs = refs[n_tok + n_par:n_tok + n_par + n_ct]
        d_refs = refs[n_tok + n_par + n_ct:n_tok + n_par + n_ct + n_d]
        dp_refs = refs[n_tok + n_par + n_ct + n_d:]
        ct_vals, pos = [], 0
        for group in cts:
            acc = None
            for c in group:
                v = ct_refs[pos][...]
                if c.coef != 1.0:
                    v = v * c.coef
                acc = v if acc is None else acc + v
                pos += 1
            ct_vals.append(acc)
        _, vjp = jax.vjp(lambda *a: tuple(fn(*a)), *tok_vals, *par_vals)
        grads = vjp(tuple(ct_vals))
        for r, (i, _, _) in zip(d_refs, d_outs):
            r[...] = grads[i].astype(r.dtype)
        if dp_refs:
            @pl.when(pl.program_id(0) == 0)
            def _():
                for r in dp_refs:
                    r[...] = jnp.zeros_like(r)
            for r, v in zip(dp_refs, grads[n_tok:]):
                r[...] += v

    outs = pl.pallas_call(
        body, name=name, grid=(t // tb,),
        in_specs=[_tok_spec(tb, w) for w in tok] + [_par_spec(p) for p in par] + [_tok_spec(tb, c) for c in flat_cts],
        out_specs=[pl.BlockSpec((tb, w), lambda i: (i, 0)) for _, w, _ in d_outs] + [_par_spec(p) for p in par],
        out_shape=[S((t, w), dt) for _, w, dt in d_outs] + [S(p.shape, F32) for p in par],
        compiler_params=pltpu.CompilerParams(dimension_semantics=("arbitrary",)),
    )(*[w.arr for w in tok], *par, *[c.arr for c in flat_cts])
    return outs[:n_d], outs[n_d:]


def _sigmoid(x):
    return 1.0 / (1.0 + jnp.exp(-x))


def _silu(x):
    return x * _sigmoid(x)


def _ln(x, g, b):
    mu = jnp.mean(x, axis=-1, keepdims=True)
    var = jnp.mean(jnp.square(x - mu), axis=-1, keepdims=True)
    return (x - mu) * lax.rsqrt(var + LN_EPS) * g + b


def f_glu(ua, ug):
    return (ua * _sigmoid(ug),)


def f_ln_silu(a1, g, b):
    return (_silu(_ln(a1, g, b)),)


def f_silu3(a, b, c):
    return (_silu(a), _silu(b), _silu(c))


def f_softplus(dt_raw, bias):
    return (jax.nn.softplus(dt_raw + bias),)


def f_gate_rms(yssd, z, g):
    y = yssd * _silu(z)
    return (y * lax.rsqrt(jnp.mean(jnp.square(y), axis=-1, keepdims=True) + LN_EPS) * g,)


def f_ln(pre, g, b):
    return (_ln(pre, g, b),)


def f_mul(a, b):
    return (a * b,)


def f_gate_mul(pp, gt):
    return (pp * _sigmoid(gt),)


def conv_fwd(name, x, w, b):
    x = _win(x)
    t, c = x.arr.shape[0], x.w
    kw = w.shape[0]
    cb = LANE
    off = x.idx * (c // cb)
    rows = min(CONV_ROWS, t)
    has_b = b is not None

    def body(*refs):
        if has_b:
            x_ref, w_ref, b_ref, y_ref, xp_ref = refs
        else:
            x_ref, w_ref, y_ref, xp_ref = refs
        xp_ref[0:CONV_PAD, :] = jnp.zeros((CONV_PAD, cb), F32)
        xp_ref[CONV_PAD:CONV_PAD + t, :] = x_ref[...]

        def step(s, carry):
            base = pl.multiple_of(s * rows, rows)
            acc = jnp.zeros((rows, cb), F32)
            if has_b:
                acc = acc + b_ref[...]
            for k in range(kw):
                acc = acc + w_ref[k:k + 1, :] * xp_ref[pl.ds(base + CONV_PAD - (kw - 1) + k, rows), :]
            y_ref[pl.ds(base, rows), :] = acc
            return carry

        lax.fori_loop(0, t // rows, step, 0)

    in_specs = [pl.BlockSpec((t, cb), lambda j: (0, off + j)), pl.BlockSpec((kw, cb), lambda j: (0, j))]
    args = [x.arr, w]
    if has_b:
        in_specs.append(pl.BlockSpec((1, cb), lambda j: (0, j)))
        args.append(b)
    return pl.pallas_call(
        body, name=name, grid=(c // cb,), in_specs=in_specs,
        out_specs=pl.BlockSpec((t, cb), lambda j: (0, j)), out_shape=S((t, c), F32),
        scratch_shapes=[pltpu.VMEM((CONV_PAD + t, cb), F32)],
        compiler_params=pltpu.CompilerParams(dimension_semantics=("parallel",)),
    )(*args)


def conv_bwd(name, x, dy, w, dx_dtype=F32):
    x, dy = _win(x), _win(dy)
    t, c = x.arr.shape[0], x.w
    kw = w.shape[0]
    cb = LANE
    xoff = x.idx * (c // cb)
    dyoff = dy.idx * (c // cb)
    rows = min(CONV_ROWS, t)

    def body(x_ref, dy_ref, w_ref, dx_ref, dw_ref, db_ref, xp_ref, dyp_ref):
        xp_ref[0:CONV_PAD, :] = jnp.zeros((CONV_PAD, cb), F32)
        xp_ref[CONV_PAD:CONV_PAD + t, :] = x_ref[...]
        dyp_ref[0:t, :] = dy_ref[...]
        dyp_ref[t:t + CONV_PAD, :] = jnp.zeros((CONV_PAD, cb), F32)

        def fold(v):
            return jnp.sum(v.reshape(rows // SUBLANE, SUBLANE, cb), axis=0)

        def step(s, carry):
            base = pl.multiple_of(s * rows, rows)
            dyc = dy_ref[pl.ds(base, rows), :]
            acc = jnp.zeros((rows, cb), F32)
            new = []
            for k in range(kw):
                acc = acc + w_ref[k:k + 1, :] * dyp_ref[pl.ds(base + (kw - 1) - k, rows), :]
                new.append(carry[k] + fold(dyc * xp_ref[pl.ds(base + CONV_PAD - (kw - 1) + k, rows), :]))
            new.append(carry[kw] + fold(dyc))
            dx_ref[pl.ds(base, rows), :] = acc.astype(dx_ref.dtype)
            return tuple(new)

        init = tuple(jnp.zeros((SUBLANE, cb), F32) for _ in range(kw + 1))
        parts = lax.fori_loop(0, t // rows, step, init)
        for k in range(kw):
            dw_ref[k:k + 1, :] = jnp.sum(parts[k], axis=0, keepdims=True)
        db_ref[...] = jnp.sum(parts[kw], axis=0, keepdims=True)

    return pl.pallas_call(
        body, name=name, grid=(c // cb,),
        in_specs=[pl.BlockSpec((t, cb), lambda j: (0, xoff + j)), pl.BlockSpec((t, cb), lambda j: (0, dyoff + j)),
                  pl.BlockSpec((kw, cb), lambda j: (0, j))],
        out_specs=[pl.BlockSpec((t, cb), lambda j: (0, j)), pl.BlockSpec((kw, cb), lambda j: (0, j)),
                   pl.BlockSpec((1, cb), lambda j: (0, j))],
        out_shape=[S((t, c), dx_dtype), S((kw, c), F32), S((1, c), F32)],
        scratch_shapes=[pltpu.VMEM((CONV_PAD + t, cb), F32), pltpu.VMEM((CONV_PAD + t, cb), F32)],
        compiler_params=pltpu.CompilerParams(dimension_semantics=("parallel",)),
    )(x.arr, dy.arr, w)


def gated_conv_fwd(name, hpre, w, b, out_dtype):
    t, c2 = hpre.shape
    ff = c2 // 2
    kw = w.shape[0]
    cb = LANE
    nb = ff // cb
    rows = min(CONV_ROWS, t)

    def body(h1_ref, h2_ref, w1_ref, w2_ref, b1_ref, b2_ref, y_ref, xp1_ref, xp2_ref):
        for xp_ref, h_ref in ((xp1_ref, h1_ref), (xp2_ref, h2_ref)):
            xp_ref[0:CONV_PAD, :] = jnp.zeros((CONV_PAD, cb), F32)
            xp_ref[CONV_PAD:CONV_PAD + t, :] = h_ref[...]

        def step(s, carry):
            base = pl.multiple_of(s * rows, rows)
            h1 = jnp.zeros((rows, cb), F32) + b1_ref[...]
            h2 = jnp.zeros((rows, cb), F32) + b2_ref[...]
            for k in range(kw):
                at = pl.ds(base + CONV_PAD - (kw - 1) + k, rows)
                h1 = h1 + w1_ref[k:k + 1, :] * xp1_ref[at, :]
                h2 = h2 + w2_ref[k:k + 1, :] * xp2_ref[at, :]
            y_ref[pl.ds(base, rows), :] = (_silu(h1) * h2).astype(y_ref.dtype)
            return carry

        lax.fori_loop(0, t // rows, step, 0)

    col1 = lambda r: pl.BlockSpec((r, cb), lambda j: (0, j))
    col2 = lambda r: pl.BlockSpec((r, cb), lambda j: (0, nb + j))
    return pl.pallas_call(
        body, name=name, grid=(nb,),
        in_specs=[col1(t), col2(t), col1(kw), col2(kw), col1(1), col2(1)],
        out_specs=col1(t), out_shape=S((t, ff), out_dtype),
        scratch_shapes=[pltpu.VMEM((CONV_PAD + t, cb), F32)] * 2,
        compiler_params=pltpu.CompilerParams(dimension_semantics=("parallel",)),
    )(hpre, hpre, w, w, b, b)


def gated_conv_bwd(name, hpre, dact, w, b, dx_dtype):
    t, c2 = hpre.shape
    ff = c2 // 2
    kw = w.shape[0]
    cb = LANE
    nb = ff // cb
    rows = min(CONV_ROWS, t)

    def body(own_ref, oth_ref, da_ref, wo_ref, wt_ref, bo_ref, bt_ref, dx_ref, dw_ref, db_ref,
             xpo_ref, xpt_ref, dhp_ref):
        for xp_ref, h_ref in ((xpo_ref, own_ref), (xpt_ref, oth_ref)):
            xp_ref[0:CONV_PAD, :] = jnp.zeros((CONV_PAD, cb), F32)
            xp_ref[CONV_PAD:CONV_PAD + t, :] = h_ref[...]
        dhp_ref[t:t + CONV_PAD, :] = jnp.zeros((CONV_PAD, cb), F32)

        def fold(v):
            return jnp.sum(v.reshape(rows // SUBLANE, SUBLANE, cb), axis=0)

        def first_pass(own_is_gate):
            def step(s, carry):
                base = pl.multiple_of(s * rows, rows)
                ho = jnp.zeros((rows, cb), F32) + bo_ref[...]
                ht = jnp.zeros((rows, cb), F32) + bt_ref[...]
                for k in range(kw):
                    at = pl.ds(base + CONV_PAD - (kw - 1) + k, rows)
                    ho = ho + wo_ref[k:k + 1, :] * xpo_ref[at, :]
                    ht = ht + wt_ref[k:k + 1, :] * xpt_ref[at, :]
                da = da_ref[pl.ds(base, rows), :]
                if own_is_gate:
                    sg = _sigmoid(ho)
                    dh = da * ht * (sg * (1.0 + ho * (1.0 - sg)))
                else:
                    dh = da * _silu(ht)
                dhp_ref[pl.ds(base, rows), :] = dh
                new = [carry[k] + fold(dh * xpo_ref[pl.ds(base + CONV_PAD - (kw - 1) + k, rows), :]) for k in range(kw)]
                new.append(carry[kw] + fold(dh))
                return tuple(new)

            init = tuple(jnp.zeros((SUBLANE, cb), F32) for _ in range(kw + 1))
            parts = lax.fori_loop(0, t // rows, step, init)
            for k in range(kw):
                dw_ref[k:k + 1, :] = jnp.sum(parts[k], axis=0, keepdims=True)
            db_ref[...] = jnp.sum(parts[kw], axis=0, keepdims=True)

        half = pl.program_id(0)

        @pl.when(half == 0)
        def _():
            first_pass(True)

        @pl.when(half == 1)
        def _():
            first_pass(False)

        def second(s, carry):
            base = pl.multiple_of(s * rows, rows)
            acc = jnp.zeros((rows, cb), F32)
            for k in range(kw):
                acc = acc + wo_ref[k:k + 1, :] * dhp_ref[pl.ds(base + (kw - 1) - k, rows), :]
            dx_ref[pl.ds(base, rows), :] = acc.astype(dx_ref.dtype)
            return carry

        lax.fori_loop(0, t // rows, second, 0)

    own = lambda r: pl.BlockSpec((r, cb), lambda h, j: (0, h * nb + j))
    oth = lambda r: pl.BlockSpec((r, cb), lambda h, j: (0, (1 - h) * nb + j))
    return pl.pallas_call(
        body, name=name, grid=(2, nb),
        in_specs=[own(t), oth(t), pl.BlockSpec((t, cb), lambda h, j: (0, j)), own(kw), oth(kw), own(1), oth(1)],
        out_specs=[own(t), own(kw), own(1)],
        out_shape=[S((t, c2), dx_dtype), S((kw, c2), F32), S((1, c2), F32)],
        scratch_shapes=[pltpu.VMEM((CONV_PAD + t, cb), F32)] * 3,
        compiler_params=pltpu.CompilerParams(dimension_semantics=("parallel", "parallel")),
    )(hpre, hpre, dact, w, w, b, b)


def _bdot(a, b, ca, cb):
    return lax.dot_general(a.astype(MXU_DTYPE), b.astype(MXU_DTYPE), (((ca,), (cb,)), ((0,), (0,))),
                           preferred_element_type=F32)


@jax.custom_vjp
def bmm_nn(a, b):
    return _bdot(a, b, 2, 1)


bmm_nn.defvjp(lambda a, b: (_bdot(a, b, 2, 1), (a, b)),
              lambda r, g: (_bdot(g, r[1], 2, 2), _bdot(r[0], g, 1, 1)))


@jax.custom_vjp
def bmm_tn(a, b):
    return _bdot(a, b, 1, 1)


bmm_tn.defvjp(lambda a, b: (_bdot(a, b, 1, 1), (a, b)),
              lambda r, g: (_bdot(r[1], g, 2, 2), _bdot(r[0], g, 2, 1)))


@jax.custom_vjp
def bmm_nt(a, b):
    return _bdot(a, b, 2, 2)


bmm_nt.defvjp(lambda a, b: (_bdot(a, b, 2, 2), (a, b)),
              lambda r, g: (_bdot(g, r[1], 2, 1), _bdot(g, r[0], 1, 1)))


def ssd_chunk(x, dt, bm, cm, hprev, a_log, dsk):
    hg, ln, _ = x.shape
    n = bm.shape[1]
    ii = lax.broadcasted_iota(jnp.int32, (ln, ln), 0)
    jj = lax.broadcasted_iota(jnp.int32, (ln, ln), 1)
    tril, eye, triu = (ii >= jj)[None], (ii == jj)[None], (ii <= jj)[None]
    da = dt * (-jnp.exp(a_log))
    da_row = jnp.sum(jnp.where(eye, da, 0.0), axis=1, keepdims=True)
    dt_row = jnp.sum(jnp.where(eye, dt, 0.0), axis=1, keepdims=True)
    cum_c = jnp.sum(jnp.where(tril, da_row, 0.0), axis=2, keepdims=True)
    cum_r = jnp.sum(jnp.where(triu, da, 0.0), axis=1, keepdims=True)
    last = jnp.sum(da, axis=1, keepdims=True)
    decay = jnp.where(tril, jnp.exp(jnp.where(tril, cum_c - cum_r, 0.0)), 0.0)
    cb = bmm_nt(cm[None], bm[None])
    y_diag = bmm_nn(cb * decay * dt_row, x)
    bb = jnp.broadcast_to(bm[None], (hg, ln, n))
    cc = jnp.broadcast_to(cm[None], (hg, ln, n))
    states = bmm_tn(x * (jnp.exp(last - cum_c) * dt), bb)
    y_off = bmm_nt(cc, hprev) * jnp.exp(cum_c)
    hnew = hprev * jnp.exp(last) + states
    return y_diag + y_off + dsk * x, hnew


def _ssd_dims(xh, bm):
    h, t, p = xh.shape
    hg = h // N_GROUPS
    n = bm.shape[1] // N_GROUPS
    return h, t, p, hg, n, t // CHUNK


def ssd_fwd(name, xh, dth, bm, cm, a_log, dsk):
    h, t, p, hg, n, nc = _ssd_dims(xh, bm)

    def body(al_ref, dk_ref, x_ref, dt_ref, b_ref, c_ref, y_ref, hp_ref, h_scr):
        @pl.when(pl.program_id(1) == 0)
        def _():
            h_scr[...] = jnp.zeros_like(h_scr)

        hprev = h_scr[...]
        hp_ref[:, 0] = hprev
        y, hnew = ssd_chunk(x_ref[...], dt_ref[...], b_ref[...], c_ref[...], hprev, al_ref[...], dk_ref[...])
        y_ref[...] = y
        h_scr[...] = hnew

    head = pl.BlockSpec((hg, 1, 1), lambda g, c: (g, 0, 0))
    return pl.pallas_call(
        body, name=name, grid=(N_GROUPS, nc),
        in_specs=[head, head, pl.BlockSpec((hg, CHUNK, p), lambda g, c: (g, c, 0)),
                  pl.BlockSpec((hg, CHUNK, 1), lambda g, c: (g, c, 0)),
                  pl.BlockSpec((CHUNK, n), lambda g, c: (c, g)), pl.BlockSpec((CHUNK, n), lambda g, c: (c, g))],
        out_specs=[pl.BlockSpec((hg, CHUNK, p), lambda g, c: (g, c, 0)),
                   pl.BlockSpec((hg, 1, p, n), lambda g, c: (g, c, 0, 0))],
        out_shape=[S((h, t, p), F32), S((h, nc, p, n), F32)],
        scratch_shapes=[pltpu.VMEM((hg, p, n), F32)],
        compiler_params=pltpu.CompilerParams(dimension_semantics=("parallel", "arbitrary")),
    )(a_log, dsk, xh, dth, bm, cm)


def ssd_bwd(name, xh, dth, bm, cm, a_log, dsk, hp, dyh):
    h, t, p, hg, n, nc = _ssd_dims(xh, bm)

    def body(al_ref, dk_ref, x_ref, dt_ref, b_ref, c_ref, hp_ref, dy_ref,
             dx_ref, ddt_ref, db_ref, dc_ref, dal_ref, ddk_ref, dh_scr):
        @pl.when(pl.program_id(1) == 0)
        def _():
            dh_scr[...] = jnp.zeros_like(dh_scr)
            dal_ref[...] = jnp.zeros_like(dal_ref)
            ddk_ref[...] = jnp.zeros_like(ddk_ref)

        _, vjp = jax.vjp(ssd_chunk, x_ref[...], dt_ref[...], b_ref[...], c_ref[...], hp_ref[:, 0],
                         al_ref[...], dk_ref[...])
        gx, gdt, gb, gc, ghp, gal, gdk = vjp((dy_ref[...], dh_scr[...]))
        dx_ref[...] = gx
        ddt_ref[...] = gdt
        db_ref[...] = gb
        dc_ref[...] = gc
        dh_scr[...] = ghp
        dal_ref[...] += gal
        ddk_ref[...] += gdk

    head = pl.BlockSpec((hg, 1, 1), lambda g, c: (g, 0, 0))
    xs = pl.BlockSpec((hg, CHUNK, p), lambda g, c: (g, nc - 1 - c, 0))
    ds = pl.BlockSpec((hg, CHUNK, 1), lambda g, c: (g, nc - 1 - c, 0))
    bs = pl.BlockSpec((CHUNK, n), lambda g, c: (nc - 1 - c, g))
    return pl.pallas_call(
        body, name=name, grid=(N_GROUPS, nc),
        in_specs=[head, head, xs, ds, bs, bs, pl.BlockSpec((hg, 1, p, n), lambda g, c: (g, nc - 1 - c, 0, 0)), xs],
        out_specs=[xs, ds, bs, bs, head, head],
        out_shape=[S((h, t, p), F32), S((h, t, 1), F32), S(bm.shape, F32), S(cm.shape, F32),
                   S((h, 1, 1), F32), S((h, 1, 1), F32)],
        scratch_shapes=[pltpu.VMEM((hg, p, n), F32)],
        compiler_params=pltpu.CompilerParams(dimension_semantics=("parallel", "arbitrary")),
    )(a_log, dsk, xh, dth, bm, cm, hp, dyh)


def to_heads(v):
    t = v.shape[0]
    return v.reshape(t, -1, HEAD_DIM).transpose(1, 0, 2)


def from_heads(v):
    return v.transpose(1, 0, 2).reshape(v.shape[1], -1)


def _alpha(depth):
    return (2.0 * depth) ** 0.25


def _pad_lanes(v):
    return jnp.pad(v, ((0, 0), (0, LANE - v.shape[1])))


def split_even_weights(w, j):
    d = w["e_w_in"].shape[2]
    da = w["e_conv_a_w"].shape[2]
    db = w["e_norm_b_g"].shape[1]
    gn = N_GROUPS * N_STATE
    nh = w["e_dt_bias"].shape[1]
    main = 2 * da + 2 * db + 2 * gn
    win = w["e_w_in"][j]
    ox = 2 * da + db
    cw, cbias = w["e_conv_b_w"][j], w["e_conv_b_b"][j][None]
    return dict(
        d=d, da=da, db=db, gn=gn, nh=nh, main=main,
        win_main=win[:main], win_dt=jnp.pad(win[main:], ((0, LANE - nh), (0, 0))),
        caw=w["e_conv_a_w"][j], cab=w["e_conv_a_b"][j][None], lag=w["e_ln_a_g"][j][None], lab=w["e_ln_a_b"][j][None],
        cw_xs=cw[:, :db], cw_b=cw[:, db:db + gn], cw_c=cw[:, db + gn:],
        cb_xs=cbias[:, :db], cb_b=cbias[:, db:db + gn], cb_c=cbias[:, db + gn:],
        dt_bias=_pad_lanes(w["e_dt_bias"][j][None]), a_log=w["e_a_log"][j].reshape(nh, 1, 1),
        dsk=w["e_d_skip"][j].reshape(nh, 1, 1), norm_g=w["e_norm_b_g"][j][None],
        wout_a=w["e_w_out"][j][:da], wout_b=w["e_w_out"][j][da:],
    )


def even_fwd(tag, x, xm, lw, ln_g, ln_b, alpha):
    t = x.shape[0]
    da, db, gn, nh = lw["da"], lw["db"], lw["gn"], lw["nh"]
    u = mm(tag + "_win", xm, lw["win_main"], "nt")
    udt = mm(tag + "_windt", xm, lw["win_dt"], "nt")
    ua, ug, z, xs_pre = Win(u, da, 0), Win(u, da, 1), Win(u, db, 2 * da // db), Win(u, db, (2 * da + db) // db)
    b_pre, c_pre = Win(u, gn, (2 * da + 2 * db) // gn), Win(u, gn, (2 * da + 2 * db + gn) // gn)
    (a0,) = rowwise(tag + "_glu", f_glu, [ua, ug], [], [da])
    a1 = conv_fwd(tag + "_conva", a0, lw["caw"], lw["cab"])
    (ya,) = rowwise(tag + "_lna", f_ln_silu, [a1], [lw["lag"], lw["lab"]], [da], out_dtypes=[MXU_DTYPE])
    xs_c = conv_fwd(tag + "_convxs", xs_pre, lw["cw_xs"], lw["cb_xs"])
    b_c = conv_fwd(tag + "_convb", b_pre, lw["cw_b"], lw["cb_b"])
    c_c = conv_fwd(tag + "_convc", c_pre, lw["cw_c"], lw["cb_c"])
    xs, bm, cm = rowwise(tag + "_silu3", f_silu3, [xs_c, b_c, c_c], [], [db, gn, gn])
    (dt,) = rowwise(tag + "_dt", f_softplus, [udt], [lw["dt_bias"]], [LANE])
    xs_h = to_heads(xs)
    dt_h = dt[:, :nh].T.reshape(nh, t, 1)
    y_h, hp = ssd_fwd(tag + "_ssd", xs_h, dt_h, bm, cm, lw["a_log"], lw["dsk"])
    yssd = from_heads(y_h)
    (yb,) = rowwise(tag + "_gate", f_gate_rms, [yssd, z], [lw["norm_g"]], [db], out_dtypes=[MXU_DTYPE])
    ma = mm(tag + "_wouta", ya, lw["wout_a"], "nn")
    mb = mm(tag + "_woutb", yb, lw["wout_b"], "nn")

    def f_res(xv, mav, mbv, g, b):
        pre = alpha * xv + mav + mbv
        y = _ln(pre, g, b)
        return y, y, pre

    x1, x1m, pre = rowwise(tag + "_res", f_res, [x, ma, mb], [ln_g, ln_b], [x.shape[1]] * 3,
                           out_dtypes=[F32, MXU_DTYPE, F32])
    saved = dict(xm=xm, u=u, udt=udt, a0=a0, a1=a1, ya=ya, xs_c=xs_c, b_c=b_c, c_c=c_c, xs_h=xs_h, dt_h=dt_h, bm=bm,
                 cm=cm, hp=hp, yssd=yssd, yb=yb, pre=pre)
    return x1, x1m, saved


def even_bwd(tag, dx1_pieces, sv, lw, ln_g, ln_b, alpha):
    t = sv["u"].shape[0]
    da, db, gn, nh = lw["da"], lw["db"], lw["gn"], lw["nh"]
    u, xm = sv["u"], sv["xm"]
    mx = (MXU_DTYPE,)
    ua, ug, z, xs_pre = Win(u, da, 0), Win(u, da, 1), Win(u, db, 2 * da // db), Win(u, db, (2 * da + db) // db)
    b_pre, c_pre = Win(u, gn, (2 * da + 2 * db) // gn), Win(u, gn, (2 * da + 2 * db + gn) // gn)
    (dpre, dprem), (dg0, db0) = rowwise_bwd(tag + "_res_b", f_ln, [sv["pre"]], [ln_g, ln_b], [dx1_pieces],
                                            d_dtypes=[(F32, MXU_DTYPE)])
    dya = mm(tag + "_dya", dprem, lw["wout_a"], "nt")
    dyb = mm(tag + "_dyb", dprem, lw["wout_b"], "nt")
    dwout_a = mm(tag + "_dwouta", sv["ya"], dprem, "tn")
    dwout_b = mm(tag + "_dwoutb", sv["yb"], dprem, "tn")
    (dyssd, dz), (dnorm_g,) = rowwise_bwd(tag + "_gate_b", f_gate_rms, [sv["yssd"], z], [lw["norm_g"]], [[dyb]],
                                          d_dtypes=[(F32,), mx])
    dxs_h, ddt_h, dbm, dcm, dalog, ddsk = ssd_bwd(tag + "_ssd_b", sv["xs_h"], sv["dt_h"], sv["bm"], sv["cm"],
                                                  lw["a_log"], lw["dsk"], sv["hp"], to_heads(dyssd))
    dxs = from_heads(dxs_h)
    ddt = _pad_lanes(ddt_h.reshape(nh, t).T)
    (dudt,), (ddt_bias,) = rowwise_bwd(tag + "_dt_b", f_softplus, [sv["udt"]], [lw["dt_bias"]], [[ddt]],
                                       d_dtypes=[mx])
    (dxs_c, db_c, dc_c), _ = rowwise_bwd(tag + "_silu3_b", f_silu3, [sv["xs_c"], sv["b_c"], sv["c_c"]], [],
                                         [[dxs], [dbm], [dcm]])
    dxs_pre, dcw_xs, dcb_xs = conv_bwd(tag + "_convxs_b", xs_pre, dxs_c, lw["cw_xs"], MXU_DTYPE)
    db_pre, dcw_b, dcb_b = conv_bwd(tag + "_convb_b", b_pre, db_c, lw["cw_b"], MXU_DTYPE)
    dc_pre, dcw_c, dcb_c = conv_bwd(tag + "_convc_b", c_pre, dc_c, lw["cw_c"], MXU_DTYPE)
    (da1,), (dlag, dlab) = rowwise_bwd(tag + "_lna_b", f_ln_silu, [sv["a1"]], [lw["lag"], lw["lab"]], [[dya]])
    da0, dcaw, dcab = conv_bwd(tag + "_conva_b", sv["a0"], da1, lw["caw"])
    (dua, dug), _ = rowwise_bwd(tag + "_glu_b", f_glu, [ua, ug], [], [[da0]], d_dtypes=[mx, mx])
    du = jnp.concatenate([dua, dug, dz, dxs_pre, db_pre, dc_pre], axis=1)
    dx_m = mm(tag + "_dxm", du, lw["win_main"], "nn")
    dx_dt = mm(tag + "_dxdt", dudt, lw["win_dt"], "nn")
    dwin_main = mm(tag + "_dwin", du, xm, "tn")
    dwin_dt = mm(tag + "_dwindt", dudt, xm, "tn")
    grads = dict(
        e_w_in=jnp.concatenate([dwin_main, dwin_dt[:nh]], axis=0),
        e_conv_a_w=dcaw, e_conv_a_b=dcab[0], e_ln_a_g=dlag[0], e_ln_a_b=dlab[0],
        e_conv_b_w=jnp.concatenate([dcw_xs, dcw_b, dcw_c], axis=1),
        e_conv_b_b=jnp.concatenate([dcb_xs, dcb_b, dcb_c], axis=1)[0],
        e_dt_bias=ddt_bias[0, :nh], e_a_log=dalog.reshape(nh), e_d_skip=ddsk.reshape(nh), e_norm_b_g=dnorm_g[0],
        e_w_out=jnp.concatenate([dwout_a, dwout_b], axis=0), ln_g0=dg0[0], ln_b0=db0[0],
    )
    return [Win(dpre, coef=alpha), dx_m, dx_dt], grads


def odd_fwd(tag, x, xm, w, j, ln_g, ln_b, alpha):
    d = x.shape[1]
    u = mm(tag + "_win", xm, w["o_w_in"][j], "nt")
    bg, cg, v = Win(u, d, 0), Win(u, d, 1), Win(u, d, 2)
    (s,) = rowwise(tag + "_cv", f_mul, [cg, v], [], [d])
    cs = conv_fwd(tag + "_conv", s, w["o_conv_w"][j], None)
    (m,) = rowwise(tag + "_bm", f_mul, [bg, cs], [], [d], out_dtypes=[MXU_DTYPE])
    mix = mm(tag + "_wout", m, w["o_w_out"][j], "nn")

    def f_res(xv, mv, g, b):
        pre = alpha * xv + mv
        y = _ln(pre, g, b)
        return y, y, pre

    x1, x1m, pre = rowwise(tag + "_res", f_res, [x, mix], [ln_g, ln_b], [d] * 3, out_dtypes=[F32, MXU_DTYPE, F32])
    return x1, x1m, dict(xm=xm, u=u, s=s, cs=cs, m=m, pre=pre)


def odd_bwd(tag, dx1_pieces, sv, w, j, ln_g, ln_b, alpha):
    xm, u = sv["xm"], sv["u"]
    d = xm.shape[1]
    mx = (MXU_DTYPE,)
    bg, cg, v = Win(u, d, 0), Win(u, d, 1), Win(u, d, 2)
    (dpre, dprem), (dg0, db0) = rowwise_bwd(tag + "_res_b", f_ln, [sv["pre"]], [ln_g, ln_b], [dx1_pieces],
                                            d_dtypes=[(F32, MXU_DTYPE)])
    dm = mm(tag + "_dm", dprem, w["o_w_out"][j], "nt")
    dwout = mm(tag + "_dwout", sv["m"], dprem, "tn")
    (dbg, dcs), _ = rowwise_bwd(tag + "_bm_b", f_mul, [bg, sv["cs"]], [], [[dm]], d_dtypes=[mx, (F32,)])
    ds, dcw, _ = conv_bwd(tag + "_conv_b", sv["s"], dcs, w["o_conv_w"][j])
    (dcg, dv), _ = rowwise_bwd(tag + "_cv_b", f_mul, [cg, v], [], [[ds]], d_dtypes=[mx, mx])
    du = jnp.concatenate([dbg, dcg, dv], axis=1)
    dx_u = mm(tag + "_dx", du, w["o_w_in"][j], "nn")
    dwin = mm(tag + "_dwin", du, xm, "tn")
    grads = dict(o_w_in=dwin, o_conv_w=dcw, o_w_out=dwout, ln_g0=dg0[0], ln_b0=db0[0])
    return [Win(dpre, coef=alpha), dx_u], grads


def ffn_fwd(tag, x1, x1m, p_i, w, i, ln_g, ln_b, alpha):
    d = x1.shape[1]
    hpre = mm(tag + "_wup", x1m, w["f_w_up"][i], "nt")
    act = gated_conv_fwd(tag + "_fgate", hpre, w["f_conv_w"][i], w["f_conv_b"][i][None], MXU_DTYPE)
    ffn = mm(tag + "_wdown", act, w["f_w_down"][i], "nn")
    pp = mm(tag + "_pproj", p_i, w["ple_w_proj"][i], "nt")
    gt = mm(tag + "_pgate", x1m, w["ple_w_gate"][i], "nn")

    def f_res2(xv, fv, ppv, gtv, g, b):
        pre = alpha * xv + fv + ppv * _sigmoid(gtv)
        y = _ln(pre, g, b)
        return y, y, pre

    x2, x2m, pre = rowwise(tag + "_res2", f_res2, [x1, ffn, pp, gt], [ln_g, ln_b], [d] * 3,
                           out_dtypes=[F32, MXU_DTYPE, F32])
    return x2, x2m, dict(x1m=x1m, hpre=hpre, act=act, pp=pp, gt=gt, pre=pre)


def ffn_bwd(tag, dx2_pieces, sv, p_i, w, i, ln_g, ln_b, alpha):
    x1m = sv["x1m"]
    mx = (MXU_DTYPE,)
    (dpre, dprem), (dg1, db1) = rowwise_bwd(tag + "_res2_b", f_ln, [sv["pre"]], [ln_g, ln_b], [dx2_pieces],
                                            d_dtypes=[(F32, MXU_DTYPE)])
    (dpp, dgt), _ = rowwise_bwd(tag + "_pg_b", f_gate_mul, [sv["pp"], sv["gt"]], [], [[dpre]], d_dtypes=[mx, mx])
    dwproj = mm(tag + "_dwproj", dpp, p_i, "tn")
    dwgate = mm(tag + "_dwgate", x1m, dgt, "tn")
    dx1_a = mm(tag + "_dx1a", dgt, w["ple_w_gate"][i], "nt")
    dact = mm(tag + "_dact", dprem, w["f_w_down"][i], "nt")
    dwdown = mm(tag + "_dwdown", sv["act"], dprem, "tn")
    dhpre, dfcw, dfcb = gated_conv_bwd(tag + "_fgate_b", sv["hpre"], dact, w["f_conv_w"][i], w["f_conv_b"][i][None],
                                       MXU_DTYPE)
    dwup = mm(tag + "_dwup", dhpre, x1m, "tn")
    dx1_b = mm(tag + "_dx1b", dhpre, w["f_w_up"][i], "nn")
    grads = dict(f_w_up=dwup, f_conv_w=dfcw, f_conv_b=dfcb[0], f_w_down=dwdown, ple_w_proj=dwproj, ple_w_gate=dwgate,
                 ln_g1=dg1[0], ln_b1=db1[0])
    return [Win(dpre, coef=alpha), dx1_a, dx1_b], grads


def local_step(x, p, w, target):
    depth = w["ln_g"].shape[0]
    alpha = _alpha(depth)
    d = x.shape[1]
    saved = []
    h = hm = x
    for i in range(depth):
        j = i // 2
        g0, b0, g1, b1 = w["ln_g"][i, 0][None], w["ln_b"][i, 0][None], w["ln_g"][i, 1][None], w["ln_b"][i, 1][None]
        tag = "l%d" % i
        if i % 2 == 0:
            lw = split_even_weights(w, j)
            h, hm, sv_m = even_fwd(tag, h, hm, lw, g0, b0, alpha)
        else:
            lw = None
            h, hm, sv_m = odd_fwd(tag, h, hm, w, j, g0, b0, alpha)
        h, hm, sv_f = ffn_fwd(tag, h, hm, p[i], w, i, g1, b1, alpha)
        saved.append((lw, sv_m, sv_f, (g0, b0, g1, b1)))

    def f_loss(xf, tg):
        diff = xf - tg
        sq = jnp.sum(jnp.sum(jnp.square(diff), axis=1, keepdims=True), axis=0, keepdims=True)
        return diff * (1.0 / d), jnp.broadcast_to(sq, (1, LANE))

    dxf, sq = rowwise("loss", f_loss, [h, target], [], [d], red_widths=[LANE])
    loss = sq[0, 0] * (0.5 / d)

    per_layer = []
    pieces = [dxf]
    for i in reversed(range(depth)):
        j = i // 2
        lw, sv_m, sv_f, (g0, b0, g1, b1) = saved[i]
        tag = "l%d" % i
        pieces, gf = ffn_bwd(tag, pieces, sv_f, p[i], w, i, g1, b1, alpha)
        if i % 2 == 0:
            pieces, gm = even_bwd(tag, pieces, sv_m, lw, g0, b0, alpha)
        else:
            pieces, gm = odd_bwd(tag, pieces, sv_m, w, j, g0, b0, alpha)
        per_layer.append((i, gm, gf))

    def f_sum(*vs):
        acc = None
        for v, c in zip(vs, [pc.coef for pc in map(_win, pieces)]):
            v = v if c == 1.0 else v * c
            acc = v if acc is None else acc + v
        return (acc,)

    (grad_x,) = rowwise("grad_x", f_sum, [Win(_win(pc).arr) for pc in pieces], [], [d])

    by_layer = {i: (gm, gf) for i, gm, gf in per_layer}
    grads = {}
    n_even, n_odd = (depth + 1) // 2, depth // 2
    for name in ("e_w_in", "e_conv_a_w", "e_conv_a_b", "e_ln_a_g", "e_ln_a_b", "e_conv_b_w", "e_conv_b_b", "e_dt_bias",
                 "e_a_log", "e_d_skip", "e_norm_b_g", "e_w_out"):
        grads[name] = jnp.stack([by_layer[2 * j][0][name] for j in range(n_even)])
    for name in ("o_w_in", "o_conv_w", "o_w_out"):
        grads[name] = jnp.stack([by_layer[2 * j + 1][0][name] for j in range(n_odd)])
    for name in ("f_w_up", "f_conv_w", "f_conv_b", "f_w_down", "ple_w_proj", "ple_w_gate"):
        grads[name] = jnp.stack([by_layer[i][1][name] for i in range(depth)])
    grads["ln_g"] = jnp.stack([jnp.stack([by_layer[i][0]["ln_g0"], by_layer[i][1]["ln_g1"]]) for i in range(depth)])
    grads["ln_b"] = jnp.stack([jnp.stack([by_layer[i][0]["ln_b0"], by_layer[i][1]["ln_b1"]]) for i in range(depth)])
    return loss, grad_x, grads


_ANY = pl.BlockSpec(memory_space=pl.ANY)
_MESH = pl.DeviceIdType.MESH


def all_gather(name, xl):
    r, c_ = xl.shape

    def body(x_ref, out_ref, send_sems, recv_sems, local_sem):
        x, y, c = lax.axis_index("x"), lax.axis_index("y"), lax.axis_index("c")
        me, sibling = (x, y, c), (x, y, 1 - c)
        chips = [(1 - x, y), (x, 1 - y), (1 - x, 1 - y)]

        def slot(px, py, pc):
            return out_ref.at[4 * px + 2 * py + pc]

        def copy(k, block, to, src=None):
            return pltpu.make_async_remote_copy(
                src_ref=slot(*block) if src is None else src, dst_ref=slot(*block),
                send_sem=send_sems.at[k], recv_sem=recv_sems.at[k], device_id=to, device_id_type=_MESH)

        mine = pltpu.make_async_copy(x_ref, slot(*me), local_sem)
        mine.start()
        first = [copy(0, me, sibling, src=x_ref)]
        first += [copy(1 + j, me, (*chip, c), src=x_ref) for j, chip in enumerate(chips)]
        for cp in first:
            cp.start()
        passed = [copy(4 + j, (*chip, c), sibling) for j, chip in enumerate(chips)]
        for j, chip in enumerate(chips):
            copy(1 + j, (*chip, c), me).wait_recv()
            passed[j].start()
        copy(0, sibling, me).wait_recv()
        for j, chip in enumerate(chips):
            copy(4 + j, (*chip, 1 - c), me).wait_recv()
        for cp in first + passed:
            cp.wait_send()
        mine.wait()

    return pl.pallas_call(
        body, name=name, out_shape=S((N_DEV, r, c_), xl.dtype), in_specs=[_ANY], out_specs=_ANY,
        scratch_shapes=[pltpu.SemaphoreType.DMA((7,)), pltpu.SemaphoreType.DMA((7,)), pltpu.SemaphoreType.DMA],
    )(xl)


def exchange_sibling(name, g4):
    _, _, r, c_ = g4.shape

    def body(g_ref, out_ref, send_sems, recv_sems):
        x, y, c = lax.axis_index("x"), lax.axis_index("y"), lax.axis_index("c")
        copies = [pltpu.make_async_remote_copy(
            src_ref=g_ref.at[q, 1 - c], dst_ref=out_ref.at[q], send_sem=send_sems.at[q], recv_sem=recv_sems.at[q],
            device_id=(x, y, 1 - c), device_id_type=_MESH) for q in range(4)]
        for cp in copies:
            cp.start()
        for cp in copies:
            cp.wait()

    return pl.pallas_call(
        body, name=name, out_shape=S((4, r, c_), g4.dtype), in_specs=[_ANY], out_specs=_ANY,
        scratch_shapes=[pltpu.SemaphoreType.DMA((4,)), pltpu.SemaphoreType.DMA((4,))],
    )(g4)


def exchange_chips(name, p4):
    _, r, c_ = p4.shape

    def body(p_ref, out_ref, send_sems, recv_sems):
        x, y, c = lax.axis_index("x"), lax.axis_index("y"), lax.axis_index("c")
        chips = [(1 - x, y), (x, 1 - y), (1 - x, 1 - y)]
        copies = [pltpu.make_async_remote_copy(
            src_ref=p_ref.at[2 * cx + cy], dst_ref=out_ref.at[k], send_sem=send_sems.at[k], recv_sem=recv_sems.at[k],
            device_id=(cx, cy, c), device_id_type=_MESH) for k, (cx, cy) in enumerate(chips)]
        for cp in copies:
            cp.start()
        for cp in copies:
            cp.wait()

    return pl.pallas_call(
        body, name=name, out_shape=S((3, r, c_), p4.dtype), in_specs=[_ANY], out_specs=_ANY,
        scratch_shapes=[pltpu.SemaphoreType.DMA((3,)), pltpu.SemaphoreType.DMA((3,))],
    )(p4)


def sum_with_sibling(name, g4, recv, core):
    _, _, r, c_ = g4.shape
    tr = _pick(r, (256, 128, 64, 32, 16, 8))

    def body(core_ref, g_ref, r_ref, o_ref, ox_ref):
        s = g_ref[0] + r_ref[...]
        o_ref[...] = s
        ox_ref[...] = s.astype(ox_ref.dtype)

    out_spec = pl.BlockSpec((1, tr, c_), lambda q, i, cr: (q, i, 0))
    return pl.pallas_call(
        body, name=name, out_shape=[S((4, r, c_), F32), S((4, r, c_), EXCHANGE_DTYPE)],
        grid_spec=pltpu.PrefetchScalarGridSpec(
            num_scalar_prefetch=1, grid=(4, r // tr),
            in_specs=[pl.BlockSpec((1, 1, tr, c_), lambda q, i, cr: (q, cr[0], i, 0)),
                      pl.BlockSpec((1, tr, c_), lambda q, i, cr: (q, i, 0))],
            out_specs=[out_spec, out_spec]),
    )(core, g4, recv)


def sum_with_chips(name, p4, recv, chip):
    _, r, c_ = p4.shape
    tr = _pick(r, (256, 128, 64, 32, 16, 8))

    def body(chip_ref, p_ref, r_ref, o_ref):
        o_ref[...] = ((p_ref[0] + r_ref[0].astype(F32)) + r_ref[1].astype(F32)) + r_ref[2].astype(F32)

    return pl.pallas_call(
        body, name=name, out_shape=S((r, c_), F32),
        grid_spec=pltpu.PrefetchScalarGridSpec(
            num_scalar_prefetch=1, grid=(r // tr,),
            in_specs=[pl.BlockSpec((1, tr, c_), lambda i, ch: (ch[0], i, 0)),
                      pl.BlockSpec((3, tr, c_), lambda i, ch: (0, i, 0))],
            out_specs=pl.BlockSpec((tr, c_), lambda i, ch: (i, 0))),
    )(chip, p4, recv)


def sum_devices(name, g8):
    _, r, c_ = g8.shape

    def body(g_ref, o_ref):
        acc = g_ref[0]
        for k in range(1, N_DEV):
            acc = acc + g_ref[k]
        o_ref[...] = acc

    return pl.pallas_call(body, name=name, out_shape=S((r, c_), F32))(g8)


def _flatten(parts, cols, row_mult):
    flat = jnp.concatenate([v.reshape(-1) for v in parts])
    n = flat.shape[0]
    rows = -(-n // (cols * row_mult)) * row_mult
    return jnp.pad(flat, (0, rows * cols - n)).reshape(rows, cols)


def _exchange_dims(name, lshape):
    l, r, c = lshape
    return (l, c, r) if name in TRANSPOSED else (l, r, c)


def gather_big(name, local):
    cols = local["e_w_out"].shape[2]
    parts = []
    for n in EXCHANGE_ORDER:
        v = local[n].astype(MXU_DTYPE)
        parts.append((v.transpose(0, 2, 1) if n in TRANSPOSED else v).reshape(-1, cols))
    got = all_gather(name, jnp.concatenate(parts, axis=0))
    full, r0 = {}, 0
    for n, part in zip(EXCHANGE_ORDER, parts):
        l, a, b = _exchange_dims(n, local[n].shape)
        seg = got[:, r0:r0 + part.shape[0]].reshape(N_DEV, l, a, b)
        full[n] = seg.transpose(1, 0, 2, 3).reshape(l, N_DEV * a, b)
        r0 += part.shape[0]
    return full


def reduce_scatter_big(grads, local_shapes):
    cols = local_shapes["e_w_out"][2]
    parts = []
    for n in EXCHANGE_ORDER:
        l, a, b = _exchange_dims(n, local_shapes[n])
        parts.append(grads[n].reshape(l, N_DEV, a, b).transpose(1, 0, 2, 3).reshape(N_DEV, -1, cols))
    n_rows = sum(pt.shape[1] for pt in parts)
    rows = -(-n_rows // 256) * 256
    if rows > n_rows:
        parts.append(jnp.zeros((N_DEV, rows - n_rows, cols), F32))
    g4 = jnp.concatenate(parts, axis=1).reshape(4, 2, rows, cols)
    x, y, c = lax.axis_index("x"), lax.axis_index("y"), lax.axis_index("c")
    from_sibling = exchange_sibling("rs_sibling", g4)
    p4, p4x = sum_with_sibling("rs_sum_sibling", g4, from_sibling, c.astype(jnp.int32).reshape(1))
    from_chips = exchange_chips("rs_chips", p4x)
    mine = sum_with_chips("rs_sum_chips", p4, from_chips, (2 * x + y).astype(jnp.int32).reshape(1))
    out, r0 = {}, 0
    for n, part in zip(EXCHANGE_ORDER, parts):
        l, a, b = _exchange_dims(n, local_shapes[n])
        seg = mine[r0:r0 + part.shape[1]].reshape(l, a, b)
        out[n] = seg.transpose(0, 2, 1) if n in TRANSPOSED else seg
        r0 += part.shape[1]
    return out


def gather_small(name, local, names):
    flat = _flatten([local[n] for n in names], LANE, 1)
    got = all_gather(name, flat).reshape(N_DEV, -1)
    full, off = {}, 0
    for n in names:
        size = math.prod(local[n].shape)
        seg = got[:, off:off + size].reshape((N_DEV,) + local[n].shape)
        full[n] = seg.transpose(1, 2, 0, 3).reshape(seg.shape[1], seg.shape[2], -1)
        off += size
    return full


def all_reduce_small(grads, names):
    flat = _flatten([grads[n] for n in names], LANE, SUBLANE)
    total = sum_devices("ar_sum", all_gather("ar_gather", flat)).reshape(-1)
    out, off = {}, 0
    for nm in names:
        size = math.prod(grads[nm].shape)
        out[nm] = total[off:off + size].reshape(grads[nm].shape)
        off += size
    return out


def adamw(name, w, g, m, v):
    shape = w.shape
    cols = shape[-1]
    rows = math.prod(shape[:-1])
    tr = _pick(rows, (256, 128, 64, 32, 16, 8)) if rows * cols > 256 * 1024 else rows
    c1 = 1.0 - ADAM_B1 ** ADAM_STEP
    c2 = 1.0 - ADAM_B2 ** ADAM_STEP

    def body(w_ref, g_ref, m_ref, v_ref, d_ref, nm_ref, nv_ref):
        gv = g_ref[...]
        m2 = ADAM_B1 * m_ref[...] + (1.0 - ADAM_B1) * gv
        v2 = ADAM_B2 * v_ref[...] + (1.0 - ADAM_B2) * jnp.square(gv)
        d_ref[...] = -ADAM_LR * ((m2 / c1) / (jnp.sqrt(v2 / c2) + ADAM_EPS) + ADAM_WD * w_ref[...])
        nm_ref[...] = m2
        nv_ref[...] = v2

    spec = pl.BlockSpec((tr, cols), lambda i: (i, 0))
    outs = pl.pallas_call(
        body, name=name, grid=(rows // tr,), in_specs=[spec] * 4, out_specs=[spec] * 3,
        out_shape=[S((rows, cols), F32)] * 3,
        compiler_params=pltpu.CompilerParams(dimension_semantics=("parallel",)),
    )(*[a.reshape(rows, cols) for a in (w, g, m, v)])
    return tuple(o.reshape(shape) for o in outs)


def kernel(x, p, e_w_in, e_conv_a_w, e_conv_a_b, e_ln_a_g, e_ln_a_b, e_conv_b_w, e_conv_b_b, e_dt_bias, e_a_log, e_d_skip, e_norm_b_g, e_w_out, o_w_in, o_conv_w, o_w_out, f_w_up, f_conv_w, f_conv_b, f_w_down, ple_w_proj, ple_w_gate, ln_g, ln_b, loss_target, m_e_w_in, m_e_conv_a_w, m_e_conv_a_b, m_e_ln_a_g, m_e_ln_a_b, m_e_conv_b_w, m_e_conv_b_b, m_e_dt_bias, m_e_a_log, m_e_d_skip, m_e_norm_b_g, m_e_w_out, m_o_w_in, m_o_conv_w, m_o_w_out, m_f_w_up, m_f_conv_w, m_f_conv_b, m_f_w_down, m_ple_w_proj, m_ple_w_gate, m_ln_g, m_ln_b, v_e_w_in, v_e_conv_a_w, v_e_conv_a_b, v_e_ln_a_g, v_e_ln_a_b, v_e_conv_b_w, v_e_conv_b_b, v_e_dt_bias, v_e_a_log, v_e_d_skip, v_e_norm_b_g, v_e_w_out, v_o_w_in, v_o_conv_w, v_o_w_out, v_f_w_up, v_f_conv_w, v_f_conv_b, v_f_w_down, v_ple_w_proj, v_ple_w_gate, v_ln_g, v_ln_b):
    args = locals()
    local = {n: args[n] for n in WEIGHTS}
    mom = {n: args["m_" + n] for n in WEIGHTS}
    var = {n: args["v_" + n] for n in WEIGHTS}

    full = {n: local[n] for n in REPLICATED}
    full.update(gather_big("ag_big", local))
    full.update(gather_small("ag_small", local, SMALL_SHARDED))

    loss_local, grad_x, grads = local_step(x[0], p[:, 0], full, loss_target[0])
    loss = lax.psum(loss_local, MESH_AXES)

    g_local = reduce_scatter_big(grads, {n: local[n].shape for n in BIG})
    small = all_reduce_small(grads, REPLICATED + SMALL_SHARDED)
    dev = 4 * lax.axis_index("x") + 2 * lax.axis_index("y") + lax.axis_index("c")
    for n in REPLICATED:
        g_local[n] = small[n]
    for n in SMALL_SHARDED:
        width = local[n].shape[2]
        g_local[n] = lax.dynamic_slice_in_dim(small[n], dev * width, width, axis=2)

    delta, new_m, new_v = {}, {}, {}
    for n in WEIGHTS:
        delta[n], new_m[n], new_v[n] = adamw("adamw_" + n, local[n], g_local[n], mom[n], var[n])
    return (loss, grad_x[None], *[g_local[n] for n in WEIGHTS], *[delta[n] for n in WEIGHTS],
            *[new_m[n] for n in WEIGHTS], *[new_v[n] for n in WEIGHTS])
```

```python
import functools
import math

import jax
import jax.numpy as jnp
from jax import lax
from jax.experimental import pallas as pl
from jax.experimental.pallas import tpu as pltpu

F32 = jnp.float32
MXU_DTYPE = jnp.bfloat16
MESH_AXES = ("x", "y", "c")
N_DEV = 8
LANE = 128
SUBLANE = 8
ROWWISE_VMEM_BUDGET = 20 * 1024 * 1024
MM_TILES = (1408, 1024, 512, 256, 128)
EXCHANGE_DTYPE = jnp.bfloat16
LN_EPS = 1e-5
CHUNK = 64
HEAD_DIM = 64
N_GROUPS = 4
N_STATE = 128
CONV_PAD = 32
CONV_ROWS = 256
ADAM_LR, ADAM_B1, ADAM_B2, ADAM_EPS, ADAM_WD, ADAM_STEP = 0.001, 0.9, 0.999, 1e-08, 0.01, 10

BIG = ("e_w_in", "e_w_out", "o_w_in", "o_w_out", "f_w_up", "f_w_down", "ple_w_proj", "ple_w_gate")
SMALL_SHARDED = ("e_conv_a_w", "e_conv_b_w", "o_conv_w", "f_conv_w", "ln_g", "ln_b")
REPLICATED = ("e_conv_a_b", "e_ln_a_g", "e_ln_a_b", "e_conv_b_b", "e_dt_bias", "e_a_log", "e_d_skip",
              "e_norm_b_g", "f_conv_b")
TRANSPOSED = ("e_w_in", "o_w_in", "f_w_up", "ple_w_proj")
EXCHANGE_ORDER = ("e_w_out", "o_w_out", "ple_w_gate", "ple_w_proj", "o_w_in", "f_w_down", "f_w_up", "e_w_in")
WEIGHTS = ("e_w_in", "e_conv_a_w", "e_conv_a_b", "e_ln_a_g", "e_ln_a_b", "e_conv_b_w", "e_conv_b_b", "e_dt_bias",
           "e_a_log", "e_d_skip", "e_norm_b_g", "e_w_out", "o_w_in", "o_conv_w", "o_w_out", "f_w_up", "f_conv_w",
           "f_conv_b", "f_w_down", "ple_w_proj", "ple_w_gate", "ln_g", "ln_b")

S = jax.ShapeDtypeStruct


class Win:
    def __init__(self, arr, w=None, idx=0, coef=1.0):
        self.arr, self.w, self.idx, self.coef = arr, (arr.shape[1] if w is None else w), idx, coef


def _win(a):
    return a if isinstance(a, Win) else Win(a)


def _pick(n, prefs):
    for p in prefs:
        if p <= n and n % p == 0:
            return p
    return n


_MM_DIMS = {"nn": (1, 0), "nt": (1, 1), "tn": (0, 0)}


def mm(name, a, b, mode, out_dtype=F32):
    ca, cb = _MM_DIMS[mode]
    kdim = a.shape[ca]
    m = a.shape[1 - ca]
    n = b.shape[1 - cb]
    assert b.shape[cb] == kdim, (name, a.shape, b.shape, mode)
    tm = _pick(m, MM_TILES)
    tn = _pick(n, MM_TILES)
    tk = kdim if kdim <= MM_TILES[0] else _pick(kdim, MM_TILES)
    nk = kdim // tk
    own_acc = nk > 1 and out_dtype != F32

    def body(a_ref, b_ref, o_ref, *scratch):
        acc_ref = scratch[0] if own_acc else o_ref
        d = lax.dot_general(a_ref[...].astype(MXU_DTYPE), b_ref[...].astype(MXU_DTYPE),
                            (((ca,), (cb,)), ((), ())), preferred_element_type=F32)
        if nk == 1:
            o_ref[...] = d.astype(o_ref.dtype)
        else:
            k = pl.program_id(2)

            @pl.when(k == 0)
            def _():
                acc_ref[...] = d

            @pl.when(k > 0)
            def _():
                acc_ref[...] += d

            if own_acc:
                @pl.when(k == nk - 1)
                def _():
                    o_ref[...] = acc_ref[...].astype(o_ref.dtype)

    a_spec = pl.BlockSpec((tm, tk), lambda i, j, k: (i, k)) if ca == 1 else pl.BlockSpec((tk, tm), lambda i, j, k: (k, i))
    b_spec = pl.BlockSpec((tk, tn), lambda i, j, k: (k, j)) if cb == 0 else pl.BlockSpec((tn, tk), lambda i, j, k: (j, k))
    return pl.pallas_call(
        body, name=name, grid=(m // tm, n // tn, nk),
        in_specs=[a_spec, b_spec], out_specs=pl.BlockSpec((tm, tn), lambda i, j, k: (i, j)),
        out_shape=S((m, n), out_dtype), scratch_shapes=[pltpu.VMEM((tm, tn), F32)] if own_acc else [],
        compiler_params=pltpu.CompilerParams(dimension_semantics=("parallel", "parallel", "arbitrary")),
    )(a, b)


def _row_block(t, widths):
    tb = 512
    while tb > SUBLANE and (t % tb or tb * sum(widths) * 8 > ROWWISE_VMEM_BUDGET):
        tb //= 2
    return tb


def _tok_spec(tb, w):
    return pl.BlockSpec((tb, w.w), functools.partial(lambda i, idx: (i, idx), idx=w.idx))


def _par_spec(p):
    return pl.BlockSpec((1, p.shape[1]), lambda i: (0, 0))


def rowwise(name, fn, tok, par, out_widths, red_widths=(), out_dtypes=None):
    tok = [_win(t) for t in tok]
    t = tok[0].arr.shape[0]
    tb = _row_block(t, [w.w for w in tok] + list(out_widths))
    n_tok, n_par, n_out = len(tok), len(par), len(out_widths)
    out_dtypes = [F32] * n_out if out_dtypes is None else out_dtypes

    def body(*refs):
        ins = [r[...] for r in refs[:n_tok + n_par]]
        res = fn(*ins)
        out_refs = refs[n_tok + n_par:n_tok + n_par + n_out]
        red_refs = refs[n_tok + n_par + n_out:]
        for r, v in zip(out_refs, res[:n_out]):
            r[...] = v.astype(r.dtype)
        if red_refs:
            @pl.when(pl.program_id(0) == 0)
            def _():
                for r in red_refs:
                    r[...] = jnp.zeros_like(r)
            for r, v in zip(red_refs, res[n_out:]):
                r[...] += v

    outs = pl.pallas_call(
        body, name=name, grid=(t // tb,),
        in_specs=[_tok_spec(tb, w) for w in tok] + [_par_spec(p) for p in par],
        out_specs=[pl.BlockSpec((tb, w), lambda i: (i, 0)) for w in out_widths]
        + [pl.BlockSpec((1, w), lambda i: (0, 0)) for w in red_widths],
        out_shape=[S((t, w), dt) for w, dt in zip(out_widths, out_dtypes)] + [S((1, w), F32) for w in red_widths],
        compiler_params=pltpu.CompilerParams(dimension_semantics=("arbitrary",)),
    )(*[w.arr for w in tok], *par)
    return outs


def rowwise_bwd(name, fn, tok, par, cts, d_dtypes=None):
    tok = [_win(t) for t in tok]
    cts = [[_win(c) for c in group] for group in cts]
    d_dtypes = [(F32,)] * len(tok) if d_dtypes is None else d_dtypes
    t = tok[0].arr.shape[0]
    flat_cts = [c for group in cts for c in group]
    d_outs = [(i, w.w, dt) for i, (w, dts) in enumerate(zip(tok, d_dtypes)) for dt in dts]
    tb = _row_block(t, [w.w for w in tok] + [c.w for c in flat_cts] + [w for _, w, _ in d_outs])
    n_tok, n_par, n_ct, n_d = len(tok), len(par), len(flat_cts), len(d_outs)

    def body(*refs):
        tok_vals = [r[...] for r in refs[:n_tok]]
        par_vals = [r[...] for r in refs[n_tok:n_tok + n_par]]
        ct_refs = refs[n_tok + n_par:n_tok + n_par + n_ct]
        d_refs = refs[n_tok + n_par + n_ct:n_tok + n_par + n_ct + n_d]
        dp_refs = refs[n_tok + n_par + n_ct + n_d:]
        ct_vals, pos = [], 0
        for group in cts:
            acc = None
            for c in group:
                v = ct_refs[pos][...]
                if c.coef != 1.0:
                    v = v * c.coef
                acc = v if acc is None else acc + v
                pos += 1
            ct_vals.append(acc)
        _, vjp = jax.vjp(lambda *a: tuple(fn(*a)), *tok_vals, *par_vals)
        grads = vjp(tuple(ct_vals))
        for r, (i, _, _) in zip(d_refs, d_outs):
            r[...] = grads[i].astype(r.dtype)
        if dp_refs:
            @pl.when(pl.program_id(0) == 0)
            def _():
                for r in dp_refs:
                    r[...] = jnp.zeros_like(r)
            for r, v in zip(dp_refs, grads[n_tok:]):
                r[...] += v

    outs = pl.pallas_call(
        body, name=name, grid=(t // tb,),
        in_specs=[_tok_spec(tb, w) for w in tok] + [_par_spec(p) for p in par] + [_tok_spec(tb, c) for c in flat_cts],
        out_specs=[pl.BlockSpec((tb, w), lambda i: (i, 0)) for _, w, _ in d_outs] + [_par_spec(p) for p in par],
        out_shape=[S((t, w), dt) for _, w, dt in d_outs] + [S(p.shape, F32) for p in par],
        compiler_params=pltpu.CompilerParams(dimension_semantics=("arbitrary",)),
    )(*[w.arr for w in tok], *par, *[c.arr for c in flat_cts])
    return outs[:n_d], outs[n_d:]


def _sigmoid(x):
    return 1.0 / (1.0 + jnp.exp(-x))


def _silu(x):
    return x * _sigmoid(x)


def _ln(x, g, b):
    mu = jnp.mean(x, axis=-1, keepdims=True)
    var = jnp.mean(jnp.square(x - mu), axis=-1, keepdims=True)
    return (x - mu) * lax.rsqrt(var + LN_EPS) * g + b


def f_glu(ua, ug):
    return (ua * _sigmoid(ug),)


def f_ln_silu(a1, g, b):
    return (_silu(_ln(a1, g, b)),)


def f_silu3(a, b, c):
    return (_silu(a), _silu(b), _silu(c))


def f_softplus(dt_raw, bias):
    return (jax.nn.softplus(dt_raw + bias),)


def f_gate_rms(yssd, z, g):
    y = yssd * _silu(z)
    return (y * lax.rsqrt(jnp.mean(jnp.square(y), axis=-1, keepdims=True) + LN_EPS) * g,)


def f_ln(pre, g, b):
    return (_ln(pre, g, b),)


def f_mul(a, b):
    return (a * b,)


def f_gate_mul(pp, gt):
    return (pp * _sigmoid(gt),)


def conv_fwd(name, x, w, b):
    x = _win(x)
    t, c = x.arr.shape[0], x.w
    kw = w.shape[0]
    cb = LANE
    off = x.idx * (c // cb)
    rows = min(CONV_ROWS, t)
    has_b = b is not None

    def body(*refs):
        if has_b:
            x_ref, w_ref, b_ref, y_ref, xp_ref = refs
        else:
            x_ref, w_ref, y_ref, xp_ref = refs
        xp_ref[0:CONV_PAD, :] = jnp.zeros((CONV_PAD, cb), F32)
        xp_ref[CONV_PAD:CONV_PAD + t, :] = x_ref[...]

        def step(s, carry):
            base = pl.multiple_of(s * rows, rows)
            acc = jnp.zeros((rows, cb), F32)
            if has_b:
                acc = acc + b_ref[...]
            for k in range(kw):
                acc = acc + w_ref[k:k + 1, :] * xp_ref[pl.ds(base + CONV_PAD - (kw - 1) + k, rows), :]
            y_ref[pl.ds(base, rows), :] = acc
            return carry

        lax.fori_loop(0, t // rows, step, 0)

    in_specs = [pl.BlockSpec((t, cb), lambda j: (0, off + j)), pl.BlockSpec((kw, cb), lambda j: (0, j))]
    args = [x.arr, w]
    if has_b:
        in_specs.append(pl.BlockSpec((1, cb), lambda j: (0, j)))
        args.append(b)
    return pl.pallas_call(
        body, name=name, grid=(c // cb,), in_specs=in_specs,
        out_specs=pl.BlockSpec((t, cb), lambda j: (0, j)), out_shape=S((t, c), F32),
        scratch_shapes=[pltpu.VMEM((CONV_PAD + t, cb), F32)],
        compiler_params=pltpu.CompilerParams(dimension_semantics=("parallel",)),
    )(*args)


def conv_bwd(name, x, dy, w, dx_dtype=F32):
    x, dy = _win(x), _win(dy)
    t, c = x.arr.shape[0], x.w
    kw = w.shape[0]
    cb = LANE
    xoff = x.idx * (c // cb)
    dyoff = dy.idx * (c // cb)
    rows = min(CONV_ROWS, t)

    def body(x_ref, dy_ref, w_ref, dx_ref, dw_ref, db_ref, xp_ref, dyp_ref):
        xp_ref[0:CONV_PAD, :] = jnp.zeros((CONV_PAD, cb), F32)
        xp_ref[CONV_PAD:CONV_PAD + t, :] = x_ref[...]
        dyp_ref[0:t, :] = dy_ref[...]
        dyp_ref[t:t + CONV_PAD, :] = jnp.zeros((CONV_PAD, cb), F32)

        def fold(v):
            return jnp.sum(v.reshape(rows // SUBLANE, SUBLANE, cb), axis=0)

        def step(s, carry):
            base = pl.multiple_of(s * rows, rows)
            dyc = dy_ref[pl.ds(base, rows), :]
            acc = jnp.zeros((rows, cb), F32)
            new = []
            for k in range(kw):
                acc = acc + w_ref[k:k + 1, :] * dyp_ref[pl.ds(base + (kw - 1) - k, rows), :]
                new.append(carry[k] + fold(dyc * xp_ref[pl.ds(base + CONV_PAD - (kw - 1) + k, rows), :]))
            new.append(carry[kw] + fold(dyc))
            dx_ref[pl.ds(base, rows), :] = acc.astype(dx_ref.dtype)
            return tuple(new)

        init = tuple(jnp.zeros((SUBLANE, cb), F32) for _ in range(kw + 1))
        parts = lax.fori_loop(0, t // rows, step, init)
        for k in range(kw):
            dw_ref[k:k + 1, :] = jnp.sum(parts[k], axis=0, keepdims=True)
        db_ref[...] = jnp.sum(parts[kw], axis=0, keepdims=True)

    return pl.pallas_call(
        body, name=name, grid=(c // cb,),
        in_specs=[pl.BlockSpec((t, cb), lambda j: (0, xoff + j)), pl.BlockSpec((t, cb), lambda j: (0, dyoff + j)),
                  pl.BlockSpec((kw, cb), lambda j: (0, j))],
        out_specs=[pl.BlockSpec((t, cb), lambda j: (0, j)), pl.BlockSpec((kw, cb), lambda j: (0, j)),
                   pl.BlockSpec((1, cb), lambda j: (0, j))],
        out_shape=[S((t, c), dx_dtype), S((kw, c), F32), S((1, c), F32)],
        scratch_shapes=[pltpu.VMEM((CONV_PAD + t, cb), F32), pltpu.VMEM((CONV_PAD + t, cb), F32)],
        compiler_params=pltpu.CompilerParams(dimension_semantics=("parallel",)),
    )(x.arr, dy.arr, w)


def gated_conv_fwd(name, hpre, w, b, out_dtype):
    t, c2 = hpre.shape
    ff = c2 // 2
    kw = w.shape[0]
    cb = LANE
    nb = ff // cb
    rows = min(CONV_ROWS, t)

    def body(h1_ref, h2_ref, w1_ref, w2_ref, b1_ref, b2_ref, y_ref, xp1_ref, xp2_ref):
        for xp_ref, h_ref in ((xp1_ref, h1_ref), (xp2_ref, h2_ref)):
            xp_ref[0:CONV_PAD, :] = jnp.zeros((CONV_PAD, cb), F32)
            xp_ref[CONV_PAD:CONV_PAD + t, :] = h_ref[...]

        def step(s, carry):
            base = pl.multiple_of(s * rows, rows)
            h1 = jnp.zeros((rows, cb), F32) + b1_ref[...]
            h2 = jnp.zeros((rows, cb), F32) + b2_ref[...]
            for k in range(kw):
                at = pl.ds(base + CONV_PAD - (kw - 1) + k, rows)
                h1 = h1 + w1_ref[k:k + 1, :] * xp1_ref[at, :]
                h2 = h2 + w2_ref[k:k + 1, :] * xp2_ref[at, :]
            y_ref[pl.ds(base, rows), :] = (_silu(h1) * h2).astype(y_ref.dtype)
            return carry

        lax.fori_loop(0, t // rows, step, 0)

    col1 = lambda r: pl.BlockSpec((r, cb), lambda j: (0, j))
    col2 = lambda r: pl.BlockSpec((r, cb), lambda j: (0, nb + j))
    return pl.pallas_call(
        body, name=name, grid=(nb,),
        in_specs=[col1(t), col2(t), col1(kw), col2(kw), col1(1), col2(1)],
        out_specs=col1(t), out_shape=S((t, ff), out_dtype),
        scratch_shapes=[pltpu.VMEM((CONV_PAD + t, cb), F32)] * 2,
        compiler_params=pltpu.CompilerParams(dimension_semantics=("parallel",)),
    )(hpre, hpre, w, w, b, b)


def gated_conv_bwd(name, hpre, dact, w, b, dx_dtype):
    t, c2 = hpre.shape
    ff = c2 // 2
    kw = w.shape[0]
    cb = LANE
    nb = ff // cb
    rows = min(CONV_ROWS, t)

    def body(own_ref, oth_ref, da_ref, wo_ref, wt_ref, bo_ref, bt_ref, dx_ref, dw_ref, db_ref,
             xpo_ref, xpt_ref, dhp_ref):
        for xp_ref, h_ref in ((xpo_ref, own_ref), (xpt_ref, oth_ref)):
            xp_ref[0:CONV_PAD, :] = jnp.zeros((CONV_PAD, cb), F32)
            xp_ref[CONV_PAD:CONV_PAD + t, :] = h_ref[...]
        dhp_ref[t:t + CONV_PAD, :] = jnp.zeros((CONV_PAD, cb), F32)

        def fold(v):
            return jnp.sum(v.reshape(rows // SUBLANE, SUBLANE, cb), axis=0)

        def first_pass(own_is_gate):
            def step(s, carry):
                base = pl.multiple_of(s * rows, rows)
                ho = jnp.zeros((rows, cb), F32) + bo_ref[...]
                ht = jnp.zeros((rows, cb), F32) + bt_ref[...]
                for k in range(kw):
                    at = pl.ds(base + CONV_PAD - (kw - 1) + k, rows)
                    ho = ho + wo_ref[k:k + 1, :] * xpo_ref[at, :]
                    ht = ht + wt_ref[k:k + 1, :] * xpt_ref[at, :]
                da = da_ref[pl.ds(base, rows), :]
                if own_is_gate:
                    sg = _sigmoid(ho)
                    dh = da * ht * (sg * (1.0 + ho * (1.0 - sg)))
                else:
                    dh = da * _silu(ht)
                dhp_ref[pl.ds(base, rows), :] = dh
                new = [carry[k] + fold(dh * xpo_ref[pl.ds(base + CONV_PAD - (kw - 1) + k, rows), :]) for k in range(kw)]
                new.append(carry[kw] + fold(dh))
                return tuple(new)

            init = tuple(jnp.zeros((SUBLANE, cb), F32) for _ in range(kw + 1))
            parts = lax.fori_loop(0, t // rows, step, init)
            for k in range(kw):
                dw_ref[k:k + 1, :] = jnp.sum(parts[k], axis=0, keepdims=True)
            db_ref[...] = jnp.sum(parts[kw], axis=0, keepdims=True)

        half = pl.program_id(0)

        @pl.when(half == 0)
        def _():
            first_pass(True)

        @pl.when(half == 1)
        def _():
            first_pass(False)

        def second(s, carry):
            base = pl.multiple_of(s * rows, rows)
            acc = jnp.zeros((rows, cb), F32)
            for k in range(kw):
                acc = acc + wo_ref[k:k + 1, :] * dhp_ref[pl.ds(base + (kw - 1) - k, rows), :]
            dx_ref[pl.ds(base, rows), :] = acc.astype(dx_ref.dtype)
            return carry

        lax.fori_loop(0, t // rows, second, 0)

    own = lambda r: pl.BlockSpec((r, cb), lambda h, j: (0, h * nb + j))
    oth = lambda r: pl.BlockSpec((r, cb), lambda h, j: (0, (1 - h) * nb + j))
    return pl.pallas_call(
        body, name=name, grid=(2, nb),
        in_specs=[own(t), oth(t), pl.BlockSpec((t, cb), lambda h, j: (0, j)), own(kw), oth(kw), own(1), oth(1)],
        out_specs=[own(t), own(kw), own(1)],
        out_shape=[S((t, c2), dx_dtype), S((kw, c2), F32), S((1, c2), F32)],
        scratch_shapes=[pltpu.VMEM((CONV_PAD + t, cb), F32)] * 3,
        compiler_params=pltpu.CompilerParams(dimension_semantics=("parallel", "parallel")),
    )(hpre, hpre, dact, w, w, b, b)


def _bdot(a, b, ca, cb):
    return lax.dot_general(a.astype(MXU_DTYPE), b.astype(MXU_DTYPE), (((ca,), (cb,)), ((0,), (0,))),
                           preferred_element_type=F32)


@jax.custom_vjp
def bmm_nn(a, b):
    return _bdot(a, b, 2, 1)


bmm_nn.defvjp(lambda a, b: (_bdot(a, b, 2, 1), (a, b)),
              lambda r, g: (_bdot(g, r[1], 2, 2), _bdot(r[0], g, 1, 1)))


@jax.custom_vjp
def bmm_tn(a, b):
    return _bdot(a, b, 1, 1)


bmm_tn.defvjp(lambda a, b: (_bdot(a, b, 1, 1), (a, b)),
              lambda r, g: (_bdot(r[1], g, 2, 2), _bdot(r[0], g, 2, 1)))


@jax.custom_vjp
def bmm_nt(a, b):
    return _bdot(a, b, 2, 2)


bmm_nt.defvjp(lambda a, b: (_bdot(a, b, 2, 2), (a, b)),
              lambda r, g: (_bdot(g, r[1], 2, 1), _bdot(g, r[0], 1, 1)))


def ssd_chunk(x, dt, bm, cm, hprev, a_log, dsk):
    hg, ln, _ = x.shape
    n = bm.shape[1]
    ii = lax.broadcasted_iota(jnp.int32, (ln, ln), 0)
    jj = lax.broadcasted_iota(jnp.int32, (ln, ln), 1)
    tril, eye, triu = (ii >= jj)[None], (ii == jj)[None], (ii <= jj)[None]
    da = dt * (-jnp.exp(a_log))
    da_row = jnp.sum(jnp.where(eye, da, 0.0), axis=1, keepdims=True)
    dt_row = jnp.sum(jnp.where(eye, dt, 0.0), axis=1, keepdims=True)
    cum_c = jnp.sum(jnp.where(tril, da_row, 0.0), axis=2, keepdims=True)
    cum_r = jnp.sum(jnp.where(triu, da, 0.0), axis=1, keepdims=True)
    last = jnp.sum(da, axis=1, keepdims=True)
    decay = jnp.where(tril, jnp.exp(jnp.where(tril, cum_c - cum_r, 0.0)), 0.0)
    cb = bmm_nt(cm[None], bm[None])
    y_diag = bmm_nn(cb * decay * dt_row, x)
    bb = jnp.broadcast_to(bm[None], (hg, ln, n))
    cc = jnp.broadcast_to(cm[None], (hg, ln, n))
    states = bmm_tn(x * (jnp.exp(last - cum_c) * dt), bb)
    y_off = bmm_nt(cc, hprev) * jnp.exp(cum_c)
    hnew = hprev * jnp.exp(last) + states
    return y_diag + y_off + dsk * x, hnew


def _ssd_dims(xh, bm):
    h, t, p = xh.shape
    hg = h // N_GROUPS
    n = bm.shape[1] // N_GROUPS
    return h, t, p, hg, n, t // CHUNK


def ssd_fwd(name, xh, dth, bm, cm, a_log, dsk):
    h, t, p, hg, n, nc = _ssd_dims(xh, bm)

    def body(al_ref, dk_ref, x_ref, dt_ref, b_ref, c_ref, y_ref, hp_ref, h_scr):
        @pl.when(pl.program_id(0) == 0)
        def _():
            h_scr[...] = jnp.zeros_like(h_scr)

        for g in range(N_GROUPS):
            hs, ns = slice(g * hg, (g + 1) * hg), slice(g * n, (g + 1) * n)
            hprev = h_scr[hs]
            hp_ref[hs, 0] = hprev
            y, hnew = ssd_chunk(x_ref[hs], dt_ref[hs], b_ref[:, ns], c_ref[:, ns], hprev, al_ref[hs], dk_ref[hs])
            y_ref[hs] = y
            h_scr[hs] = hnew

    head = pl.BlockSpec((h, 1, 1), lambda c: (0, 0, 0))
    return pl.pallas_call(
        body, name=name, grid=(nc,),
        in_specs=[head, head, pl.BlockSpec((h, CHUNK, p), lambda c: (0, c, 0)),
                  pl.BlockSpec((h, CHUNK, 1), lambda c: (0, c, 0)),
                  pl.BlockSpec((CHUNK, N_GROUPS * n), lambda c: (c, 0)),
                  pl.BlockSpec((CHUNK, N_GROUPS * n), lambda c: (c, 0))],
        out_specs=[pl.BlockSpec((h, CHUNK, p), lambda c: (0, c, 0)),
                   pl.BlockSpec((h, 1, p, n), lambda c: (0, c, 0, 0))],
        out_shape=[S((h, t, p), F32), S((h, nc, p, n), F32)],
        scratch_shapes=[pltpu.VMEM((h, p, n), F32)],
        compiler_params=pltpu.CompilerParams(dimension_semantics=("arbitrary",)),
    )(a_log, dsk, xh, dth, bm, cm)


def ssd_bwd(name, xh, dth, bm, cm, a_log, dsk, hp, dyh):
    h, t, p, hg, n, nc = _ssd_dims(xh, bm)

    def body(al_ref, dk_ref, x_ref, dt_ref, b_ref, c_ref, hp_ref, dy_ref,
             dx_ref, ddt_ref, db_ref, dc_ref, dal_ref, ddk_ref, dh_scr):
        @pl.when(pl.program_id(0) == 0)
        def _():
            dh_scr[...] = jnp.zeros_like(dh_scr)
            dal_ref[...] = jnp.zeros_like(dal_ref)
            ddk_ref[...] = jnp.zeros_like(ddk_ref)

        for g in range(N_GROUPS):
            hs, ns = slice(g * hg, (g + 1) * hg), slice(g * n, (g + 1) * n)
            _, vjp = jax.vjp(ssd_chunk, x_ref[hs], dt_ref[hs], b_ref[:, ns], c_ref[:, ns], hp_ref[hs, 0],
                             al_ref[hs], dk_ref[hs])
            gx, gdt, gb, gc, ghp, gal, gdk = vjp((dy_ref[hs], dh_scr[hs]))
            dx_ref[hs] = gx
            ddt_ref[hs] = gdt
            db_ref[:, ns] = gb
            dc_ref[:, ns] = gc
            dh_scr[hs] = ghp
            dal_ref[hs] += gal
            ddk_ref[hs] += gdk

    head = pl.BlockSpec((h, 1, 1), lambda c: (0, 0, 0))
    xs = pl.BlockSpec((h, CHUNK, p), lambda c: (0, nc - 1 - c, 0))
    ds = pl.BlockSpec((h, CHUNK, 1), lambda c: (0, nc - 1 - c, 0))
    bs = pl.BlockSpec((CHUNK, N_GROUPS * n), lambda c: (nc - 1 - c, 0))
    return pl.pallas_call(
        body, name=name, grid=(nc,),
        in_specs=[head, head, xs, ds, bs, bs, pl.BlockSpec((h, 1, p, n), lambda c: (0, nc - 1 - c, 0, 0)), xs],
        out_specs=[xs, ds, bs, bs, head, head],
        out_shape=[S((h, t, p), F32), S((h, t, 1), F32), S(bm.shape, F32), S(cm.shape, F32),
                   S((h, 1, 1), F32), S((h, 1, 1), F32)],
        scratch_shapes=[pltpu.VMEM((h, p, n), F32)],
        compiler_params=pltpu.CompilerParams(dimension_semantics=("arbitrary",)),
    )(a_log, dsk, xh, dth, bm, cm, hp, dyh)


def to_heads(v):
    t = v.shape[0]
    return v.reshape(t, -1, HEAD_DIM).transpose(1, 0, 2)


def from_heads(v):
    return v.transpose(1, 0, 2).reshape(v.shape[1], -1)


def _alpha(depth):
    return (2.0 * depth) ** 0.25


def _pad_lanes(v):
    return jnp.pad(v, ((0, 0), (0, LANE - v.shape[1])))


def split_even_weights(w, j):
    d = w["e_w_in"].shape[2]
    da = w["e_conv_a_w"].shape[2]
    db = w["e_norm_b_g"].shape[1]
    gn = N_GROUPS * N_STATE
    nh = w["e_dt_bias"].shape[1]
    main = 2 * da + 2 * db + 2 * gn
    win = w["e_w_in"][j]
    ox = 2 * da + db
    cw, cbias = w["e_conv_b_w"][j], w["e_conv_b_b"][j][None]
    return dict(
        d=d, da=da, db=db, gn=gn, nh=nh, main=main,
        win_main=win[:main], win_dt=jnp.pad(win[main:], ((0, LANE - nh), (0, 0))),
        caw=w["e_conv_a_w"][j], cab=w["e_conv_a_b"][j][None], lag=w["e_ln_a_g"][j][None], lab=w["e_ln_a_b"][j][None],
        cw_xs=cw[:, :db], cw_b=cw[:, db:db + gn], cw_c=cw[:, db + gn:],
        cb_xs=cbias[:, :db], cb_b=cbias[:, db:db + gn], cb_c=cbias[:, db + gn:],
        dt_bias=_pad_lanes(w["e_dt_bias"][j][None]), a_log=w["e_a_log"][j].reshape(nh, 1, 1),
        dsk=w["e_d_skip"][j].reshape(nh, 1, 1), norm_g=w["e_norm_b_g"][j][None],
        wout_a=w["e_w_out"][j][:da], wout_b=w["e_w_out"][j][da:],
    )


def even_fwd(tag, x, xm, lw, ln_g, ln_b, alpha):
    t = x.shape[0]
    da, db, gn, nh = lw["da"], lw["db"], lw["gn"], lw["nh"]
    u = mm(tag + "_win", xm, lw["win_main"], "nt")
    udt = mm(tag + "_windt", xm, lw["win_dt"], "nt")
    ua, ug, z, xs_pre = Win(u, da, 0), Win(u, da, 1), Win(u, db, 2 * da // db), Win(u, db, (2 * da + db) // db)
    b_pre, c_pre = Win(u, gn, (2 * da + 2 * db) // gn), Win(u, gn, (2 * da + 2 * db + gn) // gn)
    (a0,) = rowwise(tag + "_glu", f_glu, [ua, ug], [], [da])
    a1 = conv_fwd(tag + "_conva", a0, lw["caw"], lw["cab"])
    (ya,) = rowwise(tag + "_lna", f_ln_silu, [a1], [lw["lag"], lw["lab"]], [da], out_dtypes=[MXU_DTYPE])
    xs_c = conv_fwd(tag + "_convxs", xs_pre, lw["cw_xs"], lw["cb_xs"])
    b_c = conv_fwd(tag + "_convb", b_pre, lw["cw_b"], lw["cb_b"])
    c_c = conv_fwd(tag + "_convc", c_pre, lw["cw_c"], lw["cb_c"])
    xs, bm, cm = rowwise(tag + "_silu3", f_silu3, [xs_c, b_c, c_c], [], [db, gn, gn])
    (dt,) = rowwise(tag + "_dt", f_softplus, [udt], [lw["dt_bias"]], [LANE])
    xs_h = to_heads(xs)
    dt_h = dt[:, :nh].T.reshape(nh, t, 1)
    y_h, hp = ssd_fwd(tag + "_ssd", xs_h, dt_h, bm, cm, lw["a_log"], lw["dsk"])
    yssd = from_heads(y_h)
    (yb,) = rowwise(tag + "_gate", f_gate_rms, [yssd, z], [lw["norm_g"]], [db], out_dtypes=[MXU_DTYPE])
    ma = mm(tag + "_wouta", ya, lw["wout_a"], "nn")
    mb = mm(tag + "_woutb", yb, lw["wout_b"], "nn")

    def f_res(xv, mav, mbv, g, b):
        pre = alpha * xv + mav + mbv
        y = _ln(pre, g, b)
        return y, y, pre

    x1, x1m, pre = rowwise(tag + "_res", f_res, [x, ma, mb], [ln_g, ln_b], [x.shape[1]] * 3,
                           out_dtypes=[F32, MXU_DTYPE, F32])
    saved = dict(xm=xm, u=u, udt=udt, a0=a0, a1=a1, ya=ya, xs_c=xs_c, b_c=b_c, c_c=c_c, xs_h=xs_h, dt_h=dt_h, bm=bm,
                 cm=cm, hp=hp, yssd=yssd, yb=yb, pre=pre)
    return x1, x1m, saved


def even_bwd(tag, dx1_pieces, sv, lw, ln_g, ln_b, alpha):
    t = sv["u"].shape[0]
    da, db, gn, nh = lw["da"], lw["db"], lw["gn"], lw["nh"]
    u, xm = sv["u"], sv["xm"]
    mx = (MXU_DTYPE,)
    ua, ug, z, xs_pre = Win(u, da, 0), Win(u, da, 1), Win(u, db, 2 * da // db), Win(u, db, (2 * da + db) // db)
    b_pre, c_pre = Win(u, gn, (2 * da + 2 * db) // gn), Win(u, gn, (2 * da + 2 * db + gn) // gn)
    (dpre, dprem), (dg0, db0) = rowwise_bwd(tag + "_res_b", f_ln, [sv["pre"]], [ln_g, ln_b], [dx1_pieces],
                                            d_dtypes=[(F32, MXU_DTYPE)])
    dya = mm(tag + "_dya", dprem, lw["wout_a"], "nt")
    dyb = mm(tag + "_dyb", dprem, lw["wout_b"], "nt")
    dwout_a = mm(tag + "_dwouta", sv["ya"], dprem, "tn", EXCHANGE_DTYPE)
    dwout_b = mm(tag + "_dwoutb", sv["yb"], dprem, "tn", EXCHANGE_DTYPE)
    (dyssd, dz), (dnorm_g,) = rowwise_bwd(tag + "_gate_b", f_gate_rms, [sv["yssd"], z], [lw["norm_g"]], [[dyb]],
                                          d_dtypes=[(F32,), mx])
    dxs_h, ddt_h, dbm, dcm, dalog, ddsk = ssd_bwd(tag + "_ssd_b", sv["xs_h"], sv["dt_h"], sv["bm"], sv["cm"],
                                                  lw["a_log"], lw["dsk"], sv["hp"], to_heads(dyssd))
    dxs = from_heads(dxs_h)
    ddt = _pad_lanes(ddt_h.reshape(nh, t).T)
    (dudt,), (ddt_bias,) = rowwise_bwd(tag + "_dt_b", f_softplus, [sv["udt"]], [lw["dt_bias"]], [[ddt]],
                                       d_dtypes=[mx])
    (dxs_c, db_c, dc_c), _ = rowwise_bwd(tag + "_silu3_b", f_silu3, [sv["xs_c"], sv["b_c"], sv["c_c"]], [],
                                         [[dxs], [dbm], [dcm]])
    dxs_pre, dcw_xs, dcb_xs = conv_bwd(tag + "_convxs_b", xs_pre, dxs_c, lw["cw_xs"], MXU_DTYPE)
    db_pre, dcw_b, dcb_b = conv_bwd(tag + "_convb_b", b_pre, db_c, lw["cw_b"], MXU_DTYPE)
    dc_pre, dcw_c, dcb_c = conv_bwd(tag + "_convc_b", c_pre, dc_c, lw["cw_c"], MXU_DTYPE)
    (da1,), (dlag, dlab) = rowwise_bwd(tag + "_lna_b", f_ln_silu, [sv["a1"]], [lw["lag"], lw["lab"]], [[dya]])
    da0, dcaw, dcab = conv_bwd(tag + "_conva_b", sv["a0"], da1, lw["caw"])
    (dua, dug), _ = rowwise_bwd(tag + "_glu_b", f_glu, [ua, ug], [], [[da0]], d_dtypes=[mx, mx])
    du = jnp.concatenate([dua, dug, dz, dxs_pre, db_pre, dc_pre], axis=1)
    dx_m = mm(tag + "_dxm", du, lw["win_main"], "nn")
    dx_dt = mm(tag + "_dxdt", dudt, lw["win_dt"], "nn")
    dwin_main = mm(tag + "_dwin", du, xm, "tn", EXCHANGE_DTYPE)
    dwin_dt = mm(tag + "_dwindt", dudt, xm, "tn", EXCHANGE_DTYPE)
    grads = dict(
        e_w_in=jnp.concatenate([dwin_main, dwin_dt[:nh]], axis=0),
        e_conv_a_w=dcaw, e_conv_a_b=dcab[0], e_ln_a_g=dlag[0], e_ln_a_b=dlab[0],
        e_conv_b_w=jnp.concatenate([dcw_xs, dcw_b, dcw_c], axis=1),
        e_conv_b_b=jnp.concatenate([dcb_xs, dcb_b, dcb_c], axis=1)[0],
        e_dt_bias=ddt_bias[0, :nh], e_a_log=dalog.reshape(nh), e_d_skip=ddsk.reshape(nh), e_norm_b_g=dnorm_g[0],
        e_w_out=jnp.concatenate([dwout_a, dwout_b], axis=0), ln_g0=dg0[0], ln_b0=db0[0],
    )
    return [Win(dpre, coef=alpha), dx_m, dx_dt], grads


def odd_fwd(tag, x, xm, w, j, ln_g, ln_b, alpha):
    d = x.shape[1]
    u = mm(tag + "_win", xm, w["o_w_in"][j], "nt")
    bg, cg, v = Win(u, d, 0), Win(u, d, 1), Win(u, d, 2)
    (s,) = rowwise(tag + "_cv", f_mul, [cg, v], [], [d])
    cs = conv_fwd(tag + "_conv", s, w["o_conv_w"][j], None)
    (m,) = rowwise(tag + "_bm", f_mul, [bg, cs], [], [d], out_dtypes=[MXU_DTYPE])
    mix = mm(tag + "_wout", m, w["o_w_out"][j], "nn")

    def f_res(xv, mv, g, b):
        pre = alpha * xv + mv
        y = _ln(pre, g, b)
        return y, y, pre

    x1, x1m, pre = rowwise(tag + "_res", f_res, [x, mix], [ln_g, ln_b], [d] * 3, out_dtypes=[F32, MXU_DTYPE, F32])
    return x1, x1m, dict(xm=xm, u=u, s=s, cs=cs, m=m, pre=pre)


def odd_bwd(tag, dx1_pieces, sv, w, j, ln_g, ln_b, alpha):
    xm, u = sv["xm"], sv["u"]
    d = xm.shape[1]
    mx = (MXU_DTYPE,)
    bg, cg, v = Win(u, d, 0), Win(u, d, 1), Win(u, d, 2)
    (dpre, dprem), (dg0, db0) = rowwise_bwd(tag + "_res_b", f_ln, [sv["pre"]], [ln_g, ln_b], [dx1_pieces],
                                            d_dtypes=[(F32, MXU_DTYPE)])
    dm = mm(tag + "_dm", dprem, w["o_w_out"][j], "nt")
    dwout = mm(tag + "_dwout", sv["m"], dprem, "tn", EXCHANGE_DTYPE)
    (dbg, dcs), _ = rowwise_bwd(tag + "_bm_b", f_mul, [bg, sv["cs"]], [], [[dm]], d_dtypes=[mx, (F32,)])
    ds, dcw, _ = conv_bwd(tag + "_conv_b", sv["s"], dcs, w["o_conv_w"][j])
    (dcg, dv), _ = rowwise_bwd(tag + "_cv_b", f_mul, [cg, v], [], [[ds]], d_dtypes=[mx, mx])
    du = jnp.concatenate([dbg, dcg, dv], axis=1)
    dx_u = mm(tag + "_dx", du, w["o_w_in"][j], "nn")
    dwin = mm(tag + "_dwin", du, xm, "tn", EXCHANGE_DTYPE)
    grads = dict(o_w_in=dwin, o_conv_w=dcw, o_w_out=dwout, ln_g0=dg0[0], ln_b0=db0[0])
    return [Win(dpre, coef=alpha), dx_u], grads


def ffn_fwd(tag, x1, x1m, p_i, w, i, ln_g, ln_b, alpha):
    d = x1.shape[1]
    hpre = mm(tag + "_wup", x1m, w["f_w_up"][i], "nt")
    act = gated_conv_fwd(tag + "_fgate", hpre, w["f_conv_w"][i], w["f_conv_b"][i][None], MXU_DTYPE)
    ffn = mm(tag + "_wdown", act, w["f_w_down"][i], "nn")
    pp = mm(tag + "_pproj", p_i, w["ple_w_proj"][i], "nt")
    gt = mm(tag + "_pgate", x1m, w["ple_w_gate"][i], "nn")

    def f_res2(xv, fv, ppv, gtv, g, b):
        pre = alpha * xv + fv + ppv * _sigmoid(gtv)
        y = _ln(pre, g, b)
        return y, y, pre

    x2, x2m, pre = rowwise(tag + "_res2", f_res2, [x1, ffn, pp, gt], [ln_g, ln_b], [d] * 3,
                           out_dtypes=[F32, MXU_DTYPE, F32])
    return x2, x2m, dict(x1m=x1m, hpre=hpre, act=act, pp=pp, gt=gt, pre=pre)


def ffn_bwd(tag, dx2_pieces, sv, p_i, w, i, ln_g, ln_b, alpha):
    x1m = sv["x1m"]
    mx = (MXU_DTYPE,)
    (dpre, dprem), (dg1, db1) = rowwise_bwd(tag + "_res2_b", f_ln, [sv["pre"]], [ln_g, ln_b], [dx2_pieces],
                                            d_dtypes=[(F32, MXU_DTYPE)])
    (dpp, dgt), _ = rowwise_bwd(tag + "_pg_b", f_gate_mul, [sv["pp"], sv["gt"]], [], [[dpre]], d_dtypes=[mx, mx])
    dwproj = mm(tag + "_dwproj", dpp, p_i, "tn", EXCHANGE_DTYPE)
    dwgate = mm(tag + "_dwgate", x1m, dgt, "tn", EXCHANGE_DTYPE)
    dx1_a = mm(tag + "_dx1a", dgt, w["ple_w_gate"][i], "nt")
    dact = mm(tag + "_dact", dprem, w["f_w_down"][i], "nt")
    dwdown = mm(tag + "_dwdown", sv["act"], dprem, "tn", EXCHANGE_DTYPE)
    dhpre, dfcw, dfcb = gated_conv_bwd(tag + "_fgate_b", sv["hpre"], dact, w["f_conv_w"][i], w["f_conv_b"][i][None],
                                       MXU_DTYPE)
    dwup = mm(tag + "_dwup", dhpre, x1m, "tn", EXCHANGE_DTYPE)
    dx1_b = mm(tag + "_dx1b", dhpre, w["f_w_up"][i], "nn")
    grads = dict(f_w_up=dwup, f_conv_w=dfcw, f_conv_b=dfcb[0], f_w_down=dwdown, ple_w_proj=dwproj, ple_w_gate=dwgate,
                 ln_g1=dg1[0], ln_b1=db1[0])
    return [Win(dpre, coef=alpha), dx1_a, dx1_b], grads


def local_step(x, p, w, target):
    depth = w["ln_g"].shape[0]
    alpha = _alpha(depth)
    d = x.shape[1]
    saved = []
    h = hm = x
    for i in range(depth):
        j = i // 2
        g0, b0, g1, b1 = w["ln_g"][i, 0][None], w["ln_b"][i, 0][None], w["ln_g"][i, 1][None], w["ln_b"][i, 1][None]
        tag = "l%d" % i
        if i % 2 == 0:
            lw = split_even_weights(w, j)
            h, hm, sv_m = even_fwd(tag, h, hm, lw, g0, b0, alpha)
        else:
            lw = None
            h, hm, sv_m = odd_fwd(tag, h, hm, w, j, g0, b0, alpha)
        h, hm, sv_f = ffn_fwd(tag, h, hm, p[i], w, i, g1, b1, alpha)
        saved.append((lw, sv_m, sv_f, (g0, b0, g1, b1)))

    def f_loss(xf, tg):
        diff = xf - tg
        sq = jnp.sum(jnp.sum(jnp.square(diff), axis=1, keepdims=True), axis=0, keepdims=True)
        return diff * (1.0 / d), jnp.broadcast_to(sq, (1, LANE))

    dxf, sq = rowwise("loss", f_loss, [h, target], [], [d], red_widths=[LANE])
    loss = sq[0, 0] * (0.5 / d)

    per_layer = []
    pieces = [dxf]
    for i in reversed(range(depth)):
        j = i // 2
        lw, sv_m, sv_f, (g0, b0, g1, b1) = saved[i]
        tag = "l%d" % i
        pieces, gf = ffn_bwd(tag, pieces, sv_f, p[i], w, i, g1, b1, alpha)
        if i % 2 == 0:
            pieces, gm = even_bwd(tag, pieces, sv_m, lw, g0, b0, alpha)
        else:
            pieces, gm = odd_bwd(tag, pieces, sv_m, w, j, g0, b0, alpha)
        per_layer.append((i, gm, gf))

    def f_sum(*vs):
        acc = None
        for v, c in zip(vs, [pc.coef for pc in map(_win, pieces)]):
            v = v if c == 1.0 else v * c
            acc = v if acc is None else acc + v
        return (acc,)

    (grad_x,) = rowwise("grad_x", f_sum, [Win(_win(pc).arr) for pc in pieces], [], [d])

    by_layer = {i: (gm, gf) for i, gm, gf in per_layer}
    grads = {}
    n_even, n_odd = (depth + 1) // 2, depth // 2
    collect = lambda name, per_layer: per_layer if name in BIG else jnp.stack(per_layer)
    for name in ("e_w_in", "e_conv_a_w", "e_conv_a_b", "e_ln_a_g", "e_ln_a_b", "e_conv_b_w", "e_conv_b_b", "e_dt_bias",
                 "e_a_log", "e_d_skip", "e_norm_b_g", "e_w_out"):
        grads[name] = collect(name, [by_layer[2 * j][0][name] for j in range(n_even)])
    for name in ("o_w_in", "o_conv_w", "o_w_out"):
        grads[name] = collect(name, [by_layer[2 * j + 1][0][name] for j in range(n_odd)])
    for name in ("f_w_up", "f_conv_w", "f_conv_b", "f_w_down", "ple_w_proj", "ple_w_gate"):
        grads[name] = collect(name, [by_layer[i][1][name] for i in range(depth)])
    grads["ln_g"] = jnp.stack([jnp.stack([by_layer[i][0]["ln_g0"], by_layer[i][1]["ln_g1"]]) for i in range(depth)])
    grads["ln_b"] = jnp.stack([jnp.stack([by_layer[i][0]["ln_b0"], by_layer[i][1]["ln_b1"]]) for i in range(depth)])
    return loss, grad_x, grads


_ANY = pl.BlockSpec(memory_space=pl.ANY)
_MESH = pl.DeviceIdType.MESH


def all_gather(name, xl):
    r, c_ = xl.shape

    def body(x_ref, out_ref, send_sems, recv_sems, local_sem):
        x, y, c = lax.axis_index("x"), lax.axis_index("y"), lax.axis_index("c")
        me, sibling = (x, y, c), (x, y, 1 - c)
        chips = [(1 - x, y), (x, 1 - y), (1 - x, 1 - y)]

        def slot(px, py, pc):
            return out_ref.at[4 * px + 2 * py + pc]

        def copy(k, block, to, src=None):
            return pltpu.make_async_remote_copy(
                src_ref=slot(*block) if src is None else src, dst_ref=slot(*block),
                send_sem=send_sems.at[k], recv_sem=recv_sems.at[k], device_id=to, device_id_type=_MESH)

        mine = pltpu.make_async_copy(x_ref, slot(*me), local_sem)
        mine.start()
        first = [copy(0, me, sibling, src=x_ref)]
        first += [copy(1 + j, me, (*chip, c), src=x_ref) for j, chip in enumerate(chips)]
        for cp in first:
            cp.start()
        passed = [copy(4 + j, (*chip, c), sibling) for j, chip in enumerate(chips)]
        for j, chip in enumerate(chips):
            copy(1 + j, (*chip, c), me).wait_recv()
            passed[j].start()
        copy(0, sibling, me).wait_recv()
        for j, chip in enumerate(chips):
            copy(4 + j, (*chip, 1 - c), me).wait_recv()
        for cp in first + passed:
            cp.wait_send()
        mine.wait()

    return pl.pallas_call(
        body, name=name, out_shape=S((N_DEV, r, c_), xl.dtype), in_specs=[_ANY], out_specs=_ANY,
        scratch_shapes=[pltpu.SemaphoreType.DMA((7,)), pltpu.SemaphoreType.DMA((7,)), pltpu.SemaphoreType.DMA],
    )(xl)


def exchange_sibling(name, g4):
    _, _, r, c_ = g4.shape

    def body(g_ref, out_ref, send_sems, recv_sems):
        x, y, c = lax.axis_index("x"), lax.axis_index("y"), lax.axis_index("c")
        copies = [pltpu.make_async_remote_copy(
            src_ref=g_ref.at[q, 1 - c], dst_ref=out_ref.at[q], send_sem=send_sems.at[q], recv_sem=recv_sems.at[q],
            device_id=(x, y, 1 - c), device_id_type=_MESH) for q in range(4)]
        for cp in copies:
            cp.start()
        for cp in copies:
            cp.wait()

    return pl.pallas_call(
        body, name=name, out_shape=S((4, r, c_), g4.dtype), in_specs=[_ANY], out_specs=_ANY,
        scratch_shapes=[pltpu.SemaphoreType.DMA((4,)), pltpu.SemaphoreType.DMA((4,))],
    )(g4)


def exchange_chips(name, p4):
    _, r, c_ = p4.shape

    def body(p_ref, out_ref, send_sems, recv_sems):
        x, y, c = lax.axis_index("x"), lax.axis_index("y"), lax.axis_index("c")
        chips = [(1 - x, y), (x, 1 - y), (1 - x, 1 - y)]
        copies = [pltpu.make_async_remote_copy(
            src_ref=p_ref.at[2 * cx + cy], dst_ref=out_ref.at[k], send_sem=send_sems.at[k], recv_sem=recv_sems.at[k],
            device_id=(cx, cy, c), device_id_type=_MESH) for k, (cx, cy) in enumerate(chips)]
        for cp in copies:
            cp.start()
        for cp in copies:
            cp.wait()

    return pl.pallas_call(
        body, name=name, out_shape=S((3, r, c_), p4.dtype), in_specs=[_ANY], out_specs=_ANY,
        scratch_shapes=[pltpu.SemaphoreType.DMA((3,)), pltpu.SemaphoreType.DMA((3,))],
    )(p4)


def sum_with_sibling(name, g4, recv, core):
    _, _, r, c_ = g4.shape
    tr = _pick(r, (256, 128, 64, 32, 16, 8))

    def body(core_ref, g_ref, r_ref, o_ref, ox_ref):
        s = g_ref[0].astype(F32) + r_ref[...].astype(F32)
        o_ref[...] = s
        ox_ref[...] = s.astype(ox_ref.dtype)

    out_spec = pl.BlockSpec((1, tr, c_), lambda q, i, cr: (q, i, 0))
    return pl.pallas_call(
        body, name=name, out_shape=[S((4, r, c_), F32), S((4, r, c_), EXCHANGE_DTYPE)],
        grid_spec=pltpu.PrefetchScalarGridSpec(
            num_scalar_prefetch=1, grid=(4, r // tr),
            in_specs=[pl.BlockSpec((1, 1, tr, c_), lambda q, i, cr: (q, cr[0], i, 0)),
                      pl.BlockSpec((1, tr, c_), lambda q, i, cr: (q, i, 0))],
            out_specs=[out_spec, out_spec]),
    )(core, g4, recv)


def sum_with_chips(name, p4, recv, chip):
    _, r, c_ = p4.shape
    tr = _pick(r, (256, 128, 64, 32, 16, 8))

    def body(chip_ref, p_ref, r_ref, o_ref):
        o_ref[...] = ((p_ref[0] + r_ref[0].astype(F32)) + r_ref[1].astype(F32)) + r_ref[2].astype(F32)

    return pl.pallas_call(
        body, name=name, out_shape=S((r, c_), F32),
        grid_spec=pltpu.PrefetchScalarGridSpec(
            num_scalar_prefetch=1, grid=(r // tr,),
            in_specs=[pl.BlockSpec((1, tr, c_), lambda i, ch: (ch[0], i, 0)),
                      pl.BlockSpec((3, tr, c_), lambda i, ch: (0, i, 0))],
            out_specs=pl.BlockSpec((tr, c_), lambda i, ch: (i, 0))),
    )(chip, p4, recv)


def sum_devices(name, g8):
    _, r, c_ = g8.shape

    def body(g_ref, o_ref):
        acc = g_ref[0]
        for k in range(1, N_DEV):
            acc = acc + g_ref[k]
        o_ref[...] = acc

    return pl.pallas_call(body, name=name, out_shape=S((r, c_), F32))(g8)


def _flatten(parts, cols, row_mult):
    flat = jnp.concatenate([v.reshape(-1) for v in parts])
    n = flat.shape[0]
    rows = -(-n // (cols * row_mult)) * row_mult
    return jnp.pad(flat, (0, rows * cols - n)).reshape(rows, cols)


def _exchange_dims(name, lshape):
    l, r, c = lshape
    return (l, c, r) if name in TRANSPOSED else (l, r, c)


def gather_big(name, local):
    cols = local["e_w_out"].shape[2]
    parts = []
    for n in EXCHANGE_ORDER:
        v = local[n].astype(MXU_DTYPE)
        parts.append((v.transpose(0, 2, 1) if n in TRANSPOSED else v).reshape(-1, cols))
    got = all_gather(name, jnp.concatenate(parts, axis=0))
    full, r0 = {}, 0
    for n, part in zip(EXCHANGE_ORDER, parts):
        l, a, b = _exchange_dims(n, local[n].shape)
        seg = got[:, r0:r0 + part.shape[0]].reshape(N_DEV, l, a, b)
        full[n] = seg.transpose(1, 0, 2, 3).reshape(l, N_DEV * a, b)
        r0 += part.shape[0]
    return full


def reduce_scatter_big(grads, local_shapes):
    cols = local_shapes["e_w_out"][2]
    parts, pieces = [], []
    for n in EXCHANGE_ORDER:
        per_layer = [g.reshape(N_DEV, -1, cols) for g in grads[n]]
        parts.append(sum(pc.shape[1] for pc in per_layer))
        pieces += per_layer
    n_rows = sum(parts)
    rows = -(-n_rows // 256) * 256
    if rows > n_rows:
        pieces.append(jnp.zeros((N_DEV, rows - n_rows, cols), EXCHANGE_DTYPE))
    g4 = jnp.concatenate(pieces, axis=1).reshape(4, 2, rows, cols)
    x, y, c = lax.axis_index("x"), lax.axis_index("y"), lax.axis_index("c")
    from_sibling = exchange_sibling("rs_sibling", g4)
    p4, p4x = sum_with_sibling("rs_sum_sibling", g4, from_sibling, c.astype(jnp.int32).reshape(1))
    from_chips = exchange_chips("rs_chips", p4x)
    mine = sum_with_chips("rs_sum_chips", p4, from_chips, (2 * x + y).astype(jnp.int32).reshape(1))
    out, r0 = {}, 0
    for n, part in zip(EXCHANGE_ORDER, parts):
        l, a, b = _exchange_dims(n, local_shapes[n])
        seg = mine[r0:r0 + part].reshape(l, a, b)
        out[n] = seg.transpose(0, 2, 1) if n in TRANSPOSED else seg
        r0 += part
    return out


def gather_small(name, local, names):
    flat = _flatten([local[n] for n in names], LANE, 1)
    got = all_gather(name, flat).reshape(N_DEV, -1)
    full, off = {}, 0
    for n in names:
        size = math.prod(local[n].shape)
        seg = got[:, off:off + size].reshape((N_DEV,) + local[n].shape)
        full[n] = seg.transpose(1, 2, 0, 3).reshape(seg.shape[1], seg.shape[2], -1)
        off += size
    return full


def all_reduce_small(grads, names):
    flat = _flatten([grads[n] for n in names], LANE, SUBLANE)
    total = sum_devices("ar_sum", all_gather("ar_gather", flat)).reshape(-1)
    out, off = {}, 0
    for nm in names:
        size = math.prod(grads[nm].shape)
        out[nm] = total[off:off + size].reshape(grads[nm].shape)
        off += size
    return out


def adamw(name, w, g, m, v):
    shape = w.shape
    cols = shape[-1]
    rows = math.prod(shape[:-1])
    tr = _pick(rows, (256, 128, 64, 32, 16, 8)) if rows * cols > 256 * 1024 else rows
    c1 = 1.0 - ADAM_B1 ** ADAM_STEP
    c2 = 1.0 - ADAM_B2 ** ADAM_STEP

    def body(w_ref, g_ref, m_ref, v_ref, d_ref, nm_ref, nv_ref):
        gv = g_ref[...]
        m2 = ADAM_B1 * m_ref[...] + (1.0 - ADAM_B1) * gv
        v2 = ADAM_B2 * v_ref[...] + (1.0 - ADAM_B2) * jnp.square(gv)
        d_ref[...] = -ADAM_LR * ((m2 / c1) / (jnp.sqrt(v2 / c2) + ADAM_EPS) + ADAM_WD * w_ref[...])
        nm_ref[...] = m2
        nv_ref[...] = v2

    spec = pl.BlockSpec((tr, cols), lambda i: (i, 0))
    outs = pl.pallas_call(
        body, name=name, grid=(rows // tr,), in_specs=[spec] * 4, out_specs=[spec] * 3,
        out_shape=[S((rows, cols), F32)] * 3,
        compiler_params=pltpu.CompilerParams(dimension_semantics=("parallel",)),
    )(*[a.reshape(rows, cols) for a in (w, g, m, v)])
    return tuple(o.reshape(shape) for o in outs)


def kernel(x, p, e_w_in, e_conv_a_w, e_conv_a_b, e_ln_a_g, e_ln_a_b, e_conv_b_w, e_conv_b_b, e_dt_bias, e_a_log, e_d_skip, e_norm_b_g, e_w_out, o_w_in, o_conv_w, o_w_out, f_w_up, f_conv_w, f_conv_b, f_w_down, ple_w_proj, ple_w_gate, ln_g, ln_b, loss_target, m_e_w_in, m_e_conv_a_w, m_e_conv_a_b, m_e_ln_a_g, m_e_ln_a_b, m_e_conv_b_w, m_e_conv_b_b, m_e_dt_bias, m_e_a_log, m_e_d_skip, m_e_norm_b_g, m_e_w_out, m_o_w_in, m_o_conv_w, m_o_w_out, m_f_w_up, m_f_conv_w, m_f_conv_b, m_f_w_down, m_ple_w_proj, m_ple_w_gate, m_ln_g, m_ln_b, v_e_w_in, v_e_conv_a_w, v_e_conv_a_b, v_e_ln_a_g, v_e_ln_a_b, v_e_conv_b_w, v_e_conv_b_b, v_e_dt_bias, v_e_a_log, v_e_d_skip, v_e_norm_b_g, v_e_w_out, v_o_w_in, v_o_conv_w, v_o_w_out, v_f_w_up, v_f_conv_w, v_f_conv_b, v_f_w_down, v_ple_w_proj, v_ple_w_gate, v_ln_g, v_ln_b):
    args = locals()
    local = {n: args[n] for n in WEIGHTS}
    mom = {n: args["m_" + n] for n in WEIGHTS}
    var = {n: args["v_" + n] for n in WEIGHTS}

    full = {n: local[n] for n in REPLICATED}
    full.update(gather_big("ag_big", local))
    full.update(gather_small("ag_small", local, SMALL_SHARDED))

    loss_local, grad_x, grads = local_step(x[0], p[:, 0], full, loss_target[0])
    loss = lax.psum(loss_local, MESH_AXES)

    g_local = reduce_scatter_big(grads, {n: local[n].shape for n in BIG})
    small = all_reduce_small(grads, REPLICATED + SMALL_SHARDED)
    dev = 4 * lax.axis_index("x") + 2 * lax.axis_index("y") + lax.axis_index("c")
    for n in REPLICATED:
        g_local[n] = small[n]
    for n in SMALL_SHARDED:
        width = local[n].shape[2]
        g_local[n] = lax.dynamic_slice_in_dim(small[n], dev * width, width, axis=2)

    delta, new_m, new_v = {}, {}, {}
    for n in WEIGHTS:
        delta[n], new_m[n], new_v[n] = adamw("adamw_" + n, local[n], g_local[n], mom[n], var[n])
    return (loss, grad_x[None], *[g_local[n] for n in WEIGHTS], *[delta[n] for n in WEIGHTS],
            *[new_m[n] for n in WEIGHTS], *[new_v[n] for n in WEIGHTS])
```

```python
import functools
import math

import jax
import jax.numpy as jnp
from jax import lax
from jax.experimental import pallas as pl
from jax.experimental.pallas import tpu as pltpu

F32 = jnp.float32
MXU_DTYPE = jnp.bfloat16
MESH_AXES = ("x", "y", "c")
N_DEV = 8
LANE = 128
SUBLANE = 8
ROWWISE_VMEM_BUDGET = 20 * 1024 * 1024
MM_TILES = (1408, 1024, 512, 256, 128)
EXCHANGE_DTYPE = jnp.bfloat16
RS_ROW_ALIGN = 16
RS_ROW_PAD = 256
LN_EPS = 1e-5
CHUNK = 64
HEAD_DIM = 64
N_GROUPS = 4
N_STATE = 128
CONV_PAD = 32
CONV_ROWS = 256
ADAM_LR, ADAM_B1, ADAM_B2, ADAM_EPS, ADAM_WD, ADAM_STEP = 0.001, 0.9, 0.999, 1e-08, 0.01, 10

BIG = ("e_w_in", "e_w_out", "o_w_in", "o_w_out", "f_w_up", "f_w_down", "ple_w_proj", "ple_w_gate")
SMALL_SHARDED = ("e_conv_a_w", "e_conv_b_w", "o_conv_w", "f_conv_w", "ln_g", "ln_b")
REPLICATED = ("e_conv_a_b", "e_ln_a_g", "e_ln_a_b", "e_conv_b_b", "e_dt_bias", "e_a_log", "e_d_skip",
              "e_norm_b_g", "f_conv_b")
TRANSPOSED = ("e_w_in", "o_w_in", "f_w_up", "ple_w_proj")
EXCHANGE_ORDER = ("e_w_out", "o_w_out", "ple_w_gate", "ple_w_proj", "o_w_in", "f_w_down", "f_w_up", "e_w_in")
WEIGHTS = ("e_w_in", "e_conv_a_w", "e_conv_a_b", "e_ln_a_g", "e_ln_a_b", "e_conv_b_w", "e_conv_b_b", "e_dt_bias",
           "e_a_log", "e_d_skip", "e_norm_b_g", "e_w_out", "o_w_in", "o_conv_w", "o_w_out", "f_w_up", "f_conv_w",
           "f_conv_b", "f_w_down", "ple_w_proj", "ple_w_gate", "ln_g", "ln_b")

S = jax.ShapeDtypeStruct


class Win:
    def __init__(self, arr, w=None, idx=0, coef=1.0):
        self.arr, self.w, self.idx, self.coef = arr, (arr.shape[1] if w is None else w), idx, coef


def _win(a):
    return a if isinstance(a, Win) else Win(a)


def _pick(n, prefs):
    for p in prefs:
        if p <= n and n % p == 0:
            return p
    return n


_MM_DIMS = {"nn": (1, 0), "nt": (1, 1), "tn": (0, 0)}


def mm(name, a, b, mode, out_dtype=F32):
    ca, cb = _MM_DIMS[mode]
    kdim = a.shape[ca]
    m = a.shape[1 - ca]
    n = b.shape[1 - cb]
    assert b.shape[cb] == kdim, (name, a.shape, b.shape, mode)
    tm = _pick(m, MM_TILES)
    tn = _pick(n, MM_TILES)
    tk = kdim if kdim <= MM_TILES[0] else _pick(kdim, MM_TILES)
    nk = kdim // tk
    own_acc = nk > 1 and out_dtype != F32

    def body(a_ref, b_ref, o_ref, *scratch):
        acc_ref = scratch[0] if own_acc else o_ref
        d = lax.dot_general(a_ref[...].astype(MXU_DTYPE), b_ref[...].astype(MXU_DTYPE),
                            (((ca,), (cb,)), ((), ())), preferred_element_type=F32)
        if nk == 1:
            o_ref[...] = d.astype(o_ref.dtype)
        else:
            k = pl.program_id(2)

            @pl.when(k == 0)
            def _():
                acc_ref[...] = d

            @pl.when(k > 0)
            def _():
                acc_ref[...] += d

            if own_acc:
                @pl.when(k == nk - 1)
                def _():
                    o_ref[...] = acc_ref[...].astype(o_ref.dtype)

    a_spec = pl.BlockSpec((tm, tk), lambda i, j, k: (i, k)) if ca == 1 else pl.BlockSpec((tk, tm), lambda i, j, k: (k, i))
    b_spec = pl.BlockSpec((tk, tn), lambda i, j, k: (k, j)) if cb == 0 else pl.BlockSpec((tn, tk), lambda i, j, k: (j, k))
    return pl.pallas_call(
        body, name=name, grid=(m // tm, n // tn, nk),
        in_specs=[a_spec, b_spec], out_specs=pl.BlockSpec((tm, tn), lambda i, j, k: (i, j)),
        out_shape=S((m, n), out_dtype), scratch_shapes=[pltpu.VMEM((tm, tn), F32)] if own_acc else [],
        compiler_params=pltpu.CompilerParams(dimension_semantics=("parallel", "parallel", "arbitrary")),
    )(a, b)


def _row_block(t, widths):
    tb = 512
    while tb > SUBLANE and (t % tb or tb * sum(widths) * 8 > ROWWISE_VMEM_BUDGET):
        tb //= 2
    return tb


def _tok_spec(tb, w):
    return pl.BlockSpec((tb, w.w), functools.partial(lambda i, idx: (i, idx), idx=w.idx))


def _par_spec(p):
    return pl.BlockSpec((1, p.shape[1]), lambda i: (0, 0))


def rowwise(name, fn, tok, par, out_widths, red_widths=(), out_dtypes=None):
    tok = [_win(t) for t in tok]
    t = tok[0].arr.shape[0]
    tb = _row_block(t, [w.w for w in tok] + list(out_widths))
    n_tok, n_par, n_out = len(tok), len(par), len(out_widths)
    out_dtypes = [F32] * n_out if out_dtypes is None else out_dtypes

    def body(*refs):
        ins = [r[...] for r in refs[:n_tok + n_par]]
        res = fn(*ins)
        out_refs = refs[n_tok + n_par:n_tok + n_par + n_out]
        red_refs = refs[n_tok + n_par + n_out:]
        for r, v in zip(out_refs, res[:n_out]):
            r[...] = v.astype(r.dtype)
        if red_refs:
            @pl.when(pl.program_id(0) == 0)
            def _():
                for r in red_refs:
                    r[...] = jnp.zeros_like(r)
            for r, v in zip(red_refs, res[n_out:]):
                r[...] += v

    outs = pl.pallas_call(
        body, name=name, grid=(t // tb,),
        in_specs=[_tok_spec(tb, w) for w in tok] + [_par_spec(p) for p in par],
        out_specs=[pl.BlockSpec((tb, w), lambda i: (i, 0)) for w in out_widths]
        + [pl.BlockSpec((1, w), lambda i: (0, 0)) for w in red_widths],
        out_shape=[S((t, w), dt) for w, dt in zip(out_widths, out_dtypes)] + [S((1, w), F32) for w in red_widths],
        compiler_params=pltpu.CompilerParams(dimension_semantics=("arbitrary",)),
    )(*[w.arr for w in tok], *par)
    return outs


def rowwise_bwd(name, fn, tok, par, cts, d_dtypes=None):
    tok = [_win(t) for t in tok]
    cts = [[_win(c) for c in group] for group in cts]
    d_dtypes = [(F32,)] * len(tok) if d_dtypes is None else d_dtypes
    t = tok[0].arr.shape[0]
    flat_cts = [c for group in cts for c in group]
    d_outs = [(i, w.w, dt) for i, (w, dts) in enumerate(zip(tok, d_dtypes)) for dt in dts]
    tb = _row_block(t, [w.w for w in tok] + [c.w for c in flat_cts] + [w for _, w, _ in d_outs])
    n_tok, n_par, n_ct, n_d = len(tok), len(par), len(flat_cts), len(d_outs)

    def body(*refs):
        tok_vals = [r[...] for r in refs[:n_tok]]
        par_vals = [r[...] for r in refs[n_tok:n_tok + n_par]]
        ct_refs = refs[n_tok + n_par:n_tok + n_par + n_ct]
        d_refs = refs[n_tok + n_par + n_ct:n_tok + n_par + n_ct + n_d]
        dp_refs = refs[n_tok + n_par + n_ct + n_d:]
        ct_vals, pos = [], 0
        for group in cts:
            acc = None
            for c in group:
                v = ct_refs[pos][...]
                if c.coef != 1.0:
                    v = v * c.coef
                acc = v if acc is None else acc + v
                pos += 1
            ct_vals.append(acc)
        _, vjp = jax.vjp(lambda *a: tuple(fn(*a)), *tok_vals, *par_vals)
        grads = vjp(tuple(ct_vals))
        for r, (i, _, _) in zip(d_refs, d_outs):
            r[...] = grads[i].astype(r.dtype)
        if dp_refs:
            @pl.when(pl.program_id(0) == 0)
            def _():
                for r in dp_refs:
                    r[...] = jnp.zeros_like(r)
            for r, v in zip(dp_refs, grads[n_tok:]):
                r[...] += v

    outs = pl.pallas_call(
        body, name=name, grid=(t // tb,),
        in_specs=[_tok_spec(tb, w) for w in tok] + [_par_spec(p) for p in par] + [_tok_spec(tb, c) for c in flat_cts],
        out_specs=[pl.BlockSpec((tb, w), lambda i: (i, 0)) for _, w, _ in d_outs] + [_par_spec(p) for p in par],
        out_shape=[S((t, w), dt) for _, w, dt in d_outs] + [S(p.shape, F32) for p in par],
        compiler_params=pltpu.CompilerParams(dimension_semantics=("arbitrary",)),
    )(*[w.arr for w in tok], *par, *[c.arr for c in flat_cts])
    return outs[:n_d], outs[n_d:]


def _sigmoid(x):
    return 1.0 / (1.0 + jnp.exp(-x))


def _silu(x):
    return x * _sigmoid(x)


def _ln(x, g, b):
    mu = jnp.mean(x, axis=-1, keepdims=True)
    var = jnp.mean(jnp.square(x - mu), axis=-1, keepdims=True)
    return (x - mu) * lax.rsqrt(var + LN_EPS) * g + b


def f_glu(ua, ug):
    return (ua * _sigmoid(ug),)


def f_ln_silu(a1, g, b):
    return (_silu(_ln(a1, g, b)),)


def f_silu3(a, b, c):
    return (_silu(a), _silu(b), _silu(c))


def f_softplus(dt_raw, bias):
    return (jax.nn.softplus(dt_raw + bias),)


def f_gate_rms(yssd, z, g):
    y = yssd * _silu(z)
    return (y * lax.rsqrt(jnp.mean(jnp.square(y), axis=-1, keepdims=True) + LN_EPS) * g,)


def f_ln(pre, g, b):
    return (_ln(pre, g, b),)


def f_mul(a, b):
    return (a * b,)


def f_gate_mul(pp, gt):
    return (pp * _sigmoid(gt),)


def conv_fwd(name, x, w, b):
    x = _win(x)
    t, c = x.arr.shape[0], x.w
    kw = w.shape[0]
    cb = LANE
    off = x.idx * (c // cb)
    rows = min(CONV_ROWS, t)
    has_b = b is not None

    def body(*refs):
        if has_b:
            x_ref, w_ref, b_ref, y_ref, xp_ref = refs
        else:
            x_ref, w_ref, y_ref, xp_ref = refs
        xp_ref[0:CONV_PAD, :] = jnp.zeros((CONV_PAD, cb), F32)
        xp_ref[CONV_PAD:CONV_PAD + t, :] = x_ref[...]

        def step(s, carry):
            base = pl.multiple_of(s * rows, rows)
            acc = jnp.zeros((rows, cb), F32)
            if has_b:
                acc = acc + b_ref[...]
            for k in range(kw):
                acc = acc + w_ref[k:k + 1, :] * xp_ref[pl.ds(base + CONV_PAD - (kw - 1) + k, rows), :]
            y_ref[pl.ds(base, rows), :] = acc
            return carry

        lax.fori_loop(0, t // rows, step, 0)

    in_specs = [pl.BlockSpec((t, cb), lambda j: (0, off + j)), pl.BlockSpec((kw, cb), lambda j: (0, j))]
    args = [x.arr, w]
    if has_b:
        in_specs.append(pl.BlockSpec((1, cb), lambda j: (0, j)))
        args.append(b)
    return pl.pallas_call(
        body, name=name, grid=(c // cb,), in_specs=in_specs,
        out_specs=pl.BlockSpec((t, cb), lambda j: (0, j)), out_shape=S((t, c), F32),
        scratch_shapes=[pltpu.VMEM((CONV_PAD + t, cb), F32)],
        compiler_params=pltpu.CompilerParams(dimension_semantics=("parallel",)),
    )(*args)


def conv_bwd(name, x, dy, w, dx_dtype=F32):
    x, dy = _win(x), _win(dy)
    t, c = x.arr.shape[0], x.w
    kw = w.shape[0]
    cb = LANE
    xoff = x.idx * (c // cb)
    dyoff = dy.idx * (c // cb)
    rows = min(CONV_ROWS, t)

    def body(x_ref, dy_ref, w_ref, dx_ref, dw_ref, db_ref, xp_ref, dyp_ref):
        xp_ref[0:CONV_PAD, :] = jnp.zeros((CONV_PAD, cb), F32)
        xp_ref[CONV_PAD:CONV_PAD + t, :] = x_ref[...]
        dyp_ref[0:t, :] = dy_ref[...]
        dyp_ref[t:t + CONV_PAD, :] = jnp.zeros((CONV_PAD, cb), F32)

        def fold(v):
            return jnp.sum(v.reshape(rows // SUBLANE, SUBLANE, cb), axis=0)

        def step(s, carry):
            base = pl.multiple_of(s * rows, rows)
            dyc = dy_ref[pl.ds(base, rows), :]
            acc = jnp.zeros((rows, cb), F32)
            new = []
            for k in range(kw):
                acc = acc + w_ref[k:k + 1, :] * dyp_ref[pl.ds(base + (kw - 1) - k, rows), :]
                new.append(carry[k] + fold(dyc * xp_ref[pl.ds(base + CONV_PAD - (kw - 1) + k, rows), :]))
            new.append(carry[kw] + fold(dyc))
            dx_ref[pl.ds(base, rows), :] = acc.astype(dx_ref.dtype)
            return tuple(new)

        init = tuple(jnp.zeros((SUBLANE, cb), F32) for _ in range(kw + 1))
        parts = lax.fori_loop(0, t // rows, step, init)
        for k in range(kw):
            dw_ref[k:k + 1, :] = jnp.sum(parts[k], axis=0, keepdims=True)
        db_ref[...] = jnp.sum(parts[kw], axis=0, keepdims=True)

    return pl.pallas_call(
        body, name=name, grid=(c // cb,),
        in_specs=[pl.BlockSpec((t, cb), lambda j: (0, xoff + j)), pl.BlockSpec((t, cb), lambda j: (0, dyoff + j)),
                  pl.BlockSpec((kw, cb), lambda j: (0, j))],
        out_specs=[pl.BlockSpec((t, cb), lambda j: (0, j)), pl.BlockSpec((kw, cb), lambda j: (0, j)),
                   pl.BlockSpec((1, cb), lambda j: (0, j))],
        out_shape=[S((t, c), dx_dtype), S((kw, c), F32), S((1, c), F32)],
        scratch_shapes=[pltpu.VMEM((CONV_PAD + t, cb), F32), pltpu.VMEM((CONV_PAD + t, cb), F32)],
        compiler_params=pltpu.CompilerParams(dimension_semantics=("parallel",)),
    )(x.arr, dy.arr, w)


def gated_conv_fwd(name, hpre, w, b, out_dtype):
    t, c2 = hpre.shape
    ff = c2 // 2
    kw = w.shape[0]
    cb = LANE
    nb = ff // cb
    rows = min(CONV_ROWS, t)

    def body(h1_ref, h2_ref, w1_ref, w2_ref, b1_ref, b2_ref, y_ref, xp1_ref, xp2_ref):
        for xp_ref, h_ref in ((xp1_ref, h1_ref), (xp2_ref, h2_ref)):
            xp_ref[0:CONV_PAD, :] = jnp.zeros((CONV_PAD, cb), F32)
            xp_ref[CONV_PAD:CONV_PAD + t, :] = h_ref[...]

        def step(s, carry):
            base = pl.multiple_of(s * rows, rows)
            h1 = jnp.zeros((rows, cb), F32) + b1_ref[...]
            h2 = jnp.zeros((rows, cb), F32) + b2_ref[...]
            for k in range(kw):
                at = pl.ds(base + CONV_PAD - (kw - 1) + k, rows)
                h1 = h1 + w1_ref[k:k + 1, :] * xp1_ref[at, :]
                h2 = h2 + w2_ref[k:k + 1, :] * xp2_ref[at, :]
            y_ref[pl.ds(base, rows), :] = (_silu(h1) * h2).astype(y_ref.dtype)
            return carry

        lax.fori_loop(0, t // rows, step, 0)

    col1 = lambda r: pl.BlockSpec((r, cb), lambda j: (0, j))
    col2 = lambda r: pl.BlockSpec((r, cb), lambda j: (0, nb + j))
    return pl.pallas_call(
        body, name=name, grid=(nb,),
        in_specs=[col1(t), col2(t), col1(kw), col2(kw), col1(1), col2(1)],
        out_specs=col1(t), out_shape=S((t, ff), out_dtype),
        scratch_shapes=[pltpu.VMEM((CONV_PAD + t, cb), F32)] * 2,
        compiler_params=pltpu.CompilerParams(dimension_semantics=("parallel",)),
    )(hpre, hpre, w, w, b, b)


def gated_conv_bwd(name, hpre, dact, w, b, dx_dtype):
    t, c2 = hpre.shape
    ff = c2 // 2
    kw = w.shape[0]
    cb = LANE
    nb = ff // cb
    rows = min(CONV_ROWS, t)

    def body(own_ref, oth_ref, da_ref, wo_ref, wt_ref, bo_ref, bt_ref, dx_ref, dw_ref, db_ref,
             xpo_ref, xpt_ref, dhp_ref):
        for xp_ref, h_ref in ((xpo_ref, own_ref), (xpt_ref, oth_ref)):
            xp_ref[0:CONV_PAD, :] = jnp.zeros((CONV_PAD, cb), F32)
            xp_ref[CONV_PAD:CONV_PAD + t, :] = h_ref[...]
        dhp_ref[t:t + CONV_PAD, :] = jnp.zeros((CONV_PAD, cb), F32)

        def fold(v):
            return jnp.sum(v.reshape(rows // SUBLANE, SUBLANE, cb), axis=0)

        def first_pass(own_is_gate):
            def step(s, carry):
                base = pl.multiple_of(s * rows, rows)
                ho = jnp.zeros((rows, cb), F32) + bo_ref[...]
                ht = jnp.zeros((rows, cb), F32) + bt_ref[...]
                for k in range(kw):
                    at = pl.ds(base + CONV_PAD - (kw - 1) + k, rows)
                    ho = ho + wo_ref[k:k + 1, :] * xpo_ref[at, :]
                    ht = ht + wt_ref[k:k + 1, :] * xpt_ref[at, :]
                da = da_ref[pl.ds(base, rows), :]
                if own_is_gate:
                    sg = _sigmoid(ho)
                    dh = da * ht * (sg * (1.0 + ho * (1.0 - sg)))
                else:
                    dh = da * _silu(ht)
                dhp_ref[pl.ds(base, rows), :] = dh
                new = [carry[k] + fold(dh * xpo_ref[pl.ds(base + CONV_PAD - (kw - 1) + k, rows), :]) for k in range(kw)]
                new.append(carry[kw] + fold(dh))
                return tuple(new)

            init = tuple(jnp.zeros((SUBLANE, cb), F32) for _ in range(kw + 1))
            parts = lax.fori_loop(0, t // rows, step, init)
            for k in range(kw):
                dw_ref[k:k + 1, :] = jnp.sum(parts[k], axis=0, keepdims=True)
            db_ref[...] = jnp.sum(parts[kw], axis=0, keepdims=True)

        half = pl.program_id(0)

        @pl.when(half == 0)
        def _():
            first_pass(True)

        @pl.when(half == 1)
        def _():
            first_pass(False)

        def second(s, carry):
            base = pl.multiple_of(s * rows, rows)
            acc = jnp.zeros((rows, cb), F32)
            for k in range(kw):
                acc = acc + wo_ref[k:k + 1, :] * dhp_ref[pl.ds(base + (kw - 1) - k, rows), :]
            dx_ref[pl.ds(base, rows), :] = acc.astype(dx_ref.dtype)
            return carry

        lax.fori_loop(0, t // rows, second, 0)

    own = lambda r: pl.BlockSpec((r, cb), lambda h, j: (0, h * nb + j))
    oth = lambda r: pl.BlockSpec((r, cb), lambda h, j: (0, (1 - h) * nb + j))
    return pl.pallas_call(
        body, name=name, grid=(2, nb),
        in_specs=[own(t), oth(t), pl.BlockSpec((t, cb), lambda h, j: (0, j)), own(kw), oth(kw), own(1), oth(1)],
        out_specs=[own(t), own(kw), own(1)],
        out_shape=[S((t, c2), dx_dtype), S((kw, c2), F32), S((1, c2), F32)],
        scratch_shapes=[pltpu.VMEM((CONV_PAD + t, cb), F32)] * 3,
        compiler_params=pltpu.CompilerParams(dimension_semantics=("parallel", "parallel")),
    )(hpre, hpre, dact, w, w, b, b)


def _bdot(a, b, ca, cb):
    return lax.dot_general(a.astype(MXU_DTYPE), b.astype(MXU_DTYPE), (((ca,), (cb,)), ((0,), (0,))),
                           preferred_element_type=F32)


@jax.custom_vjp
def bmm_nn(a, b):
    return _bdot(a, b, 2, 1)


bmm_nn.defvjp(lambda a, b: (_bdot(a, b, 2, 1), (a, b)),
              lambda r, g: (_bdot(g, r[1], 2, 2), _bdot(r[0], g, 1, 1)))


@jax.custom_vjp
def bmm_tn(a, b):
    return _bdot(a, b, 1, 1)


bmm_tn.defvjp(lambda a, b: (_bdot(a, b, 1, 1), (a, b)),
              lambda r, g: (_bdot(r[1], g, 2, 2), _bdot(r[0], g, 2, 1)))


@jax.custom_vjp
def bmm_nt(a, b):
    return _bdot(a, b, 2, 2)


bmm_nt.defvjp(lambda a, b: (_bdot(a, b, 2, 2), (a, b)),
              lambda r, g: (_bdot(g, r[1], 2, 1), _bdot(g, r[0], 1, 1)))


def ssd_chunk(x, dt, bm, cm, hprev, a_log, dsk):
    hg, ln, _ = x.shape
    n = bm.shape[1]
    ii = lax.broadcasted_iota(jnp.int32, (ln, ln), 0)
    jj = lax.broadcasted_iota(jnp.int32, (ln, ln), 1)
    tril, eye, triu = (ii >= jj)[None], (ii == jj)[None], (ii <= jj)[None]
    da = dt * (-jnp.exp(a_log))
    da_row = jnp.sum(jnp.where(eye, da, 0.0), axis=1, keepdims=True)
    dt_row = jnp.sum(jnp.where(eye, dt, 0.0), axis=1, keepdims=True)
    cum_c = jnp.sum(jnp.where(tril, da_row, 0.0), axis=2, keepdims=True)
    cum_r = jnp.sum(jnp.where(triu, da, 0.0), axis=1, keepdims=True)
    last = jnp.sum(da, axis=1, keepdims=True)
    decay = jnp.where(tril, jnp.exp(jnp.where(tril, cum_c - cum_r, 0.0)), 0.0)
    cb = bmm_nt(cm[None], bm[None])
    y_diag = bmm_nn(cb * decay * dt_row, x)
    bb = jnp.broadcast_to(bm[None], (hg, ln, n))
    cc = jnp.broadcast_to(cm[None], (hg, ln, n))
    states = bmm_tn(x * (jnp.exp(last - cum_c) * dt), bb)
    y_off = bmm_nt(cc, hprev) * jnp.exp(cum_c)
    hnew = hprev * jnp.exp(last) + states
    return y_diag + y_off + dsk * x, hnew


def _ssd_dims(xs, bm, a_log):
    t = xs.shape[0]
    h = a_log.shape[0]
    return h, t, xs.shape[1] // h, h // N_GROUPS, bm.shape[1] // N_GROUPS, t // CHUNK


def _heads_of(ref, g, hg, p):
    return jnp.stack([ref[:, (g * hg + i) * p:(g * hg + i + 1) * p] for i in range(hg)])


def _cols_of(ref, g, hg):
    return jnp.stack([ref[:, g * hg + i:g * hg + i + 1] for i in range(hg)])


def ssd_fwd(name, xs, dt, bm, cm, a_log, dsk):
    h, t, p, hg, n, nc = _ssd_dims(xs, bm, a_log)

    def body(al_ref, dk_ref, x_ref, dt_ref, b_ref, c_ref, y_ref, hp_ref, h_scr):
        @pl.when(pl.program_id(0) == 0)
        def _():
            h_scr[...] = jnp.zeros_like(h_scr)

        for g in range(N_GROUPS):
            hs, ns = slice(g * hg, (g + 1) * hg), slice(g * n, (g + 1) * n)
            hprev = h_scr[hs]
            hp_ref[hs, 0] = hprev
            y, hnew = ssd_chunk(_heads_of(x_ref, g, hg, p), _cols_of(dt_ref, g, hg), b_ref[:, ns], c_ref[:, ns],
                                hprev, al_ref[hs], dk_ref[hs])
            for i in range(hg):
                y_ref[:, (g * hg + i) * p:(g * hg + i + 1) * p] = y[i]
            h_scr[hs] = hnew

    head = pl.BlockSpec((h, 1, 1), lambda c: (0, 0, 0))
    row = lambda w: pl.BlockSpec((CHUNK, w), lambda c: (c, 0))
    return pl.pallas_call(
        body, name=name, grid=(nc,),
        in_specs=[head, head, row(h * p), row(dt.shape[1]), row(N_GROUPS * n), row(N_GROUPS * n)],
        out_specs=[row(h * p), pl.BlockSpec((h, 1, p, n), lambda c: (0, c, 0, 0))],
        out_shape=[S((t, h * p), F32), S((h, nc, p, n), F32)],
        scratch_shapes=[pltpu.VMEM((h, p, n), F32)],
        compiler_params=pltpu.CompilerParams(dimension_semantics=("arbitrary",)),
    )(a_log, dsk, xs, dt, bm, cm)


def ssd_bwd(name, xs, dt, bm, cm, a_log, dsk, hp, dy):
    h, t, p, hg, n, nc = _ssd_dims(xs, bm, a_log)

    def body(al_ref, dk_ref, x_ref, dt_ref, b_ref, c_ref, hp_ref, dy_ref,
             dx_ref, ddt_ref, db_ref, dc_ref, dal_ref, ddk_ref, dh_scr):
        @pl.when(pl.program_id(0) == 0)
        def _():
            dh_scr[...] = jnp.zeros_like(dh_scr)
            dal_ref[...] = jnp.zeros_like(dal_ref)
            ddk_ref[...] = jnp.zeros_like(ddk_ref)

        ddt_ref[...] = jnp.zeros_like(ddt_ref)
        for g in range(N_GROUPS):
            hs, ns = slice(g * hg, (g + 1) * hg), slice(g * n, (g + 1) * n)
            _, vjp = jax.vjp(ssd_chunk, _heads_of(x_ref, g, hg, p), _cols_of(dt_ref, g, hg), b_ref[:, ns],
                             c_ref[:, ns], hp_ref[hs, 0], al_ref[hs], dk_ref[hs])
            gx, gdt, gb, gc, ghp, gal, gdk = vjp((_heads_of(dy_ref, g, hg, p), dh_scr[hs]))
            for i in range(hg):
                dx_ref[:, (g * hg + i) * p:(g * hg + i + 1) * p] = gx[i]
                ddt_ref[:, g * hg + i:g * hg + i + 1] = gdt[i]
            db_ref[:, ns] = gb
            dc_ref[:, ns] = gc
            dh_scr[hs] = ghp
            dal_ref[hs] += gal
            ddk_ref[hs] += gdk

    head = pl.BlockSpec((h, 1, 1), lambda c: (0, 0, 0))
    row = lambda w: pl.BlockSpec((CHUNK, w), lambda c: (nc - 1 - c, 0))
    return pl.pallas_call(
        body, name=name, grid=(nc,),
        in_specs=[head, head, row(h * p), row(dt.shape[1]), row(N_GROUPS * n), row(N_GROUPS * n),
                  pl.BlockSpec((h, 1, p, n), lambda c: (0, nc - 1 - c, 0, 0)), row(h * p)],
        out_specs=[row(h * p), row(dt.shape[1]), row(N_GROUPS * n), row(N_GROUPS * n), head, head],
        out_shape=[S((t, h * p), F32), S(dt.shape, F32), S(bm.shape, F32), S(cm.shape, F32),
                   S((h, 1, 1), F32), S((h, 1, 1), F32)],
        scratch_shapes=[pltpu.VMEM((h, p, n), F32)],
        compiler_params=pltpu.CompilerParams(dimension_semantics=("arbitrary",)),
    )(a_log, dsk, xs, dt, bm, cm, hp, dy)


def _alpha(depth):
    return (2.0 * depth) ** 0.25


def _pad_lanes(v):
    return jnp.pad(v, ((0, 0), (0, LANE - v.shape[1])))


def split_even_weights(w, j):
    d = w["e_w_in"].shape[2]
    da = w["e_conv_a_w"].shape[2]
    db = w["e_norm_b_g"].shape[1]
    gn = N_GROUPS * N_STATE
    nh = w["e_dt_bias"].shape[1]
    main = 2 * da + 2 * db + 2 * gn
    win = w["e_w_in"][j]
    ox = 2 * da + db
    cw, cbias = w["e_conv_b_w"][j], w["e_conv_b_b"][j][None]
    return dict(
        d=d, da=da, db=db, gn=gn, nh=nh, main=main,
        win_main=win[:main], win_dt=jnp.pad(win[main:], ((0, LANE - nh), (0, 0))),
        caw=w["e_conv_a_w"][j], cab=w["e_conv_a_b"][j][None], lag=w["e_ln_a_g"][j][None], lab=w["e_ln_a_b"][j][None],
        cw_xs=cw[:, :db], cw_b=cw[:, db:db + gn], cw_c=cw[:, db + gn:],
        cb_xs=cbias[:, :db], cb_b=cbias[:, db:db + gn], cb_c=cbias[:, db + gn:],
        dt_bias=_pad_lanes(w["e_dt_bias"][j][None]), a_log=w["e_a_log"][j].reshape(nh, 1, 1),
        dsk=w["e_d_skip"][j].reshape(nh, 1, 1), norm_g=w["e_norm_b_g"][j][None],
        wout_a=w["e_w_out"][j][:da], wout_b=w["e_w_out"][j][da:],
    )


def even_fwd(tag, x, xm, lw, ln_g, ln_b, alpha):
    t = x.shape[0]
    da, db, gn, nh = lw["da"], lw["db"], lw["gn"], lw["nh"]
    u = mm(tag + "_win", xm, lw["win_main"], "nt")
    udt = mm(tag + "_windt", xm, lw["win_dt"], "nt")
    ua, ug, z, xs_pre = Win(u, da, 0), Win(u, da, 1), Win(u, db, 2 * da // db), Win(u, db, (2 * da + db) // db)
    b_pre, c_pre = Win(u, gn, (2 * da + 2 * db) // gn), Win(u, gn, (2 * da + 2 * db + gn) // gn)
    (a0,) = rowwise(tag + "_glu", f_glu, [ua, ug], [], [da])
    a1 = conv_fwd(tag + "_conva", a0, lw["caw"], lw["cab"])
    (ya,) = rowwise(tag + "_lna", f_ln_silu, [a1], [lw["lag"], lw["lab"]], [da], out_dtypes=[MXU_DTYPE])
    xs_c = conv_fwd(tag + "_convxs", xs_pre, lw["cw_xs"], lw["cb_xs"])
    b_c = conv_fwd(tag + "_convb", b_pre, lw["cw_b"], lw["cb_b"])
    c_c = conv_fwd(tag + "_convc", c_pre, lw["cw_c"], lw["cb_c"])
    xs, bm, cm = rowwise(tag + "_silu3", f_silu3, [xs_c, b_c, c_c], [], [db, gn, gn])
    (dt,) = rowwise(tag + "_dt", f_softplus, [udt], [lw["dt_bias"]], [LANE])
    yssd, hp = ssd_fwd(tag + "_ssd", xs, dt, bm, cm, lw["a_log"], lw["dsk"])
    (yb,) = rowwise(tag + "_gate", f_gate_rms, [yssd, z], [lw["norm_g"]], [db], out_dtypes=[MXU_DTYPE])
    ma = mm(tag + "_wouta", ya, lw["wout_a"], "nn")
    mb = mm(tag + "_woutb", yb, lw["wout_b"], "nn")

    def f_res(xv, mav, mbv, g, b):
        pre = alpha * xv + mav + mbv
        y = _ln(pre, g, b)
        return y, y, pre

    x1, x1m, pre = rowwise(tag + "_res", f_res, [x, ma, mb], [ln_g, ln_b], [x.shape[1]] * 3,
                           out_dtypes=[F32, MXU_DTYPE, F32])
    saved = dict(xm=xm, u=u, udt=udt, a0=a0, a1=a1, ya=ya, xs_c=xs_c, b_c=b_c, c_c=c_c, xs=xs, dt=dt, bm=bm,
                 cm=cm, hp=hp, yssd=yssd, yb=yb, pre=pre)
    return x1, x1m, saved


def even_bwd(tag, dx1_pieces, sv, lw, ln_g, ln_b, alpha):
    t = sv["u"].shape[0]
    da, db, gn, nh = lw["da"], lw["db"], lw["gn"], lw["nh"]
    u, xm = sv["u"], sv["xm"]
    mx = (MXU_DTYPE,)
    ua, ug, z, xs_pre = Win(u, da, 0), Win(u, da, 1), Win(u, db, 2 * da // db), Win(u, db, (2 * da + db) // db)
    b_pre, c_pre = Win(u, gn, (2 * da + 2 * db) // gn), Win(u, gn, (2 * da + 2 * db + gn) // gn)
    (dpre, dprem), (dg0, db0) = rowwise_bwd(tag + "_res_b", f_ln, [sv["pre"]], [ln_g, ln_b], [dx1_pieces],
                                            d_dtypes=[(F32, MXU_DTYPE)])
    dya = mm(tag + "_dya", dprem, lw["wout_a"], "nt")
    dyb = mm(tag + "_dyb", dprem, lw["wout_b"], "nt")
    dwout_a = mm(tag + "_dwouta", sv["ya"], dprem, "tn", EXCHANGE_DTYPE)
    dwout_b = mm(tag + "_dwoutb", sv["yb"], dprem, "tn", EXCHANGE_DTYPE)
    (dyssd, dz), (dnorm_g,) = rowwise_bwd(tag + "_gate_b", f_gate_rms, [sv["yssd"], z], [lw["norm_g"]], [[dyb]],
                                          d_dtypes=[(F32,), mx])
    dxs, ddt, dbm, dcm, dalog, ddsk = ssd_bwd(tag + "_ssd_b", sv["xs"], sv["dt"], sv["bm"], sv["cm"],
                                              lw["a_log"], lw["dsk"], sv["hp"], dyssd)
    (dudt,), (ddt_bias,) = rowwise_bwd(tag + "_dt_b", f_softplus, [sv["udt"]], [lw["dt_bias"]], [[ddt]],
                                       d_dtypes=[mx])
    (dxs_c, db_c, dc_c), _ = rowwise_bwd(tag + "_silu3_b", f_silu3, [sv["xs_c"], sv["b_c"], sv["c_c"]], [],
                                         [[dxs], [dbm], [dcm]])
    dxs_pre, dcw_xs, dcb_xs = conv_bwd(tag + "_convxs_b", xs_pre, dxs_c, lw["cw_xs"], MXU_DTYPE)
    db_pre, dcw_b, dcb_b = conv_bwd(tag + "_convb_b", b_pre, db_c, lw["cw_b"], MXU_DTYPE)
    dc_pre, dcw_c, dcb_c = conv_bwd(tag + "_convc_b", c_pre, dc_c, lw["cw_c"], MXU_DTYPE)
    (da1,), (dlag, dlab) = rowwise_bwd(tag + "_lna_b", f_ln_silu, [sv["a1"]], [lw["lag"], lw["lab"]], [[dya]])
    da0, dcaw, dcab = conv_bwd(tag + "_conva_b", sv["a0"], da1, lw["caw"])
    (dua, dug), _ = rowwise_bwd(tag + "_glu_b", f_glu, [ua, ug], [], [[da0]], d_dtypes=[mx, mx])
    du = jnp.concatenate([dua, dug, dz, dxs_pre, db_pre, dc_pre], axis=1)
    dx_m = mm(tag + "_dxm", du, lw["win_main"], "nn")
    dx_dt = mm(tag + "_dxdt", dudt, lw["win_dt"], "nn")
    dwin_main = mm(tag + "_dwin", du, xm, "tn", EXCHANGE_DTYPE)
    dwin_dt = mm(tag + "_dwindt", dudt, xm, "tn", EXCHANGE_DTYPE)
    grads = dict(
        e_w_in=jnp.concatenate([dwin_main, dwin_dt[:nh]], axis=0),
        e_conv_a_w=dcaw, e_conv_a_b=dcab[0], e_ln_a_g=dlag[0], e_ln_a_b=dlab[0],
        e_conv_b_w=jnp.concatenate([dcw_xs, dcw_b, dcw_c], axis=1),
        e_conv_b_b=jnp.concatenate([dcb_xs, dcb_b, dcb_c], axis=1)[0],
        e_dt_bias=ddt_bias[0, :nh], e_a_log=dalog.reshape(nh), e_d_skip=ddsk.reshape(nh), e_norm_b_g=dnorm_g[0],
        e_w_out=jnp.concatenate([dwout_a, dwout_b], axis=0), ln_g0=dg0[0], ln_b0=db0[0],
    )
    return [Win(dpre, coef=alpha), dx_m, dx_dt], grads


def odd_fwd(tag, x, xm, w, j, ln_g, ln_b, alpha):
    d = x.shape[1]
    u = mm(tag + "_win", xm, w["o_w_in"][j], "nt")
    bg, cg, v = Win(u, d, 0), Win(u, d, 1), Win(u, d, 2)
    (s,) = rowwise(tag + "_cv", f_mul, [cg, v], [], [d])
    cs = conv_fwd(tag + "_conv", s, w["o_conv_w"][j], None)
    (m,) = rowwise(tag + "_bm", f_mul, [bg, cs], [], [d], out_dtypes=[MXU_DTYPE])
    mix = mm(tag + "_wout", m, w["o_w_out"][j], "nn")

    def f_res(xv, mv, g, b):
        pre = alpha * xv + mv
        y = _ln(pre, g, b)
        return y, y, pre

    x1, x1m, pre = rowwise(tag + "_res", f_res, [x, mix], [ln_g, ln_b], [d] * 3, out_dtypes=[F32, MXU_DTYPE, F32])
    return x1, x1m, dict(xm=xm, u=u, s=s, cs=cs, m=m, pre=pre)


def odd_bwd(tag, dx1_pieces, sv, w, j, ln_g, ln_b, alpha):
    xm, u = sv["xm"], sv["u"]
    d = xm.shape[1]
    mx = (MXU_DTYPE,)
    bg, cg, v = Win(u, d, 0), Win(u, d, 1), Win(u, d, 2)
    (dpre, dprem), (dg0, db0) = rowwise_bwd(tag + "_res_b", f_ln, [sv["pre"]], [ln_g, ln_b], [dx1_pieces],
                                            d_dtypes=[(F32, MXU_DTYPE)])
    dm = mm(tag + "_dm", dprem, w["o_w_out"][j], "nt")
    dwout = mm(tag + "_dwout", sv["m"], dprem, "tn", EXCHANGE_DTYPE)
    (dbg, dcs), _ = rowwise_bwd(tag + "_bm_b", f_mul, [bg, sv["cs"]], [], [[dm]], d_dtypes=[mx, (F32,)])
    ds, dcw, _ = conv_bwd(tag + "_conv_b", sv["s"], dcs, w["o_conv_w"][j])
    (dcg, dv), _ = rowwise_bwd(tag + "_cv_b", f_mul, [cg, v], [], [[ds]], d_dtypes=[mx, mx])
    du = jnp.concatenate([dbg, dcg, dv], axis=1)
    dx_u = mm(tag + "_dx", du, w["o_w_in"][j], "nn")
    dwin = mm(tag + "_dwin", du, xm, "tn", EXCHANGE_DTYPE)
    grads = dict(o_w_in=dwin, o_conv_w=dcw, o_w_out=dwout, ln_g0=dg0[0], ln_b0=db0[0])
    return [Win(dpre, coef=alpha), dx_u], grads


def ffn_fwd(tag, x1, x1m, p_i, w, i, ln_g, ln_b, alpha):
    d = x1.shape[1]
    hpre = mm(tag + "_wup", x1m, w["f_w_up"][i], "nt")
    act = gated_conv_fwd(tag + "_fgate", hpre, w["f_conv_w"][i], w["f_conv_b"][i][None], MXU_DTYPE)
    ffn = mm(tag + "_wdown", act, w["f_w_down"][i], "nn")
    pp = mm(tag + "_pproj", p_i, w["ple_w_proj"][i], "nt")
    gt = mm(tag + "_pgate", x1m, w["ple_w_gate"][i], "nn")

    def f_res2(xv, fv, ppv, gtv, g, b):
        pre = alpha * xv + fv + ppv * _sigmoid(gtv)
        y = _ln(pre, g, b)
        return y, y, pre

    x2, x2m, pre = rowwise(tag + "_res2", f_res2, [x1, ffn, pp, gt], [ln_g, ln_b], [d] * 3,
                           out_dtypes=[F32, MXU_DTYPE, F32])
    return x2, x2m, dict(x1m=x1m, hpre=hpre, act=act, pp=pp, gt=gt, pre=pre)


def ffn_bwd(tag, dx2_pieces, sv, p_i, w, i, ln_g, ln_b, alpha):
    x1m = sv["x1m"]
    mx = (MXU_DTYPE,)
    (dpre, dprem), (dg1, db1) = rowwise_bwd(tag + "_res2_b", f_ln, [sv["pre"]], [ln_g, ln_b], [dx2_pieces],
                                            d_dtypes=[(F32, MXU_DTYPE)])
    (dpp, dgt), _ = rowwise_bwd(tag + "_pg_b", f_gate_mul, [sv["pp"], sv["gt"]], [], [[dpre]], d_dtypes=[mx, mx])
    dwproj = mm(tag + "_dwproj", dpp, p_i, "tn", EXCHANGE_DTYPE)
    dwgate = mm(tag + "_dwgate", x1m, dgt, "tn", EXCHANGE_DTYPE)
    dx1_a = mm(tag + "_dx1a", dgt, w["ple_w_gate"][i], "nt")
    dact = mm(tag + "_dact", dprem, w["f_w_down"][i], "nt")
    dwdown = mm(tag + "_dwdown", sv["act"], dprem, "tn", EXCHANGE_DTYPE)
    dhpre, dfcw, dfcb = gated_conv_bwd(tag + "_fgate_b", sv["hpre"], dact, w["f_conv_w"][i], w["f_conv_b"][i][None],
                                       MXU_DTYPE)
    dwup = mm(tag + "_dwup", dhpre, x1m, "tn", EXCHANGE_DTYPE)
    dx1_b = mm(tag + "_dx1b", dhpre, w["f_w_up"][i], "nn")
    grads = dict(f_w_up=dwup, f_conv_w=dfcw, f_conv_b=dfcb[0], f_w_down=dwdown, ple_w_proj=dwproj, ple_w_gate=dwgate,
                 ln_g1=dg1[0], ln_b1=db1[0])
    return [Win(dpre, coef=alpha), dx1_a, dx1_b], grads


def local_step(x, p, w, target):
    depth = w["ln_g"].shape[0]
    alpha = _alpha(depth)
    d = x.shape[1]
    saved = []
    h = hm = x
    for i in range(depth):
        j = i // 2
        g0, b0, g1, b1 = w["ln_g"][i, 0][None], w["ln_b"][i, 0][None], w["ln_g"][i, 1][None], w["ln_b"][i, 1][None]
        tag = "l%d" % i
        if i % 2 == 0:
            lw = split_even_weights(w, j)
            h, hm, sv_m = even_fwd(tag, h, hm, lw, g0, b0, alpha)
        else:
            lw = None
            h, hm, sv_m = odd_fwd(tag, h, hm, w, j, g0, b0, alpha)
        h, hm, sv_f = ffn_fwd(tag, h, hm, p[i], w, i, g1, b1, alpha)
        saved.append((lw, sv_m, sv_f, (g0, b0, g1, b1)))

    def f_loss(xf, tg):
        diff = xf - tg
        sq = jnp.sum(jnp.sum(jnp.square(diff), axis=1, keepdims=True), axis=0, keepdims=True)
        return diff * (1.0 / d), jnp.broadcast_to(sq, (1, LANE))

    dxf, sq = rowwise("loss", f_loss, [h, target], [], [d], red_widths=[LANE])
    loss = sq[0, 0] * (0.5 / d)

    per_layer = []
    pieces = [dxf]
    for i in reversed(range(depth)):
        j = i // 2
        lw, sv_m, sv_f, (g0, b0, g1, b1) = saved[i]
        tag = "l%d" % i
        pieces, gf = ffn_bwd(tag, pieces, sv_f, p[i], w, i, g1, b1, alpha)
        if i % 2 == 0:
            pieces, gm = even_bwd(tag, pieces, sv_m, lw, g0, b0, alpha)
        else:
            pieces, gm = odd_bwd(tag, pieces, sv_m, w, j, g0, b0, alpha)
        per_layer.append((i, gm, gf))

    def f_sum(*vs):
        acc = None
        for v, c in zip(vs, [pc.coef for pc in map(_win, pieces)]):
            v = v if c == 1.0 else v * c
            acc = v if acc is None else acc + v
        return (acc,)

    (grad_x,) = rowwise("grad_x", f_sum, [Win(_win(pc).arr) for pc in pieces], [], [d])

    by_layer = {i: (gm, gf) for i, gm, gf in per_layer}
    grads = {}
    n_even, n_odd = (depth + 1) // 2, depth // 2
    collect = lambda name, per_layer: per_layer if name in BIG else jnp.stack(per_layer)
    for name in ("e_w_in", "e_conv_a_w", "e_conv_a_b", "e_ln_a_g", "e_ln_a_b", "e_conv_b_w", "e_conv_b_b", "e_dt_bias",
                 "e_a_log", "e_d_skip", "e_norm_b_g", "e_w_out"):
        grads[name] = collect(name, [by_layer[2 * j][0][name] for j in range(n_even)])
    for name in ("o_w_in", "o_conv_w", "o_w_out"):
        grads[name] = collect(name, [by_layer[2 * j + 1][0][name] for j in range(n_odd)])
    for name in ("f_w_up", "f_conv_w", "f_conv_b", "f_w_down", "ple_w_proj", "ple_w_gate"):
        grads[name] = collect(name, [by_layer[i][1][name] for i in range(depth)])
    grads["ln_g"] = jnp.stack([jnp.stack([by_layer[i][0]["ln_g0"], by_layer[i][1]["ln_g1"]]) for i in range(depth)])
    grads["ln_b"] = jnp.stack([jnp.stack([by_layer[i][0]["ln_b0"], by_layer[i][1]["ln_b1"]]) for i in range(depth)])
    return loss, grad_x, grads


_ANY = pl.BlockSpec(memory_space=pl.ANY)
_MESH = pl.DeviceIdType.MESH


def all_gather(name, xl):
    r, c_ = xl.shape

    def body(x_ref, out_ref, send_sems, recv_sems, local_sem):
        x, y, c = lax.axis_index("x"), lax.axis_index("y"), lax.axis_index("c")
        me, sibling = (x, y, c), (x, y, 1 - c)
        chips = [(1 - x, y), (x, 1 - y), (1 - x, 1 - y)]

        def slot(px, py, pc):
            return out_ref.at[4 * px + 2 * py + pc]

        def copy(k, block, to, src=None):
            return pltpu.make_async_remote_copy(
                src_ref=slot(*block) if src is None else src, dst_ref=slot(*block),
                send_sem=send_sems.at[k], recv_sem=recv_sems.at[k], device_id=to, device_id_type=_MESH)

        mine = pltpu.make_async_copy(x_ref, slot(*me), local_sem)
        mine.start()
        first = [copy(0, me, sibling, src=x_ref)]
        first += [copy(1 + j, me, (*chip, c), src=x_ref) for j, chip in enumerate(chips)]
        for cp in first:
            cp.start()
        passed = [copy(4 + j, (*chip, c), sibling) for j, chip in enumerate(chips)]
        for j, chip in enumerate(chips):
            copy(1 + j, (*chip, c), me).wait_recv()
            passed[j].start()
        copy(0, sibling, me).wait_recv()
        for j, chip in enumerate(chips):
            copy(4 + j, (*chip, 1 - c), me).wait_recv()
        for cp in first + passed:
            cp.wait_send()
        mine.wait()

    return pl.pallas_call(
        body, name=name, out_shape=S((N_DEV, r, c_), xl.dtype), in_specs=[_ANY], out_specs=_ANY,
        scratch_shapes=[pltpu.SemaphoreType.DMA((7,)), pltpu.SemaphoreType.DMA((7,)), pltpu.SemaphoreType.DMA],
    )(xl)


def exchange_sibling(name, g4):
    _, _, r, c_ = g4.shape

    def body(g_ref, out_ref, send_sems, recv_sems):
        x, y, c = lax.axis_index("x"), lax.axis_index("y"), lax.axis_index("c")
        copies = [pltpu.make_async_remote_copy(
            src_ref=g_ref.at[q, 1 - c], dst_ref=out_ref.at[q], send_sem=send_sems.at[q], recv_sem=recv_sems.at[q],
            device_id=(x, y, 1 - c), device_id_type=_MESH) for q in range(4)]
        for cp in copies:
            cp.start()
        for cp in copies:
            cp.wait()

    return pl.pallas_call(
        body, name=name, out_shape=S((4, r, c_), g4.dtype), in_specs=[_ANY], out_specs=_ANY,
        scratch_shapes=[pltpu.SemaphoreType.DMA((4,)), pltpu.SemaphoreType.DMA((4,))],
    )(g4)


def exchange_chips(name, p4):
    _, r, c_ = p4.shape

    def body(p_ref, out_ref, send_sems, recv_sems):
        x, y, c = lax.axis_index("x"), lax.axis_index("y"), lax.axis_index("c")
        chips = [(1 - x, y), (x, 1 - y), (1 - x, 1 - y)]
        copies = [pltpu.make_async_remote_copy(
            src_ref=p_ref.at[2 * cx + cy], dst_ref=out_ref.at[k], send_sem=send_sems.at[k], recv_sem=recv_sems.at[k],
            device_id=(cx, cy, c), device_id_type=_MESH) for k, (cx, cy) in enumerate(chips)]
        for cp in copies:
            cp.start()
        for cp in copies:
            cp.wait()

    return pl.pallas_call(
        body, name=name, out_shape=S((3, r, c_), p4.dtype), in_specs=[_ANY], out_specs=_ANY,
        scratch_shapes=[pltpu.SemaphoreType.DMA((3,)), pltpu.SemaphoreType.DMA((3,))],
    )(p4)


def sum_with_sibling(name, g4, recv, core):
    _, _, r, c_ = g4.shape
    tr = _pick(r, (256, 128, 64, 32, 16, 8))

    def body(core_ref, g_ref, r_ref, o_ref, ox_ref):
        s = g_ref[0].astype(F32) + r_ref[...].astype(F32)
        o_ref[...] = s
        ox_ref[...] = s.astype(ox_ref.dtype)

    out_spec = pl.BlockSpec((1, tr, c_), lambda q, i, cr: (q, i, 0))
    return pl.pallas_call(
        body, name=name, out_shape=[S((4, r, c_), F32), S((4, r, c_), EXCHANGE_DTYPE)],
        grid_spec=pltpu.PrefetchScalarGridSpec(
            num_scalar_prefetch=1, grid=(4, r // tr),
            in_specs=[pl.BlockSpec((1, 1, tr, c_), lambda q, i, cr: (q, cr[0], i, 0)),
                      pl.BlockSpec((1, tr, c_), lambda q, i, cr: (q, i, 0))],
            out_specs=[out_spec, out_spec]),
    )(core, g4, recv)


def sum_with_chips(name, p4, recv, chip):
    _, r, c_ = p4.shape
    tr = _pick(r, (256, 128, 64, 32, 16, 8))

    def body(chip_ref, p_ref, r_ref, o_ref):
        o_ref[...] = ((p_ref[0] + r_ref[0].astype(F32)) + r_ref[1].astype(F32)) + r_ref[2].astype(F32)

    return pl.pallas_call(
        body, name=name, out_shape=S((r, c_), F32),
        grid_spec=pltpu.PrefetchScalarGridSpec(
            num_scalar_prefetch=1, grid=(r // tr,),
            in_specs=[pl.BlockSpec((1, tr, c_), lambda i, ch: (ch[0], i, 0)),
                      pl.BlockSpec((3, tr, c_), lambda i, ch: (0, i, 0))],
            out_specs=pl.BlockSpec((tr, c_), lambda i, ch: (i, 0))),
    )(chip, p4, recv)


def sum_devices(name, g8):
    _, r, c_ = g8.shape

    def body(g_ref, o_ref):
        acc = g_ref[0]
        for k in range(1, N_DEV):
            acc = acc + g_ref[k]
        o_ref[...] = acc

    return pl.pallas_call(body, name=name, out_shape=S((r, c_), F32))(g8)


def _flatten(parts, cols, row_mult):
    flat = jnp.concatenate([v.reshape(-1) for v in parts])
    n = flat.shape[0]
    rows = -(-n // (cols * row_mult)) * row_mult
    return jnp.pad(flat, (0, rows * cols - n)).reshape(rows, cols)


def _exchange_dims(name, lshape):
    l, r, c = lshape
    return (l, c, r) if name in TRANSPOSED else (l, r, c)


def gather_big(name, local):
    cols = local["e_w_out"].shape[2]
    parts = []
    for n in EXCHANGE_ORDER:
        v = local[n].astype(MXU_DTYPE)
        parts.append((v.transpose(0, 2, 1) if n in TRANSPOSED else v).reshape(-1, cols))
    got = all_gather(name, jnp.concatenate(parts, axis=0))
    full, r0 = {}, 0
    for n, part in zip(EXCHANGE_ORDER, parts):
        l, a, b = _exchange_dims(n, local[n].shape)
        seg = got[:, r0:r0 + part.shape[0]].reshape(N_DEV, l, a, b)
        full[n] = seg.transpose(1, 0, 2, 3).reshape(l, N_DEV * a, b)
        r0 += part.shape[0]
    return full


def reduce_scatter_big(grads, local_shapes):
    cols = local_shapes["e_w_out"][2]
    items, spans = [], []
    n_rows = 0
    for n in EXCHANGE_ORDER:
        per_layer = [g.reshape(N_DEV, -1, cols) for g in grads[n]]
        a = per_layer[0].shape[1]
        if a % RS_ROW_ALIGN:
            pad = -(-a // RS_ROW_PAD) * RS_ROW_PAD - a
            per_layer = [jnp.pad(pc, ((0, 0), (0, pad), (0, 0))) for pc in per_layer]
        spans.append((n, n_rows, per_layer[0].shape[1]))
        items += per_layer
        n_rows += sum(pc.shape[1] for pc in per_layer)
    rows = -(-n_rows // RS_ROW_PAD) * RS_ROW_PAD
    if rows > n_rows:
        items.append(jnp.zeros((N_DEV, rows - n_rows, cols), EXCHANGE_DTYPE))
    g4 = pack_rows("rs_pack", items, rows).reshape(4, 2, rows, cols)
    x, y, c = lax.axis_index("x"), lax.axis_index("y"), lax.axis_index("c")
    from_sibling = exchange_sibling("rs_sibling", g4)
    p4, p4x = sum_with_sibling("rs_sum_sibling", g4, from_sibling, c.astype(jnp.int32).reshape(1))
    from_chips = exchange_chips("rs_chips", p4x)
    mine = sum_with_chips("rs_sum_chips", p4, from_chips, (2 * x + y).astype(jnp.int32).reshape(1))
    out = {}
    for n, r0, stride in spans:
        l, a, b = _exchange_dims(n, local_shapes[n])
        seg = mine[r0:r0 + l * stride].reshape(l, stride, cols)[:, :a * b // cols].reshape(l, a, b)
        out[n] = seg.transpose(0, 2, 1) if n in TRANSPOSED else seg
    return out


def pack_rows(name, items, rows):
    cols = items[0].shape[2]
    n_items = len(items)
    assert sum(it.shape[1] for it in items) == rows

    def body(*refs):
        out_ref, sems = refs[n_items], refs[n_items + 1]
        copies, r0 = [], 0
        for k, src in enumerate(refs[:n_items]):
            a = src.shape[1]
            copies.append(pltpu.make_async_copy(src, out_ref.at[:, pl.ds(r0, a), :], sems.at[k]))
            r0 += a
        for cp in copies:
            cp.start()
        for cp in copies:
            cp.wait()

    return pl.pallas_call(
        body, name=name, out_shape=S((N_DEV, rows, cols), items[0].dtype), in_specs=[_ANY] * n_items, out_specs=_ANY,
        scratch_shapes=[pltpu.SemaphoreType.DMA((n_items,))],
    )(*items)


def gather_small(name, local, names):
    flat = _flatten([local[n] for n in names], LANE, 1)
    got = all_gather(name, flat).reshape(N_DEV, -1)
    full, off = {}, 0
    for n in names:
        size = math.prod(local[n].shape)
        seg = got[:, off:off + size].reshape((N_DEV,) + local[n].shape)
        full[n] = seg.transpose(1, 2, 0, 3).reshape(seg.shape[1], seg.shape[2], -1)
        off += size
    return full


def all_reduce_small(grads, names):
    flat = _flatten([grads[n] for n in names], LANE, SUBLANE)
    total = sum_devices("ar_sum", all_gather("ar_gather", flat)).reshape(-1)
    out, off = {}, 0
    for nm in names:
        size = math.prod(grads[nm].shape)
        out[nm] = total[off:off + size].reshape(grads[nm].shape)
        off += size
    return out


def adamw(name, w, g, m, v):
    shape = w.shape
    cols = shape[-1]
    rows = math.prod(shape[:-1])
    tr = _pick(rows, (256, 128, 64, 32, 16, 8)) if rows * cols > 256 * 1024 else rows
    c1 = 1.0 - ADAM_B1 ** ADAM_STEP
    c2 = 1.0 - ADAM_B2 ** ADAM_STEP

    def body(w_ref, g_ref, m_ref, v_ref, d_ref, nm_ref, nv_ref):
        gv = g_ref[...]
        m2 = ADAM_B1 * m_ref[...] + (1.0 - ADAM_B1) * gv
        v2 = ADAM_B2 * v_ref[...] + (1.0 - ADAM_B2) * jnp.square(gv)
        d_ref[...] = -ADAM_LR * ((m2 / c1) / (jnp.sqrt(v2 / c2) + ADAM_EPS) + ADAM_WD * w_ref[...])
        nm_ref[...] = m2
        nv_ref[...] = v2

    spec = pl.BlockSpec((tr, cols), lambda i: (i, 0))
    outs = pl.pallas_call(
        body, name=name, grid=(rows // tr,), in_specs=[spec] * 4, out_specs=[spec] * 3,
        out_shape=[S((rows, cols), F32)] * 3,
        compiler_params=pltpu.CompilerParams(dimension_semantics=("parallel",)),
    )(*[a.reshape(rows, cols) for a in (w, g, m, v)])
    return tuple(o.reshape(shape) for o in outs)


def kernel(x, p, e_w_in, e_conv_a_w, e_conv_a_b, e_ln_a_g, e_ln_a_b, e_conv_b_w, e_conv_b_b, e_dt_bias, e_a_log, e_d_skip, e_norm_b_g, e_w_out, o_w_in, o_conv_w, o_w_out, f_w_up, f_conv_w, f_conv_b, f_w_down, ple_w_proj, ple_w_gate, ln_g, ln_b, loss_target, m_e_w_in, m_e_conv_a_w, m_e_conv_a_b, m_e_ln_a_g, m_e_ln_a_b, m_e_conv_b_w, m_e_conv_b_b, m_e_dt_bias, m_e_a_log, m_e_d_skip, m_e_norm_b_g, m_e_w_out, m_o_w_in, m_o_conv_w, m_o_w_out, m_f_w_up, m_f_conv_w, m_f_conv_b, m_f_w_down, m_ple_w_proj, m_ple_w_gate, m_ln_g, m_ln_b, v_e_w_in, v_e_conv_a_w, v_e_conv_a_b, v_e_ln_a_g, v_e_ln_a_b, v_e_conv_b_w, v_e_conv_b_b, v_e_dt_bias, v_e_a_log, v_e_d_skip, v_e_norm_b_g, v_e_w_out, v_o_w_in, v_o_conv_w, v_o_w_out, v_f_w_up, v_f_conv_w, v_f_conv_b, v_f_w_down, v_ple_w_proj, v_ple_w_gate, v_ln_g, v_ln_b):
    args = locals()
    local = {n: args[n] for n in WEIGHTS}
    mom = {n: args["m_" + n] for n in WEIGHTS}
    var = {n: args["v_" + n] for n in WEIGHTS}

    full = {n: local[n] for n in REPLICATED}
    full.update(gather_big("ag_big", local))
    full.update(gather_small("ag_small", local, SMALL_SHARDED))

    loss_local, grad_x, grads = local_step(x[0], p[:, 0], full, loss_target[0])
    loss = lax.psum(loss_local, MESH_AXES)

    g_local = reduce_scatter_big(grads, {n: local[n].shape for n in BIG})
    small = all_reduce_small(grads, REPLICATED + SMALL_SHARDED)
    dev = 4 * lax.axis_index("x") + 2 * lax.axis_index("y") + lax.axis_index("c")
    for n in REPLICATED:
        g_local[n] = small[n]
    for n in SMALL_SHARDED:
        width = local[n].shape[2]
        g_local[n] = lax.dynamic_slice_in_dim(small[n], dev * width, width, axis=2)

    delta, new_m, new_v = {}, {}, {}
    for n in WEIGHTS:
        delta[n], new_m[n], new_v[n] = adamw("adamw_" + n, local[n], g_local[n], mom[n], var[n])
    return (loss, grad_x[None], *[g_local[n] for n in WEIGHTS], *[delta[n] for n in WEIGHTS],
            *[new_m[n] for n in WEIGHTS], *[new_v[n] for n in WEIGHTS])
```

```python
import functools
import math

import jax
import jax.numpy as jnp
from jax import lax
from jax.experimental import pallas as pl
from jax.experimental.pallas import tpu as pltpu

F32 = jnp.float32
MXU_DTYPE = jnp.bfloat16
MESH_AXES = ("x", "y", "c")
N_DEV = 8
LANE = 128
SUBLANE = 8
ROWWISE_VMEM_BUDGET = 20 * 1024 * 1024
MM_TILES = (1408, 1024, 512, 256, 128)
EXCHANGE_DTYPE = jnp.bfloat16
LN_EPS = 1e-5
CHUNK = 64
HEAD_DIM = 64
N_GROUPS = 4
N_STATE = 128
CONV_PAD = 32
CONV_ROWS = 256
ADAM_LR, ADAM_B1, ADAM_B2, ADAM_EPS, ADAM_WD, ADAM_STEP = 0.001, 0.9, 0.999, 1e-08, 0.01, 10

BIG = ("e_w_in", "e_w_out", "o_w_in", "o_w_out", "f_w_up", "f_w_down", "ple_w_proj", "ple_w_gate")
SMALL_SHARDED = ("e_conv_a_w", "e_conv_b_w", "o_conv_w", "f_conv_w", "ln_g", "ln_b")
REPLICATED = ("e_conv_a_b", "e_ln_a_g", "e_ln_a_b", "e_conv_b_b", "e_dt_bias", "e_a_log", "e_d_skip",
              "e_norm_b_g", "f_conv_b")
TRANSPOSED = ("e_w_in", "o_w_in", "f_w_up", "ple_w_proj")
EXCHANGE_ORDER = ("e_w_out", "o_w_out", "ple_w_gate", "ple_w_proj", "o_w_in", "f_w_down", "f_w_up", "e_w_in")
WEIGHTS = ("e_w_in", "e_conv_a_w", "e_conv_a_b", "e_ln_a_g", "e_ln_a_b", "e_conv_b_w", "e_conv_b_b", "e_dt_bias",
           "e_a_log", "e_d_skip", "e_norm_b_g", "e_w_out", "o_w_in", "o_conv_w", "o_w_out", "f_w_up", "f_conv_w",
           "f_conv_b", "f_w_down", "ple_w_proj", "ple_w_gate", "ln_g", "ln_b")

S = jax.ShapeDtypeStruct


class Win:
    def __init__(self, arr, w=None, idx=0, coef=1.0):
        self.arr, self.w, self.idx, self.coef = arr, (arr.shape[1] if w is None else w), idx, coef


def _win(a):
    return a if isinstance(a, Win) else Win(a)


def _pick(n, prefs):
    for p in prefs:
        if p <= n and n % p == 0:
            return p
    return n


_MM_DIMS = {"nn": (1, 0), "nt": (1, 1), "tn": (0, 0)}


def mm(name, a, b, mode, out_dtype=F32):
    ca, cb = _MM_DIMS[mode]
    kdim = a.shape[ca]
    m = a.shape[1 - ca]
    n = b.shape[1 - cb]
    assert b.shape[cb] == kdim, (name, a.shape, b.shape, mode)
    tm = _pick(m, MM_TILES)
    tn = _pick(n, MM_TILES)
    tk = kdim if kdim <= MM_TILES[0] else _pick(kdim, MM_TILES)
    nk = kdim // tk
    own_acc = nk > 1 and out_dtype != F32

    def body(a_ref, b_ref, o_ref, *scratch):
        acc_ref = scratch[0] if own_acc else o_ref
        d = lax.dot_general(a_ref[...].astype(MXU_DTYPE), b_ref[...].astype(MXU_DTYPE),
                            (((ca,), (cb,)), ((), ())), preferred_element_type=F32)
        if nk == 1:
            o_ref[...] = d.astype(o_ref.dtype)
        else:
            k = pl.program_id(2)

            @pl.when(k == 0)
            def _():
                acc_ref[...] = d

            @pl.when(k > 0)
            def _():
                acc_ref[...] += d

            if own_acc:
                @pl.when(k == nk - 1)
                def _():
                    o_ref[...] = acc_ref[...].astype(o_ref.dtype)

    a_spec = pl.BlockSpec((tm, tk), lambda i, j, k: (i, k)) if ca == 1 else pl.BlockSpec((tk, tm), lambda i, j, k: (k, i))
    b_spec = pl.BlockSpec((tk, tn), lambda i, j, k: (k, j)) if cb == 0 else pl.BlockSpec((tn, tk), lambda i, j, k: (j, k))
    return pl.pallas_call(
        body, name=name, grid=(m // tm, n // tn, nk),
        in_specs=[a_spec, b_spec], out_specs=pl.BlockSpec((tm, tn), lambda i, j, k: (i, j)),
        out_shape=S((m, n), out_dtype), scratch_shapes=[pltpu.VMEM((tm, tn), F32)] if own_acc else [],
        compiler_params=pltpu.CompilerParams(dimension_semantics=("parallel", "parallel", "arbitrary")),
    )(a, b)


def _row_block(t, widths):
    tb = 512
    while tb > SUBLANE and (t % tb or tb * sum(widths) * 8 > ROWWISE_VMEM_BUDGET):
        tb //= 2
    return tb


def _tok_spec(tb, w):
    return pl.BlockSpec((tb, w.w), functools.partial(lambda i, idx: (i, idx), idx=w.idx))


def _par_spec(p):
    return pl.BlockSpec((1, p.shape[1]), lambda i: (0, 0))


def rowwise(name, fn, tok, par, out_widths, red_widths=(), out_dtypes=None):
    tok = [_win(t) for t in tok]
    t = tok[0].arr.shape[0]
    tb = _row_block(t, [w.w for w in tok] + list(out_widths))
    n_tok, n_par, n_out = len(tok), len(par), len(out_widths)
    out_dtypes = [F32] * n_out if out_dtypes is None else out_dtypes

    def body(*refs):
        ins = [r[...] for r in refs[:n_tok + n_par]]
        res = fn(*ins)
        out_refs = refs[n_tok + n_par:n_tok + n_par + n_out]
        red_refs = refs[n_tok + n_par + n_out:]
        for r, v in zip(out_refs, res[:n_out]):
            r[...] = v.astype(r.dtype)
        if red_refs:
            @pl.when(pl.program_id(0) == 0)
            def _():
                for r in red_refs:
                    r[...] = jnp.zeros_like(r)
            for r, v in zip(red_refs, res[n_out:]):
                r[...] += v

    outs = pl.pallas_call(
        body, name=name, grid=(t // tb,),
        in_specs=[_tok_spec(tb, w) for w in tok] + [_par_spec(p) for p in par],
        out_specs=[pl.BlockSpec((tb, w), lambda i: (i, 0)) for w in out_widths]
        + [pl.BlockSpec((1, w), lambda i: (0, 0)) for w in red_widths],
        out_shape=[S((t, w), dt) for w, dt in zip(out_widths, out_dtypes)] + [S((1, w), F32) for w in red_widths],
        compiler_params=pltpu.CompilerParams(dimension_semantics=("arbitrary",)),
    )(*[w.arr for w in tok], *par)
    return outs


def rowwise_bwd(name, fn, tok, par, cts, d_dtypes=None):
    tok = [_win(t) for t in tok]
    cts = [[_win(c) for c in group] for group in cts]
    d_dtypes = [(F32,)] * len(tok) if d_dtypes is None else d_dtypes
    t = tok[0].arr.shape[0]
    flat_cts = [c for group in cts for c in group]
    d_outs = [(i, w.w, dt) for i, (w, dts) in enumerate(zip(tok, d_dtypes)) for dt in dts]
    tb = _row_block(t, [w.w for w in tok] + [c.w for c in flat_cts] + [w for _, w, _ in d_outs])
    n_tok, n_par, n_ct, n_d = len(tok), len(par), len(flat_cts), len(d_outs)

    def body(*refs):
        tok_vals = [r[...] for r in refs[:n_tok]]
        par_vals = [r[...] for r in refs[n_tok:n_tok + n_par]]
        ct_refs = refs[n_tok + n_par:n_tok + n_par + n_ct]
        d_refs = refs[n_tok + n_par + n_ct:n_tok + n_par + n_ct + n_d]
        dp_refs = refs[n_tok + n_par + n_ct + n_d:]
        ct_vals, pos = [], 0
        for group in cts:
            acc = None
            for c in group:
                v = ct_refs[pos][...]
                if c.coef != 1.0:
                    v = v * c.coef
                acc = v if acc is None else acc + v
                pos += 1
            ct_vals.append(acc)
        _, vjp = jax.vjp(lambda *a: tuple(fn(*a)), *tok_vals, *par_vals)
        grads = vjp(tuple(ct_vals))
        for r, (i, _, _) in zip(d_refs, d_outs):
            r[...] = grads[i].astype(r.dtype)
        if dp_refs:
            @pl.when(pl.program_id(0) == 0)
            def _():
                for r in dp_refs:
                    r[...] = jnp.zeros_like(r)
            for r, v in zip(dp_refs, grads[n_tok:]):
                r[...] += v

    outs = pl.pallas_call(
        body, name=name, grid=(t // tb,),
        in_specs=[_tok_spec(tb, w) for w in tok] + [_par_spec(p) for p in par] + [_tok_spec(tb, c) for c in flat_cts],
        out_specs=[pl.BlockSpec((tb, w), lambda i: (i, 0)) for _, w, _ in d_outs] + [_par_spec(p) for p in par],
        out_shape=[S((t, w), dt) for _, w, dt in d_outs] + [S(p.shape, F32) for p in par],
        compiler_params=pltpu.CompilerParams(dimension_semantics=("arbitrary",)),
    )(*[w.arr for w in tok], *par, *[c.arr for c in flat_cts])
    return outs[:n_d], outs[n_d:]


def _sigmoid(x):
    return 1.0 / (1.0 + jnp.exp(-x))


def _silu(x):
    return x * _sigmoid(x)


def _ln(x, g, b):
    mu = jnp.mean(x, axis=-1, keepdims=True)
    var = jnp.mean(jnp.square(x - mu), axis=-1, keepdims=True)
    return (x - mu) * lax.rsqrt(var + LN_EPS) * g + b


def f_glu(ua, ug):
    return (ua * _sigmoid(ug),)


def f_ln_silu(a1, g, b):
    return (_silu(_ln(a1, g, b)),)


def f_silu3(a, b, c):
    return (_silu(a), _silu(b), _silu(c))


def f_softplus(dt_raw, bias):
    return (jax.nn.softplus(dt_raw + bias),)


def f_gate_rms(yssd, z, g):
    y = yssd * _silu(z)
    return (y * lax.rsqrt(jnp.mean(jnp.square(y), axis=-1, keepdims=True) + LN_EPS) * g,)


def f_ln(pre, g, b):
    return (_ln(pre, g, b),)


def f_mul(a, b):
    return (a * b,)


def f_gate_mul(pp, gt):
    return (pp * _sigmoid(gt),)


def conv_fwd(name, x, w, b):
    x = _win(x)
    t, c = x.arr.shape[0], x.w
    kw = w.shape[0]
    cb = LANE
    off = x.idx * (c // cb)
    rows = min(CONV_ROWS, t)
    has_b = b is not None

    def body(*refs):
        if has_b:
            x_ref, w_ref, b_ref, y_ref, xp_ref = refs
        else:
            x_ref, w_ref, y_ref, xp_ref = refs
        xp_ref[0:CONV_PAD, :] = jnp.zeros((CONV_PAD, cb), F32)
        xp_ref[CONV_PAD:CONV_PAD + t, :] = x_ref[...]

        def step(s, carry):
            base = pl.multiple_of(s * rows, rows)
            acc = jnp.zeros((rows, cb), F32)
            if has_b:
                acc = acc + b_ref[...]
            for k in range(kw):
                acc = acc + w_ref[k:k + 1, :] * xp_ref[pl.ds(base + CONV_PAD - (kw - 1) + k, rows), :]
            y_ref[pl.ds(base, rows), :] = acc
            return carry

        lax.fori_loop(0, t // rows, step, 0)

    in_specs = [pl.BlockSpec((t, cb), lambda j: (0, off + j)), pl.BlockSpec((kw, cb), lambda j: (0, j))]
    args = [x.arr, w]
    if has_b:
        in_specs.append(pl.BlockSpec((1, cb), lambda j: (0, j)))
        args.append(b)
    return pl.pallas_call(
        body, name=name, grid=(c // cb,), in_specs=in_specs,
        out_specs=pl.BlockSpec((t, cb), lambda j: (0, j)), out_shape=S((t, c), F32),
        scratch_shapes=[pltpu.VMEM((CONV_PAD + t, cb), F32)],
        compiler_params=pltpu.CompilerParams(dimension_semantics=("parallel",)),
    )(*args)


def conv_bwd(name, x, dy, w, dx_dtype=F32):
    x, dy = _win(x), _win(dy)
    t, c = x.arr.shape[0], x.w
    kw = w.shape[0]
    cb = LANE
    xoff = x.idx * (c // cb)
    dyoff = dy.idx * (c // cb)
    rows = min(CONV_ROWS, t)

    def body(x_ref, dy_ref, w_ref, dx_ref, dw_ref, db_ref, xp_ref, dyp_ref):
        xp_ref[0:CONV_PAD, :] = jnp.zeros((CONV_PAD, cb), F32)
        xp_ref[CONV_PAD:CONV_PAD + t, :] = x_ref[...]
        dyp_ref[0:t, :] = dy_ref[...]
        dyp_ref[t:t + CONV_PAD, :] = jnp.zeros((CONV_PAD, cb), F32)

        def fold(v):
            return jnp.sum(v.reshape(rows // SUBLANE, SUBLANE, cb), axis=0)

        def step(s, carry):
            base = pl.multiple_of(s * rows, rows)
            dyc = dy_ref[pl.ds(base, rows), :]
            acc = jnp.zeros((rows, cb), F32)
            new = []
            for k in range(kw):
                acc = acc + w_ref[k:k + 1, :] * dyp_ref[pl.ds(base + (kw - 1) - k, rows), :]
                new.append(carry[k] + fold(dyc * xp_ref[pl.ds(base + CONV_PAD - (kw - 1) + k, rows), :]))
            new.append(carry[kw] + fold(dyc))
            dx_ref[pl.ds(base, rows), :] = acc.astype(dx_ref.dtype)
            return tuple(new)

        init = tuple(jnp.zeros((SUBLANE, cb), F32) for _ in range(kw + 1))
        parts = lax.fori_loop(0, t // rows, step, init)
        for k in range(kw):
            dw_ref[k:k + 1, :] = jnp.sum(parts[k], axis=0, keepdims=True)
        db_ref[...] = jnp.sum(parts[kw], axis=0, keepdims=True)

    return pl.pallas_call(
        body, name=name, grid=(c // cb,),
        in_specs=[pl.BlockSpec((t, cb), lambda j: (0, xoff + j)), pl.BlockSpec((t, cb), lambda j: (0, dyoff + j)),
                  pl.BlockSpec((kw, cb), lambda j: (0, j))],
        out_specs=[pl.BlockSpec((t, cb), lambda j: (0, j)), pl.BlockSpec((kw, cb), lambda j: (0, j)),
                   pl.BlockSpec((1, cb), lambda j: (0, j))],
        out_shape=[S((t, c), dx_dtype), S((kw, c), F32), S((1, c), F32)],
        scratch_shapes=[pltpu.VMEM((CONV_PAD + t, cb), F32), pltpu.VMEM((CONV_PAD + t, cb), F32)],
        compiler_params=pltpu.CompilerParams(dimension_semantics=("parallel",)),
    )(x.arr, dy.arr, w)


def gated_conv_fwd(name, hpre, w, b, out_dtype):
    t, c2 = hpre.shape
    ff = c2 // 2
    kw = w.shape[0]
    cb = LANE
    nb = ff // cb
    rows = min(CONV_ROWS, t)

    def body(h1_ref, h2_ref, w1_ref, w2_ref, b1_ref, b2_ref, y_ref, xp1_ref, xp2_ref):
        for xp_ref, h_ref in ((xp1_ref, h1_ref), (xp2_ref, h2_ref)):
            xp_ref[0:CONV_PAD, :] = jnp.zeros((CONV_PAD, cb), F32)
            xp_ref[CONV_PAD:CONV_PAD + t, :] = h_ref[...]

        def step(s, carry):
            base = pl.multiple_of(s * rows, rows)
            h1 = jnp.zeros((rows, cb), F32) + b1_ref[...]
            h2 = jnp.zeros((rows, cb), F32) + b2_ref[...]
            for k in range(kw):
                at = pl.ds(base + CONV_PAD - (kw - 1) + k, rows)
                h1 = h1 + w1_ref[k:k + 1, :] * xp1_ref[at, :]
                h2 = h2 + w2_ref[k:k + 1, :] * xp2_ref[at, :]
            y_ref[pl.ds(base, rows), :] = (_silu(h1) * h2).astype(y_ref.dtype)
            return carry

        lax.fori_loop(0, t // rows, step, 0)

    col1 = lambda r: pl.BlockSpec((r, cb), lambda j: (0, j))
    col2 = lambda r: pl.BlockSpec((r, cb), lambda j: (0, nb + j))
    return pl.pallas_call(
        body, name=name, grid=(nb,),
        in_specs=[col1(t), col2(t), col1(kw), col2(kw), col1(1), col2(1)],
        out_specs=col1(t), out_shape=S((t, ff), out_dtype),
        scratch_shapes=[pltpu.VMEM((CONV_PAD + t, cb), F32)] * 2,
        compiler_params=pltpu.CompilerParams(dimension_semantics=("parallel",)),
    )(hpre, hpre, w, w, b, b)


def gated_conv_bwd(name, hpre, dact, w, b, dx_dtype):
    t, c2 = hpre.shape
    ff = c2 // 2
    kw = w.shape[0]
    cb = LANE
    nb = ff // cb
    rows = min(CONV_ROWS, t)

    def body(own_ref, oth_ref, da_ref, wo_ref, wt_ref, bo_ref, bt_ref, dx_ref, dw_ref, db_ref,
             xpo_ref, xpt_ref, dhp_ref):
        for xp_ref, h_ref in ((xpo_ref, own_ref), (xpt_ref, oth_ref)):
            xp_ref[0:CONV_PAD, :] = jnp.zeros((CONV_PAD, cb), F32)
            xp_ref[CONV_PAD:CONV_PAD + t, :] = h_ref[...]
        dhp_ref[t:t + CONV_PAD, :] = jnp.zeros((CONV_PAD, cb), F32)

        def fold(v):
            return jnp.sum(v.reshape(rows // SUBLANE, SUBLANE, cb), axis=0)

        def first_pass(own_is_gate):
            def step(s, carry):
                base = pl.multiple_of(s * rows, rows)
                ho = jnp.zeros((rows, cb), F32) + bo_ref[...]
                ht = jnp.zeros((rows, cb), F32) + bt_ref[...]
                for k in range(kw):
                    at = pl.ds(base + CONV_PAD - (kw - 1) + k, rows)
                    ho = ho + wo_ref[k:k + 1, :] * xpo_ref[at, :]
                    ht = ht + wt_ref[k:k + 1, :] * xpt_ref[at, :]
                da = da_ref[pl.ds(base, rows), :]
                if own_is_gate:
                    sg = _sigmoid(ho)
                    dh = da * ht * (sg * (1.0 + ho * (1.0 - sg)))
                else:
                    dh = da * _silu(ht)
                dhp_ref[pl.ds(base, rows), :] = dh
                new = [carry[k] + fold(dh * xpo_ref[pl.ds(base + CONV_PAD - (kw - 1) + k, rows), :]) for k in range(kw)]
                new.append(carry[kw] + fold(dh))
                return tuple(new)

            init = tuple(jnp.zeros((SUBLANE, cb), F32) for _ in range(kw + 1))
            parts = lax.fori_loop(0, t // rows, step, init)
            for k in range(kw):
                dw_ref[k:k + 1, :] = jnp.sum(parts[k], axis=0, keepdims=True)
            db_ref[...] = jnp.sum(parts[kw], axis=0, keepdims=True)

        half = pl.program_id(0)

        @pl.when(half == 0)
        def _():
            first_pass(True)

        @pl.when(half == 1)
        def _():
            first_pass(False)

        def second(s, carry):
            base = pl.multiple_of(s * rows, rows)
            acc = jnp.zeros((rows, cb), F32)
            for k in range(kw):
                acc = acc + wo_ref[k:k + 1, :] * dhp_ref[pl.ds(base + (kw - 1) - k, rows), :]
            dx_ref[pl.ds(base, rows), :] = acc.astype(dx_ref.dtype)
            return carry

        lax.fori_loop(0, t // rows, second, 0)

    own = lambda r: pl.BlockSpec((r, cb), lambda h, j: (0, h * nb + j))
    oth = lambda r: pl.BlockSpec((r, cb), lambda h, j: (0, (1 - h) * nb + j))
    return pl.pallas_call(
        body, name=name, grid=(2, nb),
        in_specs=[own(t), oth(t), pl.BlockSpec((t, cb), lambda h, j: (0, j)), own(kw), oth(kw), own(1), oth(1)],
        out_specs=[own(t), own(kw), own(1)],
        out_shape=[S((t, c2), dx_dtype), S((kw, c2), F32), S((1, c2), F32)],
        scratch_shapes=[pltpu.VMEM((CONV_PAD + t, cb), F32)] * 3,
        compiler_params=pltpu.CompilerParams(dimension_semantics=("parallel", "parallel")),
    )(hpre, hpre, dact, w, w, b, b)


def _bdot(a, b, ca, cb):
    return lax.dot_general(a.astype(MXU_DTYPE), b.astype(MXU_DTYPE), (((ca,), (cb,)), ((0,), (0,))),
                           preferred_element_type=F32)


@jax.custom_vjp
def bmm_nn(a, b):
    return _bdot(a, b, 2, 1)


bmm_nn.defvjp(lambda a, b: (_bdot(a, b, 2, 1), (a, b)),
              lambda r, g: (_bdot(g, r[1], 2, 2), _bdot(r[0], g, 1, 1)))


@jax.custom_vjp
def bmm_tn(a, b):
    return _bdot(a, b, 1, 1)


bmm_tn.defvjp(lambda a, b: (_bdot(a, b, 1, 1), (a, b)),
              lambda r, g: (_bdot(r[1], g, 2, 2), _bdot(r[0], g, 2, 1)))


@jax.custom_vjp
def bmm_nt(a, b):
    return _bdot(a, b, 2, 2)


bmm_nt.defvjp(lambda a, b: (_bdot(a, b, 2, 2), (a, b)),
              lambda r, g: (_bdot(g, r[1], 2, 1), _bdot(g, r[0], 1, 1)))


def ssd_chunk(x, dt, bm, cm, hprev, a_log, dsk):
    hg, ln, _ = x.shape
    n = bm.shape[1]
    ii = lax.broadcasted_iota(jnp.int32, (ln, ln), 0)
    jj = lax.broadcasted_iota(jnp.int32, (ln, ln), 1)
    tril, eye, triu = (ii >= jj)[None], (ii == jj)[None], (ii <= jj)[None]
    da = dt * (-jnp.exp(a_log))
    da_row = jnp.sum(jnp.where(eye, da, 0.0), axis=1, keepdims=True)
    dt_row = jnp.sum(jnp.where(eye, dt, 0.0), axis=1, keepdims=True)
    cum_c = jnp.sum(jnp.where(tril, da_row, 0.0), axis=2, keepdims=True)
    cum_r = jnp.sum(jnp.where(triu, da, 0.0), axis=1, keepdims=True)
    last = jnp.sum(da, axis=1, keepdims=True)
    decay = jnp.where(tril, jnp.exp(jnp.where(tril, cum_c - cum_r, 0.0)), 0.0)
    cb = bmm_nt(cm[None], bm[None])
    y_diag = bmm_nn(cb * decay * dt_row, x)
    bb = jnp.broadcast_to(bm[None], (hg, ln, n))
    cc = jnp.broadcast_to(cm[None], (hg, ln, n))
    states = bmm_tn(x * (jnp.exp(last - cum_c) * dt), bb)
    y_off = bmm_nt(cc, hprev) * jnp.exp(cum_c)
    hnew = hprev * jnp.exp(last) + states
    return y_diag + y_off + dsk * x, hnew


def _ssd_dims(xs, bm, a_log):
    t = xs.shape[0]
    h = a_log.shape[0]
    return h, t, xs.shape[1] // h, h // N_GROUPS, bm.shape[1] // N_GROUPS, t // CHUNK


def _heads_of(ref, g, hg, p):
    return jnp.stack([ref[:, (g * hg + i) * p:(g * hg + i + 1) * p] for i in range(hg)])


def _cols_of(ref, g, hg):
    return jnp.stack([ref[:, g * hg + i:g * hg + i + 1] for i in range(hg)])


def ssd_fwd(name, xs, dt, bm, cm, a_log, dsk):
    h, t, p, hg, n, nc = _ssd_dims(xs, bm, a_log)

    def body(al_ref, dk_ref, x_ref, dt_ref, b_ref, c_ref, y_ref, hp_ref, h_scr):
        @pl.when(pl.program_id(0) == 0)
        def _():
            h_scr[...] = jnp.zeros_like(h_scr)

        for g in range(N_GROUPS):
            hs, ns = slice(g * hg, (g + 1) * hg), slice(g * n, (g + 1) * n)
            hprev = h_scr[hs]
            hp_ref[hs, 0] = hprev
            y, hnew = ssd_chunk(_heads_of(x_ref, g, hg, p), _cols_of(dt_ref, g, hg), b_ref[:, ns], c_ref[:, ns],
                                hprev, al_ref[hs], dk_ref[hs])
            for i in range(hg):
                y_ref[:, (g * hg + i) * p:(g * hg + i + 1) * p] = y[i]
            h_scr[hs] = hnew

    head = pl.BlockSpec((h, 1, 1), lambda c: (0, 0, 0))
    row = lambda w: pl.BlockSpec((CHUNK, w), lambda c: (c, 0))
    return pl.pallas_call(
        body, name=name, grid=(nc,),
        in_specs=[head, head, row(h * p), row(dt.shape[1]), row(N_GROUPS * n), row(N_GROUPS * n)],
        out_specs=[row(h * p), pl.BlockSpec((h, 1, p, n), lambda c: (0, c, 0, 0))],
        out_shape=[S((t, h * p), F32), S((h, nc, p, n), F32)],
        scratch_shapes=[pltpu.VMEM((h, p, n), F32)],
        compiler_params=pltpu.CompilerParams(dimension_semantics=("arbitrary",)),
    )(a_log, dsk, xs, dt, bm, cm)


def ssd_bwd(name, xs, dt, bm, cm, a_log, dsk, hp, dy):
    h, t, p, hg, n, nc = _ssd_dims(xs, bm, a_log)

    def body(al_ref, dk_ref, x_ref, dt_ref, b_ref, c_ref, hp_ref, dy_ref,
             dx_ref, ddt_ref, db_ref, dc_ref, dal_ref, ddk_ref, dh_scr):
        @pl.when(pl.program_id(0) == 0)
        def _():
            dh_scr[...] = jnp.zeros_like(dh_scr)
            dal_ref[...] = jnp.zeros_like(dal_ref)
            ddk_ref[...] = jnp.zeros_like(ddk_ref)

        ddt_ref[...] = jnp.zeros_like(ddt_ref)
        for g in range(N_GROUPS):
            hs, ns = slice(g * hg, (g + 1) * hg), slice(g * n, (g + 1) * n)
            _, vjp = jax.vjp(ssd_chunk, _heads_of(x_ref, g, hg, p), _cols_of(dt_ref, g, hg), b_ref[:, ns],
                             c_ref[:, ns], hp_ref[hs, 0], al_ref[hs], dk_ref[hs])
            gx, gdt, gb, gc, ghp, gal, gdk = vjp((_heads_of(dy_ref, g, hg, p), dh_scr[hs]))
            for i in range(hg):
                dx_ref[:, (g * hg + i) * p:(g * hg + i + 1) * p] = gx[i]
                ddt_ref[:, g * hg + i:g * hg + i + 1] = gdt[i]
            db_ref[:, ns] = gb
            dc_ref[:, ns] = gc
            dh_scr[hs] = ghp
            dal_ref[hs] += gal
            ddk_ref[hs] += gdk

    head = pl.BlockSpec((h, 1, 1), lambda c: (0, 0, 0))
    row = lambda w: pl.BlockSpec((CHUNK, w), lambda c: (nc - 1 - c, 0))
    return pl.pallas_call(
        body, name=name, grid=(nc,),
        in_specs=[head, head, row(h * p), row(dt.shape[1]), row(N_GROUPS * n), row(N_GROUPS * n),
                  pl.BlockSpec((h, 1, p, n), lambda c: (0, nc - 1 - c, 0, 0)), row(h * p)],
        out_specs=[row(h * p), row(dt.shape[1]), row(N_GROUPS * n), row(N_GROUPS * n), head, head],
        out_shape=[S((t, h * p), F32), S(dt.shape, F32), S(bm.shape, F32), S(cm.shape, F32),
                   S((h, 1, 1), F32), S((h, 1, 1), F32)],
        scratch_shapes=[pltpu.VMEM((h, p, n), F32)],
        compiler_params=pltpu.CompilerParams(dimension_semantics=("arbitrary",)),
    )(a_log, dsk, xs, dt, bm, cm, hp, dy)


def _alpha(depth):
    return (2.0 * depth) ** 0.25


def _pad_lanes(v):
    return jnp.pad(v, ((0, 0), (0, LANE - v.shape[1])))


def split_even_weights(w, j):
    d = w["e_w_in"].shape[2]
    da = w["e_conv_a_w"].shape[2]
    db = w["e_norm_b_g"].shape[1]
    gn = N_GROUPS * N_STATE
    nh = w["e_dt_bias"].shape[1]
    main = 2 * da + 2 * db + 2 * gn
    win = w["e_w_in"][j]
    ox = 2 * da + db
    cw, cbias = w["e_conv_b_w"][j], w["e_conv_b_b"][j][None]
    return dict(
        d=d, da=da, db=db, gn=gn, nh=nh, main=main,
        win_main=win[:main], win_dt=jnp.pad(win[main:], ((0, LANE - nh), (0, 0))),
        caw=w["e_conv_a_w"][j], cab=w["e_conv_a_b"][j][None], lag=w["e_ln_a_g"][j][None], lab=w["e_ln_a_b"][j][None],
        cw_xs=cw[:, :db], cw_b=cw[:, db:db + gn], cw_c=cw[:, db + gn:],
        cb_xs=cbias[:, :db], cb_b=cbias[:, db:db + gn], cb_c=cbias[:, db + gn:],
        dt_bias=_pad_lanes(w["e_dt_bias"][j][None]), a_log=w["e_a_log"][j].reshape(nh, 1, 1),
        dsk=w["e_d_skip"][j].reshape(nh, 1, 1), norm_g=w["e_norm_b_g"][j][None],
        wout_a=w["e_w_out"][j][:da], wout_b=w["e_w_out"][j][da:],
    )


def even_fwd(tag, x, xm, lw, ln_g, ln_b, alpha):
    t = x.shape[0]
    da, db, gn, nh = lw["da"], lw["db"], lw["gn"], lw["nh"]
    u = mm(tag + "_win", xm, lw["win_main"], "nt")
    udt = mm(tag + "_windt", xm, lw["win_dt"], "nt")
    ua, ug, z, xs_pre = Win(u, da, 0), Win(u, da, 1), Win(u, db, 2 * da // db), Win(u, db, (2 * da + db) // db)
    b_pre, c_pre = Win(u, gn, (2 * da + 2 * db) // gn), Win(u, gn, (2 * da + 2 * db + gn) // gn)
    (a0,) = rowwise(tag + "_glu", f_glu, [ua, ug], [], [da])
    a1 = conv_fwd(tag + "_conva", a0, lw["caw"], lw["cab"])
    (ya,) = rowwise(tag + "_lna", f_ln_silu, [a1], [lw["lag"], lw["lab"]], [da], out_dtypes=[MXU_DTYPE])
    xs_c = conv_fwd(tag + "_convxs", xs_pre, lw["cw_xs"], lw["cb_xs"])
    b_c = conv_fwd(tag + "_convb", b_pre, lw["cw_b"], lw["cb_b"])
    c_c = conv_fwd(tag + "_convc", c_pre, lw["cw_c"], lw["cb_c"])
    xs, bm, cm = rowwise(tag + "_silu3", f_silu3, [xs_c, b_c, c_c], [], [db, gn, gn])
    (dt,) = rowwise(tag + "_dt", f_softplus, [udt], [lw["dt_bias"]], [LANE])
    yssd, hp = ssd_fwd(tag + "_ssd", xs, dt, bm, cm, lw["a_log"], lw["dsk"])
    (yb,) = rowwise(tag + "_gate", f_gate_rms, [yssd, z], [lw["norm_g"]], [db], out_dtypes=[MXU_DTYPE])
    ma = mm(tag + "_wouta", ya, lw["wout_a"], "nn")
    mb = mm(tag + "_woutb", yb, lw["wout_b"], "nn")

    def f_res(xv, mav, mbv, g, b):
        pre = alpha * xv + mav + mbv
        y = _ln(pre, g, b)
        return y, y, pre

    x1, x1m, pre = rowwise(tag + "_res", f_res, [x, ma, mb], [ln_g, ln_b], [x.shape[1]] * 3,
                           out_dtypes=[F32, MXU_DTYPE, F32])
    saved = dict(xm=xm, u=u, udt=udt, a0=a0, a1=a1, ya=ya, xs_c=xs_c, b_c=b_c, c_c=c_c, xs=xs, dt=dt, bm=bm,
                 cm=cm, hp=hp, yssd=yssd, yb=yb, pre=pre)
    return x1, x1m, saved


def even_bwd(tag, dx1_pieces, sv, lw, ln_g, ln_b, alpha):
    t = sv["u"].shape[0]
    da, db, gn, nh = lw["da"], lw["db"], lw["gn"], lw["nh"]
    u, xm = sv["u"], sv["xm"]
    mx = (MXU_DTYPE,)
    ua, ug, z, xs_pre = Win(u, da, 0), Win(u, da, 1), Win(u, db, 2 * da // db), Win(u, db, (2 * da + db) // db)
    b_pre, c_pre = Win(u, gn, (2 * da + 2 * db) // gn), Win(u, gn, (2 * da + 2 * db + gn) // gn)
    (dpre, dprem), (dg0, db0) = rowwise_bwd(tag + "_res_b", f_ln, [sv["pre"]], [ln_g, ln_b], [dx1_pieces],
                                            d_dtypes=[(F32, MXU_DTYPE)])
    dya = mm(tag + "_dya", dprem, lw["wout_a"], "nt")
    dyb = mm(tag + "_dyb", dprem, lw["wout_b"], "nt")
    dwout_a = mm(tag + "_dwouta", sv["ya"], dprem, "tn", EXCHANGE_DTYPE)
    dwout_b = mm(tag + "_dwoutb", sv["yb"], dprem, "tn", EXCHANGE_DTYPE)
    (dyssd, dz), (dnorm_g,) = rowwise_bwd(tag + "_gate_b", f_gate_rms, [sv["yssd"], z], [lw["norm_g"]], [[dyb]],
                                          d_dtypes=[(F32,), mx])
    dxs, ddt, dbm, dcm, dalog, ddsk = ssd_bwd(tag + "_ssd_b", sv["xs"], sv["dt"], sv["bm"], sv["cm"],
                                              lw["a_log"], lw["dsk"], sv["hp"], dyssd)
    (dudt,), (ddt_bias,) = rowwise_bwd(tag + "_dt_b", f_softplus, [sv["udt"]], [lw["dt_bias"]], [[ddt]],
                                       d_dtypes=[mx])
    (dxs_c, db_c, dc_c), _ = rowwise_bwd(tag + "_silu3_b", f_silu3, [sv["xs_c"], sv["b_c"], sv["c_c"]], [],
                                         [[dxs], [dbm], [dcm]])
    dxs_pre, dcw_xs, dcb_xs = conv_bwd(tag + "_convxs_b", xs_pre, dxs_c, lw["cw_xs"], MXU_DTYPE)
    db_pre, dcw_b, dcb_b = conv_bwd(tag + "_convb_b", b_pre, db_c, lw["cw_b"], MXU_DTYPE)
    dc_pre, dcw_c, dcb_c = conv_bwd(tag + "_convc_b", c_pre, dc_c, lw["cw_c"], MXU_DTYPE)
    (da1,), (dlag, dlab) = rowwise_bwd(tag + "_lna_b", f_ln_silu, [sv["a1"]], [lw["lag"], lw["lab"]], [[dya]])
    da0, dcaw, dcab = conv_bwd(tag + "_conva_b", sv["a0"], da1, lw["caw"])
    (dua, dug), _ = rowwise_bwd(tag + "_glu_b", f_glu, [ua, ug], [], [[da0]], d_dtypes=[mx, mx])
    du = jnp.concatenate([dua, dug, dz, dxs_pre, db_pre, dc_pre], axis=1)
    dx_m = mm(tag + "_dxm", du, lw["win_main"], "nn")
    dx_dt = mm(tag + "_dxdt", dudt, lw["win_dt"], "nn")
    dwin_main = mm(tag + "_dwin", du, xm, "tn", EXCHANGE_DTYPE)
    dwin_dt = mm(tag + "_dwindt", dudt, xm, "tn", EXCHANGE_DTYPE)
    grads = dict(
        e_w_in=jnp.concatenate([dwin_main, dwin_dt[:nh]], axis=0),
        e_conv_a_w=dcaw, e_conv_a_b=dcab[0], e_ln_a_g=dlag[0], e_ln_a_b=dlab[0],
        e_conv_b_w=jnp.concatenate([dcw_xs, dcw_b, dcw_c], axis=1),
        e_conv_b_b=jnp.concatenate([dcb_xs, dcb_b, dcb_c], axis=1)[0],
        e_dt_bias=ddt_bias[0, :nh], e_a_log=dalog.reshape(nh), e_d_skip=ddsk.reshape(nh), e_norm_b_g=dnorm_g[0],
        e_w_out=jnp.concatenate([dwout_a, dwout_b], axis=0), ln_g0=dg0[0], ln_b0=db0[0],
    )
    return [Win(dpre, coef=alpha), dx_m, dx_dt], grads


def odd_fwd(tag, x, xm, w, j, ln_g, ln_b, alpha):
    d = x.shape[1]
    u = mm(tag + "_win", xm, w["o_w_in"][j], "nt")
    bg, cg, v = Win(u, d, 0), Win(u, d, 1), Win(u, d, 2)
    (s,) = rowwise(tag + "_cv", f_mul, [cg, v], [], [d])
    cs = conv_fwd(tag + "_conv", s, w["o_conv_w"][j], None)
    (m,) = rowwise(tag + "_bm", f_mul, [bg, cs], [], [d], out_dtypes=[MXU_DTYPE])
    mix = mm(tag + "_wout", m, w["o_w_out"][j], "nn")

    def f_res(xv, mv, g, b):
        pre = alpha * xv + mv
        y = _ln(pre, g, b)
        return y, y, pre

    x1, x1m, pre = rowwise(tag + "_res", f_res, [x, mix], [ln_g, ln_b], [d] * 3, out_dtypes=[F32, MXU_DTYPE, F32])
    return x1, x1m, dict(xm=xm, u=u, s=s, cs=cs, m=m, pre=pre)


def odd_bwd(tag, dx1_pieces, sv, w, j, ln_g, ln_b, alpha):
    xm, u = sv["xm"], sv["u"]
    d = xm.shape[1]
    mx = (MXU_DTYPE,)
    bg, cg, v = Win(u, d, 0), Win(u, d, 1), Win(u, d, 2)
    (dpre, dprem), (dg0, db0) = rowwise_bwd(tag + "_res_b", f_ln, [sv["pre"]], [ln_g, ln_b], [dx1_pieces],
                                            d_dtypes=[(F32, MXU_DTYPE)])
    dm = mm(tag + "_dm", dprem, w["o_w_out"][j], "nt")
    dwout = mm(tag + "_dwout", sv["m"], dprem, "tn", EXCHANGE_DTYPE)
    (dbg, dcs), _ = rowwise_bwd(tag + "_bm_b", f_mul, [bg, sv["cs"]], [], [[dm]], d_dtypes=[mx, (F32,)])
    ds, dcw, _ = conv_bwd(tag + "_conv_b", sv["s"], dcs, w["o_conv_w"][j])
    (dcg, dv), _ = rowwise_bwd(tag + "_cv_b", f_mul, [cg, v], [], [[ds]], d_dtypes=[mx, mx])
    du = jnp.concatenate([dbg, dcg, dv], axis=1)
    dx_u = mm(tag + "_dx", du, w["o_w_in"][j], "nn")
    dwin = mm(tag + "_dwin", du, xm, "tn", EXCHANGE_DTYPE)
    grads = dict(o_w_in=dwin, o_conv_w=dcw, o_w_out=dwout, ln_g0=dg0[0], ln_b0=db0[0])
    return [Win(dpre, coef=alpha), dx_u], grads


def ffn_fwd(tag, x1, x1m, p_i, w, i, ln_g, ln_b, alpha):
    d = x1.shape[1]
    hpre = mm(tag + "_wup", x1m, w["f_w_up"][i], "nt")
    act = gated_conv_fwd(tag + "_fgate", hpre, w["f_conv_w"][i], w["f_conv_b"][i][None], MXU_DTYPE)
    ffn = mm(tag + "_wdown", act, w["f_w_down"][i], "nn")
    pp = mm(tag + "_pproj", p_i, w["ple_w_proj"][i], "nt")
    gt = mm(tag + "_pgate", x1m, w["ple_w_gate"][i], "nn")

    def f_res2(xv, fv, ppv, gtv, g, b):
        pre = alpha * xv + fv + ppv * _sigmoid(gtv)
        y = _ln(pre, g, b)
        return y, y, pre

    x2, x2m, pre = rowwise(tag + "_res2", f_res2, [x1, ffn, pp, gt], [ln_g, ln_b], [d] * 3,
                           out_dtypes=[F32, MXU_DTYPE, F32])
    return x2, x2m, dict(x1m=x1m, hpre=hpre, act=act, pp=pp, gt=gt, pre=pre)


def ffn_bwd(tag, dx2_pieces, sv, p_i, w, i, ln_g, ln_b, alpha):
    x1m = sv["x1m"]
    mx = (MXU_DTYPE,)
    (dpre, dprem), (dg1, db1) = rowwise_bwd(tag + "_res2_b", f_ln, [sv["pre"]], [ln_g, ln_b], [dx2_pieces],
                                            d_dtypes=[(F32, MXU_DTYPE)])
    (dpp, dgt), _ = rowwise_bwd(tag + "_pg_b", f_gate_mul, [sv["pp"], sv["gt"]], [], [[dpre]], d_dtypes=[mx, mx])
    dwproj = mm(tag + "_dwproj", dpp, p_i, "tn", EXCHANGE_DTYPE)
    dwgate = mm(tag + "_dwgate", x1m, dgt, "tn", EXCHANGE_DTYPE)
    dx1_a = mm(tag + "_dx1a", dgt, w["ple_w_gate"][i], "nt")
    dact = mm(tag + "_dact", dprem, w["f_w_down"][i], "nt")
    dwdown = mm(tag + "_dwdown", sv["act"], dprem, "tn", EXCHANGE_DTYPE)
    dhpre, dfcw, dfcb = gated_conv_bwd(tag + "_fgate_b", sv["hpre"], dact, w["f_conv_w"][i], w["f_conv_b"][i][None],
                                       MXU_DTYPE)
    dwup = mm(tag + "_dwup", dhpre, x1m, "tn", EXCHANGE_DTYPE)
    dx1_b = mm(tag + "_dx1b", dhpre, w["f_w_up"][i], "nn")
    grads = dict(f_w_up=dwup, f_conv_w=dfcw, f_conv_b=dfcb[0], f_w_down=dwdown, ple_w_proj=dwproj, ple_w_gate=dwgate,
                 ln_g1=dg1[0], ln_b1=db1[0])
    return [Win(dpre, coef=alpha), dx1_a, dx1_b], grads


def local_step(x, p, w, target):
    depth = w["ln_g"].shape[0]
    alpha = _alpha(depth)
    d = x.shape[1]
    saved = []
    h = hm = x
    for i in range(depth):
        j = i // 2
        g0, b0, g1, b1 = w["ln_g"][i, 0][None], w["ln_b"][i, 0][None], w["ln_g"][i, 1][None], w["ln_b"][i, 1][None]
        tag = "l%d" % i
        if i % 2 == 0:
            lw = split_even_weights(w, j)
            h, hm, sv_m = even_fwd(tag, h, hm, lw, g0, b0, alpha)
        else:
            lw = None
            h, hm, sv_m = odd_fwd(tag, h, hm, w, j, g0, b0, alpha)
        h, hm, sv_f = ffn_fwd(tag, h, hm, p[i], w, i, g1, b1, alpha)
        saved.append((lw, sv_m, sv_f, (g0, b0, g1, b1)))

    def f_loss(xf, tg):
        diff = xf - tg
        sq = jnp.sum(jnp.sum(jnp.square(diff), axis=1, keepdims=True), axis=0, keepdims=True)
        return diff * (1.0 / d), jnp.broadcast_to(sq, (1, LANE))

    dxf, sq = rowwise("loss", f_loss, [h, target], [], [d], red_widths=[LANE])
    loss = sq[0, 0] * (0.5 / d)

    per_layer = []
    pieces = [dxf]
    for i in reversed(range(depth)):
        j = i // 2
        lw, sv_m, sv_f, (g0, b0, g1, b1) = saved[i]
        tag = "l%d" % i
        pieces, gf = ffn_bwd(tag, pieces, sv_f, p[i], w, i, g1, b1, alpha)
        if i % 2 == 0:
            pieces, gm = even_bwd(tag, pieces, sv_m, lw, g0, b0, alpha)
        else:
            pieces, gm = odd_bwd(tag, pieces, sv_m, w, j, g0, b0, alpha)
        per_layer.append((i, gm, gf))

    def f_sum(*vs):
        acc = None
        for v, c in zip(vs, [pc.coef for pc in map(_win, pieces)]):
            v = v if c == 1.0 else v * c
            acc = v if acc is None else acc + v
        return (acc,)

    (grad_x,) = rowwise("grad_x", f_sum, [Win(_win(pc).arr) for pc in pieces], [], [d])

    by_layer = {i: (gm, gf) for i, gm, gf in per_layer}
    grads = {}
    n_even, n_odd = (depth + 1) // 2, depth // 2
    collect = lambda name, per_layer: per_layer if name in BIG else jnp.stack(per_layer)
    for name in ("e_w_in", "e_conv_a_w", "e_conv_a_b", "e_ln_a_g", "e_ln_a_b", "e_conv_b_w", "e_conv_b_b", "e_dt_bias",
                 "e_a_log", "e_d_skip", "e_norm_b_g", "e_w_out"):
        grads[name] = collect(name, [by_layer[2 * j][0][name] for j in range(n_even)])
    for name in ("o_w_in", "o_conv_w", "o_w_out"):
        grads[name] = collect(name, [by_layer[2 * j + 1][0][name] for j in range(n_odd)])
    for name in ("f_w_up", "f_conv_w", "f_conv_b", "f_w_down", "ple_w_proj", "ple_w_gate"):
        grads[name] = collect(name, [by_layer[i][1][name] for i in range(depth)])
    grads["ln_g"] = jnp.stack([jnp.stack([by_layer[i][0]["ln_g0"], by_layer[i][1]["ln_g1"]]) for i in range(depth)])
    grads["ln_b"] = jnp.stack([jnp.stack([by_layer[i][0]["ln_b0"], by_layer[i][1]["ln_b1"]]) for i in range(depth)])
    return loss, grad_x, grads


_ANY = pl.BlockSpec(memory_space=pl.ANY)
_MESH = pl.DeviceIdType.MESH


def all_gather(name, xl):
    r, c_ = xl.shape

    def body(x_ref, out_ref, send_sems, recv_sems, local_sem):
        x, y, c = lax.axis_index("x"), lax.axis_index("y"), lax.axis_index("c")
        me, sibling = (x, y, c), (x, y, 1 - c)
        chips = [(1 - x, y), (x, 1 - y), (1 - x, 1 - y)]

        def slot(px, py, pc):
            return out_ref.at[4 * px + 2 * py + pc]

        def copy(k, block, to, src=None):
            return pltpu.make_async_remote_copy(
                src_ref=slot(*block) if src is None else src, dst_ref=slot(*block),
                send_sem=send_sems.at[k], recv_sem=recv_sems.at[k], device_id=to, device_id_type=_MESH)

        mine = pltpu.make_async_copy(x_ref, slot(*me), local_sem)
        mine.start()
        first = [copy(0, me, sibling, src=x_ref)]
        first += [copy(1 + j, me, (*chip, c), src=x_ref) for j, chip in enumerate(chips)]
        for cp in first:
            cp.start()
        passed = [copy(4 + j, (*chip, c), sibling) for j, chip in enumerate(chips)]
        for j, chip in enumerate(chips):
            copy(1 + j, (*chip, c), me).wait_recv()
            passed[j].start()
        copy(0, sibling, me).wait_recv()
        for j, chip in enumerate(chips):
            copy(4 + j, (*chip, 1 - c), me).wait_recv()
        for cp in first + passed:
            cp.wait_send()
        mine.wait()

    return pl.pallas_call(
        body, name=name, out_shape=S((N_DEV, r, c_), xl.dtype), in_specs=[_ANY], out_specs=_ANY,
        scratch_shapes=[pltpu.SemaphoreType.DMA((7,)), pltpu.SemaphoreType.DMA((7,)), pltpu.SemaphoreType.DMA],
    )(xl)


def _exchange(name, srcs, out_slots, copies_of):
    n = len(srcs)
    n_copies = sum(s.shape[0] for s in srcs) * out_slots

    def body(*refs):
        src_refs, out_refs, send_sems, recv_sems = refs[:n], refs[n:2 * n], refs[2 * n], refs[2 * n + 1]
        pattern = copies_of(lax.axis_index("x"), lax.axis_index("y"), lax.axis_index("c"))
        copies = []
        for s_ref, o_ref in zip(src_refs, out_refs):
            for l in range(s_ref.shape[0]):
                for src_at, slot, target in pattern:
                    k = len(copies)
                    copies.append(pltpu.make_async_remote_copy(
                        src_ref=s_ref.at[(l,) + src_at], dst_ref=o_ref.at[l, slot], send_sem=send_sems.at[k],
                        recv_sem=recv_sems.at[k], device_id=target, device_id_type=_MESH))
        for cp in copies:
            cp.start()
        for cp in copies:
            cp.wait()

    return pl.pallas_call(
        body, name=name, out_shape=[S((s.shape[0], out_slots) + s.shape[-2:], s.dtype) for s in srcs],
        in_specs=[_ANY] * n, out_specs=[_ANY] * n,
        scratch_shapes=[pltpu.SemaphoreType.DMA((n_copies,)), pltpu.SemaphoreType.DMA((n_copies,))],
    )(*srcs)


def exchange_sibling(name, gs):
    return _exchange(name, gs, 4, lambda x, y, c: [((q, 1 - c), q, (x, y, 1 - c)) for q in range(4)])


def exchange_chips(name, ps):
    def pattern(x, y, c):
        chips = [(1 - x, y), (x, 1 - y), (1 - x, 1 - y)]
        return [((2 * cx + cy,), k, (cx, cy, c)) for k, (cx, cy) in enumerate(chips)]
    return _exchange(name, ps, 3, pattern)


def sum_with_sibling(name, g5, recv, core):
    l, _, _, a, b = g5.shape

    def body(core_ref, g_ref, r_ref, o_ref, ox_ref):
        s = g_ref[0, 0, 0].astype(F32) + r_ref[0, 0].astype(F32)
        o_ref[0, 0] = s
        ox_ref[0, 0] = s.astype(ox_ref.dtype)

    slab = pl.BlockSpec((1, 1, a, b), lambda i, q, cr: (i, q, 0, 0))
    return pl.pallas_call(
        body, name=name, out_shape=[S((l, 4, a, b), F32), S((l, 4, a, b), EXCHANGE_DTYPE)],
        grid_spec=pltpu.PrefetchScalarGridSpec(
            num_scalar_prefetch=1, grid=(l, 4),
            in_specs=[pl.BlockSpec((1, 1, 1, a, b), lambda i, q, cr: (i, q, cr[0], 0, 0)), slab],
            out_specs=[slab, slab]),
    )(core, g5, recv)


def sum_with_chips(name, p4, recv, chip):
    l, _, a, b = p4.shape

    def body(chip_ref, p_ref, r_ref, o_ref):
        o_ref[0] = ((p_ref[0, 0] + r_ref[0, 0].astype(F32)) + r_ref[0, 1].astype(F32)) + r_ref[0, 2].astype(F32)

    return pl.pallas_call(
        body, name=name, out_shape=S((l, a, b), F32),
        grid_spec=pltpu.PrefetchScalarGridSpec(
            num_scalar_prefetch=1, grid=(l,),
            in_specs=[pl.BlockSpec((1, 1, a, b), lambda i, ch: (i, ch[0], 0, 0)),
                      pl.BlockSpec((1, 3, a, b), lambda i, ch: (i, 0, 0, 0))],
            out_specs=pl.BlockSpec((1, a, b), lambda i, ch: (i, 0, 0))),
    )(chip, p4, recv)


def sum_devices(name, g8):
    _, r, c_ = g8.shape

    def body(g_ref, o_ref):
        acc = g_ref[0]
        for k in range(1, N_DEV):
            acc = acc + g_ref[k]
        o_ref[...] = acc

    return pl.pallas_call(body, name=name, out_shape=S((r, c_), F32))(g8)


def _flatten(parts, cols, row_mult):
    flat = jnp.concatenate([v.reshape(-1) for v in parts])
    n = flat.shape[0]
    rows = -(-n // (cols * row_mult)) * row_mult
    return jnp.pad(flat, (0, rows * cols - n)).reshape(rows, cols)


def _exchange_dims(name, lshape):
    l, r, c = lshape
    return (l, c, r) if name in TRANSPOSED else (l, r, c)


def gather_big(name, local):
    cols = local["e_w_out"].shape[2]
    parts = []
    for n in EXCHANGE_ORDER:
        v = local[n].astype(MXU_DTYPE)
        parts.append((v.transpose(0, 2, 1) if n in TRANSPOSED else v).reshape(-1, cols))
    got = all_gather(name, jnp.concatenate(parts, axis=0))
    full, r0 = {}, 0
    for n, part in zip(EXCHANGE_ORDER, parts):
        l, a, b = _exchange_dims(n, local[n].shape)
        seg = got[:, r0:r0 + part.shape[0]].reshape(N_DEV, l, a, b)
        full[n] = seg.transpose(1, 0, 2, 3).reshape(l, N_DEV * a, b)
        r0 += part.shape[0]
    return full


def reduce_scatter_big(grads, local_shapes):
    x, y, c = lax.axis_index("x"), lax.axis_index("y"), lax.axis_index("c")
    core, chip = c.astype(jnp.int32).reshape(1), (2 * x + y).astype(jnp.int32).reshape(1)
    g5 = []
    for n in EXCHANGE_ORDER:
        l, a, b = _exchange_dims(n, local_shapes[n])
        g5.append(jnp.stack(grads[n]).reshape(l, 4, 2, a, b))
    from_sibling = exchange_sibling("rs_sibling", g5)
    sums = [sum_with_sibling("rs_sum_sibling_" + n, g, r, core) for n, g, r in zip(EXCHANGE_ORDER, g5, from_sibling)]
    from_chips = exchange_chips("rs_chips", [px for _, px in sums])
    out = {}
    for n, (p4, _), r in zip(EXCHANGE_ORDER, sums, from_chips):
        seg = sum_with_chips("rs_sum_chips_" + n, p4, r, chip)
        out[n] = seg.transpose(0, 2, 1) if n in TRANSPOSED else seg
    return out


def gather_small(name, local, names):
    flat = _flatten([local[n] for n in names], LANE, 1)
    got = all_gather(name, flat).reshape(N_DEV, -1)
    full, off = {}, 0
    for n in names:
        size = math.prod(local[n].shape)
        seg = got[:, off:off + size].reshape((N_DEV,) + local[n].shape)
        full[n] = seg.transpose(1, 2, 0, 3).reshape(seg.shape[1], seg.shape[2], -1)
        off += size
    return full


def all_reduce_small(grads, names):
    flat = _flatten([grads[n] for n in names], LANE, SUBLANE)
    total = sum_devices("ar_sum", all_gather("ar_gather", flat)).reshape(-1)
    out, off = {}, 0
    for nm in names:
        size = math.prod(grads[nm].shape)
        out[nm] = total[off:off + size].reshape(grads[nm].shape)
        off += size
    return out


def adamw(name, w, g, m, v):
    shape = w.shape
    cols = shape[-1]
    rows = math.prod(shape[:-1])
    tr = _pick(rows, (256, 128, 64, 32, 16, 8)) if rows * cols > 256 * 1024 else rows
    c1 = 1.0 - ADAM_B1 ** ADAM_STEP
    c2 = 1.0 - ADAM_B2 ** ADAM_STEP

    def body(w_ref, g_ref, m_ref, v_ref, d_ref, nm_ref, nv_ref):
        gv = g_ref[...]
        m2 = ADAM_B1 * m_ref[...] + (1.0 - ADAM_B1) * gv
        v2 = ADAM_B2 * v_ref[...] + (1.0 - ADAM_B2) * jnp.square(gv)
        d_ref[...] = -ADAM_LR * ((m2 / c1) / (jnp.sqrt(v2 / c2) + ADAM_EPS) + ADAM_WD * w_ref[...])
        nm_ref[...] = m2
        nv_ref[...] = v2

    spec = pl.BlockSpec((tr, cols), lambda i: (i, 0))
    outs = pl.pallas_call(
        body, name=name, grid=(rows // tr,), in_specs=[spec] * 4, out_specs=[spec] * 3,
        out_shape=[S((rows, cols), F32)] * 3,
        compiler_params=pltpu.CompilerParams(dimension_semantics=("parallel",)),
    )(*[a.reshape(rows, cols) for a in (w, g, m, v)])
    return tuple(o.reshape(shape) for o in outs)


def kernel(x, p, e_w_in, e_conv_a_w, e_conv_a_b, e_ln_a_g, e_ln_a_b, e_conv_b_w, e_conv_b_b, e_dt_bias, e_a_log, e_d_skip, e_norm_b_g, e_w_out, o_w_in, o_conv_w, o_w_out, f_w_up, f_conv_w, f_conv_b, f_w_down, ple_w_proj, ple_w_gate, ln_g, ln_b, loss_target, m_e_w_in, m_e_conv_a_w, m_e_conv_a_b, m_e_ln_a_g, m_e_ln_a_b, m_e_conv_b_w, m_e_conv_b_b, m_e_dt_bias, m_e_a_log, m_e_d_skip, m_e_norm_b_g, m_e_w_out, m_o_w_in, m_o_conv_w, m_o_w_out, m_f_w_up, m_f_conv_w, m_f_conv_b, m_f_w_down, m_ple_w_proj, m_ple_w_gate, m_ln_g, m_ln_b, v_e_w_in, v_e_conv_a_w, v_e_conv_a_b, v_e_ln_a_g, v_e_ln_a_b, v_e_conv_b_w, v_e_conv_b_b, v_e_dt_bias, v_e_a_log, v_e_d_skip, v_e_norm_b_g, v_e_w_out, v_o_w_in, v_o_conv_w, v_o_w_out, v_f_w_up, v_f_conv_w, v_f_conv_b, v_f_w_down, v_ple_w_proj, v_ple_w_gate, v_ln_g, v_ln_b):
    args = locals()
    local = {n: args[n] for n in WEIGHTS}
    mom = {n: args["m_" + n] for n in WEIGHTS}
    var = {n: args["v_" + n] for n in WEIGHTS}

    full = {n: local[n] for n in REPLICATED}
    full.update(gather_big("ag_big", local))
    full.update(gather_small("ag_small", local, SMALL_SHARDED))

    loss_local, grad_x, grads = local_step(x[0], p[:, 0], full, loss_target[0])
    loss = lax.psum(loss_local, MESH_AXES)

    g_local = reduce_scatter_big(grads, {n: local[n].shape for n in BIG})
    small = all_reduce_small(grads, REPLICATED + SMALL_SHARDED)
    dev = 4 * lax.axis_index("x") + 2 * lax.axis_index("y") + lax.axis_index("c")
    for n in REPLICATED:
        g_local[n] = small[n]
    for n in SMALL_SHARDED:
        width = local[n].shape[2]
        g_local[n] = lax.dynamic_slice_in_dim(small[n], dev * width, width, axis=2)

    delta, new_m, new_v = {}, {}, {}
    for n in WEIGHTS:
        delta[n], new_m[n], new_v[n] = adamw("adamw_" + n, local[n], g_local[n], mom[n], var[n])
    return (loss, grad_x[None], *[g_local[n] for n in WEIGHTS], *[delta[n] for n in WEIGHTS],
            *[new_m[n] for n in WEIGHTS], *[new_v[n] for n in WEIGHTS])
```

```python
import functools
import math

import jax
import jax.numpy as jnp
from jax import lax
from jax.experimental import pallas as pl
from jax.experimental.pallas import tpu as pltpu

F32 = jnp.float32
MXU_DTYPE = jnp.bfloat16
MESH_AXES = ("x", "y", "c")
N_DEV = 8
LANE = 128
SUBLANE = 8
ROWWISE_VMEM_BUDGET = 20 * 1024 * 1024
MM_TILES = (1408, 1024, 512, 256, 128)
EXCHANGE_DTYPE = jnp.bfloat16
LN_EPS = 1e-5
CHUNK = 64
HEAD_DIM = 64
N_GROUPS = 4
N_STATE = 128
CONV_PAD = 32
CONV_ROWS = 256
ADAM_LR, ADAM_B1, ADAM_B2, ADAM_EPS, ADAM_WD, ADAM_STEP = 0.001, 0.9, 0.999, 1e-08, 0.01, 10

BIG = ("e_w_in", "e_w_out", "o_w_in", "o_w_out", "f_w_up", "f_w_down", "ple_w_proj", "ple_w_gate")
SMALL_SHARDED = ("e_conv_a_w", "e_conv_b_w", "o_conv_w", "f_conv_w", "ln_g", "ln_b")
REPLICATED = ("e_conv_a_b", "e_ln_a_g", "e_ln_a_b", "e_conv_b_b", "e_dt_bias", "e_a_log", "e_d_skip",
              "e_norm_b_g", "f_conv_b")
TRANSPOSED = ("e_w_in", "o_w_in", "f_w_up", "ple_w_proj")
EXCHANGE_ORDER = ("e_w_out", "o_w_out", "ple_w_gate", "ple_w_proj", "o_w_in", "f_w_down", "f_w_up", "e_w_in")
WEIGHTS = ("e_w_in", "e_conv_a_w", "e_conv_a_b", "e_ln_a_g", "e_ln_a_b", "e_conv_b_w", "e_conv_b_b", "e_dt_bias",
           "e_a_log", "e_d_skip", "e_norm_b_g", "e_w_out", "o_w_in", "o_conv_w", "o_w_out", "f_w_up", "f_conv_w",
           "f_conv_b", "f_w_down", "ple_w_proj", "ple_w_gate", "ln_g", "ln_b")

S = jax.ShapeDtypeStruct


class Win:
    def __init__(self, arr, w=None, idx=0, coef=1.0):
        self.arr, self.w, self.idx, self.coef = arr, (arr.shape[1] if w is None else w), idx, coef


def _win(a):
    return a if isinstance(a, Win) else Win(a)


def _pick(n, prefs):
    for p in prefs:
        if p <= n and n % p == 0:
            return p
    return n


_MM_DIMS = {"nn": (1, 0), "nt": (1, 1), "tn": (0, 0)}


def mm(name, a, b, mode, out_dtype=F32):
    ca, cb = _MM_DIMS[mode]
    kdim = a.shape[ca]
    m = a.shape[1 - ca]
    n = b.shape[1 - cb]
    assert b.shape[cb] == kdim, (name, a.shape, b.shape, mode)
    tm = _pick(m, MM_TILES)
    tn = _pick(n, MM_TILES)
    tk = kdim if kdim <= MM_TILES[0] else _pick(kdim, MM_TILES)
    nk = kdim // tk
    own_acc = nk > 1 and out_dtype != F32

    def body(a_ref, b_ref, o_ref, *scratch):
        acc_ref = scratch[0] if own_acc else o_ref
        d = lax.dot_general(a_ref[...].astype(MXU_DTYPE), b_ref[...].astype(MXU_DTYPE),
                            (((ca,), (cb,)), ((), ())), preferred_element_type=F32)
        if nk == 1:
            o_ref[...] = d.astype(o_ref.dtype)
        else:
            k = pl.program_id(2)

            @pl.when(k == 0)
            def _():
                acc_ref[...] = d

            @pl.when(k > 0)
            def _():
                acc_ref[...] += d

            if own_acc:
                @pl.when(k == nk - 1)
                def _():
                    o_ref[...] = acc_ref[...].astype(o_ref.dtype)

    a_spec = pl.BlockSpec((tm, tk), lambda i, j, k: (i, k)) if ca == 1 else pl.BlockSpec((tk, tm), lambda i, j, k: (k, i))
    b_spec = pl.BlockSpec((tk, tn), lambda i, j, k: (k, j)) if cb == 0 else pl.BlockSpec((tn, tk), lambda i, j, k: (j, k))
    return pl.pallas_call(
        body, name=name, grid=(m // tm, n // tn, nk),
        in_specs=[a_spec, b_spec], out_specs=pl.BlockSpec((tm, tn), lambda i, j, k: (i, j)),
        out_shape=S((m, n), out_dtype), scratch_shapes=[pltpu.VMEM((tm, tn), F32)] if own_acc else [],
        compiler_params=pltpu.CompilerParams(dimension_semantics=("parallel", "parallel", "arbitrary")),
    )(a, b)


def _row_block(t, widths):
    tb = 512
    while tb > SUBLANE and (t % tb or tb * sum(widths) * 8 > ROWWISE_VMEM_BUDGET):
        tb //= 2
    return tb


def _tok_spec(tb, w):
    return pl.BlockSpec((tb, w.w), functools.partial(lambda i, idx: (i, idx), idx=w.idx))


def _par_spec(p):
    return pl.BlockSpec((1, p.shape[1]), lambda i: (0, 0))


def rowwise(name, fn, tok, par, out_widths, red_widths=(), out_dtypes=None):
    tok = [_win(t) for t in tok]
    t = tok[0].arr.shape[0]
    tb = _row_block(t, [w.w for w in tok] + list(out_widths))
    n_tok, n_par, n_out = len(tok), len(par), len(out_widths)
    out_dtypes = [F32] * n_out if out_dtypes is None else out_dtypes

    def body(*refs):
        ins = [r[...] for r in refs[:n_tok + n_par]]
        res = fn(*ins)
        out_refs = refs[n_tok + n_par:n_tok + n_par + n_out]
        red_refs = refs[n_tok + n_par + n_out:]
        for r, v in zip(out_refs, res[:n_out]):
            r[...] = v.astype(r.dtype)
        if red_refs:
            @pl.when(pl.program_id(0) == 0)
            def _():
                for r in red_refs:
                    r[...] = jnp.zeros_like(r)
            for r, v in zip(red_refs, res[n_out:]):
                r[...] += v

    outs = pl.pallas_call(
        body, name=name, grid=(t // tb,),
        in_specs=[_tok_spec(tb, w) for w in tok] + [_par_spec(p) for p in par],
        out_specs=[pl.BlockSpec((tb, w), lambda i: (i, 0)) for w in out_widths]
        + [pl.BlockSpec((1, w), lambda i: (0, 0)) for w in red_widths],
        out_shape=[S((t, w), dt) for w, dt in zip(out_widths, out_dtypes)] + [S((1, w), F32) for w in red_widths],
        compiler_params=pltpu.CompilerParams(dimension_semantics=("arbitrary",)),
    )(*[w.arr for w in tok], *par)
    return outs


def rowwise_bwd(name, fn, tok, par, cts, d_dtypes=None):
    tok = [_win(t) for t in tok]
    cts = [[_win(c) for c in group] for group in cts]
    d_dtypes = [(F32,)] * len(tok) if d_dtypes is None else d_dtypes
    t = tok[0].arr.shape[0]
    flat_cts = [c for group in cts for c in group]
    d_outs = [(i, w.w, dt) for i, (w, dts) in enumerate(zip(tok, d_dtypes)) for dt in dts]
    tb = _row_block(t, [w.w for w in tok] + [c.w for c in flat_cts] + [w for _, w, _ in d_outs])
    n_tok, n_par, n_ct, n_d = len(tok), len(par), len(flat_cts), len(d_outs)

    def body(*refs):
        tok_vals = [r[...] for r in refs[:n_tok]]
        par_vals = [r[...] for r in refs[n_tok:n_tok + n_par]]
        ct_refs = refs[n_tok + n_par:n_tok + n_par + n_ct]
        d_refs = refs[n_tok + n_par + n_ct:n_tok + n_par + n_ct + n_d]
        dp_refs = refs[n_tok + n_par + n_ct + n_d:]
        ct_vals, pos = [], 0
        for group in cts:
            acc = None
            for c in group:
                v = ct_refs[pos][...]
                if c.coef != 1.0:
                    v = v * c.coef
                acc = v if acc is None else acc + v
                pos += 1
            ct_vals.append(acc)
        _, vjp = jax.vjp(lambda *a: tuple(fn(*a)), *tok_vals, *par_vals)
        grads = vjp(tuple(ct_vals))
        for r, (i, _, _) in zip(d_refs, d_outs):
            r[...] = grads[i].astype(r.dtype)
        if dp_refs:
            @pl.when(pl.program_id(0) == 0)
            def _():
                for r in dp_refs:
                    r[...] = jnp.zeros_like(r)
            for r, v in zip(dp_refs, grads[n_tok:]):
                r[...] += v

    outs = pl.pallas_call(
        body, name=name, grid=(t // tb,),
        in_specs=[_tok_spec(tb, w) for w in tok] + [_par_spec(p) for p in par] + [_tok_spec(tb, c) for c in flat_cts],
        out_specs=[pl.BlockSpec((tb, w), lambda i: (i, 0)) for _, w, _ in d_outs] + [_par_spec(p) for p in par],
        out_shape=[S((t, w), dt) for _, w, dt in d_outs] + [S(p.shape, F32) for p in par],
        compiler_params=pltpu.CompilerParams(dimension_semantics=("arbitrary",)),
    )(*[w.arr for w in tok], *par, *[c.arr for c in flat_cts])
    return outs[:n_d], outs[n_d:]


def _sigmoid(x):
    return 1.0 / (1.0 + jnp.exp(-x))


def _silu(x):
    return x * _sigmoid(x)


def _ln(x, g, b):
    mu = jnp.mean(x, axis=-1, keepdims=True)
    var = jnp.mean(jnp.square(x - mu), axis=-1, keepdims=True)
    return (x - mu) * lax.rsqrt(var + LN_EPS) * g + b


def f_glu(ua, ug):
    return (ua * _sigmoid(ug),)


def f_ln_silu(a1, g, b):
    return (_silu(_ln(a1, g, b)),)


def f_silu3(a, b, c):
    return (_silu(a), _silu(b), _silu(c))


def f_softplus(dt_raw, bias):
    return (jax.nn.softplus(dt_raw + bias),)


def f_gate_rms(yssd, z, g):
    y = yssd * _silu(z)
    return (y * lax.rsqrt(jnp.mean(jnp.square(y), axis=-1, keepdims=True) + LN_EPS) * g,)


def f_ln(pre, g, b):
    return (_ln(pre, g, b),)


def f_mul(a, b):
    return (a * b,)


def f_gate_mul(pp, gt):
    return (pp * _sigmoid(gt),)


def conv_fwd(name, x, w, b):
    x = _win(x)
    t, c = x.arr.shape[0], x.w
    kw = w.shape[0]
    cb = LANE
    off = x.idx * (c // cb)
    rows = min(CONV_ROWS, t)
    has_b = b is not None

    def body(*refs):
        if has_b:
            x_ref, w_ref, b_ref, y_ref, xp_ref = refs
        else:
            x_ref, w_ref, y_ref, xp_ref = refs
        xp_ref[0:CONV_PAD, :] = jnp.zeros((CONV_PAD, cb), F32)
        xp_ref[CONV_PAD:CONV_PAD + t, :] = x_ref[...]

        def step(s, carry):
            base = pl.multiple_of(s * rows, rows)
            acc = jnp.zeros((rows, cb), F32)
            if has_b:
                acc = acc + b_ref[...]
            for k in range(kw):
                acc = acc + w_ref[k:k + 1, :] * xp_ref[pl.ds(base + CONV_PAD - (kw - 1) + k, rows), :]
            y_ref[pl.ds(base, rows), :] = acc
            return carry

        lax.fori_loop(0, t // rows, step, 0)

    in_specs = [pl.BlockSpec((t, cb), lambda j: (0, off + j)), pl.BlockSpec((kw, cb), lambda j: (0, j))]
    args = [x.arr, w]
    if has_b:
        in_specs.append(pl.BlockSpec((1, cb), lambda j: (0, j)))
        args.append(b)
    return pl.pallas_call(
        body, name=name, grid=(c // cb,), in_specs=in_specs,
        out_specs=pl.BlockSpec((t, cb), lambda j: (0, j)), out_shape=S((t, c), F32),
        scratch_shapes=[pltpu.VMEM((CONV_PAD + t, cb), F32)],
        compiler_params=pltpu.CompilerParams(dimension_semantics=("parallel",)),
    )(*args)


def conv_bwd(name, x, dy, w, dx_dtype=F32):
    x, dy = _win(x), _win(dy)
    t, c = x.arr.shape[0], x.w
    kw = w.shape[0]
    cb = LANE
    xoff = x.idx * (c // cb)
    dyoff = dy.idx * (c // cb)
    rows = min(CONV_ROWS, t)

    def body(x_ref, dy_ref, w_ref, dx_ref, dw_ref, db_ref, xp_ref, dyp_ref):
        xp_ref[0:CONV_PAD, :] = jnp.zeros((CONV_PAD, cb), F32)
        xp_ref[CONV_PAD:CONV_PAD + t, :] = x_ref[...]
        dyp_ref[0:t, :] = dy_ref[...]
        dyp_ref[t:t + CONV_PAD, :] = jnp.zeros((CONV_PAD, cb), F32)

        def fold(v):
            return jnp.sum(v.reshape(rows // SUBLANE, SUBLANE, cb), axis=0)

        def step(s, carry):
            base = pl.multiple_of(s * rows, rows)
            dyc = dy_ref[pl.ds(base, rows), :]
            acc = jnp.zeros((rows, cb), F32)
            new = []
            for k in range(kw):
                acc = acc + w_ref[k:k + 1, :] * dyp_ref[pl.ds(base + (kw - 1) - k, rows), :]
                new.append(carry[k] + fold(dyc * xp_ref[pl.ds(base + CONV_PAD - (kw - 1) + k, rows), :]))
            new.append(carry[kw] + fold(dyc))
            dx_ref[pl.ds(base, rows), :] = acc.astype(dx_ref.dtype)
            return tuple(new)

        init = tuple(jnp.zeros((SUBLANE, cb), F32) for _ in range(kw + 1))
        parts = lax.fori_loop(0, t // rows, step, init)
        for k in range(kw):
            dw_ref[k:k + 1, :] = jnp.sum(parts[k], axis=0, keepdims=True)
        db_ref[...] = jnp.sum(parts[kw], axis=0, keepdims=True)

    return pl.pallas_call(
        body, name=name, grid=(c // cb,),
        in_specs=[pl.BlockSpec((t, cb), lambda j: (0, xoff + j)), pl.BlockSpec((t, cb), lambda j: (0, dyoff + j)),
                  pl.BlockSpec((kw, cb), lambda j: (0, j))],
        out_specs=[pl.BlockSpec((t, cb), lambda j: (0, j)), pl.BlockSpec((kw, cb), lambda j: (0, j)),
                   pl.BlockSpec((1, cb), lambda j: (0, j))],
        out_shape=[S((t, c), dx_dtype), S((kw, c), F32), S((1, c), F32)],
        scratch_shapes=[pltpu.VMEM((CONV_PAD + t, cb), F32), pltpu.VMEM((CONV_PAD + t, cb), F32)],
        compiler_params=pltpu.CompilerParams(dimension_semantics=("parallel",)),
    )(x.arr, dy.arr, w)


def gated_conv_fwd(name, hpre, w, b, out_dtype):
    t, c2 = hpre.shape
    ff = c2 // 2
    kw = w.shape[0]
    cb = LANE
    nb = ff // cb
    rows = min(CONV_ROWS, t)

    def body(h1_ref, h2_ref, w1_ref, w2_ref, b1_ref, b2_ref, y_ref, xp1_ref, xp2_ref):
        for xp_ref, h_ref in ((xp1_ref, h1_ref), (xp2_ref, h2_ref)):
            xp_ref[0:CONV_PAD, :] = jnp.zeros((CONV_PAD, cb), F32)
            xp_ref[CONV_PAD:CONV_PAD + t, :] = h_ref[...]

        def step(s, carry):
            base = pl.multiple_of(s * rows, rows)
            h1 = jnp.zeros((rows, cb), F32) + b1_ref[...]
            h2 = jnp.zeros((rows, cb), F32) + b2_ref[...]
            for k in range(kw):
                at = pl.ds(base + CONV_PAD - (kw - 1) + k, rows)
                h1 = h1 + w1_ref[k:k + 1, :] * xp1_ref[at, :]
                h2 = h2 + w2_ref[k:k + 1, :] * xp2_ref[at, :]
            y_ref[pl.ds(base, rows), :] = (_silu(h1) * h2).astype(y_ref.dtype)
            return carry

        lax.fori_loop(0, t // rows, step, 0)

    col1 = lambda r: pl.BlockSpec((r, cb), lambda j: (0, j))
    col2 = lambda r: pl.BlockSpec((r, cb), lambda j: (0, nb + j))
    return pl.pallas_call(
        body, name=name, grid=(nb,),
        in_specs=[col1(t), col2(t), col1(kw), col2(kw), col1(1), col2(1)],
        out_specs=col1(t), out_shape=S((t, ff), out_dtype),
        scratch_shapes=[pltpu.VMEM((CONV_PAD + t, cb), F32)] * 2,
        compiler_params=pltpu.CompilerParams(dimension_semantics=("parallel",)),
    )(hpre, hpre, w, w, b, b)


def gated_conv_bwd(name, hpre, dact, w, b, dx_dtype):
    t, c2 = hpre.shape
    ff = c2 // 2
    kw = w.shape[0]
    cb = LANE
    nb = ff // cb
    rows = min(CONV_ROWS, t)

    def body(own_ref, oth_ref, da_ref, wo_ref, wt_ref, bo_ref, bt_ref, dx_ref, dw_ref, db_ref,
             xpo_ref, xpt_ref, dhp_ref):
        for xp_ref, h_ref in ((xpo_ref, own_ref), (xpt_ref, oth_ref)):
            xp_ref[0:CONV_PAD, :] = jnp.zeros((CONV_PAD, cb), F32)
            xp_ref[CONV_PAD:CONV_PAD + t, :] = h_ref[...]
        dhp_ref[t:t + CONV_PAD, :] = jnp.zeros((CONV_PAD, cb), F32)

        def fold(v):
            return jnp.sum(v.reshape(rows // SUBLANE, SUBLANE, cb), axis=0)

        def first_pass(own_is_gate):
            def step(s, carry):
                base = pl.multiple_of(s * rows, rows)
                ho = jnp.zeros((rows, cb), F32) + bo_ref[...]
                ht = jnp.zeros((rows, cb), F32) + bt_ref[...]
                for k in range(kw):
                    at = pl.ds(base + CONV_PAD - (kw - 1) + k, rows)
                    ho = ho + wo_ref[k:k + 1, :] * xpo_ref[at, :]
                    ht = ht + wt_ref[k:k + 1, :] * xpt_ref[at, :]
                da = da_ref[pl.ds(base, rows), :]
                if own_is_gate:
                    sg = _sigmoid(ho)
                    dh = da * ht * (sg * (1.0 + ho * (1.0 - sg)))
                else:
                    dh = da * _silu(ht)
                dhp_ref[pl.ds(base, rows), :] = dh
                new = [carry[k] + fold(dh * xpo_ref[pl.ds(base + CONV_PAD - (kw - 1) + k, rows), :]) for k in range(kw)]
                new.append(carry[kw] + fold(dh))
                return tuple(new)

            init = tuple(jnp.zeros((SUBLANE, cb), F32) for _ in range(kw + 1))
            parts = lax.fori_loop(0, t // rows, step, init)
            for k in range(kw):
                dw_ref[k:k + 1, :] = jnp.sum(parts[k], axis=0, keepdims=True)
            db_ref[...] = jnp.sum(parts[kw], axis=0, keepdims=True)

        half = pl.program_id(0)

        @pl.when(half == 0)
        def _():
            first_pass(True)

        @pl.when(half == 1)
        def _():
            first_pass(False)

        def second(s, carry):
            base = pl.multiple_of(s * rows, rows)
            acc = jnp.zeros((rows, cb), F32)
            for k in range(kw):
                acc = acc + wo_ref[k:k + 1, :] * dhp_ref[pl.ds(base + (kw - 1) - k, rows), :]
            dx_ref[pl.ds(base, rows), :] = acc.astype(dx_ref.dtype)
            return carry

        lax.fori_loop(0, t // rows, second, 0)

    own = lambda r: pl.BlockSpec((r, cb), lambda h, j: (0, h * nb + j))
    oth = lambda r: pl.BlockSpec((r, cb), lambda h, j: (0, (1 - h) * nb + j))
    return pl.pallas_call(
        body, name=name, grid=(2, nb),
        in_specs=[own(t), oth(t), pl.BlockSpec((t, cb), lambda h, j: (0, j)), own(kw), oth(kw), own(1), oth(1)],
        out_specs=[own(t), own(kw), own(1)],
        out_shape=[S((t, c2), dx_dtype), S((kw, c2), F32), S((1, c2), F32)],
        scratch_shapes=[pltpu.VMEM((CONV_PAD + t, cb), F32)] * 3,
        compiler_params=pltpu.CompilerParams(dimension_semantics=("parallel", "parallel")),
    )(hpre, hpre, dact, w, w, b, b)


def _bdot(a, b, ca, cb):
    return lax.dot_general(a.astype(MXU_DTYPE), b.astype(MXU_DTYPE), (((ca,), (cb,)), ((0,), (0,))),
                           preferred_element_type=F32)


@jax.custom_vjp
def bmm_nn(a, b):
    return _bdot(a, b, 2, 1)


bmm_nn.defvjp(lambda a, b: (_bdot(a, b, 2, 1), (a, b)),
              lambda r, g: (_bdot(g, r[1], 2, 2), _bdot(r[0], g, 1, 1)))


@jax.custom_vjp
def bmm_tn(a, b):
    return _bdot(a, b, 1, 1)


bmm_tn.defvjp(lambda a, b: (_bdot(a, b, 1, 1), (a, b)),
              lambda r, g: (_bdot(r[1], g, 2, 2), _bdot(r[0], g, 2, 1)))


@jax.custom_vjp
def bmm_nt(a, b):
    return _bdot(a, b, 2, 2)


bmm_nt.defvjp(lambda a, b: (_bdot(a, b, 2, 2), (a, b)),
              lambda r, g: (_bdot(g, r[1], 2, 1), _bdot(g, r[0], 1, 1)))


def ssd_chunk(x, dt, bm, cm, hprev, a_log, dsk):
    hg, ln, _ = x.shape
    n = bm.shape[1]
    ii = lax.broadcasted_iota(jnp.int32, (ln, ln), 0)
    jj = lax.broadcasted_iota(jnp.int32, (ln, ln), 1)
    tril, eye, triu = (ii >= jj)[None], (ii == jj)[None], (ii <= jj)[None]
    da = dt * (-jnp.exp(a_log))
    da_row = jnp.sum(jnp.where(eye, da, 0.0), axis=1, keepdims=True)
    dt_row = jnp.sum(jnp.where(eye, dt, 0.0), axis=1, keepdims=True)
    cum_c = jnp.sum(jnp.where(tril, da_row, 0.0), axis=2, keepdims=True)
    cum_r = jnp.sum(jnp.where(triu, da, 0.0), axis=1, keepdims=True)
    last = jnp.sum(da, axis=1, keepdims=True)
    decay = jnp.where(tril, jnp.exp(jnp.where(tril, cum_c - cum_r, 0.0)), 0.0)
    cb = bmm_nt(cm[None], bm[None])
    y_diag = bmm_nn(cb * decay * dt_row, x)
    bb = jnp.broadcast_to(bm[None], (hg, ln, n))
    cc = jnp.broadcast_to(cm[None], (hg, ln, n))
    states = bmm_tn(x * (jnp.exp(last - cum_c) * dt), bb)
    y_off = bmm_nt(cc, hprev) * jnp.exp(cum_c)
    hnew = hprev * jnp.exp(last) + states
    return y_diag + y_off + dsk * x, hnew


def _ssd_dims(xs, bm, a_log):
    t = xs.shape[0]
    h = a_log.shape[0]
    return h, t, xs.shape[1] // h, h // N_GROUPS, bm.shape[1] // N_GROUPS, t // CHUNK


def _heads_of(ref, g, hg, p):
    return jnp.stack([ref[:, (g * hg + i) * p:(g * hg + i + 1) * p] for i in range(hg)])


def _cols_of(ref, g, hg):
    return jnp.stack([ref[:, g * hg + i:g * hg + i + 1] for i in range(hg)])


def ssd_fwd(name, xs, dt, bm, cm, a_log, dsk):
    h, t, p, hg, n, nc = _ssd_dims(xs, bm, a_log)

    def body(al_ref, dk_ref, x_ref, dt_ref, b_ref, c_ref, y_ref, hp_ref, h_scr):
        @pl.when(pl.program_id(0) == 0)
        def _():
            h_scr[...] = jnp.zeros_like(h_scr)

        for g in range(N_GROUPS):
            hs, ns = slice(g * hg, (g + 1) * hg), slice(g * n, (g + 1) * n)
            hprev = h_scr[hs]
            hp_ref[hs, 0] = hprev
            y, hnew = ssd_chunk(_heads_of(x_ref, g, hg, p), _cols_of(dt_ref, g, hg), b_ref[:, ns], c_ref[:, ns],
                                hprev, al_ref[hs], dk_ref[hs])
            for i in range(hg):
                y_ref[:, (g * hg + i) * p:(g * hg + i + 1) * p] = y[i]
            h_scr[hs] = hnew

    head = pl.BlockSpec((h, 1, 1), lambda c: (0, 0, 0))
    row = lambda w: pl.BlockSpec((CHUNK, w), lambda c: (c, 0))
    return pl.pallas_call(
        body, name=name, grid=(nc,),
        in_specs=[head, head, row(h * p), row(dt.shape[1]), row(N_GROUPS * n), row(N_GROUPS * n)],
        out_specs=[row(h * p), pl.BlockSpec((h, 1, p, n), lambda c: (0, c, 0, 0))],
        out_shape=[S((t, h * p), F32), S((h, nc, p, n), F32)],
        scratch_shapes=[pltpu.VMEM((h, p, n), F32)],
        compiler_params=pltpu.CompilerParams(dimension_semantics=("arbitrary",)),
    )(a_log, dsk, xs, dt, bm, cm)


def ssd_bwd(name, xs, dt, bm, cm, a_log, dsk, hp, dy):
    h, t, p, hg, n, nc = _ssd_dims(xs, bm, a_log)

    def body(al_ref, dk_ref, x_ref, dt_ref, b_ref, c_ref, hp_ref, dy_ref,
             dx_ref, ddt_ref, db_ref, dc_ref, dal_ref, ddk_ref, dh_scr):
        @pl.when(pl.program_id(0) == 0)
        def _():
            dh_scr[...] = jnp.zeros_like(dh_scr)
            dal_ref[...] = jnp.zeros_like(dal_ref)
            ddk_ref[...] = jnp.zeros_like(ddk_ref)

        ddt_ref[...] = jnp.zeros_like(ddt_ref)
        for g in range(N_GROUPS):
            hs, ns = slice(g * hg, (g + 1) * hg), slice(g * n, (g + 1) * n)
            _, vjp = jax.vjp(ssd_chunk, _heads_of(x_ref, g, hg, p), _cols_of(dt_ref, g, hg), b_ref[:, ns],
                             c_ref[:, ns], hp_ref[hs, 0], al_ref[hs], dk_ref[hs])
            gx, gdt, gb, gc, ghp, gal, gdk = vjp((_heads_of(dy_ref, g, hg, p), dh_scr[hs]))
            for i in range(hg):
                dx_ref[:, (g * hg + i) * p:(g * hg + i + 1) * p] = gx[i]
                ddt_ref[:, g * hg + i:g * hg + i + 1] = gdt[i]
            db_ref[:, ns] = gb
            dc_ref[:, ns] = gc
            dh_scr[hs] = ghp
            dal_ref[hs] += gal
            ddk_ref[hs] += gdk

    head = pl.BlockSpec((h, 1, 1), lambda c: (0, 0, 0))
    row = lambda w: pl.BlockSpec((CHUNK, w), lambda c: (nc - 1 - c, 0))
    return pl.pallas_call(
        body, name=name, grid=(nc,),
        in_specs=[head, head, row(h * p), row(dt.shape[1]), row(N_GROUPS * n), row(N_GROUPS * n),
                  pl.BlockSpec((h, 1, p, n), lambda c: (0, nc - 1 - c, 0, 0)), row(h * p)],
        out_specs=[row(h * p), row(dt.shape[1]), row(N_GROUPS * n), row(N_GROUPS * n), head, head],
        out_shape=[S((t, h * p), F32), S(dt.shape, F32), S(bm.shape, F32), S(cm.shape, F32),
                   S((h, 1, 1), F32), S((h, 1, 1), F32)],
        scratch_shapes=[pltpu.VMEM((h, p, n), F32)],
        compiler_params=pltpu.CompilerParams(dimension_semantics=("arbitrary",)),
    )(a_log, dsk, xs, dt, bm, cm, hp, dy)


def _alpha(depth):
    return (2.0 * depth) ** 0.25


def _pad_lanes(v):
    return jnp.pad(v, ((0, 0), (0, LANE - v.shape[1])))


def split_even_weights(w, j):
    d = w["e_w_in"][j].shape[1]
    da = w["e_conv_a_w"].shape[2]
    db = w["e_norm_b_g"].shape[1]
    gn = N_GROUPS * N_STATE
    nh = w["e_dt_bias"].shape[1]
    main = 2 * da + 2 * db + 2 * gn
    win = w["e_w_in"][j]
    ox = 2 * da + db
    cw, cbias = w["e_conv_b_w"][j], w["e_conv_b_b"][j][None]
    return dict(
        d=d, da=da, db=db, gn=gn, nh=nh, main=main,
        win_main=win[:main], win_dt=jnp.pad(win[main:], ((0, LANE - nh), (0, 0))),
        caw=w["e_conv_a_w"][j], cab=w["e_conv_a_b"][j][None], lag=w["e_ln_a_g"][j][None], lab=w["e_ln_a_b"][j][None],
        cw_xs=cw[:, :db], cw_b=cw[:, db:db + gn], cw_c=cw[:, db + gn:],
        cb_xs=cbias[:, :db], cb_b=cbias[:, db:db + gn], cb_c=cbias[:, db + gn:],
        dt_bias=_pad_lanes(w["e_dt_bias"][j][None]), a_log=w["e_a_log"][j].reshape(nh, 1, 1),
        dsk=w["e_d_skip"][j].reshape(nh, 1, 1), norm_g=w["e_norm_b_g"][j][None],
        wout_a=w["e_w_out"][j][:da], wout_b=w["e_w_out"][j][da:],
    )


def even_fwd(tag, x, xm, lw, ln_g, ln_b, alpha):
    t = x.shape[0]
    da, db, gn, nh = lw["da"], lw["db"], lw["gn"], lw["nh"]
    u = mm(tag + "_win", xm, lw["win_main"], "nt")
    udt = mm(tag + "_windt", xm, lw["win_dt"], "nt")
    ua, ug, z, xs_pre = Win(u, da, 0), Win(u, da, 1), Win(u, db, 2 * da // db), Win(u, db, (2 * da + db) // db)
    b_pre, c_pre = Win(u, gn, (2 * da + 2 * db) // gn), Win(u, gn, (2 * da + 2 * db + gn) // gn)
    (a0,) = rowwise(tag + "_glu", f_glu, [ua, ug], [], [da])
    a1 = conv_fwd(tag + "_conva", a0, lw["caw"], lw["cab"])
    (ya,) = rowwise(tag + "_lna", f_ln_silu, [a1], [lw["lag"], lw["lab"]], [da], out_dtypes=[MXU_DTYPE])
    xs_c = conv_fwd(tag + "_convxs", xs_pre, lw["cw_xs"], lw["cb_xs"])
    b_c = conv_fwd(tag + "_convb", b_pre, lw["cw_b"], lw["cb_b"])
    c_c = conv_fwd(tag + "_convc", c_pre, lw["cw_c"], lw["cb_c"])
    xs, bm, cm = rowwise(tag + "_silu3", f_silu3, [xs_c, b_c, c_c], [], [db, gn, gn])
    (dt,) = rowwise(tag + "_dt", f_softplus, [udt], [lw["dt_bias"]], [LANE])
    yssd, hp = ssd_fwd(tag + "_ssd", xs, dt, bm, cm, lw["a_log"], lw["dsk"])
    (yb,) = rowwise(tag + "_gate", f_gate_rms, [yssd, z], [lw["norm_g"]], [db], out_dtypes=[MXU_DTYPE])
    ma = mm(tag + "_wouta", ya, lw["wout_a"], "nn")
    mb = mm(tag + "_woutb", yb, lw["wout_b"], "nn")

    def f_res(xv, mav, mbv, g, b):
        pre = alpha * xv + mav + mbv
        y = _ln(pre, g, b)
        return y, y, pre

    x1, x1m, pre = rowwise(tag + "_res", f_res, [x, ma, mb], [ln_g, ln_b], [x.shape[1]] * 3,
                           out_dtypes=[F32, MXU_DTYPE, F32])
    saved = dict(xm=xm, u=u, udt=udt, a0=a0, a1=a1, ya=ya, xs_c=xs_c, b_c=b_c, c_c=c_c, xs=xs, dt=dt, bm=bm,
                 cm=cm, hp=hp, yssd=yssd, yb=yb, pre=pre)
    return x1, x1m, saved


def even_bwd(tag, dx1_pieces, sv, lw, ln_g, ln_b, alpha):
    t = sv["u"].shape[0]
    da, db, gn, nh = lw["da"], lw["db"], lw["gn"], lw["nh"]
    u, xm = sv["u"], sv["xm"]
    mx = (MXU_DTYPE,)
    ua, ug, z, xs_pre = Win(u, da, 0), Win(u, da, 1), Win(u, db, 2 * da // db), Win(u, db, (2 * da + db) // db)
    b_pre, c_pre = Win(u, gn, (2 * da + 2 * db) // gn), Win(u, gn, (2 * da + 2 * db + gn) // gn)
    (dpre, dprem), (dg0, db0) = rowwise_bwd(tag + "_res_b", f_ln, [sv["pre"]], [ln_g, ln_b], [dx1_pieces],
                                            d_dtypes=[(F32, MXU_DTYPE)])
    dya = mm(tag + "_dya", dprem, lw["wout_a"], "nt")
    dyb = mm(tag + "_dyb", dprem, lw["wout_b"], "nt")
    dwout_a = mm(tag + "_dwouta", sv["ya"], dprem, "tn", EXCHANGE_DTYPE)
    dwout_b = mm(tag + "_dwoutb", sv["yb"], dprem, "tn", EXCHANGE_DTYPE)
    (dyssd, dz), (dnorm_g,) = rowwise_bwd(tag + "_gate_b", f_gate_rms, [sv["yssd"], z], [lw["norm_g"]], [[dyb]],
                                          d_dtypes=[(F32,), mx])
    dxs, ddt, dbm, dcm, dalog, ddsk = ssd_bwd(tag + "_ssd_b", sv["xs"], sv["dt"], sv["bm"], sv["cm"],
                                              lw["a_log"], lw["dsk"], sv["hp"], dyssd)
    (dudt,), (ddt_bias,) = rowwise_bwd(tag + "_dt_b", f_softplus, [sv["udt"]], [lw["dt_bias"]], [[ddt]],
                                       d_dtypes=[mx])
    (dxs_c, db_c, dc_c), _ = rowwise_bwd(tag + "_silu3_b", f_silu3, [sv["xs_c"], sv["b_c"], sv["c_c"]], [],
                                         [[dxs], [dbm], [dcm]])
    dxs_pre, dcw_xs, dcb_xs = conv_bwd(tag + "_convxs_b", xs_pre, dxs_c, lw["cw_xs"], MXU_DTYPE)
    db_pre, dcw_b, dcb_b = conv_bwd(tag + "_convb_b", b_pre, db_c, lw["cw_b"], MXU_DTYPE)
    dc_pre, dcw_c, dcb_c = conv_bwd(tag + "_convc_b", c_pre, dc_c, lw["cw_c"], MXU_DTYPE)
    (da1,), (dlag, dlab) = rowwise_bwd(tag + "_lna_b", f_ln_silu, [sv["a1"]], [lw["lag"], lw["lab"]], [[dya]])
    da0, dcaw, dcab = conv_bwd(tag + "_conva_b", sv["a0"], da1, lw["caw"])
    (dua, dug), _ = rowwise_bwd(tag + "_glu_b", f_glu, [ua, ug], [], [[da0]], d_dtypes=[mx, mx])
    du = jnp.concatenate([dua, dug, dz, dxs_pre, db_pre, dc_pre], axis=1)
    dx_m = mm(tag + "_dxm", du, lw["win_main"], "nn")
    dx_dt = mm(tag + "_dxdt", dudt, lw["win_dt"], "nn")
    dwin_main = mm(tag + "_dwin", du, xm, "tn", EXCHANGE_DTYPE)
    dwin_dt = mm(tag + "_dwindt", dudt, xm, "tn", EXCHANGE_DTYPE)
    grads = dict(
        e_w_in=jnp.concatenate([dwin_main, dwin_dt[:nh]], axis=0),
        e_conv_a_w=dcaw, e_conv_a_b=dcab[0], e_ln_a_g=dlag[0], e_ln_a_b=dlab[0],
        e_conv_b_w=jnp.concatenate([dcw_xs, dcw_b, dcw_c], axis=1),
        e_conv_b_b=jnp.concatenate([dcb_xs, dcb_b, dcb_c], axis=1)[0],
        e_dt_bias=ddt_bias[0, :nh], e_a_log=dalog.reshape(nh), e_d_skip=ddsk.reshape(nh), e_norm_b_g=dnorm_g[0],
        e_w_out=jnp.concatenate([dwout_a, dwout_b], axis=0), ln_g0=dg0[0], ln_b0=db0[0],
    )
    return [Win(dpre, coef=alpha), dx_m, dx_dt], grads


def odd_fwd(tag, x, xm, w, j, ln_g, ln_b, alpha):
    d = x.shape[1]
    u = mm(tag + "_win", xm, w["o_w_in"][j], "nt")
    bg, cg, v = Win(u, d, 0), Win(u, d, 1), Win(u, d, 2)
    (s,) = rowwise(tag + "_cv", f_mul, [cg, v], [], [d])
    cs = conv_fwd(tag + "_conv", s, w["o_conv_w"][j], None)
    (m,) = rowwise(tag + "_bm", f_mul, [bg, cs], [], [d], out_dtypes=[MXU_DTYPE])
    mix = mm(tag + "_wout", m, w["o_w_out"][j], "nn")

    def f_res(xv, mv, g, b):
        pre = alpha * xv + mv
        y = _ln(pre, g, b)
        return y, y, pre

    x1, x1m, pre = rowwise(tag + "_res", f_res, [x, mix], [ln_g, ln_b], [d] * 3, out_dtypes=[F32, MXU_DTYPE, F32])
    return x1, x1m, dict(xm=xm, u=u, s=s, cs=cs, m=m, pre=pre)


def odd_bwd(tag, dx1_pieces, sv, w, j, ln_g, ln_b, alpha):
    xm, u = sv["xm"], sv["u"]
    d = xm.shape[1]
    mx = (MXU_DTYPE,)
    bg, cg, v = Win(u, d, 0), Win(u, d, 1), Win(u, d, 2)
    (dpre, dprem), (dg0, db0) = rowwise_bwd(tag + "_res_b", f_ln, [sv["pre"]], [ln_g, ln_b], [dx1_pieces],
                                            d_dtypes=[(F32, MXU_DTYPE)])
    dm = mm(tag + "_dm", dprem, w["o_w_out"][j], "nt")
    dwout = mm(tag + "_dwout", sv["m"], dprem, "tn", EXCHANGE_DTYPE)
    (dbg, dcs), _ = rowwise_bwd(tag + "_bm_b", f_mul, [bg, sv["cs"]], [], [[dm]], d_dtypes=[mx, (F32,)])
    ds, dcw, _ = conv_bwd(tag + "_conv_b", sv["s"], dcs, w["o_conv_w"][j])
    (dcg, dv), _ = rowwise_bwd(tag + "_cv_b", f_mul, [cg, v], [], [[ds]], d_dtypes=[mx, mx])
    du = jnp.concatenate([dbg, dcg, dv], axis=1)
    dx_u = mm(tag + "_dx", du, w["o_w_in"][j], "nn")
    dwin = mm(tag + "_dwin", du, xm, "tn", EXCHANGE_DTYPE)
    grads = dict(o_w_in=dwin, o_conv_w=dcw, o_w_out=dwout, ln_g0=dg0[0], ln_b0=db0[0])
    return [Win(dpre, coef=alpha), dx_u], grads


def ffn_fwd(tag, x1, x1m, p_i, w, i, ln_g, ln_b, alpha):
    d = x1.shape[1]
    hpre = mm(tag + "_wup", x1m, w["f_w_up"][i], "nt")
    act = gated_conv_fwd(tag + "_fgate", hpre, w["f_conv_w"][i], w["f_conv_b"][i][None], MXU_DTYPE)
    ffn = mm(tag + "_wdown", act, w["f_w_down"][i], "nn")
    pp = mm(tag + "_pproj", p_i, w["ple_w_proj"][i], "nt")
    gt = mm(tag + "_pgate", x1m, w["ple_w_gate"][i], "nn")

    def f_res2(xv, fv, ppv, gtv, g, b):
        pre = alpha * xv + fv + ppv * _sigmoid(gtv)
        y = _ln(pre, g, b)
        return y, y, pre

    x2, x2m, pre = rowwise(tag + "_res2", f_res2, [x1, ffn, pp, gt], [ln_g, ln_b], [d] * 3,
                           out_dtypes=[F32, MXU_DTYPE, F32])
    return x2, x2m, dict(x1m=x1m, hpre=hpre, act=act, pp=pp, gt=gt, pre=pre)


def ffn_bwd(tag, dx2_pieces, sv, p_i, w, i, ln_g, ln_b, alpha):
    x1m = sv["x1m"]
    mx = (MXU_DTYPE,)
    (dpre, dprem), (dg1, db1) = rowwise_bwd(tag + "_res2_b", f_ln, [sv["pre"]], [ln_g, ln_b], [dx2_pieces],
                                            d_dtypes=[(F32, MXU_DTYPE)])
    (dpp, dgt), _ = rowwise_bwd(tag + "_pg_b", f_gate_mul, [sv["pp"], sv["gt"]], [], [[dpre]], d_dtypes=[mx, mx])
    dwproj = mm(tag + "_dwproj", dpp, p_i, "tn", EXCHANGE_DTYPE)
    dwgate = mm(tag + "_dwgate", x1m, dgt, "tn", EXCHANGE_DTYPE)
    dx1_a = mm(tag + "_dx1a", dgt, w["ple_w_gate"][i], "nt")
    dact = mm(tag + "_dact", dprem, w["f_w_down"][i], "nt")
    dwdown = mm(tag + "_dwdown", sv["act"], dprem, "tn", EXCHANGE_DTYPE)
    dhpre, dfcw, dfcb = gated_conv_bwd(tag + "_fgate_b", sv["hpre"], dact, w["f_conv_w"][i], w["f_conv_b"][i][None],
                                       MXU_DTYPE)
    dwup = mm(tag + "_dwup", dhpre, x1m, "tn", EXCHANGE_DTYPE)
    dx1_b = mm(tag + "_dx1b", dhpre, w["f_w_up"][i], "nn")
    grads = dict(f_w_up=dwup, f_conv_w=dfcw, f_conv_b=dfcb[0], f_w_down=dwdown, ple_w_proj=dwproj, ple_w_gate=dwgate,
                 ln_g1=dg1[0], ln_b1=db1[0])
    return [Win(dpre, coef=alpha), dx1_a, dx1_b], grads


def local_step(x, p, w, target):
    depth = w["ln_g"].shape[0]
    alpha = _alpha(depth)
    d = x.shape[1]
    saved = []
    h = hm = x
    for i in range(depth):
        j = i // 2
        g0, b0, g1, b1 = w["ln_g"][i, 0][None], w["ln_b"][i, 0][None], w["ln_g"][i, 1][None], w["ln_b"][i, 1][None]
        tag = "l%d" % i
        if i % 2 == 0:
            lw = split_even_weights(w, j)
            h, hm, sv_m = even_fwd(tag, h, hm, lw, g0, b0, alpha)
        else:
            lw = None
            h, hm, sv_m = odd_fwd(tag, h, hm, w, j, g0, b0, alpha)
        h, hm, sv_f = ffn_fwd(tag, h, hm, p[i], w, i, g1, b1, alpha)
        saved.append((lw, sv_m, sv_f, (g0, b0, g1, b1)))

    def f_loss(xf, tg):
        diff = xf - tg
        sq = jnp.sum(jnp.sum(jnp.square(diff), axis=1, keepdims=True), axis=0, keepdims=True)
        return diff * (1.0 / d), jnp.broadcast_to(sq, (1, LANE))

    dxf, sq = rowwise("loss", f_loss, [h, target], [], [d], red_widths=[LANE])
    loss = sq[0, 0] * (0.5 / d)

    per_layer = []
    pieces = [dxf]
    for i in reversed(range(depth)):
        j = i // 2
        lw, sv_m, sv_f, (g0, b0, g1, b1) = saved[i]
        tag = "l%d" % i
        pieces, gf = ffn_bwd(tag, pieces, sv_f, p[i], w, i, g1, b1, alpha)
        if i % 2 == 0:
            pieces, gm = even_bwd(tag, pieces, sv_m, lw, g0, b0, alpha)
        else:
            pieces, gm = odd_bwd(tag, pieces, sv_m, w, j, g0, b0, alpha)
        per_layer.append((i, gm, gf))

    def f_sum(*vs):
        acc = None
        for v, c in zip(vs, [pc.coef for pc in map(_win, pieces)]):
            v = v if c == 1.0 else v * c
            acc = v if acc is None else acc + v
        return (acc,)

    (grad_x,) = rowwise("grad_x", f_sum, [Win(_win(pc).arr) for pc in pieces], [], [d])

    by_layer = {i: (gm, gf) for i, gm, gf in per_layer}
    grads = {}
    n_even, n_odd = (depth + 1) // 2, depth // 2
    collect = lambda name, per_layer: per_layer if name in BIG else jnp.stack(per_layer)
    for name in ("e_w_in", "e_conv_a_w", "e_conv_a_b", "e_ln_a_g", "e_ln_a_b", "e_conv_b_w", "e_conv_b_b", "e_dt_bias",
                 "e_a_log", "e_d_skip", "e_norm_b_g", "e_w_out"):
        grads[name] = collect(name, [by_layer[2 * j][0][name] for j in range(n_even)])
    for name in ("o_w_in", "o_conv_w", "o_w_out"):
        grads[name] = collect(name, [by_layer[2 * j + 1][0][name] for j in range(n_odd)])
    for name in ("f_w_up", "f_conv_w", "f_conv_b", "f_w_down", "ple_w_proj", "ple_w_gate"):
        grads[name] = collect(name, [by_layer[i][1][name] for i in range(depth)])
    grads["ln_g"] = jnp.stack([jnp.stack([by_layer[i][0]["ln_g0"], by_layer[i][1]["ln_g1"]]) for i in range(depth)])
    grads["ln_b"] = jnp.stack([jnp.stack([by_layer[i][0]["ln_b0"], by_layer[i][1]["ln_b1"]]) for i in range(depth)])
    return loss, grad_x, grads


_ANY = pl.BlockSpec(memory_space=pl.ANY)
_MESH = pl.DeviceIdType.MESH


def all_gather(name, xl):
    r, c_ = xl.shape
    split = r // 2 // 16 * 16
    halves = ((0, split), (split, r - split)) if split else ((0, r),)
    two = len(halves) == 2

    def body(x_ref, out_ref, send_sems, recv_sems, local_sem):
        x, y, c = lax.axis_index("x"), lax.axis_index("y"), lax.axis_index("c")
        me, sibling, xn, yn, dg = (x, y, c), (x, y, 1 - c), (1 - x, y, c), (x, 1 - y, c), (1 - x, 1 - y, c)

        def rows(block, h):
            ref = out_ref.at[4 * block[0] + 2 * block[1] + block[2]]
            return ref if h is None else ref.at[pl.ds(*halves[h])]

        def copy(k, block, h, to, own=False):
            src = (x_ref if h is None else x_ref.at[pl.ds(*halves[h])]) if own else rows(block, h)
            return pltpu.make_async_remote_copy(src_ref=src, dst_ref=rows(block, h), send_sem=send_sems.at[k],
                                                recv_sem=recv_sems.at[k], device_id=to, device_id_type=_MESH)

        def other_core(block):
            return (block[0], block[1], 1 - c)

        mine = pltpu.make_async_copy(x_ref, rows(me, None), local_sem)
        mine.start()
        direct = [copy(0, me, 0, xn, own=True), copy(1, me, 1 if two else 0, yn, own=True)]
        if two:
            direct += [copy(2, me, 1, xn, own=True), copy(3, me, 0, yn, own=True)]
        direct.append(copy(6, me, None, sibling, own=True))
        for cp in direct:
            cp.start()
        started = list(direct)

        def then(waits, nxt):
            for cp in waits:
                cp.wait_recv()
            for cp in nxt:
                cp.start()
            started.extend(nxt)

        if two:
            then([copy(0, xn, 0, me)], [copy(4, xn, 0, yn)])
            then([copy(1, yn, 1, me)], [copy(5, yn, 1, xn)])
            then([copy(2, xn, 1, me)], [copy(7, xn, None, sibling)])
            then([copy(3, yn, 0, me)], [copy(8, yn, None, sibling)])
            then([copy(4, dg, 0, me), copy(5, dg, 1, me)], [copy(9, dg, None, sibling)])
        else:
            then([copy(0, xn, 0, me)], [copy(4, xn, 0, yn), copy(7, xn, None, sibling)])
            then([copy(1, yn, 0, me)], [copy(8, yn, None, sibling)])
            then([copy(4, dg, 0, me)], [copy(9, dg, None, sibling)])
        for k, block in ((6, me), (7, xn), (8, yn), (9, dg)):
            copy(k, other_core(block), None, me).wait_recv()
        for cp in started:
            cp.wait_send()
        mine.wait()

    return pl.pallas_call(
        body, name=name, out_shape=S((N_DEV, r, c_), xl.dtype), in_specs=[_ANY], out_specs=_ANY,
        scratch_shapes=[pltpu.SemaphoreType.DMA((10,)), pltpu.SemaphoreType.DMA((10,)), pltpu.SemaphoreType.DMA],
    )(xl)


def _exchange(name, srcs, out_slots, copies_of):
    n = len(srcs)
    n_copies = sum(s.shape[0] for s in srcs) * out_slots

    def body(*refs):
        src_refs, out_refs, send_sems, recv_sems = refs[:n], refs[n:2 * n], refs[2 * n], refs[2 * n + 1]
        pattern = copies_of(lax.axis_index("x"), lax.axis_index("y"), lax.axis_index("c"))
        copies = []
        for s_ref, o_ref in zip(src_refs, out_refs):
            for l in range(s_ref.shape[0]):
                for src_at, slot, target in pattern:
                    k = len(copies)
                    copies.append(pltpu.make_async_remote_copy(
                        src_ref=s_ref.at[(l,) + src_at], dst_ref=o_ref.at[l, slot], send_sem=send_sems.at[k],
                        recv_sem=recv_sems.at[k], device_id=target, device_id_type=_MESH))
        for cp in copies:
            cp.start()
        for cp in copies:
            cp.wait()

    return pl.pallas_call(
        body, name=name, out_shape=[S((s.shape[0], out_slots) + s.shape[-2:], s.dtype) for s in srcs],
        in_specs=[_ANY] * n, out_specs=[_ANY] * n,
        scratch_shapes=[pltpu.SemaphoreType.DMA((n_copies,)), pltpu.SemaphoreType.DMA((n_copies,))],
    )(*srcs)


def exchange_sibling(name, gs):
    return _exchange(name, gs, 4, lambda x, y, c: [((q, 1 - c), q, (x, y, 1 - c)) for q in range(4)])


def exchange_chips(name, ps):
    def pattern(x, y, c):
        chips = [(1 - x, y), (x, 1 - y), (1 - x, 1 - y)]
        return [((2 * cx + cy,), k, (cx, cy, c)) for k, (cx, cy) in enumerate(chips)]
    return _exchange(name, ps, 3, pattern)


def sum_with_sibling(name, g5, recv, core):
    l, _, _, a, b = g5.shape

    def body(core_ref, g_ref, r_ref, o_ref, ox_ref):
        s = g_ref[0, 0, 0].astype(F32) + r_ref[0, 0].astype(F32)
        o_ref[0, 0] = s
        ox_ref[0, 0] = s.astype(ox_ref.dtype)

    slab = pl.BlockSpec((1, 1, a, b), lambda i, q, cr: (i, q, 0, 0))
    return pl.pallas_call(
        body, name=name, out_shape=[S((l, 4, a, b), F32), S((l, 4, a, b), EXCHANGE_DTYPE)],
        grid_spec=pltpu.PrefetchScalarGridSpec(
            num_scalar_prefetch=1, grid=(l, 4),
            in_specs=[pl.BlockSpec((1, 1, 1, a, b), lambda i, q, cr: (i, q, cr[0], 0, 0)), slab],
            out_specs=[slab, slab]),
    )(core, g5, recv)


def sum_with_chips(name, p4, recv, chip):
    l, _, a, b = p4.shape

    def body(chip_ref, p_ref, r_ref, o_ref):
        o_ref[0] = ((p_ref[0, 0] + r_ref[0, 0].astype(F32)) + r_ref[0, 1].astype(F32)) + r_ref[0, 2].astype(F32)

    return pl.pallas_call(
        body, name=name, out_shape=S((l, a, b), F32),
        grid_spec=pltpu.PrefetchScalarGridSpec(
            num_scalar_prefetch=1, grid=(l,),
            in_specs=[pl.BlockSpec((1, 1, a, b), lambda i, ch: (i, ch[0], 0, 0)),
                      pl.BlockSpec((1, 3, a, b), lambda i, ch: (i, 0, 0, 0))],
            out_specs=pl.BlockSpec((1, a, b), lambda i, ch: (i, 0, 0))),
    )(chip, p4, recv)


def sum_devices(name, g8):
    _, r, c_ = g8.shape

    def body(g_ref, o_ref):
        acc = g_ref[0]
        for k in range(1, N_DEV):
            acc = acc + g_ref[k]
        o_ref[...] = acc

    return pl.pallas_call(body, name=name, out_shape=S((r, c_), F32))(g8)


def _flatten(parts, cols, row_mult):
    flat = jnp.concatenate([v.reshape(-1) for v in parts])
    n = flat.shape[0]
    rows = -(-n // (cols * row_mult)) * row_mult
    return jnp.pad(flat, (0, rows * cols - n)).reshape(rows, cols)


def _exchange_dims(name, lshape):
    l, r, c = lshape
    return (l, c, r) if name in TRANSPOSED else (l, r, c)


def gather_big(name, local):
    cols = local["e_w_out"].shape[2]
    parts = []
    for n in EXCHANGE_ORDER:
        v = local[n].astype(MXU_DTYPE)
        parts.append((v.transpose(0, 2, 1) if n in TRANSPOSED else v).reshape(-1, cols))
    got = all_gather(name, jnp.concatenate(parts, axis=0))
    full, r0 = {}, 0
    for n, part in zip(EXCHANGE_ORDER, parts):
        l, a, b = _exchange_dims(n, local[n].shape)
        per = part.shape[0] // l
        full[n] = [got[:, r0 + i * per:r0 + (i + 1) * per].reshape(N_DEV * a, b) for i in range(l)]
        r0 += part.shape[0]
    return full


def reduce_scatter_big(grads, local_shapes):
    x, y, c = lax.axis_index("x"), lax.axis_index("y"), lax.axis_index("c")
    core, chip = c.astype(jnp.int32).reshape(1), (2 * x + y).astype(jnp.int32).reshape(1)
    g5 = []
    for n in EXCHANGE_ORDER:
        l, a, b = _exchange_dims(n, local_shapes[n])
        g5.append(jnp.stack(grads[n]).reshape(l, 4, 2, a, b))
    from_sibling = exchange_sibling("rs_sibling", g5)
    sums = [sum_with_sibling("rs_sum_sibling_" + n, g, r, core) for n, g, r in zip(EXCHANGE_ORDER, g5, from_sibling)]
    from_chips = exchange_chips("rs_chips", [px for _, px in sums])
    out = {}
    for n, (p4, _), r in zip(EXCHANGE_ORDER, sums, from_chips):
        seg = sum_with_chips("rs_sum_chips_" + n, p4, r, chip)
        out[n] = seg.transpose(0, 2, 1) if n in TRANSPOSED else seg
    return out


def gather_small(name, local, names):
    flat = _flatten([local[n] for n in names], LANE, 1)
    got = all_gather(name, flat).reshape(N_DEV, -1)
    full, off = {}, 0
    for n in names:
        size = math.prod(local[n].shape)
        seg = got[:, off:off + size].reshape((N_DEV,) + local[n].shape)
        full[n] = seg.transpose(1, 2, 0, 3).reshape(seg.shape[1], seg.shape[2], -1)
        off += size
    return full


def all_reduce_small(grads, names):
    flat = _flatten([grads[n] for n in names], LANE, SUBLANE)
    total = sum_devices("ar_sum", all_gather("ar_gather", flat)).reshape(-1)
    out, off = {}, 0
    for nm in names:
        size = math.prod(grads[nm].shape)
        out[nm] = total[off:off + size].reshape(grads[nm].shape)
        off += size
    return out


def adamw(name, w, g, m, v):
    shape = w.shape
    cols = shape[-1]
    rows = math.prod(shape[:-1])
    tr = _pick(rows, (256, 128, 64, 32, 16, 8)) if rows * cols > 256 * 1024 else rows
    c1 = 1.0 - ADAM_B1 ** ADAM_STEP
    c2 = 1.0 - ADAM_B2 ** ADAM_STEP

    def body(w_ref, g_ref, m_ref, v_ref, d_ref, nm_ref, nv_ref):
        gv = g_ref[...]
        m2 = ADAM_B1 * m_ref[...] + (1.0 - ADAM_B1) * gv
        v2 = ADAM_B2 * v_ref[...] + (1.0 - ADAM_B2) * jnp.square(gv)
        d_ref[...] = -ADAM_LR * ((m2 / c1) / (jnp.sqrt(v2 / c2) + ADAM_EPS) + ADAM_WD * w_ref[...])
        nm_ref[...] = m2
        nv_ref[...] = v2

    spec = pl.BlockSpec((tr, cols), lambda i: (i, 0))
    outs = pl.pallas_call(
        body, name=name, grid=(rows // tr,), in_specs=[spec] * 4, out_specs=[spec] * 3,
        out_shape=[S((rows, cols), F32)] * 3,
        compiler_params=pltpu.CompilerParams(dimension_semantics=("parallel",)),
    )(*[a.reshape(rows, cols) for a in (w, g, m, v)])
    return tuple(o.reshape(shape) for o in outs)


def kernel(x, p, e_w_in, e_conv_a_w, e_conv_a_b, e_ln_a_g, e_ln_a_b, e_conv_b_w, e_conv_b_b, e_dt_bias, e_a_log, e_d_skip, e_norm_b_g, e_w_out, o_w_in, o_conv_w, o_w_out, f_w_up, f_conv_w, f_conv_b, f_w_down, ple_w_proj, ple_w_gate, ln_g, ln_b, loss_target, m_e_w_in, m_e_conv_a_w, m_e_conv_a_b, m_e_ln_a_g, m_e_ln_a_b, m_e_conv_b_w, m_e_conv_b_b, m_e_dt_bias, m_e_a_log, m_e_d_skip, m_e_norm_b_g, m_e_w_out, m_o_w_in, m_o_conv_w, m_o_w_out, m_f_w_up, m_f_conv_w, m_f_conv_b, m_f_w_down, m_ple_w_proj, m_ple_w_gate, m_ln_g, m_ln_b, v_e_w_in, v_e_conv_a_w, v_e_conv_a_b, v_e_ln_a_g, v_e_ln_a_b, v_e_conv_b_w, v_e_conv_b_b, v_e_dt_bias, v_e_a_log, v_e_d_skip, v_e_norm_b_g, v_e_w_out, v_o_w_in, v_o_conv_w, v_o_w_out, v_f_w_up, v_f_conv_w, v_f_conv_b, v_f_w_down, v_ple_w_proj, v_ple_w_gate, v_ln_g, v_ln_b):
    args = locals()
    local = {n: args[n] for n in WEIGHTS}
    mom = {n: args["m_" + n] for n in WEIGHTS}
    var = {n: args["v_" + n] for n in WEIGHTS}

    full = {n: local[n] for n in REPLICATED}
    full.update(gather_big("ag_big", local))
    full.update(gather_small("ag_small", local, SMALL_SHARDED))

    loss_local, grad_x, grads = local_step(x[0], p[:, 0], full, loss_target[0])
    loss = lax.psum(loss_local, MESH_AXES)

    g_local = reduce_scatter_big(grads, {n: local[n].shape for n in BIG})
    small = all_reduce_small(grads, REPLICATED + SMALL_SHARDED)
    dev = 4 * lax.axis_index("x") + 2 * lax.axis_index("y") + lax.axis_index("c")
    for n in REPLICATED:
        g_local[n] = small[n]
    for n in SMALL_SHARDED:
        width = local[n].shape[2]
        g_local[n] = lax.dynamic_slice_in_dim(small[n], dev * width, width, axis=2)

    delta, new_m, new_v = {}, {}, {}
    for n in WEIGHTS:
        delta[n], new_m[n], new_v[n] = adamw("adamw_" + n, local[n], g_local[n], mom[n], var[n])
    return (loss, grad_x[None], *[g_local[n] for n in WEIGHTS], *[delta[n] for n in WEIGHTS],
            *[new_m[n] for n in WEIGHTS], *[new_v[n] for n in WEIGHTS])
```

```python
import functools
import math

import jax
import jax.numpy as jnp
from jax import lax
from jax.experimental import pallas as pl
from jax.experimental.pallas import tpu as pltpu

F32 = jnp.float32
MXU_DTYPE = jnp.bfloat16
MESH_AXES = ("x", "y", "c")
N_DEV = 8
LANE = 128
SUBLANE = 8
ROWWISE_VMEM_BUDGET = 20 * 1024 * 1024
MM_TILES = (1408, 1024, 512, 256, 128)
EXCHANGE_DTYPE = jnp.bfloat16
LN_EPS = 1e-5
CHUNK = 64
HEAD_DIM = 64
N_GROUPS = 4
N_STATE = 128
CONV_PAD = 32
CONV_ROWS = 256
ADAM_LR, ADAM_B1, ADAM_B2, ADAM_EPS, ADAM_WD, ADAM_STEP = 0.001, 0.9, 0.999, 1e-08, 0.01, 10

BIG = ("e_w_in", "e_w_out", "o_w_in", "o_w_out", "f_w_up", "f_w_down", "ple_w_proj", "ple_w_gate")
SMALL_SHARDED = ("e_conv_a_w", "e_conv_b_w", "o_conv_w", "f_conv_w", "ln_g", "ln_b")
REPLICATED = ("e_conv_a_b", "e_ln_a_g", "e_ln_a_b", "e_conv_b_b", "e_dt_bias", "e_a_log", "e_d_skip",
              "e_norm_b_g", "f_conv_b")
TRANSPOSED = ("e_w_in", "o_w_in", "f_w_up", "ple_w_proj")
EXCHANGE_ORDER = ("e_w_out", "o_w_out", "ple_w_gate", "ple_w_proj", "o_w_in", "f_w_down", "f_w_up", "e_w_in")
WEIGHTS = ("e_w_in", "e_conv_a_w", "e_conv_a_b", "e_ln_a_g", "e_ln_a_b", "e_conv_b_w", "e_conv_b_b", "e_dt_bias",
           "e_a_log", "e_d_skip", "e_norm_b_g", "e_w_out", "o_w_in", "o_conv_w", "o_w_out", "f_w_up", "f_conv_w",
           "f_conv_b", "f_w_down", "ple_w_proj", "ple_w_gate", "ln_g", "ln_b")

S = jax.ShapeDtypeStruct


class Win:
    def __init__(self, arr, w=None, idx=0, coef=1.0):
        self.arr, self.w, self.idx, self.coef = arr, (arr.shape[1] if w is None else w), idx, coef


def _win(a):
    return a if isinstance(a, Win) else Win(a)


def _pick(n, prefs):
    for p in prefs:
        if p <= n and n % p == 0:
            return p
    return n


_MM_DIMS = {"nn": (1, 0), "nt": (1, 1), "tn": (0, 0)}


def mm(name, a, b, mode, out_dtype=F32):
    ca, cb = _MM_DIMS[mode]
    kdim = a.shape[ca]
    m = a.shape[1 - ca]
    n = b.shape[1 - cb]
    assert b.shape[cb] == kdim, (name, a.shape, b.shape, mode)
    tm = _pick(m, MM_TILES)
    tn = _pick(n, MM_TILES)
    tk = kdim if kdim <= MM_TILES[0] else _pick(kdim, MM_TILES)
    nk = kdim // tk
    own_acc = nk > 1 and out_dtype != F32

    def body(a_ref, b_ref, o_ref, *scratch):
        acc_ref = scratch[0] if own_acc else o_ref
        d = lax.dot_general(a_ref[...].astype(MXU_DTYPE), b_ref[...].astype(MXU_DTYPE),
                            (((ca,), (cb,)), ((), ())), preferred_element_type=F32)
        if nk == 1:
            o_ref[...] = d.astype(o_ref.dtype)
        else:
            k = pl.program_id(2)

            @pl.when(k == 0)
            def _():
                acc_ref[...] = d

            @pl.when(k > 0)
            def _():
                acc_ref[...] += d

            if own_acc:
                @pl.when(k == nk - 1)
                def _():
                    o_ref[...] = acc_ref[...].astype(o_ref.dtype)

    a_spec = pl.BlockSpec((tm, tk), lambda i, j, k: (i, k)) if ca == 1 else pl.BlockSpec((tk, tm), lambda i, j, k: (k, i))
    b_spec = pl.BlockSpec((tk, tn), lambda i, j, k: (k, j)) if cb == 0 else pl.BlockSpec((tn, tk), lambda i, j, k: (j, k))
    return pl.pallas_call(
        body, name=name, grid=(m // tm, n // tn, nk),
        in_specs=[a_spec, b_spec], out_specs=pl.BlockSpec((tm, tn), lambda i, j, k: (i, j)),
        out_shape=S((m, n), out_dtype), scratch_shapes=[pltpu.VMEM((tm, tn), F32)] if own_acc else [],
        compiler_params=pltpu.CompilerParams(dimension_semantics=("parallel", "parallel", "arbitrary")),
    )(a, b)


def _row_block(t, widths):
    tb = 512
    while tb > SUBLANE and (t % tb or tb * sum(widths) * 8 > ROWWISE_VMEM_BUDGET):
        tb //= 2
    return tb


def _tok_spec(tb, w):
    return pl.BlockSpec((tb, w.w), functools.partial(lambda i, idx: (i, idx), idx=w.idx))


def _par_spec(p):
    return pl.BlockSpec((1, p.shape[1]), lambda i: (0, 0))


def rowwise(name, fn, tok, par, out_widths, red_widths=(), out_dtypes=None):
    tok = [_win(t) for t in tok]
    t = tok[0].arr.shape[0]
    tb = _row_block(t, [w.w for w in tok] + list(out_widths))
    n_tok, n_par, n_out = len(tok), len(par), len(out_widths)
    out_dtypes = [F32] * n_out if out_dtypes is None else out_dtypes

    def body(*refs):
        ins = [r[...] for r in refs[:n_tok + n_par]]
        res = fn(*ins)
        out_refs = refs[n_tok + n_par:n_tok + n_par + n_out]
        red_refs = refs[n_tok + n_par + n_out:]
        for r, v in zip(out_refs, res[:n_out]):
            r[...] = v.astype(r.dtype)
        if red_refs:
            @pl.when(pl.program_id(0) == 0)
            def _():
                for r in red_refs:
                    r[...] = jnp.zeros_like(r)
            for r, v in zip(red_refs, res[n_out:]):
                r[...] += v

    outs = pl.pallas_call(
        body, name=name, grid=(t // tb,),
        in_specs=[_tok_spec(tb, w) for w in tok] + [_par_spec(p) for p in par],
        out_specs=[pl.BlockSpec((tb, w), lambda i: (i, 0)) for w in out_widths]
        + [pl.BlockSpec((1, w), lambda i: (0, 0)) for w in red_widths],
        out_shape=[S((t, w), dt) for w, dt in zip(out_widths, out_dtypes)] + [S((1, w), F32) for w in red_widths],
        compiler_params=pltpu.CompilerParams(dimension_semantics=("arbitrary",)),
    )(*[w.arr for w in tok], *par)
    return outs


def rowwise_bwd(name, fn, tok, par, cts, d_dtypes=None):
    tok = [_win(t) for t in tok]
    cts = [[_win(c) for c in group] for group in cts]
    d_dtypes = [(F32,)] * len(tok) if d_dtypes is None else d_dtypes
    t = tok[0].arr.shape[0]
    flat_cts = [c for group in cts for c in group]
    d_outs = [(i, w.w, dt) for i, (w, dts) in enumerate(zip(tok, d_dtypes)) for dt in dts]
    tb = _row_block(t, [w.w for w in tok] + [c.w for c in flat_cts] + [w for _, w, _ in d_outs])
    n_tok, n_par, n_ct, n_d = len(tok), len(par), len(flat_cts), len(d_outs)

    def body(*refs):
        tok_vals = [r[...] for r in refs[:n_tok]]
        par_vals = [r[...] for r in refs[n_tok:n_tok + n_par]]
        ct_refs = refs[n_tok + n_par:n_tok + n_par + n_ct]
        d_refs = refs[n_tok + n_par + n_ct:n_tok + n_par + n_ct + n_d]
        dp_refs = refs[n_tok + n_par + n_ct + n_d:]
        ct_vals, pos = [], 0
        for group in cts:
            acc = None
            for c in group:
                v = ct_refs[pos][...]
                if c.coef != 1.0:
                    v = v * c.coef
                acc = v if acc is None else acc + v
                pos += 1
            ct_vals.append(acc)
        _, vjp = jax.vjp(lambda *a: tuple(fn(*a)), *tok_vals, *par_vals)
        grads = vjp(tuple(ct_vals))
        for r, (i, _, _) in zip(d_refs, d_outs):
            r[...] = grads[i].astype(r.dtype)
        if dp_refs:
            @pl.when(pl.program_id(0) == 0)
            def _():
                for r in dp_refs:
                    r[...] = jnp.zeros_like(r)
            for r, v in zip(dp_refs, grads[n_tok:]):
                r[...] += v

    outs = pl.pallas_call(
        body, name=name, grid=(t // tb,),
        in_specs=[_tok_spec(tb, w) for w in tok] + [_par_spec(p) for p in par] + [_tok_spec(tb, c) for c in flat_cts],
        out_specs=[pl.BlockSpec((tb, w), lambda i: (i, 0)) for _, w, _ in d_outs] + [_par_spec(p) for p in par],
        out_shape=[S((t, w), dt) for _, w, dt in d_outs] + [S(p.shape, F32) for p in par],
        compiler_params=pltpu.CompilerParams(dimension_semantics=("arbitrary",)),
    )(*[w.arr for w in tok], *par, *[c.arr for c in flat_cts])
    return outs[:n_d], outs[n_d:]


def _sigmoid(x):
    return 1.0 / (1.0 + jnp.exp(-x))


def _silu(x):
    return x * _sigmoid(x)


def _ln(x, g, b):
    mu = jnp.mean(x, axis=-1, keepdims=True)
    var = jnp.mean(jnp.square(x - mu), axis=-1, keepdims=True)
    return (x - mu) * lax.rsqrt(var + LN_EPS) * g + b


def f_glu(ua, ug):
    return (ua * _sigmoid(ug),)


def f_ln_silu(a1, g, b):
    return (_silu(_ln(a1, g, b)),)


def f_silu3(a, b, c):
    return (_silu(a), _silu(b), _silu(c))


def f_softplus(dt_raw, bias):
    return (jax.nn.softplus(dt_raw + bias),)


def f_gate_rms(yssd, z, g):
    y = yssd * _silu(z)
    return (y * lax.rsqrt(jnp.mean(jnp.square(y), axis=-1, keepdims=True) + LN_EPS) * g,)


def f_ln(pre, g, b):
    return (_ln(pre, g, b),)


def f_mul(a, b):
    return (a * b,)


def f_gate_mul(pp, gt):
    return (pp * _sigmoid(gt),)


def conv_fwd(name, x, w, b):
    x = _win(x)
    t, c = x.arr.shape[0], x.w
    kw = w.shape[0]
    cb = LANE
    off = x.idx * (c // cb)
    rows = min(CONV_ROWS, t)
    has_b = b is not None

    def body(*refs):
        if has_b:
            x_ref, w_ref, b_ref, y_ref, xp_ref = refs
        else:
            x_ref, w_ref, y_ref, xp_ref = refs
        xp_ref[0:CONV_PAD, :] = jnp.zeros((CONV_PAD, cb), F32)
        xp_ref[CONV_PAD:CONV_PAD + t, :] = x_ref[...]

        def step(s, carry):
            base = pl.multiple_of(s * rows, rows)
            acc = jnp.zeros((rows, cb), F32)
            if has_b:
                acc = acc + b_ref[...]
            for k in range(kw):
                acc = acc + w_ref[k:k + 1, :] * xp_ref[pl.ds(base + CONV_PAD - (kw - 1) + k, rows), :]
            y_ref[pl.ds(base, rows), :] = acc
            return carry

        lax.fori_loop(0, t // rows, step, 0)

    in_specs = [pl.BlockSpec((t, cb), lambda j: (0, off + j)), pl.BlockSpec((kw, cb), lambda j: (0, j))]
    args = [x.arr, w]
    if has_b:
        in_specs.append(pl.BlockSpec((1, cb), lambda j: (0, j)))
        args.append(b)
    return pl.pallas_call(
        body, name=name, grid=(c // cb,), in_specs=in_specs,
        out_specs=pl.BlockSpec((t, cb), lambda j: (0, j)), out_shape=S((t, c), F32),
        scratch_shapes=[pltpu.VMEM((CONV_PAD + t, cb), F32)],
        compiler_params=pltpu.CompilerParams(dimension_semantics=("parallel",)),
    )(*args)


def conv_bwd(name, x, dy, w, dx_dtype=F32):
    x, dy = _win(x), _win(dy)
    t, c = x.arr.shape[0], x.w
    kw = w.shape[0]
    cb = LANE
    xoff = x.idx * (c // cb)
    dyoff = dy.idx * (c // cb)
    rows = min(CONV_ROWS, t)

    def body(x_ref, dy_ref, w_ref, dx_ref, dw_ref, db_ref, xp_ref, dyp_ref):
        xp_ref[0:CONV_PAD, :] = jnp.zeros((CONV_PAD, cb), F32)
        xp_ref[CONV_PAD:CONV_PAD + t, :] = x_ref[...]
        dyp_ref[0:t, :] = dy_ref[...]
        dyp_ref[t:t + CONV_PAD, :] = jnp.zeros((CONV_PAD, cb), F32)

        def fold(v):
            return jnp.sum(v.reshape(rows // SUBLANE, SUBLANE, cb), axis=0)

        def step(s, carry):
            base = pl.multiple_of(s * rows, rows)
            dyc = dy_ref[pl.ds(base, rows), :]
            acc = jnp.zeros((rows, cb), F32)
            new = []
            for k in range(kw):
                acc = acc + w_ref[k:k + 1, :] * dyp_ref[pl.ds(base + (kw - 1) - k, rows), :]
                new.append(carry[k] + fold(dyc * xp_ref[pl.ds(base + CONV_PAD - (kw - 1) + k, rows), :]))
            new.append(carry[kw] + fold(dyc))
            dx_ref[pl.ds(base, rows), :] = acc.astype(dx_ref.dtype)
            return tuple(new)

        init = tuple(jnp.zeros((SUBLANE, cb), F32) for _ in range(kw + 1))
        parts = lax.fori_loop(0, t // rows, step, init)
        for k in range(kw):
            dw_ref[k:k + 1, :] = jnp.sum(parts[k], axis=0, keepdims=True)
        db_ref[...] = jnp.sum(parts[kw], axis=0, keepdims=True)

    return pl.pallas_call(
        body, name=name, grid=(c // cb,),
        in_specs=[pl.BlockSpec((t, cb), lambda j: (0, xoff + j)), pl.BlockSpec((t, cb), lambda j: (0, dyoff + j)),
                  pl.BlockSpec((kw, cb), lambda j: (0, j))],
        out_specs=[pl.BlockSpec((t, cb), lambda j: (0, j)), pl.BlockSpec((kw, cb), lambda j: (0, j)),
                   pl.BlockSpec((1, cb), lambda j: (0, j))],
        out_shape=[S((t, c), dx_dtype), S((kw, c), F32), S((1, c), F32)],
        scratch_shapes=[pltpu.VMEM((CONV_PAD + t, cb), F32), pltpu.VMEM((CONV_PAD + t, cb), F32)],
        compiler_params=pltpu.CompilerParams(dimension_semantics=("parallel",)),
    )(x.arr, dy.arr, w)


def gated_conv_fwd(name, hpre, w, b, out_dtype):
    t, c2 = hpre.shape
    ff = c2 // 2
    kw = w.shape[0]
    cb = LANE
    nb = ff // cb
    rows = min(CONV_ROWS, t)

    def body(h1_ref, h2_ref, w1_ref, w2_ref, b1_ref, b2_ref, y_ref, xp1_ref, xp2_ref):
        for xp_ref, h_ref in ((xp1_ref, h1_ref), (xp2_ref, h2_ref)):
            xp_ref[0:CONV_PAD, :] = jnp.zeros((CONV_PAD, cb), F32)
            xp_ref[CONV_PAD:CONV_PAD + t, :] = h_ref[...]

        def step(s, carry):
            base = pl.multiple_of(s * rows, rows)
            h1 = jnp.zeros((rows, cb), F32) + b1_ref[...]
            h2 = jnp.zeros((rows, cb), F32) + b2_ref[...]
            for k in range(kw):
                at = pl.ds(base + CONV_PAD - (kw - 1) + k, rows)
                h1 = h1 + w1_ref[k:k + 1, :] * xp1_ref[at, :]
                h2 = h2 + w2_ref[k:k + 1, :] * xp2_ref[at, :]
            y_ref[pl.ds(base, rows), :] = (_silu(h1) * h2).astype(y_ref.dtype)
            return carry

        lax.fori_loop(0, t // rows, step, 0)

    col1 = lambda r: pl.BlockSpec((r, cb), lambda j: (0, j))
    col2 = lambda r: pl.BlockSpec((r, cb), lambda j: (0, nb + j))
    return pl.pallas_call(
        body, name=name, grid=(nb,),
        in_specs=[col1(t), col2(t), col1(kw), col2(kw), col1(1), col2(1)],
        out_specs=col1(t), out_shape=S((t, ff), out_dtype),
        scratch_shapes=[pltpu.VMEM((CONV_PAD + t, cb), F32)] * 2,
        compiler_params=pltpu.CompilerParams(dimension_semantics=("parallel",)),
    )(hpre, hpre, w, w, b, b)


def gated_conv_bwd(name, hpre, dact, w, b, dx_dtype):
    t, c2 = hpre.shape
    ff = c2 // 2
    kw = w.shape[0]
    cb = LANE
    nb = ff // cb
    rows = min(CONV_ROWS, t)

    def body(own_ref, oth_ref, da_ref, wo_ref, wt_ref, bo_ref, bt_ref, dx_ref, dw_ref, db_ref,
             xpo_ref, xpt_ref, dhp_ref):
        for xp_ref, h_ref in ((xpo_ref, own_ref), (xpt_ref, oth_ref)):
            xp_ref[0:CONV_PAD, :] = jnp.zeros((CONV_PAD, cb), F32)
            xp_ref[CONV_PAD:CONV_PAD + t, :] = h_ref[...]
        dhp_ref[t:t + CONV_PAD, :] = jnp.zeros((CONV_PAD, cb), F32)

        def fold(v):
            return jnp.sum(v.reshape(rows // SUBLANE, SUBLANE, cb), axis=0)

        def first_pass(own_is_gate):
            def step(s, carry):
                base = pl.multiple_of(s * rows, rows)
                ho = jnp.zeros((rows, cb), F32) + bo_ref[...]
                ht = jnp.zeros((rows, cb), F32) + bt_ref[...]
                for k in range(kw):
                    at = pl.ds(base + CONV_PAD - (kw - 1) + k, rows)
                    ho = ho + wo_ref[k:k + 1, :] * xpo_ref[at, :]
                    ht = ht + wt_ref[k:k + 1, :] * xpt_ref[at, :]
                da = da_ref[pl.ds(base, rows), :]
                if own_is_gate:
                    sg = _sigmoid(ho)
                    dh = da * ht * (sg * (1.0 + ho * (1.0 - sg)))
                else:
                    dh = da * _silu(ht)
                dhp_ref[pl.ds(base, rows), :] = dh
                new = [carry[k] + fold(dh * xpo_ref[pl.ds(base + CONV_PAD - (kw - 1) + k, rows), :]) for k in range(kw)]
                new.append(carry[kw] + fold(dh))
                return tuple(new)

            init = tuple(jnp.zeros((SUBLANE, cb), F32) for _ in range(kw + 1))
            parts = lax.fori_loop(0, t // rows, step, init)
            for k in range(kw):
                dw_ref[k:k + 1, :] = jnp.sum(parts[k], axis=0, keepdims=True)
            db_ref[...] = jnp.sum(parts[kw], axis=0, keepdims=True)

        half = pl.program_id(0)

        @pl.when(half == 0)
        def _():
            first_pass(True)

        @pl.when(half == 1)
        def _():
            first_pass(False)

        def second(s, carry):
            base = pl.multiple_of(s * rows, rows)
            acc = jnp.zeros((rows, cb), F32)
            for k in range(kw):
                acc = acc + wo_ref[k:k + 1, :] * dhp_ref[pl.ds(base + (kw - 1) - k, rows), :]
            dx_ref[pl.ds(base, rows), :] = acc.astype(dx_ref.dtype)
            return carry

        lax.fori_loop(0, t // rows, second, 0)

    own = lambda r: pl.BlockSpec((r, cb), lambda h, j: (0, h * nb + j))
    oth = lambda r: pl.BlockSpec((r, cb), lambda h, j: (0, (1 - h) * nb + j))
    return pl.pallas_call(
        body, name=name, grid=(2, nb),
        in_specs=[own(t), oth(t), pl.BlockSpec((t, cb), lambda h, j: (0, j)), own(kw), oth(kw), own(1), oth(1)],
        out_specs=[own(t), own(kw), own(1)],
        out_shape=[S((t, c2), dx_dtype), S((kw, c2), F32), S((1, c2), F32)],
        scratch_shapes=[pltpu.VMEM((CONV_PAD + t, cb), F32)] * 3,
        compiler_params=pltpu.CompilerParams(dimension_semantics=("parallel", "parallel")),
    )(hpre, hpre, dact, w, w, b, b)


def _bdot(a, b, ca, cb):
    return lax.dot_general(a.astype(MXU_DTYPE), b.astype(MXU_DTYPE), (((ca,), (cb,)), ((0,), (0,))),
                           preferred_element_type=F32)


@jax.custom_vjp
def bmm_nn(a, b):
    return _bdot(a, b, 2, 1)


bmm_nn.defvjp(lambda a, b: (_bdot(a, b, 2, 1), (a, b)),
              lambda r, g: (_bdot(g, r[1], 2, 2), _bdot(r[0], g, 1, 1)))


@jax.custom_vjp
def bmm_tn(a, b):
    return _bdot(a, b, 1, 1)


bmm_tn.defvjp(lambda a, b: (_bdot(a, b, 1, 1), (a, b)),
              lambda r, g: (_bdot(r[1], g, 2, 2), _bdot(r[0], g, 2, 1)))


@jax.custom_vjp
def bmm_nt(a, b):
    return _bdot(a, b, 2, 2)


bmm_nt.defvjp(lambda a, b: (_bdot(a, b, 2, 2), (a, b)),
              lambda r, g: (_bdot(g, r[1], 2, 1), _bdot(g, r[0], 1, 1)))


def ssd_chunk(x, dt, bm, cm, hprev, a_log, dsk):
    hg, ln, _ = x.shape
    n = bm.shape[1]
    ii = lax.broadcasted_iota(jnp.int32, (ln, ln), 0)
    jj = lax.broadcasted_iota(jnp.int32, (ln, ln), 1)
    tril, eye, triu = (ii >= jj)[None], (ii == jj)[None], (ii <= jj)[None]
    da = dt * (-jnp.exp(a_log))
    da_row = jnp.sum(jnp.where(eye, da, 0.0), axis=1, keepdims=True)
    dt_row = jnp.sum(jnp.where(eye, dt, 0.0), axis=1, keepdims=True)
    cum_c = jnp.sum(jnp.where(tril, da_row, 0.0), axis=2, keepdims=True)
    cum_r = jnp.sum(jnp.where(triu, da, 0.0), axis=1, keepdims=True)
    last = jnp.sum(da, axis=1, keepdims=True)
    decay = jnp.where(tril, jnp.exp(jnp.where(tril, cum_c - cum_r, 0.0)), 0.0)
    cb = bmm_nt(cm[None], bm[None])
    y_diag = bmm_nn(cb * decay * dt_row, x)
    bb = jnp.broadcast_to(bm[None], (hg, ln, n))
    cc = jnp.broadcast_to(cm[None], (hg, ln, n))
    states = bmm_tn(x * (jnp.exp(last - cum_c) * dt), bb)
    y_off = bmm_nt(cc, hprev) * jnp.exp(cum_c)
    hnew = hprev * jnp.exp(last) + states
    return y_diag + y_off + dsk * x, hnew


def _ssd_dims(xs, bm, a_log):
    t = xs.shape[0]
    h = a_log.shape[0]
    return h, t, xs.shape[1] // h, h // N_GROUPS, bm.shape[1] // N_GROUPS, t // CHUNK


def _heads_of(ref, g, hg, p):
    return jnp.stack([ref[:, (g * hg + i) * p:(g * hg + i + 1) * p] for i in range(hg)])


def _cols_of(ref, g, hg):
    return jnp.stack([ref[:, g * hg + i:g * hg + i + 1] for i in range(hg)])


def ssd_fwd(name, xs, dt, bm, cm, a_log, dsk):
    h, t, p, hg, n, nc = _ssd_dims(xs, bm, a_log)

    def body(al_ref, dk_ref, x_ref, dt_ref, b_ref, c_ref, y_ref, hp_ref, h_scr):
        @pl.when(pl.program_id(0) == 0)
        def _():
            h_scr[...] = jnp.zeros_like(h_scr)

        for g in range(N_GROUPS):
            hs, ns = slice(g * hg, (g + 1) * hg), slice(g * n, (g + 1) * n)
            hprev = h_scr[hs]
            hp_ref[hs, 0] = hprev
            y, hnew = ssd_chunk(_heads_of(x_ref, g, hg, p), _cols_of(dt_ref, g, hg), b_ref[:, ns], c_ref[:, ns],
                                hprev, al_ref[hs], dk_ref[hs])
            for i in range(hg):
                y_ref[:, (g * hg + i) * p:(g * hg + i + 1) * p] = y[i]
            h_scr[hs] = hnew

    head = pl.BlockSpec((h, 1, 1), lambda c: (0, 0, 0))
    row = lambda w: pl.BlockSpec((CHUNK, w), lambda c: (c, 0))
    return pl.pallas_call(
        body, name=name, grid=(nc,),
        in_specs=[head, head, row(h * p), row(dt.shape[1]), row(N_GROUPS * n), row(N_GROUPS * n)],
        out_specs=[row(h * p), pl.BlockSpec((h, 1, p, n), lambda c: (0, c, 0, 0))],
        out_shape=[S((t, h * p), F32), S((h, nc, p, n), F32)],
        scratch_shapes=[pltpu.VMEM((h, p, n), F32)],
        compiler_params=pltpu.CompilerParams(dimension_semantics=("arbitrary",)),
    )(a_log, dsk, xs, dt, bm, cm)


def ssd_bwd(name, xs, dt, bm, cm, a_log, dsk, hp, dy):
    h, t, p, hg, n, nc = _ssd_dims(xs, bm, a_log)

    def body(al_ref, dk_ref, x_ref, dt_ref, b_ref, c_ref, hp_ref, dy_ref,
             dx_ref, ddt_ref, db_ref, dc_ref, dal_ref, ddk_ref, dh_scr):
        @pl.when(pl.program_id(0) == 0)
        def _():
            dh_scr[...] = jnp.zeros_like(dh_scr)
            dal_ref[...] = jnp.zeros_like(dal_ref)
            ddk_ref[...] = jnp.zeros_like(ddk_ref)

        ddt_ref[...] = jnp.zeros_like(ddt_ref)
        for g in range(N_GROUPS):
            hs, ns = slice(g * hg, (g + 1) * hg), slice(g * n, (g + 1) * n)
            _, vjp = jax.vjp(ssd_chunk, _heads_of(x_ref, g, hg, p), _cols_of(dt_ref, g, hg), b_ref[:, ns],
                             c_ref[:, ns], hp_ref[hs, 0], al_ref[hs], dk_ref[hs])
            gx, gdt, gb, gc, ghp, gal, gdk = vjp((_heads_of(dy_ref, g, hg, p), dh_scr[hs]))
            for i in range(hg):
                dx_ref[:, (g * hg + i) * p:(g * hg + i + 1) * p] = gx[i]
                ddt_ref[:, g * hg + i:g * hg + i + 1] = gdt[i]
            db_ref[:, ns] = gb
            dc_ref[:, ns] = gc
            dh_scr[hs] = ghp
            dal_ref[hs] += gal
            ddk_ref[hs] += gdk

    head = pl.BlockSpec((h, 1, 1), lambda c: (0, 0, 0))
    row = lambda w: pl.BlockSpec((CHUNK, w), lambda c: (nc - 1 - c, 0))
    return pl.pallas_call(
        body, name=name, grid=(nc,),
        in_specs=[head, head, row(h * p), row(dt.shape[1]), row(N_GROUPS * n), row(N_GROUPS * n),
                  pl.BlockSpec((h, 1, p, n), lambda c: (0, nc - 1 - c, 0, 0)), row(h * p)],
        out_specs=[row(h * p), row(dt.shape[1]), row(N_GROUPS * n), row(N_GROUPS * n), head, head],
        out_shape=[S((t, h * p), F32), S(dt.shape, F32), S(bm.shape, F32), S(cm.shape, F32),
                   S((h, 1, 1), F32), S((h, 1, 1), F32)],
        scratch_shapes=[pltpu.VMEM((h, p, n), F32)],
        compiler_params=pltpu.CompilerParams(dimension_semantics=("arbitrary",)),
    )(a_log, dsk, xs, dt, bm, cm, hp, dy)


def _alpha(depth):
    return (2.0 * depth) ** 0.25


def _pad_lanes(v):
    return jnp.pad(v, ((0, 0), (0, LANE - v.shape[1])))


def split_even_weights(w, j):
    d = w["e_w_in"][j].shape[1]
    da = w["e_conv_a_w"].shape[2]
    db = w["e_norm_b_g"].shape[1]
    gn = N_GROUPS * N_STATE
    nh = w["e_dt_bias"].shape[1]
    main = 2 * da + 2 * db + 2 * gn
    win = w["e_w_in"][j]
    ox = 2 * da + db
    cw, cbias = w["e_conv_b_w"][j], w["e_conv_b_b"][j][None]
    return dict(
        d=d, da=da, db=db, gn=gn, nh=nh, main=main,
        win_main=win[:main], win_dt=jnp.pad(win[main:], ((0, LANE - nh), (0, 0))),
        caw=w["e_conv_a_w"][j], cab=w["e_conv_a_b"][j][None], lag=w["e_ln_a_g"][j][None], lab=w["e_ln_a_b"][j][None],
        cw_xs=cw[:, :db], cw_b=cw[:, db:db + gn], cw_c=cw[:, db + gn:],
        cb_xs=cbias[:, :db], cb_b=cbias[:, db:db + gn], cb_c=cbias[:, db + gn:],
        dt_bias=_pad_lanes(w["e_dt_bias"][j][None]), a_log=w["e_a_log"][j].reshape(nh, 1, 1),
        dsk=w["e_d_skip"][j].reshape(nh, 1, 1), norm_g=w["e_norm_b_g"][j][None],
        wout_a=w["e_w_out"][j][:da], wout_b=w["e_w_out"][j][da:],
    )


def even_fwd(tag, x, xm, lw, ln_g, ln_b, alpha):
    t = x.shape[0]
    da, db, gn, nh = lw["da"], lw["db"], lw["gn"], lw["nh"]
    u = mm(tag + "_win", xm, lw["win_main"], "nt")
    udt = mm(tag + "_windt", xm, lw["win_dt"], "nt")
    ua, ug, z, xs_pre = Win(u, da, 0), Win(u, da, 1), Win(u, db, 2 * da // db), Win(u, db, (2 * da + db) // db)
    b_pre, c_pre = Win(u, gn, (2 * da + 2 * db) // gn), Win(u, gn, (2 * da + 2 * db + gn) // gn)
    (a0,) = rowwise(tag + "_glu", f_glu, [ua, ug], [], [da])
    a1 = conv_fwd(tag + "_conva", a0, lw["caw"], lw["cab"])
    (ya,) = rowwise(tag + "_lna", f_ln_silu, [a1], [lw["lag"], lw["lab"]], [da], out_dtypes=[MXU_DTYPE])
    xs_c = conv_fwd(tag + "_convxs", xs_pre, lw["cw_xs"], lw["cb_xs"])
    b_c = conv_fwd(tag + "_convb", b_pre, lw["cw_b"], lw["cb_b"])
    c_c = conv_fwd(tag + "_convc", c_pre, lw["cw_c"], lw["cb_c"])
    xs, bm, cm = rowwise(tag + "_silu3", f_silu3, [xs_c, b_c, c_c], [], [db, gn, gn])
    (dt,) = rowwise(tag + "_dt", f_softplus, [udt], [lw["dt_bias"]], [LANE])
    yssd, hp = ssd_fwd(tag + "_ssd", xs, dt, bm, cm, lw["a_log"], lw["dsk"])
    (yb,) = rowwise(tag + "_gate", f_gate_rms, [yssd, z], [lw["norm_g"]], [db], out_dtypes=[MXU_DTYPE])
    ma = mm(tag + "_wouta", ya, lw["wout_a"], "nn")
    mb = mm(tag + "_woutb", yb, lw["wout_b"], "nn")

    def f_res(xv, mav, mbv, g, b):
        pre = alpha * xv + mav + mbv
        y = _ln(pre, g, b)
        return y, y, pre

    x1, x1m, pre = rowwise(tag + "_res", f_res, [x, ma, mb], [ln_g, ln_b], [x.shape[1]] * 3,
                           out_dtypes=[F32, MXU_DTYPE, F32])
    saved = dict(xm=xm, u=u, udt=udt, a0=a0, a1=a1, ya=ya, xs_c=xs_c, b_c=b_c, c_c=c_c, xs=xs, dt=dt, bm=bm,
                 cm=cm, hp=hp, yssd=yssd, yb=yb, pre=pre)
    return x1, x1m, saved


def even_bwd(tag, dx1_pieces, sv, lw, ln_g, ln_b, alpha):
    t = sv["u"].shape[0]
    da, db, gn, nh = lw["da"], lw["db"], lw["gn"], lw["nh"]
    u, xm = sv["u"], sv["xm"]
    mx = (MXU_DTYPE,)
    ua, ug, z, xs_pre = Win(u, da, 0), Win(u, da, 1), Win(u, db, 2 * da // db), Win(u, db, (2 * da + db) // db)
    b_pre, c_pre = Win(u, gn, (2 * da + 2 * db) // gn), Win(u, gn, (2 * da + 2 * db + gn) // gn)
    (dpre, dprem), (dg0, db0) = rowwise_bwd(tag + "_res_b", f_ln, [sv["pre"]], [ln_g, ln_b], [dx1_pieces],
                                            d_dtypes=[(F32, MXU_DTYPE)])
    dya = mm(tag + "_dya", dprem, lw["wout_a"], "nt")
    dyb = mm(tag + "_dyb", dprem, lw["wout_b"], "nt")
    dwout_a = mm(tag + "_dwouta", sv["ya"], dprem, "tn", EXCHANGE_DTYPE)
    dwout_b = mm(tag + "_dwoutb", sv["yb"], dprem, "tn", EXCHANGE_DTYPE)
    (dyssd, dz), (dnorm_g,) = rowwise_bwd(tag + "_gate_b", f_gate_rms, [sv["yssd"], z], [lw["norm_g"]], [[dyb]],
                                          d_dtypes=[(F32,), mx])
    dxs, ddt, dbm, dcm, dalog, ddsk = ssd_bwd(tag + "_ssd_b", sv["xs"], sv["dt"], sv["bm"], sv["cm"],
                                              lw["a_log"], lw["dsk"], sv["hp"], dyssd)
    (dudt,), (ddt_bias,) = rowwise_bwd(tag + "_dt_b", f_softplus, [sv["udt"]], [lw["dt_bias"]], [[ddt]],
                                       d_dtypes=[mx])
    (dxs_c, db_c, dc_c), _ = rowwise_bwd(tag + "_silu3_b", f_silu3, [sv["xs_c"], sv["b_c"], sv["c_c"]], [],
                                         [[dxs], [dbm], [dcm]])
    dxs_pre, dcw_xs, dcb_xs = conv_bwd(tag + "_convxs_b", xs_pre, dxs_c, lw["cw_xs"], MXU_DTYPE)
    db_pre, dcw_b, dcb_b = conv_bwd(tag + "_convb_b", b_pre, db_c, lw["cw_b"], MXU_DTYPE)
    dc_pre, dcw_c, dcb_c = conv_bwd(tag + "_convc_b", c_pre, dc_c, lw["cw_c"], MXU_DTYPE)
    (da1,), (dlag, dlab) = rowwise_bwd(tag + "_lna_b", f_ln_silu, [sv["a1"]], [lw["lag"], lw["lab"]], [[dya]])
    da0, dcaw, dcab = conv_bwd(tag + "_conva_b", sv["a0"], da1, lw["caw"])
    (dua, dug), _ = rowwise_bwd(tag + "_glu_b", f_glu, [ua, ug], [], [[da0]], d_dtypes=[mx, mx])
    du = jnp.concatenate([dua, dug, dz, dxs_pre, db_pre, dc_pre], axis=1)
    dx_m = mm(tag + "_dxm", du, lw["win_main"], "nn")
    dx_dt = mm(tag + "_dxdt", dudt, lw["win_dt"], "nn")
    dwin_main = mm(tag + "_dwin", du, xm, "tn", EXCHANGE_DTYPE)
    dwin_dt = mm(tag + "_dwindt", dudt, xm, "tn", EXCHANGE_DTYPE)
    grads = dict(
        e_w_in=jnp.concatenate([dwin_main, dwin_dt[:nh]], axis=0),
        e_conv_a_w=dcaw, e_conv_a_b=dcab[0], e_ln_a_g=dlag[0], e_ln_a_b=dlab[0],
        e_conv_b_w=jnp.concatenate([dcw_xs, dcw_b, dcw_c], axis=1),
        e_conv_b_b=jnp.concatenate([dcb_xs, dcb_b, dcb_c], axis=1)[0],
        e_dt_bias=ddt_bias[0, :nh], e_a_log=dalog.reshape(nh), e_d_skip=ddsk.reshape(nh), e_norm_b_g=dnorm_g[0],
        e_w_out=jnp.concatenate([dwout_a, dwout_b], axis=0), ln_g0=dg0[0], ln_b0=db0[0],
    )
    return [Win(dpre, coef=alpha), dx_m, dx_dt], grads


def odd_fwd(tag, x, xm, w, j, ln_g, ln_b, alpha):
    d = x.shape[1]
    u = mm(tag + "_win", xm, w["o_w_in"][j], "nt")
    bg, cg, v = Win(u, d, 0), Win(u, d, 1), Win(u, d, 2)
    (s,) = rowwise(tag + "_cv", f_mul, [cg, v], [], [d])
    cs = conv_fwd(tag + "_conv", s, w["o_conv_w"][j], None)
    (m,) = rowwise(tag + "_bm", f_mul, [bg, cs], [], [d], out_dtypes=[MXU_DTYPE])
    mix = mm(tag + "_wout", m, w["o_w_out"][j], "nn")

    def f_res(xv, mv, g, b):
        pre = alpha * xv + mv
        y = _ln(pre, g, b)
        return y, y, pre

    x1, x1m, pre = rowwise(tag + "_res", f_res, [x, mix], [ln_g, ln_b], [d] * 3, out_dtypes=[F32, MXU_DTYPE, F32])
    return x1, x1m, dict(xm=xm, u=u, s=s, cs=cs, m=m, pre=pre)


def odd_bwd(tag, dx1_pieces, sv, w, j, ln_g, ln_b, alpha):
    xm, u = sv["xm"], sv["u"]
    d = xm.shape[1]
    mx = (MXU_DTYPE,)
    bg, cg, v = Win(u, d, 0), Win(u, d, 1), Win(u, d, 2)
    (dpre, dprem), (dg0, db0) = rowwise_bwd(tag + "_res_b", f_ln, [sv["pre"]], [ln_g, ln_b], [dx1_pieces],
                                            d_dtypes=[(F32, MXU_DTYPE)])
    dm = mm(tag + "_dm", dprem, w["o_w_out"][j], "nt")
    dwout = mm(tag + "_dwout", sv["m"], dprem, "tn", EXCHANGE_DTYPE)
    (dbg, dcs), _ = rowwise_bwd(tag + "_bm_b", f_mul, [bg, sv["cs"]], [], [[dm]], d_dtypes=[mx, (F32,)])
    ds, dcw, _ = conv_bwd(tag + "_conv_b", sv["s"], dcs, w["o_conv_w"][j])
    (dcg, dv), _ = rowwise_bwd(tag + "_cv_b", f_mul, [cg, v], [], [[ds]], d_dtypes=[mx, mx])
    du = jnp.concatenate([dbg, dcg, dv], axis=1)
    dx_u = mm(tag + "_dx", du, w["o_w_in"][j], "nn")
    dwin = mm(tag + "_dwin", du, xm, "tn", EXCHANGE_DTYPE)
    grads = dict(o_w_in=dwin, o_conv_w=dcw, o_w_out=dwout, ln_g0=dg0[0], ln_b0=db0[0])
    return [Win(dpre, coef=alpha), dx_u], grads


def ffn_fwd(tag, x1, x1m, p_i, w, i, ln_g, ln_b, alpha):
    d = x1.shape[1]
    hpre = mm(tag + "_wup", x1m, w["f_w_up"][i], "nt")
    act = gated_conv_fwd(tag + "_fgate", hpre, w["f_conv_w"][i], w["f_conv_b"][i][None], MXU_DTYPE)
    ffn = mm(tag + "_wdown", act, w["f_w_down"][i], "nn")
    pp = mm(tag + "_pproj", p_i, w["ple_w_proj"][i], "nt")
    gt = mm(tag + "_pgate", x1m, w["ple_w_gate"][i], "nn")

    def f_res2(xv, fv, ppv, gtv, g, b):
        pre = alpha * xv + fv + ppv * _sigmoid(gtv)
        y = _ln(pre, g, b)
        return y, y, pre

    x2, x2m, pre = rowwise(tag + "_res2", f_res2, [x1, ffn, pp, gt], [ln_g, ln_b], [d] * 3,
                           out_dtypes=[F32, MXU_DTYPE, F32])
    return x2, x2m, dict(x1m=x1m, hpre=hpre, act=act, pp=pp, gt=gt, pre=pre)


def ffn_bwd(tag, dx2_pieces, sv, p_i, w, i, ln_g, ln_b, alpha):
    x1m = sv["x1m"]
    mx = (MXU_DTYPE,)
    (dpre, dprem), (dg1, db1) = rowwise_bwd(tag + "_res2_b", f_ln, [sv["pre"]], [ln_g, ln_b], [dx2_pieces],
                                            d_dtypes=[(F32, MXU_DTYPE)])
    (dpp, dgt), _ = rowwise_bwd(tag + "_pg_b", f_gate_mul, [sv["pp"], sv["gt"]], [], [[dpre]], d_dtypes=[mx, mx])
    dwproj = mm(tag + "_dwproj", dpp, p_i, "tn", EXCHANGE_DTYPE)
    dwgate = mm(tag + "_dwgate", x1m, dgt, "tn", EXCHANGE_DTYPE)
    dx1_a = mm(tag + "_dx1a", dgt, w["ple_w_gate"][i], "nt")
    dact = mm(tag + "_dact", dprem, w["f_w_down"][i], "nt")
    dwdown = mm(tag + "_dwdown", sv["act"], dprem, "tn", EXCHANGE_DTYPE)
    dhpre, dfcw, dfcb = gated_conv_bwd(tag + "_fgate_b", sv["hpre"], dact, w["f_conv_w"][i], w["f_conv_b"][i][None],
                                       MXU_DTYPE)
    dwup = mm(tag + "_dwup", dhpre, x1m, "tn", EXCHANGE_DTYPE)
    dx1_b = mm(tag + "_dx1b", dhpre, w["f_w_up"][i], "nn")
    grads = dict(f_w_up=dwup, f_conv_w=dfcw, f_conv_b=dfcb[0], f_w_down=dwdown, ple_w_proj=dwproj, ple_w_gate=dwgate,
                 ln_g1=dg1[0], ln_b1=db1[0])
    return [Win(dpre, coef=alpha), dx1_a, dx1_b], grads


def local_step(x, p, w, target, fetch=None):
    depth = w["ln_g"].shape[0]
    alpha = _alpha(depth)
    d = x.shape[1]
    saved = []
    h = hm = x
    for i in range(depth):
        j = i // 2
        if fetch is not None:
            fetch(i, h)
        g0, b0, g1, b1 = w["ln_g"][i, 0][None], w["ln_b"][i, 0][None], w["ln_g"][i, 1][None], w["ln_b"][i, 1][None]
        tag = "l%d" % i
        if i % 2 == 0:
            lw = split_even_weights(w, j)
            h, hm, sv_m = even_fwd(tag, h, hm, lw, g0, b0, alpha)
        else:
            lw = None
            h, hm, sv_m = odd_fwd(tag, h, hm, w, j, g0, b0, alpha)
        h, hm, sv_f = ffn_fwd(tag, h, hm, p[i], w, i, g1, b1, alpha)
        saved.append((lw, sv_m, sv_f, (g0, b0, g1, b1)))

    def f_loss(xf, tg):
        diff = xf - tg
        sq = jnp.sum(jnp.sum(jnp.square(diff), axis=1, keepdims=True), axis=0, keepdims=True)
        return diff * (1.0 / d), jnp.broadcast_to(sq, (1, LANE))

    dxf, sq = rowwise("loss", f_loss, [h, target], [], [d], red_widths=[LANE])
    loss = sq[0, 0] * (0.5 / d)

    per_layer = []
    pieces = [dxf]
    for i in reversed(range(depth)):
        j = i // 2
        lw, sv_m, sv_f, (g0, b0, g1, b1) = saved[i]
        tag = "l%d" % i
        pieces, gf = ffn_bwd(tag, pieces, sv_f, p[i], w, i, g1, b1, alpha)
        if i % 2 == 0:
            pieces, gm = even_bwd(tag, pieces, sv_m, lw, g0, b0, alpha)
        else:
            pieces, gm = odd_bwd(tag, pieces, sv_m, w, j, g0, b0, alpha)
        per_layer.append((i, gm, gf))

    def f_sum(*vs):
        acc = None
        for v, c in zip(vs, [pc.coef for pc in map(_win, pieces)]):
            v = v if c == 1.0 else v * c
            acc = v if acc is None else acc + v
        return (acc,)

    (grad_x,) = rowwise("grad_x", f_sum, [Win(_win(pc).arr) for pc in pieces], [], [d])

    by_layer = {i: (gm, gf) for i, gm, gf in per_layer}
    grads = {}
    n_even, n_odd = (depth + 1) // 2, depth // 2
    collect = lambda name, per_layer: per_layer if name in BIG else jnp.stack(per_layer)
    for name in ("e_w_in", "e_conv_a_w", "e_conv_a_b", "e_ln_a_g", "e_ln_a_b", "e_conv_b_w", "e_conv_b_b", "e_dt_bias",
                 "e_a_log", "e_d_skip", "e_norm_b_g", "e_w_out"):
        grads[name] = collect(name, [by_layer[2 * j][0][name] for j in range(n_even)])
    for name in ("o_w_in", "o_conv_w", "o_w_out"):
        grads[name] = collect(name, [by_layer[2 * j + 1][0][name] for j in range(n_odd)])
    for name in ("f_w_up", "f_conv_w", "f_conv_b", "f_w_down", "ple_w_proj", "ple_w_gate"):
        grads[name] = collect(name, [by_layer[i][1][name] for i in range(depth)])
    grads["ln_g"] = jnp.stack([jnp.stack([by_layer[i][0]["ln_g0"], by_layer[i][1]["ln_g1"]]) for i in range(depth)])
    grads["ln_b"] = jnp.stack([jnp.stack([by_layer[i][0]["ln_b0"], by_layer[i][1]["ln_b1"]]) for i in range(depth)])
    return loss, grad_x, grads


_ANY = pl.BlockSpec(memory_space=pl.ANY)
_MESH = pl.DeviceIdType.MESH


def all_gather(name, xl):
    r, c_ = xl.shape
    split = r // 2 // 16 * 16
    halves = ((0, split), (split, r - split)) if split else ((0, r),)
    two = len(halves) == 2

    def body(x_ref, out_ref, send_sems, recv_sems, local_sem):
        x, y, c = lax.axis_index("x"), lax.axis_index("y"), lax.axis_index("c")
        me, sibling, xn, yn, dg = (x, y, c), (x, y, 1 - c), (1 - x, y, c), (x, 1 - y, c), (1 - x, 1 - y, c)

        def rows(block, h):
            ref = out_ref.at[4 * block[0] + 2 * block[1] + block[2]]
            return ref if h is None else ref.at[pl.ds(*halves[h])]

        def copy(k, block, h, to, own=False):
            src = (x_ref if h is None else x_ref.at[pl.ds(*halves[h])]) if own else rows(block, h)
            return pltpu.make_async_remote_copy(src_ref=src, dst_ref=rows(block, h), send_sem=send_sems.at[k],
                                                recv_sem=recv_sems.at[k], device_id=to, device_id_type=_MESH)

        def other_core(block):
            return (block[0], block[1], 1 - c)

        mine = pltpu.make_async_copy(x_ref, rows(me, None), local_sem)
        mine.start()
        direct = [copy(0, me, 0, xn, own=True), copy(1, me, 1 if two else 0, yn, own=True)]
        if two:
            direct += [copy(2, me, 1, xn, own=True), copy(3, me, 0, yn, own=True)]
        direct.append(copy(6, me, None, sibling, own=True))
        for cp in direct:
            cp.start()
        started = list(direct)

        def then(waits, nxt):
            for cp in waits:
                cp.wait_recv()
            for cp in nxt:
                cp.start()
            started.extend(nxt)

        if two:
            then([copy(0, xn, 0, me)], [copy(4, xn, 0, yn)])
            then([copy(1, yn, 1, me)], [copy(5, yn, 1, xn)])
            then([copy(2, xn, 1, me)], [copy(7, xn, None, sibling)])
            then([copy(3, yn, 0, me)], [copy(8, yn, None, sibling)])
            then([copy(4, dg, 0, me), copy(5, dg, 1, me)], [copy(9, dg, None, sibling)])
        else:
            then([copy(0, xn, 0, me)], [copy(4, xn, 0, yn), copy(7, xn, None, sibling)])
            then([copy(1, yn, 0, me)], [copy(8, yn, None, sibling)])
            then([copy(4, dg, 0, me)], [copy(9, dg, None, sibling)])
        for k, block in ((6, me), (7, xn), (8, yn), (9, dg)):
            copy(k, other_core(block), None, me).wait_recv()
        for cp in started:
            cp.wait_send()
        mine.wait()

    return pl.pallas_call(
        body, name=name, out_shape=S((N_DEV, r, c_), xl.dtype), in_specs=[_ANY], out_specs=_ANY,
        scratch_shapes=[pltpu.SemaphoreType.DMA((10,)), pltpu.SemaphoreType.DMA((10,)), pltpu.SemaphoreType.DMA],
    )(xl)


def _exchange(name, srcs, out_slots, copies_of):
    n = len(srcs)
    n_copies = sum(s.shape[0] for s in srcs) * out_slots

    def body(*refs):
        src_refs, out_refs, send_sems, recv_sems = refs[:n], refs[n:2 * n], refs[2 * n], refs[2 * n + 1]
        pattern = copies_of(lax.axis_index("x"), lax.axis_index("y"), lax.axis_index("c"))
        copies = []
        for s_ref, o_ref in zip(src_refs, out_refs):
            for l in range(s_ref.shape[0]):
                for src_at, slot, target in pattern:
                    k = len(copies)
                    copies.append(pltpu.make_async_remote_copy(
                        src_ref=s_ref.at[(l,) + src_at], dst_ref=o_ref.at[l, slot], send_sem=send_sems.at[k],
                        recv_sem=recv_sems.at[k], device_id=target, device_id_type=_MESH))
        for cp in copies:
            cp.start()
        for cp in copies:
            cp.wait()

    return pl.pallas_call(
        body, name=name, out_shape=[S((s.shape[0], out_slots) + s.shape[-2:], s.dtype) for s in srcs],
        in_specs=[_ANY] * n, out_specs=[_ANY] * n,
        scratch_shapes=[pltpu.SemaphoreType.DMA((n_copies,)), pltpu.SemaphoreType.DMA((n_copies,))],
    )(*srcs)


def exchange_sibling(name, gs):
    return _exchange(name, gs, 4, lambda x, y, c: [((q, 1 - c), q, (x, y, 1 - c)) for q in range(4)])


def exchange_chips(name, ps):
    def pattern(x, y, c):
        chips = [(1 - x, y), (x, 1 - y), (1 - x, 1 - y)]
        return [((2 * cx + cy,), k, (cx, cy, c)) for k, (cx, cy) in enumerate(chips)]
    return _exchange(name, ps, 3, pattern)


def sum_with_sibling(name, g5, recv, core):
    l, _, _, a, b = g5.shape

    def body(core_ref, g_ref, r_ref, o_ref, ox_ref):
        s = g_ref[0, 0, 0].astype(F32) + r_ref[0, 0].astype(F32)
        o_ref[0, 0] = s
        ox_ref[0, 0] = s.astype(ox_ref.dtype)

    slab = pl.BlockSpec((1, 1, a, b), lambda i, q, cr: (i, q, 0, 0))
    return pl.pallas_call(
        body, name=name, out_shape=[S((l, 4, a, b), F32), S((l, 4, a, b), EXCHANGE_DTYPE)],
        grid_spec=pltpu.PrefetchScalarGridSpec(
            num_scalar_prefetch=1, grid=(l, 4),
            in_specs=[pl.BlockSpec((1, 1, 1, a, b), lambda i, q, cr: (i, q, cr[0], 0, 0)), slab],
            out_specs=[slab, slab]),
    )(core, g5, recv)


def sum_with_chips(name, p4, recv, chip):
    l, _, a, b = p4.shape

    def body(chip_ref, p_ref, r_ref, o_ref):
        o_ref[0] = ((p_ref[0, 0] + r_ref[0, 0].astype(F32)) + r_ref[0, 1].astype(F32)) + r_ref[0, 2].astype(F32)

    return pl.pallas_call(
        body, name=name, out_shape=S((l, a, b), F32),
        grid_spec=pltpu.PrefetchScalarGridSpec(
            num_scalar_prefetch=1, grid=(l,),
            in_specs=[pl.BlockSpec((1, 1, a, b), lambda i, ch: (i, ch[0], 0, 0)),
                      pl.BlockSpec((1, 3, a, b), lambda i, ch: (i, 0, 0, 0))],
            out_specs=pl.BlockSpec((1, a, b), lambda i, ch: (i, 0, 0))),
    )(chip, p4, recv)


def sum_devices(name, g8):
    _, r, c_ = g8.shape

    def body(g_ref, o_ref):
        acc = g_ref[0]
        for k in range(1, N_DEV):
            acc = acc + g_ref[k]
        o_ref[...] = acc

    return pl.pallas_call(body, name=name, out_shape=S((r, c_), F32))(g8)


def _flatten(parts, cols, row_mult):
    flat = jnp.concatenate([v.reshape(-1) for v in parts])
    n = flat.shape[0]
    rows = -(-n // (cols * row_mult)) * row_mult
    return jnp.pad(flat, (0, rows * cols - n)).reshape(rows, cols)


def _exchange_dims(name, lshape):
    l, r, c = lshape
    return (l, c, r) if name in TRANSPOSED else (l, r, c)


_HBM = pl.BlockSpec(memory_space=pltpu.HBM)
_SEM = pl.BlockSpec(memory_space=pltpu.SEMAPHORE)
_PEERS = [(dx, dy, dc) for dx in (0, 1) for dy in (0, 1) for dc in (0, 1)][1:]


def _to_all_peers(src_ref, land_ref, send_sems, recv_sems):
    x, y, c = lax.axis_index("x"), lax.axis_index("y"), lax.axis_index("c")
    flip = lambda v, d: 1 - v if d else v
    return [pltpu.make_async_remote_copy(
        src_ref=src_ref, dst_ref=land_ref.at[4 * x + 2 * y + c], send_sem=send_sems.at[k], recv_sem=recv_sems.at[k],
        device_id=(flip(x, dx), flip(y, dy), flip(c, dc)), device_id_type=_MESH) for k, (dx, dy, dc) in enumerate(_PEERS)]


def gather_start(name, src):
    land = lax.empty((N_DEV,) + src.shape, src.dtype)

    def body(src_ref, land_ref, send_sems, recv_sems, src_thru, land_thru, token):
        for cp in _to_all_peers(src_ref, land_ref, send_sems, recv_sems):
            cp.start()
        token[...] = jnp.zeros_like(token)

    return pl.pallas_call(
        body, name=name,
        out_shape=(pltpu.SemaphoreType.DMA((len(_PEERS),)), pltpu.SemaphoreType.DMA((len(_PEERS),)),
                   pltpu.HBM(src.shape, src.dtype), pltpu.HBM(land.shape, land.dtype), S((SUBLANE, LANE), F32)),
        in_specs=(_HBM, _HBM), out_specs=(_SEM, _SEM, _HBM, _HBM, pl.BlockSpec(memory_space=pltpu.VMEM)),
        input_output_aliases={0: 2, 1: 3},
        compiler_params=pltpu.CompilerParams(has_side_effects=pltpu.SideEffectType.DATAFLOW_SIDE_EFFECTING),
    )(pltpu.with_memory_space_constraint(src, pltpu.HBM), pltpu.with_memory_space_constraint(land, pltpu.HBM))


def gather_wait(name, send_sems, recv_sems, src_thru, land_thru, after):
    def body(src_ref, land_ref, send_sems, recv_sems, after_ref, src_dead, got_ref):
        for cp in _to_all_peers(src_ref, land_ref, send_sems, recv_sems):
            cp.wait_send()
            cp.wait_recv()

    return pl.pallas_call(
        body, name=name,
        out_shape=(pltpu.HBM(src_thru.shape, src_thru.dtype), pltpu.HBM(land_thru.shape, land_thru.dtype)),
        in_specs=(_HBM, _HBM, _SEM, _SEM, _ANY), out_specs=(_HBM, _HBM), input_output_aliases={0: 0, 1: 1},
        compiler_params=pltpu.CompilerParams(has_side_effects=pltpu.SideEffectType.DATAFLOW_SIDE_EFFECTING),
    )(src_thru, land_thru, send_sems, recv_sems, after)


def _layer_weights(depth):
    table = []
    for i in range(depth):
        mixer = ("e_w_out", "e_w_in") if i % 2 == 0 else ("o_w_out", "o_w_in")
        table.append([("f_w_up", i), ("f_w_down", i), ("ple_w_gate", i), ("ple_w_proj", i)]
                     + [(n, i // 2) for n in mixer])
    return table


def gather_big(local):
    cols = local["e_w_out"].shape[2]
    depth = local["ln_g"].shape[0]
    table = _layer_weights(depth)
    shard = {}
    for n in BIG:
        v = local[n].astype(MXU_DTYPE)
        shard[n] = (v.transpose(0, 2, 1) if n in TRANSPOSED else v).reshape(v.shape[0], -1, cols)
    bufs = [jnp.concatenate([shard[n][li] for n, li in layer], axis=0) for layer in table]
    full = {n: [None] * local[n].shape[0] for n in BIG}
    dev = 4 * lax.axis_index("x") + 2 * lax.axis_index("y") + lax.axis_index("c")

    def fill(i, got):
        r0 = 0
        for n, li in table[i]:
            _, a, b = _exchange_dims(n, local[n].shape)
            per = shard[n].shape[1]
            full[n][li] = got[:, r0:r0 + per].reshape(N_DEV * a, b)
            r0 += per

    fill(0, all_gather("ag_l0", bufs[0]))
    started = [gather_start("ag_start_l%d" % i, bufs[i]) for i in range(1, depth)]

    def fetch(i, after):
        if i > 0:
            send_sems, recv_sems, src, land, _ = started[i - 1]
            src, got = gather_wait("ag_wait_l%d" % i, send_sems, recv_sems, src, land, after)
            fill(i, lax.dynamic_update_slice(got, src[None], (dev, 0, 0)))

    token = sum(st[4][0, 0] for st in started) if started else 0.0
    return full, fetch, token


def reduce_scatter_big(grads, local_shapes):
    x, y, c = lax.axis_index("x"), lax.axis_index("y"), lax.axis_index("c")
    core, chip = c.astype(jnp.int32).reshape(1), (2 * x + y).astype(jnp.int32).reshape(1)
    g5 = []
    for n in EXCHANGE_ORDER:
        l, a, b = _exchange_dims(n, local_shapes[n])
        g5.append(jnp.stack(grads[n]).reshape(l, 4, 2, a, b))
    from_sibling = exchange_sibling("rs_sibling", g5)
    sums = [sum_with_sibling("rs_sum_sibling_" + n, g, r, core) for n, g, r in zip(EXCHANGE_ORDER, g5, from_sibling)]
    from_chips = exchange_chips("rs_chips", [px for _, px in sums])
    out = {}
    for n, (p4, _), r in zip(EXCHANGE_ORDER, sums, from_chips):
        seg = sum_with_chips("rs_sum_chips_" + n, p4, r, chip)
        out[n] = seg.transpose(0, 2, 1) if n in TRANSPOSED else seg
    return out


def gather_small(name, local, names):
    flat = _flatten([local[n] for n in names], LANE, 1)
    got = all_gather(name, flat).reshape(N_DEV, -1)
    full, off = {}, 0
    for n in names:
        size = math.prod(local[n].shape)
        seg = got[:, off:off + size].reshape((N_DEV,) + local[n].shape)
        full[n] = seg.transpose(1, 2, 0, 3).reshape(seg.shape[1], seg.shape[2], -1)
        off += size
    return full


def all_reduce_small(grads, names):
    flat = _flatten([grads[n] for n in names], LANE, SUBLANE)
    total = sum_devices("ar_sum", all_gather("ar_gather", flat)).reshape(-1)
    out, off = {}, 0
    for nm in names:
        size = math.prod(grads[nm].shape)
        out[nm] = total[off:off + size].reshape(grads[nm].shape)
        off += size
    return out


def adamw(name, w, g, m, v):
    shape = w.shape
    cols = shape[-1]
    rows = math.prod(shape[:-1])
    tr = _pick(rows, (256, 128, 64, 32, 16, 8)) if rows * cols > 256 * 1024 else rows
    c1 = 1.0 - ADAM_B1 ** ADAM_STEP
    c2 = 1.0 - ADAM_B2 ** ADAM_STEP

    def body(w_ref, g_ref, m_ref, v_ref, d_ref, nm_ref, nv_ref):
        gv = g_ref[...]
        m2 = ADAM_B1 * m_ref[...] + (1.0 - ADAM_B1) * gv
        v2 = ADAM_B2 * v_ref[...] + (1.0 - ADAM_B2) * jnp.square(gv)
        d_ref[...] = -ADAM_LR * ((m2 / c1) / (jnp.sqrt(v2 / c2) + ADAM_EPS) + ADAM_WD * w_ref[...])
        nm_ref[...] = m2
        nv_ref[...] = v2

    spec = pl.BlockSpec((tr, cols), lambda i: (i, 0))
    outs = pl.pallas_call(
        body, name=name, grid=(rows // tr,), in_specs=[spec] * 4, out_specs=[spec] * 3,
        out_shape=[S((rows, cols), F32)] * 3,
        compiler_params=pltpu.CompilerParams(dimension_semantics=("parallel",)),
    )(*[a.reshape(rows, cols) for a in (w, g, m, v)])
    return tuple(o.reshape(shape) for o in outs)


def kernel(x, p, e_w_in, e_conv_a_w, e_conv_a_b, e_ln_a_g, e_ln_a_b, e_conv_b_w, e_conv_b_b, e_dt_bias, e_a_log, e_d_skip, e_norm_b_g, e_w_out, o_w_in, o_conv_w, o_w_out, f_w_up, f_conv_w, f_conv_b, f_w_down, ple_w_proj, ple_w_gate, ln_g, ln_b, loss_target, m_e_w_in, m_e_conv_a_w, m_e_conv_a_b, m_e_ln_a_g, m_e_ln_a_b, m_e_conv_b_w, m_e_conv_b_b, m_e_dt_bias, m_e_a_log, m_e_d_skip, m_e_norm_b_g, m_e_w_out, m_o_w_in, m_o_conv_w, m_o_w_out, m_f_w_up, m_f_conv_w, m_f_conv_b, m_f_w_down, m_ple_w_proj, m_ple_w_gate, m_ln_g, m_ln_b, v_e_w_in, v_e_conv_a_w, v_e_conv_a_b, v_e_ln_a_g, v_e_ln_a_b, v_e_conv_b_w, v_e_conv_b_b, v_e_dt_bias, v_e_a_log, v_e_d_skip, v_e_norm_b_g, v_e_w_out, v_o_w_in, v_o_conv_w, v_o_w_out, v_f_w_up, v_f_conv_w, v_f_conv_b, v_f_w_down, v_ple_w_proj, v_ple_w_gate, v_ln_g, v_ln_b):
    args = locals()
    local = {n: args[n] for n in WEIGHTS}
    mom = {n: args["m_" + n] for n in WEIGHTS}
    var = {n: args["v_" + n] for n in WEIGHTS}

    full = {n: local[n] for n in REPLICATED}
    big, fetch, token = gather_big(local)
    full.update(big)
    full.update(gather_small("ag_small", local, SMALL_SHARDED))

    loss_local, grad_x, grads = local_step(x[0] + token, p[:, 0], full, loss_target[0], fetch)
    loss = lax.psum(loss_local, MESH_AXES)

    g_local = reduce_scatter_big(grads, {n: local[n].shape for n in BIG})
    small = all_reduce_small(grads, REPLICATED + SMALL_SHARDED)
    dev = 4 * lax.axis_index("x") + 2 * lax.axis_index("y") + lax.axis_index("c")
    for n in REPLICATED:
        g_local[n] = small[n]
    for n in SMALL_SHARDED:
        width = local[n].shape[2]
        g_local[n] = lax.dynamic_slice_in_dim(small[n], dev * width, width, axis=2)

    delta, new_m, new_v = {}, {}, {}
    for n in WEIGHTS:
        delta[n], new_m[n], new_v[n] = adamw("adamw_" + n, local[n], g_local[n], mom[n], var[n])
    return (loss, grad_x[None], *[g_local[n] for n in WEIGHTS], *[delta[n] for n in WEIGHTS],
            *[new_m[n] for n in WEIGHTS], *[new_v[n] for n in WEIGHTS])
```

```python
import functools
import math

import jax
import jax.numpy as jnp
from jax import lax
from jax.experimental import pallas as pl
from jax.experimental.pallas import tpu as pltpu

F32 = jnp.float32
MXU_DTYPE = jnp.bfloat16
MESH_AXES = ("x", "y", "c")
N_DEV = 8
LANE = 128
SUBLANE = 8
ROWWISE_VMEM_BUDGET = 20 * 1024 * 1024
MM_TILES = (1408, 1024, 512, 256, 128)
EXCHANGE_DTYPE = jnp.bfloat16
LN_EPS = 1e-5
CHUNK = 64
HEAD_DIM = 64
N_GROUPS = 4
N_STATE = 128
CONV_PAD = 32
CONV_ROWS = 256
ADAM_LR, ADAM_B1, ADAM_B2, ADAM_EPS, ADAM_WD, ADAM_STEP = 0.001, 0.9, 0.999, 1e-08, 0.01, 10

BIG = ("e_w_in", "e_w_out", "o_w_in", "o_w_out", "f_w_up", "f_w_down", "ple_w_proj", "ple_w_gate")
SMALL_SHARDED = ("e_conv_a_w", "e_conv_b_w", "o_conv_w", "f_conv_w", "ln_g", "ln_b")
REPLICATED = ("e_conv_a_b", "e_ln_a_g", "e_ln_a_b", "e_conv_b_b", "e_dt_bias", "e_a_log", "e_d_skip",
              "e_norm_b_g", "f_conv_b")
TRANSPOSED = ("e_w_in", "o_w_in", "f_w_up", "ple_w_proj")
WEIGHTS = ("e_w_in", "e_conv_a_w", "e_conv_a_b", "e_ln_a_g", "e_ln_a_b", "e_conv_b_w", "e_conv_b_b", "e_dt_bias",
           "e_a_log", "e_d_skip", "e_norm_b_g", "e_w_out", "o_w_in", "o_conv_w", "o_w_out", "f_w_up", "f_conv_w",
           "f_conv_b", "f_w_down", "ple_w_proj", "ple_w_gate", "ln_g", "ln_b")

S = jax.ShapeDtypeStruct


class Win:
    def __init__(self, arr, w=None, idx=0, coef=1.0):
        self.arr, self.w, self.idx, self.coef = arr, (arr.shape[1] if w is None else w), idx, coef


def _win(a):
    return a if isinstance(a, Win) else Win(a)


def _pick(n, prefs):
    for p in prefs:
        if p <= n and n % p == 0:
            return p
    return n


_MM_DIMS = {"nn": (1, 0), "nt": (1, 1), "tn": (0, 0)}


def mm(name, a, b, mode, out_dtype=F32):
    ca, cb = _MM_DIMS[mode]
    kdim = a.shape[ca]
    m = a.shape[1 - ca]
    n = b.shape[1 - cb]
    assert b.shape[cb] == kdim, (name, a.shape, b.shape, mode)
    tm = _pick(m, MM_TILES)
    tn = _pick(n, MM_TILES)
    tk = kdim if kdim <= MM_TILES[0] else _pick(kdim, MM_TILES)
    nk = kdim // tk
    own_acc = nk > 1 and out_dtype != F32

    def body(a_ref, b_ref, o_ref, *scratch):
        acc_ref = scratch[0] if own_acc else o_ref
        d = lax.dot_general(a_ref[...].astype(MXU_DTYPE), b_ref[...].astype(MXU_DTYPE),
                            (((ca,), (cb,)), ((), ())), preferred_element_type=F32)
        if nk == 1:
            o_ref[...] = d.astype(o_ref.dtype)
        else:
            k = pl.program_id(2)

            @pl.when(k == 0)
            def _():
                acc_ref[...] = d

            @pl.when(k > 0)
            def _():
                acc_ref[...] += d

            if own_acc:
                @pl.when(k == nk - 1)
                def _():
                    o_ref[...] = acc_ref[...].astype(o_ref.dtype)

    a_spec = pl.BlockSpec((tm, tk), lambda i, j, k: (i, k)) if ca == 1 else pl.BlockSpec((tk, tm), lambda i, j, k: (k, i))
    b_spec = pl.BlockSpec((tk, tn), lambda i, j, k: (k, j)) if cb == 0 else pl.BlockSpec((tn, tk), lambda i, j, k: (j, k))
    return pl.pallas_call(
        body, name=name, grid=(m // tm, n // tn, nk),
        in_specs=[a_spec, b_spec], out_specs=pl.BlockSpec((tm, tn), lambda i, j, k: (i, j)),
        out_shape=S((m, n), out_dtype), scratch_shapes=[pltpu.VMEM((tm, tn), F32)] if own_acc else [],
        compiler_params=pltpu.CompilerParams(dimension_semantics=("parallel", "parallel", "arbitrary")),
    )(a, b)


def _row_block(t, widths):
    tb = 512
    while tb > SUBLANE and (t % tb or tb * sum(widths) * 8 > ROWWISE_VMEM_BUDGET):
        tb //= 2
    return tb


def _tok_spec(tb, w):
    return pl.BlockSpec((tb, w.w), functools.partial(lambda i, idx: (i, idx), idx=w.idx))


def _par_spec(p):
    return pl.BlockSpec((1, p.shape[1]), lambda i: (0, 0))


def rowwise(name, fn, tok, par, out_widths, red_widths=(), out_dtypes=None):
    tok = [_win(t) for t in tok]
    t = tok[0].arr.shape[0]
    tb = _row_block(t, [w.w for w in tok] + list(out_widths))
    n_tok, n_par, n_out = len(tok), len(par), len(out_widths)
    out_dtypes = [F32] * n_out if out_dtypes is None else out_dtypes

    def body(*refs):
        ins = [r[...] for r in refs[:n_tok + n_par]]
        res = fn(*ins)
        out_refs = refs[n_tok + n_par:n_tok + n_par + n_out]
        red_refs = refs[n_tok + n_par + n_out:]
        for r, v in zip(out_refs, res[:n_out]):
            r[...] = v.astype(r.dtype)
        if red_refs:
            @pl.when(pl.program_id(0) == 0)
            def _():
                for r in red_refs:
                    r[...] = jnp.zeros_like(r)
            for r, v in zip(red_refs, res[n_out:]):
                r[...] += v

    outs = pl.pallas_call(
        body, name=name, grid=(t // tb,),
        in_specs=[_tok_spec(tb, w) for w in tok] + [_par_spec(p) for p in par],
        out_specs=[pl.BlockSpec((tb, w), lambda i: (i, 0)) for w in out_widths]
        + [pl.BlockSpec((1, w), lambda i: (0, 0)) for w in red_widths],
        out_shape=[S((t, w), dt) for w, dt in zip(out_widths, out_dtypes)] + [S((1, w), F32) for w in red_widths],
        compiler_params=pltpu.CompilerParams(dimension_semantics=("arbitrary",)),
    )(*[w.arr for w in tok], *par)
    return outs


def rowwise_bwd(name, fn, tok, par, cts, d_dtypes=None):
    tok = [_win(t) for t in tok]
    cts = [[_win(c) for c in group] for group in cts]
    d_dtypes = [(F32,)] * len(tok) if d_dtypes is None else d_dtypes
    t = tok[0].arr.shape[0]
    flat_cts = [c for group in cts for c in group]
    d_outs = [(i, w.w, dt) for i, (w, dts) in enumerate(zip(tok, d_dtypes)) for dt in dts]
    tb = _row_block(t, [w.w for w in tok] + [c.w for c in flat_cts] + [w for _, w, _ in d_outs])
    n_tok, n_par, n_ct, n_d = len(tok), len(par), len(flat_cts), len(d_outs)

    def body(*refs):
        tok_vals = [r[...] for r in refs[:n_tok]]
        par_vals = [r[...] for r in refs[n_tok:n_tok + n_par]]
        ct_refs = refs[n_tok + n_par:n_tok + n_par + n_ct]
        d_refs = refs[n_tok + n_par + n_ct:n_tok + n_par + n_ct + n_d]
        dp_refs = refs[n_tok + n_par + n_ct + n_d:]
        ct_vals, pos = [], 0
        for group in cts:
            acc = None
            for c in group:
                v = ct_refs[pos][...]
                if c.coef != 1.0:
                    v = v * c.coef
                acc = v if acc is None else acc + v
                pos += 1
            ct_vals.append(acc)
        _, vjp = jax.vjp(lambda *a: tuple(fn(*a)), *tok_vals, *par_vals)
        grads = vjp(tuple(ct_vals))
        for r, (i, _, _) in zip(d_refs, d_outs):
            r[...] = grads[i].astype(r.dtype)
        if dp_refs:
            @pl.when(pl.program_id(0) == 0)
            def _():
                for r in dp_refs:
                    r[...] = jnp.zeros_like(r)
            for r, v in zip(dp_refs, grads[n_tok:]):
                r[...] += v

    outs = pl.pallas_call(
        body, name=name, grid=(t // tb,),
        in_specs=[_tok_spec(tb, w) for w in tok] + [_par_spec(p) for p in par] + [_tok_spec(tb, c) for c in flat_cts],
        out_specs=[pl.BlockSpec((tb, w), lambda i: (i, 0)) for _, w, _ in d_outs] + [_par_spec(p) for p in par],
        out_shape=[S((t, w), dt) for _, w, dt in d_outs] + [S(p.shape, F32) for p in par],
        compiler_params=pltpu.CompilerParams(dimension_semantics=("arbitrary",)),
    )(*[w.arr for w in tok], *par, *[c.arr for c in flat_cts])
    return outs[:n_d], outs[n_d:]


def _sigmoid(x):
    return 1.0 / (1.0 + jnp.exp(-x))


def _silu(x):
    return x * _sigmoid(x)


def _ln(x, g, b):
    mu = jnp.mean(x, axis=-1, keepdims=True)
    var = jnp.mean(jnp.square(x - mu), axis=-1, keepdims=True)
    return (x - mu) * lax.rsqrt(var + LN_EPS) * g + b


def f_glu(ua, ug):
    return (ua * _sigmoid(ug),)


def f_ln_silu(a1, g, b):
    return (_silu(_ln(a1, g, b)),)


def f_silu3(a, b, c):
    return (_silu(a), _silu(b), _silu(c))


def f_softplus(dt_raw, bias):
    return (jax.nn.softplus(dt_raw + bias),)


def f_gate_rms(yssd, z, g):
    y = yssd * _silu(z)
    return (y * lax.rsqrt(jnp.mean(jnp.square(y), axis=-1, keepdims=True) + LN_EPS) * g,)


def f_ln(pre, g, b):
    return (_ln(pre, g, b),)


def f_mul(a, b):
    return (a * b,)


def f_gate_mul(pp, gt):
    return (pp * _sigmoid(gt),)


def conv_fwd(name, x, w, b):
    x = _win(x)
    t, c = x.arr.shape[0], x.w
    kw = w.shape[0]
    cb = LANE
    off = x.idx * (c // cb)
    rows = min(CONV_ROWS, t)
    has_b = b is not None

    def body(*refs):
        if has_b:
            x_ref, w_ref, b_ref, y_ref, xp_ref = refs
        else:
            x_ref, w_ref, y_ref, xp_ref = refs
        xp_ref[0:CONV_PAD, :] = jnp.zeros((CONV_PAD, cb), F32)
        xp_ref[CONV_PAD:CONV_PAD + t, :] = x_ref[...]

        def step(s, carry):
            base = pl.multiple_of(s * rows, rows)
            acc = jnp.zeros((rows, cb), F32)
            if has_b:
                acc = acc + b_ref[...]
            for k in range(kw):
                acc = acc + w_ref[k:k + 1, :] * xp_ref[pl.ds(base + CONV_PAD - (kw - 1) + k, rows), :]
            y_ref[pl.ds(base, rows), :] = acc
            return carry

        lax.fori_loop(0, t // rows, step, 0)

    in_specs = [pl.BlockSpec((t, cb), lambda j: (0, off + j)), pl.BlockSpec((kw, cb), lambda j: (0, j))]
    args = [x.arr, w]
    if has_b:
        in_specs.append(pl.BlockSpec((1, cb), lambda j: (0, j)))
        args.append(b)
    return pl.pallas_call(
        body, name=name, grid=(c // cb,), in_specs=in_specs,
        out_specs=pl.BlockSpec((t, cb), lambda j: (0, j)), out_shape=S((t, c), F32),
        scratch_shapes=[pltpu.VMEM((CONV_PAD + t, cb), F32)],
        compiler_params=pltpu.CompilerParams(dimension_semantics=("parallel",)),
    )(*args)


def conv_bwd(name, x, dy, w, dx_dtype=F32):
    x, dy = _win(x), _win(dy)
    t, c = x.arr.shape[0], x.w
    kw = w.shape[0]
    cb = LANE
    xoff = x.idx * (c // cb)
    dyoff = dy.idx * (c // cb)
    rows = min(CONV_ROWS, t)

    def body(x_ref, dy_ref, w_ref, dx_ref, dw_ref, db_ref, xp_ref, dyp_ref):
        xp_ref[0:CONV_PAD, :] = jnp.zeros((CONV_PAD, cb), F32)
        xp_ref[CONV_PAD:CONV_PAD + t, :] = x_ref[...]
        dyp_ref[0:t, :] = dy_ref[...]
        dyp_ref[t:t + CONV_PAD, :] = jnp.zeros((CONV_PAD, cb), F32)

        def fold(v):
            return jnp.sum(v.reshape(rows // SUBLANE, SUBLANE, cb), axis=0)

        def step(s, carry):
            base = pl.multiple_of(s * rows, rows)
            dyc = dy_ref[pl.ds(base, rows), :]
            acc = jnp.zeros((rows, cb), F32)
            new = []
            for k in range(kw):
                acc = acc + w_ref[k:k + 1, :] * dyp_ref[pl.ds(base + (kw - 1) - k, rows), :]
                new.append(carry[k] + fold(dyc * xp_ref[pl.ds(base + CONV_PAD - (kw - 1) + k, rows), :]))
            new.append(carry[kw] + fold(dyc))
            dx_ref[pl.ds(base, rows), :] = acc.astype(dx_ref.dtype)
            return tuple(new)

        init = tuple(jnp.zeros((SUBLANE, cb), F32) for _ in range(kw + 1))
        parts = lax.fori_loop(0, t // rows, step, init)
        for k in range(kw):
            dw_ref[k:k + 1, :] = jnp.sum(parts[k], axis=0, keepdims=True)
        db_ref[...] = jnp.sum(parts[kw], axis=0, keepdims=True)

    return pl.pallas_call(
        body, name=name, grid=(c // cb,),
        in_specs=[pl.BlockSpec((t, cb), lambda j: (0, xoff + j)), pl.BlockSpec((t, cb), lambda j: (0, dyoff + j)),
                  pl.BlockSpec((kw, cb), lambda j: (0, j))],
        out_specs=[pl.BlockSpec((t, cb), lambda j: (0, j)), pl.BlockSpec((kw, cb), lambda j: (0, j)),
                   pl.BlockSpec((1, cb), lambda j: (0, j))],
        out_shape=[S((t, c), dx_dtype), S((kw, c), F32), S((1, c), F32)],
        scratch_shapes=[pltpu.VMEM((CONV_PAD + t, cb), F32), pltpu.VMEM((CONV_PAD + t, cb), F32)],
        compiler_params=pltpu.CompilerParams(dimension_semantics=("parallel",)),
    )(x.arr, dy.arr, w)


def gated_conv_fwd(name, hpre, w, b, out_dtype):
    t, c2 = hpre.shape
    ff = c2 // 2
    kw = w.shape[0]
    cb = LANE
    nb = ff // cb
    rows = min(CONV_ROWS, t)

    def body(h1_ref, h2_ref, w1_ref, w2_ref, b1_ref, b2_ref, y_ref, xp1_ref, xp2_ref):
        for xp_ref, h_ref in ((xp1_ref, h1_ref), (xp2_ref, h2_ref)):
            xp_ref[0:CONV_PAD, :] = jnp.zeros((CONV_PAD, cb), F32)
            xp_ref[CONV_PAD:CONV_PAD + t, :] = h_ref[...]

        def step(s, carry):
            base = pl.multiple_of(s * rows, rows)
            h1 = jnp.zeros((rows, cb), F32) + b1_ref[...]
            h2 = jnp.zeros((rows, cb), F32) + b2_ref[...]
            for k in range(kw):
                at = pl.ds(base + CONV_PAD - (kw - 1) + k, rows)
                h1 = h1 + w1_ref[k:k + 1, :] * xp1_ref[at, :]
                h2 = h2 + w2_ref[k:k + 1, :] * xp2_ref[at, :]
            y_ref[pl.ds(base, rows), :] = (_silu(h1) * h2).astype(y_ref.dtype)
            return carry

        lax.fori_loop(0, t // rows, step, 0)

    col1 = lambda r: pl.BlockSpec((r, cb), lambda j: (0, j))
    col2 = lambda r: pl.BlockSpec((r, cb), lambda j: (0, nb + j))
    return pl.pallas_call(
        body, name=name, grid=(nb,),
        in_specs=[col1(t), col2(t), col1(kw), col2(kw), col1(1), col2(1)],
        out_specs=col1(t), out_shape=S((t, ff), out_dtype),
        scratch_shapes=[pltpu.VMEM((CONV_PAD + t, cb), F32)] * 2,
        compiler_params=pltpu.CompilerParams(dimension_semantics=("parallel",)),
    )(hpre, hpre, w, w, b, b)


def gated_conv_bwd(name, hpre, dact, w, b, dx_dtype):
    t, c2 = hpre.shape
    ff = c2 // 2
    kw = w.shape[0]
    cb = LANE
    nb = ff // cb
    rows = min(CONV_ROWS, t)

    def body(own_ref, oth_ref, da_ref, wo_ref, wt_ref, bo_ref, bt_ref, dx_ref, dw_ref, db_ref,
             xpo_ref, xpt_ref, dhp_ref):
        for xp_ref, h_ref in ((xpo_ref, own_ref), (xpt_ref, oth_ref)):
            xp_ref[0:CONV_PAD, :] = jnp.zeros((CONV_PAD, cb), F32)
            xp_ref[CONV_PAD:CONV_PAD + t, :] = h_ref[...]
        dhp_ref[t:t + CONV_PAD, :] = jnp.zeros((CONV_PAD, cb), F32)

        def fold(v):
            return jnp.sum(v.reshape(rows // SUBLANE, SUBLANE, cb), axis=0)

        def first_pass(own_is_gate):
            def step(s, carry):
                base = pl.multiple_of(s * rows, rows)
                ho = jnp.zeros((rows, cb), F32) + bo_ref[...]
                ht = jnp.zeros((rows, cb), F32) + bt_ref[...]
                for k in range(kw):
                    at = pl.ds(base + CONV_PAD - (kw - 1) + k, rows)
                    ho = ho + wo_ref[k:k + 1, :] * xpo_ref[at, :]
                    ht = ht + wt_ref[k:k + 1, :] * xpt_ref[at, :]
                da = da_ref[pl.ds(base, rows), :]
                if own_is_gate:
                    sg = _sigmoid(ho)
                    dh = da * ht * (sg * (1.0 + ho * (1.0 - sg)))
                else:
                    dh = da * _silu(ht)
                dhp_ref[pl.ds(base, rows), :] = dh
                new = [carry[k] + fold(dh * xpo_ref[pl.ds(base + CONV_PAD - (kw - 1) + k, rows), :]) for k in range(kw)]
                new.append(carry[kw] + fold(dh))
                return tuple(new)

            init = tuple(jnp.zeros((SUBLANE, cb), F32) for _ in range(kw + 1))
            parts = lax.fori_loop(0, t // rows, step, init)
            for k in range(kw):
                dw_ref[k:k + 1, :] = jnp.sum(parts[k], axis=0, keepdims=True)
            db_ref[...] = jnp.sum(parts[kw], axis=0, keepdims=True)

        half = pl.program_id(0)

        @pl.when(half == 0)
        def _():
            first_pass(True)

        @pl.when(half == 1)
        def _():
            first_pass(False)

        def second(s, carry):
            base = pl.multiple_of(s * rows, rows)
            acc = jnp.zeros((rows, cb), F32)
            for k in range(kw):
                acc = acc + wo_ref[k:k + 1, :] * dhp_ref[pl.ds(base + (kw - 1) - k, rows), :]
            dx_ref[pl.ds(base, rows), :] = acc.astype(dx_ref.dtype)
            return carry

        lax.fori_loop(0, t // rows, second, 0)

    own = lambda r: pl.BlockSpec((r, cb), lambda h, j: (0, h * nb + j))
    oth = lambda r: pl.BlockSpec((r, cb), lambda h, j: (0, (1 - h) * nb + j))
    return pl.pallas_call(
        body, name=name, grid=(2, nb),
        in_specs=[own(t), oth(t), pl.BlockSpec((t, cb), lambda h, j: (0, j)), own(kw), oth(kw), own(1), oth(1)],
        out_specs=[own(t), own(kw), own(1)],
        out_shape=[S((t, c2), dx_dtype), S((kw, c2), F32), S((1, c2), F32)],
        scratch_shapes=[pltpu.VMEM((CONV_PAD + t, cb), F32)] * 3,
        compiler_params=pltpu.CompilerParams(dimension_semantics=("parallel", "parallel")),
    )(hpre, hpre, dact, w, w, b, b)


def _bdot(a, b, ca, cb):
    return lax.dot_general(a.astype(MXU_DTYPE), b.astype(MXU_DTYPE), (((ca,), (cb,)), ((0,), (0,))),
                           preferred_element_type=F32)


@jax.custom_vjp
def bmm_nn(a, b):
    return _bdot(a, b, 2, 1)


bmm_nn.defvjp(lambda a, b: (_bdot(a, b, 2, 1), (a, b)),
              lambda r, g: (_bdot(g, r[1], 2, 2), _bdot(r[0], g, 1, 1)))


@jax.custom_vjp
def bmm_tn(a, b):
    return _bdot(a, b, 1, 1)


bmm_tn.defvjp(lambda a, b: (_bdot(a, b, 1, 1), (a, b)),
              lambda r, g: (_bdot(r[1], g, 2, 2), _bdot(r[0], g, 2, 1)))


@jax.custom_vjp
def bmm_nt(a, b):
    return _bdot(a, b, 2, 2)


bmm_nt.defvjp(lambda a, b: (_bdot(a, b, 2, 2), (a, b)),
              lambda r, g: (_bdot(g, r[1], 2, 1), _bdot(g, r[0], 1, 1)))


def ssd_chunk(x, dt, bm, cm, hprev, a_log, dsk):
    hg, ln, _ = x.shape
    n = bm.shape[1]
    ii = lax.broadcasted_iota(jnp.int32, (ln, ln), 0)
    jj = lax.broadcasted_iota(jnp.int32, (ln, ln), 1)
    tril, eye, triu = (ii >= jj)[None], (ii == jj)[None], (ii <= jj)[None]
    da = dt * (-jnp.exp(a_log))
    da_row = jnp.sum(jnp.where(eye, da, 0.0), axis=1, keepdims=True)
    dt_row = jnp.sum(jnp.where(eye, dt, 0.0), axis=1, keepdims=True)
    cum_c = jnp.sum(jnp.where(tril, da_row, 0.0), axis=2, keepdims=True)
    cum_r = jnp.sum(jnp.where(triu, da, 0.0), axis=1, keepdims=True)
    last = jnp.sum(da, axis=1, keepdims=True)
    decay = jnp.where(tril, jnp.exp(jnp.where(tril, cum_c - cum_r, 0.0)), 0.0)
    cb = bmm_nt(cm[None], bm[None])
    y_diag = bmm_nn(cb * decay * dt_row, x)
    bb = jnp.broadcast_to(bm[None], (hg, ln, n))
    cc = jnp.broadcast_to(cm[None], (hg, ln, n))
    states = bmm_tn(x * (jnp.exp(last - cum_c) * dt), bb)
    y_off = bmm_nt(cc, hprev) * jnp.exp(cum_c)
    hnew = hprev * jnp.exp(last) + states
    return y_diag + y_off + dsk * x, hnew


def _ssd_dims(xs, bm, a_log):
    t = xs.shape[0]
    h = a_log.shape[0]
    return h, t, xs.shape[1] // h, h // N_GROUPS, bm.shape[1] // N_GROUPS, t // CHUNK


def _heads_of(ref, g, hg, p):
    return jnp.stack([ref[:, (g * hg + i) * p:(g * hg + i + 1) * p] for i in range(hg)])


def _cols_of(ref, g, hg):
    return jnp.stack([ref[:, g * hg + i:g * hg + i + 1] for i in range(hg)])


def ssd_fwd(name, xs, dt, bm, cm, a_log, dsk):
    h, t, p, hg, n, nc = _ssd_dims(xs, bm, a_log)

    def body(al_ref, dk_ref, x_ref, dt_ref, b_ref, c_ref, y_ref, hp_ref, h_scr):
        @pl.when(pl.program_id(0) == 0)
        def _():
            h_scr[...] = jnp.zeros_like(h_scr)

        for g in range(N_GROUPS):
            hs, ns = slice(g * hg, (g + 1) * hg), slice(g * n, (g + 1) * n)
            hprev = h_scr[hs]
            hp_ref[hs, 0] = hprev
            y, hnew = ssd_chunk(_heads_of(x_ref, g, hg, p), _cols_of(dt_ref, g, hg), b_ref[:, ns], c_ref[:, ns],
                                hprev, al_ref[hs], dk_ref[hs])
            for i in range(hg):
                y_ref[:, (g * hg + i) * p:(g * hg + i + 1) * p] = y[i]
            h_scr[hs] = hnew

    head = pl.BlockSpec((h, 1, 1), lambda c: (0, 0, 0))
    row = lambda w: pl.BlockSpec((CHUNK, w), lambda c: (c, 0))
    return pl.pallas_call(
        body, name=name, grid=(nc,),
        in_specs=[head, head, row(h * p), row(dt.shape[1]), row(N_GROUPS * n), row(N_GROUPS * n)],
        out_specs=[row(h * p), pl.BlockSpec((h, 1, p, n), lambda c: (0, c, 0, 0))],
        out_shape=[S((t, h * p), F32), S((h, nc, p, n), F32)],
        scratch_shapes=[pltpu.VMEM((h, p, n), F32)],
        compiler_params=pltpu.CompilerParams(dimension_semantics=("arbitrary",)),
    )(a_log, dsk, xs, dt, bm, cm)


def ssd_bwd(name, xs, dt, bm, cm, a_log, dsk, hp, dy):
    h, t, p, hg, n, nc = _ssd_dims(xs, bm, a_log)

    def body(al_ref, dk_ref, x_ref, dt_ref, b_ref, c_ref, hp_ref, dy_ref,
             dx_ref, ddt_ref, db_ref, dc_ref, dal_ref, ddk_ref, dh_scr):
        @pl.when(pl.program_id(0) == 0)
        def _():
            dh_scr[...] = jnp.zeros_like(dh_scr)
            dal_ref[...] = jnp.zeros_like(dal_ref)
            ddk_ref[...] = jnp.zeros_like(ddk_ref)

        ddt_ref[...] = jnp.zeros_like(ddt_ref)
        for g in range(N_GROUPS):
            hs, ns = slice(g * hg, (g + 1) * hg), slice(g * n, (g + 1) * n)
            _, vjp = jax.vjp(ssd_chunk, _heads_of(x_ref, g, hg, p), _cols_of(dt_ref, g, hg), b_ref[:, ns],
                             c_ref[:, ns], hp_ref[hs, 0], al_ref[hs], dk_ref[hs])
            gx, gdt, gb, gc, ghp, gal, gdk = vjp((_heads_of(dy_ref, g, hg, p), dh_scr[hs]))
            for i in range(hg):
                dx_ref[:, (g * hg + i) * p:(g * hg + i + 1) * p] = gx[i]
                ddt_ref[:, g * hg + i:g * hg + i + 1] = gdt[i]
            db_ref[:, ns] = gb
            dc_ref[:, ns] = gc
            dh_scr[hs] = ghp
            dal_ref[hs] += gal
            ddk_ref[hs] += gdk

    head = pl.BlockSpec((h, 1, 1), lambda c: (0, 0, 0))
    row = lambda w: pl.BlockSpec((CHUNK, w), lambda c: (nc - 1 - c, 0))
    return pl.pallas_call(
        body, name=name, grid=(nc,),
        in_specs=[head, head, row(h * p), row(dt.shape[1]), row(N_GROUPS * n), row(N_GROUPS * n),
                  pl.BlockSpec((h, 1, p, n), lambda c: (0, nc - 1 - c, 0, 0)), row(h * p)],
        out_specs=[row(h * p), row(dt.shape[1]), row(N_GROUPS * n), row(N_GROUPS * n), head, head],
        out_shape=[S((t, h * p), F32), S(dt.shape, F32), S(bm.shape, F32), S(cm.shape, F32),
                   S((h, 1, 1), F32), S((h, 1, 1), F32)],
        scratch_shapes=[pltpu.VMEM((h, p, n), F32)],
        compiler_params=pltpu.CompilerParams(dimension_semantics=("arbitrary",)),
    )(a_log, dsk, xs, dt, bm, cm, hp, dy)


def _alpha(depth):
    return (2.0 * depth) ** 0.25


def _pad_lanes(v):
    return jnp.pad(v, ((0, 0), (0, LANE - v.shape[1])))


def split_even_weights(w, j):
    d = w["e_w_in"][j].shape[1]
    da = w["e_conv_a_w"].shape[2]
    db = w["e_norm_b_g"].shape[1]
    gn = N_GROUPS * N_STATE
    nh = w["e_dt_bias"].shape[1]
    main = 2 * da + 2 * db + 2 * gn
    win = w["e_w_in"][j]
    ox = 2 * da + db
    cw, cbias = w["e_conv_b_w"][j], w["e_conv_b_b"][j][None]
    return dict(
        d=d, da=da, db=db, gn=gn, nh=nh, main=main,
        win_main=win[:main], win_dt=jnp.pad(win[main:], ((0, LANE - nh), (0, 0))),
        caw=w["e_conv_a_w"][j], cab=w["e_conv_a_b"][j][None], lag=w["e_ln_a_g"][j][None], lab=w["e_ln_a_b"][j][None],
        cw_xs=cw[:, :db], cw_b=cw[:, db:db + gn], cw_c=cw[:, db + gn:],
        cb_xs=cbias[:, :db], cb_b=cbias[:, db:db + gn], cb_c=cbias[:, db + gn:],
        dt_bias=_pad_lanes(w["e_dt_bias"][j][None]), a_log=w["e_a_log"][j].reshape(nh, 1, 1),
        dsk=w["e_d_skip"][j].reshape(nh, 1, 1), norm_g=w["e_norm_b_g"][j][None],
        wout_a=w["e_w_out"][j][:da], wout_b=w["e_w_out"][j][da:],
    )


def even_fwd(tag, x, xm, lw, ln_g, ln_b, alpha):
    t = x.shape[0]
    da, db, gn, nh = lw["da"], lw["db"], lw["gn"], lw["nh"]
    u = mm(tag + "_win", xm, lw["win_main"], "nt")
    udt = mm(tag + "_windt", xm, lw["win_dt"], "nt")
    ua, ug, z, xs_pre = Win(u, da, 0), Win(u, da, 1), Win(u, db, 2 * da // db), Win(u, db, (2 * da + db) // db)
    b_pre, c_pre = Win(u, gn, (2 * da + 2 * db) // gn), Win(u, gn, (2 * da + 2 * db + gn) // gn)
    (a0,) = rowwise(tag + "_glu", f_glu, [ua, ug], [], [da])
    a1 = conv_fwd(tag + "_conva", a0, lw["caw"], lw["cab"])
    (ya,) = rowwise(tag + "_lna", f_ln_silu, [a1], [lw["lag"], lw["lab"]], [da], out_dtypes=[MXU_DTYPE])
    xs_c = conv_fwd(tag + "_convxs", xs_pre, lw["cw_xs"], lw["cb_xs"])
    b_c = conv_fwd(tag + "_convb", b_pre, lw["cw_b"], lw["cb_b"])
    c_c = conv_fwd(tag + "_convc", c_pre, lw["cw_c"], lw["cb_c"])
    xs, bm, cm = rowwise(tag + "_silu3", f_silu3, [xs_c, b_c, c_c], [], [db, gn, gn])
    (dt,) = rowwise(tag + "_dt", f_softplus, [udt], [lw["dt_bias"]], [LANE])
    yssd, hp = ssd_fwd(tag + "_ssd", xs, dt, bm, cm, lw["a_log"], lw["dsk"])
    (yb,) = rowwise(tag + "_gate", f_gate_rms, [yssd, z], [lw["norm_g"]], [db], out_dtypes=[MXU_DTYPE])
    ma = mm(tag + "_wouta", ya, lw["wout_a"], "nn")
    mb = mm(tag + "_woutb", yb, lw["wout_b"], "nn")

    def f_res(xv, mav, mbv, g, b):
        pre = alpha * xv + mav + mbv
        y = _ln(pre, g, b)
        return y, y, pre

    x1, x1m, pre = rowwise(tag + "_res", f_res, [x, ma, mb], [ln_g, ln_b], [x.shape[1]] * 3,
                           out_dtypes=[F32, MXU_DTYPE, F32])
    saved = dict(xm=xm, u=u, udt=udt, a0=a0, a1=a1, ya=ya, xs_c=xs_c, b_c=b_c, c_c=c_c, xs=xs, dt=dt, bm=bm,
                 cm=cm, hp=hp, yssd=yssd, yb=yb, pre=pre)
    return x1, x1m, saved


def even_bwd(tag, dx1_pieces, sv, lw, ln_g, ln_b, alpha):
    t = sv["u"].shape[0]
    da, db, gn, nh = lw["da"], lw["db"], lw["gn"], lw["nh"]
    u, xm = sv["u"], sv["xm"]
    mx = (MXU_DTYPE,)
    ua, ug, z, xs_pre = Win(u, da, 0), Win(u, da, 1), Win(u, db, 2 * da // db), Win(u, db, (2 * da + db) // db)
    b_pre, c_pre = Win(u, gn, (2 * da + 2 * db) // gn), Win(u, gn, (2 * da + 2 * db + gn) // gn)
    (dpre, dprem), (dg0, db0) = rowwise_bwd(tag + "_res_b", f_ln, [sv["pre"]], [ln_g, ln_b], [dx1_pieces],
                                            d_dtypes=[(F32, MXU_DTYPE)])
    dya = mm(tag + "_dya", dprem, lw["wout_a"], "nt")
    dyb = mm(tag + "_dyb", dprem, lw["wout_b"], "nt")
    dwout_a = mm(tag + "_dwouta", sv["ya"], dprem, "tn", EXCHANGE_DTYPE)
    dwout_b = mm(tag + "_dwoutb", sv["yb"], dprem, "tn", EXCHANGE_DTYPE)
    (dyssd, dz), (dnorm_g,) = rowwise_bwd(tag + "_gate_b", f_gate_rms, [sv["yssd"], z], [lw["norm_g"]], [[dyb]],
                                          d_dtypes=[(F32,), mx])
    dxs, ddt, dbm, dcm, dalog, ddsk = ssd_bwd(tag + "_ssd_b", sv["xs"], sv["dt"], sv["bm"], sv["cm"],
                                              lw["a_log"], lw["dsk"], sv["hp"], dyssd)
    (dudt,), (ddt_bias,) = rowwise_bwd(tag + "_dt_b", f_softplus, [sv["udt"]], [lw["dt_bias"]], [[ddt]],
                                       d_dtypes=[mx])
    (dxs_c, db_c, dc_c), _ = rowwise_bwd(tag + "_silu3_b", f_silu3, [sv["xs_c"], sv["b_c"], sv["c_c"]], [],
                                         [[dxs], [dbm], [dcm]])
    dxs_pre, dcw_xs, dcb_xs = conv_bwd(tag + "_convxs_b", xs_pre, dxs_c, lw["cw_xs"], MXU_DTYPE)
    db_pre, dcw_b, dcb_b = conv_bwd(tag + "_convb_b", b_pre, db_c, lw["cw_b"], MXU_DTYPE)
    dc_pre, dcw_c, dcb_c = conv_bwd(tag + "_convc_b", c_pre, dc_c, lw["cw_c"], MXU_DTYPE)
    (da1,), (dlag, dlab) = rowwise_bwd(tag + "_lna_b", f_ln_silu, [sv["a1"]], [lw["lag"], lw["lab"]], [[dya]])
    da0, dcaw, dcab = conv_bwd(tag + "_conva_b", sv["a0"], da1, lw["caw"])
    (dua, dug), _ = rowwise_bwd(tag + "_glu_b", f_glu, [ua, ug], [], [[da0]], d_dtypes=[mx, mx])
    du = jnp.concatenate([dua, dug, dz, dxs_pre, db_pre, dc_pre], axis=1)
    dx_m = mm(tag + "_dxm", du, lw["win_main"], "nn")
    dx_dt = mm(tag + "_dxdt", dudt, lw["win_dt"], "nn")
    dwin_main = mm(tag + "_dwin", du, xm, "tn", EXCHANGE_DTYPE)
    dwin_dt = mm(tag + "_dwindt", dudt, xm, "tn", EXCHANGE_DTYPE)
    grads = dict(
        e_w_in=jnp.concatenate([dwin_main, dwin_dt[:nh]], axis=0),
        e_conv_a_w=dcaw, e_conv_a_b=dcab[0], e_ln_a_g=dlag[0], e_ln_a_b=dlab[0],
        e_conv_b_w=jnp.concatenate([dcw_xs, dcw_b, dcw_c], axis=1),
        e_conv_b_b=jnp.concatenate([dcb_xs, dcb_b, dcb_c], axis=1)[0],
        e_dt_bias=ddt_bias[0, :nh], e_a_log=dalog.reshape(nh), e_d_skip=ddsk.reshape(nh), e_norm_b_g=dnorm_g[0],
        e_w_out=jnp.concatenate([dwout_a, dwout_b], axis=0), ln_g0=dg0[0], ln_b0=db0[0],
    )
    return [Win(dpre, coef=alpha), dx_m, dx_dt], grads


def odd_fwd(tag, x, xm, w, j, ln_g, ln_b, alpha):
    d = x.shape[1]
    u = mm(tag + "_win", xm, w["o_w_in"][j], "nt")
    bg, cg, v = Win(u, d, 0), Win(u, d, 1), Win(u, d, 2)
    (s,) = rowwise(tag + "_cv", f_mul, [cg, v], [], [d])
    cs = conv_fwd(tag + "_conv", s, w["o_conv_w"][j], None)
    (m,) = rowwise(tag + "_bm", f_mul, [bg, cs], [], [d], out_dtypes=[MXU_DTYPE])
    mix = mm(tag + "_wout", m, w["o_w_out"][j], "nn")

    def f_res(xv, mv, g, b):
        pre = alpha * xv + mv
        y = _ln(pre, g, b)
        return y, y, pre

    x1, x1m, pre = rowwise(tag + "_res", f_res, [x, mix], [ln_g, ln_b], [d] * 3, out_dtypes=[F32, MXU_DTYPE, F32])
    return x1, x1m, dict(xm=xm, u=u, s=s, cs=cs, m=m, pre=pre)


def odd_bwd(tag, dx1_pieces, sv, w, j, ln_g, ln_b, alpha):
    xm, u = sv["xm"], sv["u"]
    d = xm.shape[1]
    mx = (MXU_DTYPE,)
    bg, cg, v = Win(u, d, 0), Win(u, d, 1), Win(u, d, 2)
    (dpre, dprem), (dg0, db0) = rowwise_bwd(tag + "_res_b", f_ln, [sv["pre"]], [ln_g, ln_b], [dx1_pieces],
                                            d_dtypes=[(F32, MXU_DTYPE)])
    dm = mm(tag + "_dm", dprem, w["o_w_out"][j], "nt")
    dwout = mm(tag + "_dwout", sv["m"], dprem, "tn", EXCHANGE_DTYPE)
    (dbg, dcs), _ = rowwise_bwd(tag + "_bm_b", f_mul, [bg, sv["cs"]], [], [[dm]], d_dtypes=[mx, (F32,)])
    ds, dcw, _ = conv_bwd(tag + "_conv_b", sv["s"], dcs, w["o_conv_w"][j])
    (dcg, dv), _ = rowwise_bwd(tag + "_cv_b", f_mul, [cg, v], [], [[ds]], d_dtypes=[mx, mx])
    du = jnp.concatenate([dbg, dcg, dv], axis=1)
    dx_u = mm(tag + "_dx", du, w["o_w_in"][j], "nn")
    dwin = mm(tag + "_dwin", du, xm, "tn", EXCHANGE_DTYPE)
    grads = dict(o_w_in=dwin, o_conv_w=dcw, o_w_out=dwout, ln_g0=dg0[0], ln_b0=db0[0])
    return [Win(dpre, coef=alpha), dx_u], grads


def ffn_fwd(tag, x1, x1m, p_i, w, i, ln_g, ln_b, alpha):
    d = x1.shape[1]
    hpre = mm(tag + "_wup", x1m, w["f_w_up"][i], "nt")
    act = gated_conv_fwd(tag + "_fgate", hpre, w["f_conv_w"][i], w["f_conv_b"][i][None], MXU_DTYPE)
    ffn = mm(tag + "_wdown", act, w["f_w_down"][i], "nn")
    pp = mm(tag + "_pproj", p_i, w["ple_w_proj"][i], "nt")
    gt = mm(tag + "_pgate", x1m, w["ple_w_gate"][i], "nn")

    def f_res2(xv, fv, ppv, gtv, g, b):
        pre = alpha * xv + fv + ppv * _sigmoid(gtv)
        y = _ln(pre, g, b)
        return y, y, pre

    x2, x2m, pre = rowwise(tag + "_res2", f_res2, [x1, ffn, pp, gt], [ln_g, ln_b], [d] * 3,
                           out_dtypes=[F32, MXU_DTYPE, F32])
    return x2, x2m, dict(x1m=x1m, hpre=hpre, act=act, pp=pp, gt=gt, pre=pre)


def ffn_bwd(tag, dx2_pieces, sv, p_i, w, i, ln_g, ln_b, alpha):
    x1m = sv["x1m"]
    mx = (MXU_DTYPE,)
    (dpre, dprem), (dg1, db1) = rowwise_bwd(tag + "_res2_b", f_ln, [sv["pre"]], [ln_g, ln_b], [dx2_pieces],
                                            d_dtypes=[(F32, MXU_DTYPE)])
    (dpp, dgt), _ = rowwise_bwd(tag + "_pg_b", f_gate_mul, [sv["pp"], sv["gt"]], [], [[dpre]], d_dtypes=[mx, mx])
    dwproj = mm(tag + "_dwproj", dpp, p_i, "tn", EXCHANGE_DTYPE)
    dwgate = mm(tag + "_dwgate", x1m, dgt, "tn", EXCHANGE_DTYPE)
    dx1_a = mm(tag + "_dx1a", dgt, w["ple_w_gate"][i], "nt")
    dact = mm(tag + "_dact", dprem, w["f_w_down"][i], "nt")
    dwdown = mm(tag + "_dwdown", sv["act"], dprem, "tn", EXCHANGE_DTYPE)
    dhpre, dfcw, dfcb = gated_conv_bwd(tag + "_fgate_b", sv["hpre"], dact, w["f_conv_w"][i], w["f_conv_b"][i][None],
                                       MXU_DTYPE)
    dwup = mm(tag + "_dwup", dhpre, x1m, "tn", EXCHANGE_DTYPE)
    dx1_b = mm(tag + "_dx1b", dhpre, w["f_w_up"][i], "nn")
    grads = dict(f_w_up=dwup, f_conv_w=dfcw, f_conv_b=dfcb[0], f_w_down=dwdown, ple_w_proj=dwproj, ple_w_gate=dwgate,
                 ln_g1=dg1[0], ln_b1=db1[0])
    return [Win(dpre, coef=alpha), dx1_a, dx1_b], grads


def local_step(x, p, w, target, fetch=None, emit=None):
    depth = w["ln_g"].shape[0]
    alpha = _alpha(depth)
    d = x.shape[1]
    saved = []
    h = hm = x
    for i in range(depth):
        j = i // 2
        if fetch is not None:
            fetch(i, h)
        g0, b0, g1, b1 = w["ln_g"][i, 0][None], w["ln_b"][i, 0][None], w["ln_g"][i, 1][None], w["ln_b"][i, 1][None]
        tag = "l%d" % i
        if i % 2 == 0:
            lw = split_even_weights(w, j)
            h, hm, sv_m = even_fwd(tag, h, hm, lw, g0, b0, alpha)
        else:
            lw = None
            h, hm, sv_m = odd_fwd(tag, h, hm, w, j, g0, b0, alpha)
        h, hm, sv_f = ffn_fwd(tag, h, hm, p[i], w, i, g1, b1, alpha)
        saved.append((lw, sv_m, sv_f, (g0, b0, g1, b1)))

    def f_loss(xf, tg):
        diff = xf - tg
        sq = jnp.sum(jnp.sum(jnp.square(diff), axis=1, keepdims=True), axis=0, keepdims=True)
        return diff * (1.0 / d), jnp.broadcast_to(sq, (1, LANE))

    dxf, sq = rowwise("loss", f_loss, [h, target], [], [d], red_widths=[LANE])
    loss = sq[0, 0] * (0.5 / d)

    per_layer = []
    pieces = [dxf]
    token = None
    for i in reversed(range(depth)):
        j = i // 2
        lw, sv_m, sv_f, (g0, b0, g1, b1) = saved[i]
        tag = "l%d" % i
        pieces, gf = ffn_bwd(tag, pieces, sv_f, p[i], w, i, g1 if token is None else g1 + token, b1, alpha)
        if emit is not None:
            token = emit(tag + "f", i, gf)
        g0 = g0 if token is None else g0 + token
        if i % 2 == 0:
            pieces, gm = even_bwd(tag, pieces, sv_m, lw, g0, b0, alpha)
        else:
            pieces, gm = odd_bwd(tag, pieces, sv_m, w, j, g0, b0, alpha)
        if emit is not None:
            token = emit(tag + "m", j, gm)
        per_layer.append((i, gm, gf))

    def f_sum(*vs):
        acc = None
        for v, c in zip(vs, [pc.coef for pc in map(_win, pieces)]):
            v = v if c == 1.0 else v * c
            acc = v if acc is None else acc + v
        return (acc,)

    (grad_x,) = rowwise("grad_x", f_sum, [Win(_win(pc).arr) for pc in pieces], [], [d])

    by_layer = {i: (gm, gf) for i, gm, gf in per_layer}
    grads = {}
    n_even, n_odd = (depth + 1) // 2, depth // 2
    collect = lambda name, per_layer: per_layer if name in BIG else jnp.stack(per_layer)
    for name in ("e_w_in", "e_conv_a_w", "e_conv_a_b", "e_ln_a_g", "e_ln_a_b", "e_conv_b_w", "e_conv_b_b", "e_dt_bias",
                 "e_a_log", "e_d_skip", "e_norm_b_g", "e_w_out"):
        grads[name] = collect(name, [by_layer[2 * j][0][name] for j in range(n_even)])
    for name in ("o_w_in", "o_conv_w", "o_w_out"):
        grads[name] = collect(name, [by_layer[2 * j + 1][0][name] for j in range(n_odd)])
    for name in ("f_w_up", "f_conv_w", "f_conv_b", "f_w_down", "ple_w_proj", "ple_w_gate"):
        grads[name] = collect(name, [by_layer[i][1][name] for i in range(depth)])
    grads["ln_g"] = jnp.stack([jnp.stack([by_layer[i][0]["ln_g0"], by_layer[i][1]["ln_g1"]]) for i in range(depth)])
    grads["ln_b"] = jnp.stack([jnp.stack([by_layer[i][0]["ln_b0"], by_layer[i][1]["ln_b1"]]) for i in range(depth)])
    return loss, grad_x, grads


_ANY = pl.BlockSpec(memory_space=pl.ANY)
_MESH = pl.DeviceIdType.MESH


def all_gather(name, xl):
    r, c_ = xl.shape
    split = r // 2 // 16 * 16
    halves = ((0, split), (split, r - split)) if split else ((0, r),)
    two = len(halves) == 2

    def body(x_ref, out_ref, send_sems, recv_sems, local_sem):
        x, y, c = lax.axis_index("x"), lax.axis_index("y"), lax.axis_index("c")
        me, sibling, xn, yn, dg = (x, y, c), (x, y, 1 - c), (1 - x, y, c), (x, 1 - y, c), (1 - x, 1 - y, c)

        def rows(block, h):
            ref = out_ref.at[4 * block[0] + 2 * block[1] + block[2]]
            return ref if h is None else ref.at[pl.ds(*halves[h])]

        def copy(k, block, h, to, own=False):
            src = (x_ref if h is None else x_ref.at[pl.ds(*halves[h])]) if own else rows(block, h)
            return pltpu.make_async_remote_copy(src_ref=src, dst_ref=rows(block, h), send_sem=send_sems.at[k],
                                                recv_sem=recv_sems.at[k], device_id=to, device_id_type=_MESH)

        def other_core(block):
            return (block[0], block[1], 1 - c)

        mine = pltpu.make_async_copy(x_ref, rows(me, None), local_sem)
        mine.start()
        direct = [copy(0, me, 0, xn, own=True), copy(1, me, 1 if two else 0, yn, own=True)]
        if two:
            direct += [copy(2, me, 1, xn, own=True), copy(3, me, 0, yn, own=True)]
        direct.append(copy(6, me, None, sibling, own=True))
        for cp in direct:
            cp.start()
        started = list(direct)

        def then(waits, nxt):
            for cp in waits:
                cp.wait_recv()
            for cp in nxt:
                cp.start()
            started.extend(nxt)

        if two:
            then([copy(0, xn, 0, me)], [copy(4, xn, 0, yn)])
            then([copy(1, yn, 1, me)], [copy(5, yn, 1, xn)])
            then([copy(2, xn, 1, me)], [copy(7, xn, None, sibling)])
            then([copy(3, yn, 0, me)], [copy(8, yn, None, sibling)])
            then([copy(4, dg, 0, me), copy(5, dg, 1, me)], [copy(9, dg, None, sibling)])
        else:
            then([copy(0, xn, 0, me)], [copy(4, xn, 0, yn), copy(7, xn, None, sibling)])
            then([copy(1, yn, 0, me)], [copy(8, yn, None, sibling)])
            then([copy(4, dg, 0, me)], [copy(9, dg, None, sibling)])
        for k, block in ((6, me), (7, xn), (8, yn), (9, dg)):
            copy(k, other_core(block), None, me).wait_recv()
        for cp in started:
            cp.wait_send()
        mine.wait()

    return pl.pallas_call(
        body, name=name, out_shape=S((N_DEV, r, c_), xl.dtype), in_specs=[_ANY], out_specs=_ANY,
        scratch_shapes=[pltpu.SemaphoreType.DMA((10,)), pltpu.SemaphoreType.DMA((10,)), pltpu.SemaphoreType.DMA],
    )(xl)


def sum_devices(name, g8):
    _, r, c_ = g8.shape

    def body(g_ref, o_ref):
        acc = g_ref[0]
        for k in range(1, N_DEV):
            acc = acc + g_ref[k]
        o_ref[...] = acc

    return pl.pallas_call(body, name=name, out_shape=S((r, c_), F32))(g8)


def _flatten(parts, cols, row_mult):
    flat = jnp.concatenate([v.reshape(-1) for v in parts])
    n = flat.shape[0]
    rows = -(-n // (cols * row_mult)) * row_mult
    return jnp.pad(flat, (0, rows * cols - n)).reshape(rows, cols)


def _exchange_dims(name, lshape):
    l, r, c = lshape
    return (l, c, r) if name in TRANSPOSED else (l, r, c)


_HBM = pl.BlockSpec(memory_space=pltpu.HBM)
_SEM = pl.BlockSpec(memory_space=pltpu.SEMAPHORE)
_PEERS = [(dx, dy, dc) for dx in (0, 1) for dy in (0, 1) for dc in (0, 1)][1:]


def _to_all_peers(src_ref, land_ref, send_sems, recv_sems):
    x, y, c = lax.axis_index("x"), lax.axis_index("y"), lax.axis_index("c")
    flip = lambda v, d: 1 - v if d else v
    return [pltpu.make_async_remote_copy(
        src_ref=src_ref, dst_ref=land_ref.at[4 * x + 2 * y + c], send_sem=send_sems.at[k], recv_sem=recv_sems.at[k],
        device_id=(flip(x, dx), flip(y, dy), flip(c, dc)), device_id_type=_MESH) for k, (dx, dy, dc) in enumerate(_PEERS)]


def gather_start(name, src):
    land = lax.empty((N_DEV,) + src.shape, src.dtype)

    def body(src_ref, land_ref, send_sems, recv_sems, src_thru, land_thru, token):
        for cp in _to_all_peers(src_ref, land_ref, send_sems, recv_sems):
            cp.start()
        token[...] = jnp.zeros_like(token)

    return pl.pallas_call(
        body, name=name,
        out_shape=(pltpu.SemaphoreType.DMA((len(_PEERS),)), pltpu.SemaphoreType.DMA((len(_PEERS),)),
                   pltpu.HBM(src.shape, src.dtype), pltpu.HBM(land.shape, land.dtype), S((SUBLANE, LANE), F32)),
        in_specs=(_HBM, _HBM), out_specs=(_SEM, _SEM, _HBM, _HBM, pl.BlockSpec(memory_space=pltpu.VMEM)),
        input_output_aliases={0: 2, 1: 3},
        compiler_params=pltpu.CompilerParams(has_side_effects=pltpu.SideEffectType.DATAFLOW_SIDE_EFFECTING),
    )(pltpu.with_memory_space_constraint(src, pltpu.HBM), pltpu.with_memory_space_constraint(land, pltpu.HBM))


def gather_wait(name, send_sems, recv_sems, src_thru, land_thru, after):
    def body(src_ref, land_ref, send_sems, recv_sems, after_ref, src_dead, got_ref):
        for cp in _to_all_peers(src_ref, land_ref, send_sems, recv_sems):
            cp.wait_send()
            cp.wait_recv()

    return pl.pallas_call(
        body, name=name,
        out_shape=(pltpu.HBM(src_thru.shape, src_thru.dtype), pltpu.HBM(land_thru.shape, land_thru.dtype)),
        in_specs=(_HBM, _HBM, _SEM, _SEM, _ANY), out_specs=(_HBM, _HBM), input_output_aliases={0: 0, 1: 1},
        compiler_params=pltpu.CompilerParams(has_side_effects=pltpu.SideEffectType.DATAFLOW_SIDE_EFFECTING),
    )(src_thru, land_thru, send_sems, recv_sems, after)


def _layer_weights(depth):
    table = []
    for i in range(depth):
        mixer = ("e_w_out", "e_w_in") if i % 2 == 0 else ("o_w_out", "o_w_in")
        table.append([("f_w_up", i), ("f_w_down", i), ("ple_w_gate", i), ("ple_w_proj", i)]
                     + [(n, i // 2) for n in mixer])
    return table


def gather_big(local):
    cols = local["e_w_out"].shape[2]
    depth = local["ln_g"].shape[0]
    table = _layer_weights(depth)
    shard = {}
    for n in BIG:
        v = local[n].astype(MXU_DTYPE)
        shard[n] = (v.transpose(0, 2, 1) if n in TRANSPOSED else v).reshape(v.shape[0], -1, cols)
    bufs = [jnp.concatenate([shard[n][li] for n, li in layer], axis=0) for layer in table]
    full = {n: [None] * local[n].shape[0] for n in BIG}
    dev = 4 * lax.axis_index("x") + 2 * lax.axis_index("y") + lax.axis_index("c")

    def fill(i, got):
        r0 = 0
        for n, li in table[i]:
            _, a, b = _exchange_dims(n, local[n].shape)
            per = shard[n].shape[1]
            full[n][li] = got[:, r0:r0 + per].reshape(N_DEV * a, b)
            r0 += per

    fill(0, all_gather("ag_l0", bufs[0]))
    started = [gather_start("ag_start_l%d" % i, bufs[i]) for i in range(1, depth)]

    def fetch(i, after):
        if i > 0:
            send_sems, recv_sems, src, land, _ = started[i - 1]
            src, got = gather_wait("ag_wait_l%d" % i, send_sems, recv_sems, src, land, after)
            fill(i, lax.dynamic_update_slice(got, src[None], (dev, 0, 0)))

    token = sum(st[4][0, 0] for st in started) if started else 0.0
    return full, fetch, token


def _to_owners(g_refs, land_refs, send_sems, recv_sems):
    x, y, c = lax.axis_index("x"), lax.axis_index("y"), lax.axis_index("c")
    flip = lambda v, d: 1 - v if d else v
    copies = []
    for g_ref, land_ref in zip(g_refs, land_refs):
        for dx, dy, dc in _PEERS:
            px, py, pc = flip(x, dx), flip(y, dy), flip(c, dc)
            k = len(copies)
            copies.append(pltpu.make_async_remote_copy(
                src_ref=g_ref.at[4 * px + 2 * py + pc], dst_ref=land_ref.at[4 * x + 2 * y + c],
                send_sem=send_sems.at[k], recv_sem=recv_sems.at[k], device_id=(px, py, pc), device_id_type=_MESH))
    return copies


def scatter_start(name, gs):
    n = len(gs)
    lands = [lax.empty(g.shape, g.dtype) for g in gs]
    n_copies = n * len(_PEERS)

    def body(*refs):
        send_sems, recv_sems, token = refs[2 * n], refs[2 * n + 1], refs[-1]
        for cp in _to_owners(refs[:n], refs[n:2 * n], send_sems, recv_sems):
            cp.start()
        token[...] = jnp.zeros_like(token)

    hbm = lambda v: pltpu.with_memory_space_constraint(v, pltpu.HBM)
    return pl.pallas_call(
        body, name=name,
        out_shape=(pltpu.SemaphoreType.DMA((n_copies,)), pltpu.SemaphoreType.DMA((n_copies,)),
                   *[pltpu.HBM(v.shape, v.dtype) for v in gs + lands], S((SUBLANE, LANE), F32)),
        in_specs=(_HBM,) * (2 * n), out_specs=(_SEM, _SEM) + (_HBM,) * (2 * n) + (pl.BlockSpec(memory_space=pltpu.VMEM),),
        input_output_aliases={i: 2 + i for i in range(2 * n)},
        compiler_params=pltpu.CompilerParams(has_side_effects=pltpu.SideEffectType.DATAFLOW_SIDE_EFFECTING),
    )(*[hbm(v) for v in gs + lands])


def scatter_wait(name, started, after):
    send_sems, recv_sems = started[0], started[1]
    thru = list(started[2:-1])
    n = len(thru) // 2

    def body(*refs):
        send_sems, recv_sems = refs[2 * n], refs[2 * n + 1]
        for cp in _to_owners(refs[:n], refs[n:2 * n], send_sems, recv_sems):
            cp.wait_send()
            cp.wait_recv()

    outs = pl.pallas_call(
        body, name=name, out_shape=tuple(pltpu.HBM(v.shape, v.dtype) for v in thru),
        in_specs=(_HBM,) * (2 * n) + (_SEM, _SEM, _ANY), out_specs=(_HBM,) * (2 * n),
        input_output_aliases={i: i for i in range(2 * n)},
        compiler_params=pltpu.CompilerParams(has_side_effects=pltpu.SideEffectType.DATAFLOW_SIDE_EFFECTING),
    )(*thru, send_sems, recv_sems, after)
    return outs[:n], outs[n:]


def sum_landed(name, g, land, me):
    _, a, b = g.shape
    tb = b // 2 if b % (2 * LANE) == 0 else b

    def body(me_ref, g_ref, land_ref, o_ref):
        acc = jnp.zeros(o_ref.shape, F32)
        for k in range(N_DEV):
            acc = acc + jnp.where(me_ref[0] == k, g_ref[0], land_ref[k]).astype(F32)
        o_ref[...] = acc

    return pl.pallas_call(
        body, name=name, out_shape=S((a, b), F32),
        grid_spec=pltpu.PrefetchScalarGridSpec(
            num_scalar_prefetch=1, grid=(b // tb,),
            in_specs=[pl.BlockSpec((1, a, tb), lambda j, me_: (me_[0], 0, j)),
                      pl.BlockSpec((N_DEV, a, tb), lambda j, me_: (0, 0, j))],
            out_specs=pl.BlockSpec((a, tb), lambda j, me_: (0, j))),
    )(me, g, land)


class GradScatter:
    def __init__(self, local_shapes):
        self.local_shapes = local_shapes
        self.pending = []

    def emit(self, tag, index, grads):
        names = [n for n in BIG if n in grads]
        gs = [grads[n].reshape(N_DEV, -1, grads[n].shape[1]) for n in names]
        started = scatter_start("rs_start_" + tag, gs)
        self.pending.append((tag, index, names, started))
        return started[-1][0:1, 0:1]

    def finish(self, after):
        me = (4 * lax.axis_index("x") + 2 * lax.axis_index("y") + lax.axis_index("c")).astype(jnp.int32).reshape(1)
        shards = {n: [None] * self.local_shapes[n][0] for n in BIG}
        for tag, index, names, started in self.pending:
            gs, lands = scatter_wait("rs_wait_" + tag, started, after)
            for n, g, land in zip(names, gs, lands):
                shards[n][index] = sum_landed("rs_sum_%s_%s" % (tag, n), g, land, me)
        out = {}
        for n in BIG:
            seg = jnp.stack(shards[n])
            out[n] = seg.transpose(0, 2, 1) if n in TRANSPOSED else seg
        return out


def gather_small(name, local, names):
    flat = _flatten([local[n] for n in names], LANE, 1)
    got = all_gather(name, flat).reshape(N_DEV, -1)
    full, off = {}, 0
    for n in names:
        size = math.prod(local[n].shape)
        seg = got[:, off:off + size].reshape((N_DEV,) + local[n].shape)
        full[n] = seg.transpose(1, 2, 0, 3).reshape(seg.shape[1], seg.shape[2], -1)
        off += size
    return full


def all_reduce_small(grads, names):
    flat = _flatten([grads[n] for n in names], LANE, SUBLANE)
    total = sum_devices("ar_sum", all_gather("ar_gather", flat)).reshape(-1)
    out, off = {}, 0
    for nm in names:
        size = math.prod(grads[nm].shape)
        out[nm] = total[off:off + size].reshape(grads[nm].shape)
        off += size
    return out


def adamw(name, w, g, m, v):
    shape = w.shape
    cols = shape[-1]
    rows = math.prod(shape[:-1])
    tr = _pick(rows, (256, 128, 64, 32, 16, 8)) if rows * cols > 256 * 1024 else rows
    c1 = 1.0 - ADAM_B1 ** ADAM_STEP
    c2 = 1.0 - ADAM_B2 ** ADAM_STEP

    def body(w_ref, g_ref, m_ref, v_ref, d_ref, nm_ref, nv_ref):
        gv = g_ref[...]
        m2 = ADAM_B1 * m_ref[...] + (1.0 - ADAM_B1) * gv
        v2 = ADAM_B2 * v_ref[...] + (1.0 - ADAM_B2) * jnp.square(gv)
        d_ref[...] = -ADAM_LR * ((m2 / c1) / (jnp.sqrt(v2 / c2) + ADAM_EPS) + ADAM_WD * w_ref[...])
        nm_ref[...] = m2
        nv_ref[...] = v2

    spec = pl.BlockSpec((tr, cols), lambda i: (i, 0))
    outs = pl.pallas_call(
        body, name=name, grid=(rows // tr,), in_specs=[spec] * 4, out_specs=[spec] * 3,
        out_shape=[S((rows, cols), F32)] * 3,
        compiler_params=pltpu.CompilerParams(dimension_semantics=("parallel",)),
    )(*[a.reshape(rows, cols) for a in (w, g, m, v)])
    return tuple(o.reshape(shape) for o in outs)


def kernel(x, p, e_w_in, e_conv_a_w, e_conv_a_b, e_ln_a_g, e_ln_a_b, e_conv_b_w, e_conv_b_b, e_dt_bias, e_a_log, e_d_skip, e_norm_b_g, e_w_out, o_w_in, o_conv_w, o_w_out, f_w_up, f_conv_w, f_conv_b, f_w_down, ple_w_proj, ple_w_gate, ln_g, ln_b, loss_target, m_e_w_in, m_e_conv_a_w, m_e_conv_a_b, m_e_ln_a_g, m_e_ln_a_b, m_e_conv_b_w, m_e_conv_b_b, m_e_dt_bias, m_e_a_log, m_e_d_skip, m_e_norm_b_g, m_e_w_out, m_o_w_in, m_o_conv_w, m_o_w_out, m_f_w_up, m_f_conv_w, m_f_conv_b, m_f_w_down, m_ple_w_proj, m_ple_w_gate, m_ln_g, m_ln_b, v_e_w_in, v_e_conv_a_w, v_e_conv_a_b, v_e_ln_a_g, v_e_ln_a_b, v_e_conv_b_w, v_e_conv_b_b, v_e_dt_bias, v_e_a_log, v_e_d_skip, v_e_norm_b_g, v_e_w_out, v_o_w_in, v_o_conv_w, v_o_w_out, v_f_w_up, v_f_conv_w, v_f_conv_b, v_f_w_down, v_ple_w_proj, v_ple_w_gate, v_ln_g, v_ln_b):
    args = locals()
    local = {n: args[n] for n in WEIGHTS}
    mom = {n: args["m_" + n] for n in WEIGHTS}
    var = {n: args["v_" + n] for n in WEIGHTS}

    full = {n: local[n] for n in REPLICATED}
    big, fetch, token = gather_big(local)
    full.update(big)
    full.update(gather_small("ag_small", local, SMALL_SHARDED))

    scatter = GradScatter({n: local[n].shape for n in BIG})
    loss_local, grad_x, grads = local_step(x[0] + token, p[:, 0], full, loss_target[0], fetch, scatter.emit)
    loss = lax.psum(loss_local, MESH_AXES)

    g_local = scatter.finish(grad_x)
    small = all_reduce_small(grads, REPLICATED + SMALL_SHARDED)
    dev = 4 * lax.axis_index("x") + 2 * lax.axis_index("y") + lax.axis_index("c")
    for n in REPLICATED:
        g_local[n] = small[n]
    for n in SMALL_SHARDED:
        width = local[n].shape[2]
        g_local[n] = lax.dynamic_slice_in_dim(small[n], dev * width, width, axis=2)

    delta, new_m, new_v = {}, {}, {}
    for n in WEIGHTS:
        delta[n], new_m[n], new_v[n] = adamw("adamw_" + n, local[n], g_local[n], mom[n], var[n])
    return (loss, grad_x[None], *[g_local[n] for n in WEIGHTS], *[delta[n] for n in WEIGHTS],
            *[new_m[n] for n in WEIGHTS], *[new_v[n] for n in WEIGHTS])
```

```python
import functools
import math

import jax
import jax.numpy as jnp
from jax import lax
from jax.experimental import pallas as pl
from jax.experimental.pallas import tpu as pltpu

F32 = jnp.float32
MXU_DTYPE = jnp.bfloat16
MESH_AXES = ("x", "y", "c")
N_DEV = 8
LANE = 128
SUBLANE = 8
ROWWISE_VMEM_BUDGET = 20 * 1024 * 1024
MM_TILES = (1408, 1024, 512, 256, 128)
EXCHANGE_DTYPE = jnp.bfloat16
LN_EPS = 1e-5
CHUNK = 64
HEAD_DIM = 64
N_GROUPS = 4
N_STATE = 128
CONV_PAD = 32
CONV_ROWS = 256
ADAM_LR, ADAM_B1, ADAM_B2, ADAM_EPS, ADAM_WD, ADAM_STEP = 0.001, 0.9, 0.999, 1e-08, 0.01, 10

BIG = ("e_w_in", "e_w_out", "o_w_in", "o_w_out", "f_w_up", "f_w_down", "ple_w_proj", "ple_w_gate")
SMALL_SHARDED = ("e_conv_a_w", "e_conv_b_w", "o_conv_w", "f_conv_w", "ln_g", "ln_b")
REPLICATED = ("e_conv_a_b", "e_ln_a_g", "e_ln_a_b", "e_conv_b_b", "e_dt_bias", "e_a_log", "e_d_skip",
              "e_norm_b_g", "f_conv_b")
TRANSPOSED = ("e_w_in", "o_w_in", "f_w_up", "ple_w_proj")
WEIGHTS = ("e_w_in", "e_conv_a_w", "e_conv_a_b", "e_ln_a_g", "e_ln_a_b", "e_conv_b_w", "e_conv_b_b", "e_dt_bias",
           "e_a_log", "e_d_skip", "e_norm_b_g", "e_w_out", "o_w_in", "o_conv_w", "o_w_out", "f_w_up", "f_conv_w",
           "f_conv_b", "f_w_down", "ple_w_proj", "ple_w_gate", "ln_g", "ln_b")

S = jax.ShapeDtypeStruct


class Win:
    def __init__(self, arr, w=None, idx=0, coef=1.0):
        self.arr, self.w, self.idx, self.coef = arr, (arr.shape[1] if w is None else w), idx, coef


def _win(a):
    return a if isinstance(a, Win) else Win(a)


def _pick(n, prefs):
    for p in prefs:
        if p <= n and n % p == 0:
            return p
    return n


_MM_DIMS = {"nn": (1, 0), "nt": (1, 1), "tn": (0, 0)}


def mm(name, a, b, mode, out_dtype=F32):
    ca, cb = _MM_DIMS[mode]
    kdim = a.shape[ca]
    m = a.shape[1 - ca]
    n = b.shape[1 - cb]
    assert b.shape[cb] == kdim, (name, a.shape, b.shape, mode)
    tm = _pick(m, MM_TILES)
    tn = _pick(n, MM_TILES)
    tk = kdim if kdim <= MM_TILES[0] else _pick(kdim, MM_TILES)
    nk = kdim // tk
    own_acc = nk > 1 and out_dtype != F32

    def body(a_ref, b_ref, o_ref, *scratch):
        acc_ref = scratch[0] if own_acc else o_ref
        d = lax.dot_general(a_ref[...].astype(MXU_DTYPE), b_ref[...].astype(MXU_DTYPE),
                            (((ca,), (cb,)), ((), ())), preferred_element_type=F32)
        if nk == 1:
            o_ref[...] = d.astype(o_ref.dtype)
        else:
            k = pl.program_id(2)

            @pl.when(k == 0)
            def _():
                acc_ref[...] = d

            @pl.when(k > 0)
            def _():
                acc_ref[...] += d

            if own_acc:
                @pl.when(k == nk - 1)
                def _():
                    o_ref[...] = acc_ref[...].astype(o_ref.dtype)

    a_spec = pl.BlockSpec((tm, tk), lambda i, j, k: (i, k)) if ca == 1 else pl.BlockSpec((tk, tm), lambda i, j, k: (k, i))
    b_spec = pl.BlockSpec((tk, tn), lambda i, j, k: (k, j)) if cb == 0 else pl.BlockSpec((tn, tk), lambda i, j, k: (j, k))
    return pl.pallas_call(
        body, name=name, grid=(m // tm, n // tn, nk),
        in_specs=[a_spec, b_spec], out_specs=pl.BlockSpec((tm, tn), lambda i, j, k: (i, j)),
        out_shape=S((m, n), out_dtype), scratch_shapes=[pltpu.VMEM((tm, tn), F32)] if own_acc else [],
        compiler_params=pltpu.CompilerParams(dimension_semantics=("parallel", "parallel", "arbitrary")),
    )(a, b)


def _row_block(t, widths):
    tb = 512
    while tb > SUBLANE and (t % tb or tb * sum(widths) * 8 > ROWWISE_VMEM_BUDGET):
        tb //= 2
    return tb


def _tok_spec(tb, w):
    return pl.BlockSpec((tb, w.w), functools.partial(lambda i, idx: (i, idx), idx=w.idx))


def _par_spec(p):
    return pl.BlockSpec((1, p.shape[1]), lambda i: (0, 0))


def rowwise(name, fn, tok, par, out_widths, red_widths=(), out_dtypes=None):
    tok = [_win(t) for t in tok]
    t = tok[0].arr.shape[0]
    tb = _row_block(t, [w.w for w in tok] + list(out_widths))
    n_tok, n_par, n_out = len(tok), len(par), len(out_widths)
    out_dtypes = [F32] * n_out if out_dtypes is None else out_dtypes

    def body(*refs):
        ins = [r[...] for r in refs[:n_tok + n_par]]
        res = fn(*ins)
        out_refs = refs[n_tok + n_par:n_tok + n_par + n_out]
        red_refs = refs[n_tok + n_par + n_out:]
        for r, v in zip(out_refs, res[:n_out]):
            r[...] = v.astype(r.dtype)
        if red_refs:
            @pl.when(pl.program_id(0) == 0)
            def _():
                for r in red_refs:
                    r[...] = jnp.zeros_like(r)
            for r, v in zip(red_refs, res[n_out:]):
                r[...] += v

    outs = pl.pallas_call(
        body, name=name, grid=(t // tb,),
        in_specs=[_tok_spec(tb, w) for w in tok] + [_par_spec(p) for p in par],
        out_specs=[pl.BlockSpec((tb, w), lambda i: (i, 0)) for w in out_widths]
        + [pl.BlockSpec((1, w), lambda i: (0, 0)) for w in red_widths],
        out_shape=[S((t, w), dt) for w, dt in zip(out_widths, out_dtypes)] + [S((1, w), F32) for w in red_widths],
        compiler_params=pltpu.CompilerParams(dimension_semantics=("arbitrary",)),
    )(*[w.arr for w in tok], *par)
    return outs


def rowwise_bwd(name, fn, tok, par, cts, d_dtypes=None):
    tok = [_win(t) for t in tok]
    cts = [[_win(c) for c in group] for group in cts]
    d_dtypes = [(F32,)] * len(tok) if d_dtypes is None else d_dtypes
    t = tok[0].arr.shape[0]
    flat_cts = [c for group in cts for c in group]
    d_outs = [(i, w.w, dt) for i, (w, dts) in enumerate(zip(tok, d_dtypes)) for dt in dts]
    tb = _row_block(t, [w.w for w in tok] + [c.w for c in flat_cts] + [w for _, w, _ in d_outs])
    n_tok, n_par, n_ct, n_d = len(tok), len(par), len(flat_cts), len(d_outs)

    def body(*refs):
        tok_vals = [r[...] for r in refs[:n_tok]]
        par_vals = [r[...] for r in refs[n_tok:n_tok + n_par]]
        ct_refs = refs[n_tok + n_par:n_tok + n_par + n_ct]
        d_refs = refs[n_tok + n_par + n_ct:n_tok + n_par + n_ct + n_d]
        dp_refs = refs[n_tok + n_par + n_ct + n_d:]
        ct_vals, pos = [], 0
        for group in cts:
            acc = None
            for c in group:
                v = ct_refs[pos][...]
                if c.coef != 1.0:
                    v = v * c.coef
                acc = v if acc is None else acc + v
                pos += 1
            ct_vals.append(acc)
        _, vjp = jax.vjp(lambda *a: tuple(fn(*a)), *tok_vals, *par_vals)
        grads = vjp(tuple(ct_vals))
        for r, (i, _, _) in zip(d_refs, d_outs):
            r[...] = grads[i].astype(r.dtype)
        if dp_refs:
            @pl.when(pl.program_id(0) == 0)
            def _():
                for r in dp_refs:
                    r[...] = jnp.zeros_like(r)
            for r, v in zip(dp_refs, grads[n_tok:]):
                r[...] += v

    outs = pl.pallas_call(
        body, name=name, grid=(t // tb,),
        in_specs=[_tok_spec(tb, w) for w in tok] + [_par_spec(p) for p in par] + [_tok_spec(tb, c) for c in flat_cts],
        out_specs=[pl.BlockSpec((tb, w), lambda i: (i, 0)) for _, w, _ in d_outs] + [_par_spec(p) for p in par],
        out_shape=[S((t, w), dt) for _, w, dt in d_outs] + [S(p.shape, F32) for p in par],
        compiler_params=pltpu.CompilerParams(dimension_semantics=("arbitrary",)),
    )(*[w.arr for w in tok], *par, *[c.arr for c in flat_cts])
    return outs[:n_d], outs[n_d:]


def _sigmoid(x):
    return 1.0 / (1.0 + jnp.exp(-x))


def _silu(x):
    return x * _sigmoid(x)


def _ln(x, g, b):
    mu = jnp.mean(x, axis=-1, keepdims=True)
    var = jnp.mean(jnp.square(x - mu), axis=-1, keepdims=True)
    return (x - mu) * lax.rsqrt(var + LN_EPS) * g + b


def f_glu(ua, ug):
    return (ua * _sigmoid(ug),)


def f_ln_silu(a1, g, b):
    return (_silu(_ln(a1, g, b)),)


def f_silu3(a, b, c):
    return (_silu(a), _silu(b), _silu(c))


def f_softplus(dt_raw, bias):
    return (jax.nn.softplus(dt_raw + bias),)


def f_gate_rms(yssd, z, g):
    y = yssd * _silu(z)
    return (y * lax.rsqrt(jnp.mean(jnp.square(y), axis=-1, keepdims=True) + LN_EPS) * g,)


def f_ln(pre, g, b):
    return (_ln(pre, g, b),)


def f_mul(a, b):
    return (a * b,)


def f_gate_mul(pp, gt):
    return (pp * _sigmoid(gt),)


def conv_fwd(name, x, w, b):
    x = _win(x)
    t, c = x.arr.shape[0], x.w
    kw = w.shape[0]
    cb = LANE
    off = x.idx * (c // cb)
    rows = min(CONV_ROWS, t)
    has_b = b is not None

    def body(*refs):
        if has_b:
            x_ref, w_ref, b_ref, y_ref, xp_ref = refs
        else:
            x_ref, w_ref, y_ref, xp_ref = refs
        xp_ref[0:CONV_PAD, :] = jnp.zeros((CONV_PAD, cb), F32)
        xp_ref[CONV_PAD:CONV_PAD + t, :] = x_ref[...]

        def step(s, carry):
            base = pl.multiple_of(s * rows, rows)
            acc = jnp.zeros((rows, cb), F32)
            if has_b:
                acc = acc + b_ref[...]
            for k in range(kw):
                acc = acc + w_ref[k:k + 1, :] * xp_ref[pl.ds(base + CONV_PAD - (kw - 1) + k, rows), :]
            y_ref[pl.ds(base, rows), :] = acc
            return carry

        lax.fori_loop(0, t // rows, step, 0)

    in_specs = [pl.BlockSpec((t, cb), lambda j: (0, off + j)), pl.BlockSpec((kw, cb), lambda j: (0, j))]
    args = [x.arr, w]
    if has_b:
        in_specs.append(pl.BlockSpec((1, cb), lambda j: (0, j)))
        args.append(b)
    return pl.pallas_call(
        body, name=name, grid=(c // cb,), in_specs=in_specs,
        out_specs=pl.BlockSpec((t, cb), lambda j: (0, j)), out_shape=S((t, c), F32),
        scratch_shapes=[pltpu.VMEM((CONV_PAD + t, cb), F32)],
        compiler_params=pltpu.CompilerParams(dimension_semantics=("parallel",)),
    )(*args)


def conv_bwd(name, x, dy, w, dx_dtype=F32):
    x, dy = _win(x), _win(dy)
    t, c = x.arr.shape[0], x.w
    kw = w.shape[0]
    cb = LANE
    xoff = x.idx * (c // cb)
    dyoff = dy.idx * (c // cb)
    rows = min(CONV_ROWS, t)

    def body(x_ref, dy_ref, w_ref, dx_ref, dw_ref, db_ref, xp_ref, dyp_ref):
        xp_ref[0:CONV_PAD, :] = jnp.zeros((CONV_PAD, cb), F32)
        xp_ref[CONV_PAD:CONV_PAD + t, :] = x_ref[...]
        dyp_ref[0:t, :] = dy_ref[...]
        dyp_ref[t:t + CONV_PAD, :] = jnp.zeros((CONV_PAD, cb), F32)

        def fold(v):
            return jnp.sum(v.reshape(rows // SUBLANE, SUBLANE, cb), axis=0)

        def step(s, carry):
            base = pl.multiple_of(s * rows, rows)
            dyc = dy_ref[pl.ds(base, rows), :]
            acc = jnp.zeros((rows, cb), F32)
            new = []
            for k in range(kw):
                acc = acc + w_ref[k:k + 1, :] * dyp_ref[pl.ds(base + (kw - 1) - k, rows), :]
                new.append(carry[k] + fold(dyc * xp_ref[pl.ds(base + CONV_PAD - (kw - 1) + k, rows), :]))
            new.append(carry[kw] + fold(dyc))
            dx_ref[pl.ds(base, rows), :] = acc.astype(dx_ref.dtype)
            return tuple(new)

        init = tuple(jnp.zeros((SUBLANE, cb), F32) for _ in range(kw + 1))
        parts = lax.fori_loop(0, t // rows, step, init)
        for k in range(kw):
            dw_ref[k:k + 1, :] = jnp.sum(parts[k], axis=0, keepdims=True)
        db_ref[...] = jnp.sum(parts[kw], axis=0, keepdims=True)

    return pl.pallas_call(
        body, name=name, grid=(c // cb,),
        in_specs=[pl.BlockSpec((t, cb), lambda j: (0, xoff + j)), pl.BlockSpec((t, cb), lambda j: (0, dyoff + j)),
                  pl.BlockSpec((kw, cb), lambda j: (0, j))],
        out_specs=[pl.BlockSpec((t, cb), lambda j: (0, j)), pl.BlockSpec((kw, cb), lambda j: (0, j)),
                   pl.BlockSpec((1, cb), lambda j: (0, j))],
        out_shape=[S((t, c), dx_dtype), S((kw, c), F32), S((1, c), F32)],
        scratch_shapes=[pltpu.VMEM((CONV_PAD + t, cb), F32), pltpu.VMEM((CONV_PAD + t, cb), F32)],
        compiler_params=pltpu.CompilerParams(dimension_semantics=("parallel",)),
    )(x.arr, dy.arr, w)


def gated_conv_fwd(name, hpre, w, b, out_dtype):
    t, c2 = hpre.shape
    ff = c2 // 2
    kw = w.shape[0]
    cb = LANE
    nb = ff // cb
    rows = min(CONV_ROWS, t)

    def body(h1_ref, h2_ref, w1_ref, w2_ref, b1_ref, b2_ref, y_ref, xp1_ref, xp2_ref):
        for xp_ref, h_ref in ((xp1_ref, h1_ref), (xp2_ref, h2_ref)):
            xp_ref[0:CONV_PAD, :] = jnp.zeros((CONV_PAD, cb), F32)
            xp_ref[CONV_PAD:CONV_PAD + t, :] = h_ref[...]

        def step(s, carry):
            base = pl.multiple_of(s * rows, rows)
            h1 = jnp.zeros((rows, cb), F32) + b1_ref[...]
            h2 = jnp.zeros((rows, cb), F32) + b2_ref[...]
            for k in range(kw):
                at = pl.ds(base + CONV_PAD - (kw - 1) + k, rows)
                h1 = h1 + w1_ref[k:k + 1, :] * xp1_ref[at, :]
                h2 = h2 + w2_ref[k:k + 1, :] * xp2_ref[at, :]
            y_ref[pl.ds(base, rows), :] = (_silu(h1) * h2).astype(y_ref.dtype)
            return carry

        lax.fori_loop(0, t // rows, step, 0)

    col1 = lambda r: pl.BlockSpec((r, cb), lambda j: (0, j))
    col2 = lambda r: pl.BlockSpec((r, cb), lambda j: (0, nb + j))
    return pl.pallas_call(
        body, name=name, grid=(nb,),
        in_specs=[col1(t), col2(t), col1(kw), col2(kw), col1(1), col2(1)],
        out_specs=col1(t), out_shape=S((t, ff), out_dtype),
        scratch_shapes=[pltpu.VMEM((CONV_PAD + t, cb), F32)] * 2,
        compiler_params=pltpu.CompilerParams(dimension_semantics=("parallel",)),
    )(hpre, hpre, w, w, b, b)


def gated_conv_bwd(name, hpre, dact, w, b, dx_dtype):
    t, c2 = hpre.shape
    ff = c2 // 2
    kw = w.shape[0]
    cb = LANE
    nb = ff // cb
    rows = min(CONV_ROWS, t)

    def body(own_ref, oth_ref, da_ref, wo_ref, wt_ref, bo_ref, bt_ref, dx_ref, dw_ref, db_ref,
             xpo_ref, xpt_ref, dhp_ref):
        for xp_ref, h_ref in ((xpo_ref, own_ref), (xpt_ref, oth_ref)):
            xp_ref[0:CONV_PAD, :] = jnp.zeros((CONV_PAD, cb), F32)
            xp_ref[CONV_PAD:CONV_PAD + t, :] = h_ref[...]
        dhp_ref[t:t + CONV_PAD, :] = jnp.zeros((CONV_PAD, cb), F32)

        def fold(v):
            return jnp.sum(v.reshape(rows // SUBLANE, SUBLANE, cb), axis=0)

        def first_pass(own_is_gate):
            def step(s, carry):
                base = pl.multiple_of(s * rows, rows)
                ho = jnp.zeros((rows, cb), F32) + bo_ref[...]
                ht = jnp.zeros((rows, cb), F32) + bt_ref[...]
                for k in range(kw):
                    at = pl.ds(base + CONV_PAD - (kw - 1) + k, rows)
                    ho = ho + wo_ref[k:k + 1, :] * xpo_ref[at, :]
                    ht = ht + wt_ref[k:k + 1, :] * xpt_ref[at, :]
                da = da_ref[pl.ds(base, rows), :]
                if own_is_gate:
                    sg = _sigmoid(ho)
                    dh = da * ht * (sg * (1.0 + ho * (1.0 - sg)))
                else:
                    dh = da * _silu(ht)
                dhp_ref[pl.ds(base, rows), :] = dh
                new = [carry[k] + fold(dh * xpo_ref[pl.ds(base + CONV_PAD - (kw - 1) + k, rows), :]) for k in range(kw)]
                new.append(carry[kw] + fold(dh))
                return tuple(new)

            init = tuple(jnp.zeros((SUBLANE, cb), F32) for _ in range(kw + 1))
            parts = lax.fori_loop(0, t // rows, step, init)
            for k in range(kw):
                dw_ref[k:k + 1, :] = jnp.sum(parts[k], axis=0, keepdims=True)
            db_ref[...] = jnp.sum(parts[kw], axis=0, keepdims=True)

        half = pl.program_id(0)

        @pl.when(half == 0)
        def _():
            first_pass(True)

        @pl.when(half == 1)
        def _():
            first_pass(False)

        def second(s, carry):
            base = pl.multiple_of(s * rows, rows)
            acc = jnp.zeros((rows, cb), F32)
            for k in range(kw):
                acc = acc + wo_ref[k:k + 1, :] * dhp_ref[pl.ds(base + (kw - 1) - k, rows), :]
            dx_ref[pl.ds(base, rows), :] = acc.astype(dx_ref.dtype)
            return carry

        lax.fori_loop(0, t // rows, second, 0)

    own = lambda r: pl.BlockSpec((r, cb), lambda h, j: (0, h * nb + j))
    oth = lambda r: pl.BlockSpec((r, cb), lambda h, j: (0, (1 - h) * nb + j))
    return pl.pallas_call(
        body, name=name, grid=(2, nb),
        in_specs=[own(t), oth(t), pl.BlockSpec((t, cb), lambda h, j: (0, j)), own(kw), oth(kw), own(1), oth(1)],
        out_specs=[own(t), own(kw), own(1)],
        out_shape=[S((t, c2), dx_dtype), S((kw, c2), F32), S((1, c2), F32)],
        scratch_shapes=[pltpu.VMEM((CONV_PAD + t, cb), F32)] * 3,
        compiler_params=pltpu.CompilerParams(dimension_semantics=("parallel", "parallel")),
    )(hpre, hpre, dact, w, w, b, b)


def _bdot(a, b, ca, cb):
    return lax.dot_general(a.astype(MXU_DTYPE), b.astype(MXU_DTYPE), (((ca,), (cb,)), ((0,), (0,))),
                           preferred_element_type=F32)


@jax.custom_vjp
def bmm_nn(a, b):
    return _bdot(a, b, 2, 1)


bmm_nn.defvjp(lambda a, b: (_bdot(a, b, 2, 1), (a, b)),
              lambda r, g: (_bdot(g, r[1], 2, 2), _bdot(r[0], g, 1, 1)))


@jax.custom_vjp
def bmm_tn(a, b):
    return _bdot(a, b, 1, 1)


bmm_tn.defvjp(lambda a, b: (_bdot(a, b, 1, 1), (a, b)),
              lambda r, g: (_bdot(r[1], g, 2, 2), _bdot(r[0], g, 2, 1)))


@jax.custom_vjp
def bmm_nt(a, b):
    return _bdot(a, b, 2, 2)


bmm_nt.defvjp(lambda a, b: (_bdot(a, b, 2, 2), (a, b)),
              lambda r, g: (_bdot(g, r[1], 2, 1), _bdot(g, r[0], 1, 1)))


def ssd_chunk(x, dt, dt_row, bm, cm, hprev, a_log, dsk):
    hg, ln, _ = x.shape
    n = bm.shape[1]
    ii = lax.broadcasted_iota(jnp.int32, (ln, ln), 0)
    jj = lax.broadcasted_iota(jnp.int32, (ln, ln), 1)
    tril, triu = (ii >= jj)[None], (ii <= jj)[None]
    a = -jnp.exp(a_log)
    da = dt * a
    da_row = dt_row * a
    cum_c = jnp.sum(jnp.where(tril, da_row, 0.0), axis=2, keepdims=True)
    cum_r = jnp.sum(jnp.where(triu, da, 0.0), axis=1, keepdims=True)
    last = jnp.sum(da, axis=1, keepdims=True)
    decay = jnp.where(tril, jnp.exp(jnp.where(tril, cum_c - cum_r, 0.0)), 0.0)
    cb = bmm_nt(cm[None], bm[None])
    y_diag = bmm_nn(cb * decay * dt_row, x)
    bb = jnp.broadcast_to(bm[None], (hg, ln, n))
    cc = jnp.broadcast_to(cm[None], (hg, ln, n))
    states = bmm_tn(x * (jnp.exp(last - cum_c) * dt), bb)
    y_off = bmm_nt(cc, hprev) * jnp.exp(cum_c)
    hnew = hprev * jnp.exp(last) + states
    return y_diag + y_off + dsk * x, hnew


def _ssd_dims(xs, bm, a_log):
    t = xs.shape[0]
    h = a_log.shape[0]
    return h, t, xs.shape[1] // h, h // N_GROUPS, bm.shape[1] // N_GROUPS, t // CHUNK


def _heads_of(ref, g, hg, p):
    return jnp.stack([ref[:, (g * hg + i) * p:(g * hg + i + 1) * p] for i in range(hg)])


def _cols_of(ref, g, hg):
    return jnp.stack([ref[:, g * hg + i:g * hg + i + 1] for i in range(hg)])


def dt_rows(dt, h):
    t = dt.shape[0]
    return dt[:, :h].T.reshape(h, t // CHUNK, 1, CHUNK).transpose(1, 0, 2, 3)


def dt_cols(dtr, lanes):
    nc, h, _, ln = dtr.shape
    return jnp.pad(dtr.transpose(1, 0, 2, 3).reshape(h, nc * ln).T, ((0, 0), (0, lanes - h)))


def ssd_fwd(name, xs, dt, dtr, bm, cm, a_log, dsk):
    h, t, p, hg, n, nc = _ssd_dims(xs, bm, a_log)

    def body(al_ref, dk_ref, x_ref, dt_ref, dtr_ref, b_ref, c_ref, y_ref, hp_ref, h_scr):
        @pl.when(pl.program_id(0) == 0)
        def _():
            h_scr[...] = jnp.zeros_like(h_scr)

        for g in range(N_GROUPS):
            hs, ns = slice(g * hg, (g + 1) * hg), slice(g * n, (g + 1) * n)
            hprev = h_scr[hs]
            hp_ref[hs, 0] = hprev
            y, hnew = ssd_chunk(_heads_of(x_ref, g, hg, p), _cols_of(dt_ref, g, hg), dtr_ref[0, hs], b_ref[:, ns],
                                c_ref[:, ns], hprev, al_ref[hs], dk_ref[hs])
            for i in range(hg):
                y_ref[:, (g * hg + i) * p:(g * hg + i + 1) * p] = y[i]
            h_scr[hs] = hnew

    head = pl.BlockSpec((h, 1, 1), lambda c: (0, 0, 0))
    row = lambda w: pl.BlockSpec((CHUNK, w), lambda c: (c, 0))
    return pl.pallas_call(
        body, name=name, grid=(nc,),
        in_specs=[head, head, row(h * p), row(dt.shape[1]), pl.BlockSpec((1, h, 1, CHUNK), lambda c: (c, 0, 0, 0)),
                  row(N_GROUPS * n), row(N_GROUPS * n)],
        out_specs=[row(h * p), pl.BlockSpec((h, 1, p, n), lambda c: (0, c, 0, 0))],
        out_shape=[S((t, h * p), F32), S((h, nc, p, n), F32)],
        scratch_shapes=[pltpu.VMEM((h, p, n), F32)],
        compiler_params=pltpu.CompilerParams(dimension_semantics=("arbitrary",)),
    )(a_log, dsk, xs, dt, dtr, bm, cm)


def ssd_bwd(name, xs, dt, dtr, bm, cm, a_log, dsk, hp, dy):
    h, t, p, hg, n, nc = _ssd_dims(xs, bm, a_log)

    def body(al_ref, dk_ref, x_ref, dt_ref, dtr_ref, b_ref, c_ref, hp_ref, dy_ref,
             dx_ref, ddt_ref, ddtr_ref, db_ref, dc_ref, dal_ref, ddk_ref, dh_scr):
        @pl.when(pl.program_id(0) == 0)
        def _():
            dh_scr[...] = jnp.zeros_like(dh_scr)
            dal_ref[...] = jnp.zeros_like(dal_ref)
            ddk_ref[...] = jnp.zeros_like(ddk_ref)

        ddt_ref[...] = jnp.zeros_like(ddt_ref)
        for g in range(N_GROUPS):
            hs, ns = slice(g * hg, (g + 1) * hg), slice(g * n, (g + 1) * n)
            _, vjp = jax.vjp(ssd_chunk, _heads_of(x_ref, g, hg, p), _cols_of(dt_ref, g, hg), dtr_ref[0, hs],
                             b_ref[:, ns], c_ref[:, ns], hp_ref[hs, 0], al_ref[hs], dk_ref[hs])
            gx, gdt, gdtr, gb, gc, ghp, gal, gdk = vjp((_heads_of(dy_ref, g, hg, p), dh_scr[hs]))
            for i in range(hg):
                dx_ref[:, (g * hg + i) * p:(g * hg + i + 1) * p] = gx[i]
                ddt_ref[:, g * hg + i:g * hg + i + 1] = gdt[i]
            ddtr_ref[0, hs] = gdtr
            db_ref[:, ns] = gb
            dc_ref[:, ns] = gc
            dh_scr[hs] = ghp
            dal_ref[hs] += gal
            ddk_ref[hs] += gdk

    head = pl.BlockSpec((h, 1, 1), lambda c: (0, 0, 0))
    row = lambda w: pl.BlockSpec((CHUNK, w), lambda c: (nc - 1 - c, 0))
    rows = pl.BlockSpec((1, h, 1, CHUNK), lambda c: (nc - 1 - c, 0, 0, 0))
    return pl.pallas_call(
        body, name=name, grid=(nc,),
        in_specs=[head, head, row(h * p), row(dt.shape[1]), rows, row(N_GROUPS * n), row(N_GROUPS * n),
                  pl.BlockSpec((h, 1, p, n), lambda c: (0, nc - 1 - c, 0, 0)), row(h * p)],
        out_specs=[row(h * p), row(dt.shape[1]), rows, row(N_GROUPS * n), row(N_GROUPS * n), head, head],
        out_shape=[S((t, h * p), F32), S(dt.shape, F32), S(dtr.shape, F32), S(bm.shape, F32), S(cm.shape, F32),
                   S((h, 1, 1), F32), S((h, 1, 1), F32)],
        scratch_shapes=[pltpu.VMEM((h, p, n), F32)],
        compiler_params=pltpu.CompilerParams(dimension_semantics=("arbitrary",)),
    )(a_log, dsk, xs, dt, dtr, bm, cm, hp, dy)


def _alpha(depth):
    return (2.0 * depth) ** 0.25


def _pad_lanes(v):
    return jnp.pad(v, ((0, 0), (0, LANE - v.shape[1])))


def split_even_weights(w, j):
    d = w["e_w_in"][j].shape[1]
    da = w["e_conv_a_w"].shape[2]
    db = w["e_norm_b_g"].shape[1]
    gn = N_GROUPS * N_STATE
    nh = w["e_dt_bias"].shape[1]
    main = 2 * da + 2 * db + 2 * gn
    win = w["e_w_in"][j]
    ox = 2 * da + db
    cw, cbias = w["e_conv_b_w"][j], w["e_conv_b_b"][j][None]
    return dict(
        d=d, da=da, db=db, gn=gn, nh=nh, main=main,
        win_main=win[:main], win_dt=jnp.pad(win[main:], ((0, LANE - nh), (0, 0))),
        caw=w["e_conv_a_w"][j], cab=w["e_conv_a_b"][j][None], lag=w["e_ln_a_g"][j][None], lab=w["e_ln_a_b"][j][None],
        cw_xs=cw[:, :db], cw_b=cw[:, db:db + gn], cw_c=cw[:, db + gn:],
        cb_xs=cbias[:, :db], cb_b=cbias[:, db:db + gn], cb_c=cbias[:, db + gn:],
        dt_bias=_pad_lanes(w["e_dt_bias"][j][None]), a_log=w["e_a_log"][j].reshape(nh, 1, 1),
        dsk=w["e_d_skip"][j].reshape(nh, 1, 1), norm_g=w["e_norm_b_g"][j][None],
        wout_a=w["e_w_out"][j][:da], wout_b=w["e_w_out"][j][da:],
    )


def even_fwd(tag, x, xm, lw, ln_g, ln_b, alpha):
    t = x.shape[0]
    da, db, gn, nh = lw["da"], lw["db"], lw["gn"], lw["nh"]
    u = mm(tag + "_win", xm, lw["win_main"], "nt")
    udt = mm(tag + "_windt", xm, lw["win_dt"], "nt")
    ua, ug, z, xs_pre = Win(u, da, 0), Win(u, da, 1), Win(u, db, 2 * da // db), Win(u, db, (2 * da + db) // db)
    b_pre, c_pre = Win(u, gn, (2 * da + 2 * db) // gn), Win(u, gn, (2 * da + 2 * db + gn) // gn)
    (a0,) = rowwise(tag + "_glu", f_glu, [ua, ug], [], [da])
    a1 = conv_fwd(tag + "_conva", a0, lw["caw"], lw["cab"])
    (ya,) = rowwise(tag + "_lna", f_ln_silu, [a1], [lw["lag"], lw["lab"]], [da], out_dtypes=[MXU_DTYPE])
    xs_c = conv_fwd(tag + "_convxs", xs_pre, lw["cw_xs"], lw["cb_xs"])
    b_c = conv_fwd(tag + "_convb", b_pre, lw["cw_b"], lw["cb_b"])
    c_c = conv_fwd(tag + "_convc", c_pre, lw["cw_c"], lw["cb_c"])
    xs, bm, cm = rowwise(tag + "_silu3", f_silu3, [xs_c, b_c, c_c], [], [db, gn, gn])
    (dt,) = rowwise(tag + "_dt", f_softplus, [udt], [lw["dt_bias"]], [LANE])
    dtr = dt_rows(dt, nh)
    yssd, hp = ssd_fwd(tag + "_ssd", xs, dt, dtr, bm, cm, lw["a_log"], lw["dsk"])
    (yb,) = rowwise(tag + "_gate", f_gate_rms, [yssd, z], [lw["norm_g"]], [db], out_dtypes=[MXU_DTYPE])
    ma = mm(tag + "_wouta", ya, lw["wout_a"], "nn")
    mb = mm(tag + "_woutb", yb, lw["wout_b"], "nn")

    def f_res(xv, mav, mbv, g, b):
        pre = alpha * xv + mav + mbv
        y = _ln(pre, g, b)
        return y, y, pre

    x1, x1m, pre = rowwise(tag + "_res", f_res, [x, ma, mb], [ln_g, ln_b], [x.shape[1]] * 3,
                           out_dtypes=[F32, MXU_DTYPE, F32])
    saved = dict(xm=xm, u=u, udt=udt, a0=a0, a1=a1, ya=ya, xs_c=xs_c, b_c=b_c, c_c=c_c, xs=xs, dt=dt, dtr=dtr, bm=bm,
                 cm=cm, hp=hp, yssd=yssd, yb=yb, pre=pre)
    return x1, x1m, saved


def even_bwd(tag, dx1_pieces, sv, lw, ln_g, ln_b, alpha):
    t = sv["u"].shape[0]
    da, db, gn, nh = lw["da"], lw["db"], lw["gn"], lw["nh"]
    u, xm = sv["u"], sv["xm"]
    mx = (MXU_DTYPE,)
    ua, ug, z, xs_pre = Win(u, da, 0), Win(u, da, 1), Win(u, db, 2 * da // db), Win(u, db, (2 * da + db) // db)
    b_pre, c_pre = Win(u, gn, (2 * da + 2 * db) // gn), Win(u, gn, (2 * da + 2 * db + gn) // gn)
    (dpre, dprem), (dg0, db0) = rowwise_bwd(tag + "_res_b", f_ln, [sv["pre"]], [ln_g, ln_b], [dx1_pieces],
                                            d_dtypes=[(F32, MXU_DTYPE)])
    dya = mm(tag + "_dya", dprem, lw["wout_a"], "nt")
    dyb = mm(tag + "_dyb", dprem, lw["wout_b"], "nt")
    dwout_a = mm(tag + "_dwouta", sv["ya"], dprem, "tn", EXCHANGE_DTYPE)
    dwout_b = mm(tag + "_dwoutb", sv["yb"], dprem, "tn", EXCHANGE_DTYPE)
    (dyssd, dz), (dnorm_g,) = rowwise_bwd(tag + "_gate_b", f_gate_rms, [sv["yssd"], z], [lw["norm_g"]], [[dyb]],
                                          d_dtypes=[(F32,), mx])
    dxs, ddt, ddtr, dbm, dcm, dalog, ddsk = ssd_bwd(tag + "_ssd_b", sv["xs"], sv["dt"], sv["dtr"], sv["bm"], sv["cm"],
                                                    lw["a_log"], lw["dsk"], sv["hp"], dyssd)
    ddt_pieces = [ddt, dt_cols(ddtr, ddt.shape[1])]
    (dudt,), (ddt_bias,) = rowwise_bwd(tag + "_dt_b", f_softplus, [sv["udt"]], [lw["dt_bias"]], [ddt_pieces],
                                       d_dtypes=[mx])
    (dxs_c, db_c, dc_c), _ = rowwise_bwd(tag + "_silu3_b", f_silu3, [sv["xs_c"], sv["b_c"], sv["c_c"]], [],
                                         [[dxs], [dbm], [dcm]])
    dxs_pre, dcw_xs, dcb_xs = conv_bwd(tag + "_convxs_b", xs_pre, dxs_c, lw["cw_xs"], MXU_DTYPE)
    db_pre, dcw_b, dcb_b = conv_bwd(tag + "_convb_b", b_pre, db_c, lw["cw_b"], MXU_DTYPE)
    dc_pre, dcw_c, dcb_c = conv_bwd(tag + "_convc_b", c_pre, dc_c, lw["cw_c"], MXU_DTYPE)
    (da1,), (dlag, dlab) = rowwise_bwd(tag + "_lna_b", f_ln_silu, [sv["a1"]], [lw["lag"], lw["lab"]], [[dya]])
    da0, dcaw, dcab = conv_bwd(tag + "_conva_b", sv["a0"], da1, lw["caw"])
    (dua, dug), _ = rowwise_bwd(tag + "_glu_b", f_glu, [ua, ug], [], [[da0]], d_dtypes=[mx, mx])
    du = jnp.concatenate([dua, dug, dz, dxs_pre, db_pre, dc_pre], axis=1)
    dx_m = mm(tag + "_dxm", du, lw["win_main"], "nn")
    dx_dt = mm(tag + "_dxdt", dudt, lw["win_dt"], "nn")
    dwin_main = mm(tag + "_dwin", du, xm, "tn", EXCHANGE_DTYPE)
    dwin_dt = mm(tag + "_dwindt", dudt, xm, "tn", EXCHANGE_DTYPE)
    grads = dict(
        e_w_in=jnp.concatenate([dwin_main, dwin_dt[:nh]], axis=0),
        e_conv_a_w=dcaw, e_conv_a_b=dcab[0], e_ln_a_g=dlag[0], e_ln_a_b=dlab[0],
        e_conv_b_w=jnp.concatenate([dcw_xs, dcw_b, dcw_c], axis=1),
        e_conv_b_b=jnp.concatenate([dcb_xs, dcb_b, dcb_c], axis=1)[0],
        e_dt_bias=ddt_bias[0, :nh], e_a_log=dalog.reshape(nh), e_d_skip=ddsk.reshape(nh), e_norm_b_g=dnorm_g[0],
        e_w_out=jnp.concatenate([dwout_a, dwout_b], axis=0), ln_g0=dg0[0], ln_b0=db0[0],
    )
    return [Win(dpre, coef=alpha), dx_m, dx_dt], grads


def odd_fwd(tag, x, xm, w, j, ln_g, ln_b, alpha):
    d = x.shape[1]
    u = mm(tag + "_win", xm, w["o_w_in"][j], "nt")
    bg, cg, v = Win(u, d, 0), Win(u, d, 1), Win(u, d, 2)
    (s,) = rowwise(tag + "_cv", f_mul, [cg, v], [], [d])
    cs = conv_fwd(tag + "_conv", s, w["o_conv_w"][j], None)
    (m,) = rowwise(tag + "_bm", f_mul, [bg, cs], [], [d], out_dtypes=[MXU_DTYPE])
    mix = mm(tag + "_wout", m, w["o_w_out"][j], "nn")

    def f_res(xv, mv, g, b):
        pre = alpha * xv + mv
        y = _ln(pre, g, b)
        return y, y, pre

    x1, x1m, pre = rowwise(tag + "_res", f_res, [x, mix], [ln_g, ln_b], [d] * 3, out_dtypes=[F32, MXU_DTYPE, F32])
    return x1, x1m, dict(xm=xm, u=u, s=s, cs=cs, m=m, pre=pre)


def odd_bwd(tag, dx1_pieces, sv, w, j, ln_g, ln_b, alpha):
    xm, u = sv["xm"], sv["u"]
    d = xm.shape[1]
    mx = (MXU_DTYPE,)
    bg, cg, v = Win(u, d, 0), Win(u, d, 1), Win(u, d, 2)
    (dpre, dprem), (dg0, db0) = rowwise_bwd(tag + "_res_b", f_ln, [sv["pre"]], [ln_g, ln_b], [dx1_pieces],
                                            d_dtypes=[(F32, MXU_DTYPE)])
    dm = mm(tag + "_dm", dprem, w["o_w_out"][j], "nt")
    dwout = mm(tag + "_dwout", sv["m"], dprem, "tn", EXCHANGE_DTYPE)
    (dbg, dcs), _ = rowwise_bwd(tag + "_bm_b", f_mul, [bg, sv["cs"]], [], [[dm]], d_dtypes=[mx, (F32,)])
    ds, dcw, _ = conv_bwd(tag + "_conv_b", sv["s"], dcs, w["o_conv_w"][j])
    (dcg, dv), _ = rowwise_bwd(tag + "_cv_b", f_mul, [cg, v], [], [[ds]], d_dtypes=[mx, mx])
    du = jnp.concatenate([dbg, dcg, dv], axis=1)
    dx_u = mm(tag + "_dx", du, w["o_w_in"][j], "nn")
    dwin = mm(tag + "_dwin", du, xm, "tn", EXCHANGE_DTYPE)
    grads = dict(o_w_in=dwin, o_conv_w=dcw, o_w_out=dwout, ln_g0=dg0[0], ln_b0=db0[0])
    return [Win(dpre, coef=alpha), dx_u], grads


def ffn_fwd(tag, x1, x1m, p_i, w, i, ln_g, ln_b, alpha):
    d = x1.shape[1]
    hpre = mm(tag + "_wup", x1m, w["f_w_up"][i], "nt")
    act = gated_conv_fwd(tag + "_fgate", hpre, w["f_conv_w"][i], w["f_conv_b"][i][None], MXU_DTYPE)
    ffn = mm(tag + "_wdown", act, w["f_w_down"][i], "nn")
    pp = mm(tag + "_pproj", p_i, w["ple_w_proj"][i], "nt")
    gt = mm(tag + "_pgate", x1m, w["ple_w_gate"][i], "nn")

    def f_res2(xv, fv, ppv, gtv, g, b):
        pre = alpha * xv + fv + ppv * _sigmoid(gtv)
        y = _ln(pre, g, b)
        return y, y, pre

    x2, x2m, pre = rowwise(tag + "_res2", f_res2, [x1, ffn, pp, gt], [ln_g, ln_b], [d] * 3,
                           out_dtypes=[F32, MXU_DTYPE, F32])
    return x2, x2m, dict(x1m=x1m, hpre=hpre, act=act, pp=pp, gt=gt, pre=pre)


def ffn_bwd(tag, dx2_pieces, sv, p_i, w, i, ln_g, ln_b, alpha):
    x1m = sv["x1m"]
    mx = (MXU_DTYPE,)
    (dpre, dprem), (dg1, db1) = rowwise_bwd(tag + "_res2_b", f_ln, [sv["pre"]], [ln_g, ln_b], [dx2_pieces],
                                            d_dtypes=[(F32, MXU_DTYPE)])
    (dpp, dgt), _ = rowwise_bwd(tag + "_pg_b", f_gate_mul, [sv["pp"], sv["gt"]], [], [[dpre]], d_dtypes=[mx, mx])
    dwproj = mm(tag + "_dwproj", dpp, p_i, "tn", EXCHANGE_DTYPE)
    dwgate = mm(tag + "_dwgate", x1m, dgt, "tn", EXCHANGE_DTYPE)
    dx1_a = mm(tag + "_dx1a", dgt, w["ple_w_gate"][i], "nt")
    dact = mm(tag + "_dact", dprem, w["f_w_down"][i], "nt")
    dwdown = mm(tag + "_dwdown", sv["act"], dprem, "tn", EXCHANGE_DTYPE)
    dhpre, dfcw, dfcb = gated_conv_bwd(tag + "_fgate_b", sv["hpre"], dact, w["f_conv_w"][i], w["f_conv_b"][i][None],
                                       MXU_DTYPE)
    dwup = mm(tag + "_dwup", dhpre, x1m, "tn", EXCHANGE_DTYPE)
    dx1_b = mm(tag + "_dx1b", dhpre, w["f_w_up"][i], "nn")
    grads = dict(f_w_up=dwup, f_conv_w=dfcw, f_conv_b=dfcb[0], f_w_down=dwdown, ple_w_proj=dwproj, ple_w_gate=dwgate,
                 ln_g1=dg1[0], ln_b1=db1[0])
    return [Win(dpre, coef=alpha), dx1_a, dx1_b], grads


def local_step(x, p, w, target, fetch=None, emit=None):
    depth = w["ln_g"].shape[0]
    alpha = _alpha(depth)
    d = x.shape[1]
    saved = []
    h = hm = x
    for i in range(depth):
        j = i // 2
        if fetch is not None:
            fetch(i, h)
        g0, b0, g1, b1 = w["ln_g"][i, 0][None], w["ln_b"][i, 0][None], w["ln_g"][i, 1][None], w["ln_b"][i, 1][None]
        tag = "l%d" % i
        if i % 2 == 0:
            lw = split_even_weights(w, j)
            h, hm, sv_m = even_fwd(tag, h, hm, lw, g0, b0, alpha)
        else:
            lw = None
            h, hm, sv_m = odd_fwd(tag, h, hm, w, j, g0, b0, alpha)
        h, hm, sv_f = ffn_fwd(tag, h, hm, p[i], w, i, g1, b1, alpha)
        saved.append((lw, sv_m, sv_f, (g0, b0, g1, b1)))

    def f_loss(xf, tg):
        diff = xf - tg
        sq = jnp.sum(jnp.sum(jnp.square(diff), axis=1, keepdims=True), axis=0, keepdims=True)
        return diff * (1.0 / d), jnp.broadcast_to(sq, (1, LANE))

    dxf, sq = rowwise("loss", f_loss, [h, target], [], [d], red_widths=[LANE])
    loss = sq[0, 0] * (0.5 / d)

    per_layer = []
    pieces = [dxf]
    token = None
    for i in reversed(range(depth)):
        j = i // 2
        lw, sv_m, sv_f, (g0, b0, g1, b1) = saved[i]
        tag = "l%d" % i
        pieces, gf = ffn_bwd(tag, pieces, sv_f, p[i], w, i, g1 if token is None else g1 + token, b1, alpha)
        if emit is not None:
            token = emit(tag + "f", i, gf)
        g0 = g0 if token is None else g0 + token
        if i % 2 == 0:
            pieces, gm = even_bwd(tag, pieces, sv_m, lw, g0, b0, alpha)
        else:
            pieces, gm = odd_bwd(tag, pieces, sv_m, w, j, g0, b0, alpha)
        if emit is not None:
            token = emit(tag + "m", j, gm)
        per_layer.append((i, gm, gf))

    def f_sum(*vs):
        acc = None
        for v, c in zip(vs, [pc.coef for pc in map(_win, pieces)]):
            v = v if c == 1.0 else v * c
            acc = v if acc is None else acc + v
        return (acc,)

    (grad_x,) = rowwise("grad_x", f_sum, [Win(_win(pc).arr) for pc in pieces], [], [d])

    by_layer = {i: (gm, gf) for i, gm, gf in per_layer}
    grads = {}
    n_even, n_odd = (depth + 1) // 2, depth // 2
    collect = lambda name, per_layer: per_layer if name in BIG else jnp.stack(per_layer)
    for name in ("e_w_in", "e_conv_a_w", "e_conv_a_b", "e_ln_a_g", "e_ln_a_b", "e_conv_b_w", "e_conv_b_b", "e_dt_bias",
                 "e_a_log", "e_d_skip", "e_norm_b_g", "e_w_out"):
        grads[name] = collect(name, [by_layer[2 * j][0][name] for j in range(n_even)])
    for name in ("o_w_in", "o_conv_w", "o_w_out"):
        grads[name] = collect(name, [by_layer[2 * j + 1][0][name] for j in range(n_odd)])
    for name in ("f_w_up", "f_conv_w", "f_conv_b", "f_w_down", "ple_w_proj", "ple_w_gate"):
        grads[name] = collect(name, [by_layer[i][1][name] for i in range(depth)])
    grads["ln_g"] = jnp.stack([jnp.stack([by_layer[i][0]["ln_g0"], by_layer[i][1]["ln_g1"]]) for i in range(depth)])
    grads["ln_b"] = jnp.stack([jnp.stack([by_layer[i][0]["ln_b0"], by_layer[i][1]["ln_b1"]]) for i in range(depth)])
    return loss, grad_x, grads


_ANY = pl.BlockSpec(memory_space=pl.ANY)
_MESH = pl.DeviceIdType.MESH


def all_gather(name, xl):
    r, c_ = xl.shape
    split = r // 2 // 16 * 16
    halves = ((0, split), (split, r - split)) if split else ((0, r),)
    two = len(halves) == 2

    def body(x_ref, out_ref, send_sems, recv_sems, local_sem):
        x, y, c = lax.axis_index("x"), lax.axis_index("y"), lax.axis_index("c")
        me, sibling, xn, yn, dg = (x, y, c), (x, y, 1 - c), (1 - x, y, c), (x, 1 - y, c), (1 - x, 1 - y, c)

        def rows(block, h):
            ref = out_ref.at[4 * block[0] + 2 * block[1] + block[2]]
            return ref if h is None else ref.at[pl.ds(*halves[h])]

        def copy(k, block, h, to, own=False):
            src = (x_ref if h is None else x_ref.at[pl.ds(*halves[h])]) if own else rows(block, h)
            return pltpu.make_async_remote_copy(src_ref=src, dst_ref=rows(block, h), send_sem=send_sems.at[k],
                                                recv_sem=recv_sems.at[k], device_id=to, device_id_type=_MESH)

        def other_core(block):
            return (block[0], block[1], 1 - c)

        mine = pltpu.make_async_copy(x_ref, rows(me, None), local_sem)
        mine.start()
        direct = [copy(0, me, 0, xn, own=True), copy(1, me, 1 if two else 0, yn, own=True)]
        if two:
            direct += [copy(2, me, 1, xn, own=True), copy(3, me, 0, yn, own=True)]
        direct.append(copy(6, me, None, sibling, own=True))
        for cp in direct:
            cp.start()
        started = list(direct)

        def then(waits, nxt):
            for cp in waits:
                cp.wait_recv()
            for cp in nxt:
                cp.start()
            started.extend(nxt)

        if two:
            then([copy(0, xn, 0, me)], [copy(4, xn, 0, yn)])
            then([copy(1, yn, 1, me)], [copy(5, yn, 1, xn)])
            then([copy(2, xn, 1, me)], [copy(7, xn, None, sibling)])
            then([copy(3, yn, 0, me)], [copy(8, yn, None, sibling)])
            then([copy(4, dg, 0, me), copy(5, dg, 1, me)], [copy(9, dg, None, sibling)])
        else:
            then([copy(0, xn, 0, me)], [copy(4, xn, 0, yn), copy(7, xn, None, sibling)])
            then([copy(1, yn, 0, me)], [copy(8, yn, None, sibling)])
            then([copy(4, dg, 0, me)], [copy(9, dg, None, sibling)])
        for k, block in ((6, me), (7, xn), (8, yn), (9, dg)):
            copy(k, other_core(block), None, me).wait_recv()
        for cp in started:
            cp.wait_send()
        mine.wait()

    return pl.pallas_call(
        body, name=name, out_shape=S((N_DEV, r, c_), xl.dtype), in_specs=[_ANY], out_specs=_ANY,
        scratch_shapes=[pltpu.SemaphoreType.DMA((10,)), pltpu.SemaphoreType.DMA((10,)), pltpu.SemaphoreType.DMA],
    )(xl)


def sum_devices(name, g8):
    _, r, c_ = g8.shape

    def body(g_ref, o_ref):
        acc = g_ref[0]
        for k in range(1, N_DEV):
            acc = acc + g_ref[k]
        o_ref[...] = acc

    return pl.pallas_call(body, name=name, out_shape=S((r, c_), F32))(g8)


def _flatten(parts, cols, row_mult):
    flat = jnp.concatenate([v.reshape(-1) for v in parts])
    n = flat.shape[0]
    rows = -(-n // (cols * row_mult)) * row_mult
    return jnp.pad(flat, (0, rows * cols - n)).reshape(rows, cols)


def _exchange_dims(name, lshape):
    l, r, c = lshape
    return (l, c, r) if name in TRANSPOSED else (l, r, c)


_HBM = pl.BlockSpec(memory_space=pltpu.HBM)
_SEM = pl.BlockSpec(memory_space=pltpu.SEMAPHORE)
_PEERS = [(dx, dy, dc) for dx in (0, 1) for dy in (0, 1) for dc in (0, 1)][1:]


def _peer_copies(scatter, src_refs, land_refs, send_sems, recv_sems):
    x, y, c = lax.axis_index("x"), lax.axis_index("y"), lax.axis_index("c")
    flip = lambda v, d: 1 - v if d else v
    copies = []
    for s_ref, land_ref in zip(src_refs, land_refs):
        for dx, dy, dc in _PEERS:
            px, py, pc = flip(x, dx), flip(y, dy), flip(c, dc)
            k = len(copies)
            copies.append(pltpu.make_async_remote_copy(
                src_ref=s_ref.at[4 * px + 2 * py + pc] if scatter else s_ref, dst_ref=land_ref.at[4 * x + 2 * y + c],
                send_sem=send_sems.at[k], recv_sem=recv_sems.at[k], device_id=(px, py, pc), device_id_type=_MESH))
    return copies


def split_start(name, srcs, scatter):
    n = len(srcs)
    lands = [lax.empty(s.shape if scatter else (N_DEV,) + s.shape, s.dtype) for s in srcs]
    n_copies = n * len(_PEERS)

    def body(*refs):
        send_sems, recv_sems, token = refs[2 * n], refs[2 * n + 1], refs[-1]
        for cp in _peer_copies(scatter, refs[:n], refs[n:2 * n], send_sems, recv_sems):
            cp.start()
        token[...] = jnp.zeros_like(token)

    return pl.pallas_call(
        body, name=name,
        out_shape=(pltpu.SemaphoreType.DMA((n_copies,)), pltpu.SemaphoreType.DMA((n_copies,)),
                   *[pltpu.HBM(v.shape, v.dtype) for v in srcs + lands], S((SUBLANE, LANE), F32)),
        in_specs=(_HBM,) * (2 * n), out_specs=(_SEM, _SEM) + (_HBM,) * (2 * n) + (pl.BlockSpec(memory_space=pltpu.VMEM),),
        input_output_aliases={i: 2 + i for i in range(2 * n)},
        compiler_params=pltpu.CompilerParams(has_side_effects=pltpu.SideEffectType.DATAFLOW_SIDE_EFFECTING),
    )(*[pltpu.with_memory_space_constraint(v, pltpu.HBM) for v in srcs + lands])


def split_wait(name, started, scatter, after):
    send_sems, recv_sems = started[0], started[1]
    thru = list(started[2:-1])
    n = len(thru) // 2

    def body(*refs):
        send_sems, recv_sems = refs[2 * n], refs[2 * n + 1]
        for cp in _peer_copies(scatter, refs[:n], refs[n:2 * n], send_sems, recv_sems):
            cp.wait_send()
            cp.wait_recv()

    outs = pl.pallas_call(
        body, name=name, out_shape=tuple(pltpu.HBM(v.shape, v.dtype) for v in thru),
        in_specs=(_HBM,) * (2 * n) + (_SEM, _SEM, _ANY), out_specs=(_HBM,) * (2 * n),
        input_output_aliases={i: i for i in range(2 * n)},
        compiler_params=pltpu.CompilerParams(has_side_effects=pltpu.SideEffectType.DATAFLOW_SIDE_EFFECTING),
    )(*thru, send_sems, recv_sems, after)
    return outs[:n], outs[n:]


def _layer_weights(depth):
    table = []
    for i in range(depth):
        mixer = ("e_w_out", "e_w_in") if i % 2 == 0 else ("o_w_out", "o_w_in")
        table.append([("f_w_up", i), ("f_w_down", i), ("ple_w_gate", i), ("ple_w_proj", i)]
                     + [(n, i // 2) for n in mixer])
    return table


def gather_big(local):
    cols = local["e_w_out"].shape[2]
    depth = local["ln_g"].shape[0]
    table = _layer_weights(depth)
    shard = {}
    for n in BIG:
        v = local[n].astype(MXU_DTYPE)
        shard[n] = v.transpose(0, 2, 1) if n in TRANSPOSED else v
    full = {n: [None] * local[n].shape[0] for n in BIG}
    dev = 4 * lax.axis_index("x") + 2 * lax.axis_index("y") + lax.axis_index("c")

    got, r0 = all_gather("ag_l0", jnp.concatenate([shard[n][li].reshape(-1, cols) for n, li in table[0]], axis=0)), 0
    for n, li in table[0]:
        _, a, b = shard[n].shape
        per = a * b // cols
        full[n][li] = got[:, r0:r0 + per].reshape(N_DEV * a, b)
        r0 += per
    started = [split_start("ag_start_l%d" % i, [shard[n][li] for n, li in table[i]], False) for i in range(1, depth)]

    def fetch(i, after):
        if i > 0:
            srcs, lands = split_wait("ag_wait_l%d" % i, started[i - 1], False, after)
            for (n, li), src, land in zip(table[i], srcs, lands):
                _, a, b = shard[n].shape
                full[n][li] = lax.dynamic_update_slice(land, src[None], (dev, 0, 0)).reshape(N_DEV * a, b)

    token = sum(st[-1][0, 0] for st in started) if started else 0.0
    return full, fetch, token


def sum_landed(name, g, land, me):
    _, a, b = g.shape
    tb = b // 2 if b % (2 * LANE) == 0 else b

    def body(me_ref, g_ref, land_ref, o_ref):
        acc = jnp.zeros(o_ref.shape, F32)
        for k in range(N_DEV):
            acc = acc + jnp.where(me_ref[0] == k, g_ref[0], land_ref[k]).astype(F32)
        o_ref[...] = acc

    return pl.pallas_call(
        body, name=name, out_shape=S((a, b), F32),
        grid_spec=pltpu.PrefetchScalarGridSpec(
            num_scalar_prefetch=1, grid=(b // tb,),
            in_specs=[pl.BlockSpec((1, a, tb), lambda j, me_: (me_[0], 0, j)),
                      pl.BlockSpec((N_DEV, a, tb), lambda j, me_: (0, 0, j))],
            out_specs=pl.BlockSpec((a, tb), lambda j, me_: (0, j))),
    )(me, g, land)


class GradScatter:
    def __init__(self, local_shapes):
        self.local_shapes = local_shapes
        self.pending = []

    def emit(self, tag, index, grads):
        names = [n for n in BIG if n in grads]
        gs = [grads[n].reshape(N_DEV, -1, grads[n].shape[1]) for n in names]
        started = split_start("rs_start_" + tag, gs, True)
        self.pending.append((tag, index, names, started))
        return started[-1][0:1, 0:1]

    def finish(self, after):
        me = (4 * lax.axis_index("x") + 2 * lax.axis_index("y") + lax.axis_index("c")).astype(jnp.int32).reshape(1)
        shards = {n: [None] * self.local_shapes[n][0] for n in BIG}
        for tag, index, names, started in self.pending:
            gs, lands = split_wait("rs_wait_" + tag, started, True, after)
            for n, g, land in zip(names, gs, lands):
                shards[n][index] = sum_landed("rs_sum_%s_%s" % (tag, n), g, land, me)
        out = {}
        for n in BIG:
            seg = jnp.stack(shards[n])
            out[n] = seg.transpose(0, 2, 1) if n in TRANSPOSED else seg
        return out


def gather_small(name, local, names):
    flat = _flatten([local[n] for n in names], LANE, 1)
    got = all_gather(name, flat).reshape(N_DEV, -1)
    full, off = {}, 0
    for n in names:
        size = math.prod(local[n].shape)
        seg = got[:, off:off + size].reshape((N_DEV,) + local[n].shape)
        full[n] = seg.transpose(1, 2, 0, 3).reshape(seg.shape[1], seg.shape[2], -1)
        off += size
    return full


def all_reduce_small(grads, names):
    flat = _flatten([grads[n] for n in names], LANE, SUBLANE)
    total = sum_devices("ar_sum", all_gather("ar_gather", flat)).reshape(-1)
    out, off = {}, 0
    for nm in names:
        size = math.prod(grads[nm].shape)
        out[nm] = total[off:off + size].reshape(grads[nm].shape)
        off += size
    return out


def adamw(name, w, g, m, v):
    shape = w.shape
    cols = shape[-1]
    rows = math.prod(shape[:-1])
    tr = _pick(rows, (256, 128, 64, 32, 16, 8)) if rows * cols > 256 * 1024 else rows
    c1 = 1.0 - ADAM_B1 ** ADAM_STEP
    c2 = 1.0 - ADAM_B2 ** ADAM_STEP

    def body(w_ref, g_ref, m_ref, v_ref, d_ref, nm_ref, nv_ref):
        gv = g_ref[...]
        m2 = ADAM_B1 * m_ref[...] + (1.0 - ADAM_B1) * gv
        v2 = ADAM_B2 * v_ref[...] + (1.0 - ADAM_B2) * jnp.square(gv)
        d_ref[...] = -ADAM_LR * ((m2 / c1) / (jnp.sqrt(v2 / c2) + ADAM_EPS) + ADAM_WD * w_ref[...])
        nm_ref[...] = m2
        nv_ref[...] = v2

    spec = pl.BlockSpec((tr, cols), lambda i: (i, 0))
    outs = pl.pallas_call(
        body, name=name, grid=(rows // tr,), in_specs=[spec] * 4, out_specs=[spec] * 3,
        out_shape=[S((rows, cols), F32)] * 3,
        compiler_params=pltpu.CompilerParams(dimension_semantics=("parallel",)),
    )(*[a.reshape(rows, cols) for a in (w, g, m, v)])
    return tuple(o.reshape(shape) for o in outs)


def kernel(x, p, e_w_in, e_conv_a_w, e_conv_a_b, e_ln_a_g, e_ln_a_b, e_conv_b_w, e_conv_b_b, e_dt_bias, e_a_log, e_d_skip, e_norm_b_g, e_w_out, o_w_in, o_conv_w, o_w_out, f_w_up, f_conv_w, f_conv_b, f_w_down, ple_w_proj, ple_w_gate, ln_g, ln_b, loss_target, m_e_w_in, m_e_conv_a_w, m_e_conv_a_b, m_e_ln_a_g, m_e_ln_a_b, m_e_conv_b_w, m_e_conv_b_b, m_e_dt_bias, m_e_a_log, m_e_d_skip, m_e_norm_b_g, m_e_w_out, m_o_w_in, m_o_conv_w, m_o_w_out, m_f_w_up, m_f_conv_w, m_f_conv_b, m_f_w_down, m_ple_w_proj, m_ple_w_gate, m_ln_g, m_ln_b, v_e_w_in, v_e_conv_a_w, v_e_conv_a_b, v_e_ln_a_g, v_e_ln_a_b, v_e_conv_b_w, v_e_conv_b_b, v_e_dt_bias, v_e_a_log, v_e_d_skip, v_e_norm_b_g, v_e_w_out, v_o_w_in, v_o_conv_w, v_o_w_out, v_f_w_up, v_f_conv_w, v_f_conv_b, v_f_w_down, v_ple_w_proj, v_ple_w_gate, v_ln_g, v_ln_b):
    args = locals()
    local = {n: args[n] for n in WEIGHTS}
    mom = {n: args["m_" + n] for n in WEIGHTS}
    var = {n: args["v_" + n] for n in WEIGHTS}

    full = {n: local[n] for n in REPLICATED}
    big, fetch, token = gather_big(local)
    full.update(big)
    full.update(gather_small("ag_small", local, SMALL_SHARDED))

    scatter = GradScatter({n: local[n].shape for n in BIG})
    loss_local, grad_x, grads = local_step(x[0] + token, p[:, 0], full, loss_target[0], fetch, scatter.emit)
    loss = lax.psum(loss_local, MESH_AXES)

    g_local = scatter.finish(grad_x)
    small = all_reduce_small(grads, REPLICATED + SMALL_SHARDED)
    dev = 4 * lax.axis_index("x") + 2 * lax.axis_index("y") + lax.axis_index("c")
    for n in REPLICATED:
        g_local[n] = small[n]
    for n in SMALL_SHARDED:
        width = local[n].shape[2]
        g_local[n] = lax.dynamic_slice_in_dim(small[n], dev * width, width, axis=2)

    delta, new_m, new_v = {}, {}, {}
    for n in WEIGHTS:
        delta[n], new_m[n], new_v[n] = adamw("adamw_" + n, local[n], g_local[n], mom[n], var[n])
    return (loss, grad_x[None], *[g_local[n] for n in WEIGHTS], *[delta[n] for n in WEIGHTS],
            *[new_m[n] for n in WEIGHTS], *[new_v[n] for n in WEIGHTS])
```

```python
import functools
import math

import jax
import jax.numpy as jnp
from jax import lax
from jax.experimental import pallas as pl
from jax.experimental.pallas import tpu as pltpu

F32 = jnp.float32
MXU_DTYPE = jnp.bfloat16
MESH_AXES = ("x", "y", "c")
N_DEV = 8
LANE = 128
SUBLANE = 8
ROWWISE_VMEM_BUDGET = 20 * 1024 * 1024
MM_TILES = (1408, 1024, 512, 256, 128)
EXCHANGE_DTYPE = jnp.bfloat16
LN_EPS = 1e-5
CHUNK = 64
HEAD_DIM = 64
N_GROUPS = 4
N_STATE = 128
CONV_PAD = 32
CONV_ROWS = 256
ADAM_LR, ADAM_B1, ADAM_B2, ADAM_EPS, ADAM_WD, ADAM_STEP = 0.001, 0.9, 0.999, 1e-08, 0.01, 10

BIG = ("e_w_in", "e_w_out", "o_w_in", "o_w_out", "f_w_up", "f_w_down", "ple_w_proj", "ple_w_gate")
SMALL_SHARDED = ("e_conv_a_w", "e_conv_b_w", "o_conv_w", "f_conv_w", "ln_g", "ln_b")
REPLICATED = ("e_conv_a_b", "e_ln_a_g", "e_ln_a_b", "e_conv_b_b", "e_dt_bias", "e_a_log", "e_d_skip",
              "e_norm_b_g", "f_conv_b")
TRANSPOSED = ("e_w_in", "o_w_in", "f_w_up", "ple_w_proj")
WEIGHTS = ("e_w_in", "e_conv_a_w", "e_conv_a_b", "e_ln_a_g", "e_ln_a_b", "e_conv_b_w", "e_conv_b_b", "e_dt_bias",
           "e_a_log", "e_d_skip", "e_norm_b_g", "e_w_out", "o_w_in", "o_conv_w", "o_w_out", "f_w_up", "f_conv_w",
           "f_conv_b", "f_w_down", "ple_w_proj", "ple_w_gate", "ln_g", "ln_b")

S = jax.ShapeDtypeStruct


class Win:
    def __init__(self, arr, w=None, idx=0, coef=1.0):
        self.arr, self.w, self.idx, self.coef = arr, (arr.shape[1] if w is None else w), idx, coef


def _win(a):
    return a if isinstance(a, Win) else Win(a)


def _pick(n, prefs):
    for p in prefs:
        if p <= n and n % p == 0:
            return p
    return n


_MM_DIMS = {"nn": (1, 0), "nt": (1, 1), "tn": (0, 0)}


def mm(name, a, b, mode, out_dtype=F32):
    ca, cb = _MM_DIMS[mode]
    kdim = a.shape[ca]
    m = a.shape[1 - ca]
    n = b.shape[1 - cb]
    assert b.shape[cb] == kdim, (name, a.shape, b.shape, mode)
    tm = _pick(m, MM_TILES)
    tn = _pick(n, MM_TILES)
    tk = kdim if kdim <= MM_TILES[0] else _pick(kdim, MM_TILES)
    nk = kdim // tk
    own_acc = nk > 1 and out_dtype != F32

    def body(a_ref, b_ref, o_ref, *scratch):
        acc_ref = scratch[0] if own_acc else o_ref
        d = lax.dot_general(a_ref[...].astype(MXU_DTYPE), b_ref[...].astype(MXU_DTYPE),
                            (((ca,), (cb,)), ((), ())), preferred_element_type=F32)
        if nk == 1:
            o_ref[...] = d.astype(o_ref.dtype)
        else:
            k = pl.program_id(2)

            @pl.when(k == 0)
            def _():
                acc_ref[...] = d

            @pl.when(k > 0)
            def _():
                acc_ref[...] += d

            if own_acc:
                @pl.when(k == nk - 1)
                def _():
                    o_ref[...] = acc_ref[...].astype(o_ref.dtype)

    a_spec = pl.BlockSpec((tm, tk), lambda i, j, k: (i, k)) if ca == 1 else pl.BlockSpec((tk, tm), lambda i, j, k: (k, i))
    b_spec = pl.BlockSpec((tk, tn), lambda i, j, k: (k, j)) if cb == 0 else pl.BlockSpec((tn, tk), lambda i, j, k: (j, k))
    return pl.pallas_call(
        body, name=name, grid=(m // tm, n // tn, nk),
        in_specs=[a_spec, b_spec], out_specs=pl.BlockSpec((tm, tn), lambda i, j, k: (i, j)),
        out_shape=S((m, n), out_dtype), scratch_shapes=[pltpu.VMEM((tm, tn), F32)] if own_acc else [],
        compiler_params=pltpu.CompilerParams(dimension_semantics=("parallel", "parallel", "arbitrary")),
    )(a, b)


def _row_block(t, widths):
    tb = 512
    while tb > SUBLANE and (t % tb or tb * sum(widths) * 8 > ROWWISE_VMEM_BUDGET):
        tb //= 2
    return tb


def _tok_spec(tb, w):
    return pl.BlockSpec((tb, w.w), functools.partial(lambda i, idx: (i, idx), idx=w.idx))


def _par_spec(p):
    return pl.BlockSpec((1, p.shape[1]), lambda i: (0, 0))


def rowwise(name, fn, tok, par, out_widths, red_widths=(), out_dtypes=None):
    tok = [_win(t) for t in tok]
    t = tok[0].arr.shape[0]
    tb = _row_block(t, [w.w for w in tok] + list(out_widths))
    n_tok, n_par, n_out = len(tok), len(par), len(out_widths)
    out_dtypes = [F32] * n_out if out_dtypes is None else out_dtypes

    def body(*refs):
        ins = [r[...] for r in refs[:n_tok + n_par]]
        res = fn(*ins)
        out_refs = refs[n_tok + n_par:n_tok + n_par + n_out]
        red_refs = refs[n_tok + n_par + n_out:]
        for r, v in zip(out_refs, res[:n_out]):
            r[...] = v.astype(r.dtype)
        if red_refs:
            @pl.when(pl.program_id(0) == 0)
            def _():
                for r in red_refs:
                    r[...] = jnp.zeros_like(r)
            for r, v in zip(red_refs, res[n_out:]):
                r[...] += v

    outs = pl.pallas_call(
        body, name=name, grid=(t // tb,),
        in_specs=[_tok_spec(tb, w) for w in tok] + [_par_spec(p) for p in par],
        out_specs=[pl.BlockSpec((tb, w), lambda i: (i, 0)) for w in out_widths]
        + [pl.BlockSpec((1, w), lambda i: (0, 0)) for w in red_widths],
        out_shape=[S((t, w), dt) for w, dt in zip(out_widths, out_dtypes)] + [S((1, w), F32) for w in red_widths],
        compiler_params=pltpu.CompilerParams(dimension_semantics=("arbitrary",)),
    )(*[w.arr for w in tok], *par)
    return outs


def rowwise_bwd(name, fn, tok, par, cts, d_dtypes=None):
    tok = [_win(t) for t in tok]
    cts = [[_win(c) for c in group] for group in cts]
    d_dtypes = [(F32,)] * len(tok) if d_dtypes is None else d_dtypes
    t = tok[0].arr.shape[0]
    flat_cts = [c for group in cts for c in group]
    d_outs = [(i, w.w, dt) for i, (w, dts) in enumerate(zip(tok, d_dtypes)) for dt in dts]
    tb = _row_block(t, [w.w for w in tok] + [c.w for c in flat_cts] + [w for _, w, _ in d_outs])
    n_tok, n_par, n_ct, n_d = len(tok), len(par), len(flat_cts), len(d_outs)

    def body(*refs):
        tok_vals = [r[...] for r in refs[:n_tok]]
        par_vals = [r[...] for r in refs[n_tok:n_tok + n_par]]
        ct_refs = refs[n_tok + n_par:n_tok + n_par + n_ct]
        d_refs = refs[n_tok + n_par + n_ct:n_tok + n_par + n_ct + n_d]
        dp_refs = refs[n_tok + n_par + n_ct + n_d:]
        ct_vals, pos = [], 0
        for group in cts:
            acc = None
            for c in group:
                v = ct_refs[pos][...]
                if c.coef != 1.0:
                    v = v * c.coef
                acc = v if acc is None else acc + v
                pos += 1
            ct_vals.append(acc)
        _, vjp = jax.vjp(lambda *a: tuple(fn(*a)), *tok_vals, *par_vals)
        grads = vjp(tuple(ct_vals))
        for r, (i, _, _) in zip(d_refs, d_outs):
            r[...] = grads[i].astype(r.dtype)
        if dp_refs:
            @pl.when(pl.program_id(0) == 0)
            def _():
                for r in dp_refs:
                    r[...] = jnp.zeros_like(r)
            for r, v in zip(dp_refs, grads[n_tok:]):
                r[...] += v

    outs = pl.pallas_call(
        body, name=name, grid=(t // tb,),
        in_specs=[_tok_spec(tb, w) for w in tok] + [_par_spec(p) for p in par] + [_tok_spec(tb, c) for c in flat_cts],
        out_specs=[pl.BlockSpec((tb, w), lambda i: (i, 0)) for _, w, _ in d_outs] + [_par_spec(p) for p in par],
        out_shape=[S((t, w), dt) for _, w, dt in d_outs] + [S(p.shape, F32) for p in par],
        compiler_params=pltpu.CompilerParams(dimension_semantics=("arbitrary",)),
    )(*[w.arr for w in tok], *par, *[c.arr for c in flat_cts])
    return outs[:n_d], outs[n_d:]


def _sigmoid(x):
    return 1.0 / (1.0 + jnp.exp(-x))


def _silu(x):
    return x * _sigmoid(x)


def _ln(x, g, b):
    mu = jnp.mean(x, axis=-1, keepdims=True)
    var = jnp.mean(jnp.square(x - mu), axis=-1, keepdims=True)
    return (x - mu) * lax.rsqrt(var + LN_EPS) * g + b


def f_glu(ua, ug):
    return (ua * _sigmoid(ug),)


def f_ln_silu(a1, g, b):
    return (_silu(_ln(a1, g, b)),)


def f_silu3(a, b, c):
    return (_silu(a), _silu(b), _silu(c))


def f_softplus(dt_raw, bias):
    return (jax.nn.softplus(dt_raw + bias),)


def f_gate_rms(yssd, z, g):
    y = yssd * _silu(z)
    return (y * lax.rsqrt(jnp.mean(jnp.square(y), axis=-1, keepdims=True) + LN_EPS) * g,)


def f_ln(pre, g, b):
    return (_ln(pre, g, b),)


def f_mul(a, b):
    return (a * b,)


def f_gate_mul(pp, gt):
    return (pp * _sigmoid(gt),)


def conv_fwd(name, x, w, b):
    x = _win(x)
    t, c = x.arr.shape[0], x.w
    kw = w.shape[0]
    cb = LANE
    off = x.idx * (c // cb)
    rows = min(CONV_ROWS, t)
    has_b = b is not None

    def body(*refs):
        if has_b:
            x_ref, w_ref, b_ref, y_ref, xp_ref = refs
        else:
            x_ref, w_ref, y_ref, xp_ref = refs
        xp_ref[0:CONV_PAD, :] = jnp.zeros((CONV_PAD, cb), F32)
        xp_ref[CONV_PAD:CONV_PAD + t, :] = x_ref[...]

        def step(s, carry):
            base = pl.multiple_of(s * rows, rows)
            acc = jnp.zeros((rows, cb), F32)
            if has_b:
                acc = acc + b_ref[...]
            for k in range(kw):
                acc = acc + w_ref[k:k + 1, :] * xp_ref[pl.ds(base + CONV_PAD - (kw - 1) + k, rows), :]
            y_ref[pl.ds(base, rows), :] = acc
            return carry

        lax.fori_loop(0, t // rows, step, 0)

    in_specs = [pl.BlockSpec((t, cb), lambda j: (0, off + j)), pl.BlockSpec((kw, cb), lambda j: (0, j))]
    args = [x.arr, w]
    if has_b:
        in_specs.append(pl.BlockSpec((1, cb), lambda j: (0, j)))
        args.append(b)
    return pl.pallas_call(
        body, name=name, grid=(c // cb,), in_specs=in_specs,
        out_specs=pl.BlockSpec((t, cb), lambda j: (0, j)), out_shape=S((t, c), F32),
        scratch_shapes=[pltpu.VMEM((CONV_PAD + t, cb), F32)],
        compiler_params=pltpu.CompilerParams(dimension_semantics=("parallel",)),
    )(*args)


def conv_bwd(name, x, dy, w, dx_dtype=F32):
    x, dy = _win(x), _win(dy)
    t, c = x.arr.shape[0], x.w
    kw = w.shape[0]
    cb = LANE
    xoff = x.idx * (c // cb)
    dyoff = dy.idx * (c // cb)
    rows = min(CONV_ROWS, t)

    def body(x_ref, dy_ref, w_ref, dx_ref, dw_ref, db_ref, xp_ref, dyp_ref):
        xp_ref[0:CONV_PAD, :] = jnp.zeros((CONV_PAD, cb), F32)
        xp_ref[CONV_PAD:CONV_PAD + t, :] = x_ref[...]
        dyp_ref[0:t, :] = dy_ref[...]
        dyp_ref[t:t + CONV_PAD, :] = jnp.zeros((CONV_PAD, cb), F32)

        def fold(v):
            return jnp.sum(v.reshape(rows // SUBLANE, SUBLANE, cb), axis=0)

        def step(s, carry):
            base = pl.multiple_of(s * rows, rows)
            dyc = dy_ref[pl.ds(base, rows), :]
            acc = jnp.zeros((rows, cb), F32)
            new = []
            for k in range(kw):
                acc = acc + w_ref[k:k + 1, :] * dyp_ref[pl.ds(base + (kw - 1) - k, rows), :]
                new.append(carry[k] + fold(dyc * xp_ref[pl.ds(base + CONV_PAD - (kw - 1) + k, rows), :]))
            new.append(carry[kw] + fold(dyc))
            dx_ref[pl.ds(base, rows), :] = acc.astype(dx_ref.dtype)
            return tuple(new)

        init = tuple(jnp.zeros((SUBLANE, cb), F32) for _ in range(kw + 1))
        parts = lax.fori_loop(0, t // rows, step, init)
        for k in range(kw):
            dw_ref[k:k + 1, :] = jnp.sum(parts[k], axis=0, keepdims=True)
        db_ref[...] = jnp.sum(parts[kw], axis=0, keepdims=True)

    return pl.pallas_call(
        body, name=name, grid=(c // cb,),
        in_specs=[pl.BlockSpec((t, cb), lambda j: (0, xoff + j)), pl.BlockSpec((t, cb), lambda j: (0, dyoff + j)),
                  pl.BlockSpec((kw, cb), lambda j: (0, j))],
        out_specs=[pl.BlockSpec((t, cb), lambda j: (0, j)), pl.BlockSpec((kw, cb), lambda j: (0, j)),
                   pl.BlockSpec((1, cb), lambda j: (0, j))],
        out_shape=[S((t, c), dx_dtype), S((kw, c), F32), S((1, c), F32)],
        scratch_shapes=[pltpu.VMEM((CONV_PAD + t, cb), F32), pltpu.VMEM((CONV_PAD + t, cb), F32)],
        compiler_params=pltpu.CompilerParams(dimension_semantics=("parallel",)),
    )(x.arr, dy.arr, w)


def gated_conv_fwd(name, hpre, w, b, out_dtype):
    t, c2 = hpre.shape
    ff = c2 // 2
    kw = w.shape[0]
    cb = LANE
    nb = ff // cb
    rows = min(CONV_ROWS, t)

    def body(h1_ref, h2_ref, w1_ref, w2_ref, b1_ref, b2_ref, y_ref, xp1_ref, xp2_ref):
        for xp_ref, h_ref in ((xp1_ref, h1_ref), (xp2_ref, h2_ref)):
            xp_ref[0:CONV_PAD, :] = jnp.zeros((CONV_PAD, cb), F32)
            xp_ref[CONV_PAD:CONV_PAD + t, :] = h_ref[...]

        def step(s, carry):
            base = pl.multiple_of(s * rows, rows)
            h1 = jnp.zeros((rows, cb), F32) + b1_ref[...]
            h2 = jnp.zeros((rows, cb), F32) + b2_ref[...]
            for k in range(kw):
                at = pl.ds(base + CONV_PAD - (kw - 1) + k, rows)
                h1 = h1 + w1_ref[k:k + 1, :] * xp1_ref[at, :]
                h2 = h2 + w2_ref[k:k + 1, :] * xp2_ref[at, :]
            y_ref[pl.ds(base, rows), :] = (_silu(h1) * h2).astype(y_ref.dtype)
            return carry

        lax.fori_loop(0, t // rows, step, 0)

    col1 = lambda r: pl.BlockSpec((r, cb), lambda j: (0, j))
    col2 = lambda r: pl.BlockSpec((r, cb), lambda j: (0, nb + j))
    return pl.pallas_call(
        body, name=name, grid=(nb,),
        in_specs=[col1(t), col2(t), col1(kw), col2(kw), col1(1), col2(1)],
        out_specs=col1(t), out_shape=S((t, ff), out_dtype),
        scratch_shapes=[pltpu.VMEM((CONV_PAD + t, cb), F32)] * 2,
        compiler_params=pltpu.CompilerParams(dimension_semantics=("parallel",)),
    )(hpre, hpre, w, w, b, b)


def gated_conv_bwd(name, hpre, dact, w, b, dx_dtype):
    t, c2 = hpre.shape
    ff = c2 // 2
    kw = w.shape[0]
    cb = LANE
    nb = ff // cb
    rows = min(CONV_ROWS, t)

    def body(own_ref, oth_ref, da_ref, wo_ref, wt_ref, bo_ref, bt_ref, dx_ref, dw_ref, db_ref,
             xpo_ref, xpt_ref, dhp_ref):
        for xp_ref, h_ref in ((xpo_ref, own_ref), (xpt_ref, oth_ref)):
            xp_ref[0:CONV_PAD, :] = jnp.zeros((CONV_PAD, cb), F32)
            xp_ref[CONV_PAD:CONV_PAD + t, :] = h_ref[...]
        dhp_ref[t:t + CONV_PAD, :] = jnp.zeros((CONV_PAD, cb), F32)

        def fold(v):
            return jnp.sum(v.reshape(rows // SUBLANE, SUBLANE, cb), axis=0)

        def first_pass(own_is_gate):
            def step(s, carry):
                base = pl.multiple_of(s * rows, rows)
                ho = jnp.zeros((rows, cb), F32) + bo_ref[...]
                ht = jnp.zeros((rows, cb), F32) + bt_ref[...]
                for k in range(kw):
                    at = pl.ds(base + CONV_PAD - (kw - 1) + k, rows)
                    ho = ho + wo_ref[k:k + 1, :] * xpo_ref[at, :]
                    ht = ht + wt_ref[k:k + 1, :] * xpt_ref[at, :]
                da = da_ref[pl.ds(base, rows), :]
                if own_is_gate:
                    sg = _sigmoid(ho)
                    dh = da * ht * (sg * (1.0 + ho * (1.0 - sg)))
                else:
                    dh = da * _silu(ht)
                dhp_ref[pl.ds(base, rows), :] = dh
                new = [carry[k] + fold(dh * xpo_ref[pl.ds(base + CONV_PAD - (kw - 1) + k, rows), :]) for k in range(kw)]
                new.append(carry[kw] + fold(dh))
                return tuple(new)

            init = tuple(jnp.zeros((SUBLANE, cb), F32) for _ in range(kw + 1))
            parts = lax.fori_loop(0, t // rows, step, init)
            for k in range(kw):
                dw_ref[k:k + 1, :] = jnp.sum(parts[k], axis=0, keepdims=True)
            db_ref[...] = jnp.sum(parts[kw], axis=0, keepdims=True)

        half = pl.program_id(0)

        @pl.when(half == 0)
        def _():
            first_pass(True)

        @pl.when(half == 1)
        def _():
            first_pass(False)

        def second(s, carry):
            base = pl.multiple_of(s * rows, rows)
            acc = jnp.zeros((rows, cb), F32)
            for k in range(kw):
                acc = acc + wo_ref[k:k + 1, :] * dhp_ref[pl.ds(base + (kw - 1) - k, rows), :]
            dx_ref[pl.ds(base, rows), :] = acc.astype(dx_ref.dtype)
            return carry

        lax.fori_loop(0, t // rows, second, 0)

    own = lambda r: pl.BlockSpec((r, cb), lambda h, j: (0, h * nb + j))
    oth = lambda r: pl.BlockSpec((r, cb), lambda h, j: (0, (1 - h) * nb + j))
    return pl.pallas_call(
        body, name=name, grid=(2, nb),
        in_specs=[own(t), oth(t), pl.BlockSpec((t, cb), lambda h, j: (0, j)), own(kw), oth(kw), own(1), oth(1)],
        out_specs=[own(t), own(kw), own(1)],
        out_shape=[S((t, c2), dx_dtype), S((kw, c2), F32), S((1, c2), F32)],
        scratch_shapes=[pltpu.VMEM((CONV_PAD + t, cb), F32)] * 3,
        compiler_params=pltpu.CompilerParams(dimension_semantics=("parallel", "parallel")),
    )(hpre, hpre, dact, w, w, b, b)


def _bdot(a, b, ca, cb):
    return lax.dot_general(a.astype(MXU_DTYPE), b.astype(MXU_DTYPE), (((ca,), (cb,)), ((0,), (0,))),
                           preferred_element_type=F32)


@jax.custom_vjp
def bmm_nn(a, b):
    return _bdot(a, b, 2, 1)


bmm_nn.defvjp(lambda a, b: (_bdot(a, b, 2, 1), (a, b)),
              lambda r, g: (_bdot(g, r[1], 2, 2), _bdot(r[0], g, 1, 1)))


@jax.custom_vjp
def bmm_tn(a, b):
    return _bdot(a, b, 1, 1)


bmm_tn.defvjp(lambda a, b: (_bdot(a, b, 1, 1), (a, b)),
              lambda r, g: (_bdot(r[1], g, 2, 2), _bdot(r[0], g, 2, 1)))


@jax.custom_vjp
def bmm_nt(a, b):
    return _bdot(a, b, 2, 2)


bmm_nt.defvjp(lambda a, b: (_bdot(a, b, 2, 2), (a, b)),
              lambda r, g: (_bdot(g, r[1], 2, 1), _bdot(g, r[0], 1, 1)))


def ssd_chunk(x, dt, dt_row, bm, cm, hprev, a_log, dsk):
    hg, ln, _ = x.shape
    n = bm.shape[1]
    ii = lax.broadcasted_iota(jnp.int32, (ln, ln), 0)
    jj = lax.broadcasted_iota(jnp.int32, (ln, ln), 1)
    tril, triu = (ii >= jj)[None], (ii <= jj)[None]
    a = -jnp.exp(a_log)
    da = dt * a
    da_row = dt_row * a
    cum_c = jnp.sum(jnp.where(tril, da_row, 0.0), axis=2, keepdims=True)
    cum_r = jnp.sum(jnp.where(triu, da, 0.0), axis=1, keepdims=True)
    last = jnp.sum(da, axis=1, keepdims=True)
    decay = jnp.where(tril, jnp.exp(jnp.where(tril, cum_c - cum_r, 0.0)), 0.0)
    cb = bmm_nt(cm[None], bm[None])
    y_diag = bmm_nn(cb * decay * dt_row, x)
    bb = jnp.broadcast_to(bm[None], (hg, ln, n))
    cc = jnp.broadcast_to(cm[None], (hg, ln, n))
    states = bmm_tn(x * (jnp.exp(last - cum_c) * dt), bb)
    y_off = bmm_nt(cc, hprev) * jnp.exp(cum_c)
    hnew = hprev * jnp.exp(last) + states
    return y_diag + y_off + dsk * x, hnew


def _ssd_dims(xs, bm, a_log):
    t = xs.shape[0]
    h = a_log.shape[0]
    return h, t, xs.shape[1] // h, h // N_GROUPS, bm.shape[1] // N_GROUPS, t // CHUNK


def _heads_of(ref, g, hg, p):
    return jnp.stack([ref[:, (g * hg + i) * p:(g * hg + i + 1) * p] for i in range(hg)])


def _cols_of(ref, g, hg):
    return jnp.stack([ref[:, g * hg + i:g * hg + i + 1] for i in range(hg)])


def dt_rows(dt, h):
    t = dt.shape[0]
    return dt[:, :h].T.reshape(h, t // CHUNK, 1, CHUNK).transpose(1, 0, 2, 3)


def dt_cols(dtr, lanes):
    nc, h, _, ln = dtr.shape
    return jnp.pad(dtr.transpose(1, 0, 2, 3).reshape(h, nc * ln).T, ((0, 0), (0, lanes - h)))


def ssd_fwd(name, xs, dt, dtr, bm, cm, a_log, dsk):
    h, t, p, hg, n, nc = _ssd_dims(xs, bm, a_log)

    def body(al_ref, dk_ref, x_ref, dt_ref, dtr_ref, b_ref, c_ref, y_ref, hp_ref, h_scr):
        @pl.when(pl.program_id(0) == 0)
        def _():
            h_scr[...] = jnp.zeros_like(h_scr)

        for g in range(N_GROUPS):
            hs, ns = slice(g * hg, (g + 1) * hg), slice(g * n, (g + 1) * n)
            hprev = h_scr[hs]
            hp_ref[hs, 0] = hprev
            y, hnew = ssd_chunk(_heads_of(x_ref, g, hg, p), _cols_of(dt_ref, g, hg), dtr_ref[0, hs], b_ref[:, ns],
                                c_ref[:, ns], hprev, al_ref[hs], dk_ref[hs])
            for i in range(hg):
                y_ref[:, (g * hg + i) * p:(g * hg + i + 1) * p] = y[i]
            h_scr[hs] = hnew

    head = pl.BlockSpec((h, 1, 1), lambda c: (0, 0, 0))
    row = lambda w: pl.BlockSpec((CHUNK, w), lambda c: (c, 0))
    return pl.pallas_call(
        body, name=name, grid=(nc,),
        in_specs=[head, head, row(h * p), row(dt.shape[1]), pl.BlockSpec((1, h, 1, CHUNK), lambda c: (c, 0, 0, 0)),
                  row(N_GROUPS * n), row(N_GROUPS * n)],
        out_specs=[row(h * p), pl.BlockSpec((h, 1, p, n), lambda c: (0, c, 0, 0))],
        out_shape=[S((t, h * p), F32), S((h, nc, p, n), F32)],
        scratch_shapes=[pltpu.VMEM((h, p, n), F32)],
        compiler_params=pltpu.CompilerParams(dimension_semantics=("arbitrary",)),
    )(a_log, dsk, xs, dt, dtr, bm, cm)


def ssd_bwd(name, xs, dt, dtr, bm, cm, a_log, dsk, hp, dy):
    h, t, p, hg, n, nc = _ssd_dims(xs, bm, a_log)

    def body(al_ref, dk_ref, x_ref, dt_ref, dtr_ref, b_ref, c_ref, hp_ref, dy_ref,
             dx_ref, ddt_ref, ddtr_ref, db_ref, dc_ref, dal_ref, ddk_ref, dh_scr):
        @pl.when(pl.program_id(0) == 0)
        def _():
            dh_scr[...] = jnp.zeros_like(dh_scr)
            dal_ref[...] = jnp.zeros_like(dal_ref)
            ddk_ref[...] = jnp.zeros_like(ddk_ref)

        ddt_ref[...] = jnp.zeros_like(ddt_ref)
        for g in range(N_GROUPS):
            hs, ns = slice(g * hg, (g + 1) * hg), slice(g * n, (g + 1) * n)
            _, vjp = jax.vjp(ssd_chunk, _heads_of(x_ref, g, hg, p), _cols_of(dt_ref, g, hg), dtr_ref[0, hs],
                             b_ref[:, ns], c_ref[:, ns], hp_ref[hs, 0], al_ref[hs], dk_ref[hs])
            gx, gdt, gdtr, gb, gc, ghp, gal, gdk = vjp((_heads_of(dy_ref, g, hg, p), dh_scr[hs]))
            for i in range(hg):
                dx_ref[:, (g * hg + i) * p:(g * hg + i + 1) * p] = gx[i]
                ddt_ref[:, g * hg + i:g * hg + i + 1] = gdt[i]
            ddtr_ref[0, hs] = gdtr
            db_ref[:, ns] = gb
            dc_ref[:, ns] = gc
            dh_scr[hs] = ghp
            dal_ref[hs] += gal
            ddk_ref[hs] += gdk

    head = pl.BlockSpec((h, 1, 1), lambda c: (0, 0, 0))
    row = lambda w: pl.BlockSpec((CHUNK, w), lambda c: (nc - 1 - c, 0))
    rows = pl.BlockSpec((1, h, 1, CHUNK), lambda c: (nc - 1 - c, 0, 0, 0))
    return pl.pallas_call(
        body, name=name, grid=(nc,),
        in_specs=[head, head, row(h * p), row(dt.shape[1]), rows, row(N_GROUPS * n), row(N_GROUPS * n),
                  pl.BlockSpec((h, 1, p, n), lambda c: (0, nc - 1 - c, 0, 0)), row(h * p)],
        out_specs=[row(h * p), row(dt.shape[1]), rows, row(N_GROUPS * n), row(N_GROUPS * n), head, head],
        out_shape=[S((t, h * p), F32), S(dt.shape, F32), S(dtr.shape, F32), S(bm.shape, F32), S(cm.shape, F32),
                   S((h, 1, 1), F32), S((h, 1, 1), F32)],
        scratch_shapes=[pltpu.VMEM((h, p, n), F32)],
        compiler_params=pltpu.CompilerParams(dimension_semantics=("arbitrary",)),
    )(a_log, dsk, xs, dt, dtr, bm, cm, hp, dy)


def _alpha(depth):
    return (2.0 * depth) ** 0.25


def _pad_lanes(v):
    return jnp.pad(v, ((0, 0), (0, LANE - v.shape[1])))


def split_even_weights(w, j):
    d = w["e_w_in"][j].shape[1]
    da = w["e_conv_a_w"].shape[2]
    db = w["e_norm_b_g"].shape[1]
    gn = N_GROUPS * N_STATE
    nh = w["e_dt_bias"].shape[1]
    main = 2 * da + 2 * db + 2 * gn
    win = w["e_w_in"][j]
    ox = 2 * da + db
    cw, cbias = w["e_conv_b_w"][j], w["e_conv_b_b"][j][None]
    return dict(
        d=d, da=da, db=db, gn=gn, nh=nh, main=main,
        win_main=win[:main], win_dt=jnp.pad(win[main:], ((0, LANE - nh), (0, 0))),
        caw=w["e_conv_a_w"][j], cab=w["e_conv_a_b"][j][None], lag=w["e_ln_a_g"][j][None], lab=w["e_ln_a_b"][j][None],
        cw_xs=cw[:, :db], cw_b=cw[:, db:db + gn], cw_c=cw[:, db + gn:],
        cb_xs=cbias[:, :db], cb_b=cbias[:, db:db + gn], cb_c=cbias[:, db + gn:],
        dt_bias=_pad_lanes(w["e_dt_bias"][j][None]), a_log=w["e_a_log"][j].reshape(nh, 1, 1),
        dsk=w["e_d_skip"][j].reshape(nh, 1, 1), norm_g=w["e_norm_b_g"][j][None],
        wout_a=w["e_w_out"][j][:da], wout_b=w["e_w_out"][j][da:],
    )


def even_fwd(tag, x, xm, lw, ln_g, ln_b, alpha):
    t = x.shape[0]
    da, db, gn, nh = lw["da"], lw["db"], lw["gn"], lw["nh"]
    u = mm(tag + "_win", xm, lw["win_main"], "nt")
    udt = mm(tag + "_windt", xm, lw["win_dt"], "nt")
    ua, ug, z, xs_pre = Win(u, da, 0), Win(u, da, 1), Win(u, db, 2 * da // db), Win(u, db, (2 * da + db) // db)
    b_pre, c_pre = Win(u, gn, (2 * da + 2 * db) // gn), Win(u, gn, (2 * da + 2 * db + gn) // gn)
    (a0,) = rowwise(tag + "_glu", f_glu, [ua, ug], [], [da])
    a1 = conv_fwd(tag + "_conva", a0, lw["caw"], lw["cab"])
    (ya,) = rowwise(tag + "_lna", f_ln_silu, [a1], [lw["lag"], lw["lab"]], [da], out_dtypes=[MXU_DTYPE])
    xs_c = conv_fwd(tag + "_convxs", xs_pre, lw["cw_xs"], lw["cb_xs"])
    b_c = conv_fwd(tag + "_convb", b_pre, lw["cw_b"], lw["cb_b"])
    c_c = conv_fwd(tag + "_convc", c_pre, lw["cw_c"], lw["cb_c"])
    xs, bm, cm = rowwise(tag + "_silu3", f_silu3, [xs_c, b_c, c_c], [], [db, gn, gn])
    (dt,) = rowwise(tag + "_dt", f_softplus, [udt], [lw["dt_bias"]], [LANE])
    dtr = dt_rows(dt, nh)
    yssd, hp = ssd_fwd(tag + "_ssd", xs, dt, dtr, bm, cm, lw["a_log"], lw["dsk"])
    (yb,) = rowwise(tag + "_gate", f_gate_rms, [yssd, z], [lw["norm_g"]], [db], out_dtypes=[MXU_DTYPE])
    ma = mm(tag + "_wouta", ya, lw["wout_a"], "nn")
    mb = mm(tag + "_woutb", yb, lw["wout_b"], "nn")

    def f_res(xv, mav, mbv, g, b):
        pre = alpha * xv + mav + mbv
        y = _ln(pre, g, b)
        return y, y, pre

    x1, x1m, pre = rowwise(tag + "_res", f_res, [x, ma, mb], [ln_g, ln_b], [x.shape[1]] * 3,
                           out_dtypes=[F32, MXU_DTYPE, F32])
    saved = dict(xm=xm, u=u, udt=udt, a0=a0, a1=a1, ya=ya, xs_c=xs_c, b_c=b_c, c_c=c_c, xs=xs, dt=dt, dtr=dtr, bm=bm,
                 cm=cm, hp=hp, yssd=yssd, yb=yb, pre=pre)
    return x1, x1m, saved


def even_bwd(tag, dx1_pieces, sv, lw, ln_g, ln_b, alpha):
    t = sv["u"].shape[0]
    da, db, gn, nh = lw["da"], lw["db"], lw["gn"], lw["nh"]
    u, xm = sv["u"], sv["xm"]
    mx = (MXU_DTYPE,)
    ua, ug, z, xs_pre = Win(u, da, 0), Win(u, da, 1), Win(u, db, 2 * da // db), Win(u, db, (2 * da + db) // db)
    b_pre, c_pre = Win(u, gn, (2 * da + 2 * db) // gn), Win(u, gn, (2 * da + 2 * db + gn) // gn)
    (dpre, dprem), (dg0, db0) = rowwise_bwd(tag + "_res_b", f_ln, [sv["pre"]], [ln_g, ln_b], [dx1_pieces],
                                            d_dtypes=[(F32, MXU_DTYPE)])
    dya = mm(tag + "_dya", dprem, lw["wout_a"], "nt")
    dyb = mm(tag + "_dyb", dprem, lw["wout_b"], "nt")
    dwout_a = mm(tag + "_dwouta", sv["ya"], dprem, "tn", EXCHANGE_DTYPE)
    dwout_b = mm(tag + "_dwoutb", sv["yb"], dprem, "tn", EXCHANGE_DTYPE)
    (dyssd, dz), (dnorm_g,) = rowwise_bwd(tag + "_gate_b", f_gate_rms, [sv["yssd"], z], [lw["norm_g"]], [[dyb]],
                                          d_dtypes=[(F32,), mx])
    dxs, ddt, ddtr, dbm, dcm, dalog, ddsk = ssd_bwd(tag + "_ssd_b", sv["xs"], sv["dt"], sv["dtr"], sv["bm"], sv["cm"],
                                                    lw["a_log"], lw["dsk"], sv["hp"], dyssd)
    ddt_pieces = [ddt, dt_cols(ddtr, ddt.shape[1])]
    (dudt,), (ddt_bias,) = rowwise_bwd(tag + "_dt_b", f_softplus, [sv["udt"]], [lw["dt_bias"]], [ddt_pieces],
                                       d_dtypes=[mx])
    (dxs_c, db_c, dc_c), _ = rowwise_bwd(tag + "_silu3_b", f_silu3, [sv["xs_c"], sv["b_c"], sv["c_c"]], [],
                                         [[dxs], [dbm], [dcm]])
    dxs_pre, dcw_xs, dcb_xs = conv_bwd(tag + "_convxs_b", xs_pre, dxs_c, lw["cw_xs"], MXU_DTYPE)
    db_pre, dcw_b, dcb_b = conv_bwd(tag + "_convb_b", b_pre, db_c, lw["cw_b"], MXU_DTYPE)
    dc_pre, dcw_c, dcb_c = conv_bwd(tag + "_convc_b", c_pre, dc_c, lw["cw_c"], MXU_DTYPE)
    (da1,), (dlag, dlab) = rowwise_bwd(tag + "_lna_b", f_ln_silu, [sv["a1"]], [lw["lag"], lw["lab"]], [[dya]])
    da0, dcaw, dcab = conv_bwd(tag + "_conva_b", sv["a0"], da1, lw["caw"])
    (dua, dug), _ = rowwise_bwd(tag + "_glu_b", f_glu, [ua, ug], [], [[da0]], d_dtypes=[mx, mx])
    du = jnp.concatenate([dua, dug, dz, dxs_pre, db_pre, dc_pre], axis=1)
    dx_m = mm(tag + "_dxm", du, lw["win_main"], "nn")
    dx_dt = mm(tag + "_dxdt", dudt, lw["win_dt"], "nn")
    dwin_main = mm(tag + "_dwin", du, xm, "tn", EXCHANGE_DTYPE)
    dwin_dt = mm(tag + "_dwindt", dudt, xm, "tn", EXCHANGE_DTYPE)
    grads = dict(
        e_w_in=jnp.concatenate([dwin_main, dwin_dt[:nh]], axis=0),
        e_conv_a_w=dcaw, e_conv_a_b=dcab[0], e_ln_a_g=dlag[0], e_ln_a_b=dlab[0],
        e_conv_b_w=jnp.concatenate([dcw_xs, dcw_b, dcw_c], axis=1),
        e_conv_b_b=jnp.concatenate([dcb_xs, dcb_b, dcb_c], axis=1)[0],
        e_dt_bias=ddt_bias[0, :nh], e_a_log=dalog.reshape(nh), e_d_skip=ddsk.reshape(nh), e_norm_b_g=dnorm_g[0],
        e_w_out=jnp.concatenate([dwout_a, dwout_b], axis=0), ln_g0=dg0[0], ln_b0=db0[0],
    )
    return [Win(dpre, coef=alpha), dx_m, dx_dt], grads


def odd_fwd(tag, x, xm, w, j, ln_g, ln_b, alpha):
    d = x.shape[1]
    u = mm(tag + "_win", xm, w["o_w_in"][j], "nt")
    bg, cg, v = Win(u, d, 0), Win(u, d, 1), Win(u, d, 2)
    (s,) = rowwise(tag + "_cv", f_mul, [cg, v], [], [d])
    cs = conv_fwd(tag + "_conv", s, w["o_conv_w"][j], None)
    (m,) = rowwise(tag + "_bm", f_mul, [bg, cs], [], [d], out_dtypes=[MXU_DTYPE])
    mix = mm(tag + "_wout", m, w["o_w_out"][j], "nn")

    def f_res(xv, mv, g, b):
        pre = alpha * xv + mv
        y = _ln(pre, g, b)
        return y, y, pre

    x1, x1m, pre = rowwise(tag + "_res", f_res, [x, mix], [ln_g, ln_b], [d] * 3, out_dtypes=[F32, MXU_DTYPE, F32])
    return x1, x1m, dict(xm=xm, u=u, s=s, cs=cs, m=m, pre=pre)


def odd_bwd(tag, dx1_pieces, sv, w, j, ln_g, ln_b, alpha):
    xm, u = sv["xm"], sv["u"]
    d = xm.shape[1]
    mx = (MXU_DTYPE,)
    bg, cg, v = Win(u, d, 0), Win(u, d, 1), Win(u, d, 2)
    (dpre, dprem), (dg0, db0) = rowwise_bwd(tag + "_res_b", f_ln, [sv["pre"]], [ln_g, ln_b], [dx1_pieces],
                                            d_dtypes=[(F32, MXU_DTYPE)])
    dm = mm(tag + "_dm", dprem, w["o_w_out"][j], "nt")
    dwout = mm(tag + "_dwout", sv["m"], dprem, "tn", EXCHANGE_DTYPE)
    (dbg, dcs), _ = rowwise_bwd(tag + "_bm_b", f_mul, [bg, sv["cs"]], [], [[dm]], d_dtypes=[mx, (F32,)])
    ds, dcw, _ = conv_bwd(tag + "_conv_b", sv["s"], dcs, w["o_conv_w"][j])
    (dcg, dv), _ = rowwise_bwd(tag + "_cv_b", f_mul, [cg, v], [], [[ds]], d_dtypes=[mx, mx])
    du = jnp.concatenate([dbg, dcg, dv], axis=1)
    dx_u = mm(tag + "_dx", du, w["o_w_in"][j], "nn")
    dwin = mm(tag + "_dwin", du, xm, "tn", EXCHANGE_DTYPE)
    grads = dict(o_w_in=dwin, o_conv_w=dcw, o_w_out=dwout, ln_g0=dg0[0], ln_b0=db0[0])
    return [Win(dpre, coef=alpha), dx_u], grads


def ffn_fwd(tag, x1, x1m, p_i, w, i, ln_g, ln_b, alpha):
    d = x1.shape[1]
    hpre = mm(tag + "_wup", x1m, w["f_w_up"][i], "nt")
    act = gated_conv_fwd(tag + "_fgate", hpre, w["f_conv_w"][i], w["f_conv_b"][i][None], MXU_DTYPE)
    ffn = mm(tag + "_wdown", act, w["f_w_down"][i], "nn")
    pp = mm(tag + "_pproj", p_i, w["ple_w_proj"][i], "nt")
    gt = mm(tag + "_pgate", x1m, w["ple_w_gate"][i], "nn")

    def f_res2(xv, fv, ppv, gtv, g, b):
        pre = alpha * xv + fv + ppv * _sigmoid(gtv)
        y = _ln(pre, g, b)
        return y, y, pre

    x2, x2m, pre = rowwise(tag + "_res2", f_res2, [x1, ffn, pp, gt], [ln_g, ln_b], [d] * 3,
                           out_dtypes=[F32, MXU_DTYPE, F32])
    return x2, x2m, dict(x1m=x1m, hpre=hpre, act=act, pp=pp, gt=gt, pre=pre)


def ffn_bwd(tag, dx2_pieces, sv, p_i, w, i, ln_g, ln_b, alpha):
    x1m = sv["x1m"]
    mx = (MXU_DTYPE,)
    (dpre, dprem), (dg1, db1) = rowwise_bwd(tag + "_res2_b", f_ln, [sv["pre"]], [ln_g, ln_b], [dx2_pieces],
                                            d_dtypes=[(F32, MXU_DTYPE)])
    (dpp, dgt), _ = rowwise_bwd(tag + "_pg_b", f_gate_mul, [sv["pp"], sv["gt"]], [], [[dpre]], d_dtypes=[mx, mx])
    dwproj = mm(tag + "_dwproj", dpp, p_i, "tn", EXCHANGE_DTYPE)
    dwgate = mm(tag + "_dwgate", x1m, dgt, "tn", EXCHANGE_DTYPE)
    dx1_a = mm(tag + "_dx1a", dgt, w["ple_w_gate"][i], "nt")
    dact = mm(tag + "_dact", dprem, w["f_w_down"][i], "nt")
    dwdown = mm(tag + "_dwdown", sv["act"], dprem, "tn", EXCHANGE_DTYPE)
    dhpre, dfcw, dfcb = gated_conv_bwd(tag + "_fgate_b", sv["hpre"], dact, w["f_conv_w"][i], w["f_conv_b"][i][None],
                                       MXU_DTYPE)
    dwup = mm(tag + "_dwup", dhpre, x1m, "tn", EXCHANGE_DTYPE)
    dx1_b = mm(tag + "_dx1b", dhpre, w["f_w_up"][i], "nn")
    grads = dict(f_w_up=dwup, f_conv_w=dfcw, f_conv_b=dfcb[0], f_w_down=dwdown, ple_w_proj=dwproj, ple_w_gate=dwgate,
                 ln_g1=dg1[0], ln_b1=db1[0])
    return [Win(dpre, coef=alpha), dx1_a, dx1_b], grads


def local_step(x, p, w, target, fetch=None, emit=None):
    depth = w["ln_g"].shape[0]
    alpha = _alpha(depth)
    d = x.shape[1]
    saved = []
    h = hm = x
    for i in range(depth):
        j = i // 2
        if fetch is not None:
            fetch(i, "m", h)
        g0, b0, g1, b1 = w["ln_g"][i, 0][None], w["ln_b"][i, 0][None], w["ln_g"][i, 1][None], w["ln_b"][i, 1][None]
        tag = "l%d" % i
        if i % 2 == 0:
            lw = split_even_weights(w, j)
            h, hm, sv_m = even_fwd(tag, h, hm, lw, g0, b0, alpha)
        else:
            lw = None
            h, hm, sv_m = odd_fwd(tag, h, hm, w, j, g0, b0, alpha)
        if fetch is not None:
            fetch(i, "f", h)
        h, hm, sv_f = ffn_fwd(tag, h, hm, p[i], w, i, g1, b1, alpha)
        saved.append((lw, sv_m, sv_f, (g0, b0, g1, b1)))

    def f_loss(xf, tg):
        diff = xf - tg
        sq = jnp.sum(jnp.sum(jnp.square(diff), axis=1, keepdims=True), axis=0, keepdims=True)
        return diff * (1.0 / d), jnp.broadcast_to(sq, (1, LANE))

    dxf, sq = rowwise("loss", f_loss, [h, target], [], [d], red_widths=[LANE])
    loss = sq[0, 0] * (0.5 / d)

    per_layer = []
    pieces = [dxf]
    token = None
    for i in reversed(range(depth)):
        j = i // 2
        lw, sv_m, sv_f, (g0, b0, g1, b1) = saved[i]
        tag = "l%d" % i
        pieces, gf = ffn_bwd(tag, pieces, sv_f, p[i], w, i, g1 if token is None else g1 + token, b1, alpha)
        if emit is not None:
            token = emit(tag + "f", i, gf)
        g0 = g0 if token is None else g0 + token
        if i % 2 == 0:
            pieces, gm = even_bwd(tag, pieces, sv_m, lw, g0, b0, alpha)
        else:
            pieces, gm = odd_bwd(tag, pieces, sv_m, w, j, g0, b0, alpha)
        if emit is not None:
            token = emit(tag + "m", j, gm)
        per_layer.append((i, gm, gf))

    def f_sum(*vs):
        acc = None
        for v, c in zip(vs, [pc.coef for pc in map(_win, pieces)]):
            v = v if c == 1.0 else v * c
            acc = v if acc is None else acc + v
        return (acc,)

    (grad_x,) = rowwise("grad_x", f_sum, [Win(_win(pc).arr) for pc in pieces], [], [d])

    by_layer = {i: (gm, gf) for i, gm, gf in per_layer}
    grads = {}
    n_even, n_odd = (depth + 1) // 2, depth // 2
    collect = lambda name, per_layer: per_layer if name in BIG else jnp.stack(per_layer)
    for name in ("e_w_in", "e_conv_a_w", "e_conv_a_b", "e_ln_a_g", "e_ln_a_b", "e_conv_b_w", "e_conv_b_b", "e_dt_bias",
                 "e_a_log", "e_d_skip", "e_norm_b_g", "e_w_out"):
        grads[name] = collect(name, [by_layer[2 * j][0][name] for j in range(n_even)])
    for name in ("o_w_in", "o_conv_w", "o_w_out"):
        grads[name] = collect(name, [by_layer[2 * j + 1][0][name] for j in range(n_odd)])
    for name in ("f_w_up", "f_conv_w", "f_conv_b", "f_w_down", "ple_w_proj", "ple_w_gate"):
        grads[name] = collect(name, [by_layer[i][1][name] for i in range(depth)])
    grads["ln_g"] = jnp.stack([jnp.stack([by_layer[i][0]["ln_g0"], by_layer[i][1]["ln_g1"]]) for i in range(depth)])
    grads["ln_b"] = jnp.stack([jnp.stack([by_layer[i][0]["ln_b0"], by_layer[i][1]["ln_b1"]]) for i in range(depth)])
    return loss, grad_x, grads


_ANY = pl.BlockSpec(memory_space=pl.ANY)
_MESH = pl.DeviceIdType.MESH


def all_gather(name, xl):
    r, c_ = xl.shape
    split = r // 2 // 16 * 16
    halves = ((0, split), (split, r - split)) if split else ((0, r),)
    two = len(halves) == 2

    def body(x_ref, out_ref, send_sems, recv_sems, local_sem):
        x, y, c = lax.axis_index("x"), lax.axis_index("y"), lax.axis_index("c")
        me, sibling, xn, yn, dg = (x, y, c), (x, y, 1 - c), (1 - x, y, c), (x, 1 - y, c), (1 - x, 1 - y, c)

        def rows(block, h):
            ref = out_ref.at[4 * block[0] + 2 * block[1] + block[2]]
            return ref if h is None else ref.at[pl.ds(*halves[h])]

        def copy(k, block, h, to, own=False):
            src = (x_ref if h is None else x_ref.at[pl.ds(*halves[h])]) if own else rows(block, h)
            return pltpu.make_async_remote_copy(src_ref=src, dst_ref=rows(block, h), send_sem=send_sems.at[k],
                                                recv_sem=recv_sems.at[k], device_id=to, device_id_type=_MESH)

        def other_core(block):
            return (block[0], block[1], 1 - c)

        mine = pltpu.make_async_copy(x_ref, rows(me, None), local_sem)
        mine.start()
        direct = [copy(0, me, 0, xn, own=True), copy(1, me, 1 if two else 0, yn, own=True)]
        if two:
            direct += [copy(2, me, 1, xn, own=True), copy(3, me, 0, yn, own=True)]
        direct.append(copy(6, me, None, sibling, own=True))
        for cp in direct:
            cp.start()
        started = list(direct)

        def then(waits, nxt):
            for cp in waits:
                cp.wait_recv()
            for cp in nxt:
                cp.start()
            started.extend(nxt)

        if two:
            then([copy(0, xn, 0, me)], [copy(4, xn, 0, yn)])
            then([copy(1, yn, 1, me)], [copy(5, yn, 1, xn)])
            then([copy(2, xn, 1, me)], [copy(7, xn, None, sibling)])
            then([copy(3, yn, 0, me)], [copy(8, yn, None, sibling)])
            then([copy(4, dg, 0, me), copy(5, dg, 1, me)], [copy(9, dg, None, sibling)])
        else:
            then([copy(0, xn, 0, me)], [copy(4, xn, 0, yn), copy(7, xn, None, sibling)])
            then([copy(1, yn, 0, me)], [copy(8, yn, None, sibling)])
            then([copy(4, dg, 0, me)], [copy(9, dg, None, sibling)])
        for k, block in ((6, me), (7, xn), (8, yn), (9, dg)):
            copy(k, other_core(block), None, me).wait_recv()
        for cp in started:
            cp.wait_send()
        mine.wait()

    return pl.pallas_call(
        body, name=name, out_shape=S((N_DEV, r, c_), xl.dtype), in_specs=[_ANY], out_specs=_ANY,
        scratch_shapes=[pltpu.SemaphoreType.DMA((10,)), pltpu.SemaphoreType.DMA((10,)), pltpu.SemaphoreType.DMA],
    )(xl)


def sum_devices(name, g8):
    _, r, c_ = g8.shape

    def body(g_ref, o_ref):
        acc = g_ref[0]
        for k in range(1, N_DEV):
            acc = acc + g_ref[k]
        o_ref[...] = acc

    return pl.pallas_call(body, name=name, out_shape=S((r, c_), F32))(g8)


def _flatten(parts, cols, row_mult):
    flat = jnp.concatenate([v.reshape(-1) for v in parts])
    n = flat.shape[0]
    rows = -(-n // (cols * row_mult)) * row_mult
    return jnp.pad(flat, (0, rows * cols - n)).reshape(rows, cols)


def _exchange_dims(name, lshape):
    l, r, c = lshape
    return (l, c, r) if name in TRANSPOSED else (l, r, c)


_HBM = pl.BlockSpec(memory_space=pltpu.HBM)
_SEM = pl.BlockSpec(memory_space=pltpu.SEMAPHORE)
_PEERS = [(dx, dy, dc) for dx in (0, 1) for dy in (0, 1) for dc in (0, 1)][1:]


def _peer_copies(scatter, src_refs, land_refs, send_sems, recv_sems):
    x, y, c = lax.axis_index("x"), lax.axis_index("y"), lax.axis_index("c")
    flip = lambda v, d: 1 - v if d else v
    copies = []
    for s_ref, land_ref in zip(src_refs, land_refs):
        for dx, dy, dc in _PEERS:
            px, py, pc = flip(x, dx), flip(y, dy), flip(c, dc)
            k = len(copies)
            copies.append(pltpu.make_async_remote_copy(
                src_ref=s_ref.at[4 * px + 2 * py + pc] if scatter else s_ref, dst_ref=land_ref.at[4 * x + 2 * y + c],
                send_sem=send_sems.at[k], recv_sem=recv_sems.at[k], device_id=(px, py, pc), device_id_type=_MESH))
    return copies


def split_start(name, srcs, scatter):
    n = len(srcs)
    lands = [lax.empty(s.shape if scatter else (N_DEV,) + s.shape, s.dtype) for s in srcs]
    n_copies = n * len(_PEERS)

    def body(*refs):
        send_sems, recv_sems, token = refs[2 * n], refs[2 * n + 1], refs[-1]
        for cp in _peer_copies(scatter, refs[:n], refs[n:2 * n], send_sems, recv_sems):
            cp.start()
        token[...] = jnp.zeros_like(token)

    return pl.pallas_call(
        body, name=name,
        out_shape=(pltpu.SemaphoreType.DMA((n_copies,)), pltpu.SemaphoreType.DMA((n_copies,)),
                   *[pltpu.HBM(v.shape, v.dtype) for v in srcs + lands], S((SUBLANE, LANE), F32)),
        in_specs=(_HBM,) * (2 * n), out_specs=(_SEM, _SEM) + (_HBM,) * (2 * n) + (pl.BlockSpec(memory_space=pltpu.VMEM),),
        input_output_aliases={i: 2 + i for i in range(2 * n)},
        compiler_params=pltpu.CompilerParams(has_side_effects=pltpu.SideEffectType.DATAFLOW_SIDE_EFFECTING),
    )(*[pltpu.with_memory_space_constraint(v, pltpu.HBM) for v in srcs + lands])


def split_wait(name, started, scatter, after):
    send_sems, recv_sems = started[0], started[1]
    thru = list(started[2:-1])
    n = len(thru) // 2

    def body(*refs):
        send_sems, recv_sems = refs[2 * n], refs[2 * n + 1]
        for cp in _peer_copies(scatter, refs[:n], refs[n:2 * n], send_sems, recv_sems):
            cp.wait_send()
            cp.wait_recv()

    outs = pl.pallas_call(
        body, name=name, out_shape=tuple(pltpu.HBM(v.shape, v.dtype) for v in thru),
        in_specs=(_HBM,) * (2 * n) + (_SEM, _SEM, _ANY), out_specs=(_HBM,) * (2 * n),
        input_output_aliases={i: i for i in range(2 * n)},
        compiler_params=pltpu.CompilerParams(has_side_effects=pltpu.SideEffectType.DATAFLOW_SIDE_EFFECTING),
    )(*thru, send_sems, recv_sems, after)
    return outs[:n], outs[n:]


def _layer_weights(depth):
    table = []
    for i in range(depth):
        mixer = ("e_w_out", "e_w_in") if i % 2 == 0 else ("o_w_out", "o_w_in")
        table.append([("f_w_up", i), ("f_w_down", i), ("ple_w_gate", i), ("ple_w_proj", i)]
                     + [(n, i // 2) for n in mixer])
    return table


def gather_big(local):
    cols = local["e_w_out"].shape[2]
    depth = local["ln_g"].shape[0]
    table = _layer_weights(depth)
    shard = {}
    for n in BIG:
        v = local[n].astype(MXU_DTYPE)
        shard[n] = v.transpose(0, 2, 1) if n in TRANSPOSED else v
    full = {n: [None] * local[n].shape[0] for n in BIG}
    dev = 4 * lax.axis_index("x") + 2 * lax.axis_index("y") + lax.axis_index("c")

    first = [(n, li) for n, li in table[0] if n.startswith("e_")]
    table[0] = [(n, li) for n, li in table[0] if not n.startswith("e_")]
    got, r0 = all_gather("ag_l0", jnp.concatenate([shard[n][li].reshape(-1, cols) for n, li in first], axis=0)), 0
    for n, li in first:
        _, a, b = shard[n].shape
        per = a * b // cols
        full[n][li] = got[:, r0:r0 + per].reshape(N_DEV * a, b)
        r0 += per
    started = [split_start("ag_start_l%d" % i, [shard[n][li] for n, li in table[i]], False) for i in range(depth)]

    def fetch(i, part, after):
        if part == ("f" if i == 0 else "m"):
            srcs, lands = split_wait("ag_wait_l%d" % i, started[i], False, after)
            for (n, li), src, land in zip(table[i], srcs, lands):
                _, a, b = shard[n].shape
                full[n][li] = lax.dynamic_update_slice(land, src[None], (dev, 0, 0)).reshape(N_DEV * a, b)

    token = sum(st[-1][0, 0] for st in started)
    return full, fetch, token


def sum_landed(name, g, land, me):
    _, a, b = g.shape
    tb = b // 2 if b % (2 * LANE) == 0 else b

    def body(me_ref, g_ref, land_ref, o_ref):
        acc = jnp.zeros(o_ref.shape, F32)
        for k in range(N_DEV):
            acc = acc + jnp.where(me_ref[0] == k, g_ref[0], land_ref[k]).astype(F32)
        o_ref[...] = acc

    return pl.pallas_call(
        body, name=name, out_shape=S((a, b), F32),
        grid_spec=pltpu.PrefetchScalarGridSpec(
            num_scalar_prefetch=1, grid=(b // tb,),
            in_specs=[pl.BlockSpec((1, a, tb), lambda j, me_: (me_[0], 0, j)),
                      pl.BlockSpec((N_DEV, a, tb), lambda j, me_: (0, 0, j))],
            out_specs=pl.BlockSpec((a, tb), lambda j, me_: (0, j))),
    )(me, g, land)


class GradScatter:
    def __init__(self, local_shapes):
        self.local_shapes = local_shapes
        self.pending = []

    def emit(self, tag, index, grads):
        names = [n for n in BIG if n in grads]
        gs = [grads[n].reshape(N_DEV, -1, grads[n].shape[1]) for n in names]
        started = split_start("rs_start_" + tag, gs, True)
        self.pending.append((tag, index, names, started))
        return started[-1][0:1, 0:1]

    def finish(self, after):
        me = (4 * lax.axis_index("x") + 2 * lax.axis_index("y") + lax.axis_index("c")).astype(jnp.int32).reshape(1)
        shards = {n: [None] * self.local_shapes[n][0] for n in BIG}
        for tag, index, names, started in self.pending:
            gs, lands = split_wait("rs_wait_" + tag, started, True, after)
            for n, g, land in zip(names, gs, lands):
                shards[n][index] = sum_landed("rs_sum_%s_%s" % (tag, n), g, land, me)
        out = {}
        for n in BIG:
            seg = jnp.stack(shards[n])
            out[n] = seg.transpose(0, 2, 1) if n in TRANSPOSED else seg
        return out


def gather_small(name, local, names):
    flat = _flatten([local[n] for n in names], LANE, 1)
    got = all_gather(name, flat).reshape(N_DEV, -1)
    full, off = {}, 0
    for n in names:
        size = math.prod(local[n].shape)
        seg = got[:, off:off + size].reshape((N_DEV,) + local[n].shape)
        full[n] = seg.transpose(1, 2, 0, 3).reshape(seg.shape[1], seg.shape[2], -1)
        off += size
    return full


def all_reduce_small(grads, names):
    flat = _flatten([grads[n] for n in names], LANE, SUBLANE)
    total = sum_devices("ar_sum", all_gather("ar_gather", flat)).reshape(-1)
    out, off = {}, 0
    for nm in names:
        size = math.prod(grads[nm].shape)
        out[nm] = total[off:off + size].reshape(grads[nm].shape)
        off += size
    return out


def adamw(name, w, g, m, v):
    shape = w.shape
    cols = shape[-1]
    rows = math.prod(shape[:-1])
    tr = _pick(rows, (256, 128, 64, 32, 16, 8)) if rows * cols > 256 * 1024 else rows
    c1 = 1.0 - ADAM_B1 ** ADAM_STEP
    c2 = 1.0 - ADAM_B2 ** ADAM_STEP

    def body(w_ref, g_ref, m_ref, v_ref, d_ref, nm_ref, nv_ref):
        gv = g_ref[...]
        m2 = ADAM_B1 * m_ref[...] + (1.0 - ADAM_B1) * gv
        v2 = ADAM_B2 * v_ref[...] + (1.0 - ADAM_B2) * jnp.square(gv)
        d_ref[...] = -ADAM_LR * ((m2 / c1) / (jnp.sqrt(v2 / c2) + ADAM_EPS) + ADAM_WD * w_ref[...])
        nm_ref[...] = m2
        nv_ref[...] = v2

    spec = pl.BlockSpec((tr, cols), lambda i: (i, 0))
    outs = pl.pallas_call(
        body, name=name, grid=(rows // tr,), in_specs=[spec] * 4, out_specs=[spec] * 3,
        out_shape=[S((rows, cols), F32)] * 3,
        compiler_params=pltpu.CompilerParams(dimension_semantics=("parallel",)),
    )(*[a.reshape(rows, cols) for a in (w, g, m, v)])
    return tuple(o.reshape(shape) for o in outs)


def kernel(x, p, e_w_in, e_conv_a_w, e_conv_a_b, e_ln_a_g, e_ln_a_b, e_conv_b_w, e_conv_b_b, e_dt_bias, e_a_log, e_d_skip, e_norm_b_g, e_w_out, o_w_in, o_conv_w, o_w_out, f_w_up, f_conv_w, f_conv_b, f_w_down, ple_w_proj, ple_w_gate, ln_g, ln_b, loss_target, m_e_w_in, m_e_conv_a_w, m_e_conv_a_b, m_e_ln_a_g, m_e_ln_a_b, m_e_conv_b_w, m_e_conv_b_b, m_e_dt_bias, m_e_a_log, m_e_d_skip, m_e_norm_b_g, m_e_w_out, m_o_w_in, m_o_conv_w, m_o_w_out, m_f_w_up, m_f_conv_w, m_f_conv_b, m_f_w_down, m_ple_w_proj, m_ple_w_gate, m_ln_g, m_ln_b, v_e_w_in, v_e_conv_a_w, v_e_conv_a_b, v_e_ln_a_g, v_e_ln_a_b, v_e_conv_b_w, v_e_conv_b_b, v_e_dt_bias, v_e_a_log, v_e_d_skip, v_e_norm_b_g, v_e_w_out, v_o_w_in, v_o_conv_w, v_o_w_out, v_f_w_up, v_f_conv_w, v_f_conv_b, v_f_w_down, v_ple_w_proj, v_ple_w_gate, v_ln_g, v_ln_b):
    args = locals()
    local = {n: args[n] for n in WEIGHTS}
    mom = {n: args["m_" + n] for n in WEIGHTS}
    var = {n: args["v_" + n] for n in WEIGHTS}

    full = {n: local[n] for n in REPLICATED}
    big, fetch, token = gather_big(local)
    full.update(big)
    full.update(gather_small("ag_small", local, SMALL_SHARDED))

    scatter = GradScatter({n: local[n].shape for n in BIG})
    loss_local, grad_x, grads = local_step(x[0] + token, p[:, 0], full, loss_target[0], fetch, scatter.emit)
    loss = lax.psum(loss_local, MESH_AXES)

    small = all_reduce_small(grads, REPLICATED + SMALL_SHARDED)
    dev = 4 * lax.axis_index("x") + 2 * lax.axis_index("y") + lax.axis_index("c")
    g_local = {n: small[n] for n in REPLICATED}
    for n in SMALL_SHARDED:
        width = local[n].shape[2]
        g_local[n] = lax.dynamic_slice_in_dim(small[n], dev * width, width, axis=2)
    delta, new_m, new_v = {}, {}, {}
    for n in REPLICATED + SMALL_SHARDED:
        delta[n], new_m[n], new_v[n] = adamw("adamw_" + n, local[n], g_local[n], mom[n], var[n])

    done = sum(delta[n].reshape(-1)[0] for n in REPLICATED + SMALL_SHARDED)
    g_local.update(scatter.finish(delta[SMALL_SHARDED[-1]] + 0.0 * done))
    for n in BIG:
        delta[n], new_m[n], new_v[n] = adamw("adamw_" + n, local[n], g_local[n], mom[n], var[n])
    return (loss, grad_x[None], *[g_local[n] for n in WEIGHTS], *[delta[n] for n in WEIGHTS],
            *[new_m[n] for n in WEIGHTS], *[new_v[n] for n in WEIGHTS])
```

```python
import functools
import math

import jax
import jax.numpy as jnp
from jax import lax
from jax.experimental import pallas as pl
from jax.experimental.pallas import tpu as pltpu

F32 = jnp.float32
MXU_DTYPE = jnp.bfloat16
MESH_AXES = ("x", "y", "c")
N_DEV = 8
LANE = 128
SUBLANE = 8
ROWWISE_VMEM_BUDGET = 20 * 1024 * 1024
MM_TILES = (1408, 1024, 512, 256, 128)
EXCHANGE_DTYPE = jnp.bfloat16
LN_EPS = 1e-5
CHUNK = 64
HEAD_DIM = 64
N_GROUPS = 4
N_STATE = 128
CONV_PAD = 32
CONV_ROWS = 256
ADAM_LR, ADAM_B1, ADAM_B2, ADAM_EPS, ADAM_WD, ADAM_STEP = 0.001, 0.9, 0.999, 1e-08, 0.01, 10

BIG = ("e_w_in", "e_w_out", "o_w_in", "o_w_out", "f_w_up", "f_w_down", "ple_w_proj", "ple_w_gate")
SMALL_SHARDED = ("e_conv_a_w", "e_conv_b_w", "o_conv_w", "f_conv_w", "ln_g", "ln_b")
REPLICATED = ("e_conv_a_b", "e_ln_a_g", "e_ln_a_b", "e_conv_b_b", "e_dt_bias", "e_a_log", "e_d_skip",
              "e_norm_b_g", "f_conv_b")
TRANSPOSED = ("e_w_in", "o_w_in", "f_w_up", "ple_w_proj")
WEIGHTS = ("e_w_in", "e_conv_a_w", "e_conv_a_b", "e_ln_a_g", "e_ln_a_b", "e_conv_b_w", "e_conv_b_b", "e_dt_bias",
           "e_a_log", "e_d_skip", "e_norm_b_g", "e_w_out", "o_w_in", "o_conv_w", "o_w_out", "f_w_up", "f_conv_w",
           "f_conv_b", "f_w_down", "ple_w_proj", "ple_w_gate", "ln_g", "ln_b")

S = jax.ShapeDtypeStruct


class Win:
    def __init__(self, arr, w=None, idx=0, coef=1.0):
        self.arr, self.w, self.idx, self.coef = arr, (arr.shape[1] if w is None else w), idx, coef


def _win(a):
    return a if isinstance(a, Win) else Win(a)


def _pick(n, prefs):
    for p in prefs:
        if p <= n and n % p == 0:
            return p
    return n


_MM_DIMS = {"nn": (1, 0), "nt": (1, 1), "tn": (0, 0)}


def mm(name, a, b, mode, out_dtype=F32):
    ca, cb = _MM_DIMS[mode]
    kdim = a.shape[ca]
    m = a.shape[1 - ca]
    n = b.shape[1 - cb]
    assert b.shape[cb] == kdim, (name, a.shape, b.shape, mode)
    tm = _pick(m, MM_TILES)
    tn = _pick(n, MM_TILES)
    tk = kdim if kdim <= MM_TILES[0] else _pick(kdim, MM_TILES)
    nk = kdim // tk
    own_acc = nk > 1 and out_dtype != F32

    def body(a_ref, b_ref, o_ref, *scratch):
        acc_ref = scratch[0] if own_acc else o_ref
        d = lax.dot_general(a_ref[...].astype(MXU_DTYPE), b_ref[...].astype(MXU_DTYPE),
                            (((ca,), (cb,)), ((), ())), preferred_element_type=F32)
        if nk == 1:
            o_ref[...] = d.astype(o_ref.dtype)
        else:
            k = pl.program_id(2)

            @pl.when(k == 0)
            def _():
                acc_ref[...] = d

            @pl.when(k > 0)
            def _():
                acc_ref[...] += d

            if own_acc:
                @pl.when(k == nk - 1)
                def _():
                    o_ref[...] = acc_ref[...].astype(o_ref.dtype)

    a_spec = pl.BlockSpec((tm, tk), lambda i, j, k: (i, k)) if ca == 1 else pl.BlockSpec((tk, tm), lambda i, j, k: (k, i))
    b_spec = pl.BlockSpec((tk, tn), lambda i, j, k: (k, j)) if cb == 0 else pl.BlockSpec((tn, tk), lambda i, j, k: (j, k))
    return pl.pallas_call(
        body, name=name, grid=(m // tm, n // tn, nk),
        in_specs=[a_spec, b_spec], out_specs=pl.BlockSpec((tm, tn), lambda i, j, k: (i, j)),
        out_shape=S((m, n), out_dtype), scratch_shapes=[pltpu.VMEM((tm, tn), F32)] if own_acc else [],
        compiler_params=pltpu.CompilerParams(dimension_semantics=("parallel", "parallel", "arbitrary")),
    )(a, b)


def _row_block(t, widths):
    tb = 512
    while tb > SUBLANE and (t % tb or tb * sum(widths) * 8 > ROWWISE_VMEM_BUDGET):
        tb //= 2
    return tb


def _tok_spec(tb, w):
    return pl.BlockSpec((tb, w.w), functools.partial(lambda i, idx: (i, idx), idx=w.idx))


def _par_spec(p):
    return pl.BlockSpec((1, p.shape[1]), lambda i: (0, 0))


def rowwise(name, fn, tok, par, out_widths, red_widths=(), out_dtypes=None):
    tok = [_win(t) for t in tok]
    t = tok[0].arr.shape[0]
    tb = _row_block(t, [w.w for w in tok] + list(out_widths))
    n_tok, n_par, n_out = len(tok), len(par), len(out_widths)
    out_dtypes = [F32] * n_out if out_dtypes is None else out_dtypes

    def body(*refs):
        ins = [r[...] for r in refs[:n_tok + n_par]]
        res = fn(*ins)
        out_refs = refs[n_tok + n_par:n_tok + n_par + n_out]
        red_refs = refs[n_tok + n_par + n_out:]
        for r, v in zip(out_refs, res[:n_out]):
            r[...] = v.astype(r.dtype)
        if red_refs:
            @pl.when(pl.program_id(0) == 0)
            def _():
                for r in red_refs:
                    r[...] = jnp.zeros_like(r)
            for r, v in zip(red_refs, res[n_out:]):
                r[...] += v

    outs = pl.pallas_call(
        body, name=name, grid=(t // tb,),
        in_specs=[_tok_spec(tb, w) for w in tok] + [_par_spec(p) for p in par],
        out_specs=[pl.BlockSpec((tb, w), lambda i: (i, 0)) for w in out_widths]
        + [pl.BlockSpec((1, w), lambda i: (0, 0)) for w in red_widths],
        out_shape=[S((t, w), dt) for w, dt in zip(out_widths, out_dtypes)] + [S((1, w), F32) for w in red_widths],
        compiler_params=pltpu.CompilerParams(dimension_semantics=("arbitrary",)),
    )(*[w.arr for w in tok], *par)
    return outs


def rowwise_bwd(name, fn, tok, par, cts, d_dtypes=None):
    tok = [_win(t) for t in tok]
    cts = [[_win(c) for c in group] for group in cts]
    d_dtypes = [(F32,)] * len(tok) if d_dtypes is None else d_dtypes
    t = tok[0].arr.shape[0]
    flat_cts = [c for group in cts for c in group]
    d_outs = [(i, w.w, dt) for i, (w, dts) in enumerate(zip(tok, d_dtypes)) for dt in dts]
    tb = _row_block(t, [w.w for w in tok] + [c.w for c in flat_cts] + [w for _, w, _ in d_outs])
    n_tok, n_par, n_ct, n_d = len(tok), len(par), len(flat_cts), len(d_outs)

    def body(*refs):
        tok_vals = [r[...] for r in refs[:n_tok]]
        par_vals = [r[...] for r in refs[n_tok:n_tok + n_par]]
        ct_refs = refs[n_tok + n_par:n_tok + n_par + n_ct]
        d_refs = refs[n_tok + n_par + n_ct:n_tok + n_par + n_ct + n_d]
        dp_refs = refs[n_tok + n_par + n_ct + n_d:]
        ct_vals, pos = [], 0
        for group in cts:
            acc = None
            for c in group:
                v = ct_refs[pos][...]
                if c.coef != 1.0:
                    v = v * c.coef
                acc = v if acc is None else acc + v
                pos += 1
            ct_vals.append(acc)
        _, vjp = jax.vjp(lambda *a: tuple(fn(*a)), *tok_vals, *par_vals)
        grads = vjp(tuple(ct_vals))
        for r, (i, _, _) in zip(d_refs, d_outs):
            r[...] = grads[i].astype(r.dtype)
        if dp_refs:
            @pl.when(pl.program_id(0) == 0)
            def _():
                for r in dp_refs:
                    r[...] = jnp.zeros_like(r)
            for r, v in zip(dp_refs, grads[n_tok:]):
                r[...] += v

    outs = pl.pallas_call(
        body, name=name, grid=(t // tb,),
        in_specs=[_tok_spec(tb, w) for w in tok] + [_par_spec(p) for p in par] + [_tok_spec(tb, c) for c in flat_cts],
        out_specs=[pl.BlockSpec((tb, w), lambda i: (i, 0)) for _, w, _ in d_outs] + [_par_spec(p) for p in par],
        out_shape=[S((t, w), dt) for _, w, dt in d_outs] + [S(p.shape, F32) for p in par],
        compiler_params=pltpu.CompilerParams(dimension_semantics=("arbitrary",)),
    )(*[w.arr for w in tok], *par, *[c.arr for c in flat_cts])
    return outs[:n_d], outs[n_d:]


def _sigmoid(x):
    return 1.0 / (1.0 + jnp.exp(-x))


def _silu(x):
    return x * _sigmoid(x)


def _ln(x, g, b):
    mu = jnp.mean(x, axis=-1, keepdims=True)
    var = jnp.mean(jnp.square(x - mu), axis=-1, keepdims=True)
    return (x - mu) * lax.rsqrt(var + LN_EPS) * g + b


def f_glu(ua, ug):
    return (ua * _sigmoid(ug),)


def f_ln_silu(a1, g, b):
    return (_silu(_ln(a1, g, b)),)


def f_silu3(a, b, c):
    return (_silu(a), _silu(b), _silu(c))


def f_softplus(dt_raw, bias):
    return (jax.nn.softplus(dt_raw + bias),)


def f_gate_rms(yssd, z, g):
    y = yssd * _silu(z)
    return (y * lax.rsqrt(jnp.mean(jnp.square(y), axis=-1, keepdims=True) + LN_EPS) * g,)


def f_ln(pre, g, b):
    return (_ln(pre, g, b),)


def f_mul(a, b):
    return (a * b,)


def f_gate_mul(pp, gt):
    return (pp * _sigmoid(gt),)


def conv_fwd(name, x, w, b):
    x = _win(x)
    t, c = x.arr.shape[0], x.w
    kw = w.shape[0]
    cb = LANE
    off = x.idx * (c // cb)
    rows = min(CONV_ROWS, t)
    has_b = b is not None

    def body(*refs):
        if has_b:
            x_ref, w_ref, b_ref, y_ref, xp_ref = refs
        else:
            x_ref, w_ref, y_ref, xp_ref = refs
        xp_ref[0:CONV_PAD, :] = jnp.zeros((CONV_PAD, cb), F32)
        xp_ref[CONV_PAD:CONV_PAD + t, :] = x_ref[...]

        def step(s, carry):
            base = pl.multiple_of(s * rows, rows)
            acc = jnp.zeros((rows, cb), F32)
            if has_b:
                acc = acc + b_ref[...]
            for k in range(kw):
                acc = acc + w_ref[k:k + 1, :] * xp_ref[pl.ds(base + CONV_PAD - (kw - 1) + k, rows), :]
            y_ref[pl.ds(base, rows), :] = acc
            return carry

        lax.fori_loop(0, t // rows, step, 0)

    in_specs = [pl.BlockSpec((t, cb), lambda j: (0, off + j)), pl.BlockSpec((kw, cb), lambda j: (0, j))]
    args = [x.arr, w]
    if has_b:
        in_specs.append(pl.BlockSpec((1, cb), lambda j: (0, j)))
        args.append(b)
    return pl.pallas_call(
        body, name=name, grid=(c // cb,), in_specs=in_specs,
        out_specs=pl.BlockSpec((t, cb), lambda j: (0, j)), out_shape=S((t, c), F32),
        scratch_shapes=[pltpu.VMEM((CONV_PAD + t, cb), F32)],
        compiler_params=pltpu.CompilerParams(dimension_semantics=("parallel",)),
    )(*args)


def conv_bwd(name, x, dy, w, dx_dtype=F32):
    x, dy = _win(x), _win(dy)
    t, c = x.arr.shape[0], x.w
    kw = w.shape[0]
    cb = LANE
    xoff = x.idx * (c // cb)
    dyoff = dy.idx * (c // cb)
    rows = min(CONV_ROWS, t)

    def body(x_ref, dy_ref, w_ref, dx_ref, dw_ref, db_ref, xp_ref, dyp_ref):
        xp_ref[0:CONV_PAD, :] = jnp.zeros((CONV_PAD, cb), F32)
        xp_ref[CONV_PAD:CONV_PAD + t, :] = x_ref[...]
        dyp_ref[0:t, :] = dy_ref[...]
        dyp_ref[t:t + CONV_PAD, :] = jnp.zeros((CONV_PAD, cb), F32)

        def fold(v):
            return jnp.sum(v.reshape(rows // SUBLANE, SUBLANE, cb), axis=0)

        def step(s, carry):
            base = pl.multiple_of(s * rows, rows)
            dyc = dy_ref[pl.ds(base, rows), :]
            acc = jnp.zeros((rows, cb), F32)
            new = []
            for k in range(kw):
                acc = acc + w_ref[k:k + 1, :] * dyp_ref[pl.ds(base + (kw - 1) - k, rows), :]
                new.append(carry[k] + fold(dyc * xp_ref[pl.ds(base + CONV_PAD - (kw - 1) + k, rows), :]))
            new.append(carry[kw] + fold(dyc))
            dx_ref[pl.ds(base, rows), :] = acc.astype(dx_ref.dtype)
            return tuple(new)

        init = tuple(jnp.zeros((SUBLANE, cb), F32) for _ in range(kw + 1))
        parts = lax.fori_loop(0, t // rows, step, init)
        for k in range(kw):
            dw_ref[k:k + 1, :] = jnp.sum(parts[k], axis=0, keepdims=True)
        db_ref[...] = jnp.sum(parts[kw], axis=0, keepdims=True)

    return pl.pallas_call(
        body, name=name, grid=(c // cb,),
        in_specs=[pl.BlockSpec((t, cb), lambda j: (0, xoff + j)), pl.BlockSpec((t, cb), lambda j: (0, dyoff + j)),
                  pl.BlockSpec((kw, cb), lambda j: (0, j))],
        out_specs=[pl.BlockSpec((t, cb), lambda j: (0, j)), pl.BlockSpec((kw, cb), lambda j: (0, j)),
                   pl.BlockSpec((1, cb), lambda j: (0, j))],
        out_shape=[S((t, c), dx_dtype), S((kw, c), F32), S((1, c), F32)],
        scratch_shapes=[pltpu.VMEM((CONV_PAD + t, cb), F32), pltpu.VMEM((CONV_PAD + t, cb), F32)],
        compiler_params=pltpu.CompilerParams(dimension_semantics=("parallel",)),
    )(x.arr, dy.arr, w)


def gated_conv_fwd(name, hpre, w, b, out_dtype):
    t, c2 = hpre.shape
    ff = c2 // 2
    kw = w.shape[0]
    cb = LANE
    nb = ff // cb
    rows = min(CONV_ROWS, t)

    def body(h1_ref, h2_ref, w1_ref, w2_ref, b1_ref, b2_ref, y_ref, xp1_ref, xp2_ref):
        for xp_ref, h_ref in ((xp1_ref, h1_ref), (xp2_ref, h2_ref)):
            xp_ref[0:CONV_PAD, :] = jnp.zeros((CONV_PAD, cb), F32)
            xp_ref[CONV_PAD:CONV_PAD + t, :] = h_ref[...]

        def step(s, carry):
            base = pl.multiple_of(s * rows, rows)
            h1 = jnp.zeros((rows, cb), F32) + b1_ref[...]
            h2 = jnp.zeros((rows, cb), F32) + b2_ref[...]
            for k in range(kw):
                at = pl.ds(base + CONV_PAD - (kw - 1) + k, rows)
                h1 = h1 + w1_ref[k:k + 1, :] * xp1_ref[at, :]
                h2 = h2 + w2_ref[k:k + 1, :] * xp2_ref[at, :]
            y_ref[pl.ds(base, rows), :] = (_silu(h1) * h2).astype(y_ref.dtype)
            return carry

        lax.fori_loop(0, t // rows, step, 0)

    col1 = lambda r: pl.BlockSpec((r, cb), lambda j: (0, j))
    col2 = lambda r: pl.BlockSpec((r, cb), lambda j: (0, nb + j))
    return pl.pallas_call(
        body, name=name, grid=(nb,),
        in_specs=[col1(t), col2(t), col1(kw), col2(kw), col1(1), col2(1)],
        out_specs=col1(t), out_shape=S((t, ff), out_dtype),
        scratch_shapes=[pltpu.VMEM((CONV_PAD + t, cb), F32)] * 2,
        compiler_params=pltpu.CompilerParams(dimension_semantics=("parallel",)),
    )(hpre, hpre, w, w, b, b)


def gated_conv_bwd(name, hpre, dact, w, b, dx_dtype):
    t, c2 = hpre.shape
    ff = c2 // 2
    kw = w.shape[0]
    cb = LANE
    nb = ff // cb
    rows = min(CONV_ROWS, t)

    def body(own_ref, oth_ref, da_ref, wo_ref, wt_ref, bo_ref, bt_ref, dx_ref, dw_ref, db_ref,
             xpo_ref, xpt_ref, dhp_ref):
        for xp_ref, h_ref in ((xpo_ref, own_ref), (xpt_ref, oth_ref)):
            xp_ref[0:CONV_PAD, :] = jnp.zeros((CONV_PAD, cb), F32)
            xp_ref[CONV_PAD:CONV_PAD + t, :] = h_ref[...]
        dhp_ref[t:t + CONV_PAD, :] = jnp.zeros((CONV_PAD, cb), F32)

        def fold(v):
            return jnp.sum(v.reshape(rows // SUBLANE, SUBLANE, cb), axis=0)

        def first_pass(own_is_gate):
            def step(s, carry):
                base = pl.multiple_of(s * rows, rows)
                ho = jnp.zeros((rows, cb), F32) + bo_ref[...]
                ht = jnp.zeros((rows, cb), F32) + bt_ref[...]
                for k in range(kw):
                    at = pl.ds(base + CONV_PAD - (kw - 1) + k, rows)
                    ho = ho + wo_ref[k:k + 1, :] * xpo_ref[at, :]
                    ht = ht + wt_ref[k:k + 1, :] * xpt_ref[at, :]
                da = da_ref[pl.ds(base, rows), :]
                if own_is_gate:
                    sg = _sigmoid(ho)
                    dh = da * ht * (sg * (1.0 + ho * (1.0 - sg)))
                else:
                    dh = da * _silu(ht)
                dhp_ref[pl.ds(base, rows), :] = dh
                new = [carry[k] + fold(dh * xpo_ref[pl.ds(base + CONV_PAD - (kw - 1) + k, rows), :]) for k in range(kw)]
                new.append(carry[kw] + fold(dh))
                return tuple(new)

            init = tuple(jnp.zeros((SUBLANE, cb), F32) for _ in range(kw + 1))
            parts = lax.fori_loop(0, t // rows, step, init)
            for k in range(kw):
                dw_ref[k:k + 1, :] = jnp.sum(parts[k], axis=0, keepdims=True)
            db_ref[...] = jnp.sum(parts[kw], axis=0, keepdims=True)

        half = pl.program_id(0)

        @pl.when(half == 0)
        def _():
            first_pass(True)

        @pl.when(half == 1)
        def _():
            first_pass(False)

        def second(s, carry):
            base = pl.multiple_of(s * rows, rows)
            acc = jnp.zeros((rows, cb), F32)
            for k in range(kw):
                acc = acc + wo_ref[k:k + 1, :] * dhp_ref[pl.ds(base + (kw - 1) - k, rows), :]
            dx_ref[pl.ds(base, rows), :] = acc.astype(dx_ref.dtype)
            return carry

        lax.fori_loop(0, t // rows, second, 0)

    own = lambda r: pl.BlockSpec((r, cb), lambda h, j: (0, h * nb + j))
    oth = lambda r: pl.BlockSpec((r, cb), lambda h, j: (0, (1 - h) * nb + j))
    return pl.pallas_call(
        body, name=name, grid=(2, nb),
        in_specs=[own(t), oth(t), pl.BlockSpec((t, cb), lambda h, j: (0, j)), own(kw), oth(kw), own(1), oth(1)],
        out_specs=[own(t), own(kw), own(1)],
        out_shape=[S((t, c2), dx_dtype), S((kw, c2), F32), S((1, c2), F32)],
        scratch_shapes=[pltpu.VMEM((CONV_PAD + t, cb), F32)] * 3,
        compiler_params=pltpu.CompilerParams(dimension_semantics=("parallel", "parallel")),
    )(hpre, hpre, dact, w, w, b, b)


def _bdot(a, b, ca, cb):
    return lax.dot_general(a.astype(MXU_DTYPE), b.astype(MXU_DTYPE), (((ca,), (cb,)), ((0,), (0,))),
                           preferred_element_type=F32)


@jax.custom_vjp
def bmm_nn(a, b):
    return _bdot(a, b, 2, 1)


bmm_nn.defvjp(lambda a, b: (_bdot(a, b, 2, 1), (a, b)),
              lambda r, g: (_bdot(g, r[1], 2, 2), _bdot(r[0], g, 1, 1)))


@jax.custom_vjp
def bmm_tn(a, b):
    return _bdot(a, b, 1, 1)


bmm_tn.defvjp(lambda a, b: (_bdot(a, b, 1, 1), (a, b)),
              lambda r, g: (_bdot(r[1], g, 2, 2), _bdot(r[0], g, 2, 1)))


@jax.custom_vjp
def bmm_nt(a, b):
    return _bdot(a, b, 2, 2)


bmm_nt.defvjp(lambda a, b: (_bdot(a, b, 2, 2), (a, b)),
              lambda r, g: (_bdot(g, r[1], 2, 1), _bdot(g, r[0], 1, 1)))


def ssd_chunk(x, dt, dt_row, bm, cm, hprev, a_log, dsk):
    hg, ln, _ = x.shape
    n = bm.shape[1]
    ii = lax.broadcasted_iota(jnp.int32, (ln, ln), 0)
    jj = lax.broadcasted_iota(jnp.int32, (ln, ln), 1)
    tril, triu = (ii >= jj)[None], (ii <= jj)[None]
    a = -jnp.exp(a_log)
    da = dt * a
    da_row = dt_row * a
    cum_c = jnp.sum(jnp.where(tril, da_row, 0.0), axis=2, keepdims=True)
    cum_r = jnp.sum(jnp.where(triu, da, 0.0), axis=1, keepdims=True)
    last = jnp.sum(da, axis=1, keepdims=True)
    decay = jnp.where(tril, jnp.exp(jnp.where(tril, cum_c - cum_r, 0.0)), 0.0)
    cb = bmm_nt(cm[None], bm[None])
    y_diag = bmm_nn(cb * decay * dt_row, x)
    bb = jnp.broadcast_to(bm[None], (hg, ln, n))
    cc = jnp.broadcast_to(cm[None], (hg, ln, n))
    states = bmm_tn(x * (jnp.exp(last - cum_c) * dt), bb)
    y_off = bmm_nt(cc, hprev) * jnp.exp(cum_c)
    hnew = hprev * jnp.exp(last) + states
    return y_diag + y_off + dsk * x, hnew


def _ssd_dims(xs, bm, a_log):
    t = xs.shape[0]
    h = a_log.shape[0]
    return h, t, xs.shape[1] // h, h // N_GROUPS, bm.shape[1] // N_GROUPS, t // CHUNK


def _heads_of(ref, g, hg, p):
    return jnp.stack([ref[:, (g * hg + i) * p:(g * hg + i + 1) * p] for i in range(hg)])


def _cols_of(ref, g, hg):
    return jnp.stack([ref[:, g * hg + i:g * hg + i + 1] for i in range(hg)])


def dt_rows(dt, h):
    t = dt.shape[0]
    return dt[:, :h].T.reshape(h, t // CHUNK, 1, CHUNK).transpose(1, 0, 2, 3)


def dt_cols(dtr, lanes):
    nc, h, _, ln = dtr.shape
    return jnp.pad(dtr.transpose(1, 0, 2, 3).reshape(h, nc * ln).T, ((0, 0), (0, lanes - h)))


def ssd_fwd(name, xs, dt, dtr, bm, cm, a_log, dsk):
    h, t, p, hg, n, nc = _ssd_dims(xs, bm, a_log)

    def body(al_ref, dk_ref, x_ref, dt_ref, dtr_ref, b_ref, c_ref, y_ref, hp_ref, h_scr):
        @pl.when(pl.program_id(0) == 0)
        def _():
            h_scr[...] = jnp.zeros_like(h_scr)

        for g in range(N_GROUPS):
            hs, ns = slice(g * hg, (g + 1) * hg), slice(g * n, (g + 1) * n)
            hprev = h_scr[hs]
            hp_ref[hs, 0] = hprev
            y, hnew = ssd_chunk(_heads_of(x_ref, g, hg, p), _cols_of(dt_ref, g, hg), dtr_ref[0, hs], b_ref[:, ns],
                                c_ref[:, ns], hprev, al_ref[hs], dk_ref[hs])
            for i in range(hg):
                y_ref[:, (g * hg + i) * p:(g * hg + i + 1) * p] = y[i]
            h_scr[hs] = hnew

    head = pl.BlockSpec((h, 1, 1), lambda c: (0, 0, 0))
    row = lambda w: pl.BlockSpec((CHUNK, w), lambda c: (c, 0))
    return pl.pallas_call(
        body, name=name, grid=(nc,),
        in_specs=[head, head, row(h * p), row(dt.shape[1]), pl.BlockSpec((1, h, 1, CHUNK), lambda c: (c, 0, 0, 0)),
                  row(N_GROUPS * n), row(N_GROUPS * n)],
        out_specs=[row(h * p), pl.BlockSpec((h, 1, p, n), lambda c: (0, c, 0, 0))],
        out_shape=[S((t, h * p), F32), S((h, nc, p, n), F32)],
        scratch_shapes=[pltpu.VMEM((h, p, n), F32)],
        compiler_params=pltpu.CompilerParams(dimension_semantics=("arbitrary",)),
    )(a_log, dsk, xs, dt, dtr, bm, cm)


def ssd_bwd(name, xs, dt, dtr, bm, cm, a_log, dsk, hp, dy):
    h, t, p, hg, n, nc = _ssd_dims(xs, bm, a_log)

    def body(al_ref, dk_ref, x_ref, dt_ref, dtr_ref, b_ref, c_ref, hp_ref, dy_ref,
             dx_ref, ddt_ref, ddtr_ref, db_ref, dc_ref, dal_ref, ddk_ref, dh_scr):
        @pl.when(pl.program_id(0) == 0)
        def _():
            dh_scr[...] = jnp.zeros_like(dh_scr)
            dal_ref[...] = jnp.zeros_like(dal_ref)
            ddk_ref[...] = jnp.zeros_like(ddk_ref)

        ddt_ref[...] = jnp.zeros_like(ddt_ref)
        for g in range(N_GROUPS):
            hs, ns = slice(g * hg, (g + 1) * hg), slice(g * n, (g + 1) * n)
            _, vjp = jax.vjp(ssd_chunk, _heads_of(x_ref, g, hg, p), _cols_of(dt_ref, g, hg), dtr_ref[0, hs],
                             b_ref[:, ns], c_ref[:, ns], hp_ref[hs, 0], al_ref[hs], dk_ref[hs])
            gx, gdt, gdtr, gb, gc, ghp, gal, gdk = vjp((_heads_of(dy_ref, g, hg, p), dh_scr[hs]))
            for i in range(hg):
                dx_ref[:, (g * hg + i) * p:(g * hg + i + 1) * p] = gx[i]
                ddt_ref[:, g * hg + i:g * hg + i + 1] = gdt[i]
            ddtr_ref[0, hs] = gdtr
            db_ref[:, ns] = gb
            dc_ref[:, ns] = gc
            dh_scr[hs] = ghp
            dal_ref[hs] += gal
            ddk_ref[hs] += gdk

    head = pl.BlockSpec((h, 1, 1), lambda c: (0, 0, 0))
    row = lambda w: pl.BlockSpec((CHUNK, w), lambda c: (nc - 1 - c, 0))
    rows = pl.BlockSpec((1, h, 1, CHUNK), lambda c: (nc - 1 - c, 0, 0, 0))
    return pl.pallas_call(
        body, name=name, grid=(nc,),
        in_specs=[head, head, row(h * p), row(dt.shape[1]), rows, row(N_GROUPS * n), row(N_GROUPS * n),
                  pl.BlockSpec((h, 1, p, n), lambda c: (0, nc - 1 - c, 0, 0)), row(h * p)],
        out_specs=[row(h * p), row(dt.shape[1]), rows, row(N_GROUPS * n), row(N_GROUPS * n), head, head],
        out_shape=[S((t, h * p), F32), S(dt.shape, F32), S(dtr.shape, F32), S(bm.shape, F32), S(cm.shape, F32),
                   S((h, 1, 1), F32), S((h, 1, 1), F32)],
        scratch_shapes=[pltpu.VMEM((h, p, n), F32)],
        compiler_params=pltpu.CompilerParams(dimension_semantics=("arbitrary",)),
    )(a_log, dsk, xs, dt, dtr, bm, cm, hp, dy)


def _alpha(depth):
    return (2.0 * depth) ** 0.25


def _pad_lanes(v):
    return jnp.pad(v, ((0, 0), (0, LANE - v.shape[1])))


def split_even_weights(w, j):
    d = w["e_w_in"][j].shape[1]
    da = w["e_conv_a_w"].shape[2]
    db = w["e_norm_b_g"].shape[1]
    gn = N_GROUPS * N_STATE
    nh = w["e_dt_bias"].shape[1]
    main = 2 * da + 2 * db + 2 * gn
    win = w["e_w_in"][j]
    ox = 2 * da + db
    cw, cbias = w["e_conv_b_w"][j], w["e_conv_b_b"][j][None]
    return dict(
        d=d, da=da, db=db, gn=gn, nh=nh, main=main,
        win_main=win[:main], win_dt=jnp.pad(win[main:], ((0, LANE - nh), (0, 0))),
        caw=w["e_conv_a_w"][j], cab=w["e_conv_a_b"][j][None], lag=w["e_ln_a_g"][j][None], lab=w["e_ln_a_b"][j][None],
        cw_xs=cw[:, :db], cw_b=cw[:, db:db + gn], cw_c=cw[:, db + gn:],
        cb_xs=cbias[:, :db], cb_b=cbias[:, db:db + gn], cb_c=cbias[:, db + gn:],
        dt_bias=_pad_lanes(w["e_dt_bias"][j][None]), a_log=w["e_a_log"][j].reshape(nh, 1, 1),
        dsk=w["e_d_skip"][j].reshape(nh, 1, 1), norm_g=w["e_norm_b_g"][j][None],
        wout_a=w["e_w_out"][j][:da], wout_b=w["e_w_out"][j][da:],
    )


def even_fwd(tag, x, xm, lw, ln_g, ln_b, alpha):
    t = x.shape[0]
    da, db, gn, nh = lw["da"], lw["db"], lw["gn"], lw["nh"]
    u = mm(tag + "_win", xm, lw["win_main"], "nt")
    udt = mm(tag + "_windt", xm, lw["win_dt"], "nt")
    ua, ug, z, xs_pre = Win(u, da, 0), Win(u, da, 1), Win(u, db, 2 * da // db), Win(u, db, (2 * da + db) // db)
    b_pre, c_pre = Win(u, gn, (2 * da + 2 * db) // gn), Win(u, gn, (2 * da + 2 * db + gn) // gn)
    (a0,) = rowwise(tag + "_glu", f_glu, [ua, ug], [], [da])
    a1 = conv_fwd(tag + "_conva", a0, lw["caw"], lw["cab"])
    (ya,) = rowwise(tag + "_lna", f_ln_silu, [a1], [lw["lag"], lw["lab"]], [da], out_dtypes=[MXU_DTYPE])
    xs_c = conv_fwd(tag + "_convxs", xs_pre, lw["cw_xs"], lw["cb_xs"])
    b_c = conv_fwd(tag + "_convb", b_pre, lw["cw_b"], lw["cb_b"])
    c_c = conv_fwd(tag + "_convc", c_pre, lw["cw_c"], lw["cb_c"])
    xs, bm, cm = rowwise(tag + "_silu3", f_silu3, [xs_c, b_c, c_c], [], [db, gn, gn])
    (dt,) = rowwise(tag + "_dt", f_softplus, [udt], [lw["dt_bias"]], [LANE])
    dtr = dt_rows(dt, nh)
    yssd, hp = ssd_fwd(tag + "_ssd", xs, dt, dtr, bm, cm, lw["a_log"], lw["dsk"])
    (yb,) = rowwise(tag + "_gate", f_gate_rms, [yssd, z], [lw["norm_g"]], [db], out_dtypes=[MXU_DTYPE])
    ma = mm(tag + "_wouta", ya, lw["wout_a"], "nn")
    mb = mm(tag + "_woutb", yb, lw["wout_b"], "nn")

    def f_res(xv, mav, mbv, g, b):
        pre = alpha * xv + mav + mbv
        y = _ln(pre, g, b)
        return y, y, pre

    x1, x1m, pre = rowwise(tag + "_res", f_res, [x, ma, mb], [ln_g, ln_b], [x.shape[1]] * 3,
                           out_dtypes=[F32, MXU_DTYPE, F32])
    saved = dict(xm=xm, u=u, udt=udt, a0=a0, a1=a1, ya=ya, xs_c=xs_c, b_c=b_c, c_c=c_c, xs=xs, dt=dt, dtr=dtr, bm=bm,
                 cm=cm, hp=hp, yssd=yssd, yb=yb, pre=pre)
    return x1, x1m, saved


def even_bwd(tag, dx1_pieces, sv, lw, ln_g, ln_b, alpha):
    t = sv["u"].shape[0]
    da, db, gn, nh = lw["da"], lw["db"], lw["gn"], lw["nh"]
    u, xm = sv["u"], sv["xm"]
    mx = (MXU_DTYPE,)
    ua, ug, z, xs_pre = Win(u, da, 0), Win(u, da, 1), Win(u, db, 2 * da // db), Win(u, db, (2 * da + db) // db)
    b_pre, c_pre = Win(u, gn, (2 * da + 2 * db) // gn), Win(u, gn, (2 * da + 2 * db + gn) // gn)
    (dpre, dprem), (dg0, db0) = rowwise_bwd(tag + "_res_b", f_ln, [sv["pre"]], [ln_g, ln_b], [dx1_pieces],
                                            d_dtypes=[(F32, MXU_DTYPE)])
    dya = mm(tag + "_dya", dprem, lw["wout_a"], "nt")
    dyb = mm(tag + "_dyb", dprem, lw["wout_b"], "nt")
    dwout_a = mm(tag + "_dwouta", sv["ya"], dprem, "tn", EXCHANGE_DTYPE)
    dwout_b = mm(tag + "_dwoutb", sv["yb"], dprem, "tn", EXCHANGE_DTYPE)
    (dyssd, dz), (dnorm_g,) = rowwise_bwd(tag + "_gate_b", f_gate_rms, [sv["yssd"], z], [lw["norm_g"]], [[dyb]],
                                          d_dtypes=[(F32,), mx])
    dxs, ddt, ddtr, dbm, dcm, dalog, ddsk = ssd_bwd(tag + "_ssd_b", sv["xs"], sv["dt"], sv["dtr"], sv["bm"], sv["cm"],
                                                    lw["a_log"], lw["dsk"], sv["hp"], dyssd)
    ddt_pieces = [ddt, dt_cols(ddtr, ddt.shape[1])]
    (dudt,), (ddt_bias,) = rowwise_bwd(tag + "_dt_b", f_softplus, [sv["udt"]], [lw["dt_bias"]], [ddt_pieces],
                                       d_dtypes=[mx])
    (dxs_c, db_c, dc_c), _ = rowwise_bwd(tag + "_silu3_b", f_silu3, [sv["xs_c"], sv["b_c"], sv["c_c"]], [],
                                         [[dxs], [dbm], [dcm]])
    dxs_pre, dcw_xs, dcb_xs = conv_bwd(tag + "_convxs_b", xs_pre, dxs_c, lw["cw_xs"], MXU_DTYPE)
    db_pre, dcw_b, dcb_b = conv_bwd(tag + "_convb_b", b_pre, db_c, lw["cw_b"], MXU_DTYPE)
    dc_pre, dcw_c, dcb_c = conv_bwd(tag + "_convc_b", c_pre, dc_c, lw["cw_c"], MXU_DTYPE)
    (da1,), (dlag, dlab) = rowwise_bwd(tag + "_lna_b", f_ln_silu, [sv["a1"]], [lw["lag"], lw["lab"]], [[dya]])
    da0, dcaw, dcab = conv_bwd(tag + "_conva_b", sv["a0"], da1, lw["caw"])
    (dua, dug), _ = rowwise_bwd(tag + "_glu_b", f_glu, [ua, ug], [], [[da0]], d_dtypes=[mx, mx])
    du = jnp.concatenate([dua, dug, dz, dxs_pre, db_pre, dc_pre], axis=1)
    dx_m = mm(tag + "_dxm", du, lw["win_main"], "nn")
    dx_dt = mm(tag + "_dxdt", dudt, lw["win_dt"], "nn")
    dwin_main = mm(tag + "_dwin", du, xm, "tn", EXCHANGE_DTYPE)
    dwin_dt = mm(tag + "_dwindt", dudt, xm, "tn", EXCHANGE_DTYPE)
    grads = dict(
        e_w_in=jnp.concatenate([dwin_main, dwin_dt[:nh]], axis=0),
        e_conv_a_w=dcaw, e_conv_a_b=dcab[0], e_ln_a_g=dlag[0], e_ln_a_b=dlab[0],
        e_conv_b_w=jnp.concatenate([dcw_xs, dcw_b, dcw_c], axis=1),
        e_conv_b_b=jnp.concatenate([dcb_xs, dcb_b, dcb_c], axis=1)[0],
        e_dt_bias=ddt_bias[0, :nh], e_a_log=dalog.reshape(nh), e_d_skip=ddsk.reshape(nh), e_norm_b_g=dnorm_g[0],
        e_w_out=jnp.concatenate([dwout_a, dwout_b], axis=0), ln_g0=dg0[0], ln_b0=db0[0],
    )
    return [Win(dpre, coef=alpha), dx_m, dx_dt], grads


def odd_fwd(tag, x, xm, w, j, ln_g, ln_b, alpha):
    d = x.shape[1]
    u = mm(tag + "_win", xm, w["o_w_in"][j], "nt")
    bg, cg, v = Win(u, d, 0), Win(u, d, 1), Win(u, d, 2)
    (s,) = rowwise(tag + "_cv", f_mul, [cg, v], [], [d])
    cs = conv_fwd(tag + "_conv", s, w["o_conv_w"][j], None)
    (m,) = rowwise(tag + "_bm", f_mul, [bg, cs], [], [d], out_dtypes=[MXU_DTYPE])
    mix = mm(tag + "_wout", m, w["o_w_out"][j], "nn")

    def f_res(xv, mv, g, b):
        pre = alpha * xv + mv
        y = _ln(pre, g, b)
        return y, y, pre

    x1, x1m, pre = rowwise(tag + "_res", f_res, [x, mix], [ln_g, ln_b], [d] * 3, out_dtypes=[F32, MXU_DTYPE, F32])
    return x1, x1m, dict(xm=xm, u=u, s=s, cs=cs, m=m, pre=pre)


def odd_bwd(tag, dx1_pieces, sv, w, j, ln_g, ln_b, alpha):
    xm, u = sv["xm"], sv["u"]
    d = xm.shape[1]
    mx = (MXU_DTYPE,)
    bg, cg, v = Win(u, d, 0), Win(u, d, 1), Win(u, d, 2)
    (dpre, dprem), (dg0, db0) = rowwise_bwd(tag + "_res_b", f_ln, [sv["pre"]], [ln_g, ln_b], [dx1_pieces],
                                            d_dtypes=[(F32, MXU_DTYPE)])
    dm = mm(tag + "_dm", dprem, w["o_w_out"][j], "nt")
    dwout = mm(tag + "_dwout", sv["m"], dprem, "tn", EXCHANGE_DTYPE)
    (dbg, dcs), _ = rowwise_bwd(tag + "_bm_b", f_mul, [bg, sv["cs"]], [], [[dm]], d_dtypes=[mx, (F32,)])
    ds, dcw, _ = conv_bwd(tag + "_conv_b", sv["s"], dcs, w["o_conv_w"][j])
    (dcg, dv), _ = rowwise_bwd(tag + "_cv_b", f_mul, [cg, v], [], [[ds]], d_dtypes=[mx, mx])
    du = jnp.concatenate([dbg, dcg, dv], axis=1)
    dx_u = mm(tag + "_dx", du, w["o_w_in"][j], "nn")
    dwin = mm(tag + "_dwin", du, xm, "tn", EXCHANGE_DTYPE)
    grads = dict(o_w_in=dwin, o_conv_w=dcw, o_w_out=dwout, ln_g0=dg0[0], ln_b0=db0[0])
    return [Win(dpre, coef=alpha), dx_u], grads


def ffn_fwd(tag, x1, x1m, p_i, w, i, ln_g, ln_b, alpha):
    d = x1.shape[1]
    hpre = mm(tag + "_wup", x1m, w["f_w_up"][i], "nt")
    act = gated_conv_fwd(tag + "_fgate", hpre, w["f_conv_w"][i], w["f_conv_b"][i][None], MXU_DTYPE)
    ffn = mm(tag + "_wdown", act, w["f_w_down"][i], "nn")
    pp = mm(tag + "_pproj", p_i, w["ple_w_proj"][i], "nt")
    gt = mm(tag + "_pgate", x1m, w["ple_w_gate"][i], "nn")

    def f_res2(xv, fv, ppv, gtv, g, b):
        pre = alpha * xv + fv + ppv * _sigmoid(gtv)
        y = _ln(pre, g, b)
        return y, y, pre

    x2, x2m, pre = rowwise(tag + "_res2", f_res2, [x1, ffn, pp, gt], [ln_g, ln_b], [d] * 3,
                           out_dtypes=[F32, MXU_DTYPE, F32])
    return x2, x2m, dict(x1m=x1m, hpre=hpre, act=act, pp=pp, gt=gt, pre=pre)


def ffn_bwd(tag, dx2_pieces, sv, p_i, w, i, ln_g, ln_b, alpha):
    x1m = sv["x1m"]
    mx = (MXU_DTYPE,)
    (dpre, dprem), (dg1, db1) = rowwise_bwd(tag + "_res2_b", f_ln, [sv["pre"]], [ln_g, ln_b], [dx2_pieces],
                                            d_dtypes=[(F32, MXU_DTYPE)])
    (dpp, dgt), _ = rowwise_bwd(tag + "_pg_b", f_gate_mul, [sv["pp"], sv["gt"]], [], [[dpre]], d_dtypes=[mx, mx])
    dwproj = mm(tag + "_dwproj", dpp, p_i, "tn", EXCHANGE_DTYPE)
    dwgate = mm(tag + "_dwgate", x1m, dgt, "tn", EXCHANGE_DTYPE)
    dx1_a = mm(tag + "_dx1a", dgt, w["ple_w_gate"][i], "nt")
    dact = mm(tag + "_dact", dprem, w["f_w_down"][i], "nt")
    dwdown = mm(tag + "_dwdown", sv["act"], dprem, "tn", EXCHANGE_DTYPE)
    dhpre, dfcw, dfcb = gated_conv_bwd(tag + "_fgate_b", sv["hpre"], dact, w["f_conv_w"][i], w["f_conv_b"][i][None],
                                       MXU_DTYPE)
    dwup = mm(tag + "_dwup", dhpre, x1m, "tn", EXCHANGE_DTYPE)
    dx1_b = mm(tag + "_dx1b", dhpre, w["f_w_up"][i], "nn")
    grads = dict(f_w_up=dwup, f_conv_w=dfcw, f_conv_b=dfcb[0], f_w_down=dwdown, ple_w_proj=dwproj, ple_w_gate=dwgate,
                 ln_g1=dg1[0], ln_b1=db1[0])
    return [Win(dpre, coef=alpha), dx1_a, dx1_b], grads


def local_step(x, p, w, target, fetch=None, emit=None):
    depth = w["ln_g"].shape[0]
    alpha = _alpha(depth)
    d = x.shape[1]
    saved = []
    h = hm = x
    for i in range(depth):
        j = i // 2
        if fetch is not None:
            fetch(i, "m", h)
        g0, b0, g1, b1 = w["ln_g"][i, 0][None], w["ln_b"][i, 0][None], w["ln_g"][i, 1][None], w["ln_b"][i, 1][None]
        tag = "l%d" % i
        if i % 2 == 0:
            lw = split_even_weights(w, j)
            h, hm, sv_m = even_fwd(tag, h, hm, lw, g0, b0, alpha)
        else:
            lw = None
            h, hm, sv_m = odd_fwd(tag, h, hm, w, j, g0, b0, alpha)
        if fetch is not None:
            fetch(i, "f", h)
        h, hm, sv_f = ffn_fwd(tag, h, hm, p[i], w, i, g1, b1, alpha)
        saved.append((lw, sv_m, sv_f, (g0, b0, g1, b1)))

    def f_loss(xf, tg):
        diff = xf - tg
        sq = jnp.sum(jnp.sum(jnp.square(diff), axis=1, keepdims=True), axis=0, keepdims=True)
        return diff * (1.0 / d), jnp.broadcast_to(sq, (1, LANE))

    dxf, sq = rowwise("loss", f_loss, [h, target], [], [d], red_widths=[LANE])
    loss = sq[0, 0] * (0.5 / d)

    per_layer = []
    pieces = [dxf]
    token = None
    for i in reversed(range(depth)):
        j = i // 2
        lw, sv_m, sv_f, (g0, b0, g1, b1) = saved[i]
        tag = "l%d" % i
        pieces, gf = ffn_bwd(tag, pieces, sv_f, p[i], w, i, g1 if token is None else g1 + token, b1, alpha)
        if emit is not None:
            token = emit(tag + "f", i, gf)
        g0 = g0 if token is None else g0 + token
        if i % 2 == 0:
            pieces, gm = even_bwd(tag, pieces, sv_m, lw, g0, b0, alpha)
        else:
            pieces, gm = odd_bwd(tag, pieces, sv_m, w, j, g0, b0, alpha)
        if emit is not None:
            token = emit(tag + "m", j, gm)
        per_layer.append((i, gm, gf))

    def f_sum(*vs):
        acc = None
        for v, c in zip(vs, [pc.coef for pc in map(_win, pieces)]):
            v = v if c == 1.0 else v * c
            acc = v if acc is None else acc + v
        return (acc,)

    (grad_x,) = rowwise("grad_x", f_sum, [Win(_win(pc).arr) for pc in pieces], [], [d])

    by_layer = {i: (gm, gf) for i, gm, gf in per_layer}
    grads = {}
    n_even, n_odd = (depth + 1) // 2, depth // 2
    collect = lambda name, per_layer: per_layer if name in BIG else jnp.stack(per_layer)
    for name in ("e_w_in", "e_conv_a_w", "e_conv_a_b", "e_ln_a_g", "e_ln_a_b", "e_conv_b_w", "e_conv_b_b", "e_dt_bias",
                 "e_a_log", "e_d_skip", "e_norm_b_g", "e_w_out"):
        grads[name] = collect(name, [by_layer[2 * j][0][name] for j in range(n_even)])
    for name in ("o_w_in", "o_conv_w", "o_w_out"):
        grads[name] = collect(name, [by_layer[2 * j + 1][0][name] for j in range(n_odd)])
    for name in ("f_w_up", "f_conv_w", "f_conv_b", "f_w_down", "ple_w_proj", "ple_w_gate"):
        grads[name] = collect(name, [by_layer[i][1][name] for i in range(depth)])
    grads["ln_g"] = jnp.stack([jnp.stack([by_layer[i][0]["ln_g0"], by_layer[i][1]["ln_g1"]]) for i in range(depth)])
    grads["ln_b"] = jnp.stack([jnp.stack([by_layer[i][0]["ln_b0"], by_layer[i][1]["ln_b1"]]) for i in range(depth)])
    return loss, grad_x, grads


_ANY = pl.BlockSpec(memory_space=pl.ANY)
_MESH = pl.DeviceIdType.MESH


def all_gather(name, xl):
    r, c_ = xl.shape
    split = r // 2 // 16 * 16
    halves = ((0, split), (split, r - split)) if split else ((0, r),)
    two = len(halves) == 2

    def body(x_ref, out_ref, send_sems, recv_sems, local_sem):
        x, y, c = lax.axis_index("x"), lax.axis_index("y"), lax.axis_index("c")
        me, sibling, xn, yn, dg = (x, y, c), (x, y, 1 - c), (1 - x, y, c), (x, 1 - y, c), (1 - x, 1 - y, c)

        def rows(block, h):
            ref = out_ref.at[4 * block[0] + 2 * block[1] + block[2]]
            return ref if h is None else ref.at[pl.ds(*halves[h])]

        def copy(k, block, h, to, own=False):
            src = (x_ref if h is None else x_ref.at[pl.ds(*halves[h])]) if own else rows(block, h)
            return pltpu.make_async_remote_copy(src_ref=src, dst_ref=rows(block, h), send_sem=send_sems.at[k],
                                                recv_sem=recv_sems.at[k], device_id=to, device_id_type=_MESH)

        def other_core(block):
            return (block[0], block[1], 1 - c)

        mine = pltpu.make_async_copy(x_ref, rows(me, None), local_sem)
        mine.start()
        direct = [copy(0, me, 0, xn, own=True), copy(1, me, 1 if two else 0, yn, own=True)]
        if two:
            direct += [copy(2, me, 1, xn, own=True), copy(3, me, 0, yn, own=True)]
        direct.append(copy(6, me, None, sibling, own=True))
        for cp in direct:
            cp.start()
        started = list(direct)

        def then(waits, nxt):
            for cp in waits:
                cp.wait_recv()
            for cp in nxt:
                cp.start()
            started.extend(nxt)

        if two:
            then([copy(0, xn, 0, me)], [copy(4, xn, 0, yn)])
            then([copy(1, yn, 1, me)], [copy(5, yn, 1, xn)])
            then([copy(2, xn, 1, me)], [copy(7, xn, None, sibling)])
            then([copy(3, yn, 0, me)], [copy(8, yn, None, sibling)])
            then([copy(4, dg, 0, me), copy(5, dg, 1, me)], [copy(9, dg, None, sibling)])
        else:
            then([copy(0, xn, 0, me)], [copy(4, xn, 0, yn), copy(7, xn, None, sibling)])
            then([copy(1, yn, 0, me)], [copy(8, yn, None, sibling)])
            then([copy(4, dg, 0, me)], [copy(9, dg, None, sibling)])
        for k, block in ((6, me), (7, xn), (8, yn), (9, dg)):
            copy(k, other_core(block), None, me).wait_recv()
        for cp in started:
            cp.wait_send()
        mine.wait()

    return pl.pallas_call(
        body, name=name, out_shape=S((N_DEV, r, c_), xl.dtype), in_specs=[_ANY], out_specs=_ANY,
        scratch_shapes=[pltpu.SemaphoreType.DMA((10,)), pltpu.SemaphoreType.DMA((10,)), pltpu.SemaphoreType.DMA],
    )(xl)


def sum_devices(name, g8):
    _, r, c_ = g8.shape

    def body(g_ref, o_ref):
        acc = g_ref[0]
        for k in range(1, N_DEV):
            acc = acc + g_ref[k]
        o_ref[...] = acc

    return pl.pallas_call(body, name=name, out_shape=S((r, c_), F32))(g8)


def _flatten(parts, cols, row_mult):
    flat = jnp.concatenate([v.reshape(-1) for v in parts])
    n = flat.shape[0]
    rows = -(-n // (cols * row_mult)) * row_mult
    return jnp.pad(flat, (0, rows * cols - n)).reshape(rows, cols)


def _exchange_dims(name, lshape):
    l, r, c = lshape
    return (l, c, r) if name in TRANSPOSED else (l, r, c)


_HBM = pl.BlockSpec(memory_space=pltpu.HBM)
_SEM = pl.BlockSpec(memory_space=pltpu.SEMAPHORE)
_PEERS = [(dx, dy, dc) for dx in (0, 1) for dy in (0, 1) for dc in (0, 1)][1:]


def _peer_copies(scatter, src_refs, land_refs, send_sems, recv_sems):
    x, y, c = lax.axis_index("x"), lax.axis_index("y"), lax.axis_index("c")
    flip = lambda v, d: 1 - v if d else v
    copies = []
    for s_ref, land_ref in zip(src_refs, land_refs):
        for dx, dy, dc in _PEERS:
            px, py, pc = flip(x, dx), flip(y, dy), flip(c, dc)
            k = len(copies)
            copies.append(pltpu.make_async_remote_copy(
                src_ref=s_ref.at[4 * px + 2 * py + pc] if scatter else s_ref, dst_ref=land_ref.at[4 * x + 2 * y + c],
                send_sem=send_sems.at[k], recv_sem=recv_sems.at[k], device_id=(px, py, pc), device_id_type=_MESH))
    return copies


def split_start(name, srcs, scatter):
    n = len(srcs)
    lands = [lax.empty(s.shape if scatter else (N_DEV,) + s.shape, s.dtype) for s in srcs]
    n_copies = n * len(_PEERS)

    def body(*refs):
        send_sems, recv_sems, token = refs[2 * n], refs[2 * n + 1], refs[-1]
        for cp in _peer_copies(scatter, refs[:n], refs[n:2 * n], send_sems, recv_sems):
            cp.start()
        token[...] = jnp.zeros_like(token)

    return pl.pallas_call(
        body, name=name,
        out_shape=(pltpu.SemaphoreType.DMA((n_copies,)), pltpu.SemaphoreType.DMA((n_copies,)),
                   *[pltpu.HBM(v.shape, v.dtype) for v in srcs + lands], S((SUBLANE, LANE), F32)),
        in_specs=(_HBM,) * (2 * n), out_specs=(_SEM, _SEM) + (_HBM,) * (2 * n) + (pl.BlockSpec(memory_space=pltpu.VMEM),),
        input_output_aliases={i: 2 + i for i in range(2 * n)},
        compiler_params=pltpu.CompilerParams(has_side_effects=pltpu.SideEffectType.DATAFLOW_SIDE_EFFECTING),
    )(*[pltpu.with_memory_space_constraint(v, pltpu.HBM) for v in srcs + lands])


def split_wait(name, started, scatter, after):
    send_sems, recv_sems = started[0], started[1]
    thru = list(started[2:-1])
    n = len(thru) // 2

    def body(*refs):
        send_sems, recv_sems = refs[2 * n], refs[2 * n + 1]
        for cp in _peer_copies(scatter, refs[:n], refs[n:2 * n], send_sems, recv_sems):
            cp.wait_send()
            cp.wait_recv()

    outs = pl.pallas_call(
        body, name=name, out_shape=tuple(pltpu.HBM(v.shape, v.dtype) for v in thru),
        in_specs=(_HBM,) * (2 * n) + (_SEM, _SEM, _ANY), out_specs=(_HBM,) * (2 * n),
        input_output_aliases={i: i for i in range(2 * n)},
        compiler_params=pltpu.CompilerParams(has_side_effects=pltpu.SideEffectType.DATAFLOW_SIDE_EFFECTING),
    )(*thru, send_sems, recv_sems, after)
    return outs[:n], outs[n:]


def _layer_weights(depth):
    table = []
    for i in range(depth):
        mixer = ("e_w_out", "e_w_in") if i % 2 == 0 else ("o_w_out", "o_w_in")
        table.append([("f_w_up", i), ("f_w_down", i), ("ple_w_gate", i), ("ple_w_proj", i)]
                     + [(n, i // 2) for n in mixer])
    return table


def gather_big(local):
    cols = local["e_w_out"].shape[2]
    depth = local["ln_g"].shape[0]
    table = _layer_weights(depth)
    shard = {}
    for n in BIG:
        v = local[n].astype(MXU_DTYPE)
        shard[n] = v.transpose(0, 2, 1) if n in TRANSPOSED else v
    full = {n: [None] * local[n].shape[0] for n in BIG}
    dev = 4 * lax.axis_index("x") + 2 * lax.axis_index("y") + lax.axis_index("c")

    first = [(n, li) for n, li in table[0] if n.startswith("e_")]
    table[0] = [(n, li) for n, li in table[0] if not n.startswith("e_")]
    got, r0 = all_gather("ag_l0", jnp.concatenate([shard[n][li].reshape(-1, cols) for n, li in first], axis=0)), 0
    for n, li in first:
        _, a, b = shard[n].shape
        per = a * b // cols
        full[n][li] = got[:, r0:r0 + per].reshape(N_DEV * a, b)
        r0 += per
    started = [split_start("ag_start_l%d" % i, [shard[n][li] for n, li in table[i]], False) for i in range(depth)]

    def fetch(i, part, after):
        if part == ("f" if i == 0 else "m"):
            srcs, lands = split_wait("ag_wait_l%d" % i, started[i], False, after)
            for (n, li), src, land in zip(table[i], srcs, lands):
                _, a, b = shard[n].shape
                full[n][li] = lax.dynamic_update_slice(land, src[None], (dev, 0, 0)).reshape(N_DEV * a, b)

    token = sum(st[-1][0, 0] for st in started)
    return full, fetch, token


def sum_landed(name, g, land, me):
    _, a, b = g.shape
    tb = b // 2 if b % (2 * LANE) == 0 else b

    def body(me_ref, g_ref, land_ref, o_ref):
        acc = jnp.zeros(o_ref.shape, F32)
        for k in range(N_DEV):
            acc = acc + jnp.where(me_ref[0] == k, g_ref[0], land_ref[k]).astype(F32)
        o_ref[...] = acc

    return pl.pallas_call(
        body, name=name, out_shape=S((a, b), F32),
        grid_spec=pltpu.PrefetchScalarGridSpec(
            num_scalar_prefetch=1, grid=(b // tb,),
            in_specs=[pl.BlockSpec((1, a, tb), lambda j, me_: (me_[0], 0, j)),
                      pl.BlockSpec((N_DEV, a, tb), lambda j, me_: (0, 0, j))],
            out_specs=pl.BlockSpec((a, tb), lambda j, me_: (0, j))),
    )(me, g, land)


class GradScatter:
    def __init__(self, local_shapes):
        self.local_shapes = local_shapes
        self.pending = []

    def emit(self, tag, index, grads):
        names = [n for n in BIG if n in grads]
        gs = [grads[n].reshape(N_DEV, -1, grads[n].shape[1]) for n in names]
        started = split_start("rs_start_" + tag, gs, True)
        self.pending.append((tag, index, names, started))
        return started[-1][0:1, 0:1]

    def finish(self, after):
        me = (4 * lax.axis_index("x") + 2 * lax.axis_index("y") + lax.axis_index("c")).astype(jnp.int32).reshape(1)
        shards = {n: [None] * self.local_shapes[n][0] for n in BIG}
        for tag, index, names, started in self.pending:
            gs, lands = split_wait("rs_wait_" + tag, started, True, after)
            for n, g, land in zip(names, gs, lands):
                shards[n][index] = sum_landed("rs_sum_%s_%s" % (tag, n), g, land, me)
        out = {}
        for n in BIG:
            seg = jnp.stack(shards[n])
            out[n] = seg.transpose(0, 2, 1) if n in TRANSPOSED else seg
        return out


def gather_small(name, local, names):
    flat = _flatten([local[n] for n in names], LANE, 1)
    got = all_gather(name, flat).reshape(N_DEV, -1)
    full, off = {}, 0
    for n in names:
        size = math.prod(local[n].shape)
        seg = got[:, off:off + size].reshape((N_DEV,) + local[n].shape)
        full[n] = seg.transpose(1, 2, 0, 3).reshape(seg.shape[1], seg.shape[2], -1)
        off += size
    return full


def all_reduce_small(grads, names):
    flat = _flatten([grads[n] for n in names], LANE, SUBLANE)
    total = sum_devices("ar_sum", all_gather("ar_gather", flat)).reshape(-1)
    out, off = {}, 0
    for nm in names:
        size = math.prod(grads[nm].shape)
        out[nm] = total[off:off + size].reshape(grads[nm].shape)
        off += size
    return out


def adamw(name, w, g, m, v):
    shape = w.shape
    cols = shape[-1]
    rows = math.prod(shape[:-1])
    tr = _pick(rows, (256, 128, 64, 32, 16, 8)) if rows * cols > 256 * 1024 else rows
    c1 = 1.0 - ADAM_B1 ** ADAM_STEP
    c2 = 1.0 - ADAM_B2 ** ADAM_STEP

    def body(w_ref, g_ref, m_ref, v_ref, d_ref, nm_ref, nv_ref):
        gv = g_ref[...]
        m2 = ADAM_B1 * m_ref[...] + (1.0 - ADAM_B1) * gv
        v2 = ADAM_B2 * v_ref[...] + (1.0 - ADAM_B2) * jnp.square(gv)
        d_ref[...] = -ADAM_LR * ((m2 / c1) / (jnp.sqrt(v2 / c2) + ADAM_EPS) + ADAM_WD * w_ref[...])
        nm_ref[...] = m2
        nv_ref[...] = v2

    spec = pl.BlockSpec((tr, cols), lambda i: (i, 0))
    outs = pl.pallas_call(
        body, name=name, grid=(rows // tr,), in_specs=[spec] * 4, out_specs=[spec] * 3,
        out_shape=[S((rows, cols), F32)] * 3,
        compiler_params=pltpu.CompilerParams(dimension_semantics=("parallel",)),
    )(*[a.reshape(rows, cols) for a in (w, g, m, v)])
    return tuple(o.reshape(shape) for o in outs)


def kernel(x, p, e_w_in, e_conv_a_w, e_conv_a_b, e_ln_a_g, e_ln_a_b, e_conv_b_w, e_conv_b_b, e_dt_bias, e_a_log, e_d_skip, e_norm_b_g, e_w_out, o_w_in, o_conv_w, o_w_out, f_w_up, f_conv_w, f_conv_b, f_w_down, ple_w_proj, ple_w_gate, ln_g, ln_b, loss_target, m_e_w_in, m_e_conv_a_w, m_e_conv_a_b, m_e_ln_a_g, m_e_ln_a_b, m_e_conv_b_w, m_e_conv_b_b, m_e_dt_bias, m_e_a_log, m_e_d_skip, m_e_norm_b_g, m_e_w_out, m_o_w_in, m_o_conv_w, m_o_w_out, m_f_w_up, m_f_conv_w, m_f_conv_b, m_f_w_down, m_ple_w_proj, m_ple_w_gate, m_ln_g, m_ln_b, v_e_w_in, v_e_conv_a_w, v_e_conv_a_b, v_e_ln_a_g, v_e_ln_a_b, v_e_conv_b_w, v_e_conv_b_b, v_e_dt_bias, v_e_a_log, v_e_d_skip, v_e_norm_b_g, v_e_w_out, v_o_w_in, v_o_conv_w, v_o_w_out, v_f_w_up, v_f_conv_w, v_f_conv_b, v_f_w_down, v_ple_w_proj, v_ple_w_gate, v_ln_g, v_ln_b):
    args = locals()
    local = {n: args[n] for n in WEIGHTS}
    mom = {n: args["m_" + n] for n in WEIGHTS}
    var = {n: args["v_" + n] for n in WEIGHTS}

    full = {n: local[n] for n in REPLICATED}
    big, fetch, token = gather_big(local)
    full.update(big)
    full.update(gather_small("ag_small", local, SMALL_SHARDED))

    scatter = GradScatter({n: local[n].shape for n in BIG})
    loss_local, grad_x, grads = local_step(x[0] + token, p[:, 0], full, loss_target[0], fetch, scatter.emit)
    loss = lax.psum(loss_local, MESH_AXES)

    grads["ln_b"] = grads["ln_b"] + scatter.pending[-1][3][-1][0, 0]
    small = all_reduce_small(grads, REPLICATED + SMALL_SHARDED)
    dev = 4 * lax.axis_index("x") + 2 * lax.axis_index("y") + lax.axis_index("c")
    g_local = {n: small[n] for n in REPLICATED}
    for n in SMALL_SHARDED:
        width = local[n].shape[2]
        g_local[n] = lax.dynamic_slice_in_dim(small[n], dev * width, width, axis=2)
    delta, new_m, new_v = {}, {}, {}
    for n in REPLICATED + SMALL_SHARDED:
        delta[n], new_m[n], new_v[n] = adamw("adamw_" + n, local[n], g_local[n], mom[n], var[n])

    done = sum(delta[n].reshape(-1)[0] for n in REPLICATED + SMALL_SHARDED)
    g_local.update(scatter.finish(delta[SMALL_SHARDED[-1]] + 0.0 * done))
    for n in BIG:
        delta[n], new_m[n], new_v[n] = adamw("adamw_" + n, local[n], g_local[n], mom[n], var[n])
    return (loss, grad_x[None], *[g_local[n] for n in WEIGHTS], *[delta[n] for n in WEIGHTS],
            *[new_m[n] for n in WEIGHTS], *[new_v[n] for n in WEIGHTS])
```

```python
import functools
import math

import jax
import jax.numpy as jnp
from jax import lax
from jax.experimental import pallas as pl
from jax.experimental.pallas import tpu as pltpu

F32 = jnp.float32
MXU_DTYPE = jnp.bfloat16
MESH_AXES = ("x", "y", "c")
N_DEV = 8
LANE = 128
SUBLANE = 8
ROWWISE_VMEM_BUDGET = 20 * 1024 * 1024
MM_TILES = (1408, 1024, 512, 256, 128)
EXCHANGE_DTYPE = jnp.bfloat16
LN_EPS = 1e-5
CHUNK = 64
HEAD_DIM = 64
N_GROUPS = 4
N_STATE = 128
CONV_PAD = 32
CONV_ROWS = 256
ADAM_LR, ADAM_B1, ADAM_B2, ADAM_EPS, ADAM_WD, ADAM_STEP = 0.001, 0.9, 0.999, 1e-08, 0.01, 10

BIG = ("e_w_in", "e_w_out", "o_w_in", "o_w_out", "f_w_up", "f_w_down", "ple_w_proj", "ple_w_gate")
SMALL_SHARDED = ("e_conv_a_w", "e_conv_b_w", "o_conv_w", "f_conv_w", "ln_g", "ln_b")
REPLICATED = ("e_conv_a_b", "e_ln_a_g", "e_ln_a_b", "e_conv_b_b", "e_dt_bias", "e_a_log", "e_d_skip",
              "e_norm_b_g", "f_conv_b")
TRANSPOSED = ("e_w_in", "o_w_in", "f_w_up", "ple_w_proj")
WEIGHTS = ("e_w_in", "e_conv_a_w", "e_conv_a_b", "e_ln_a_g", "e_ln_a_b", "e_conv_b_w", "e_conv_b_b", "e_dt_bias",
           "e_a_log", "e_d_skip", "e_norm_b_g", "e_w_out", "o_w_in", "o_conv_w", "o_w_out", "f_w_up", "f_conv_w",
           "f_conv_b", "f_w_down", "ple_w_proj", "ple_w_gate", "ln_g", "ln_b")

S = jax.ShapeDtypeStruct


class Win:
    def __init__(self, arr, w=None, idx=0, coef=1.0):
        self.arr, self.w, self.idx, self.coef = arr, (arr.shape[1] if w is None else w), idx, coef


def _win(a):
    return a if isinstance(a, Win) else Win(a)


def _pick(n, prefs):
    for p in prefs:
        if p <= n and n % p == 0:
            return p
    return n


_MM_DIMS = {"nn": (1, 0), "nt": (1, 1), "tn": (0, 0)}


def mm(name, a, b, mode, out_dtype=F32):
    ca, cb = _MM_DIMS[mode]
    kdim = a.shape[ca]
    m = a.shape[1 - ca]
    n = b.shape[1 - cb]
    assert b.shape[cb] == kdim, (name, a.shape, b.shape, mode)
    tm = _pick(m, MM_TILES)
    tn = _pick(n, MM_TILES)
    tk = kdim if kdim <= MM_TILES[0] else _pick(kdim, MM_TILES)
    nk = kdim // tk
    own_acc = nk > 1 and out_dtype != F32

    def body(a_ref, b_ref, o_ref, *scratch):
        acc_ref = scratch[0] if own_acc else o_ref
        d = lax.dot_general(a_ref[...].astype(MXU_DTYPE), b_ref[...].astype(MXU_DTYPE),
                            (((ca,), (cb,)), ((), ())), preferred_element_type=F32)
        if nk == 1:
            o_ref[...] = d.astype(o_ref.dtype)
        else:
            k = pl.program_id(2)

            @pl.when(k == 0)
            def _():
                acc_ref[...] = d

            @pl.when(k > 0)
            def _():
                acc_ref[...] += d

            if own_acc:
                @pl.when(k == nk - 1)
                def _():
                    o_ref[...] = acc_ref[...].astype(o_ref.dtype)

    a_spec = pl.BlockSpec((tm, tk), lambda i, j, k: (i, k)) if ca == 1 else pl.BlockSpec((tk, tm), lambda i, j, k: (k, i))
    b_spec = pl.BlockSpec((tk, tn), lambda i, j, k: (k, j)) if cb == 0 else pl.BlockSpec((tn, tk), lambda i, j, k: (j, k))
    return pl.pallas_call(
        body, name=name, grid=(m // tm, n // tn, nk),
        in_specs=[a_spec, b_spec], out_specs=pl.BlockSpec((tm, tn), lambda i, j, k: (i, j)),
        out_shape=S((m, n), out_dtype), scratch_shapes=[pltpu.VMEM((tm, tn), F32)] if own_acc else [],
        compiler_params=pltpu.CompilerParams(dimension_semantics=("parallel", "parallel", "arbitrary")),
    )(a, b)


def _row_block(t, widths):
    tb = 512
    while tb > SUBLANE and (t % tb or tb * sum(widths) * 8 > ROWWISE_VMEM_BUDGET):
        tb //= 2
    return tb


def _tok_spec(tb, w):
    return pl.BlockSpec((tb, w.w), functools.partial(lambda i, idx: (i, idx), idx=w.idx))


def _par_spec(p):
    return pl.BlockSpec((1, p.shape[1]), lambda i: (0, 0))


def rowwise(name, fn, tok, par, out_widths, red_widths=(), out_dtypes=None):
    tok = [_win(t) for t in tok]
    t = tok[0].arr.shape[0]
    tb = _row_block(t, [w.w for w in tok] + list(out_widths))
    n_tok, n_par, n_out = len(tok), len(par), len(out_widths)
    out_dtypes = [F32] * n_out if out_dtypes is None else out_dtypes

    def body(*refs):
        ins = [r[...] for r in refs[:n_tok + n_par]]
        res = fn(*ins)
        out_refs = refs[n_tok + n_par:n_tok + n_par + n_out]
        red_refs = refs[n_tok + n_par + n_out:]
        for r, v in zip(out_refs, res[:n_out]):
            r[...] = v.astype(r.dtype)
        if red_refs:
            @pl.when(pl.program_id(0) == 0)
            def _():
                for r in red_refs:
                    r[...] = jnp.zeros_like(r)
            for r, v in zip(red_refs, res[n_out:]):
                r[...] += v

    outs = pl.pallas_call(
        body, name=name, grid=(t // tb,),
        in_specs=[_tok_spec(tb, w) for w in tok] + [_par_spec(p) for p in par],
        out_specs=[pl.BlockSpec((tb, w), lambda i: (i, 0)) for w in out_widths]
        + [pl.BlockSpec((1, w), lambda i: (0, 0)) for w in red_widths],
        out_shape=[S((t, w), dt) for w, dt in zip(out_widths, out_dtypes)] + [S((1, w), F32) for w in red_widths],
        compiler_params=pltpu.CompilerParams(dimension_semantics=("arbitrary",)),
    )(*[w.arr for w in tok], *par)
    return outs


def rowwise_bwd(name, fn, tok, par, cts, d_dtypes=None):
    tok = [_win(t) for t in tok]
    cts = [[_win(c) for c in group] for group in cts]
    d_dtypes = [(F32,)] * len(tok) if d_dtypes is None else d_dtypes
    t = tok[0].arr.shape[0]
    flat_cts = [c for group in cts for c in group]
    d_outs = [(i, w.w, dt) for i, (w, dts) in enumerate(zip(tok, d_dtypes)) for dt in dts]
    tb = _row_block(t, [w.w for w in tok] + [c.w for c in flat_cts] + [w for _, w, _ in d_outs])
    n_tok, n_par, n_ct, n_d = len(tok), len(par), len(flat_cts), len(d_outs)

    def body(*refs):
        tok_vals = [r[...] for r in refs[:n_tok]]
        par_vals = [r[...] for r in refs[n_tok:n_tok + n_par]]
        ct_refs = refs[n_tok + n_par:n_tok + n_par + n_ct]
        d_refs = refs[n_tok + n_par + n_ct:n_tok + n_par + n_ct + n_d]
        dp_refs = refs[n_tok + n_par + n_ct + n_d:]
        ct_vals, pos = [], 0
        for group in cts:
            acc = None
            for c in group:
                v = ct_refs[pos][...]
                if c.coef != 1.0:
                    v = v * c.coef
                acc = v if acc is None else acc + v
                pos += 1
            ct_vals.append(acc)
        _, vjp = jax.vjp(lambda *a: tuple(fn(*a)), *tok_vals, *par_vals)
        grads = vjp(tuple(ct_vals))
        for r, (i, _, _) in zip(d_refs, d_outs):
            r[...] = grads[i].astype(r.dtype)
        if dp_refs:
            @pl.when(pl.program_id(0) == 0)
            def _():
                for r in dp_refs:
                    r[...] = jnp.zeros_like(r)
            for r, v in zip(dp_refs, grads[n_tok:]):
                r[...] += v

    outs = pl.pallas_call(
        body, name=name, grid=(t // tb,),
        in_specs=[_tok_spec(tb, w) for w in tok] + [_par_spec(p) for p in par] + [_tok_spec(tb, c) for c in flat_cts],
        out_specs=[pl.BlockSpec((tb, w), lambda i: (i, 0)) for _, w, _ in d_outs] + [_par_spec(p) for p in par],
        out_shape=[S((t, w), dt) for _, w, dt in d_outs] + [S(p.shape, F32) for p in par],
        compiler_params=pltpu.CompilerParams(dimension_semantics=("arbitrary",)),
    )(*[w.arr for w in tok], *par, *[c.arr for c in flat_cts])
    return outs[:n_d], outs[n_d:]


def _sigmoid(x):
    return 1.0 / (1.0 + jnp.exp(-x))


def _silu(x):
    return x * _sigmoid(x)


def _ln(x, g, b):
    mu = jnp.mean(x, axis=-1, keepdims=True)
    var = jnp.mean(jnp.square(x - mu), axis=-1, keepdims=True)
    return (x - mu) * lax.rsqrt(var + LN_EPS) * g + b


def f_glu(ua, ug):
    return (ua * _sigmoid(ug),)


def f_ln_silu(a1, g, b):
    return (_silu(_ln(a1, g, b)),)


def f_silu3(a, b, c):
    return (_silu(a), _silu(b), _silu(c))


def f_softplus(dt_raw, bias):
    return (jax.nn.softplus(dt_raw + bias),)


def f_gate_rms(yssd, z, g):
    y = yssd * _silu(z)
    return (y * lax.rsqrt(jnp.mean(jnp.square(y), axis=-1, keepdims=True) + LN_EPS) * g,)


def f_ln(pre, g, b):
    return (_ln(pre, g, b),)


def f_mul(a, b):
    return (a * b,)


def f_gate_mul(pp, gt):
    return (pp * _sigmoid(gt),)


def conv_fwd(name, x, w, b):
    x = _win(x)
    t, c = x.arr.shape[0], x.w
    kw = w.shape[0]
    cb = LANE
    off = x.idx * (c // cb)
    rows = min(CONV_ROWS, t)
    has_b = b is not None

    def body(*refs):
        if has_b:
            x_ref, w_ref, b_ref, y_ref, xp_ref = refs
        else:
            x_ref, w_ref, y_ref, xp_ref = refs
        xp_ref[0:CONV_PAD, :] = jnp.zeros((CONV_PAD, cb), F32)
        xp_ref[CONV_PAD:CONV_PAD + t, :] = x_ref[...]

        def step(s, carry):
            base = pl.multiple_of(s * rows, rows)
            acc = jnp.zeros((rows, cb), F32)
            if has_b:
                acc = acc + b_ref[...]
            for k in range(kw):
                acc = acc + w_ref[k:k + 1, :] * xp_ref[pl.ds(base + CONV_PAD - (kw - 1) + k, rows), :]
            y_ref[pl.ds(base, rows), :] = acc
            return carry

        lax.fori_loop(0, t // rows, step, 0)

    in_specs = [pl.BlockSpec((t, cb), lambda j: (0, off + j)), pl.BlockSpec((kw, cb), lambda j: (0, j))]
    args = [x.arr, w]
    if has_b:
        in_specs.append(pl.BlockSpec((1, cb), lambda j: (0, j)))
        args.append(b)
    return pl.pallas_call(
        body, name=name, grid=(c // cb,), in_specs=in_specs,
        out_specs=pl.BlockSpec((t, cb), lambda j: (0, j)), out_shape=S((t, c), F32),
        scratch_shapes=[pltpu.VMEM((CONV_PAD + t, cb), F32)],
        compiler_params=pltpu.CompilerParams(dimension_semantics=("parallel",)),
    )(*args)


def conv_bwd(name, x, dy, w, dx_dtype=F32):
    x, dy = _win(x), _win(dy)
    t, c = x.arr.shape[0], x.w
    kw = w.shape[0]
    cb = LANE
    xoff = x.idx * (c // cb)
    dyoff = dy.idx * (c // cb)
    rows = min(CONV_ROWS, t)

    def body(x_ref, dy_ref, w_ref, dx_ref, dw_ref, db_ref, xp_ref, dyp_ref):
        xp_ref[0:CONV_PAD, :] = jnp.zeros((CONV_PAD, cb), F32)
        xp_ref[CONV_PAD:CONV_PAD + t, :] = x_ref[...]
        dyp_ref[0:t, :] = dy_ref[...]
        dyp_ref[t:t + CONV_PAD, :] = jnp.zeros((CONV_PAD, cb), F32)

        def fold(v):
            return jnp.sum(v.reshape(rows // SUBLANE, SUBLANE, cb), axis=0)

        def step(s, carry):
            base = pl.multiple_of(s * rows, rows)
            dyc = dy_ref[pl.ds(base, rows), :]
            acc = jnp.zeros((rows, cb), F32)
            new = []
            for k in range(kw):
                acc = acc + w_ref[k:k + 1, :] * dyp_ref[pl.ds(base + (kw - 1) - k, rows), :]
                new.append(carry[k] + fold(dyc * xp_ref[pl.ds(base + CONV_PAD - (kw - 1) + k, rows), :]))
            new.append(carry[kw] + fold(dyc))
            dx_ref[pl.ds(base, rows), :] = acc.astype(dx_ref.dtype)
            return tuple(new)

        init = tuple(jnp.zeros((SUBLANE, cb), F32) for _ in range(kw + 1))
        parts = lax.fori_loop(0, t // rows, step, init)
        for k in range(kw):
            dw_ref[k:k + 1, :] = jnp.sum(parts[k], axis=0, keepdims=True)
        db_ref[...] = jnp.sum(parts[kw], axis=0, keepdims=True)

    return pl.pallas_call(
        body, name=name, grid=(c // cb,),
        in_specs=[pl.BlockSpec((t, cb), lambda j: (0, xoff + j)), pl.BlockSpec((t, cb), lambda j: (0, dyoff + j)),
                  pl.BlockSpec((kw, cb), lambda j: (0, j))],
        out_specs=[pl.BlockSpec((t, cb), lambda j: (0, j)), pl.BlockSpec((kw, cb), lambda j: (0, j)),
                   pl.BlockSpec((1, cb), lambda j: (0, j))],
        out_shape=[S((t, c), dx_dtype), S((kw, c), F32), S((1, c), F32)],
        scratch_shapes=[pltpu.VMEM((CONV_PAD + t, cb), F32), pltpu.VMEM((CONV_PAD + t, cb), F32)],
        compiler_params=pltpu.CompilerParams(dimension_semantics=("parallel",)),
    )(x.arr, dy.arr, w)


def gated_conv_fwd(name, hpre, w, b, out_dtype):
    t, c2 = hpre.shape
    ff = c2 // 2
    kw = w.shape[0]
    cb = LANE
    nb = ff // cb
    rows = min(CONV_ROWS, t)

    def body(h1_ref, h2_ref, w1_ref, w2_ref, b1_ref, b2_ref, y_ref, xp1_ref, xp2_ref):
        for xp_ref, h_ref in ((xp1_ref, h1_ref), (xp2_ref, h2_ref)):
            xp_ref[0:CONV_PAD, :] = jnp.zeros((CONV_PAD, cb), F32)
            xp_ref[CONV_PAD:CONV_PAD + rows, :] = h_ref[0:rows, :]

        def chunk(base, first):
            def win(xp_ref, h_ref, k):
                if first:
                    return xp_ref[CONV_PAD - (kw - 1) + k:CONV_PAD - (kw - 1) + k + rows, :]
                return h_ref[pl.ds(base - (kw - 1) + k, rows), :]

            h1 = jnp.zeros((rows, cb), F32) + b1_ref[...]
            h2 = jnp.zeros((rows, cb), F32) + b2_ref[...]
            for k in range(kw):
                h1 = h1 + w1_ref[k:k + 1, :] * win(xp1_ref, h1_ref, k)
                h2 = h2 + w2_ref[k:k + 1, :] * win(xp2_ref, h2_ref, k)
            y_ref[pl.ds(base, rows), :] = (_silu(h1) * h2).astype(y_ref.dtype)

        def step(s, carry):
            chunk(pl.multiple_of(s * rows, rows), False)
            return carry

        chunk(0, True)
        lax.fori_loop(1, t // rows, step, 0)

    col1 = lambda r: pl.BlockSpec((r, cb), lambda j: (0, j))
    col2 = lambda r: pl.BlockSpec((r, cb), lambda j: (0, nb + j))
    return pl.pallas_call(
        body, name=name, grid=(nb,),
        in_specs=[col1(t), col2(t), col1(kw), col2(kw), col1(1), col2(1)],
        out_specs=col1(t), out_shape=S((t, ff), out_dtype),
        scratch_shapes=[pltpu.VMEM((CONV_PAD + rows, cb), F32)] * 2,
        compiler_params=pltpu.CompilerParams(dimension_semantics=("parallel",)),
    )(hpre, hpre, w, w, b, b)


def gated_conv_bwd(name, hpre, dact, w, b, dx_dtype):
    t, c2 = hpre.shape
    ff = c2 // 2
    kw = w.shape[0]
    cb = LANE
    nb = ff // cb
    rows = min(CONV_ROWS, t)

    def body(own_ref, oth_ref, da_ref, wo_ref, wt_ref, bo_ref, bt_ref, dx_ref, dw_ref, db_ref,
             xpo_ref, xpt_ref, dhp_ref):
        for xp_ref, h_ref in ((xpo_ref, own_ref), (xpt_ref, oth_ref)):
            xp_ref[0:CONV_PAD, :] = jnp.zeros((CONV_PAD, cb), F32)
            xp_ref[CONV_PAD:CONV_PAD + rows, :] = h_ref[0:rows, :]
        dhp_ref[t:t + CONV_PAD, :] = jnp.zeros((CONV_PAD, cb), F32)

        def fold(v):
            return jnp.sum(v.reshape(rows // SUBLANE, SUBLANE, cb), axis=0)

        def first_pass(own_is_gate):
            def chunk(base, first, carry):
                def win(xp_ref, h_ref, k):
                    if first:
                        return xp_ref[CONV_PAD - (kw - 1) + k:CONV_PAD - (kw - 1) + k + rows, :]
                    return h_ref[pl.ds(base - (kw - 1) + k, rows), :]

                ho = jnp.zeros((rows, cb), F32) + bo_ref[...]
                ht = jnp.zeros((rows, cb), F32) + bt_ref[...]
                for k in range(kw):
                    ho = ho + wo_ref[k:k + 1, :] * win(xpo_ref, own_ref, k)
                    ht = ht + wt_ref[k:k + 1, :] * win(xpt_ref, oth_ref, k)
                da = da_ref[pl.ds(base, rows), :]
                if own_is_gate:
                    sg = _sigmoid(ho)
                    dh = da * ht * (sg * (1.0 + ho * (1.0 - sg)))
                else:
                    dh = da * _silu(ht)
                dhp_ref[pl.ds(base, rows), :] = dh
                new = [carry[k] + fold(dh * win(xpo_ref, own_ref, k)) for k in range(kw)]
                new.append(carry[kw] + fold(dh))
                return tuple(new)

            init = tuple(jnp.zeros((SUBLANE, cb), F32) for _ in range(kw + 1))
            parts = lax.fori_loop(1, t // rows, lambda s, carry: chunk(pl.multiple_of(s * rows, rows), False, carry),
                                  chunk(0, True, init))
            for k in range(kw):
                dw_ref[k:k + 1, :] = jnp.sum(parts[k], axis=0, keepdims=True)
            db_ref[...] = jnp.sum(parts[kw], axis=0, keepdims=True)

        half = pl.program_id(0)

        @pl.when(half == 0)
        def _():
            first_pass(True)

        @pl.when(half == 1)
        def _():
            first_pass(False)

        def second(s, carry):
            base = pl.multiple_of(s * rows, rows)
            acc = jnp.zeros((rows, cb), F32)
            for k in range(kw):
                acc = acc + wo_ref[k:k + 1, :] * dhp_ref[pl.ds(base + (kw - 1) - k, rows), :]
            dx_ref[pl.ds(base, rows), :] = acc.astype(dx_ref.dtype)
            return carry

        lax.fori_loop(0, t // rows, second, 0)

    own = lambda r: pl.BlockSpec((r, cb), lambda h, j: (0, h * nb + j))
    oth = lambda r: pl.BlockSpec((r, cb), lambda h, j: (0, (1 - h) * nb + j))
    return pl.pallas_call(
        body, name=name, grid=(2, nb),
        in_specs=[own(t), oth(t), pl.BlockSpec((t, cb), lambda h, j: (0, j)), own(kw), oth(kw), own(1), oth(1)],
        out_specs=[own(t), own(kw), own(1)],
        out_shape=[S((t, c2), dx_dtype), S((kw, c2), F32), S((1, c2), F32)],
        scratch_shapes=[pltpu.VMEM((CONV_PAD + rows, cb), F32)] * 2 + [pltpu.VMEM((CONV_PAD + t, cb), F32)],
        compiler_params=pltpu.CompilerParams(dimension_semantics=("parallel", "parallel")),
    )(hpre, hpre, dact, w, w, b, b)


def _bdot(a, b, ca, cb):
    return lax.dot_general(a.astype(MXU_DTYPE), b.astype(MXU_DTYPE), (((ca,), (cb,)), ((0,), (0,))),
                           preferred_element_type=F32)


@jax.custom_vjp
def bmm_nn(a, b):
    return _bdot(a, b, 2, 1)


bmm_nn.defvjp(lambda a, b: (_bdot(a, b, 2, 1), (a, b)),
              lambda r, g: (_bdot(g, r[1], 2, 2), _bdot(r[0], g, 1, 1)))


@jax.custom_vjp
def bmm_tn(a, b):
    return _bdot(a, b, 1, 1)


bmm_tn.defvjp(lambda a, b: (_bdot(a, b, 1, 1), (a, b)),
              lambda r, g: (_bdot(r[1], g, 2, 2), _bdot(r[0], g, 2, 1)))


@jax.custom_vjp
def bmm_nt(a, b):
    return _bdot(a, b, 2, 2)


bmm_nt.defvjp(lambda a, b: (_bdot(a, b, 2, 2), (a, b)),
              lambda r, g: (_bdot(g, r[1], 2, 1), _bdot(g, r[0], 1, 1)))


def ssd_chunk(x, dt, dt_row, bm, cm, hprev, a_log, dsk):
    hg, ln, _ = x.shape
    n = bm.shape[1]
    ii = lax.broadcasted_iota(jnp.int32, (ln, ln), 0)
    jj = lax.broadcasted_iota(jnp.int32, (ln, ln), 1)
    tril, triu = (ii >= jj)[None], (ii <= jj)[None]
    a = -jnp.exp(a_log)
    da = dt * a
    da_row = dt_row * a
    cum_c = jnp.sum(jnp.where(tril, da_row, 0.0), axis=2, keepdims=True)
    cum_r = jnp.sum(jnp.where(triu, da, 0.0), axis=1, keepdims=True)
    last = jnp.sum(da, axis=1, keepdims=True)
    decay = jnp.where(tril, jnp.exp(jnp.where(tril, cum_c - cum_r, 0.0)), 0.0)
    cb = bmm_nt(cm[None], bm[None])
    y_diag = bmm_nn(cb * decay * dt_row, x)
    bb = jnp.broadcast_to(bm[None], (hg, ln, n))
    cc = jnp.broadcast_to(cm[None], (hg, ln, n))
    states = bmm_tn(x * (jnp.exp(last - cum_c) * dt), bb)
    y_off = bmm_nt(cc, hprev) * jnp.exp(cum_c)
    hnew = hprev * jnp.exp(last) + states
    return y_diag + y_off + dsk * x, hnew


def _ssd_dims(xs, bm, a_log):
    t = xs.shape[0]
    h = a_log.shape[0]
    return h, t, xs.shape[1] // h, h // N_GROUPS, bm.shape[1] // N_GROUPS, t // CHUNK


def _heads_of(ref, g, hg, p):
    return jnp.stack([ref[:, (g * hg + i) * p:(g * hg + i + 1) * p] for i in range(hg)])


def _cols_of(ref, g, hg):
    return jnp.stack([ref[:, g * hg + i:g * hg + i + 1] for i in range(hg)])


def dt_rows(dt, h):
    t = dt.shape[0]
    return dt[:, :h].T.reshape(h, t // CHUNK, 1, CHUNK).transpose(1, 0, 2, 3)


def dt_cols(dtr, lanes):
    nc, h, _, ln = dtr.shape
    return jnp.pad(dtr.transpose(1, 0, 2, 3).reshape(h, nc * ln).T, ((0, 0), (0, lanes - h)))


def ssd_fwd(name, xs, dt, dtr, bm, cm, a_log, dsk):
    h, t, p, hg, n, nc = _ssd_dims(xs, bm, a_log)

    def body(al_ref, dk_ref, x_ref, dt_ref, dtr_ref, b_ref, c_ref, y_ref, hp_ref, h_scr):
        @pl.when(pl.program_id(0) == 0)
        def _():
            h_scr[...] = jnp.zeros_like(h_scr)

        for g in range(N_GROUPS):
            hs, ns = slice(g * hg, (g + 1) * hg), slice(g * n, (g + 1) * n)
            hprev = h_scr[hs]
            hp_ref[hs, 0] = hprev
            y, hnew = ssd_chunk(_heads_of(x_ref, g, hg, p), _cols_of(dt_ref, g, hg), dtr_ref[0, hs], b_ref[:, ns],
                                c_ref[:, ns], hprev, al_ref[hs], dk_ref[hs])
            for i in range(hg):
                y_ref[:, (g * hg + i) * p:(g * hg + i + 1) * p] = y[i]
            h_scr[hs] = hnew

    head = pl.BlockSpec((h, 1, 1), lambda c: (0, 0, 0))
    row = lambda w: pl.BlockSpec((CHUNK, w), lambda c: (c, 0))
    return pl.pallas_call(
        body, name=name, grid=(nc,),
        in_specs=[head, head, row(h * p), row(dt.shape[1]), pl.BlockSpec((1, h, 1, CHUNK), lambda c: (c, 0, 0, 0)),
                  row(N_GROUPS * n), row(N_GROUPS * n)],
        out_specs=[row(h * p), pl.BlockSpec((h, 1, p, n), lambda c: (0, c, 0, 0))],
        out_shape=[S((t, h * p), F32), S((h, nc, p, n), F32)],
        scratch_shapes=[pltpu.VMEM((h, p, n), F32)],
        compiler_params=pltpu.CompilerParams(dimension_semantics=("arbitrary",)),
    )(a_log, dsk, xs, dt, dtr, bm, cm)


def ssd_bwd(name, xs, dt, dtr, bm, cm, a_log, dsk, hp, dy):
    h, t, p, hg, n, nc = _ssd_dims(xs, bm, a_log)

    def body(al_ref, dk_ref, x_ref, dt_ref, dtr_ref, b_ref, c_ref, hp_ref, dy_ref,
             dx_ref, ddt_ref, ddtr_ref, db_ref, dc_ref, dal_ref, ddk_ref, dh_scr):
        @pl.when(pl.program_id(0) == 0)
        def _():
            dh_scr[...] = jnp.zeros_like(dh_scr)
            dal_ref[...] = jnp.zeros_like(dal_ref)
            ddk_ref[...] = jnp.zeros_like(ddk_ref)

        ddt_ref[...] = jnp.zeros_like(ddt_ref)
        for g in range(N_GROUPS):
            hs, ns = slice(g * hg, (g + 1) * hg), slice(g * n, (g + 1) * n)
            _, vjp = jax.vjp(ssd_chunk, _heads_of(x_ref, g, hg, p), _cols_of(dt_ref, g, hg), dtr_ref[0, hs],
                             b_ref[:, ns], c_ref[:, ns], hp_ref[hs, 0], al_ref[hs], dk_ref[hs])
            gx, gdt, gdtr, gb, gc, ghp, gal, gdk = vjp((_heads_of(dy_ref, g, hg, p), dh_scr[hs]))
            for i in range(hg):
                dx_ref[:, (g * hg + i) * p:(g * hg + i + 1) * p] = gx[i]
                ddt_ref[:, g * hg + i:g * hg + i + 1] = gdt[i]
            ddtr_ref[0, hs] = gdtr
            db_ref[:, ns] = gb
            dc_ref[:, ns] = gc
            dh_scr[hs] = ghp
            dal_ref[hs] += gal
            ddk_ref[hs] += gdk

    head = pl.BlockSpec((h, 1, 1), lambda c: (0, 0, 0))
    row = lambda w: pl.BlockSpec((CHUNK, w), lambda c: (nc - 1 - c, 0))
    rows = pl.BlockSpec((1, h, 1, CHUNK), lambda c: (nc - 1 - c, 0, 0, 0))
    return pl.pallas_call(
        body, name=name, grid=(nc,),
        in_specs=[head, head, row(h * p), row(dt.shape[1]), rows, row(N_GROUPS * n), row(N_GROUPS * n),
                  pl.BlockSpec((h, 1, p, n), lambda c: (0, nc - 1 - c, 0, 0)), row(h * p)],
        out_specs=[row(h * p), row(dt.shape[1]), rows, row(N_GROUPS * n), row(N_GROUPS * n), head, head],
        out_shape=[S((t, h * p), F32), S(dt.shape, F32), S(dtr.shape, F32), S(bm.shape, F32), S(cm.shape, F32),
                   S((h, 1, 1), F32), S((h, 1, 1), F32)],
        scratch_shapes=[pltpu.VMEM((h, p, n), F32)],
        compiler_params=pltpu.CompilerParams(dimension_semantics=("arbitrary",)),
    )(a_log, dsk, xs, dt, dtr, bm, cm, hp, dy)


def _alpha(depth):
    return (2.0 * depth) ** 0.25


def _pad_lanes(v):
    return jnp.pad(v, ((0, 0), (0, LANE - v.shape[1])))


def split_even_weights(w, j):
    d = w["e_w_in"][j].shape[1]
    da = w["e_conv_a_w"].shape[2]
    db = w["e_norm_b_g"].shape[1]
    gn = N_GROUPS * N_STATE
    nh = w["e_dt_bias"].shape[1]
    main = 2 * da + 2 * db + 2 * gn
    win = w["e_w_in"][j]
    ox = 2 * da + db
    cw, cbias = w["e_conv_b_w"][j], w["e_conv_b_b"][j][None]
    return dict(
        d=d, da=da, db=db, gn=gn, nh=nh, main=main,
        win_main=win[:main], win_dt=jnp.pad(win[main:], ((0, LANE - nh), (0, 0))),
        caw=w["e_conv_a_w"][j], cab=w["e_conv_a_b"][j][None], lag=w["e_ln_a_g"][j][None], lab=w["e_ln_a_b"][j][None],
        cw_xs=cw[:, :db], cw_b=cw[:, db:db + gn], cw_c=cw[:, db + gn:],
        cb_xs=cbias[:, :db], cb_b=cbias[:, db:db + gn], cb_c=cbias[:, db + gn:],
        dt_bias=_pad_lanes(w["e_dt_bias"][j][None]), a_log=w["e_a_log"][j].reshape(nh, 1, 1),
        dsk=w["e_d_skip"][j].reshape(nh, 1, 1), norm_g=w["e_norm_b_g"][j][None],
        wout_a=w["e_w_out"][j][:da], wout_b=w["e_w_out"][j][da:],
    )


def even_fwd(tag, x, xm, lw, ln_g, ln_b, alpha):
    t = x.shape[0]
    da, db, gn, nh = lw["da"], lw["db"], lw["gn"], lw["nh"]
    u = mm(tag + "_win", xm, lw["win_main"], "nt")
    udt = mm(tag + "_windt", xm, lw["win_dt"], "nt")
    ua, ug, z, xs_pre = Win(u, da, 0), Win(u, da, 1), Win(u, db, 2 * da // db), Win(u, db, (2 * da + db) // db)
    b_pre, c_pre = Win(u, gn, (2 * da + 2 * db) // gn), Win(u, gn, (2 * da + 2 * db + gn) // gn)
    (a0,) = rowwise(tag + "_glu", f_glu, [ua, ug], [], [da])
    a1 = conv_fwd(tag + "_conva", a0, lw["caw"], lw["cab"])
    (ya,) = rowwise(tag + "_lna", f_ln_silu, [a1], [lw["lag"], lw["lab"]], [da], out_dtypes=[MXU_DTYPE])
    xs_c = conv_fwd(tag + "_convxs", xs_pre, lw["cw_xs"], lw["cb_xs"])
    b_c = conv_fwd(tag + "_convb", b_pre, lw["cw_b"], lw["cb_b"])
    c_c = conv_fwd(tag + "_convc", c_pre, lw["cw_c"], lw["cb_c"])
    xs, bm, cm = rowwise(tag + "_silu3", f_silu3, [xs_c, b_c, c_c], [], [db, gn, gn])
    (dt,) = rowwise(tag + "_dt", f_softplus, [udt], [lw["dt_bias"]], [LANE])
    dtr = dt_rows(dt, nh)
    yssd, hp = ssd_fwd(tag + "_ssd", xs, dt, dtr, bm, cm, lw["a_log"], lw["dsk"])
    (yb,) = rowwise(tag + "_gate", f_gate_rms, [yssd, z], [lw["norm_g"]], [db], out_dtypes=[MXU_DTYPE])
    ma = mm(tag + "_wouta", ya, lw["wout_a"], "nn")
    mb = mm(tag + "_woutb", yb, lw["wout_b"], "nn")

    def f_res(xv, mav, mbv, g, b):
        pre = alpha * xv + mav + mbv
        y = _ln(pre, g, b)
        return y, y, pre

    x1, x1m, pre = rowwise(tag + "_res", f_res, [x, ma, mb], [ln_g, ln_b], [x.shape[1]] * 3,
                           out_dtypes=[F32, MXU_DTYPE, F32])
    saved = dict(xm=xm, u=u, udt=udt, a0=a0, a1=a1, ya=ya, xs_c=xs_c, b_c=b_c, c_c=c_c, xs=xs, dt=dt, dtr=dtr, bm=bm,
                 cm=cm, hp=hp, yssd=yssd, yb=yb, pre=pre)
    return x1, x1m, saved


def even_bwd(tag, dx1_pieces, sv, lw, ln_g, ln_b, alpha):
    t = sv["u"].shape[0]
    da, db, gn, nh = lw["da"], lw["db"], lw["gn"], lw["nh"]
    u, xm = sv["u"], sv["xm"]
    mx = (MXU_DTYPE,)
    ua, ug, z, xs_pre = Win(u, da, 0), Win(u, da, 1), Win(u, db, 2 * da // db), Win(u, db, (2 * da + db) // db)
    b_pre, c_pre = Win(u, gn, (2 * da + 2 * db) // gn), Win(u, gn, (2 * da + 2 * db + gn) // gn)
    (dpre, dprem), (dg0, db0) = rowwise_bwd(tag + "_res_b", f_ln, [sv["pre"]], [ln_g, ln_b], [dx1_pieces],
                                            d_dtypes=[(F32, MXU_DTYPE)])
    dya = mm(tag + "_dya", dprem, lw["wout_a"], "nt")
    dyb = mm(tag + "_dyb", dprem, lw["wout_b"], "nt")
    dwout_a = mm(tag + "_dwouta", sv["ya"], dprem, "tn", EXCHANGE_DTYPE)
    dwout_b = mm(tag + "_dwoutb", sv["yb"], dprem, "tn", EXCHANGE_DTYPE)
    (dyssd, dz), (dnorm_g,) = rowwise_bwd(tag + "_gate_b", f_gate_rms, [sv["yssd"], z], [lw["norm_g"]], [[dyb]],
                                          d_dtypes=[(F32,), mx])
    dxs, ddt, ddtr, dbm, dcm, dalog, ddsk = ssd_bwd(tag + "_ssd_b", sv["xs"], sv["dt"], sv["dtr"], sv["bm"], sv["cm"],
                                                    lw["a_log"], lw["dsk"], sv["hp"], dyssd)
    ddt_pieces = [ddt, dt_cols(ddtr, ddt.shape[1])]
    (dudt,), (ddt_bias,) = rowwise_bwd(tag + "_dt_b", f_softplus, [sv["udt"]], [lw["dt_bias"]], [ddt_pieces],
                                       d_dtypes=[mx])
    (dxs_c, db_c, dc_c), _ = rowwise_bwd(tag + "_silu3_b", f_silu3, [sv["xs_c"], sv["b_c"], sv["c_c"]], [],
                                         [[dxs], [dbm], [dcm]])
    dxs_pre, dcw_xs, dcb_xs = conv_bwd(tag + "_convxs_b", xs_pre, dxs_c, lw["cw_xs"], MXU_DTYPE)
    db_pre, dcw_b, dcb_b = conv_bwd(tag + "_convb_b", b_pre, db_c, lw["cw_b"], MXU_DTYPE)
    dc_pre, dcw_c, dcb_c = conv_bwd(tag + "_convc_b", c_pre, dc_c, lw["cw_c"], MXU_DTYPE)
    (da1,), (dlag, dlab) = rowwise_bwd(tag + "_lna_b", f_ln_silu, [sv["a1"]], [lw["lag"], lw["lab"]], [[dya]])
    da0, dcaw, dcab = conv_bwd(tag + "_conva_b", sv["a0"], da1, lw["caw"])
    (dua, dug), _ = rowwise_bwd(tag + "_glu_b", f_glu, [ua, ug], [], [[da0]], d_dtypes=[mx, mx])
    du = jnp.concatenate([dua, dug, dz, dxs_pre, db_pre, dc_pre], axis=1)
    dx_m = mm(tag + "_dxm", du, lw["win_main"], "nn")
    dx_dt = mm(tag + "_dxdt", dudt, lw["win_dt"], "nn")
    dwin_main = mm(tag + "_dwin", du, xm, "tn", EXCHANGE_DTYPE)
    dwin_dt = mm(tag + "_dwindt", dudt, xm, "tn", EXCHANGE_DTYPE)
    grads = dict(
        e_w_in=jnp.concatenate([dwin_main, dwin_dt[:nh]], axis=0),
        e_conv_a_w=dcaw, e_conv_a_b=dcab[0], e_ln_a_g=dlag[0], e_ln_a_b=dlab[0],
        e_conv_b_w=jnp.concatenate([dcw_xs, dcw_b, dcw_c], axis=1),
        e_conv_b_b=jnp.concatenate([dcb_xs, dcb_b, dcb_c], axis=1)[0],
        e_dt_bias=ddt_bias[0, :nh], e_a_log=dalog.reshape(nh), e_d_skip=ddsk.reshape(nh), e_norm_b_g=dnorm_g[0],
        e_w_out=jnp.concatenate([dwout_a, dwout_b], axis=0), ln_g0=dg0[0], ln_b0=db0[0],
    )
    return [Win(dpre, coef=alpha), dx_m, dx_dt], grads


def odd_fwd(tag, x, xm, w, j, ln_g, ln_b, alpha):
    d = x.shape[1]
    u = mm(tag + "_win", xm, w["o_w_in"][j], "nt")
    bg, cg, v = Win(u, d, 0), Win(u, d, 1), Win(u, d, 2)
    (s,) = rowwise(tag + "_cv", f_mul, [cg, v], [], [d])
    cs = conv_fwd(tag + "_conv", s, w["o_conv_w"][j], None)
    (m,) = rowwise(tag + "_bm", f_mul, [bg, cs], [], [d], out_dtypes=[MXU_DTYPE])
    mix = mm(tag + "_wout", m, w["o_w_out"][j], "nn")

    def f_res(xv, mv, g, b):
        pre = alpha * xv + mv
        y = _ln(pre, g, b)
        return y, y, pre

    x1, x1m, pre = rowwise(tag + "_res", f_res, [x, mix], [ln_g, ln_b], [d] * 3, out_dtypes=[F32, MXU_DTYPE, F32])
    return x1, x1m, dict(xm=xm, u=u, s=s, cs=cs, m=m, pre=pre)


def odd_bwd(tag, dx1_pieces, sv, w, j, ln_g, ln_b, alpha):
    xm, u = sv["xm"], sv["u"]
    d = xm.shape[1]
    mx = (MXU_DTYPE,)
    bg, cg, v = Win(u, d, 0), Win(u, d, 1), Win(u, d, 2)
    (dpre, dprem), (dg0, db0) = rowwise_bwd(tag + "_res_b", f_ln, [sv["pre"]], [ln_g, ln_b], [dx1_pieces],
                                            d_dtypes=[(F32, MXU_DTYPE)])
    dm = mm(tag + "_dm", dprem, w["o_w_out"][j], "nt")
    dwout = mm(tag + "_dwout", sv["m"], dprem, "tn", EXCHANGE_DTYPE)
    (dbg, dcs), _ = rowwise_bwd(tag + "_bm_b", f_mul, [bg, sv["cs"]], [], [[dm]], d_dtypes=[mx, (F32,)])
    ds, dcw, _ = conv_bwd(tag + "_conv_b", sv["s"], dcs, w["o_conv_w"][j])
    (dcg, dv), _ = rowwise_bwd(tag + "_cv_b", f_mul, [cg, v], [], [[ds]], d_dtypes=[mx, mx])
    du = jnp.concatenate([dbg, dcg, dv], axis=1)
    dx_u = mm(tag + "_dx", du, w["o_w_in"][j], "nn")
    dwin = mm(tag + "_dwin", du, xm, "tn", EXCHANGE_DTYPE)
    grads = dict(o_w_in=dwin, o_conv_w=dcw, o_w_out=dwout, ln_g0=dg0[0], ln_b0=db0[0])
    return [Win(dpre, coef=alpha), dx_u], grads


def ffn_fwd(tag, x1, x1m, p_i, w, i, ln_g, ln_b, alpha):
    d = x1.shape[1]
    hpre = mm(tag + "_wup", x1m, w["f_w_up"][i], "nt")
    act = gated_conv_fwd(tag + "_fgate", hpre, w["f_conv_w"][i], w["f_conv_b"][i][None], MXU_DTYPE)
    ffn = mm(tag + "_wdown", act, w["f_w_down"][i], "nn")
    pp = mm(tag + "_pproj", p_i, w["ple_w_proj"][i], "nt")
    gt = mm(tag + "_pgate", x1m, w["ple_w_gate"][i], "nn")

    def f_res2(xv, fv, ppv, gtv, g, b):
        pre = alpha * xv + fv + ppv * _sigmoid(gtv)
        y = _ln(pre, g, b)
        return y, y, pre

    x2, x2m, pre = rowwise(tag + "_res2", f_res2, [x1, ffn, pp, gt], [ln_g, ln_b], [d] * 3,
                           out_dtypes=[F32, MXU_DTYPE, F32])
    return x2, x2m, dict(x1m=x1m, hpre=hpre, act=act, pp=pp, gt=gt, pre=pre)


def ffn_bwd(tag, dx2_pieces, sv, p_i, w, i, ln_g, ln_b, alpha):
    x1m = sv["x1m"]
    mx = (MXU_DTYPE,)
    (dpre, dprem), (dg1, db1) = rowwise_bwd(tag + "_res2_b", f_ln, [sv["pre"]], [ln_g, ln_b], [dx2_pieces],
                                            d_dtypes=[(F32, MXU_DTYPE)])
    (dpp, dgt), _ = rowwise_bwd(tag + "_pg_b", f_gate_mul, [sv["pp"], sv["gt"]], [], [[dpre]], d_dtypes=[mx, mx])
    dwproj = mm(tag + "_dwproj", dpp, p_i, "tn", EXCHANGE_DTYPE)
    dwgate = mm(tag + "_dwgate", x1m, dgt, "tn", EXCHANGE_DTYPE)
    dx1_a = mm(tag + "_dx1a", dgt, w["ple_w_gate"][i], "nt")
    dact = mm(tag + "_dact", dprem, w["f_w_down"][i], "nt")
    dwdown = mm(tag + "_dwdown", sv["act"], dprem, "tn", EXCHANGE_DTYPE)
    dhpre, dfcw, dfcb = gated_conv_bwd(tag + "_fgate_b", sv["hpre"], dact, w["f_conv_w"][i], w["f_conv_b"][i][None],
                                       MXU_DTYPE)
    dwup = mm(tag + "_dwup", dhpre, x1m, "tn", EXCHANGE_DTYPE)
    dx1_b = mm(tag + "_dx1b", dhpre, w["f_w_up"][i], "nn")
    grads = dict(f_w_up=dwup, f_conv_w=dfcw, f_conv_b=dfcb[0], f_w_down=dwdown, ple_w_proj=dwproj, ple_w_gate=dwgate,
                 ln_g1=dg1[0], ln_b1=db1[0])
    return [Win(dpre, coef=alpha), dx1_a, dx1_b], grads


def local_step(x, p, w, target, fetch=None, emit=None):
    depth = w["ln_g"].shape[0]
    alpha = _alpha(depth)
    d = x.shape[1]
    saved = []
    h = hm = x
    for i in range(depth):
        j = i // 2
        if fetch is not None:
            fetch(i, "m", h)
        g0, b0, g1, b1 = w["ln_g"][i, 0][None], w["ln_b"][i, 0][None], w["ln_g"][i, 1][None], w["ln_b"][i, 1][None]
        tag = "l%d" % i
        if i % 2 == 0:
            lw = split_even_weights(w, j)
            h, hm, sv_m = even_fwd(tag, h, hm, lw, g0, b0, alpha)
        else:
            lw = None
            h, hm, sv_m = odd_fwd(tag, h, hm, w, j, g0, b0, alpha)
        if fetch is not None:
            fetch(i, "f", h)
        h, hm, sv_f = ffn_fwd(tag, h, hm, p[i], w, i, g1, b1, alpha)
        saved.append((lw, sv_m, sv_f, (g0, b0, g1, b1)))

    def f_loss(xf, tg):
        diff = xf - tg
        sq = jnp.sum(jnp.sum(jnp.square(diff), axis=1, keepdims=True), axis=0, keepdims=True)
        return diff * (1.0 / d), jnp.broadcast_to(sq, (1, LANE))

    dxf, sq = rowwise("loss", f_loss, [h, target], [], [d], red_widths=[LANE])
    loss = sq[0, 0] * (0.5 / d)

    per_layer = []
    pieces = [dxf]
    token = None
    for i in reversed(range(depth)):
        j = i // 2
        lw, sv_m, sv_f, (g0, b0, g1, b1) = saved[i]
        tag = "l%d" % i
        pieces, gf = ffn_bwd(tag, pieces, sv_f, p[i], w, i, g1 if token is None else g1 + token, b1, alpha)
        if emit is not None:
            token = emit(tag + "f", i, gf)
        g0 = g0 if token is None else g0 + token
        if i % 2 == 0:
            pieces, gm = even_bwd(tag, pieces, sv_m, lw, g0, b0, alpha)
        else:
            pieces, gm = odd_bwd(tag, pieces, sv_m, w, j, g0, b0, alpha)
        if emit is not None:
            token = emit(tag + "m", j, gm)
        per_layer.append((i, gm, gf))

    def f_sum(*vs):
        acc = None
        for v, c in zip(vs, [pc.coef for pc in map(_win, pieces)]):
            v = v if c == 1.0 else v * c
            acc = v if acc is None else acc + v
        return (acc,)

    (grad_x,) = rowwise("grad_x", f_sum, [Win(_win(pc).arr) for pc in pieces], [], [d])

    by_layer = {i: (gm, gf) for i, gm, gf in per_layer}
    grads = {}
    n_even, n_odd = (depth + 1) // 2, depth // 2
    collect = lambda name, per_layer: per_layer if name in BIG else jnp.stack(per_layer)
    for name in ("e_w_in", "e_conv_a_w", "e_conv_a_b", "e_ln_a_g", "e_ln_a_b", "e_conv_b_w", "e_conv_b_b", "e_dt_bias",
                 "e_a_log", "e_d_skip", "e_norm_b_g", "e_w_out"):
        grads[name] = collect(name, [by_layer[2 * j][0][name] for j in range(n_even)])
    for name in ("o_w_in", "o_conv_w", "o_w_out"):
        grads[name] = collect(name, [by_layer[2 * j + 1][0][name] for j in range(n_odd)])
    for name in ("f_w_up", "f_conv_w", "f_conv_b", "f_w_down", "ple_w_proj", "ple_w_gate"):
        grads[name] = collect(name, [by_layer[i][1][name] for i in range(depth)])
    grads["ln_g"] = jnp.stack([jnp.stack([by_layer[i][0]["ln_g0"], by_layer[i][1]["ln_g1"]]) for i in range(depth)])
    grads["ln_b"] = jnp.stack([jnp.stack([by_layer[i][0]["ln_b0"], by_layer[i][1]["ln_b1"]]) for i in range(depth)])
    return loss, grad_x, grads


_ANY = pl.BlockSpec(memory_space=pl.ANY)
_MESH = pl.DeviceIdType.MESH


def all_gather(name, xl):
    r, c_ = xl.shape
    split = r // 2 // 16 * 16
    halves = ((0, split), (split, r - split)) if split else ((0, r),)
    two = len(halves) == 2

    def body(x_ref, out_ref, send_sems, recv_sems, local_sem):
        x, y, c = lax.axis_index("x"), lax.axis_index("y"), lax.axis_index("c")
        me, sibling, xn, yn, dg = (x, y, c), (x, y, 1 - c), (1 - x, y, c), (x, 1 - y, c), (1 - x, 1 - y, c)

        def rows(block, h):
            ref = out_ref.at[4 * block[0] + 2 * block[1] + block[2]]
            return ref if h is None else ref.at[pl.ds(*halves[h])]

        def copy(k, block, h, to, own=False):
            src = (x_ref if h is None else x_ref.at[pl.ds(*halves[h])]) if own else rows(block, h)
            return pltpu.make_async_remote_copy(src_ref=src, dst_ref=rows(block, h), send_sem=send_sems.at[k],
                                                recv_sem=recv_sems.at[k], device_id=to, device_id_type=_MESH)

        def other_core(block):
            return (block[0], block[1], 1 - c)

        mine = pltpu.make_async_copy(x_ref, rows(me, None), local_sem)
        mine.start()
        direct = [copy(0, me, 0, xn, own=True), copy(1, me, 1 if two else 0, yn, own=True)]
        if two:
            direct += [copy(2, me, 1, xn, own=True), copy(3, me, 0, yn, own=True)]
        direct.append(copy(6, me, None, sibling, own=True))
        for cp in direct:
            cp.start()
        started = list(direct)

        def then(waits, nxt):
            for cp in waits:
                cp.wait_recv()
            for cp in nxt:
                cp.start()
            started.extend(nxt)

        if two:
            then([copy(0, xn, 0, me)], [copy(4, xn, 0, yn)])
            then([copy(1, yn, 1, me)], [copy(5, yn, 1, xn)])
            then([copy(2, xn, 1, me)], [copy(7, xn, None, sibling)])
            then([copy(3, yn, 0, me)], [copy(8, yn, None, sibling)])
            then([copy(4, dg, 0, me), copy(5, dg, 1, me)], [copy(9, dg, None, sibling)])
        else:
            then([copy(0, xn, 0, me)], [copy(4, xn, 0, yn), copy(7, xn, None, sibling)])
            then([copy(1, yn, 0, me)], [copy(8, yn, None, sibling)])
            then([copy(4, dg, 0, me)], [copy(9, dg, None, sibling)])
        for k, block in ((6, me), (7, xn), (8, yn), (9, dg)):
            copy(k, other_core(block), None, me).wait_recv()
        for cp in started:
            cp.wait_send()
        mine.wait()

    return pl.pallas_call(
        body, name=name, out_shape=S((N_DEV, r, c_), xl.dtype), in_specs=[_ANY], out_specs=_ANY,
        scratch_shapes=[pltpu.SemaphoreType.DMA((10,)), pltpu.SemaphoreType.DMA((10,)), pltpu.SemaphoreType.DMA],
    )(xl)


def sum_devices(name, g8):
    _, r, c_ = g8.shape

    def body(g_ref, o_ref):
        acc = g_ref[0]
        for k in range(1, N_DEV):
            acc = acc + g_ref[k]
        o_ref[...] = acc

    return pl.pallas_call(body, name=name, out_shape=S((r, c_), F32))(g8)


def _flatten(parts, cols, row_mult):
    flat = jnp.concatenate([v.reshape(-1) for v in parts])
    n = flat.shape[0]
    rows = -(-n // (cols * row_mult)) * row_mult
    return jnp.pad(flat, (0, rows * cols - n)).reshape(rows, cols)


def _exchange_dims(name, lshape):
    l, r, c = lshape
    return (l, c, r) if name in TRANSPOSED else (l, r, c)


_HBM = pl.BlockSpec(memory_space=pltpu.HBM)
_SEM = pl.BlockSpec(memory_space=pltpu.SEMAPHORE)
_PEERS = [(dx, dy, dc) for dx in (0, 1) for dy in (0, 1) for dc in (0, 1)][1:]


def _peer_copies(scatter, src_refs, land_refs, send_sems, recv_sems):
    x, y, c = lax.axis_index("x"), lax.axis_index("y"), lax.axis_index("c")
    flip = lambda v, d: 1 - v if d else v
    copies = []
    for s_ref, land_ref in zip(src_refs, land_refs):
        for dx, dy, dc in _PEERS:
            px, py, pc = flip(x, dx), flip(y, dy), flip(c, dc)
            k = len(copies)
            copies.append(pltpu.make_async_remote_copy(
                src_ref=s_ref.at[4 * px + 2 * py + pc] if scatter else s_ref, dst_ref=land_ref.at[4 * x + 2 * y + c],
                send_sem=send_sems.at[k], recv_sem=recv_sems.at[k], device_id=(px, py, pc), device_id_type=_MESH))
    return copies


def split_start(name, srcs, scatter):
    n = len(srcs)
    lands = [lax.empty(s.shape if scatter else (N_DEV,) + s.shape, s.dtype) for s in srcs]
    n_copies = n * len(_PEERS)

    def body(*refs):
        send_sems, recv_sems, token = refs[2 * n], refs[2 * n + 1], refs[-1]
        for cp in _peer_copies(scatter, refs[:n], refs[n:2 * n], send_sems, recv_sems):
            cp.start()
        token[...] = jnp.zeros_like(token)

    return pl.pallas_call(
        body, name=name,
        out_shape=(pltpu.SemaphoreType.DMA((n_copies,)), pltpu.SemaphoreType.DMA((n_copies,)),
                   *[pltpu.HBM(v.shape, v.dtype) for v in srcs + lands], S((SUBLANE, LANE), F32)),
        in_specs=(_HBM,) * (2 * n), out_specs=(_SEM, _SEM) + (_HBM,) * (2 * n) + (pl.BlockSpec(memory_space=pltpu.VMEM),),
        input_output_aliases={i: 2 + i for i in range(2 * n)},
        compiler_params=pltpu.CompilerParams(has_side_effects=pltpu.SideEffectType.DATAFLOW_SIDE_EFFECTING),
    )(*[pltpu.with_memory_space_constraint(v, pltpu.HBM) for v in srcs + lands])


def split_wait(name, started, scatter, after):
    send_sems, recv_sems = started[0], started[1]
    thru = list(started[2:-1])
    n = len(thru) // 2

    def body(*refs):
        send_sems, recv_sems = refs[2 * n], refs[2 * n + 1]
        for cp in _peer_copies(scatter, refs[:n], refs[n:2 * n], send_sems, recv_sems):
            cp.wait_send()
            cp.wait_recv()

    outs = pl.pallas_call(
        body, name=name, out_shape=tuple(pltpu.HBM(v.shape, v.dtype) for v in thru),
        in_specs=(_HBM,) * (2 * n) + (_SEM, _SEM, _ANY), out_specs=(_HBM,) * (2 * n),
        input_output_aliases={i: i for i in range(2 * n)},
        compiler_params=pltpu.CompilerParams(has_side_effects=pltpu.SideEffectType.DATAFLOW_SIDE_EFFECTING),
    )(*thru, send_sems, recv_sems, after)
    return outs[:n], outs[n:]


def _layer_weights(depth):
    table = []
    for i in range(depth):
        mixer = ("e_w_out", "e_w_in") if i % 2 == 0 else ("o_w_out", "o_w_in")
        table.append([("f_w_up", i), ("f_w_down", i), ("ple_w_gate", i), ("ple_w_proj", i)]
                     + [(n, i // 2) for n in mixer])
    return table


def gather_big(local):
    cols = local["e_w_out"].shape[2]
    depth = local["ln_g"].shape[0]
    table = _layer_weights(depth)
    shard = {}
    for n in BIG:
        v = local[n].astype(MXU_DTYPE)
        shard[n] = v.transpose(0, 2, 1) if n in TRANSPOSED else v
    full = {n: [None] * local[n].shape[0] for n in BIG}
    dev = 4 * lax.axis_index("x") + 2 * lax.axis_index("y") + lax.axis_index("c")

    first = [(n, li) for n, li in table[0] if n.startswith("e_")]
    table[0] = [(n, li) for n, li in table[0] if not n.startswith("e_")]
    got, r0 = all_gather("ag_l0", jnp.concatenate([shard[n][li].reshape(-1, cols) for n, li in first], axis=0)), 0
    for n, li in first:
        _, a, b = shard[n].shape
        per = a * b // cols
        full[n][li] = got[:, r0:r0 + per].reshape(N_DEV * a, b)
        r0 += per
    started = [split_start("ag_start_l%d" % i, [shard[n][li] for n, li in table[i]], False) for i in range(depth)]

    def fetch(i, part, after):
        if part == ("f" if i == 0 else "m"):
            srcs, lands = split_wait("ag_wait_l%d" % i, started[i], False, after)
            for (n, li), src, land in zip(table[i], srcs, lands):
                _, a, b = shard[n].shape
                full[n][li] = lax.dynamic_update_slice(land, src[None], (dev, 0, 0)).reshape(N_DEV * a, b)

    token = sum(st[-1][0, 0] for st in started)
    return full, fetch, token


def sum_landed(name, g, land, me):
    _, a, b = g.shape
    tb = b // 2 if b % (2 * LANE) == 0 else b

    def body(me_ref, g_ref, land_ref, o_ref):
        acc = jnp.zeros(o_ref.shape, F32)
        for k in range(N_DEV):
            acc = acc + jnp.where(me_ref[0] == k, g_ref[0], land_ref[k]).astype(F32)
        o_ref[...] = acc

    return pl.pallas_call(
        body, name=name, out_shape=S((a, b), F32),
        grid_spec=pltpu.PrefetchScalarGridSpec(
            num_scalar_prefetch=1, grid=(b // tb,),
            in_specs=[pl.BlockSpec((1, a, tb), lambda j, me_: (me_[0], 0, j)),
                      pl.BlockSpec((N_DEV, a, tb), lambda j, me_: (0, 0, j))],
            out_specs=pl.BlockSpec((a, tb), lambda j, me_: (0, j))),
    )(me, g, land)


class GradScatter:
    def __init__(self, local_shapes):
        self.local_shapes = local_shapes
        self.pending = []

    def emit(self, tag, index, grads):
        names = [n for n in BIG if n in grads]
        gs = [grads[n].reshape(N_DEV, -1, grads[n].shape[1]) for n in names]
        started = split_start("rs_start_" + tag, gs, True)
        self.pending.append((tag, index, names, started))
        return started[-1][0:1, 0:1]

    def finish(self, after):
        me = (4 * lax.axis_index("x") + 2 * lax.axis_index("y") + lax.axis_index("c")).astype(jnp.int32).reshape(1)
        shards = {n: [None] * self.local_shapes[n][0] for n in BIG}
        for tag, index, names, started in self.pending:
            gs, lands = split_wait("rs_wait_" + tag, started, True, after)
            for n, g, land in zip(names, gs, lands):
                shards[n][index] = sum_landed("rs_sum_%s_%s" % (tag, n), g, land, me)
        out = {}
        for n in BIG:
            seg = jnp.stack(shards[n])
            out[n] = seg.transpose(0, 2, 1) if n in TRANSPOSED else seg
        return out


def gather_small(name, local, names):
    flat = _flatten([local[n] for n in names], LANE, 1)
    got = all_gather(name, flat).reshape(N_DEV, -1)
    full, off = {}, 0
    for n in names:
        size = math.prod(local[n].shape)
        seg = got[:, off:off + size].reshape((N_DEV,) + local[n].shape)
        full[n] = seg.transpose(1, 2, 0, 3).reshape(seg.shape[1], seg.shape[2], -1)
        off += size
    return full


def all_reduce_small(grads, names):
    flat = _flatten([grads[n] for n in names], LANE, SUBLANE)
    total = sum_devices("ar_sum", all_gather("ar_gather", flat)).reshape(-1)
    out, off = {}, 0
    for nm in names:
        size = math.prod(grads[nm].shape)
        out[nm] = total[off:off + size].reshape(grads[nm].shape)
        off += size
    return out


def adamw(name, w, g, m, v):
    shape = w.shape
    cols = shape[-1]
    rows = math.prod(shape[:-1])
    tr = _pick(rows, (256, 128, 64, 32, 16, 8)) if rows * cols > 256 * 1024 else rows
    c1 = 1.0 - ADAM_B1 ** ADAM_STEP
    c2 = 1.0 - ADAM_B2 ** ADAM_STEP

    def body(w_ref, g_ref, m_ref, v_ref, d_ref, nm_ref, nv_ref):
        gv = g_ref[...]
        m2 = ADAM_B1 * m_ref[...] + (1.0 - ADAM_B1) * gv
        v2 = ADAM_B2 * v_ref[...] + (1.0 - ADAM_B2) * jnp.square(gv)
        d_ref[...] = -ADAM_LR * ((m2 / c1) / (jnp.sqrt(v2 / c2) + ADAM_EPS) + ADAM_WD * w_ref[...])
        nm_ref[...] = m2
        nv_ref[...] = v2

    spec = pl.BlockSpec((tr, cols), lambda i: (i, 0))
    outs = pl.pallas_call(
        body, name=name, grid=(rows // tr,), in_specs=[spec] * 4, out_specs=[spec] * 3,
        out_shape=[S((rows, cols), F32)] * 3,
        compiler_params=pltpu.CompilerParams(dimension_semantics=("parallel",)),
    )(*[a.reshape(rows, cols) for a in (w, g, m, v)])
    return tuple(o.reshape(shape) for o in outs)


def kernel(x, p, e_w_in, e_conv_a_w, e_conv_a_b, e_ln_a_g, e_ln_a_b, e_conv_b_w, e_conv_b_b, e_dt_bias, e_a_log, e_d_skip, e_norm_b_g, e_w_out, o_w_in, o_conv_w, o_w_out, f_w_up, f_conv_w, f_conv_b, f_w_down, ple_w_proj, ple_w_gate, ln_g, ln_b, loss_target, m_e_w_in, m_e_conv_a_w, m_e_conv_a_b, m_e_ln_a_g, m_e_ln_a_b, m_e_conv_b_w, m_e_conv_b_b, m_e_dt_bias, m_e_a_log, m_e_d_skip, m_e_norm_b_g, m_e_w_out, m_o_w_in, m_o_conv_w, m_o_w_out, m_f_w_up, m_f_conv_w, m_f_conv_b, m_f_w_down, m_ple_w_proj, m_ple_w_gate, m_ln_g, m_ln_b, v_e_w_in, v_e_conv_a_w, v_e_conv_a_b, v_e_ln_a_g, v_e_ln_a_b, v_e_conv_b_w, v_e_conv_b_b, v_e_dt_bias, v_e_a_log, v_e_d_skip, v_e_norm_b_g, v_e_w_out, v_o_w_in, v_o_conv_w, v_o_w_out, v_f_w_up, v_f_conv_w, v_f_conv_b, v_f_w_down, v_ple_w_proj, v_ple_w_gate, v_ln_g, v_ln_b):
    args = locals()
    local = {n: args[n] for n in WEIGHTS}
    mom = {n: args["m_" + n] for n in WEIGHTS}
    var = {n: args["v_" + n] for n in WEIGHTS}

    full = {n: local[n] for n in REPLICATED}
    big, fetch, token = gather_big(local)
    full.update(big)
    full.update(gather_small("ag_small", local, SMALL_SHARDED))

    scatter = GradScatter({n: local[n].shape for n in BIG})
    loss_local, grad_x, grads = local_step(x[0] + token, p[:, 0], full, loss_target[0], fetch, scatter.emit)
    loss = lax.psum(loss_local, MESH_AXES)

    grads["ln_b"] = grads["ln_b"] + scatter.pending[-1][3][-1][0, 0]
    small = all_reduce_small(grads, REPLICATED + SMALL_SHARDED)
    dev = 4 * lax.axis_index("x") + 2 * lax.axis_index("y") + lax.axis_index("c")
    g_local = {n: small[n] for n in REPLICATED}
    for n in SMALL_SHARDED:
        width = local[n].shape[2]
        g_local[n] = lax.dynamic_slice_in_dim(small[n], dev * width, width, axis=2)
    delta, new_m, new_v = {}, {}, {}
    for n in REPLICATED + SMALL_SHARDED:
        delta[n], new_m[n], new_v[n] = adamw("adamw_" + n, local[n], g_local[n], mom[n], var[n])

    done = sum(delta[n].reshape(-1)[0] for n in REPLICATED + SMALL_SHARDED)
    g_local.update(scatter.finish(delta[SMALL_SHARDED[-1]] + 0.0 * done))
    for n in BIG:
        delta[n], new_m[n], new_v[n] = adamw("adamw_" + n, local[n], g_local[n], mom[n], var[n])
    return (loss, grad_x[None], *[g_local[n] for n in WEIGHTS], *[delta[n] for n in WEIGHTS],
            *[new_m[n] for n in WEIGHTS], *[new_v[n] for n in WEIGHTS])
```

```python
import functools
import math

import jax
import jax.numpy as jnp
from jax import lax
from jax.experimental import pallas as pl
from jax.experimental.pallas import tpu as pltpu

F32 = jnp.float32
MXU_DTYPE = jnp.bfloat16
MESH_AXES = ("x", "y", "c")
N_DEV = 8
LANE = 128
SUBLANE = 8
ROWWISE_VMEM_BUDGET = 20 * 1024 * 1024
MM_TILES = (1408, 1024, 512, 256, 128)
EXCHANGE_DTYPE = jnp.bfloat16
LN_EPS = 1e-5
CHUNK = 64
HEAD_DIM = 64
N_GROUPS = 4
N_STATE = 128
CONV_PAD = 32
CONV_ROWS = 256
ADAM_LR, ADAM_B1, ADAM_B2, ADAM_EPS, ADAM_WD, ADAM_STEP = 0.001, 0.9, 0.999, 1e-08, 0.01, 10

BIG = ("e_w_in", "e_w_out", "o_w_in", "o_w_out", "f_w_up", "f_w_down", "ple_w_proj", "ple_w_gate")
SMALL_SHARDED = ("e_conv_a_w", "e_conv_b_w", "o_conv_w", "f_conv_w", "ln_g", "ln_b")
REPLICATED = ("e_conv_a_b", "e_ln_a_g", "e_ln_a_b", "e_conv_b_b", "e_dt_bias", "e_a_log", "e_d_skip",
              "e_norm_b_g", "f_conv_b")
TRANSPOSED = ("e_w_in", "o_w_in", "f_w_up", "ple_w_proj")
WEIGHTS = ("e_w_in", "e_conv_a_w", "e_conv_a_b", "e_ln_a_g", "e_ln_a_b", "e_conv_b_w", "e_conv_b_b", "e_dt_bias",
           "e_a_log", "e_d_skip", "e_norm_b_g", "e_w_out", "o_w_in", "o_conv_w", "o_w_out", "f_w_up", "f_conv_w",
           "f_conv_b", "f_w_down", "ple_w_proj", "ple_w_gate", "ln_g", "ln_b")

S = jax.ShapeDtypeStruct


class Win:
    def __init__(self, arr, w=None, idx=0, coef=1.0):
        self.arr, self.w, self.idx, self.coef = arr, (arr.shape[1] if w is None else w), idx, coef


def _win(a):
    return a if isinstance(a, Win) else Win(a)


def _pick(n, prefs):
    for p in prefs:
        if p <= n and n % p == 0:
            return p
    return n


_MM_DIMS = {"nn": (1, 0), "nt": (1, 1), "tn": (0, 0)}


def mm(name, a, b, mode, out_dtype=F32):
    ca, cb = _MM_DIMS[mode]
    kdim = a.shape[ca]
    m = a.shape[1 - ca]
    n = b.shape[1 - cb]
    assert b.shape[cb] == kdim, (name, a.shape, b.shape, mode)
    tm = _pick(m, MM_TILES)
    tn = _pick(n, MM_TILES)
    tk = kdim if kdim <= MM_TILES[0] else _pick(kdim, MM_TILES)
    nk = kdim // tk
    own_acc = nk > 1 and out_dtype != F32

    def body(a_ref, b_ref, o_ref, *scratch):
        acc_ref = scratch[0] if own_acc else o_ref
        d = lax.dot_general(a_ref[...].astype(MXU_DTYPE), b_ref[...].astype(MXU_DTYPE),
                            (((ca,), (cb,)), ((), ())), preferred_element_type=F32)
        if nk == 1:
            o_ref[...] = d.astype(o_ref.dtype)
        else:
            k = pl.program_id(2)

            @pl.when(k == 0)
            def _():
                acc_ref[...] = d

            @pl.when(k > 0)
            def _():
                acc_ref[...] += d

            if own_acc:
                @pl.when(k == nk - 1)
                def _():
                    o_ref[...] = acc_ref[...].astype(o_ref.dtype)

    a_spec = pl.BlockSpec((tm, tk), lambda i, j, k: (i, k)) if ca == 1 else pl.BlockSpec((tk, tm), lambda i, j, k: (k, i))
    b_spec = pl.BlockSpec((tk, tn), lambda i, j, k: (k, j)) if cb == 0 else pl.BlockSpec((tn, tk), lambda i, j, k: (j, k))
    return pl.pallas_call(
        body, name=name, grid=(m // tm, n // tn, nk),
        in_specs=[a_spec, b_spec], out_specs=pl.BlockSpec((tm, tn), lambda i, j, k: (i, j)),
        out_shape=S((m, n), out_dtype), scratch_shapes=[pltpu.VMEM((tm, tn), F32)] if own_acc else [],
        compiler_params=pltpu.CompilerParams(dimension_semantics=("parallel", "parallel", "arbitrary")),
    )(a, b)


def _row_block(t, widths):
    tb = 512
    while tb > SUBLANE and (t % tb or tb * sum(widths) * 8 > ROWWISE_VMEM_BUDGET):
        tb //= 2
    return tb


def _tok_spec(tb, w):
    return pl.BlockSpec((tb, w.w), functools.partial(lambda i, idx: (i, idx), idx=w.idx))


def _par_spec(p):
    return pl.BlockSpec((1, p.shape[1]), lambda i: (0, 0))


def rowwise(name, fn, tok, par, out_widths, red_widths=(), out_dtypes=None):
    tok = [_win(t) for t in tok]
    t = tok[0].arr.shape[0]
    tb = _row_block(t, [w.w for w in tok] + list(out_widths))
    n_tok, n_par, n_out = len(tok), len(par), len(out_widths)
    out_dtypes = [F32] * n_out if out_dtypes is None else out_dtypes

    def body(*refs):
        ins = [r[...] for r in refs[:n_tok + n_par]]
        res = fn(*ins)
        out_refs = refs[n_tok + n_par:n_tok + n_par + n_out]
        red_refs = refs[n_tok + n_par + n_out:]
        for r, v in zip(out_refs, res[:n_out]):
            r[...] = v.astype(r.dtype)
        if red_refs:
            @pl.when(pl.program_id(0) == 0)
            def _():
                for r in red_refs:
                    r[...] = jnp.zeros_like(r)
            for r, v in zip(red_refs, res[n_out:]):
                r[...] += v

    outs = pl.pallas_call(
        body, name=name, grid=(t // tb,),
        in_specs=[_tok_spec(tb, w) for w in tok] + [_par_spec(p) for p in par],
        out_specs=[pl.BlockSpec((tb, w), lambda i: (i, 0)) for w in out_widths]
        + [pl.BlockSpec((1, w), lambda i: (0, 0)) for w in red_widths],
        out_shape=[S((t, w), dt) for w, dt in zip(out_widths, out_dtypes)] + [S((1, w), F32) for w in red_widths],
        compiler_params=pltpu.CompilerParams(dimension_semantics=("arbitrary",)),
    )(*[w.arr for w in tok], *par)
    return outs


def rowwise_bwd(name, fn, tok, par, cts, d_dtypes=None):
    tok = [_win(t) for t in tok]
    cts = [[_win(c) for c in group] for group in cts]
    d_dtypes = [(F32,)] * len(tok) if d_dtypes is None else d_dtypes
    t = tok[0].arr.shape[0]
    flat_cts = [c for group in cts for c in group]
    d_outs = [(i, w.w, dt) for i, (w, dts) in enumerate(zip(tok, d_dtypes)) for dt in dts]
    tb = _row_block(t, [w.w for w in tok] + [c.w for c in flat_cts] + [w for _, w, _ in d_outs])
    n_tok, n_par, n_ct, n_d = len(tok), len(par), len(flat_cts), len(d_outs)

    def body(*refs):
        tok_vals = [r[...] for r in refs[:n_tok]]
        par_vals = [r[...] for r in refs[n_tok:n_tok + n_par]]
        ct_refs = refs[n_tok + n_par:n_tok + n_par + n_ct]
        d_refs = refs[n_tok + n_par + n_ct:n_tok + n_par + n_ct + n_d]
        dp_refs = refs[n_tok + n_par + n_ct + n_d:]
        ct_vals, pos = [], 0
        for group in cts:
            acc = None
            for c in group:
                v = ct_refs[pos][...]
                if c.coef != 1.0:
                    v = v * c.coef
                acc = v if acc is None else acc + v
                pos += 1
            ct_vals.append(acc)
        _, vjp = jax.vjp(lambda *a: tuple(fn(*a)), *tok_vals, *par_vals)
        grads = vjp(tuple(ct_vals))
        for r, (i, _, _) in zip(d_refs, d_outs):
            r[...] = grads[i].astype(r.dtype)
        if dp_refs:
            @pl.when(pl.program_id(0) == 0)
            def _():
                for r in dp_refs:
                    r[...] = jnp.zeros_like(r)
            for r, v in zip(dp_refs, grads[n_tok:]):
                r[...] += v

    outs = pl.pallas_call(
        body, name=name, grid=(t // tb,),
        in_specs=[_tok_spec(tb, w) for w in tok] + [_par_spec(p) for p in par] + [_tok_spec(tb, c) for c in flat_cts],
        out_specs=[pl.BlockSpec((tb, w), lambda i: (i, 0)) for _, w, _ in d_outs] + [_par_spec(p) for p in par],
        out_shape=[S((t, w), dt) for _, w, dt in d_outs] + [S(p.shape, F32) for p in par],
        compiler_params=pltpu.CompilerParams(dimension_semantics=("arbitrary",)),
    )(*[w.arr for w in tok], *par, *[c.arr for c in flat_cts])
    return outs[:n_d], outs[n_d:]


def _sigmoid(x):
    return 1.0 / (1.0 + jnp.exp(-x))


def _silu(x):
    return x * _sigmoid(x)


def _ln(x, g, b):
    mu = jnp.mean(x, axis=-1, keepdims=True)
    var = jnp.mean(jnp.square(x - mu), axis=-1, keepdims=True)
    return (x - mu) * lax.rsqrt(var + LN_EPS) * g + b


def f_glu(ua, ug):
    return (ua * _sigmoid(ug),)


def f_ln_silu(a1, g, b):
    return (_silu(_ln(a1, g, b)),)


def f_silu3(a, b, c):
    return (_silu(a), _silu(b), _silu(c))


def f_softplus(dt_raw, bias):
    return (jax.nn.softplus(dt_raw + bias),)


def f_gate_rms(yssd, z, g):
    y = yssd * _silu(z)
    return (y * lax.rsqrt(jnp.mean(jnp.square(y), axis=-1, keepdims=True) + LN_EPS) * g,)


def f_ln(pre, g, b):
    return (_ln(pre, g, b),)


def f_mul(a, b):
    return (a * b,)


def f_gate_mul(pp, gt):
    return (pp * _sigmoid(gt),)


def conv_fwd(name, x, w, b):
    x = _win(x)
    t, c = x.arr.shape[0], x.w
    kw = w.shape[0]
    cb = LANE
    off = x.idx * (c // cb)
    rows = min(CONV_ROWS, t)
    has_b = b is not None

    def body(*refs):
        if has_b:
            x_ref, w_ref, b_ref, y_ref, xp_ref = refs
        else:
            x_ref, w_ref, y_ref, xp_ref = refs
        xp_ref[0:CONV_PAD, :] = jnp.zeros((CONV_PAD, cb), F32)
        xp_ref[CONV_PAD:CONV_PAD + t, :] = x_ref[...]

        def step(s, carry):
            base = pl.multiple_of(s * rows, rows)
            acc = jnp.zeros((rows, cb), F32)
            if has_b:
                acc = acc + b_ref[...]
            for k in range(kw):
                acc = acc + w_ref[k:k + 1, :] * xp_ref[pl.ds(base + CONV_PAD - (kw - 1) + k, rows), :]
            y_ref[pl.ds(base, rows), :] = acc
            return carry

        lax.fori_loop(0, t // rows, step, 0)

    in_specs = [pl.BlockSpec((t, cb), lambda j: (0, off + j)), pl.BlockSpec((kw, cb), lambda j: (0, j))]
    args = [x.arr, w]
    if has_b:
        in_specs.append(pl.BlockSpec((1, cb), lambda j: (0, j)))
        args.append(b)
    return pl.pallas_call(
        body, name=name, grid=(c // cb,), in_specs=in_specs,
        out_specs=pl.BlockSpec((t, cb), lambda j: (0, j)), out_shape=S((t, c), F32),
        scratch_shapes=[pltpu.VMEM((CONV_PAD + t, cb), F32)],
        compiler_params=pltpu.CompilerParams(dimension_semantics=("parallel",)),
    )(*args)


def conv_bwd(name, x, dy, w, dx_dtype=F32):
    x, dy = _win(x), _win(dy)
    t, c = x.arr.shape[0], x.w
    kw = w.shape[0]
    cb = LANE
    xoff = x.idx * (c // cb)
    dyoff = dy.idx * (c // cb)
    rows = min(CONV_ROWS, t)

    def body(x_ref, dy_ref, w_ref, dx_ref, dw_ref, db_ref, xp_ref, dyp_ref):
        xp_ref[0:CONV_PAD, :] = jnp.zeros((CONV_PAD, cb), F32)
        xp_ref[CONV_PAD:CONV_PAD + t, :] = x_ref[...]
        dyp_ref[0:t, :] = dy_ref[...]
        dyp_ref[t:t + CONV_PAD, :] = jnp.zeros((CONV_PAD, cb), F32)

        def fold(v):
            return jnp.sum(v.reshape(rows // SUBLANE, SUBLANE, cb), axis=0)

        def step(s, carry):
            base = pl.multiple_of(s * rows, rows)
            dyc = dy_ref[pl.ds(base, rows), :]
            acc = jnp.zeros((rows, cb), F32)
            new = []
            for k in range(kw):
                acc = acc + w_ref[k:k + 1, :] * dyp_ref[pl.ds(base + (kw - 1) - k, rows), :]
                new.append(carry[k] + fold(dyc * xp_ref[pl.ds(base + CONV_PAD - (kw - 1) + k, rows), :]))
            new.append(carry[kw] + fold(dyc))
            dx_ref[pl.ds(base, rows), :] = acc.astype(dx_ref.dtype)
            return tuple(new)

        init = tuple(jnp.zeros((SUBLANE, cb), F32) for _ in range(kw + 1))
        parts = lax.fori_loop(0, t // rows, step, init)
        for k in range(kw):
            dw_ref[k:k + 1, :] = jnp.sum(parts[k], axis=0, keepdims=True)
        db_ref[...] = jnp.sum(parts[kw], axis=0, keepdims=True)

    return pl.pallas_call(
        body, name=name, grid=(c // cb,),
        in_specs=[pl.BlockSpec((t, cb), lambda j: (0, xoff + j)), pl.BlockSpec((t, cb), lambda j: (0, dyoff + j)),
                  pl.BlockSpec((kw, cb), lambda j: (0, j))],
        out_specs=[pl.BlockSpec((t, cb), lambda j: (0, j)), pl.BlockSpec((kw, cb), lambda j: (0, j)),
                   pl.BlockSpec((1, cb), lambda j: (0, j))],
        out_shape=[S((t, c), dx_dtype), S((kw, c), F32), S((1, c), F32)],
        scratch_shapes=[pltpu.VMEM((CONV_PAD + t, cb), F32), pltpu.VMEM((CONV_PAD + t, cb), F32)],
        compiler_params=pltpu.CompilerParams(dimension_semantics=("parallel",)),
    )(x.arr, dy.arr, w)


def gated_conv_fwd(name, hpre, w, b, out_dtype):
    t, c2 = hpre.shape
    ff = c2 // 2
    kw = w.shape[0]
    cb = LANE
    nb = ff // cb
    rows = min(CONV_ROWS, t)

    def body(h1_ref, h2_ref, w1_ref, w2_ref, b1_ref, b2_ref, y_ref, xp1_ref, xp2_ref):
        for xp_ref, h_ref in ((xp1_ref, h1_ref), (xp2_ref, h2_ref)):
            xp_ref[0:CONV_PAD, :] = jnp.zeros((CONV_PAD, cb), F32)
            xp_ref[CONV_PAD:CONV_PAD + rows, :] = h_ref[0:rows, :]

        def chunk(base, first):
            def win(xp_ref, h_ref, k):
                if first:
                    return xp_ref[CONV_PAD - (kw - 1) + k:CONV_PAD - (kw - 1) + k + rows, :]
                return h_ref[pl.ds(base - (kw - 1) + k, rows), :]

            h1 = jnp.zeros((rows, cb), F32) + b1_ref[...]
            h2 = jnp.zeros((rows, cb), F32) + b2_ref[...]
            for k in range(kw):
                h1 = h1 + w1_ref[k:k + 1, :] * win(xp1_ref, h1_ref, k)
                h2 = h2 + w2_ref[k:k + 1, :] * win(xp2_ref, h2_ref, k)
            y_ref[pl.ds(base, rows), :] = (_silu(h1) * h2).astype(y_ref.dtype)

        def step(s, carry):
            chunk(pl.multiple_of(s * rows, rows), False)
            return carry

        chunk(0, True)
        lax.fori_loop(1, t // rows, step, 0)

    col1 = lambda r: pl.BlockSpec((r, cb), lambda j: (0, j))
    col2 = lambda r: pl.BlockSpec((r, cb), lambda j: (0, nb + j))
    return pl.pallas_call(
        body, name=name, grid=(nb,),
        in_specs=[col1(t), col2(t), col1(kw), col2(kw), col1(1), col2(1)],
        out_specs=col1(t), out_shape=S((t, ff), out_dtype),
        scratch_shapes=[pltpu.VMEM((CONV_PAD + rows, cb), F32)] * 2,
        compiler_params=pltpu.CompilerParams(dimension_semantics=("parallel",)),
    )(hpre, hpre, w, w, b, b)


def gated_conv_bwd(name, hpre, dact, w, b, dx_dtype):
    t, c2 = hpre.shape
    ff = c2 // 2
    kw = w.shape[0]
    cb = LANE
    nb = ff // cb
    rows = min(CONV_ROWS, t)

    def body(own_ref, oth_ref, da_ref, wo_ref, wt_ref, bo_ref, bt_ref, dx_ref, dw_ref, db_ref,
             xpo_ref, xpt_ref, dhp_ref):
        for xp_ref, h_ref in ((xpo_ref, own_ref), (xpt_ref, oth_ref)):
            xp_ref[0:CONV_PAD, :] = jnp.zeros((CONV_PAD, cb), F32)
            xp_ref[CONV_PAD:CONV_PAD + rows, :] = h_ref[0:rows, :]
        dhp_ref[t:t + CONV_PAD, :] = jnp.zeros((CONV_PAD, cb), F32)

        def fold(v):
            return jnp.sum(v.reshape(rows // SUBLANE, SUBLANE, cb), axis=0)

        def first_pass(own_is_gate):
            def chunk(base, first, carry):
                def win(xp_ref, h_ref, k):
                    if first:
                        return xp_ref[CONV_PAD - (kw - 1) + k:CONV_PAD - (kw - 1) + k + rows, :]
                    return h_ref[pl.ds(base - (kw - 1) + k, rows), :]

                ho = jnp.zeros((rows, cb), F32) + bo_ref[...]
                ht = jnp.zeros((rows, cb), F32) + bt_ref[...]
                for k in range(kw):
                    ho = ho + wo_ref[k:k + 1, :] * win(xpo_ref, own_ref, k)
                    ht = ht + wt_ref[k:k + 1, :] * win(xpt_ref, oth_ref, k)
                da = da_ref[pl.ds(base, rows), :]
                if own_is_gate:
                    sg = _sigmoid(ho)
                    dh = da * ht * (sg * (1.0 + ho * (1.0 - sg)))
                else:
                    dh = da * _silu(ht)
                dhp_ref[pl.ds(base, rows), :] = dh
                new = [carry[k] + fold(dh * win(xpo_ref, own_ref, k)) for k in range(kw)]
                new.append(carry[kw] + fold(dh))
                return tuple(new)

            init = tuple(jnp.zeros((SUBLANE, cb), F32) for _ in range(kw + 1))
            parts = lax.fori_loop(1, t // rows, lambda s, carry: chunk(pl.multiple_of(s * rows, rows), False, carry),
                                  chunk(0, True, init))
            for k in range(kw):
                dw_ref[k:k + 1, :] = jnp.sum(parts[k], axis=0, keepdims=True)
            db_ref[...] = jnp.sum(parts[kw], axis=0, keepdims=True)

        half = pl.program_id(0)

        @pl.when(half == 0)
        def _():
            first_pass(True)

        @pl.when(half == 1)
        def _():
            first_pass(False)

        def second(s, carry):
            base = pl.multiple_of(s * rows, rows)
            acc = jnp.zeros((rows, cb), F32)
            for k in range(kw):
                acc = acc + wo_ref[k:k + 1, :] * dhp_ref[pl.ds(base + (kw - 1) - k, rows), :]
            dx_ref[pl.ds(base, rows), :] = acc.astype(dx_ref.dtype)
            return carry

        lax.fori_loop(0, t // rows, second, 0)

    own = lambda r: pl.BlockSpec((r, cb), lambda h, j: (0, h * nb + j))
    oth = lambda r: pl.BlockSpec((r, cb), lambda h, j: (0, (1 - h) * nb + j))
    return pl.pallas_call(
        body, name=name, grid=(2, nb),
        in_specs=[own(t), oth(t), pl.BlockSpec((t, cb), lambda h, j: (0, j)), own(kw), oth(kw), own(1), oth(1)],
        out_specs=[own(t), own(kw), own(1)],
        out_shape=[S((t, c2), dx_dtype), S((kw, c2), F32), S((1, c2), F32)],
        scratch_shapes=[pltpu.VMEM((CONV_PAD + rows, cb), F32)] * 2 + [pltpu.VMEM((CONV_PAD + t, cb), F32)],
        compiler_params=pltpu.CompilerParams(dimension_semantics=("parallel", "parallel")),
    )(hpre, hpre, dact, w, w, b, b)


def _bdot(a, b, ca, cb):
    return lax.dot_general(a.astype(MXU_DTYPE), b.astype(MXU_DTYPE), (((ca,), (cb,)), ((0,), (0,))),
                           preferred_element_type=F32)


@jax.custom_vjp
def bmm_nn(a, b):
    return _bdot(a, b, 2, 1)


bmm_nn.defvjp(lambda a, b: (_bdot(a, b, 2, 1), (a, b)),
              lambda r, g: (_bdot(g, r[1], 2, 2), _bdot(r[0], g, 1, 1)))


@jax.custom_vjp
def bmm_tn(a, b):
    return _bdot(a, b, 1, 1)


bmm_tn.defvjp(lambda a, b: (_bdot(a, b, 1, 1), (a, b)),
              lambda r, g: (_bdot(r[1], g, 2, 2), _bdot(r[0], g, 2, 1)))


@jax.custom_vjp
def bmm_nt(a, b):
    return _bdot(a, b, 2, 2)


bmm_nt.defvjp(lambda a, b: (_bdot(a, b, 2, 2), (a, b)),
              lambda r, g: (_bdot(g, r[1], 2, 1), _bdot(g, r[0], 1, 1)))


def ssd_chunk(x, dt, dt_row, bm, cm, hprev, a_log, dsk):
    hg, ln, _ = x.shape
    n = bm.shape[1]
    ii = lax.broadcasted_iota(jnp.int32, (ln, ln), 0)
    jj = lax.broadcasted_iota(jnp.int32, (ln, ln), 1)
    tril, triu = (ii >= jj)[None], (ii <= jj)[None]
    a = -jnp.exp(a_log)
    da = dt * a
    da_row = dt_row * a
    cum_c = jnp.sum(jnp.where(tril, da_row, 0.0), axis=2, keepdims=True)
    cum_r = jnp.sum(jnp.where(triu, da, 0.0), axis=1, keepdims=True)
    last = jnp.sum(da, axis=1, keepdims=True)
    decay = jnp.where(tril, jnp.exp(jnp.where(tril, cum_c - cum_r, 0.0)), 0.0)
    cb = bmm_nt(cm[None], bm[None])
    y_diag = bmm_nn(cb * decay * dt_row, x)
    bb = jnp.broadcast_to(bm[None], (hg, ln, n))
    cc = jnp.broadcast_to(cm[None], (hg, ln, n))
    states = bmm_tn(x * (jnp.exp(last - cum_c) * dt), bb)
    y_off = bmm_nt(cc, hprev) * jnp.exp(cum_c)
    hnew = hprev * jnp.exp(last) + states
    return y_diag + y_off + dsk * x, hnew


def _ssd_dims(xs, bm, a_log):
    t = xs.shape[0]
    h = a_log.shape[0]
    return h, t, xs.shape[1] // h, h // N_GROUPS, bm.shape[1] // N_GROUPS, t // CHUNK


def _heads_of(ref, g, hg, p):
    return jnp.stack([ref[:, (g * hg + i) * p:(g * hg + i + 1) * p] for i in range(hg)])


def _cols_of(ref, g, hg):
    return jnp.stack([ref[:, g * hg + i:g * hg + i + 1] for i in range(hg)])


def dt_rows(dt, h):
    t = dt.shape[0]
    return dt[:, :h].T.reshape(h, t // CHUNK, 1, CHUNK).transpose(1, 0, 2, 3)


def dt_cols(dtr, lanes):
    nc, h, _, ln = dtr.shape
    return jnp.pad(dtr.transpose(1, 0, 2, 3).reshape(h, nc * ln).T, ((0, 0), (0, lanes - h)))


def ssd_fwd(name, xs, dt, dtr, bm, cm, a_log, dsk):
    h, t, p, hg, n, nc = _ssd_dims(xs, bm, a_log)

    def body(al_ref, dk_ref, x_ref, dt_ref, dtr_ref, b_ref, c_ref, y_ref, hp_ref, h_scr):
        @pl.when(pl.program_id(0) == 0)
        def _():
            h_scr[...] = jnp.zeros_like(h_scr)

        for g in range(N_GROUPS):
            hs, ns = slice(g * hg, (g + 1) * hg), slice(g * n, (g + 1) * n)
            hprev = h_scr[hs]
            hp_ref[hs, 0] = hprev
            y, hnew = ssd_chunk(_heads_of(x_ref, g, hg, p), _cols_of(dt_ref, g, hg), dtr_ref[0, hs], b_ref[:, ns],
                                c_ref[:, ns], hprev, al_ref[hs], dk_ref[hs])
            for i in range(hg):
                y_ref[:, (g * hg + i) * p:(g * hg + i + 1) * p] = y[i]
            h_scr[hs] = hnew

    head = pl.BlockSpec((h, 1, 1), lambda c: (0, 0, 0))
    row = lambda w: pl.BlockSpec((CHUNK, w), lambda c: (c, 0))
    return pl.pallas_call(
        body, name=name, grid=(nc,),
        in_specs=[head, head, row(h * p), row(dt.shape[1]), pl.BlockSpec((1, h, 1, CHUNK), lambda c: (c, 0, 0, 0)),
                  row(N_GROUPS * n), row(N_GROUPS * n)],
        out_specs=[row(h * p), pl.BlockSpec((h, 1, p, n), lambda c: (0, c, 0, 0))],
        out_shape=[S((t, h * p), F32), S((h, nc, p, n), F32)],
        scratch_shapes=[pltpu.VMEM((h, p, n), F32)],
        compiler_params=pltpu.CompilerParams(dimension_semantics=("arbitrary",)),
    )(a_log, dsk, xs, dt, dtr, bm, cm)


def ssd_bwd(name, xs, dt, dtr, bm, cm, a_log, dsk, hp, dy):
    h, t, p, hg, n, nc = _ssd_dims(xs, bm, a_log)

    def body(al_ref, dk_ref, x_ref, dt_ref, dtr_ref, b_ref, c_ref, hp_ref, dy_ref,
             dx_ref, ddt_ref, ddtr_ref, db_ref, dc_ref, dal_ref, ddk_ref, dh_scr):
        @pl.when(pl.program_id(0) == 0)
        def _():
            dh_scr[...] = jnp.zeros_like(dh_scr)
            dal_ref[...] = jnp.zeros_like(dal_ref)
            ddk_ref[...] = jnp.zeros_like(ddk_ref)

        ddt_ref[...] = jnp.zeros_like(ddt_ref)
        for g in range(N_GROUPS):
            hs, ns = slice(g * hg, (g + 1) * hg), slice(g * n, (g + 1) * n)
            _, vjp = jax.vjp(ssd_chunk, _heads_of(x_ref, g, hg, p), _cols_of(dt_ref, g, hg), dtr_ref[0, hs],
                             b_ref[:, ns], c_ref[:, ns], hp_ref[hs, 0], al_ref[hs], dk_ref[hs])
            gx, gdt, gdtr, gb, gc, ghp, gal, gdk = vjp((_heads_of(dy_ref, g, hg, p), dh_scr[hs]))
            for i in range(hg):
                dx_ref[:, (g * hg + i) * p:(g * hg + i + 1) * p] = gx[i]
                ddt_ref[:, g * hg + i:g * hg + i + 1] = gdt[i]
            ddtr_ref[0, hs] = gdtr
            db_ref[:, ns] = gb
            dc_ref[:, ns] = gc
            dh_scr[hs] = ghp
            dal_ref[hs] += gal
            ddk_ref[hs] += gdk

    head = pl.BlockSpec((h, 1, 1), lambda c: (0, 0, 0))
    row = lambda w: pl.BlockSpec((CHUNK, w), lambda c: (nc - 1 - c, 0))
    rows = pl.BlockSpec((1, h, 1, CHUNK), lambda c: (nc - 1 - c, 0, 0, 0))
    return pl.pallas_call(
        body, name=name, grid=(nc,),
        in_specs=[head, head, row(h * p), row(dt.shape[1]), rows, row(N_GROUPS * n), row(N_GROUPS * n),
                  pl.BlockSpec((h, 1, p, n), lambda c: (0, nc - 1 - c, 0, 0)), row(h * p)],
        out_specs=[row(h * p), row(dt.shape[1]), rows, row(N_GROUPS * n), row(N_GROUPS * n), head, head],
        out_shape=[S((t, h * p), F32), S(dt.shape, F32), S(dtr.shape, F32), S(bm.shape, F32), S(cm.shape, F32),
                   S((h, 1, 1), F32), S((h, 1, 1), F32)],
        scratch_shapes=[pltpu.VMEM((h, p, n), F32)],
        compiler_params=pltpu.CompilerParams(dimension_semantics=("arbitrary",)),
    )(a_log, dsk, xs, dt, dtr, bm, cm, hp, dy)


def _alpha(depth):
    return (2.0 * depth) ** 0.25


def _pad_lanes(v):
    return jnp.pad(v, ((0, 0), (0, LANE - v.shape[1])))


def split_even_weights(w, j):
    d = w["e_w_in"][j].shape[1]
    da = w["e_conv_a_w"].shape[2]
    db = w["e_norm_b_g"].shape[1]
    gn = N_GROUPS * N_STATE
    nh = w["e_dt_bias"].shape[1]
    main = 2 * da + 2 * db + 2 * gn
    win = w["e_w_in"][j]
    ox = 2 * da + db
    cw, cbias = w["e_conv_b_w"][j], w["e_conv_b_b"][j][None]
    return dict(
        d=d, da=da, db=db, gn=gn, nh=nh, main=main,
        win_main=win[:main], win_dt=jnp.pad(win[main:], ((0, LANE - nh), (0, 0))),
        caw=w["e_conv_a_w"][j], cab=w["e_conv_a_b"][j][None], lag=w["e_ln_a_g"][j][None], lab=w["e_ln_a_b"][j][None],
        cw_xs=cw[:, :db], cw_b=cw[:, db:db + gn], cw_c=cw[:, db + gn:],
        cb_xs=cbias[:, :db], cb_b=cbias[:, db:db + gn], cb_c=cbias[:, db + gn:],
        dt_bias=_pad_lanes(w["e_dt_bias"][j][None]), a_log=w["e_a_log"][j].reshape(nh, 1, 1),
        dsk=w["e_d_skip"][j].reshape(nh, 1, 1), norm_g=w["e_norm_b_g"][j][None],
        wout_a=w["e_w_out"][j][:da], wout_b=w["e_w_out"][j][da:],
    )


def even_fwd(tag, x, xm, lw, ln_g, ln_b, alpha):
    t = x.shape[0]
    da, db, gn, nh = lw["da"], lw["db"], lw["gn"], lw["nh"]
    u = mm(tag + "_win", xm, lw["win_main"], "nt")
    udt = mm(tag + "_windt", xm, lw["win_dt"], "nt")
    ua, ug, z, xs_pre = Win(u, da, 0), Win(u, da, 1), Win(u, db, 2 * da // db), Win(u, db, (2 * da + db) // db)
    b_pre, c_pre = Win(u, gn, (2 * da + 2 * db) // gn), Win(u, gn, (2 * da + 2 * db + gn) // gn)
    (a0,) = rowwise(tag + "_glu", f_glu, [ua, ug], [], [da])
    a1 = conv_fwd(tag + "_conva", a0, lw["caw"], lw["cab"])
    (ya,) = rowwise(tag + "_lna", f_ln_silu, [a1], [lw["lag"], lw["lab"]], [da], out_dtypes=[MXU_DTYPE])
    xs_c = conv_fwd(tag + "_convxs", xs_pre, lw["cw_xs"], lw["cb_xs"])
    b_c = conv_fwd(tag + "_convb", b_pre, lw["cw_b"], lw["cb_b"])
    c_c = conv_fwd(tag + "_convc", c_pre, lw["cw_c"], lw["cb_c"])
    xs, bm, cm = rowwise(tag + "_silu3", f_silu3, [xs_c, b_c, c_c], [], [db, gn, gn])
    (dt,) = rowwise(tag + "_dt", f_softplus, [udt], [lw["dt_bias"]], [LANE])
    dtr = dt_rows(dt, nh)
    yssd, hp = ssd_fwd(tag + "_ssd", xs, dt, dtr, bm, cm, lw["a_log"], lw["dsk"])
    (yb,) = rowwise(tag + "_gate", f_gate_rms, [yssd, z], [lw["norm_g"]], [db], out_dtypes=[MXU_DTYPE])
    ma = mm(tag + "_wouta", ya, lw["wout_a"], "nn")
    mb = mm(tag + "_woutb", yb, lw["wout_b"], "nn")

    def f_res(xv, mav, mbv, g, b):
        pre = alpha * xv + mav + mbv
        y = _ln(pre, g, b)
        return y, y, pre

    x1, x1m, pre = rowwise(tag + "_res", f_res, [x, ma, mb], [ln_g, ln_b], [x.shape[1]] * 3,
                           out_dtypes=[F32, MXU_DTYPE, F32])
    saved = dict(xm=xm, u=u, udt=udt, a0=a0, a1=a1, ya=ya, xs_c=xs_c, b_c=b_c, c_c=c_c, xs=xs, dt=dt, dtr=dtr, bm=bm,
                 cm=cm, hp=hp, yssd=yssd, yb=yb, pre=pre)
    return x1, x1m, saved


def even_bwd(tag, dx1_pieces, sv, lw, ln_g, ln_b, alpha):
    t = sv["u"].shape[0]
    da, db, gn, nh = lw["da"], lw["db"], lw["gn"], lw["nh"]
    u, xm = sv["u"], sv["xm"]
    mx = (MXU_DTYPE,)
    ua, ug, z, xs_pre = Win(u, da, 0), Win(u, da, 1), Win(u, db, 2 * da // db), Win(u, db, (2 * da + db) // db)
    b_pre, c_pre = Win(u, gn, (2 * da + 2 * db) // gn), Win(u, gn, (2 * da + 2 * db + gn) // gn)
    (dpre, dprem), (dg0, db0) = rowwise_bwd(tag + "_res_b", f_ln, [sv["pre"]], [ln_g, ln_b], [dx1_pieces],
                                            d_dtypes=[(F32, MXU_DTYPE)])
    dya = mm(tag + "_dya", dprem, lw["wout_a"], "nt")
    dyb = mm(tag + "_dyb", dprem, lw["wout_b"], "nt")
    dwout_a = mm(tag + "_dwouta", sv["ya"], dprem, "tn", EXCHANGE_DTYPE)
    dwout_b = mm(tag + "_dwoutb", sv["yb"], dprem, "tn", EXCHANGE_DTYPE)
    (dyssd, dz), (dnorm_g,) = rowwise_bwd(tag + "_gate_b", f_gate_rms, [sv["yssd"], z], [lw["norm_g"]], [[dyb]],
                                          d_dtypes=[(F32,), mx])
    dxs, ddt, ddtr, dbm, dcm, dalog, ddsk = ssd_bwd(tag + "_ssd_b", sv["xs"], sv["dt"], sv["dtr"], sv["bm"], sv["cm"],
                                                    lw["a_log"], lw["dsk"], sv["hp"], dyssd)
    ddt_pieces = [ddt, dt_cols(ddtr, ddt.shape[1])]
    (dudt,), (ddt_bias,) = rowwise_bwd(tag + "_dt_b", f_softplus, [sv["udt"]], [lw["dt_bias"]], [ddt_pieces],
                                       d_dtypes=[mx])
    (dxs_c, db_c, dc_c), _ = rowwise_bwd(tag + "_silu3_b", f_silu3, [sv["xs_c"], sv["b_c"], sv["c_c"]], [],
                                         [[dxs], [dbm], [dcm]])
    dxs_pre, dcw_xs, dcb_xs = conv_bwd(tag + "_convxs_b", xs_pre, dxs_c, lw["cw_xs"], MXU_DTYPE)
    db_pre, dcw_b, dcb_b = conv_bwd(tag + "_convb_b", b_pre, db_c, lw["cw_b"], MXU_DTYPE)
    dc_pre, dcw_c, dcb_c = conv_bwd(tag + "_convc_b", c_pre, dc_c, lw["cw_c"], MXU_DTYPE)
    (da1,), (dlag, dlab) = rowwise_bwd(tag + "_lna_b", f_ln_silu, [sv["a1"]], [lw["lag"], lw["lab"]], [[dya]])
    da0, dcaw, dcab = conv_bwd(tag + "_conva_b", sv["a0"], da1, lw["caw"])
    (dua, dug), _ = rowwise_bwd(tag + "_glu_b", f_glu, [ua, ug], [], [[da0]], d_dtypes=[mx, mx])
    du = jnp.concatenate([dua, dug, dz, dxs_pre, db_pre, dc_pre], axis=1)
    dwin_main = mm(tag + "_dwin", du, xm, "tn", EXCHANGE_DTYPE)
    dwin_dt = mm(tag + "_dwindt", dudt, xm, "tn", EXCHANGE_DTYPE)
    dx_m = mm(tag + "_dxm", du, lw["win_main"], "nn")
    dx_dt = mm(tag + "_dxdt", dudt, lw["win_dt"], "nn")
    grads = dict(
        e_w_in=jnp.concatenate([dwin_main, dwin_dt[:nh]], axis=0),
        e_conv_a_w=dcaw, e_conv_a_b=dcab[0], e_ln_a_g=dlag[0], e_ln_a_b=dlab[0],
        e_conv_b_w=jnp.concatenate([dcw_xs, dcw_b, dcw_c], axis=1),
        e_conv_b_b=jnp.concatenate([dcb_xs, dcb_b, dcb_c], axis=1)[0],
        e_dt_bias=ddt_bias[0, :nh], e_a_log=dalog.reshape(nh), e_d_skip=ddsk.reshape(nh), e_norm_b_g=dnorm_g[0],
        e_w_out=jnp.concatenate([dwout_a, dwout_b], axis=0), ln_g0=dg0[0], ln_b0=db0[0],
    )
    return [Win(dpre, coef=alpha), dx_m, dx_dt], grads


def odd_fwd(tag, x, xm, w, j, ln_g, ln_b, alpha):
    d = x.shape[1]
    u = mm(tag + "_win", xm, w["o_w_in"][j], "nt")
    bg, cg, v = Win(u, d, 0), Win(u, d, 1), Win(u, d, 2)
    (s,) = rowwise(tag + "_cv", f_mul, [cg, v], [], [d])
    cs = conv_fwd(tag + "_conv", s, w["o_conv_w"][j], None)
    (m,) = rowwise(tag + "_bm", f_mul, [bg, cs], [], [d], out_dtypes=[MXU_DTYPE])
    mix = mm(tag + "_wout", m, w["o_w_out"][j], "nn")

    def f_res(xv, mv, g, b):
        pre = alpha * xv + mv
        y = _ln(pre, g, b)
        return y, y, pre

    x1, x1m, pre = rowwise(tag + "_res", f_res, [x, mix], [ln_g, ln_b], [d] * 3, out_dtypes=[F32, MXU_DTYPE, F32])
    return x1, x1m, dict(xm=xm, u=u, s=s, cs=cs, m=m, pre=pre)


def odd_bwd(tag, dx1_pieces, sv, w, j, ln_g, ln_b, alpha):
    xm, u = sv["xm"], sv["u"]
    d = xm.shape[1]
    mx = (MXU_DTYPE,)
    bg, cg, v = Win(u, d, 0), Win(u, d, 1), Win(u, d, 2)
    (dpre, dprem), (dg0, db0) = rowwise_bwd(tag + "_res_b", f_ln, [sv["pre"]], [ln_g, ln_b], [dx1_pieces],
                                            d_dtypes=[(F32, MXU_DTYPE)])
    dm = mm(tag + "_dm", dprem, w["o_w_out"][j], "nt")
    dwout = mm(tag + "_dwout", sv["m"], dprem, "tn", EXCHANGE_DTYPE)
    (dbg, dcs), _ = rowwise_bwd(tag + "_bm_b", f_mul, [bg, sv["cs"]], [], [[dm]], d_dtypes=[mx, (F32,)])
    ds, dcw, _ = conv_bwd(tag + "_conv_b", sv["s"], dcs, w["o_conv_w"][j])
    (dcg, dv), _ = rowwise_bwd(tag + "_cv_b", f_mul, [cg, v], [], [[ds]], d_dtypes=[mx, mx])
    du = jnp.concatenate([dbg, dcg, dv], axis=1)
    dx_u = mm(tag + "_dx", du, w["o_w_in"][j], "nn")
    dwin = mm(tag + "_dwin", du, xm, "tn", EXCHANGE_DTYPE)
    grads = dict(o_w_in=dwin, o_conv_w=dcw, o_w_out=dwout, ln_g0=dg0[0], ln_b0=db0[0])
    return [Win(dpre, coef=alpha), dx_u], grads


def ffn_fwd(tag, x1, x1m, p_i, w, i, ln_g, ln_b, alpha):
    d = x1.shape[1]
    hpre = mm(tag + "_wup", x1m, w["f_w_up"][i], "nt")
    act = gated_conv_fwd(tag + "_fgate", hpre, w["f_conv_w"][i], w["f_conv_b"][i][None], MXU_DTYPE)
    ffn = mm(tag + "_wdown", act, w["f_w_down"][i], "nn")
    pp = mm(tag + "_pproj", p_i, w["ple_w_proj"][i], "nt")
    gt = mm(tag + "_pgate", x1m, w["ple_w_gate"][i], "nn")

    def f_res2(xv, fv, ppv, gtv, g, b):
        pre = alpha * xv + fv + ppv * _sigmoid(gtv)
        y = _ln(pre, g, b)
        return y, y, pre

    x2, x2m, pre = rowwise(tag + "_res2", f_res2, [x1, ffn, pp, gt], [ln_g, ln_b], [d] * 3,
                           out_dtypes=[F32, MXU_DTYPE, F32])
    return x2, x2m, dict(x1m=x1m, hpre=hpre, act=act, pp=pp, gt=gt, pre=pre)


def ffn_bwd(tag, dx2_pieces, sv, p_i, w, i, ln_g, ln_b, alpha):
    x1m = sv["x1m"]
    mx = (MXU_DTYPE,)
    (dpre, dprem), (dg1, db1) = rowwise_bwd(tag + "_res2_b", f_ln, [sv["pre"]], [ln_g, ln_b], [dx2_pieces],
                                            d_dtypes=[(F32, MXU_DTYPE)])
    (dpp, dgt), _ = rowwise_bwd(tag + "_pg_b", f_gate_mul, [sv["pp"], sv["gt"]], [], [[dpre]], d_dtypes=[mx, mx])
    dwproj = mm(tag + "_dwproj", dpp, p_i, "tn", EXCHANGE_DTYPE)
    dwgate = mm(tag + "_dwgate", x1m, dgt, "tn", EXCHANGE_DTYPE)
    dx1_a = mm(tag + "_dx1a", dgt, w["ple_w_gate"][i], "nt")
    dact = mm(tag + "_dact", dprem, w["f_w_down"][i], "nt")
    dwdown = mm(tag + "_dwdown", sv["act"], dprem, "tn", EXCHANGE_DTYPE)
    dhpre, dfcw, dfcb = gated_conv_bwd(tag + "_fgate_b", sv["hpre"], dact, w["f_conv_w"][i], w["f_conv_b"][i][None],
                                       MXU_DTYPE)
    dwup = mm(tag + "_dwup", dhpre, x1m, "tn", EXCHANGE_DTYPE)
    dx1_b = mm(tag + "_dx1b", dhpre, w["f_w_up"][i], "nn")
    grads = dict(f_w_up=dwup, f_conv_w=dfcw, f_conv_b=dfcb[0], f_w_down=dwdown, ple_w_proj=dwproj, ple_w_gate=dwgate,
                 ln_g1=dg1[0], ln_b1=db1[0])
    return [Win(dpre, coef=alpha), dx1_a, dx1_b], grads


def local_step(x, p, w, target, fetch=None, emit=None):
    depth = w["ln_g"].shape[0]
    alpha = _alpha(depth)
    d = x.shape[1]
    saved = []
    h = hm = x
    for i in range(depth):
        j = i // 2
        if fetch is not None:
            fetch(i, "m", h)
        g0, b0, g1, b1 = w["ln_g"][i, 0][None], w["ln_b"][i, 0][None], w["ln_g"][i, 1][None], w["ln_b"][i, 1][None]
        tag = "l%d" % i
        if i % 2 == 0:
            lw = split_even_weights(w, j)
            h, hm, sv_m = even_fwd(tag, h, hm, lw, g0, b0, alpha)
        else:
            lw = None
            h, hm, sv_m = odd_fwd(tag, h, hm, w, j, g0, b0, alpha)
        if fetch is not None:
            fetch(i, "f", h)
        h, hm, sv_f = ffn_fwd(tag, h, hm, p[i], w, i, g1, b1, alpha)
        saved.append((lw, sv_m, sv_f, (g0, b0, g1, b1)))

    def f_loss(xf, tg):
        diff = xf - tg
        sq = jnp.sum(jnp.sum(jnp.square(diff), axis=1, keepdims=True), axis=0, keepdims=True)
        return diff * (1.0 / d), jnp.broadcast_to(sq, (1, LANE))

    dxf, sq = rowwise("loss", f_loss, [h, target], [], [d], red_widths=[LANE])
    loss = sq[0, 0] * (0.5 / d)

    per_layer = []
    pieces = [dxf]
    token = None
    for i in reversed(range(depth)):
        j = i // 2
        lw, sv_m, sv_f, (g0, b0, g1, b1) = saved[i]
        tag = "l%d" % i
        pieces, gf = ffn_bwd(tag, pieces, sv_f, p[i], w, i, g1 if token is None else g1 + token, b1, alpha)
        if emit is not None:
            token = emit(tag + "f", i, gf)
        g0 = g0 if token is None else g0 + token
        if i % 2 == 0:
            pieces, gm = even_bwd(tag, pieces, sv_m, lw, g0, b0, alpha)
        else:
            pieces, gm = odd_bwd(tag, pieces, sv_m, w, j, g0, b0, alpha)
        if emit is not None:
            token = emit(tag + "m", j, gm)
        per_layer.append((i, gm, gf))

    def f_sum(*vs):
        acc = None
        for v, c in zip(vs, [pc.coef for pc in map(_win, pieces)]):
            v = v if c == 1.0 else v * c
            acc = v if acc is None else acc + v
        return (acc,)

    (grad_x,) = rowwise("grad_x", f_sum, [Win(_win(pc).arr) for pc in pieces], [], [d])

    by_layer = {i: (gm, gf) for i, gm, gf in per_layer}
    grads = {}
    n_even, n_odd = (depth + 1) // 2, depth // 2
    collect = lambda name, per_layer: per_layer if name in BIG else jnp.stack(per_layer)
    for name in ("e_w_in", "e_conv_a_w", "e_conv_a_b", "e_ln_a_g", "e_ln_a_b", "e_conv_b_w", "e_conv_b_b", "e_dt_bias",
                 "e_a_log", "e_d_skip", "e_norm_b_g", "e_w_out"):
        grads[name] = collect(name, [by_layer[2 * j][0][name] for j in range(n_even)])
    for name in ("o_w_in", "o_conv_w", "o_w_out"):
        grads[name] = collect(name, [by_layer[2 * j + 1][0][name] for j in range(n_odd)])
    for name in ("f_w_up", "f_conv_w", "f_conv_b", "f_w_down", "ple_w_proj", "ple_w_gate"):
        grads[name] = collect(name, [by_layer[i][1][name] for i in range(depth)])
    grads["ln_g"] = jnp.stack([jnp.stack([by_layer[i][0]["ln_g0"], by_layer[i][1]["ln_g1"]]) for i in range(depth)])
    grads["ln_b"] = jnp.stack([jnp.stack([by_layer[i][0]["ln_b0"], by_layer[i][1]["ln_b1"]]) for i in range(depth)])
    return loss, grad_x, grads


_ANY = pl.BlockSpec(memory_space=pl.ANY)
_MESH = pl.DeviceIdType.MESH


def all_gather(name, xl):
    r, c_ = xl.shape
    split = r // 2 // 16 * 16
    halves = ((0, split), (split, r - split)) if split else ((0, r),)
    two = len(halves) == 2

    def body(x_ref, out_ref, send_sems, recv_sems, local_sem):
        x, y, c = lax.axis_index("x"), lax.axis_index("y"), lax.axis_index("c")
        me, sibling, xn, yn, dg = (x, y, c), (x, y, 1 - c), (1 - x, y, c), (x, 1 - y, c), (1 - x, 1 - y, c)

        def rows(block, h):
            ref = out_ref.at[4 * block[0] + 2 * block[1] + block[2]]
            return ref if h is None else ref.at[pl.ds(*halves[h])]

        def copy(k, block, h, to, own=False):
            src = (x_ref if h is None else x_ref.at[pl.ds(*halves[h])]) if own else rows(block, h)
            return pltpu.make_async_remote_copy(src_ref=src, dst_ref=rows(block, h), send_sem=send_sems.at[k],
                                                recv_sem=recv_sems.at[k], device_id=to, device_id_type=_MESH)

        def other_core(block):
            return (block[0], block[1], 1 - c)

        mine = pltpu.make_async_copy(x_ref, rows(me, None), local_sem)
        mine.start()
        direct = [copy(0, me, 0, xn, own=True), copy(1, me, 1 if two else 0, yn, own=True)]
        if two:
            direct += [copy(2, me, 1, xn, own=True), copy(3, me, 0, yn, own=True)]
        direct.append(copy(6, me, None, sibling, own=True))
        for cp in direct:
            cp.start()
        started = list(direct)

        def then(waits, nxt):
            for cp in waits:
                cp.wait_recv()
            for cp in nxt:
                cp.start()
            started.extend(nxt)

        if two:
            then([copy(0, xn, 0, me)], [copy(4, xn, 0, yn)])
            then([copy(1, yn, 1, me)], [copy(5, yn, 1, xn)])
            then([copy(2, xn, 1, me)], [copy(7, xn, None, sibling)])
            then([copy(3, yn, 0, me)], [copy(8, yn, None, sibling)])
            then([copy(4, dg, 0, me), copy(5, dg, 1, me)], [copy(9, dg, None, sibling)])
        else:
            then([copy(0, xn, 0, me)], [copy(4, xn, 0, yn), copy(7, xn, None, sibling)])
            then([copy(1, yn, 0, me)], [copy(8, yn, None, sibling)])
            then([copy(4, dg, 0, me)], [copy(9, dg, None, sibling)])
        for k, block in ((6, me), (7, xn), (8, yn), (9, dg)):
            copy(k, other_core(block), None, me).wait_recv()
        for cp in started:
            cp.wait_send()
        mine.wait()

    return pl.pallas_call(
        body, name=name, out_shape=S((N_DEV, r, c_), xl.dtype), in_specs=[_ANY], out_specs=_ANY,
        scratch_shapes=[pltpu.SemaphoreType.DMA((10,)), pltpu.SemaphoreType.DMA((10,)), pltpu.SemaphoreType.DMA],
    )(xl)


def sum_devices(name, g8):
    _, r, c_ = g8.shape

    def body(g_ref, o_ref):
        acc = g_ref[0]
        for k in range(1, N_DEV):
            acc = acc + g_ref[k]
        o_ref[...] = acc

    return pl.pallas_call(body, name=name, out_shape=S((r, c_), F32))(g8)


def _flatten(parts, cols, row_mult):
    flat = jnp.concatenate([v.reshape(-1) for v in parts])
    n = flat.shape[0]
    rows = -(-n // (cols * row_mult)) * row_mult
    return jnp.pad(flat, (0, rows * cols - n)).reshape(rows, cols)


def _exchange_dims(name, lshape):
    l, r, c = lshape
    return (l, c, r) if name in TRANSPOSED else (l, r, c)


_HBM = pl.BlockSpec(memory_space=pltpu.HBM)
_SEM = pl.BlockSpec(memory_space=pltpu.SEMAPHORE)
_PEERS = [(dx, dy, dc) for dx in (0, 1) for dy in (0, 1) for dc in (0, 1)][1:]


def _peer_copies(scatter, src_refs, land_refs, send_sems, recv_sems):
    x, y, c = lax.axis_index("x"), lax.axis_index("y"), lax.axis_index("c")
    flip = lambda v, d: 1 - v if d else v
    copies = []
    for s_ref, land_ref in zip(src_refs, land_refs):
        for dx, dy, dc in _PEERS:
            px, py, pc = flip(x, dx), flip(y, dy), flip(c, dc)
            k = len(copies)
            copies.append(pltpu.make_async_remote_copy(
                src_ref=s_ref.at[4 * px + 2 * py + pc] if scatter else s_ref, dst_ref=land_ref.at[4 * x + 2 * y + c],
                send_sem=send_sems.at[k], recv_sem=recv_sems.at[k], device_id=(px, py, pc), device_id_type=_MESH))
    return copies


def split_start(name, srcs, scatter):
    n = len(srcs)
    lands = [lax.empty(s.shape if scatter else (N_DEV,) + s.shape, s.dtype) for s in srcs]
    n_copies = n * len(_PEERS)

    def body(*refs):
        send_sems, recv_sems, token = refs[2 * n], refs[2 * n + 1], refs[-1]
        for cp in _peer_copies(scatter, refs[:n], refs[n:2 * n], send_sems, recv_sems):
            cp.start()
        token[...] = jnp.zeros_like(token)

    return pl.pallas_call(
        body, name=name,
        out_shape=(pltpu.SemaphoreType.DMA((n_copies,)), pltpu.SemaphoreType.DMA((n_copies,)),
                   *[pltpu.HBM(v.shape, v.dtype) for v in srcs + lands], S((SUBLANE, LANE), F32)),
        in_specs=(_HBM,) * (2 * n), out_specs=(_SEM, _SEM) + (_HBM,) * (2 * n) + (pl.BlockSpec(memory_space=pltpu.VMEM),),
        input_output_aliases={i: 2 + i for i in range(2 * n)},
        compiler_params=pltpu.CompilerParams(has_side_effects=pltpu.SideEffectType.DATAFLOW_SIDE_EFFECTING),
    )(*[pltpu.with_memory_space_constraint(v, pltpu.HBM) for v in srcs + lands])


def split_wait(name, started, scatter, after):
    send_sems, recv_sems = started[0], started[1]
    thru = list(started[2:-1])
    n = len(thru) // 2

    def body(*refs):
        send_sems, recv_sems = refs[2 * n], refs[2 * n + 1]
        for cp in _peer_copies(scatter, refs[:n], refs[n:2 * n], send_sems, recv_sems):
            cp.wait_send()
            cp.wait_recv()

    outs = pl.pallas_call(
        body, name=name, out_shape=tuple(pltpu.HBM(v.shape, v.dtype) for v in thru),
        in_specs=(_HBM,) * (2 * n) + (_SEM, _SEM, _ANY), out_specs=(_HBM,) * (2 * n),
        input_output_aliases={i: i for i in range(2 * n)},
        compiler_params=pltpu.CompilerParams(has_side_effects=pltpu.SideEffectType.DATAFLOW_SIDE_EFFECTING),
    )(*thru, send_sems, recv_sems, after)
    return outs[:n], outs[n:]


def _layer_weights(depth):
    table = []
    for i in range(depth):
        mixer = ("e_w_out", "e_w_in") if i % 2 == 0 else ("o_w_out", "o_w_in")
        table.append([("f_w_up", i), ("f_w_down", i), ("ple_w_gate", i), ("ple_w_proj", i)]
                     + [(n, i // 2) for n in mixer])
    return table


def gather_big(local):
    cols = local["e_w_out"].shape[2]
    depth = local["ln_g"].shape[0]
    table = _layer_weights(depth)
    shard = {}
    for n in BIG:
        v = local[n].astype(MXU_DTYPE)
        shard[n] = v.transpose(0, 2, 1) if n in TRANSPOSED else v
    full = {n: [None] * local[n].shape[0] for n in BIG}
    dev = 4 * lax.axis_index("x") + 2 * lax.axis_index("y") + lax.axis_index("c")

    first = [(n, li) for n, li in table[0] if n.startswith("e_")]
    table[0] = [(n, li) for n, li in table[0] if not n.startswith("e_")]
    got, r0 = all_gather("ag_l0", jnp.concatenate([shard[n][li].reshape(-1, cols) for n, li in first], axis=0)), 0
    for n, li in first:
        _, a, b = shard[n].shape
        per = a * b // cols
        full[n][li] = got[:, r0:r0 + per].reshape(N_DEV * a, b)
        r0 += per
    started = [split_start("ag_start_l%d" % i, [shard[n][li] for n, li in table[i]], False) for i in range(depth)]

    def fetch(i, part, after):
        if part == ("f" if i == 0 else "m"):
            srcs, lands = split_wait("ag_wait_l%d" % i, started[i], False, after)
            for (n, li), src, land in zip(table[i], srcs, lands):
                _, a, b = shard[n].shape
                full[n][li] = lax.dynamic_update_slice(land, src[None], (dev, 0, 0)).reshape(N_DEV * a, b)

    token = sum(st[-1][0, 0] for st in started)
    return full, fetch, token


def sum_landed(name, g, land, me):
    _, a, b = g.shape
    tb = b // 2 if b % (2 * LANE) == 0 else b

    def body(me_ref, g_ref, land_ref, o_ref):
        acc = jnp.zeros(o_ref.shape, F32)
        for k in range(N_DEV):
            acc = acc + jnp.where(me_ref[0] == k, g_ref[0], land_ref[k]).astype(F32)
        o_ref[...] = acc

    return pl.pallas_call(
        body, name=name, out_shape=S((a, b), F32),
        grid_spec=pltpu.PrefetchScalarGridSpec(
            num_scalar_prefetch=1, grid=(b // tb,),
            in_specs=[pl.BlockSpec((1, a, tb), lambda j, me_: (me_[0], 0, j)),
                      pl.BlockSpec((N_DEV, a, tb), lambda j, me_: (0, 0, j))],
            out_specs=pl.BlockSpec((a, tb), lambda j, me_: (0, j))),
    )(me, g, land)


class GradScatter:
    def __init__(self, local_shapes):
        self.local_shapes = local_shapes
        self.pending = []

    def emit(self, tag, index, grads):
        names = [n for n in BIG if n in grads]
        gs = [grads[n].reshape(N_DEV, -1, grads[n].shape[1]) for n in names]
        started = split_start("rs_start_" + tag, gs, True)
        self.pending.append((tag, index, names, started))
        return started[-1][0:1, 0:1]

    def finish(self, after):
        me = (4 * lax.axis_index("x") + 2 * lax.axis_index("y") + lax.axis_index("c")).astype(jnp.int32).reshape(1)
        shards = {n: [None] * self.local_shapes[n][0] for n in BIG}
        for tag, index, names, started in self.pending:
            gs, lands = split_wait("rs_wait_" + tag, started, True, after)
            for n, g, land in zip(names, gs, lands):
                shards[n][index] = sum_landed("rs_sum_%s_%s" % (tag, n), g, land, me)
        out = {}
        for n in BIG:
            seg = jnp.stack(shards[n])
            out[n] = seg.transpose(0, 2, 1) if n in TRANSPOSED else seg
        return out


def gather_small(name, local, names):
    flat = _flatten([local[n] for n in names], LANE, 1)
    got = all_gather(name, flat).reshape(N_DEV, -1)
    full, off = {}, 0
    for n in names:
        size = math.prod(local[n].shape)
        seg = got[:, off:off + size].reshape((N_DEV,) + local[n].shape)
        full[n] = seg.transpose(1, 2, 0, 3).reshape(seg.shape[1], seg.shape[2], -1)
        off += size
    return full


def all_reduce_small(grads, names):
    flat = _flatten([grads[n] for n in names], LANE, SUBLANE)
    total = sum_devices("ar_sum", all_gather("ar_gather", flat)).reshape(-1)
    out, off = {}, 0
    for nm in names:
        size = math.prod(grads[nm].shape)
        out[nm] = total[off:off + size].reshape(grads[nm].shape)
        off += size
    return out


def adamw(name, w, g, m, v):
    shape = w.shape
    cols = shape[-1]
    rows = math.prod(shape[:-1])
    tr = _pick(rows, (256, 128, 64, 32, 16, 8)) if rows * cols > 256 * 1024 else rows
    c1 = 1.0 - ADAM_B1 ** ADAM_STEP
    c2 = 1.0 - ADAM_B2 ** ADAM_STEP

    def body(w_ref, g_ref, m_ref, v_ref, d_ref, nm_ref, nv_ref):
        gv = g_ref[...]
        m2 = ADAM_B1 * m_ref[...] + (1.0 - ADAM_B1) * gv
        v2 = ADAM_B2 * v_ref[...] + (1.0 - ADAM_B2) * jnp.square(gv)
        d_ref[...] = -ADAM_LR * ((m2 / c1) / (jnp.sqrt(v2 / c2) + ADAM_EPS) + ADAM_WD * w_ref[...])
        nm_ref[...] = m2
        nv_ref[...] = v2

    spec = pl.BlockSpec((tr, cols), lambda i: (i, 0))
    outs = pl.pallas_call(
        body, name=name, grid=(rows // tr,), in_specs=[spec] * 4, out_specs=[spec] * 3,
        out_shape=[S((rows, cols), F32)] * 3,
        compiler_params=pltpu.CompilerParams(dimension_semantics=("parallel",)),
    )(*[a.reshape(rows, cols) for a in (w, g, m, v)])
    return tuple(o.reshape(shape) for o in outs)


def kernel(x, p, e_w_in, e_conv_a_w, e_conv_a_b, e_ln_a_g, e_ln_a_b, e_conv_b_w, e_conv_b_b, e_dt_bias, e_a_log, e_d_skip, e_norm_b_g, e_w_out, o_w_in, o_conv_w, o_w_out, f_w_up, f_conv_w, f_conv_b, f_w_down, ple_w_proj, ple_w_gate, ln_g, ln_b, loss_target, m_e_w_in, m_e_conv_a_w, m_e_conv_a_b, m_e_ln_a_g, m_e_ln_a_b, m_e_conv_b_w, m_e_conv_b_b, m_e_dt_bias, m_e_a_log, m_e_d_skip, m_e_norm_b_g, m_e_w_out, m_o_w_in, m_o_conv_w, m_o_w_out, m_f_w_up, m_f_conv_w, m_f_conv_b, m_f_w_down, m_ple_w_proj, m_ple_w_gate, m_ln_g, m_ln_b, v_e_w_in, v_e_conv_a_w, v_e_conv_a_b, v_e_ln_a_g, v_e_ln_a_b, v_e_conv_b_w, v_e_conv_b_b, v_e_dt_bias, v_e_a_log, v_e_d_skip, v_e_norm_b_g, v_e_w_out, v_o_w_in, v_o_conv_w, v_o_w_out, v_f_w_up, v_f_conv_w, v_f_conv_b, v_f_w_down, v_ple_w_proj, v_ple_w_gate, v_ln_g, v_ln_b):
    args = locals()
    local = {n: args[n] for n in WEIGHTS}
    mom = {n: args["m_" + n] for n in WEIGHTS}
    var = {n: args["v_" + n] for n in WEIGHTS}

    full = {n: local[n] for n in REPLICATED}
    big, fetch, token = gather_big(local)
    full.update(big)
    full.update(gather_small("ag_small", local, SMALL_SHARDED))

    scatter = GradScatter({n: local[n].shape for n in BIG})
    loss_local, grad_x, grads = local_step(x[0] + token, p[:, 0], full, loss_target[0], fetch, scatter.emit)
    loss = lax.psum(loss_local, MESH_AXES)

    grads["ln_b"] = grads["ln_b"] + scatter.pending[-1][3][-1][0, 0]
    small = all_reduce_small(grads, REPLICATED + SMALL_SHARDED)
    dev = 4 * lax.axis_index("x") + 2 * lax.axis_index("y") + lax.axis_index("c")
    g_local = {n: small[n] for n in REPLICATED}
    for n in SMALL_SHARDED:
        width = local[n].shape[2]
        g_local[n] = lax.dynamic_slice_in_dim(small[n], dev * width, width, axis=2)
    delta, new_m, new_v = {}, {}, {}
    for n in REPLICATED + SMALL_SHARDED:
        delta[n], new_m[n], new_v[n] = adamw("adamw_" + n, local[n], g_local[n], mom[n], var[n])

    done = sum(delta[n].reshape(-1)[0] for n in REPLICATED + SMALL_SHARDED)
    g_local.update(scatter.finish(delta[SMALL_SHARDED[-1]] + 0.0 * done))
    for n in BIG:
        delta[n], new_m[n], new_v[n] = adamw("adamw_" + n, local[n], g_local[n], mom[n], var[n])
    return (loss, grad_x[None], *[g_local[n] for n in WEIGHTS], *[delta[n] for n in WEIGHTS],
            *[new_m[n] for n in WEIGHTS], *[new_v[n] for n in WEIGHTS])
```
